```python
import jax, jax.numpy as jnp
from jax import lax
import numpy as np

D_MODEL = 1024
BATCH = 32
SEQ = 2048
DEPTH = 4

D_RG = D_MODEL
RG_BLOCK_W = 256
RG_BLOCKS = D_RG // RG_BLOCK_W
CONV_W = 4
RG_C = 8.0
HG_EXPAND = 128
HG_HEADS = D_MODEL // HG_EXPAND
HG_DK = HG_EXPAND
HG_DV = D_MODEL // HG_HEADS
HG_CHUNK = 32
F_MIN = 1e-30
D_FF = 4 * D_MODEL
NORM_EPS = 1e-6
SPLIT_SIZES = (D_RG, D_RG, HG_HEADS * HG_DK, HG_HEADS * HG_DK, HG_HEADS * HG_DV, HG_HEADS * HG_DV, D_MODEL, D_MODEL)
D_IN = sum(SPLIT_SIZES)
SPLIT_POINTS = tuple(np.cumsum(SPLIT_SIZES)[:-1].tolist())

kernel_name = 'hybrid_rglru_hgrn2_gated_trunk'


def rms_norm(x, gain):
    xf = x.astype(jnp.float32)
    y = xf * lax.rsqrt(jnp.mean(xf * xf, axis=-1, keepdims=True) + NORM_EPS)
    return (y * gain.astype(jnp.float32)).astype(x.dtype)


def causal_depthwise_conv(x, w, b):
    y = lax.conv_general_dilated(
        x, w[:, None, :].astype(x.dtype), window_strides=(1,),
        padding=((CONV_W - 1, 0),), dimension_numbers=('NWC', 'WIO', 'NWC'),
        feature_group_count=x.shape[-1])
    return y + b.astype(x.dtype)


def rg_lru(x, w_r, b_r, w_i, b_i, lam):
    B_, S_, _ = x.shape
    xf = x.astype(jnp.float32)
    xb = xf.reshape(B_, S_, RG_BLOCKS, RG_BLOCK_W)
    r = jax.nn.sigmoid(jnp.einsum('bsnj,njk->bsnk', xb, w_r.astype(jnp.float32)).reshape(B_, S_, D_RG) + b_r.astype(jnp.float32))
    i = jax.nn.sigmoid(jnp.einsum('bsnj,njk->bsnk', xb, w_i.astype(jnp.float32)).reshape(B_, S_, D_RG) + b_i.astype(jnp.float32))
    log_a = -RG_C * r * jax.nn.softplus(-lam.astype(jnp.float32))
    a = jnp.exp(log_a)
    u = jnp.sqrt(jnp.maximum(-jnp.expm1(2.0 * log_a), 0.0)) * (i * xf)

    def combine(left, right):
        a1, b1 = left
        a2, b2 = right
        return a1 * a2, a2 * b1 + b2

    _, h = lax.associative_scan(combine, (a, u), axis=1)
    return h.astype(x.dtype)


def hgrn2_chunkwise(q, k, log_f, v):
    B_, S_, H, DK = q.shape
    DV = v.shape[-1]
    n_chunks = S_ // HG_CHUNK

    def to_chunks(t):
        return t.reshape(B_, n_chunks, HG_CHUNK, H, t.shape[-1]).transpose(1, 0, 3, 2, 4)

    causal = jnp.tril(jnp.ones((HG_CHUNK, HG_CHUNK), dtype=bool))

    def step(state, chunk):
        qc, kc, lfc, vc = chunk
        cum = jnp.cumsum(lfc, axis=2)
        o_inter = jnp.einsum('bhtk,bhkv->bhtv', qc * jnp.exp(cum), state)
        rel = cum[:, :, :, None, :] - cum[:, :, None, :, :]
        decay = jnp.where(causal[:, :, None], jnp.exp(jnp.minimum(rel, 0.0)), 0.0)
        scores = jnp.einsum('bhtsk,bhsk->bhts', qc[:, :, :, None, :] * decay, kc)
        o_intra = jnp.einsum('bhts,bhsv->bhtv', scores, vc)
        last = cum[:, :, -1, :]
        k_to_end = kc * jnp.exp(jnp.minimum(last[:, :, None, :] - cum, 0.0))
        state = jnp.exp(last)[..., None] * state + jnp.einsum('bhsk,bhsv->bhkv', k_to_end, vc)
        return state, o_inter + o_intra

    state0 = jnp.zeros((B_, H, DK, DV), jnp.float32)
    _, o = lax.scan(step, state0, (to_chunks(q), to_chunks(k), to_chunks(log_f), to_chunks(v)))
    return o.transpose(1, 0, 3, 2, 4).reshape(B_, S_, H, DV)


def hybrid_mixer(h, lower_bound, w_in, conv_w, conv_b, w_r, b_r, w_i, b_i, lam, hg_norm, w_out):
    B_, S_, _ = h.shape
    proj = h @ w_in
    xa, ga, q, f, v, g, m_a, m_b = jnp.split(proj, SPLIT_POINTS, axis=-1)
    xa = causal_depthwise_conv(xa, conv_w, conv_b)
    y_a = rg_lru(xa, w_r, b_r, w_i, b_i, lam) * jax.nn.gelu(ga)
    qf = jax.nn.silu(q.astype(jnp.float32)).reshape(B_, S_, HG_HEADS, HG_DK)
    zf = f.astype(jnp.float32).reshape(B_, S_, HG_HEADS, HG_DK)
    lb = lower_bound.reshape(HG_HEADS, HG_DK)
    sig = jax.nn.sigmoid(zf)
    f_gate = lb + (1.0 - lb) * sig
    log_f = jnp.log(jnp.maximum(f_gate, F_MIN))
    kf = (1.0 - lb) * jax.nn.sigmoid(-zf)
    vf = v.astype(jnp.float32).reshape(B_, S_, HG_HEADS, HG_DV)
    o = rms_norm(hgrn2_chunkwise(qf, kf, log_f, vf), hg_norm)
    y_b = o.reshape(B_, S_, HG_HEADS * HG_DV).astype(h.dtype) * jax.nn.silu(g)
    y = jax.nn.sigmoid(m_a) * y_a + jax.nn.sigmoid(m_b) * y_b
    return y @ w_out


def _fwd_setup_inputs(seed: int = 0) -> dict:
    key = jax.random.key(seed)
    ks = jax.random.split(key, 20)
    f32 = jnp.float32
    nrm = lambda k, shape, scale: jax.random.normal(k, shape, f32) * scale
    a_c = jax.random.uniform(ks[10], (DEPTH, D_RG), f32, minval=0.9, maxval=0.999)
    a0 = a_c ** (1.0 / RG_C)
    lam = jnp.log(a0) - jnp.log1p(-a0)
    return {
        'x': jax.random.normal(ks[0], (BATCH, SEQ, D_MODEL), f32),
        'lb_logits': nrm(ks[1], (DEPTH, HG_HEADS * HG_DK), 0.1),
        'norm_mix': 1.0 + nrm(ks[2], (DEPTH, D_MODEL), 0.02),
        'w_in': nrm(ks[3], (DEPTH, D_MODEL, D_IN), D_MODEL ** -0.5),
        'conv_w': nrm(ks[4], (DEPTH, CONV_W, D_RG), CONV_W ** -0.5),
        'conv_b': nrm(ks[5], (DEPTH, D_RG), 0.02),
        'w_r': nrm(ks[6], (DEPTH, RG_BLOCKS, RG_BLOCK_W, RG_BLOCK_W), RG_BLOCK_W ** -0.5),
        'b_r': nrm(ks[7], (DEPTH, D_RG), 0.02),
        'w_i': nrm(ks[8], (DEPTH, RG_BLOCKS, RG_BLOCK_W, RG_BLOCK_W), RG_BLOCK_W ** -0.5),
        'b_i': nrm(ks[9], (DEPTH, D_RG), 0.02),
        'lam': lam,
        'hg_norm': 1.0 + nrm(ks[11], (DEPTH, HG_DV), 0.02),
        'w_out': nrm(ks[12], (DEPTH, D_MODEL, D_MODEL), D_MODEL ** -0.5),
        'norm_mlp': 1.0 + nrm(ks[13], (DEPTH, D_MODEL), 0.02),
        'w_up': nrm(ks[14], (DEPTH, D_MODEL, D_FF), D_MODEL ** -0.5),
        'w_down': nrm(ks[15], (DEPTH, D_FF, D_MODEL), D_FF ** -0.5),
        'norm_final': 1.0 + nrm(ks[16], (D_MODEL,), 0.02),
    }


def _fwd_reference(x, lb_logits, norm_mix, w_in, conv_w, conv_b, w_r, b_r, w_i, b_i, lam, hg_norm, w_out, norm_mlp, w_up, w_down, norm_final):
    sm = jax.nn.softmax(lb_logits.astype(jnp.float32), axis=0)
    lower_bounds = jnp.clip(jnp.cumsum(sm, axis=0) - sm[0], 0.0, 1.0)
    for l in range(DEPTH):
        h = rms_norm(x, norm_mix[l])
        x = x + hybrid_mixer(h, lower_bounds[l], w_in[l], conv_w[l], conv_b[l], w_r[l], b_r[l],
                             w_i[l], b_i[l], lam[l], hg_norm[l], w_out[l])
        h = rms_norm(x, norm_mlp[l])
        x = x + jnp.square(jax.nn.relu(h @ w_up[l])) @ w_down[l]
    return rms_norm(x, norm_final)


import jax as _jax
import jax.numpy as _jnp

TWIN_FORMAT = 'train_step'
FWD_PARAMS = ['x', 'lb_logits', 'norm_mix', 'w_in', 'conv_w', 'conv_b', 'w_r', 'b_r', 'w_i', 'b_i', 'lam', 'hg_norm', 'w_out', 'norm_mlp', 'w_up', 'w_down', 'norm_final']
TWIN_WEIGHTS = ['lb_logits', 'norm_mix', 'w_in', 'conv_w', 'conv_b', 'w_r', 'b_r', 'w_i', 'b_i', 'lam', 'hg_norm', 'w_out', 'norm_mlp', 'w_up', 'w_down', 'norm_final']
TWIN_DIFF_INPUT = 'x'
TWIN_INPUTS = ['x', 'lb_logits', 'norm_mix', 'w_in', 'conv_w', 'conv_b', 'w_r', 'b_r', 'w_i', 'b_i', 'lam', 'hg_norm', 'w_out', 'norm_mlp', 'w_up', 'w_down', 'norm_final', 'loss_target', 'm_lb_logits', 'm_norm_mix', 'm_w_in', 'm_conv_w', 'm_conv_b', 'm_w_r', 'm_b_r', 'm_w_i', 'm_b_i', 'm_lam', 'm_hg_norm', 'm_w_out', 'm_norm_mlp', 'm_w_up', 'm_w_down', 'm_norm_final', 'v_lb_logits', 'v_norm_mix', 'v_w_in', 'v_conv_w', 'v_conv_b', 'v_w_r', 'v_b_r', 'v_w_i', 'v_b_i', 'v_lam', 'v_hg_norm', 'v_w_out', 'v_norm_mlp', 'v_w_up', 'v_w_down', 'v_norm_final']
TWIN_OUTPUTS = ['loss', 'grad_x', 'grad_lb_logits', 'grad_norm_mix', 'grad_w_in', 'grad_conv_w', 'grad_conv_b', 'grad_w_r', 'grad_b_r', 'grad_w_i', 'grad_b_i', 'grad_lam', 'grad_hg_norm', 'grad_w_out', 'grad_norm_mlp', 'grad_w_up', 'grad_w_down', 'grad_norm_final', 'delta_lb_logits', 'delta_norm_mix', 'delta_w_in', 'delta_conv_w', 'delta_conv_b', 'delta_w_r', 'delta_b_r', 'delta_w_i', 'delta_b_i', 'delta_lam', 'delta_hg_norm', 'delta_w_out', 'delta_norm_mlp', 'delta_w_up', 'delta_w_down', 'delta_norm_final', 'new_m_lb_logits', 'new_m_norm_mix', 'new_m_w_in', 'new_m_conv_w', 'new_m_conv_b', 'new_m_w_r', 'new_m_b_r', 'new_m_w_i', 'new_m_b_i', 'new_m_lam', 'new_m_hg_norm', 'new_m_w_out', 'new_m_norm_mlp', 'new_m_w_up', 'new_m_w_down', 'new_m_norm_final', 'new_v_lb_logits', 'new_v_norm_mix', 'new_v_w_in', 'new_v_conv_w', 'new_v_conv_b', 'new_v_w_r', 'new_v_b_r', 'new_v_w_i', 'new_v_b_i', 'new_v_lam', 'new_v_hg_norm', 'new_v_w_out', 'new_v_norm_mlp', 'new_v_w_up', 'new_v_w_down', 'new_v_norm_final']
TWIN_LEAF_KINDS = {'loss': 'loss', 'grad_x': 'grad_x', 'grad_lb_logits': 'grad_w', 'grad_norm_mix': 'grad_w', 'grad_w_in': 'grad_w', 'grad_conv_w': 'grad_w', 'grad_conv_b': 'grad_w', 'grad_w_r': 'grad_w', 'grad_b_r': 'grad_w', 'grad_w_i': 'grad_w', 'grad_b_i': 'grad_w', 'grad_lam': 'grad_w', 'grad_hg_norm': 'grad_w', 'grad_w_out': 'grad_w', 'grad_norm_mlp': 'grad_w', 'grad_w_up': 'grad_w', 'grad_w_down': 'grad_w', 'grad_norm_final': 'grad_w', 'delta_lb_logits': 'delta_w', 'delta_norm_mix': 'delta_w', 'delta_w_in': 'delta_w', 'delta_conv_w': 'delta_w', 'delta_conv_b': 'delta_w', 'delta_w_r': 'delta_w', 'delta_b_r': 'delta_w', 'delta_w_i': 'delta_w', 'delta_b_i': 'delta_w', 'delta_lam': 'delta_w', 'delta_hg_norm': 'delta_w', 'delta_w_out': 'delta_w', 'delta_norm_mlp': 'delta_w', 'delta_w_up': 'delta_w', 'delta_w_down': 'delta_w', 'delta_norm_final': 'delta_w', 'new_m_lb_logits': 'new_m', 'new_m_norm_mix': 'new_m', 'new_m_w_in': 'new_m', 'new_m_conv_w': 'new_m', 'new_m_conv_b': 'new_m', 'new_m_w_r': 'new_m', 'new_m_b_r': 'new_m', 'new_m_w_i': 'new_m', 'new_m_b_i': 'new_m', 'new_m_lam': 'new_m', 'new_m_hg_norm': 'new_m', 'new_m_w_out': 'new_m', 'new_m_norm_mlp': 'new_m', 'new_m_w_up': 'new_m', 'new_m_w_down': 'new_m', 'new_m_norm_final': 'new_m', 'new_v_lb_logits': 'new_v', 'new_v_norm_mix': 'new_v', 'new_v_w_in': 'new_v', 'new_v_conv_w': 'new_v', 'new_v_conv_b': 'new_v', 'new_v_w_r': 'new_v', 'new_v_b_r': 'new_v', 'new_v_w_i': 'new_v', 'new_v_b_i': 'new_v', 'new_v_lam': 'new_v', 'new_v_hg_norm': 'new_v', 'new_v_w_out': 'new_v', 'new_v_norm_mlp': 'new_v', 'new_v_w_up': 'new_v', 'new_v_w_down': 'new_v', 'new_v_norm_final': 'new_v'}


def _forward(args):
    return _fwd_reference(*[args[k] for k in FWD_PARAMS])


def _output_shape():
    out = _jax.eval_shape(lambda: _forward(_fwd_setup_inputs(0)))
    return out.shape, out.dtype

N_MICROBATCH = 1
ADAM_LR = 0.001
ADAM_B1 = 0.9
ADAM_B2 = 0.999
ADAM_EPS = 1e-08
ADAM_WD = 0.01
ADAM_STEP = 10
PER_EXAMPLE_BATCH_AXIS = {'x': 0, 'loss_target': 0}
SHARED_INPUTS = []
_WEIGHT_DTYPES = {'lb_logits': _jnp.float32, 'norm_mix': _jnp.float32, 'w_in': _jnp.float32, 'conv_w': _jnp.float32, 'conv_b': _jnp.float32, 'w_r': _jnp.float32, 'b_r': _jnp.float32, 'w_i': _jnp.float32, 'b_i': _jnp.float32, 'lam': _jnp.float32, 'hg_norm': _jnp.float32, 'w_out': _jnp.float32, 'norm_mlp': _jnp.float32, 'w_up': _jnp.float32, 'w_down': _jnp.float32, 'norm_final': _jnp.float32}
MOMENT_SCALE = {'lb_logits': 2.508689e-03, 'norm_mix': 2.201877e-01, 'w_in': 7.480929e-02, 'conv_w': 1.740843e-01, 'conv_b': 7.679018e-01, 'w_r': 1.631222e-02, 'b_r': 2.721956e-02, 'w_i': 3.116049e-02, 'b_i': 5.311297e-02, 'lam': 7.671983e-02, 'hg_norm': 1.757751e-01, 'w_out': 1.525782e-01, 'norm_mlp': 1.877944e-01, 'w_up': 9.516721e-02, 'w_down': 2.126894e-01, 'norm_final': 6.620807e+01}


def _to_microbatches(a, axis):
    t = _jnp.moveaxis(a, axis, 0)
    t = t.reshape((N_MICROBATCH, t.shape[0] // N_MICROBATCH) + t.shape[1:])
    return _jnp.moveaxis(t, 1, axis + 1)


def setup_inputs(seed: int = 0) -> dict:
    inp = _fwd_setup_inputs(seed)
    key = _jax.random.fold_in(_jax.random.key(seed), 7919)
    shape, _ = _output_shape()
    out = dict(inp)
    out["loss_target"] = _jax.random.normal(_jax.random.fold_in(key, 0), shape, _jnp.float32)
    for i, name in enumerate(TWIN_WEIGHTS):
        w = inp[name].astype(_jnp.float32)
        if MOMENT_SCALE is None:
            s = _jnp.sqrt(_jnp.mean(_jnp.square(w)) + 1e-30)
        else:
            s = MOMENT_SCALE[name]
        km, kv = _jax.random.split(_jax.random.fold_in(key, i + 1))
        out[name] = w
        out["m_" + name] = s * _jax.random.normal(km, w.shape, _jnp.float32)
        out["v_" + name] = (s * s) * _jax.random.uniform(kv, w.shape, _jnp.float32, 0.5, 1.5)
    if N_MICROBATCH > 1:
        for name, axis in PER_EXAMPLE_BATCH_AXIS.items():
            out[name] = _to_microbatches(out[name], axis)
    return {'x': out['x'], 'lb_logits': out['lb_logits'], 'norm_mix': out['norm_mix'], 'w_in': out['w_in'], 'conv_w': out['conv_w'], 'conv_b': out['conv_b'], 'w_r': out['w_r'], 'b_r': out['b_r'], 'w_i': out['w_i'], 'b_i': out['b_i'], 'lam': out['lam'], 'hg_norm': out['hg_norm'], 'w_out': out['w_out'], 'norm_mlp': out['norm_mlp'], 'w_up': out['w_up'], 'w_down': out['w_down'], 'norm_final': out['norm_final'], 'loss_target': out['loss_target'], 'm_lb_logits': out['m_lb_logits'], 'm_norm_mix': out['m_norm_mix'], 'm_w_in': out['m_w_in'], 'm_conv_w': out['m_conv_w'], 'm_conv_b': out['m_conv_b'], 'm_w_r': out['m_w_r'], 'm_b_r': out['m_b_r'], 'm_w_i': out['m_w_i'], 'm_b_i': out['m_b_i'], 'm_lam': out['m_lam'], 'm_hg_norm': out['m_hg_norm'], 'm_w_out': out['m_w_out'], 'm_norm_mlp': out['m_norm_mlp'], 'm_w_up': out['m_w_up'], 'm_w_down': out['m_w_down'], 'm_norm_final': out['m_norm_final'], 'v_lb_logits': out['v_lb_logits'], 'v_norm_mix': out['v_norm_mix'], 'v_w_in': out['v_w_in'], 'v_conv_w': out['v_conv_w'], 'v_conv_b': out['v_conv_b'], 'v_w_r': out['v_w_r'], 'v_b_r': out['v_b_r'], 'v_w_i': out['v_w_i'], 'v_b_i': out['v_b_i'], 'v_lam': out['v_lam'], 'v_hg_norm': out['v_hg_norm'], 'v_w_out': out['v_w_out'], 'v_norm_mlp': out['v_norm_mlp'], 'v_w_up': out['v_w_up'], 'v_w_down': out['v_w_down'], 'v_norm_final': out['v_norm_final']}


def _loss(weights, diff, rest, loss_target):
    with _jax.named_scope("forward"):
        args = {**rest, TWIN_DIFF_INPUT: diff, **{k: w.astype(_WEIGHT_DTYPES[k]) for k, w in weights.items()}}
        y = _forward(args)
    with _jax.named_scope("loss_head"):
        err = _jnp.square(y.astype(_jnp.float32) - loss_target)
        return 0.5 * _jnp.sum(_jnp.mean(err, axis=-1)) if err.ndim else 0.5 * err


def _adamw(w, g, m, v):
    m = ADAM_B1 * m + (1.0 - ADAM_B1) * g
    v = ADAM_B2 * v + (1.0 - ADAM_B2) * _jnp.square(g)
    m_hat = m / (1.0 - ADAM_B1 ** ADAM_STEP)
    v_hat = v / (1.0 - ADAM_B2 ** ADAM_STEP)
    delta = -ADAM_LR * (m_hat / (_jnp.sqrt(v_hat) + ADAM_EPS) + ADAM_WD * w)
    return delta, m, v


def reference(x, lb_logits, norm_mix, w_in, conv_w, conv_b, w_r, b_r, w_i, b_i, lam, hg_norm, w_out, norm_mlp, w_up, w_down, norm_final, loss_target, m_lb_logits, m_norm_mix, m_w_in, m_conv_w, m_conv_b, m_w_r, m_b_r, m_w_i, m_b_i, m_lam, m_hg_norm, m_w_out, m_norm_mlp, m_w_up, m_w_down, m_norm_final, v_lb_logits, v_norm_mix, v_w_in, v_conv_w, v_conv_b, v_w_r, v_b_r, v_w_i, v_b_i, v_lam, v_hg_norm, v_w_out, v_norm_mlp, v_w_up, v_w_down, v_norm_final):
    given = dict(x=x, lb_logits=lb_logits, norm_mix=norm_mix, w_in=w_in, conv_w=conv_w, conv_b=conv_b, w_r=w_r, b_r=b_r, w_i=w_i, b_i=b_i, lam=lam, hg_norm=hg_norm, w_out=w_out, norm_mlp=norm_mlp, w_up=w_up, w_down=w_down, norm_final=norm_final, loss_target=loss_target, m_lb_logits=m_lb_logits, m_norm_mix=m_norm_mix, m_w_in=m_w_in, m_conv_w=m_conv_w, m_conv_b=m_conv_b, m_w_r=m_w_r, m_b_r=m_b_r, m_w_i=m_w_i, m_b_i=m_b_i, m_lam=m_lam, m_hg_norm=m_hg_norm, m_w_out=m_w_out, m_norm_mlp=m_norm_mlp, m_w_up=m_w_up, m_w_down=m_w_down, m_norm_final=m_norm_final, v_lb_logits=v_lb_logits, v_norm_mix=v_norm_mix, v_w_in=v_w_in, v_conv_w=v_conv_w, v_conv_b=v_conv_b, v_w_r=v_w_r, v_b_r=v_b_r, v_w_i=v_w_i, v_b_i=v_b_i, v_lam=v_lam, v_hg_norm=v_hg_norm, v_w_out=v_w_out, v_norm_mlp=v_norm_mlp, v_w_up=v_w_up, v_w_down=v_w_down, v_norm_final=v_norm_final)
    weights = {n: given[n] for n in TWIN_WEIGHTS}
    shared = {n: given[n] for n in SHARED_INPUTS}
    per_example = {n: given[n] for n in ['x']}
    grad_fn = _jax.value_and_grad(_loss, argnums=(0, 1))

    def one_microbatch(ex, loss_target):
        ex = dict(ex)
        diff = ex.pop(TWIN_DIFF_INPUT)
        return grad_fn(weights, diff, {**shared, **ex}, loss_target)

    if N_MICROBATCH == 1:
        loss, (grad_w, grad_x) = one_microbatch(per_example, given["loss_target"])
    else:
        def body(carry, xs):
            loss_sum, grad_sum = carry
            l_k, (gw_k, gx_k) = one_microbatch(xs[0], xs[1])
            with _jax.named_scope("update"):
                return (loss_sum + l_k, _jax.tree.map(_jnp.add, grad_sum, gw_k)), gx_k

        init = (_jnp.zeros((), _jnp.float32), _jax.tree.map(_jnp.zeros_like, weights))
        (loss, grad_w), grad_x = _jax.lax.scan(body, init, (per_example, given["loss_target"]))
    with _jax.named_scope("update"):
        delta_w, new_m, new_v = {}, {}, {}
        for n in TWIN_WEIGHTS:
            delta_w[n], new_m[n], new_v[n] = _adamw(weights[n], grad_w[n], given["m_" + n], given["v_" + n])
    return (loss, grad_x, *[grad_w[n] for n in TWIN_WEIGHTS], *[delta_w[n] for n in TWIN_WEIGHTS],
            *[new_m[n] for n in TWIN_WEIGHTS], *[new_v[n] for n in TWIN_WEIGHTS])
```

```python
import numpy as np

import jax
import jax.numpy as jnp
from jax import lax
from jax.experimental import pallas as pl
from jax.experimental.pallas import tpu as pltpu

F32 = jnp.float32
BF16 = jnp.bfloat16
MESH_ID = pl.DeviceIdType.MESH

N_DEV = 8
NORM_EPS = 1e-6
RG_C = 8.0
RG_BLOCK_W = 256
CONV_W = 4
HG_DK = 128
F_MIN = 1e-30
HG_CHUNK = 16
SUBLANES = 8
LANES = 128
PACK_ROWS = 16
ROW_CHUNK = 256
VMEM_LIMIT_V7X = 56 * 1024 * 1024

ADAM_LR = 0.001
ADAM_B1 = 0.9
ADAM_B2 = 0.999
ADAM_EPS = 1e-08
ADAM_WD = 0.01
ADAM_STEP = 10

GELU_C = 0.7978845608028654
GELU_K = 0.044715


def _cp(*sem):
    return pltpu.CompilerParams(dimension_semantics=sem, vmem_limit_bytes=VMEM_LIMIT_V7X)


def _row_tile(n, cap):
    if n <= cap:
        return n
    t = cap - cap % 16
    while n % t:
        t -= 16
    return t


def _dot(a, b):
    return jnp.dot(a, b, preferred_element_type=F32)


def _dot_nt(a, b):
    return lax.dot_general(a, b, (((1,), (1,)), ((), ())), preferred_element_type=F32)


def _dot_tn(a, b):
    return lax.dot_general(a, b, (((0,), (0,)), ((), ())), preferred_element_type=F32)


def _sigmoid(x):
    return jax.nn.sigmoid(x)


def _log1p_pos(y):
    return jnp.where(y < 0.01, y * (1.0 - y * (0.5 - y * (1.0 / 3.0))), jnp.log(1.0 + y))


def _softplus(x):
    return jnp.maximum(x, 0.0) + _log1p_pos(jnp.exp(-jnp.abs(x)))


def _one_minus_exp(x):
    series = -x * (1.0 + x * 0.5 * (1.0 + x * (1.0 / 3.0) * (1.0 + x * 0.25 * (1.0 + x * 0.2))))
    return jnp.where(x > -0.1, series, 1.0 - jnp.exp(x))


def _gelu_and_grad(x):
    x2 = x * x
    t = jnp.tanh(GELU_C * x * (1.0 + GELU_K * x2))
    g = 0.5 * x * (1.0 + t)
    dg = 0.5 * (1.0 + t) + 0.5 * x * (1.0 - t * t) * GELU_C * (1.0 + 3.0 * GELU_K * x2)
    return g, dg


def _silu_and_grad(x):
    s = _sigmoid(x)
    return x * s, s * (1.0 + x * (1.0 - s))


def _rstd(x):
    return lax.rsqrt(jnp.mean(x * x, axis=-1, keepdims=True) + NORM_EPS)


def _rms_bwd(dh, x, g):
    rstd = _rstd(x)
    xh = x * rstd
    dxh = dh * g
    dx = rstd * (dxh - xh * jnp.mean(dxh * xh, axis=-1, keepdims=True))
    return dx, jnp.sum(dh * xh, axis=0, keepdims=True)


def _shift_rows(x, k):
    n = x.shape[0]
    k = k % n
    return x if k == 0 else pltpu.roll(x, k, axis=0)


def _seg_cumsum(x, seg, reverse=False):
    n = x.shape[0]
    rid = lax.broadcasted_iota(jnp.int32, x.shape, 0) & (seg - 1)
    d = 1
    while d < seg:
        if reverse:
            x = jnp.where(rid < seg - d, x + _shift_rows(x, n - d), x)
        else:
            x = jnp.where(rid >= d, x + _shift_rows(x, d), x)
        d *= 2
    return x


def _scan_rows(a_ref, b_ref, out_ref, n_rows, width, reverse):
    rid = lax.broadcasted_iota(jnp.int32, (SUBLANES, width), 0)
    n_groups = n_rows // SUBLANES

    def group(i, carry):
        g = n_groups - 1 - i if reverse else i
        r0 = pl.multiple_of(g * SUBLANES, SUBLANES)
        a = a_ref[pl.ds(r0, SUBLANES), :]
        b = b_ref[pl.ds(r0, SUBLANES), :]
        for d in (1, 2, 4):
            if reverse:
                keep = rid < SUBLANES - d
                a_sh, b_sh = _shift_rows(a, SUBLANES - d), _shift_rows(b, SUBLANES - d)
            else:
                keep = rid >= d
                a_sh, b_sh = _shift_rows(a, d), _shift_rows(b, d)
            b = jnp.where(keep, a * b_sh + b, b)
            a = jnp.where(keep, a * a_sh, a)
        out = a * carry + b
        out_ref[pl.ds(r0, SUBLANES), :] = out
        edge = out[0:1, :] if reverse else out[SUBLANES - 1:SUBLANES, :]
        return jnp.broadcast_to(edge, (SUBLANES, width))

    lax.fori_loop(0, n_groups, group, jnp.zeros((SUBLANES, width), F32), unroll=4)


def _lb_softmax_rows(x_ref, depth):
    rows = [x_ref[pl.ds(l, 1), :] for l in range(depth)]
    top = rows[0]
    for r in rows[1:]:
        top = jnp.maximum(top, r)
    e = [jnp.exp(r - top) for r in rows]
    tot = e[0]
    for r in e[1:]:
        tot = tot + r
    return [r / tot for r in e]


def _lower_bounds_fwd(lb_logits):
    depth, d = lb_logits.shape

    def body(x_ref, o_ref):
        sm = _lb_softmax_rows(x_ref, depth)
        cum = jnp.zeros((1, d), F32)
        for l in range(depth):
            cum = cum + sm[l]
            o_ref[pl.ds(l, 1), :] = jnp.clip(cum - sm[0], 0.0, 1.0)

    return pl.pallas_call(body, name="lower_bounds_fwd",
                          out_shape=jax.ShapeDtypeStruct((depth, d), F32))(lb_logits)


def _lower_bounds_bwd(lb_logits, d_lbs):
    depth, d = lb_logits.shape

    def body(x_ref, g_ref, o_ref):
        sm = _lb_softmax_rows(x_ref, depth)
        cum = jnp.zeros((1, d), F32)
        d_cum = []
        for l in range(depth):
            cum = cum + sm[l]
            v = cum - sm[0]
            d_cum.append(jnp.where((v > 0.0) & (v < 1.0), g_ref[pl.ds(l, 1), :], 0.0))
        d_sm = []
        tail = jnp.zeros((1, d), F32)
        for l in reversed(range(depth)):
            tail = tail + d_cum[l]
            d_sm.append(tail)
        d_sm = d_sm[::-1]
        d_sm[0] = d_sm[0] - tail
        inner = jnp.zeros((1, d), F32)
        for l in range(depth):
            inner = inner + sm[l] * d_sm[l]
        for l in range(depth):
            o_ref[pl.ds(l, 1), :] = sm[l] * (d_sm[l] - inner)

    return pl.pallas_call(body, name="lower_bounds_bwd",
                          out_shape=jax.ShapeDtypeStruct((depth, d), F32))(lb_logits, d_lbs)


def _inproj_fwd(x, gain, w_seg):
    t_rows, d = x.shape
    tm = _row_tile(t_rows, 512)

    def body(x_ref, g_ref, w_ref, proj_ref, h_ref):
        @pl.when(pl.program_id(1) == 0)
        def _():
            xv = x_ref[...]
            h_ref[...] = (xv * _rstd(xv) * g_ref[...]).astype(BF16)

        proj_ref[...] = _dot(h_ref[...], w_ref[...])

    return pl.pallas_call(
        body, name="inproj_fwd", grid=(t_rows // tm, N_DEV),
        in_specs=[pl.BlockSpec((tm, d), lambda i, j: (i, 0)),
                  pl.BlockSpec((1, d), lambda i, j: (0, 0)),
                  pl.BlockSpec((None, d, d), lambda i, j: (j, 0, 0))],
        out_specs=[pl.BlockSpec((None, tm, d), lambda i, j: (j, i, 0)),
                   pl.BlockSpec((tm, d), lambda i, j: (i, 0))],
        out_shape=[jax.ShapeDtypeStruct((N_DEV, t_rows, d), F32),
                   jax.ShapeDtypeStruct((t_rows, d), BF16)],
        compiler_params=_cp("parallel", "arbitrary"))(x, gain, w_seg)


def _merge_out_fwd(proj, y_a, o_n, x, w_out):
    t_rows, d = x.shape
    tm = _row_tile(t_rows, 256)

    def body(g_ref, ma_ref, mb_ref, ya_ref, on_ref, x_ref, w_ref, xmid_ref, y_ref):
        g = g_ref[...]
        y = _sigmoid(ma_ref[...]) * ya_ref[...] + _sigmoid(mb_ref[...]) * (on_ref[...] * (g * _sigmoid(g)))
        yb = y.astype(BF16)
        y_ref[...] = yb
        xmid_ref[...] = x_ref[...] + _dot(yb, w_ref[...])

    seg = lambda k: pl.BlockSpec((None, tm, d), lambda i, k=k: (k, i, 0))
    row = pl.BlockSpec((tm, d), lambda i: (i, 0))
    return pl.pallas_call(
        body, name="merge_out_fwd", grid=(t_rows // tm,),
        in_specs=[seg(5), seg(6), seg(7), row, row, row, pl.BlockSpec((d, d), lambda i: (0, 0))],
        out_specs=[row, row],
        out_shape=[jax.ShapeDtypeStruct((t_rows, d), F32), jax.ShapeDtypeStruct((t_rows, d), BF16)],
        compiler_params=_cp("parallel"))(proj, proj, proj, y_a, o_n, x, w_out)


def _mlp_fwd(x_mid, gain, w_up, w_down):
    t_rows, d = x_mid.shape
    f8 = w_up.shape[2]
    tm = _row_tile(t_rows, 512)

    def body(x_ref, g_ref, wu_ref, wd_ref, out_ref, u_ref, h_ref):
        @pl.when(pl.program_id(1) == 0)
        def _():
            xv = x_ref[...]
            h_ref[...] = (xv * _rstd(xv) * g_ref[...]).astype(BF16)
            out_ref[...] = xv

        u = _dot(h_ref[...], wu_ref[...])
        u_ref[...] = u
        r = jnp.maximum(u, 0.0)
        out_ref[...] += _dot((r * r).astype(BF16), wd_ref[...])

    row = pl.BlockSpec((tm, d), lambda i, j: (i, 0))
    return pl.pallas_call(
        body, name="mlp_fwd", grid=(t_rows // tm, N_DEV),
        in_specs=[row, pl.BlockSpec((1, d), lambda i, j: (0, 0)),
                  pl.BlockSpec((None, d, f8), lambda i, j: (j, 0, 0)),
                  pl.BlockSpec((None, f8, d), lambda i, j: (j, 0, 0))],
        out_specs=[row, pl.BlockSpec((None, tm, f8), lambda i, j: (j, i, 0)), row],
        out_shape=[jax.ShapeDtypeStruct((t_rows, d), F32),
                   jax.ShapeDtypeStruct((N_DEV, t_rows, f8), F32),
                   jax.ShapeDtypeStruct((t_rows, d), BF16)],
        compiler_params=_cp("parallel", "arbitrary"))(x_mid, gain, w_up, w_down)


def _loss_head(x, gain, target):
    t_rows, d = x.shape
    tm = _row_tile(t_rows, 512)

    def body(x_ref, g_ref, t_ref, loss_ref, dx_ref, dxb_ref, dg_ref):
        @pl.when(pl.program_id(0) == 0)
        def _():
            loss_ref[...] = jnp.zeros_like(loss_ref)
            dg_ref[...] = jnp.zeros_like(dg_ref)

        xv = x_ref[...]
        g = g_ref[...]
        err = xv * _rstd(xv) * g - t_ref[...]
        loss_ref[...] += (0.5 / d) * jnp.sum(err * err)
        dx, dg = _rms_bwd(err * (1.0 / d), xv, g)
        dx_ref[...] = dx
        dxb_ref[...] = dx.astype(BF16)
        dg_ref[...] += dg

    row = pl.BlockSpec((tm, d), lambda i: (i, 0))
    vec = pl.BlockSpec((1, d), lambda i: (0, 0))
    return pl.pallas_call(
        body, name="loss_head", grid=(t_rows // tm,),
        in_specs=[row, vec, row],
        out_specs=[pl.BlockSpec((SUBLANES, LANES), lambda i: (0, 0)), row, row, vec],
        out_shape=[jax.ShapeDtypeStruct((SUBLANES, LANES), F32),
                   jax.ShapeDtypeStruct((t_rows, d), F32),
                   jax.ShapeDtypeStruct((t_rows, d), BF16),
                   jax.ShapeDtypeStruct((1, d), F32)],
        compiler_params=_cp("arbitrary"))(x, gain, target)


def _mlp_bwd(d_out, u, x_mid, gain, w_up, w_down):
    t_rows, d = x_mid.shape
    f8 = w_up.shape[2]
    tm = _row_tile(t_rows, 512)

    def body(do_ref, u_ref, x_ref, g_ref, wu_ref, wd_ref, dx_ref, dxb_ref, du_ref, act_ref, dg_ref, acc_ref):
        j = pl.program_id(1)

        @pl.when((pl.program_id(0) == 0) & (j == 0))
        def _():
            dg_ref[...] = jnp.zeros_like(dg_ref)

        @pl.when(j == 0)
        def _():
            acc_ref[...] = jnp.zeros_like(acc_ref)

        r = jnp.maximum(u_ref[...], 0.0)
        act_ref[...] = (r * r).astype(BF16)
        du = (_dot_nt(do_ref[...].astype(BF16), wd_ref[...]) * (2.0 * r)).astype(BF16)
        du_ref[...] = du
        acc_ref[...] += _dot_nt(du, wu_ref[...])

        @pl.when(j == N_DEV - 1)
        def _():
            dx, dg = _rms_bwd(acc_ref[...], x_ref[...], g_ref[...])
            dx = dx + do_ref[...]
            dx_ref[...] = dx
            dxb_ref[...] = dx.astype(BF16)
            dg_ref[...] += dg

    row = pl.BlockSpec((tm, d), lambda i, j: (i, 0))
    vec = pl.BlockSpec((1, d), lambda i, j: (0, 0))
    hid = pl.BlockSpec((None, tm, f8), lambda i, j: (j, i, 0))
    return pl.pallas_call(
        body, name="mlp_bwd", grid=(t_rows // tm, N_DEV),
        in_specs=[row, hid, row, vec,
                  pl.BlockSpec((None, d, f8), lambda i, j: (j, 0, 0)),
                  pl.BlockSpec((None, f8, d), lambda i, j: (j, 0, 0))],
        out_specs=[row, row, hid, hid, vec],
        out_shape=[jax.ShapeDtypeStruct((t_rows, d), F32),
                   jax.ShapeDtypeStruct((t_rows, d), BF16),
                   jax.ShapeDtypeStruct((N_DEV, t_rows, f8), BF16),
                   jax.ShapeDtypeStruct((N_DEV, t_rows, f8), BF16),
                   jax.ShapeDtypeStruct((1, d), F32)],
        scratch_shapes=[pltpu.VMEM((tm, d), F32)],
        compiler_params=_cp("arbitrary", "arbitrary"))(d_out, u, x_mid, gain, w_up, w_down)


def _outproj_bwd(dx_mid_b, w_out, proj, y_a, o_n):
    t_rows, d = y_a.shape
    tm = _row_tile(t_rows, 256)

    def body(dx_ref, w_ref, g_ref, ma_ref, mb_ref, ya_ref, on_ref, dya_ref, don_ref, dp_ref):
        dy = _dot_nt(dx_ref[...], w_ref[...])
        sa = _sigmoid(ma_ref[...])
        sb = _sigmoid(mb_ref[...])
        sg, dsg = _silu_and_grad(g_ref[...])
        ya = ya_ref[...]
        on = on_ref[...]
        dya_ref[...] = dy * sa
        t = dy * sb
        don_ref[...] = t * sg
        dp_ref[0] = (t * on * dsg).astype(BF16)
        dp_ref[1] = (dy * ya * sa * (1.0 - sa)).astype(BF16)
        dp_ref[2] = (dy * on * sg * sb * (1.0 - sb)).astype(BF16)

    seg = lambda k: pl.BlockSpec((None, tm, d), lambda i, k=k: (k, i, 0))
    row = pl.BlockSpec((tm, d), lambda i: (i, 0))
    return pl.pallas_call(
        body, name="outproj_bwd", grid=(t_rows // tm,),
        in_specs=[row, pl.BlockSpec((d, d), lambda i: (0, 0)), seg(5), seg(6), seg(7), row, row],
        out_specs=[row, row, pl.BlockSpec((3, tm, d), lambda i: (0, i, 0))],
        out_shape=[jax.ShapeDtypeStruct((t_rows, d), F32),
                   jax.ShapeDtypeStruct((t_rows, d), F32),
                   jax.ShapeDtypeStruct((3, t_rows, d), BF16)],
        compiler_params=_cp("parallel"))(dx_mid_b, w_out, proj, proj, proj, y_a, o_n)


def _inproj_bwd(dx_mid, dp_a, dp_b, dp_c, w_seg, x_in, gain):
    t_rows, d = x_in.shape
    tm = _row_tile(t_rows, 512)
    n_a, n_b = dp_a.shape[0], dp_b.shape[0]

    def body(dxm_ref, a_ref, b_ref, c_ref, w_ref, x_ref, g_ref, dx_ref, dxb_ref, dg_ref, acc_ref):
        j = pl.program_id(1)

        @pl.when((pl.program_id(0) == 0) & (j == 0))
        def _():
            dg_ref[...] = jnp.zeros_like(dg_ref)

        @pl.when(j == 0)
        def _():
            acc_ref[...] = jnp.zeros_like(acc_ref)

        @pl.when(j < n_a)
        def _():
            acc_ref[...] += _dot_nt(a_ref[...], w_ref[...])

        @pl.when((j >= n_a) & (j < n_a + n_b))
        def _():
            acc_ref[...] += _dot_nt(b_ref[...], w_ref[...])

        @pl.when(j >= n_a + n_b)
        def _():
            acc_ref[...] += _dot_nt(c_ref[...], w_ref[...])

        @pl.when(j == N_DEV - 1)
        def _():
            dx, dg = _rms_bwd(acc_ref[...], x_ref[...], g_ref[...])
            dx = dx + dxm_ref[...]
            dx_ref[...] = dx
            dxb_ref[...] = dx.astype(BF16)
            dg_ref[...] += dg

    def part(first, n):
        return pl.BlockSpec((None, tm, d), lambda i, j: (jnp.clip(j - first, 0, n - 1), i, 0))

    row = pl.BlockSpec((tm, d), lambda i, j: (i, 0))
    vec = pl.BlockSpec((1, d), lambda i, j: (0, 0))
    return pl.pallas_call(
        body, name="inproj_bwd", grid=(t_rows // tm, N_DEV),
        in_specs=[row, part(0, n_a), part(n_a, n_b), part(n_a + n_b, dp_c.shape[0]),
                  pl.BlockSpec((None, d, d), lambda i, j: (j, 0, 0)), row, vec],
        out_specs=[row, row, vec],
        out_shape=[jax.ShapeDtypeStruct((t_rows, d), F32),
                   jax.ShapeDtypeStruct((t_rows, d), BF16),
                   jax.ShapeDtypeStruct((1, d), F32)],
        scratch_shapes=[pltpu.VMEM((tm, d), F32)],
        compiler_params=_cp("arbitrary", "arbitrary"))(dx_mid, dp_a, dp_b, dp_c, w_seg, x_in, gain)


def _wgrad(a3, b3, name):
    n_a, t_rows, k_a = a3.shape
    n_b, _, n_cols = b3.shape
    n = max(n_a, n_b)
    bk = _row_tile(k_a, 1024)
    bn = n_cols if n_cols <= 1024 else 1024
    tt = _row_tile(t_rows, 512)
    n_t = t_rows // tt

    def body(a_ref, b_ref, o_ref):
        @pl.when(pl.program_id(3) == 0)
        def _():
            o_ref[...] = jnp.zeros_like(o_ref)

        o_ref[...] += _dot_tn(a_ref[...], b_ref[...])

    return pl.pallas_call(
        body, name=name, grid=(n, k_a // bk, n_cols // bn, n_t),
        in_specs=[pl.BlockSpec((None, tt, bk), lambda j, p, q, t: (j if n_a > 1 else 0, t, p)),
                  pl.BlockSpec((None, tt, bn), lambda j, p, q, t: (j if n_b > 1 else 0, t, q))],
        out_specs=pl.BlockSpec((None, bk, bn), lambda j, p, q, t: (j, p, q)),
        out_shape=jax.ShapeDtypeStruct((n, k_a, n_cols), F32),
        compiler_params=_cp("parallel", "parallel", "parallel", "arbitrary"))(a3, b3)


def _conv_taps(xe, n):
    return [_shift_rows(xe, CONV_W - 1 - j)[SUBLANES:SUBLANES + n, :] for j in range(CONV_W)]


def _rg_gates(xc, w_r, b_r, w_i, b_i, sp8):
    xb = xc.astype(BF16)
    r = _sigmoid(_dot(xb, w_r) + b_r)
    i = _sigmoid(_dot(xb, w_i) + b_i)
    return r, i


def _mixer_a_fwd(proj, conv_w, conv_b, w_r, b_r, w_i, b_i, lam, seq):
    _, t_rows, d = proj.shape
    n_seq, n_blk = t_rows // seq, d // RG_BLOCK_W
    wb = RG_BLOCK_W
    ch = _row_tile(seq, ROW_CHUNK)

    def body(xa_ref, ga_ref, cw_ref, cb_ref, wr_ref, br_ref, wi_ref, bi_ref, lam_ref, h_ref, ya_ref,
             xpad, a_s, u_s):
        xpad[0:SUBLANES, :] = jnp.zeros((SUBLANES, wb), F32)
        xpad[SUBLANES:, :] = xa_ref[...]
        sp8 = RG_C * _softplus(-lam_ref[...])

        def gates(c, _):
            r0 = pl.multiple_of(c * ch, ch)
            taps = _conv_taps(xpad[pl.ds(r0, ch + SUBLANES), :], ch)
            xc = cb_ref[...] + sum(cw_ref[pl.ds(j, 1), :] * taps[j] for j in range(CONV_W))
            r, i = _rg_gates(xc, wr_ref[...], br_ref[...], wi_ref[...], bi_ref[...], sp8)
            log_a = -(r * sp8)
            a_s[pl.ds(r0, ch), :] = jnp.exp(log_a)
            u_s[pl.ds(r0, ch), :] = jnp.sqrt(jnp.maximum(_one_minus_exp(2.0 * log_a), 0.0)) * (i * xc)
            return 0

        lax.fori_loop(0, seq // ch, gates, 0)
        _scan_rows(a_s, u_s, h_ref, seq, wb, reverse=False)

        def gate_out(c, _):
            r0 = pl.multiple_of(c * ch, ch)
            gl, _ = _gelu_and_grad(ga_ref[pl.ds(r0, ch), :])
            ya_ref[pl.ds(r0, ch), :] = h_ref[pl.ds(r0, ch), :] * gl
            return 0

        lax.fori_loop(0, seq // ch, gate_out, 0)

    seg = lambda k: pl.BlockSpec((None, seq, wb), lambda s, b, k=k: (k, s, b))
    blk = pl.BlockSpec((seq, wb), lambda s, b: (s, b))
    vec = pl.BlockSpec((1, wb), lambda s, b: (0, b))
    wsp = pl.BlockSpec((None, wb, wb), lambda s, b: (b, 0, 0))
    return pl.pallas_call(
        body, name="mixer_a_fwd", grid=(n_seq, n_blk),
        in_specs=[seg(0), seg(1), pl.BlockSpec((CONV_W, wb), lambda s, b: (0, b)), vec, wsp, vec, wsp, vec, vec],
        out_specs=[blk, blk],
        out_shape=[jax.ShapeDtypeStruct((t_rows, d), F32), jax.ShapeDtypeStruct((t_rows, d), F32)],
        scratch_shapes=[pltpu.VMEM((seq + SUBLANES, wb), F32), pltpu.VMEM((seq, wb), F32),
                        pltpu.VMEM((seq, wb), F32)],
        compiler_params=_cp("parallel", "parallel"))(proj, proj, conv_w, conv_b, w_r, b_r, w_i, b_i, lam)


def _mixer_a_bwd(proj, h, d_ya, conv_w, conv_b, w_r, b_r, w_i, b_i, lam, seq):
    _, t_rows, d = proj.shape
    n_seq, n_blk = t_rows // seq, d // RG_BLOCK_W
    wb = RG_BLOCK_W
    ch = _row_tile(seq, ROW_CHUNK)
    n_ch = seq // ch

    def body(xa_ref, ga_ref, h_ref, dya_ref, cw_ref, cb_ref, wr_ref, br_ref, wi_ref, bi_ref, lam_ref,
             dp_ref, dwr_ref, dwi_ref, dbr_ref, dbi_ref, dlam_ref, dcw_ref, dcb_ref,
             xpad, hpad, a_s, e_pad, g_s, xc_s, r_s, i_s, dxc_pad):
        @pl.when(pl.program_id(1) == 0)
        def _():
            for ref in (dwr_ref, dwi_ref, dbr_ref, dbi_ref, dlam_ref, dcw_ref, dcb_ref):
                ref[...] = jnp.zeros_like(ref)

        zeros8 = jnp.zeros((SUBLANES, wb), F32)
        xpad[0:SUBLANES, :] = zeros8
        xpad[SUBLANES:, :] = xa_ref[...]
        hpad[0:SUBLANES, :] = zeros8
        hpad[SUBLANES:, :] = h_ref[...]
        e_pad[seq:, :] = zeros8
        dxc_pad[seq:, :] = zeros8
        lam_v = lam_ref[...]
        sp8 = RG_C * _softplus(-lam_v)

        def recompute(c, _):
            r0 = pl.multiple_of(c * ch, ch)
            rows = pl.ds(r0, ch)
            taps = _conv_taps(xpad[pl.ds(r0, ch + SUBLANES), :], ch)
            xc = cb_ref[...] + sum(cw_ref[pl.ds(j, 1), :] * taps[j] for j in range(CONV_W))
            r, i = _rg_gates(xc, wr_ref[...], br_ref[...], wi_ref[...], bi_ref[...], sp8)
            a = jnp.exp(-(r * sp8))
            gl, dgl = _gelu_and_grad(ga_ref[rows, :])
            dya = dya_ref[rows, :]
            g = dya * gl
            dp_ref[1, rows, :] = (dya * h_ref[rows, :] * dgl).astype(BF16)
            a_s[rows, :] = a
            e_pad[rows, :] = a * g
            g_s[rows, :] = g
            xc_s[rows, :] = xc
            r_s[rows, :] = r
            i_s[rows, :] = i
            return 0

        lax.fori_loop(0, n_ch, recompute, 0)
        _scan_rows(a_s, e_pad, e_pad, seq, wb, reverse=True)

        def grads(c, _):
            r0 = pl.multiple_of(c * ch, ch)
            rows = pl.ds(r0, ch)
            halo = pl.ds(r0, ch + SUBLANES)
            dh = g_s[rows, :] + _shift_rows(e_pad[halo, :], ch + SUBLANES - 1)[0:ch, :]
            h_prev = _shift_rows(hpad[halo, :], 1)[SUBLANES:, :]
            xc, r, i = xc_s[rows, :], r_s[rows, :], i_s[rows, :]
            log_a = -(r * sp8)
            a = jnp.exp(log_a)
            om = _one_minus_exp(2.0 * log_a)
            sq = jnp.sqrt(jnp.maximum(om, 0.0))
            t1 = dh * xc
            d_i = t1 * sq
            d_la = dh * h_prev * a + jnp.where(om > 0.0, -(t1 * i) * (1.0 - om) / sq, 0.0)
            dpr = -(d_la * sp8) * r * (1.0 - r)
            dpi = d_i * i * (1.0 - i)
            dprb, dpib, xb = dpr.astype(BF16), dpi.astype(BF16), xc.astype(BF16)
            dxc = dh * sq * i + _dot_nt(dprb, wr_ref[...]) + _dot_nt(dpib, wi_ref[...])
            dwr_ref[...] += _dot_tn(xb, dprb)
            dwi_ref[...] += _dot_tn(xb, dpib)
            dbr_ref[...] += jnp.sum(dpr, axis=0, keepdims=True)
            dbi_ref[...] += jnp.sum(dpi, axis=0, keepdims=True)
            dlam_ref[...] += jnp.sum(d_la * r, axis=0, keepdims=True) * (RG_C * _sigmoid(-lam_v))
            dcb_ref[...] += jnp.sum(dxc, axis=0, keepdims=True)
            taps = _conv_taps(xpad[halo, :], ch)
            for j in range(CONV_W):
                dcw_ref[pl.ds(j, 1), :] += jnp.sum(dxc * taps[j], axis=0, keepdims=True)
            dxc_pad[rows, :] = dxc
            return 0

        lax.fori_loop(0, n_ch, grads, 0)

        def conv_bwd(c, _):
            r0 = pl.multiple_of(c * ch, ch)
            de = dxc_pad[pl.ds(r0, ch + SUBLANES), :]
            dxa = sum(cw_ref[pl.ds(j, 1), :] * _shift_rows(de, ch + SUBLANES - (CONV_W - 1 - j))[0:ch, :]
                      for j in range(CONV_W))
            dp_ref[0, pl.ds(r0, ch), :] = dxa.astype(BF16)
            return 0

        lax.fori_loop(0, n_ch, conv_bwd, 0)

    seg = lambda k: pl.BlockSpec((None, seq, wb), lambda b, s, k=k: (k, s, b))
    blk = pl.BlockSpec((seq, wb), lambda b, s: (s, b))
    vec = pl.BlockSpec((1, wb), lambda b, s: (0, b))
    taps = pl.BlockSpec((CONV_W, wb), lambda b, s: (0, b))
    wsp = pl.BlockSpec((None, wb, wb), lambda b, s: (b, 0, 0))
    vec_shape = jax.ShapeDtypeStruct((1, d), F32)
    w_shape = jax.ShapeDtypeStruct((n_blk, wb, wb), F32)
    pad = pltpu.VMEM((seq + SUBLANES, wb), F32)
    full = pltpu.VMEM((seq, wb), F32)
    return pl.pallas_call(
        body, name="mixer_a_bwd", grid=(n_blk, n_seq),
        in_specs=[seg(0), seg(1), blk, blk, taps, vec, wsp, vec, wsp, vec, vec],
        out_specs=[pl.BlockSpec((2, seq, wb), lambda b, s: (0, s, b)), wsp, wsp, vec, vec, vec, taps, vec],
        out_shape=[jax.ShapeDtypeStruct((2, t_rows, d), BF16), w_shape, w_shape, vec_shape, vec_shape,
                   vec_shape, jax.ShapeDtypeStruct((CONV_W, d), F32), vec_shape],
        scratch_shapes=[pad, pad, full, pad, full, full, full, full, pad],
        compiler_params=_cp("parallel", "arbitrary"))(
            proj, proj, h, d_ya, conv_w, conv_b, w_r, b_r, w_i, b_i, lam)


def _hg_prepare(q_ref, z_ref, lb, rows):
    z = z_ref[rows, :]
    sig = _sigmoid(z)
    fg = lb + (1.0 - lb) * sig
    log_f = jnp.log(jnp.maximum(fg, F_MIN))
    key = (1.0 - lb) * _sigmoid(-z)
    qs, _ = _silu_and_grad(q_ref[rows, :])
    return qs, key, log_f, sig, fg


def _hg_pair_terms(g_ref, r0, gc, s):
    rid = lax.broadcasted_iota(jnp.int32, (HG_CHUNK, HG_DK), 0)
    gs = g_ref[pl.ds(r0 + s, 1), :]
    return jnp.where(rid >= s, jnp.exp(jnp.minimum(gc - gs, 0.0)), 0.0)


def _hgrn_fwd(proj, lower_bound, hg_gain, seq):
    _, t_rows, d = proj.shape
    n_seq, n_head = t_rows // seq, d // HG_DK
    ch = _row_tile(seq, ROW_CHUNK)
    n_chunks = seq // HG_CHUNK

    def body(q_ref, z_ref, v_ref, lb_ref, gain_ref, o_ref, on_ref, qs_s, k_s, g_s, st_ref):
        lb = lb_ref[...]

        def prepare(c, _):
            rows = pl.ds(pl.multiple_of(c * ch, ch), ch)
            qs, key, log_f, _, _ = _hg_prepare(q_ref, z_ref, lb, rows)
            qs_s[rows, :] = qs
            k_s[rows, :] = key
            g_s[rows, :] = _seg_cumsum(log_f, HG_CHUNK)
            return 0

        lax.fori_loop(0, seq // ch, prepare, 0)
        st_ref[...] = jnp.zeros_like(st_ref)
        ones = jnp.ones((HG_DK, HG_DK), BF16)

        def chunk(c, _):
            r0 = pl.multiple_of(c * HG_CHUNK, HG_CHUNK)
            rows = pl.ds(r0, HG_CHUNK)
            qc, kc, gc, vc = qs_s[rows, :], k_s[rows, :], g_s[rows, :], v_ref[rows, :]
            g_last = gc[HG_CHUNK - 1:HG_CHUNK, :]
            st = st_ref[...]
            o = _dot_nt((qc * jnp.exp(gc)).astype(BF16), st.astype(BF16))
            pairs = [(qc * _hg_pair_terms(g_s, r0, gc, s) * k_s[pl.ds(r0 + s, 1), :]).astype(BF16)
                     for s in range(HG_CHUNK)]
            score = _dot(jnp.concatenate(pairs, axis=0), ones)
            for s in range(HG_CHUNK):
                o = o + score[s * HG_CHUNK:(s + 1) * HG_CHUNK, :] * v_ref[pl.ds(r0 + s, 1), :]
            o_ref[rows, :] = o
            k_end = kc * jnp.exp(g_last - gc)
            st_ref[...] = st * jnp.exp(g_last) + _dot_tn(vc.astype(BF16), k_end.astype(BF16))
            return 0

        lax.fori_loop(0, n_chunks, chunk, 0)

        def norm(c, _):
            rows = pl.ds(pl.multiple_of(c * ch, ch), ch)
            o = o_ref[rows, :]
            on_ref[rows, :] = o * _rstd(o) * gain_ref[...]
            return 0

        lax.fori_loop(0, seq // ch, norm, 0)

    seg = lambda k: pl.BlockSpec((None, seq, HG_DK), lambda s, h, k=k: (k, s, h))
    blk = pl.BlockSpec((seq, HG_DK), lambda s, h: (s, h))
    full = pltpu.VMEM((seq, HG_DK), F32)
    return pl.pallas_call(
        body, name="hgrn_fwd", grid=(n_seq, n_head),
        in_specs=[seg(2), seg(3), seg(4), pl.BlockSpec((1, HG_DK), lambda s, h: (0, h)),
                  pl.BlockSpec((1, HG_DK), lambda s, h: (0, 0))],
        out_specs=[blk, blk],
        out_shape=[jax.ShapeDtypeStruct((t_rows, d), F32), jax.ShapeDtypeStruct((t_rows, d), F32)],
        scratch_shapes=[full, full, full, pltpu.VMEM((HG_DK, HG_DK), F32)],
        compiler_params=_cp("parallel", "parallel"))(proj, proj, proj, lower_bound, hg_gain)


def _hgrn_bwd(proj, lower_bound, hg_gain, o, d_on, seq):
    _, t_rows, d = proj.shape
    n_seq, n_head = t_rows // seq, d // HG_DK
    ch = _row_tile(seq, ROW_CHUNK)
    n_chunks = seq // HG_CHUNK
    cc = HG_CHUNK

    def body(q_ref, z_ref, v_ref, lb_ref, gain_ref, o_ref, don_ref, dp_ref, dlb_ref, dgain_ref,
             qs_s, k_s, g_s, do_s, dqs_s, dk_s, dlf_s, states, dst_ref):
        hh, ss = pl.program_id(0), pl.program_id(1)
        lb = lb_ref[...]

        @pl.when(ss == 0)
        def _():
            dlb_ref[...] = jnp.zeros_like(dlb_ref)

        @pl.when((ss == 0) & (hh == 0))
        def _():
            dgain_ref[...] = jnp.zeros_like(dgain_ref)

        def prepare(c, _):
            rows = pl.ds(pl.multiple_of(c * ch, ch), ch)
            qs, key, log_f, _, _ = _hg_prepare(q_ref, z_ref, lb, rows)
            qs_s[rows, :] = qs
            k_s[rows, :] = key
            g_s[rows, :] = _seg_cumsum(log_f, cc)
            do, dgain = _rms_bwd(don_ref[rows, :], o_ref[rows, :], gain_ref[...])
            do_s[rows, :] = do
            dgain_ref[...] += dgain
            return 0

        lax.fori_loop(0, seq // ch, prepare, 0)

        def replay(c, st):
            rows = pl.ds(pl.multiple_of(c * cc, cc), cc)
            states[c] = st
            kc, gc, vc = k_s[rows, :], g_s[rows, :], v_ref[rows, :]
            g_last = gc[cc - 1:cc, :]
            k_end = kc * jnp.exp(g_last - gc)
            return st * jnp.exp(g_last) + _dot_tn(vc.astype(BF16), k_end.astype(BF16))

        lax.fori_loop(0, n_chunks, replay, jnp.zeros((HG_DK, HG_DK), F32))
        dst_ref[...] = jnp.zeros_like(dst_ref)
        ones = jnp.ones((HG_DK, HG_DK), BF16)
        rid = lax.broadcasted_iota(jnp.int32, (cc, HG_DK), 0)

        def chunk(i, _):
            c = n_chunks - 1 - i
            r0 = pl.multiple_of(c * cc, cc)
            rows = pl.ds(r0, cc)
            qc, kc, gc, vc, doc = qs_s[rows, :], k_s[rows, :], g_s[rows, :], v_ref[rows, :], do_s[rows, :]
            st, dst = states[c], dst_ref[...]
            g_last = gc[cc - 1:cc, :]
            e_g, e_last, e_end = jnp.exp(gc), jnp.exp(g_last), jnp.exp(g_last - gc)
            dob, dstb = doc.astype(BF16), dst.astype(BF16)
            q_in = qc * e_g
            k_end = kc * e_end
            dqs = _dot(dob, st.astype(BF16)) * e_g
            dk_state = _dot(vc.astype(BF16), dstb) * e_end
            dv = _dot_nt(k_end.astype(BF16), dstb)
            dst_ref[...] = dst * e_last + _dot_tn(dob, q_in.astype(BF16))
            d_glast = (e_last * jnp.sum(dst * st, axis=0, keepdims=True)
                       + jnp.sum(kc * dk_state, axis=0, keepdims=True))
            pairs, cots = [], []
            for s in range(cc):
                qe = qc * _hg_pair_terms(g_s, r0, gc, s)
                pairs.append((qe * k_s[pl.ds(r0 + s, 1), :]).astype(BF16))
                cots.append((doc * v_ref[pl.ds(r0 + s, 1), :]).astype(BF16))
            score = _dot(jnp.concatenate(pairs, axis=0), ones)
            d_score = _dot(jnp.concatenate(cots, axis=0), ones)
            dk = dk_state
            for s in range(cc):
                sl = slice(s * cc, (s + 1) * cc)
                decay = _hg_pair_terms(g_s, r0, gc, s)
                da_decay = d_score[sl, :] * decay
                dqs = dqs + da_decay * k_s[pl.ds(r0 + s, 1), :]
                dk_row = jnp.sum(da_decay * qc, axis=0, keepdims=True)
                dv_row = jnp.sum(score[sl, :] * doc, axis=0, keepdims=True)
                dk = jnp.where(rid == s, dk + dk_row, dk)
                dv = jnp.where(rid == s, dv + dv_row, dv)
            d_g = qc * dqs - kc * dk
            dlf_s[rows, :] = _seg_cumsum(d_g, cc, reverse=True) + d_glast
            dqs_s[rows, :] = dqs
            dk_s[rows, :] = dk
            dp_ref[2, rows, :] = dv.astype(BF16)
            return 0

        lax.fori_loop(0, n_chunks, chunk, 0)

        def finish(c, _):
            rows = pl.ds(pl.multiple_of(c * ch, ch), ch)
            z = z_ref[rows, :]
            sig = _sigmoid(z)
            nsig = _sigmoid(-z)
            fg = lb + (1.0 - lb) * sig
            _, dsilu = _silu_and_grad(q_ref[rows, :])
            dp_ref[0, rows, :] = (dqs_s[rows, :] * dsilu).astype(BF16)
            dfg = jnp.where(fg > F_MIN, dlf_s[rows, :] / fg, 0.0)
            dk = dk_s[rows, :]
            dp_ref[1, rows, :] = ((dfg - dk) * (1.0 - lb) * sig * nsig).astype(BF16)
            dlb_ref[...] += jnp.sum((dfg - dk) * nsig, axis=0, keepdims=True)
            return 0

        lax.fori_loop(0, seq // ch, finish, 0)

    seg = lambda k: pl.BlockSpec((None, seq, HG_DK), lambda h, s, k=k: (k, s, h))
    blk = pl.BlockSpec((seq, HG_DK), lambda h, s: (s, h))
    full = pltpu.VMEM((seq, HG_DK), F32)
    return pl.pallas_call(
        body, name="hgrn_bwd", grid=(n_head, n_seq),
        in_specs=[seg(2), seg(3), seg(4), pl.BlockSpec((1, HG_DK), lambda h, s: (0, h)),
                  pl.BlockSpec((1, HG_DK), lambda h, s: (0, 0)), blk, blk],
        out_specs=[pl.BlockSpec((3, seq, HG_DK), lambda h, s: (0, s, h)),
                   pl.BlockSpec((1, HG_DK), lambda h, s: (0, h)),
                   pl.BlockSpec((1, HG_DK), lambda h, s: (0, 0))],
        out_shape=[jax.ShapeDtypeStruct((3, t_rows, d), BF16), jax.ShapeDtypeStruct((1, d), F32),
                   jax.ShapeDtypeStruct((1, HG_DK), F32)],
        scratch_shapes=[full, full, full, full, full, full, full,
                        pltpu.VMEM((n_chunks, HG_DK, HG_DK), F32), pltpu.VMEM((HG_DK, HG_DK), F32)],
        compiler_params=_cp("arbitrary", "arbitrary"))(proj, proj, proj, lower_bound, hg_gain, o, d_on)


def _mesh_place():
    x, y, c = lax.axis_index("x"), lax.axis_index("y"), lax.axis_index("c")
    return x, y, c


def _peer(place, k):
    x, y, c = place
    px = 1 - x if k & 4 else x
    py = 1 - y if k & 2 else y
    pc = 1 - c if k & 1 else c
    return (px, py, pc), 4 * px + 2 * py + pc


def _exchange(src, name, gather):
    shape = (N_DEV,) + tuple(src.shape if gather else src.shape[1:])

    def body(src_ref, out_ref, send_sems, recv_sems, local_sem):
        place = _mesh_place()
        me = 4 * place[0] + 2 * place[1] + place[2]

        def outgoing(idx):
            return src_ref if gather else src_ref.at[idx]

        local = pltpu.make_async_copy(outgoing(me), out_ref.at[me], local_sem)
        local.start()
        sends = []
        for k in range(1, N_DEV):
            peer, peer_idx = _peer(place, k)
            cp = pltpu.make_async_remote_copy(
                src_ref=outgoing(peer_idx), dst_ref=out_ref.at[me],
                send_sem=send_sems.at[k - 1], recv_sem=recv_sems.at[k - 1],
                device_id=peer, device_id_type=MESH_ID)
            cp.start()
            sends.append(cp)
        for k in range(1, N_DEV):
            peer, peer_idx = _peer(place, k)
            pltpu.make_async_remote_copy(
                src_ref=outgoing(peer_idx), dst_ref=out_ref.at[peer_idx],
                send_sem=send_sems.at[k - 1], recv_sem=recv_sems.at[k - 1],
                device_id=peer, device_id_type=MESH_ID).wait_recv()
        for cp in sends:
            cp.wait_send()
        local.wait()

    any_space = pl.BlockSpec(memory_space=pl.ANY)
    return pl.pallas_call(
        body, name=name, in_specs=[any_space], out_specs=any_space,
        out_shape=jax.ShapeDtypeStruct(shape, src.dtype),
        scratch_shapes=[pltpu.SemaphoreType.DMA((N_DEV - 1,)), pltpu.SemaphoreType.DMA((N_DEV - 1,)),
                        pltpu.SemaphoreType.DMA])(src)


def _reduce_adamw(parts, w, m, v, name):
    rows, cols = w.shape
    tr = _row_tile(rows, 128)
    c1 = np.float32(1.0 - ADAM_B1 ** ADAM_STEP)
    c2 = np.float32(1.0 - ADAM_B2 ** ADAM_STEP)

    def body(p_ref, w_ref, m_ref, v_ref, g_ref, d_ref, nm_ref, nv_ref):
        g = p_ref[0]
        for k in range(1, N_DEV):
            g = g + p_ref[k]
        nm = ADAM_B1 * m_ref[...] + (1.0 - ADAM_B1) * g
        nv = ADAM_B2 * v_ref[...] + (1.0 - ADAM_B2) * (g * g)
        g_ref[...] = g
        nm_ref[...] = nm
        nv_ref[...] = nv
        d_ref[...] = -ADAM_LR * ((nm / c1) / (jnp.sqrt(nv / c2) + ADAM_EPS) + ADAM_WD * w_ref[...])

    blk = pl.BlockSpec((tr, cols), lambda i: (i, 0))
    shp = jax.ShapeDtypeStruct((rows, cols), F32)
    return pl.pallas_call(
        body, name=name, grid=(rows // tr,),
        in_specs=[pl.BlockSpec((N_DEV, tr, cols), lambda i: (0, i, 0)), blk, blk, blk],
        out_specs=[blk, blk, blk, blk], out_shape=[shp, shp, shp, shp],
        compiler_params=_cp("parallel"))(parts, w, m, v)


def _pack(arrays, lead=0):
    parts = []
    for a in arrays:
        f = a.reshape(a.shape[:lead] + (-1, LANES))
        pad = -f.shape[lead] % PACK_ROWS
        if pad:
            f = jnp.pad(f, [(0, 0)] * lead + [(0, pad), (0, 0)])
        parts.append(f)
    return jnp.concatenate(parts, axis=lead)


def _unpack(buf, shapes, lead=0):
    out, r = [], 0
    for shp in shapes:
        n = int(np.prod(shp)) // LANES
        part = lax.slice_in_dim(buf, r, r + n, axis=lead)
        out.append(part.reshape(buf.shape[:lead] + tuple(shp)))
        r += n + (-n % PACK_ROWS)
    return out


REPLICATED = ("lb_logits", "norm_mix", "conv_b", "b_r", "b_i", "lam", "hg_norm", "norm_mlp", "norm_final")
SMALL_SHARDED = ("conv_w", "w_r", "w_i")
LARGE_SHARDED = ("w_in", "w_out", "w_up", "w_down")
WEIGHTS = ("lb_logits", "norm_mix", "w_in", "conv_w", "conv_b", "w_r", "b_r", "w_i", "b_i", "lam", "hg_norm",
           "w_out", "norm_mlp", "w_up", "w_down", "norm_final")


def _gather_weights(p):
    depth, d, _ = p["w_in"].shape
    d8, f8 = d // N_DEV, p["w_up"].shape[2]
    n_blk = d // RG_BLOCK_W
    rb = RG_BLOCK_W // N_DEV
    per_layer = [p["w_in"], p["w_out"], p["w_up"].reshape(depth, f8, d), p["w_down"]]
    big = jnp.concatenate([a.astype(BF16) for a in per_layer], axis=1)
    rows_l = big.shape[1]
    big = _exchange(big.reshape(depth * rows_l, d), "gather_matmul_weights", gather=True)
    big = big.reshape(N_DEV, depth, rows_l, d)
    small = _exchange(_pack([p["conv_w"], p["w_r"], p["w_i"]]), "gather_mixer_weights", gather=True)
    conv_w, w_r, w_i = _unpack(small, [p["conv_w"].shape, p["w_r"].shape, p["w_i"].shape], lead=1)
    conv_w = conv_w.transpose(1, 2, 0, 3).reshape(depth, CONV_W, d)
    w_r = w_r.transpose(1, 2, 0, 3, 4).reshape(depth, n_blk, RG_BLOCK_W, RG_BLOCK_W).astype(BF16)
    w_i = w_i.transpose(1, 2, 0, 3, 4).reshape(depth, n_blk, RG_BLOCK_W, RG_BLOCK_W).astype(BF16)
    layers = []
    for l in range(depth):
        g = big[:, l]
        layers.append(dict(
            w_in=g[:, :d],
            w_out=g[:, d:d + d8].reshape(d, d),
            w_up=g[:, d + d8:d + d8 + f8].reshape(N_DEV, d, f8),
            w_down=g[:, d + d8 + f8:],
            conv_w=conv_w[l], w_r=w_r[l], w_i=w_i[l]))
    return layers


def _local_step(x, target, p, layers):
    bl, seq, d = x.shape
    depth = len(layers)
    t_rows = bl * seq
    row = lambda a, l: a[l:l + 1]
    lbs = _lower_bounds_fwd(p["lb_logits"])
    cur = x.reshape(t_rows, d)
    saved = []
    for l in range(depth):
        w = layers[l]
        proj, h = _inproj_fwd(cur, row(p["norm_mix"], l), w["w_in"])
        hs, y_a = _mixer_a_fwd(proj, w["conv_w"], row(p["conv_b"], l), w["w_r"], row(p["b_r"], l), w["w_i"],
                               row(p["b_i"], l), row(p["lam"], l), seq)
        o, o_n = _hgrn_fwd(proj, row(lbs, l), row(p["hg_norm"], l), seq)
        x_mid, y = _merge_out_fwd(proj, y_a, o_n, cur, w["w_out"])
        x_out, u, h2 = _mlp_fwd(x_mid, row(p["norm_mlp"], l), w["w_up"], w["w_down"])
        saved.append(dict(x_in=cur, proj=proj, h=h, hs=hs, y_a=y_a, o=o, o_n=o_n, x_mid=x_mid, y=y, u=u, h2=h2))
        cur = x_out
    loss8, dx, dxb, g_norm_final = _loss_head(cur, p["norm_final"].reshape(1, d), target.reshape(t_rows, d))
    g = {k: [None] * depth for k in WEIGHTS if k not in ("lb_logits", "norm_final")}
    d_lbs = [None] * depth
    for l in reversed(range(depth)):
        s, w = saved[l], layers[l]
        dx_mid, dx_mid_b, du, act, g["norm_mlp"][l] = _mlp_bwd(dx, s["u"], s["x_mid"], row(p["norm_mlp"], l),
                                                               w["w_up"], w["w_down"])
        g["w_down"][l] = _wgrad(act, dxb[None], "wgrad_down")
        g["w_up"][l] = _wgrad(s["h2"][None], du, "wgrad_up")
        d_ya, d_on, dp_c = _outproj_bwd(dx_mid_b, w["w_out"], s["proj"], s["y_a"], s["o_n"])
        g["w_out"][l] = _wgrad(s["y"][None], dx_mid_b[None], "wgrad_out")[0]
        dp_b, d_lbs[l], g["hg_norm"][l] = _hgrn_bwd(s["proj"], row(lbs, l), row(p["hg_norm"], l), s["o"], d_on, seq)
        (dp_a, g["w_r"][l], g["w_i"][l], g["b_r"][l], g["b_i"][l], g["lam"][l], g["conv_w"][l],
         g["conv_b"][l]) = _mixer_a_bwd(s["proj"], s["hs"], d_ya, w["conv_w"], row(p["conv_b"], l), w["w_r"],
                                        row(p["b_r"], l), w["w_i"], row(p["b_i"], l), row(p["lam"], l), seq)
        dx, dxb, g["norm_mix"][l] = _inproj_bwd(dx_mid, dp_a, dp_b, dp_c, w["w_in"], s["x_in"],
                                                row(p["norm_mix"], l))
        hb = s["h"][None]
        g["w_in"][l] = jnp.concatenate([_wgrad(hb, dp_a, "wgrad_in_pair"), _wgrad(hb, dp_b, "wgrad_in_triple"),
                                        _wgrad(hb, dp_c, "wgrad_in_triple")], axis=0)
    grads = {k: jnp.stack(v) for k, v in g.items()}
    for k in ("norm_mix", "conv_b", "b_r", "b_i", "lam", "hg_norm", "norm_mlp"):
        grads[k] = grads[k][:, 0]
    grads["lb_logits"] = _lower_bounds_bwd(p["lb_logits"], jnp.concatenate(d_lbs, axis=0))
    grads["norm_final"] = g_norm_final[0]
    return loss8[0, 0], dx.reshape(bl, seq, d), grads


def _update(p, mom1, mom2, grads):
    depth, d, _ = p["w_in"].shape
    d8, f8 = d // N_DEV, p["w_up"].shape[2]
    n_blk, rb = d // RG_BLOCK_W, RG_BLOCK_W // N_DEV
    out = {}

    to_dev = dict(
        w_in=grads["w_in"].transpose(1, 0, 2, 3).reshape(N_DEV, depth * d, d),
        w_out=grads["w_out"].reshape(depth, N_DEV, d8, d).transpose(1, 0, 2, 3).reshape(N_DEV, depth * d8, d),
        w_up=grads["w_up"].transpose(1, 0, 2, 3).reshape(N_DEV, depth * d, f8),
        w_down=grads["w_down"].transpose(1, 0, 2, 3).reshape(N_DEV, depth * f8, d))
    for k in LARGE_SHARDED:
        shp = p[k].shape
        flat = lambda a: a.reshape(shp[0] * shp[1], shp[2])
        parts = _exchange(to_dev[k], "scatter_grad_" + k, gather=False)
        res = _reduce_adamw(parts, flat(p[k]), flat(mom1[k]), flat(mom2[k]), "adamw_" + k)
        out[k] = [r.reshape(shp) for r in res]

    small_to_dev = [
        grads["conv_w"].reshape(depth, CONV_W, N_DEV, d8).transpose(2, 0, 1, 3),
        grads["w_r"].reshape(depth, n_blk, N_DEV, rb, RG_BLOCK_W).transpose(2, 0, 1, 3, 4),
        grads["w_i"].reshape(depth, n_blk, N_DEV, rb, RG_BLOCK_W).transpose(2, 0, 1, 3, 4)]
    parts = _exchange(_pack(small_to_dev, lead=1), "scatter_grad_mixer", gather=False)
    res = _reduce_adamw(parts, *[_pack([src[k] for k in SMALL_SHARDED]) for src in (p, mom1, mom2)],
                        "adamw_mixer")
    shapes = [p[k].shape for k in SMALL_SHARDED]
    for i, vals in enumerate(zip(*[_unpack(r, shapes) for r in res])):
        out[SMALL_SHARDED[i]] = list(vals)

    parts = _exchange(_pack([grads[k] for k in REPLICATED]), "gather_grad_replicated", gather=True)
    res = _reduce_adamw(parts, *[_pack([src[k] for k in REPLICATED]) for src in (p, mom1, mom2)],
                        "adamw_replicated")
    shapes = [p[k].shape for k in REPLICATED]
    for i, vals in enumerate(zip(*[_unpack(r, shapes) for r in res])):
        out[REPLICATED[i]] = list(vals)

    return tuple(out[k][i] for i in range(4) for k in WEIGHTS)


def kernel(x, lb_logits, norm_mix, w_in, conv_w, conv_b, w_r, b_r, w_i, b_i, lam, hg_norm, w_out, norm_mlp, w_up, w_down, norm_final, loss_target, m_lb_logits, m_norm_mix, m_w_in, m_conv_w, m_conv_b, m_w_r, m_b_r, m_w_i, m_b_i, m_lam, m_hg_norm, m_w_out, m_norm_mlp, m_w_up, m_w_down, m_norm_final, v_lb_logits, v_norm_mix, v_w_in, v_conv_w, v_conv_b, v_w_r, v_b_r, v_w_i, v_b_i, v_lam, v_hg_norm, v_w_out, v_norm_mlp, v_w_up, v_w_down, v_norm_final):
    p = dict(lb_logits=lb_logits, norm_mix=norm_mix, w_in=w_in, conv_w=conv_w, conv_b=conv_b, w_r=w_r, b_r=b_r,
             w_i=w_i, b_i=b_i, lam=lam, hg_norm=hg_norm, w_out=w_out, norm_mlp=norm_mlp, w_up=w_up,
             w_down=w_down, norm_final=norm_final)
    mom1 = dict(lb_logits=m_lb_logits, norm_mix=m_norm_mix, w_in=m_w_in, conv_w=m_conv_w, conv_b=m_conv_b,
                w_r=m_w_r, b_r=m_b_r, w_i=m_w_i, b_i=m_b_i, lam=m_lam, hg_norm=m_hg_norm, w_out=m_w_out,
                norm_mlp=m_norm_mlp, w_up=m_w_up, w_down=m_w_down, norm_final=m_norm_final)
    mom2 = dict(lb_logits=v_lb_logits, norm_mix=v_norm_mix, w_in=v_w_in, conv_w=v_conv_w, conv_b=v_conv_b,
                w_r=v_w_r, b_r=v_b_r, w_i=v_w_i, b_i=v_b_i, lam=v_lam, hg_norm=v_hg_norm, w_out=v_w_out,
                norm_mlp=v_norm_mlp, w_up=v_w_up, w_down=v_w_down, norm_final=v_norm_final)
    layers = _gather_weights(p)
    loss, grad_x, grads = _local_step(x, loss_target, p, layers)
    loss = lax.psum(loss, ("x", "y", "c"))
    return (loss, grad_x) + _update(p, mom1, mom2, grads)
```

```python
import numpy as np

import jax
import jax.numpy as jnp
from jax import lax
from jax.experimental import pallas as pl
from jax.experimental.pallas import tpu as pltpu

F32 = jnp.float32
BF16 = jnp.bfloat16
MESH_ID = pl.DeviceIdType.MESH

N_DEV = 8
NORM_EPS = 1e-6
RG_C = 8.0
RG_BLOCK_W = 256
CONV_W = 4
HG_DK = 128
F_MIN = 1e-30
HG_CHUNK = 16
SUBLANES = 8
LANES = 128
PACK_ROWS = 16
ROW_CHUNK = 256
VMEM_LIMIT_V7X = 56 * 1024 * 1024

ADAM_LR = 0.001
ADAM_B1 = 0.9
ADAM_B2 = 0.999
ADAM_EPS = 1e-08
ADAM_WD = 0.01
ADAM_STEP = 10

GELU_C = 0.7978845608028654
GELU_K = 0.044715


def _cp(*sem):
    return pltpu.CompilerParams(dimension_semantics=sem, vmem_limit_bytes=VMEM_LIMIT_V7X)


def _row_tile(n, cap):
    if n <= cap:
        return n
    t = cap - cap % 16
    while n % t:
        t -= 16
    return t


def _dot(a, b):
    return jnp.dot(a, b, preferred_element_type=F32)


def _dot_nt(a, b):
    return lax.dot_general(a, b, (((1,), (1,)), ((), ())), preferred_element_type=F32)


def _dot_tn(a, b):
    return lax.dot_general(a, b, (((0,), (0,)), ((), ())), preferred_element_type=F32)


def _sigmoid(x):
    return jax.nn.sigmoid(x)


def _log1p_pos(y):
    return jnp.where(y < 0.01, y * (1.0 - y * (0.5 - y * (1.0 / 3.0))), jnp.log(1.0 + y))


def _softplus(x):
    return jnp.maximum(x, 0.0) + _log1p_pos(jnp.exp(-jnp.abs(x)))


def _one_minus_exp(x):
    series = -x * (1.0 + x * 0.5 * (1.0 + x * (1.0 / 3.0) * (1.0 + x * 0.25 * (1.0 + x * 0.2))))
    return jnp.where(x > -0.1, series, 1.0 - jnp.exp(x))


def _gelu_and_grad(x):
    x2 = x * x
    t = jnp.tanh(GELU_C * x * (1.0 + GELU_K * x2))
    g = 0.5 * x * (1.0 + t)
    dg = 0.5 * (1.0 + t) + 0.5 * x * (1.0 - t * t) * GELU_C * (1.0 + 3.0 * GELU_K * x2)
    return g, dg


def _silu_and_grad(x):
    s = _sigmoid(x)
    return x * s, s * (1.0 + x * (1.0 - s))


def _rstd(x):
    return lax.rsqrt(jnp.mean(x * x, axis=-1, keepdims=True) + NORM_EPS)


def _rms_bwd(dh, x, g):
    rstd = _rstd(x)
    xh = x * rstd
    dxh = dh * g
    dx = rstd * (dxh - xh * jnp.mean(dxh * xh, axis=-1, keepdims=True))
    return dx, jnp.sum(dh * xh, axis=0, keepdims=True)


def _shift_rows(x, k):
    n = x.shape[0]
    k = k % n
    return x if k == 0 else pltpu.roll(x, k, axis=0)


def _seg_cumsum(x, seg, reverse=False):
    n = x.shape[0]
    rid = lax.broadcasted_iota(jnp.int32, x.shape, 0) & (seg - 1)
    d = 1
    while d < seg:
        if reverse:
            x = jnp.where(rid < seg - d, x + _shift_rows(x, n - d), x)
        else:
            x = jnp.where(rid >= d, x + _shift_rows(x, d), x)
        d *= 2
    return x


def _scan_rows(a_ref, b_ref, out_ref, n_rows, width, reverse):
    rid = lax.broadcasted_iota(jnp.int32, (SUBLANES, width), 0)
    n_groups = n_rows // SUBLANES

    def group(i, carry):
        g = n_groups - 1 - i if reverse else i
        r0 = pl.multiple_of(g * SUBLANES, SUBLANES)
        a = a_ref[pl.ds(r0, SUBLANES), :]
        b = b_ref[pl.ds(r0, SUBLANES), :]
        for d in (1, 2, 4):
            if reverse:
                keep = rid < SUBLANES - d
                a_sh, b_sh = _shift_rows(a, SUBLANES - d), _shift_rows(b, SUBLANES - d)
            else:
                keep = rid >= d
                a_sh, b_sh = _shift_rows(a, d), _shift_rows(b, d)
            b = jnp.where(keep, a * b_sh + b, b)
            a = jnp.where(keep, a * a_sh, a)
        out = a * carry + b
        out_ref[pl.ds(r0, SUBLANES), :] = out
        edge = out[0:1, :] if reverse else out[SUBLANES - 1:SUBLANES, :]
        return jnp.broadcast_to(edge, (SUBLANES, width))

    lax.fori_loop(0, n_groups, group, jnp.zeros((SUBLANES, width), F32), unroll=4)


def _lb_softmax_rows(x_ref, depth):
    rows = [x_ref[pl.ds(l, 1), :] for l in range(depth)]
    top = rows[0]
    for r in rows[1:]:
        top = jnp.maximum(top, r)
    e = [jnp.exp(r - top) for r in rows]
    tot = e[0]
    for r in e[1:]:
        tot = tot + r
    return [r / tot for r in e]


def _lower_bounds_fwd(lb_logits):
    depth, d = lb_logits.shape

    def body(x_ref, o_ref):
        sm = _lb_softmax_rows(x_ref, depth)
        cum = jnp.zeros((1, d), F32)
        for l in range(depth):
            cum = cum + sm[l]
            o_ref[pl.ds(l, 1), :] = jnp.clip(cum - sm[0], 0.0, 1.0)

    return pl.pallas_call(body, name="lower_bounds_fwd",
                          out_shape=jax.ShapeDtypeStruct((depth, d), F32))(lb_logits)


def _lower_bounds_bwd(lb_logits, d_lbs):
    depth, d = lb_logits.shape

    def body(x_ref, g_ref, o_ref):
        sm = _lb_softmax_rows(x_ref, depth)
        cum = jnp.zeros((1, d), F32)
        d_cum = []
        for l in range(depth):
            cum = cum + sm[l]
            v = cum - sm[0]
            d_cum.append(jnp.where((v > 0.0) & (v < 1.0), g_ref[pl.ds(l, 1), :], 0.0))
        d_sm = []
        tail = jnp.zeros((1, d), F32)
        for l in reversed(range(depth)):
            tail = tail + d_cum[l]
            d_sm.append(tail)
        d_sm = d_sm[::-1]
        d_sm[0] = d_sm[0] - tail
        inner = jnp.zeros((1, d), F32)
        for l in range(depth):
            inner = inner + sm[l] * d_sm[l]
        for l in range(depth):
            o_ref[pl.ds(l, 1), :] = sm[l] * (d_sm[l] - inner)

    return pl.pallas_call(body, name="lower_bounds_bwd",
                          out_shape=jax.ShapeDtypeStruct((depth, d), F32))(lb_logits, d_lbs)


def _inproj_fwd(x, gain, w_seg):
    t_rows, d = x.shape
    tm = _row_tile(t_rows, 512)

    def body(x_ref, g_ref, w_ref, proj_ref, h_ref):
        @pl.when(pl.program_id(1) == 0)
        def _():
            xv = x_ref[...]
            h_ref[...] = (xv * _rstd(xv) * g_ref[...]).astype(BF16)

        proj_ref[...] = _dot(h_ref[...], w_ref[...])

    return pl.pallas_call(
        body, name="inproj_fwd", grid=(t_rows // tm, N_DEV),
        in_specs=[pl.BlockSpec((tm, d), lambda i, j: (i, 0)),
                  pl.BlockSpec((1, d), lambda i, j: (0, 0)),
                  pl.BlockSpec((None, d, d), lambda i, j: (j, 0, 0))],
        out_specs=[pl.BlockSpec((None, tm, d), lambda i, j: (j, i, 0)),
                   pl.BlockSpec((tm, d), lambda i, j: (i, 0))],
        out_shape=[jax.ShapeDtypeStruct((N_DEV, t_rows, d), F32),
                   jax.ShapeDtypeStruct((t_rows, d), BF16)],
        compiler_params=_cp("parallel", "arbitrary"))(x, gain, w_seg)


def _merge_out_fwd(proj, y_a, o_n, x, w_out):
    t_rows, d = x.shape
    tm = _row_tile(t_rows, 256)

    def body(g_ref, ma_ref, mb_ref, ya_ref, on_ref, x_ref, w_ref, xmid_ref, y_ref):
        g = g_ref[...]
        y = _sigmoid(ma_ref[...]) * ya_ref[...] + _sigmoid(mb_ref[...]) * (on_ref[...] * (g * _sigmoid(g)))
        yb = y.astype(BF16)
        y_ref[...] = yb
        xmid_ref[...] = x_ref[...] + _dot(yb, w_ref[...])

    seg = lambda k: pl.BlockSpec((None, tm, d), lambda i, k=k: (k, i, 0))
    row = pl.BlockSpec((tm, d), lambda i: (i, 0))
    return pl.pallas_call(
        body, name="merge_out_fwd", grid=(t_rows // tm,),
        in_specs=[seg(5), seg(6), seg(7), row, row, row, pl.BlockSpec((d, d), lambda i: (0, 0))],
        out_specs=[row, row],
        out_shape=[jax.ShapeDtypeStruct((t_rows, d), F32), jax.ShapeDtypeStruct((t_rows, d), BF16)],
        compiler_params=_cp("parallel"))(proj, proj, proj, y_a, o_n, x, w_out)


def _mlp_fwd(x_mid, gain, w_up, w_down):
    t_rows, d = x_mid.shape
    f8 = w_up.shape[2]
    tm = _row_tile(t_rows, 512)

    def body(x_ref, g_ref, wu_ref, wd_ref, out_ref, u_ref, h_ref):
        @pl.when(pl.program_id(1) == 0)
        def _():
            xv = x_ref[...]
            h_ref[...] = (xv * _rstd(xv) * g_ref[...]).astype(BF16)
            out_ref[...] = xv

        u = _dot(h_ref[...], wu_ref[...])
        u_ref[...] = u
        r = jnp.maximum(u, 0.0)
        out_ref[...] += _dot((r * r).astype(BF16), wd_ref[...])

    row = pl.BlockSpec((tm, d), lambda i, j: (i, 0))
    return pl.pallas_call(
        body, name="mlp_fwd", grid=(t_rows // tm, N_DEV),
        in_specs=[row, pl.BlockSpec((1, d), lambda i, j: (0, 0)),
                  pl.BlockSpec((None, d, f8), lambda i, j: (j, 0, 0)),
                  pl.BlockSpec((None, f8, d), lambda i, j: (j, 0, 0))],
        out_specs=[row, pl.BlockSpec((None, tm, f8), lambda i, j: (j, i, 0)), row],
        out_shape=[jax.ShapeDtypeStruct((t_rows, d), F32),
                   jax.ShapeDtypeStruct((N_DEV, t_rows, f8), F32),
                   jax.ShapeDtypeStruct((t_rows, d), BF16)],
        compiler_params=_cp("parallel", "arbitrary"))(x_mid, gain, w_up, w_down)


def _loss_head(x, gain, target):
    t_rows, d = x.shape
    tm = _row_tile(t_rows, 512)

    def body(x_ref, g_ref, t_ref, loss_ref, dx_ref, dxb_ref, dg_ref):
        @pl.when(pl.program_id(0) == 0)
        def _():
            loss_ref[...] = jnp.zeros_like(loss_ref)
            dg_ref[...] = jnp.zeros_like(dg_ref)

        xv = x_ref[...]
        g = g_ref[...]
        err = xv * _rstd(xv) * g - t_ref[...]
        loss_ref[...] += (0.5 / d) * jnp.sum(err * err)
        dx, dg = _rms_bwd(err * (1.0 / d), xv, g)
        dx_ref[...] = dx
        dxb_ref[...] = dx.astype(BF16)
        dg_ref[...] += dg

    row = pl.BlockSpec((tm, d), lambda i: (i, 0))
    vec = pl.BlockSpec((1, d), lambda i: (0, 0))
    return pl.pallas_call(
        body, name="loss_head", grid=(t_rows // tm,),
        in_specs=[row, vec, row],
        out_specs=[pl.BlockSpec((SUBLANES, LANES), lambda i: (0, 0)), row, row, vec],
        out_shape=[jax.ShapeDtypeStruct((SUBLANES, LANES), F32),
                   jax.ShapeDtypeStruct((t_rows, d), F32),
                   jax.ShapeDtypeStruct((t_rows, d), BF16),
                   jax.ShapeDtypeStruct((1, d), F32)],
        compiler_params=_cp("arbitrary"))(x, gain, target)


def _mlp_bwd(d_out, u, x_mid, gain, w_up, w_down):
    t_rows, d = x_mid.shape
    f8 = w_up.shape[2]
    tm = _row_tile(t_rows, 512)

    def body(do_ref, u_ref, x_ref, g_ref, wu_ref, wd_ref, dx_ref, dxb_ref, du_ref, act_ref, dg_ref, acc_ref):
        j = pl.program_id(1)

        @pl.when((pl.program_id(0) == 0) & (j == 0))
        def _():
            dg_ref[...] = jnp.zeros_like(dg_ref)

        @pl.when(j == 0)
        def _():
            acc_ref[...] = jnp.zeros_like(acc_ref)

        r = jnp.maximum(u_ref[...], 0.0)
        act_ref[...] = (r * r).astype(BF16)
        du = (_dot_nt(do_ref[...].astype(BF16), wd_ref[...]) * (2.0 * r)).astype(BF16)
        du_ref[...] = du
        acc_ref[...] += _dot_nt(du, wu_ref[...])

        @pl.when(j == N_DEV - 1)
        def _():
            dx, dg = _rms_bwd(acc_ref[...], x_ref[...], g_ref[...])
            dx = dx + do_ref[...]
            dx_ref[...] = dx
            dxb_ref[...] = dx.astype(BF16)
            dg_ref[...] += dg

    row = pl.BlockSpec((tm, d), lambda i, j: (i, 0))
    vec = pl.BlockSpec((1, d), lambda i, j: (0, 0))
    hid = pl.BlockSpec((None, tm, f8), lambda i, j: (j, i, 0))
    return pl.pallas_call(
        body, name="mlp_bwd", grid=(t_rows // tm, N_DEV),
        in_specs=[row, hid, row, vec,
                  pl.BlockSpec((None, d, f8), lambda i, j: (j, 0, 0)),
                  pl.BlockSpec((None, f8, d), lambda i, j: (j, 0, 0))],
        out_specs=[row, row, hid, hid, vec],
        out_shape=[jax.ShapeDtypeStruct((t_rows, d), F32),
                   jax.ShapeDtypeStruct((t_rows, d), BF16),
                   jax.ShapeDtypeStruct((N_DEV, t_rows, f8), BF16),
                   jax.ShapeDtypeStruct((N_DEV, t_rows, f8), BF16),
                   jax.ShapeDtypeStruct((1, d), F32)],
        scratch_shapes=[pltpu.VMEM((tm, d), F32)],
        compiler_params=_cp("arbitrary", "arbitrary"))(d_out, u, x_mid, gain, w_up, w_down)


def _outproj_bwd(dx_mid_b, w_out, proj, y_a, o_n):
    t_rows, d = y_a.shape
    tm = _row_tile(t_rows, 256)

    def body(dx_ref, w_ref, g_ref, ma_ref, mb_ref, ya_ref, on_ref, dya_ref, don_ref, dp_ref):
        dy = _dot_nt(dx_ref[...], w_ref[...])
        sa = _sigmoid(ma_ref[...])
        sb = _sigmoid(mb_ref[...])
        sg, dsg = _silu_and_grad(g_ref[...])
        ya = ya_ref[...]
        on = on_ref[...]
        dya_ref[...] = dy * sa
        t = dy * sb
        don_ref[...] = t * sg
        dp_ref[0] = (t * on * dsg).astype(BF16)
        dp_ref[1] = (dy * ya * sa * (1.0 - sa)).astype(BF16)
        dp_ref[2] = (dy * on * sg * sb * (1.0 - sb)).astype(BF16)

    seg = lambda k: pl.BlockSpec((None, tm, d), lambda i, k=k: (k, i, 0))
    row = pl.BlockSpec((tm, d), lambda i: (i, 0))
    return pl.pallas_call(
        body, name="outproj_bwd", grid=(t_rows // tm,),
        in_specs=[row, pl.BlockSpec((d, d), lambda i: (0, 0)), seg(5), seg(6), seg(7), row, row],
        out_specs=[row, row, pl.BlockSpec((3, tm, d), lambda i: (0, i, 0))],
        out_shape=[jax.ShapeDtypeStruct((t_rows, d), F32),
                   jax.ShapeDtypeStruct((t_rows, d), F32),
                   jax.ShapeDtypeStruct((3, t_rows, d), BF16)],
        compiler_params=_cp("parallel"))(dx_mid_b, w_out, proj, proj, proj, y_a, o_n)


def _inproj_bwd(dx_mid, dp_a, dp_b, dp_c, w_seg, x_in, gain):
    t_rows, d = x_in.shape
    tm = _row_tile(t_rows, 512)
    n_a, n_b = dp_a.shape[0], dp_b.shape[0]

    def body(dxm_ref, a_ref, b_ref, c_ref, w_ref, x_ref, g_ref, dx_ref, dxb_ref, dg_ref, acc_ref):
        j = pl.program_id(1)

        @pl.when((pl.program_id(0) == 0) & (j == 0))
        def _():
            dg_ref[...] = jnp.zeros_like(dg_ref)

        @pl.when(j == 0)
        def _():
            acc_ref[...] = jnp.zeros_like(acc_ref)

        @pl.when(j < n_a)
        def _():
            acc_ref[...] += _dot_nt(a_ref[...], w_ref[...])

        @pl.when((j >= n_a) & (j < n_a + n_b))
        def _():
            acc_ref[...] += _dot_nt(b_ref[...], w_ref[...])

        @pl.when(j >= n_a + n_b)
        def _():
            acc_ref[...] += _dot_nt(c_ref[...], w_ref[...])

        @pl.when(j == N_DEV - 1)
        def _():
            dx, dg = _rms_bwd(acc_ref[...], x_ref[...], g_ref[...])
            dx = dx + dxm_ref[...]
            dx_ref[...] = dx
            dxb_ref[...] = dx.astype(BF16)
            dg_ref[...] += dg

    def part(first, n):
        return pl.BlockSpec((None, tm, d), lambda i, j: (jnp.clip(j - first, 0, n - 1), i, 0))

    row = pl.BlockSpec((tm, d), lambda i, j: (i, 0))
    vec = pl.BlockSpec((1, d), lambda i, j: (0, 0))
    return pl.pallas_call(
        body, name="inproj_bwd", grid=(t_rows // tm, N_DEV),
        in_specs=[row, part(0, n_a), part(n_a, n_b), part(n_a + n_b, dp_c.shape[0]),
                  pl.BlockSpec((None, d, d), lambda i, j: (j, 0, 0)), row, vec],
        out_specs=[row, row, vec],
        out_shape=[jax.ShapeDtypeStruct((t_rows, d), F32),
                   jax.ShapeDtypeStruct((t_rows, d), BF16),
                   jax.ShapeDtypeStruct((1, d), F32)],
        scratch_shapes=[pltpu.VMEM((tm, d), F32)],
        compiler_params=_cp("arbitrary", "arbitrary"))(dx_mid, dp_a, dp_b, dp_c, w_seg, x_in, gain)


def _wgrad(a3, b3, name):
    n_a, t_rows, k_a = a3.shape
    n_b, _, n_cols = b3.shape
    n = max(n_a, n_b)
    bk = _row_tile(k_a, 1024)
    bn = n_cols if n_cols <= 1024 else 1024
    tt = _row_tile(t_rows, 512)
    n_t = t_rows // tt

    def body(a_ref, b_ref, o_ref):
        @pl.when(pl.program_id(3) == 0)
        def _():
            o_ref[...] = jnp.zeros_like(o_ref)

        o_ref[...] += _dot_tn(a_ref[...], b_ref[...])

    return pl.pallas_call(
        body, name=name, grid=(n, k_a // bk, n_cols // bn, n_t),
        in_specs=[pl.BlockSpec((None, tt, bk), lambda j, p, q, t: (j if n_a > 1 else 0, t, p)),
                  pl.BlockSpec((None, tt, bn), lambda j, p, q, t: (j if n_b > 1 else 0, t, q))],
        out_specs=pl.BlockSpec((None, bk, bn), lambda j, p, q, t: (j, p, q)),
        out_shape=jax.ShapeDtypeStruct((n, k_a, n_cols), F32),
        compiler_params=_cp("parallel", "parallel", "parallel", "arbitrary"))(a3, b3)


def _conv_taps(xe, n):
    return [_shift_rows(xe, CONV_W - 1 - j)[SUBLANES:SUBLANES + n, :] for j in range(CONV_W)]


def _rg_gates(xc, w_r, b_r, w_i, b_i, sp8):
    xb = xc.astype(BF16)
    r = _sigmoid(_dot(xb, w_r) + b_r)
    i = _sigmoid(_dot(xb, w_i) + b_i)
    return r, i


def _mixer_a_fwd(proj, conv_w, conv_b, w_r, b_r, w_i, b_i, lam, seq):
    _, t_rows, d = proj.shape
    n_seq, n_blk = t_rows // seq, d // RG_BLOCK_W
    wb = RG_BLOCK_W
    ch = _row_tile(seq, ROW_CHUNK)

    def body(xa_ref, ga_ref, cw_ref, cb_ref, wr_ref, br_ref, wi_ref, bi_ref, lam_ref, h_ref, ya_ref,
             xpad, a_s, u_s):
        xpad[0:SUBLANES, :] = jnp.zeros((SUBLANES, wb), F32)
        xpad[SUBLANES:, :] = xa_ref[...]
        sp8 = RG_C * _softplus(-lam_ref[...])

        def gates(c, _):
            r0 = pl.multiple_of(c * ch, ch)
            taps = _conv_taps(xpad[pl.ds(r0, ch + SUBLANES), :], ch)
            xc = cb_ref[...] + sum(cw_ref[pl.ds(j, 1), :] * taps[j] for j in range(CONV_W))
            r, i = _rg_gates(xc, wr_ref[...], br_ref[...], wi_ref[...], bi_ref[...], sp8)
            log_a = -(r * sp8)
            a_s[pl.ds(r0, ch), :] = jnp.exp(log_a)
            u_s[pl.ds(r0, ch), :] = jnp.sqrt(jnp.maximum(_one_minus_exp(2.0 * log_a), 0.0)) * (i * xc)
            return 0

        lax.fori_loop(0, seq // ch, gates, 0)
        _scan_rows(a_s, u_s, h_ref, seq, wb, reverse=False)

        def gate_out(c, _):
            r0 = pl.multiple_of(c * ch, ch)
            gl, _ = _gelu_and_grad(ga_ref[pl.ds(r0, ch), :])
            ya_ref[pl.ds(r0, ch), :] = h_ref[pl.ds(r0, ch), :] * gl
            return 0

        lax.fori_loop(0, seq // ch, gate_out, 0)

    seg = lambda k: pl.BlockSpec((None, seq, wb), lambda s, b, k=k: (k, s, b))
    blk = pl.BlockSpec((seq, wb), lambda s, b: (s, b))
    vec = pl.BlockSpec((1, wb), lambda s, b: (0, b))
    wsp = pl.BlockSpec((None, wb, wb), lambda s, b: (b, 0, 0))
    return pl.pallas_call(
        body, name="mixer_a_fwd", grid=(n_seq, n_blk),
        in_specs=[seg(0), seg(1), pl.BlockSpec((CONV_W, wb), lambda s, b: (0, b)), vec, wsp, vec, wsp, vec, vec],
        out_specs=[blk, blk],
        out_shape=[jax.ShapeDtypeStruct((t_rows, d), F32), jax.ShapeDtypeStruct((t_rows, d), F32)],
        scratch_shapes=[pltpu.VMEM((seq + SUBLANES, wb), F32), pltpu.VMEM((seq, wb), F32),
                        pltpu.VMEM((seq, wb), F32)],
        compiler_params=_cp("parallel", "parallel"))(proj, proj, conv_w, conv_b, w_r, b_r, w_i, b_i, lam)


def _mixer_a_bwd(proj, h, d_ya, conv_w, conv_b, w_r, b_r, w_i, b_i, lam, seq):
    _, t_rows, d = proj.shape
    n_seq, n_blk = t_rows // seq, d // RG_BLOCK_W
    wb = RG_BLOCK_W
    ch = _row_tile(seq, ROW_CHUNK)
    n_ch = seq // ch

    def body(xa_ref, ga_ref, h_ref, dya_ref, cw_ref, cb_ref, wr_ref, br_ref, wi_ref, bi_ref, lam_ref,
             dp_ref, dwr_ref, dwi_ref, dbr_ref, dbi_ref, dlam_ref, dcw_ref, dcb_ref,
             xpad, hpad, a_s, e_pad, g_s, xc_s, r_s, i_s, dxc_pad):
        @pl.when(pl.program_id(1) == 0)
        def _():
            for ref in (dwr_ref, dwi_ref, dbr_ref, dbi_ref, dlam_ref, dcw_ref, dcb_ref):
                ref[...] = jnp.zeros_like(ref)

        zeros8 = jnp.zeros((SUBLANES, wb), F32)
        xpad[0:SUBLANES, :] = zeros8
        xpad[SUBLANES:, :] = xa_ref[...]
        hpad[0:SUBLANES, :] = zeros8
        hpad[SUBLANES:, :] = h_ref[...]
        e_pad[seq:, :] = zeros8
        dxc_pad[seq:, :] = zeros8
        lam_v = lam_ref[...]
        sp8 = RG_C * _softplus(-lam_v)

        def recompute(c, _):
            r0 = pl.multiple_of(c * ch, ch)
            rows = pl.ds(r0, ch)
            taps = _conv_taps(xpad[pl.ds(r0, ch + SUBLANES), :], ch)
            xc = cb_ref[...] + sum(cw_ref[pl.ds(j, 1), :] * taps[j] for j in range(CONV_W))
            r, i = _rg_gates(xc, wr_ref[...], br_ref[...], wi_ref[...], bi_ref[...], sp8)
            a = jnp.exp(-(r * sp8))
            gl, dgl = _gelu_and_grad(ga_ref[rows, :])
            dya = dya_ref[rows, :]
            g = dya * gl
            dp_ref[1, rows, :] = (dya * h_ref[rows, :] * dgl).astype(BF16)
            a_s[rows, :] = a
            e_pad[rows, :] = a * g
            g_s[rows, :] = g
            xc_s[rows, :] = xc
            r_s[rows, :] = r
            i_s[rows, :] = i
            return 0

        lax.fori_loop(0, n_ch, recompute, 0)
        _scan_rows(a_s, e_pad, e_pad, seq, wb, reverse=True)

        def grads(c, _):
            r0 = pl.multiple_of(c * ch, ch)
            rows = pl.ds(r0, ch)
            halo = pl.ds(r0, ch + SUBLANES)
            dh = g_s[rows, :] + _shift_rows(e_pad[halo, :], ch + SUBLANES - 1)[0:ch, :]
            h_prev = _shift_rows(hpad[halo, :], 1)[SUBLANES:, :]
            xc, r, i = xc_s[rows, :], r_s[rows, :], i_s[rows, :]
            log_a = -(r * sp8)
            a = jnp.exp(log_a)
            om = _one_minus_exp(2.0 * log_a)
            sq = jnp.sqrt(jnp.maximum(om, 0.0))
            t1 = dh * xc
            d_i = t1 * sq
            d_la = dh * h_prev * a + jnp.where(om > 0.0, -(t1 * i) * (1.0 - om) / sq, 0.0)
            dpr = -(d_la * sp8) * r * (1.0 - r)
            dpi = d_i * i * (1.0 - i)
            dprb, dpib, xb = dpr.astype(BF16), dpi.astype(BF16), xc.astype(BF16)
            dxc = dh * sq * i + _dot_nt(dprb, wr_ref[...]) + _dot_nt(dpib, wi_ref[...])
            dwr_ref[...] += _dot_tn(xb, dprb)
            dwi_ref[...] += _dot_tn(xb, dpib)
            dbr_ref[...] += jnp.sum(dpr, axis=0, keepdims=True)
            dbi_ref[...] += jnp.sum(dpi, axis=0, keepdims=True)
            dlam_ref[...] += jnp.sum(d_la * r, axis=0, keepdims=True) * (RG_C * _sigmoid(-lam_v))
            dcb_ref[...] += jnp.sum(dxc, axis=0, keepdims=True)
            taps = _conv_taps(xpad[halo, :], ch)
            for j in range(CONV_W):
                dcw_ref[pl.ds(j, 1), :] += jnp.sum(dxc * taps[j], axis=0, keepdims=True)
            dxc_pad[rows, :] = dxc
            return 0

        lax.fori_loop(0, n_ch, grads, 0)

        def conv_bwd(c, _):
            r0 = pl.multiple_of(c * ch, ch)
            de = dxc_pad[pl.ds(r0, ch + SUBLANES), :]
            dxa = sum(cw_ref[pl.ds(j, 1), :] * _shift_rows(de, ch + SUBLANES - (CONV_W - 1 - j))[0:ch, :]
                      for j in range(CONV_W))
            dp_ref[0, pl.ds(r0, ch), :] = dxa.astype(BF16)
            return 0

        lax.fori_loop(0, n_ch, conv_bwd, 0)

    seg = lambda k: pl.BlockSpec((None, seq, wb), lambda b, s, k=k: (k, s, b))
    blk = pl.BlockSpec((seq, wb), lambda b, s: (s, b))
    vec = pl.BlockSpec((1, wb), lambda b, s: (0, b))
    taps = pl.BlockSpec((CONV_W, wb), lambda b, s: (0, b))
    wsp = pl.BlockSpec((None, wb, wb), lambda b, s: (b, 0, 0))
    vec_shape = jax.ShapeDtypeStruct((1, d), F32)
    w_shape = jax.ShapeDtypeStruct((n_blk, wb, wb), F32)
    pad = pltpu.VMEM((seq + SUBLANES, wb), F32)
    full = pltpu.VMEM((seq, wb), F32)
    return pl.pallas_call(
        body, name="mixer_a_bwd", grid=(n_blk, n_seq),
        in_specs=[seg(0), seg(1), blk, blk, taps, vec, wsp, vec, wsp, vec, vec],
        out_specs=[pl.BlockSpec((2, seq, wb), lambda b, s: (0, s, b)), wsp, wsp, vec, vec, vec, taps, vec],
        out_shape=[jax.ShapeDtypeStruct((2, t_rows, d), BF16), w_shape, w_shape, vec_shape, vec_shape,
                   vec_shape, jax.ShapeDtypeStruct((CONV_W, d), F32), vec_shape],
        scratch_shapes=[pad, pad, full, pad, full, full, full, full, pad],
        compiler_params=_cp("parallel", "arbitrary"))(
            proj, proj, h, d_ya, conv_w, conv_b, w_r, b_r, w_i, b_i, lam)


def _hg_prepare(q_ref, z_ref, lb, rows):
    z = z_ref[rows, :]
    sig = _sigmoid(z)
    fg = lb + (1.0 - lb) * sig
    log_f = jnp.log(jnp.maximum(fg, F_MIN))
    key = (1.0 - lb) * _sigmoid(-z)
    qs, _ = _silu_and_grad(q_ref[rows, :])
    return qs, key, log_f, sig, fg


def _hg_pair_terms(g_ref, r0, gc, s):
    rid = lax.broadcasted_iota(jnp.int32, (HG_CHUNK, HG_DK), 0)
    gs = g_ref[pl.ds(r0 + s, 1), :]
    return jnp.where(rid >= s, jnp.exp(gc - gs), 0.0)


def _chunk_row_sums():
    row = lax.broadcasted_iota(jnp.int32, (HG_CHUNK, HG_CHUNK * HG_CHUNK), 0)
    col = lax.broadcasted_iota(jnp.int32, (HG_CHUNK, HG_CHUNK * HG_CHUNK), 1)
    lo = row * HG_CHUNK
    return jnp.where((col >= lo) & (col < lo + HG_CHUNK), 1.0, 0.0).astype(BF16)


def _for_chunks(n, unroll, *stages):
    unroll = min(unroll, n)
    assert n % unroll == 0

    def trip(i, _):
        chunks = [i * unroll + u for u in range(unroll)]
        carried = [stages[0](c) for c in chunks]
        for stage in stages[1:]:
            carried = [stage(c, x) for c, x in zip(chunks, carried)]
        return 0

    lax.fori_loop(0, n // unroll, trip, 0)


def _hg_state_terms(v_ref, k_ref, g_ref, states, n_chunks):
    def one(c):
        rows = pl.ds(pl.multiple_of(c * HG_CHUNK, HG_CHUNK), HG_CHUNK)
        gc = g_ref[rows, :]
        k_end = k_ref[rows, :] * jnp.exp(gc[HG_CHUNK - 1:HG_CHUNK, :] - gc)
        states[c] = _dot_tn(v_ref[rows, :].astype(BF16), k_end.astype(BF16))

    _for_chunks(n_chunks, 8, one)


def _hg_state_chain(states, g_ref, carry_ref, n_chunks, reverse):
    unroll = min(8, n_chunks)
    assert n_chunks % unroll == 0
    carry_ref[...] = jnp.zeros_like(carry_ref)

    def trip(i, _):
        st = carry_ref[...]
        for u in range(unroll):
            k = i * unroll + u
            c = n_chunks - 1 - k if reverse else k
            term = states[c]
            states[c] = st
            st = st * jnp.exp(g_ref[pl.ds(c * HG_CHUNK + HG_CHUNK - 1, 1), :]) + term
        carry_ref[...] = st
        return 0

    lax.fori_loop(0, n_chunks // unroll, trip, 0)


def _hgrn_fwd(proj, lower_bound, hg_gain, seq):
    _, t_rows, d = proj.shape
    n_seq, n_head = t_rows // seq, d // HG_DK
    ch = _row_tile(seq, ROW_CHUNK)
    n_chunks = seq // HG_CHUNK

    def body(q_ref, z_ref, v_ref, lb_ref, gain_ref, o_ref, on_ref, qs_s, k_s, g_s, states, st_ref):
        lb = lb_ref[...]

        def prepare(c, _):
            rows = pl.ds(pl.multiple_of(c * ch, ch), ch)
            qs, key, log_f, _, _ = _hg_prepare(q_ref, z_ref, lb, rows)
            qs_s[rows, :] = qs
            k_s[rows, :] = key
            g_s[rows, :] = _seg_cumsum(log_f, HG_CHUNK)
            return 0

        lax.fori_loop(0, seq // ch, prepare, 0)
        _hg_state_terms(v_ref, k_s, g_s, states, n_chunks)
        _hg_state_chain(states, g_s, st_ref, n_chunks, reverse=False)
        ones = jnp.ones((HG_DK, HG_DK), BF16)

        def issue(c):
            r0 = pl.multiple_of(c * HG_CHUNK, HG_CHUNK)
            rows = pl.ds(r0, HG_CHUNK)
            qc, gc = qs_s[rows, :], g_s[rows, :]
            o = _dot_nt((qc * jnp.exp(gc)).astype(BF16), states[c].astype(BF16))
            pairs = [(qc * _hg_pair_terms(g_s, r0, gc, s) * k_s[pl.ds(r0 + s, 1), :]).astype(BF16)
                     for s in range(HG_CHUNK)]
            score = _dot(jnp.concatenate(pairs, axis=0), ones)
            return o, score

        def combine(c, issued):
            o, score = issued
            r0 = pl.multiple_of(c * HG_CHUNK, HG_CHUNK)
            for s in range(HG_CHUNK):
                o = o + score[s * HG_CHUNK:(s + 1) * HG_CHUNK, :] * v_ref[pl.ds(r0 + s, 1), :]
            o_ref[pl.ds(r0, HG_CHUNK), :] = o

        _for_chunks(n_chunks, 8, issue, combine)

        def norm(c, _):
            rows = pl.ds(pl.multiple_of(c * ch, ch), ch)
            o = o_ref[rows, :]
            on_ref[rows, :] = o * _rstd(o) * gain_ref[...]
            return 0

        lax.fori_loop(0, seq // ch, norm, 0)

    seg = lambda k: pl.BlockSpec((None, seq, HG_DK), lambda s, h, k=k: (k, s, h))
    blk = pl.BlockSpec((seq, HG_DK), lambda s, h: (s, h))
    full = pltpu.VMEM((seq, HG_DK), F32)
    return pl.pallas_call(
        body, name="hgrn_fwd", grid=(n_seq, n_head),
        in_specs=[seg(2), seg(3), seg(4), pl.BlockSpec((1, HG_DK), lambda s, h: (0, h)),
                  pl.BlockSpec((1, HG_DK), lambda s, h: (0, 0))],
        out_specs=[blk, blk],
        out_shape=[jax.ShapeDtypeStruct((t_rows, d), F32), jax.ShapeDtypeStruct((t_rows, d), F32)],
        scratch_shapes=[full, full, full, pltpu.VMEM((n_chunks, HG_DK, HG_DK), F32),
                        pltpu.VMEM((HG_DK, HG_DK), F32)],
        compiler_params=_cp("parallel", "parallel"))(proj, proj, proj, lower_bound, hg_gain)


def _hgrn_bwd(proj, lower_bound, hg_gain, o, d_on, seq):
    _, t_rows, d = proj.shape
    n_seq, n_head = t_rows // seq, d // HG_DK
    ch = _row_tile(seq, ROW_CHUNK)
    n_chunks = seq // HG_CHUNK
    cc = HG_CHUNK

    def body(q_ref, z_ref, v_ref, lb_ref, gain_ref, o_ref, don_ref, dp_ref, dlb_ref, dgain_ref,
             qs_s, k_s, g_s, do_s, dqs_s, dk_s, dlf_s, states, dstates, carry_ref):
        hh, ss = pl.program_id(0), pl.program_id(1)
        lb = lb_ref[...]

        @pl.when(ss == 0)
        def _():
            dlb_ref[...] = jnp.zeros_like(dlb_ref)

        @pl.when((ss == 0) & (hh == 0))
        def _():
            dgain_ref[...] = jnp.zeros_like(dgain_ref)

        def prepare(c, _):
            rows = pl.ds(pl.multiple_of(c * ch, ch), ch)
            qs, key, log_f, _, _ = _hg_prepare(q_ref, z_ref, lb, rows)
            qs_s[rows, :] = qs
            k_s[rows, :] = key
            g_s[rows, :] = _seg_cumsum(log_f, cc)
            do, dgain = _rms_bwd(don_ref[rows, :], o_ref[rows, :], gain_ref[...])
            do_s[rows, :] = do
            dgain_ref[...] += dgain
            return 0

        lax.fori_loop(0, seq // ch, prepare, 0)

        _hg_state_terms(v_ref, k_s, g_s, states, n_chunks)
        _hg_state_chain(states, g_s, carry_ref, n_chunks, reverse=False)

        def query_term(c):
            rows = pl.ds(pl.multiple_of(c * cc, cc), cc)
            q_in = qs_s[rows, :] * jnp.exp(g_s[rows, :])
            dstates[c] = _dot_tn(do_s[rows, :].astype(BF16), q_in.astype(BF16))

        _for_chunks(n_chunks, 8, query_term)
        _hg_state_chain(dstates, g_s, carry_ref, n_chunks, reverse=True)
        ones = jnp.ones((HG_DK, HG_DK), BF16)
        row_sums = _chunk_row_sums()

        def chunk_rows(c):
            r0 = pl.multiple_of(c * cc, cc)
            return r0, pl.ds(r0, cc)

        def through_state(c):
            r0, rows = chunk_rows(c)
            kc, gc, vc, doc = k_s[rows, :], g_s[rows, :], v_ref[rows, :], do_s[rows, :]
            st, dst = states[c], dstates[c]
            g_last = gc[cc - 1:cc, :]
            e_last, e_end = jnp.exp(g_last), jnp.exp(g_last - gc)
            dob, dstb = doc.astype(BF16), dst.astype(BF16)
            dqs = _dot(dob, st.astype(BF16))
            dk_state = _dot(vc.astype(BF16), dstb)
            dv = _dot_nt((kc * e_end).astype(BF16), dstb)
            cots = [(doc * v_ref[pl.ds(r0 + s, 1), :]).astype(BF16) for s in range(cc)]
            d_score = _dot(jnp.concatenate(cots, axis=0), ones)
            return dqs, dk_state, dv, d_score, e_last * jnp.sum(dst * st, axis=0, keepdims=True)

        def pair_terms(c, x):
            dqs, dk_state, dv, d_score, d_glast = x
            r0, rows = chunk_rows(c)
            qc, kc, gc = qs_s[rows, :], k_s[rows, :], g_s[rows, :]
            dqs = dqs * jnp.exp(gc)
            dk_state = dk_state * jnp.exp(gc[cc - 1:cc, :] - gc)
            d_glast = d_glast + jnp.sum(kc * dk_state, axis=0, keepdims=True)
            pairs, dk_terms = [], []
            for s in range(cc):
                decay = _hg_pair_terms(g_s, r0, gc, s)
                ks = k_s[pl.ds(r0 + s, 1), :]
                da_decay = d_score[s * cc:(s + 1) * cc, :] * decay
                pairs.append((qc * decay * ks).astype(BF16))
                dk_terms.append((da_decay * qc).astype(BF16))
                dqs = dqs + da_decay * ks
            score = _dot(jnp.concatenate(pairs, axis=0), ones)
            dk = dk_state + _dot(row_sums, jnp.concatenate(dk_terms, axis=0))
            return dqs, dk, dv, score, d_glast

        def value_terms(c, x):
            dqs, dk, dv, score, d_glast = x
            _, rows = chunk_rows(c)
            doc = do_s[rows, :]
            dv_terms = [(score[s * cc:(s + 1) * cc, :] * doc).astype(BF16) for s in range(cc)]
            return dqs, dk, dv + _dot(row_sums, jnp.concatenate(dv_terms, axis=0)), d_glast

        def store(c, x):
            dqs, dk, dv, d_glast = x
            _, rows = chunk_rows(c)
            d_g = qs_s[rows, :] * dqs - k_s[rows, :] * dk
            dlf_s[rows, :] = _seg_cumsum(d_g, cc, reverse=True) + d_glast
            dqs_s[rows, :] = dqs
            dk_s[rows, :] = dk
            dp_ref[2, rows, :] = dv.astype(BF16)

        _for_chunks(n_chunks, 8, through_state, pair_terms, value_terms, store)

        def finish(c, _):
            rows = pl.ds(pl.multiple_of(c * ch, ch), ch)
            z = z_ref[rows, :]
            sig = _sigmoid(z)
            nsig = _sigmoid(-z)
            fg = lb + (1.0 - lb) * sig
            _, dsilu = _silu_and_grad(q_ref[rows, :])
            dp_ref[0, rows, :] = (dqs_s[rows, :] * dsilu).astype(BF16)
            dfg = jnp.where(fg > F_MIN, dlf_s[rows, :] / fg, 0.0)
            dk = dk_s[rows, :]
            dp_ref[1, rows, :] = ((dfg - dk) * (1.0 - lb) * sig * nsig).astype(BF16)
            dlb_ref[...] += jnp.sum((dfg - dk) * nsig, axis=0, keepdims=True)
            return 0

        lax.fori_loop(0, seq // ch, finish, 0)

    seg = lambda k: pl.BlockSpec((None, seq, HG_DK), lambda h, s, k=k: (k, s, h))
    blk = pl.BlockSpec((seq, HG_DK), lambda h, s: (s, h))
    full = pltpu.VMEM((seq, HG_DK), F32)
    return pl.pallas_call(
        body, name="hgrn_bwd", grid=(n_head, n_seq),
        in_specs=[seg(2), seg(3), seg(4), pl.BlockSpec((1, HG_DK), lambda h, s: (0, h)),
                  pl.BlockSpec((1, HG_DK), lambda h, s: (0, 0)), blk, blk],
        out_specs=[pl.BlockSpec((3, seq, HG_DK), lambda h, s: (0, s, h)),
                   pl.BlockSpec((1, HG_DK), lambda h, s: (0, h)),
                   pl.BlockSpec((1, HG_DK), lambda h, s: (0, 0))],
        out_shape=[jax.ShapeDtypeStruct((3, t_rows, d), BF16), jax.ShapeDtypeStruct((1, d), F32),
                   jax.ShapeDtypeStruct((1, HG_DK), F32)],
        scratch_shapes=[full, full, full, full, full, full, full,
                        pltpu.VMEM((n_chunks, HG_DK, HG_DK), F32), pltpu.VMEM((n_chunks, HG_DK, HG_DK), F32),
                        pltpu.VMEM((HG_DK, HG_DK), F32)],
        compiler_params=_cp("arbitrary", "arbitrary"))(proj, proj, proj, lower_bound, hg_gain, o, d_on)


def _mesh_place():
    x, y, c = lax.axis_index("x"), lax.axis_index("y"), lax.axis_index("c")
    return x, y, c


def _peer(place, k):
    x, y, c = place
    px = 1 - x if k & 4 else x
    py = 1 - y if k & 2 else y
    pc = 1 - c if k & 1 else c
    return (px, py, pc), 4 * px + 2 * py + pc


def _exchange(src, name, gather):
    shape = (N_DEV,) + tuple(src.shape if gather else src.shape[1:])

    def body(src_ref, out_ref, send_sems, recv_sems, local_sem):
        place = _mesh_place()
        me = 4 * place[0] + 2 * place[1] + place[2]

        def outgoing(idx):
            return src_ref if gather else src_ref.at[idx]

        local = pltpu.make_async_copy(outgoing(me), out_ref.at[me], local_sem)
        local.start()
        sends = []
        for k in range(1, N_DEV):
            peer, peer_idx = _peer(place, k)
            cp = pltpu.make_async_remote_copy(
                src_ref=outgoing(peer_idx), dst_ref=out_ref.at[me],
                send_sem=send_sems.at[k - 1], recv_sem=recv_sems.at[k - 1],
                device_id=peer, device_id_type=MESH_ID)
            cp.start()
            sends.append(cp)
        for k in range(1, N_DEV):
            peer, peer_idx = _peer(place, k)
            pltpu.make_async_remote_copy(
                src_ref=outgoing(peer_idx), dst_ref=out_ref.at[peer_idx],
                send_sem=send_sems.at[k - 1], recv_sem=recv_sems.at[k - 1],
                device_id=peer, device_id_type=MESH_ID).wait_recv()
        for cp in sends:
            cp.wait_send()
        local.wait()

    any_space = pl.BlockSpec(memory_space=pl.ANY)
    return pl.pallas_call(
        body, name=name, in_specs=[any_space], out_specs=any_space,
        out_shape=jax.ShapeDtypeStruct(shape, src.dtype),
        scratch_shapes=[pltpu.SemaphoreType.DMA((N_DEV - 1,)), pltpu.SemaphoreType.DMA((N_DEV - 1,)),
                        pltpu.SemaphoreType.DMA])(src)


def _reduce_adamw(parts, w, m, v, name):
    rows, cols = w.shape
    tr = _row_tile(rows, 128)
    c1 = np.float32(1.0 - ADAM_B1 ** ADAM_STEP)
    c2 = np.float32(1.0 - ADAM_B2 ** ADAM_STEP)

    def body(p_ref, w_ref, m_ref, v_ref, g_ref, d_ref, nm_ref, nv_ref):
        g = p_ref[0]
        for k in range(1, N_DEV):
            g = g + p_ref[k]
        nm = ADAM_B1 * m_ref[...] + (1.0 - ADAM_B1) * g
        nv = ADAM_B2 * v_ref[...] + (1.0 - ADAM_B2) * (g * g)
        g_ref[...] = g
        nm_ref[...] = nm
        nv_ref[...] = nv
        d_ref[...] = -ADAM_LR * ((nm / c1) / (jnp.sqrt(nv / c2) + ADAM_EPS) + ADAM_WD * w_ref[...])

    blk = pl.BlockSpec((tr, cols), lambda i: (i, 0))
    shp = jax.ShapeDtypeStruct((rows, cols), F32)
    return pl.pallas_call(
        body, name=name, grid=(rows // tr,),
        in_specs=[pl.BlockSpec((N_DEV, tr, cols), lambda i: (0, i, 0)), blk, blk, blk],
        out_specs=[blk, blk, blk, blk], out_shape=[shp, shp, shp, shp],
        compiler_params=_cp("parallel"))(parts, w, m, v)


def _pack(arrays, lead=0):
    parts = []
    for a in arrays:
        f = a.reshape(a.shape[:lead] + (-1, LANES))
        pad = -f.shape[lead] % PACK_ROWS
        if pad:
            f = jnp.pad(f, [(0, 0)] * lead + [(0, pad), (0, 0)])
        parts.append(f)
    return jnp.concatenate(parts, axis=lead)


def _unpack(buf, shapes, lead=0):
    out, r = [], 0
    for shp in shapes:
        n = int(np.prod(shp)) // LANES
        part = lax.slice_in_dim(buf, r, r + n, axis=lead)
        out.append(part.reshape(buf.shape[:lead] + tuple(shp)))
        r += n + (-n % PACK_ROWS)
    return out


REPLICATED = ("lb_logits", "norm_mix", "conv_b", "b_r", "b_i", "lam", "hg_norm", "norm_mlp", "norm_final")
SMALL_SHARDED = ("conv_w", "w_r", "w_i")
LARGE_SHARDED = ("w_in", "w_out", "w_up", "w_down")
WEIGHTS = ("lb_logits", "norm_mix", "w_in", "conv_w", "conv_b", "w_r", "b_r", "w_i", "b_i", "lam", "hg_norm",
           "w_out", "norm_mlp", "w_up", "w_down", "norm_final")


def _gather_weights(p):
    depth, d, _ = p["w_in"].shape
    d8, f8 = d // N_DEV, p["w_up"].shape[2]
    n_blk = d // RG_BLOCK_W
    rb = RG_BLOCK_W // N_DEV
    per_layer = [p["w_in"], p["w_out"], p["w_up"].reshape(depth, f8, d), p["w_down"]]
    big = jnp.concatenate([a.astype(BF16) for a in per_layer], axis=1)
    rows_l = big.shape[1]
    big = _exchange(big.reshape(depth * rows_l, d), "gather_matmul_weights", gather=True)
    big = big.reshape(N_DEV, depth, rows_l, d)
    small = _exchange(_pack([p["conv_w"], p["w_r"], p["w_i"]]), "gather_mixer_weights", gather=True)
    conv_w, w_r, w_i = _unpack(small, [p["conv_w"].shape, p["w_r"].shape, p["w_i"].shape], lead=1)
    conv_w = conv_w.transpose(1, 2, 0, 3).reshape(depth, CONV_W, d)
    w_r = w_r.transpose(1, 2, 0, 3, 4).reshape(depth, n_blk, RG_BLOCK_W, RG_BLOCK_W).astype(BF16)
    w_i = w_i.transpose(1, 2, 0, 3, 4).reshape(depth, n_blk, RG_BLOCK_W, RG_BLOCK_W).astype(BF16)
    layers = []
    for l in range(depth):
        g = big[:, l]
        layers.append(dict(
            w_in=g[:, :d],
            w_out=g[:, d:d + d8].reshape(d, d),
            w_up=g[:, d + d8:d + d8 + f8].reshape(N_DEV, d, f8),
            w_down=g[:, d + d8 + f8:],
            conv_w=conv_w[l], w_r=w_r[l], w_i=w_i[l]))
    return layers


def _local_step(x, target, p, layers):
    bl, seq, d = x.shape
    depth = len(layers)
    t_rows = bl * seq
    row = lambda a, l: a[l:l + 1]
    lbs = _lower_bounds_fwd(p["lb_logits"])
    cur = x.reshape(t_rows, d)
    saved = []
    for l in range(depth):
        w = layers[l]
        proj, h = _inproj_fwd(cur, row(p["norm_mix"], l), w["w_in"])
        hs, y_a = _mixer_a_fwd(proj, w["conv_w"], row(p["conv_b"], l), w["w_r"], row(p["b_r"], l), w["w_i"],
                               row(p["b_i"], l), row(p["lam"], l), seq)
        o, o_n = _hgrn_fwd(proj, row(lbs, l), row(p["hg_norm"], l), seq)
        x_mid, y = _merge_out_fwd(proj, y_a, o_n, cur, w["w_out"])
        x_out, u, h2 = _mlp_fwd(x_mid, row(p["norm_mlp"], l), w["w_up"], w["w_down"])
        saved.append(dict(x_in=cur, proj=proj, h=h, hs=hs, y_a=y_a, o=o, o_n=o_n, x_mid=x_mid, y=y, u=u, h2=h2))
        cur = x_out
    loss8, dx, dxb, g_norm_final = _loss_head(cur, p["norm_final"].reshape(1, d), target.reshape(t_rows, d))
    g = {k: [None] * depth for k in WEIGHTS if k not in ("lb_logits", "norm_final")}
    d_lbs = [None] * depth
    for l in reversed(range(depth)):
        s, w = saved[l], layers[l]
        dx_mid, dx_mid_b, du, act, g["norm_mlp"][l] = _mlp_bwd(dx, s["u"], s["x_mid"], row(p["norm_mlp"], l),
                                                               w["w_up"], w["w_down"])
        g["w_down"][l] = _wgrad(act, dxb[None], "wgrad_down")
        g["w_up"][l] = _wgrad(s["h2"][None], du, "wgrad_up")
        d_ya, d_on, dp_c = _outproj_bwd(dx_mid_b, w["w_out"], s["proj"], s["y_a"], s["o_n"])
        g["w_out"][l] = _wgrad(s["y"][None], dx_mid_b[None], "wgrad_out")[0]
        dp_b, d_lbs[l], g["hg_norm"][l] = _hgrn_bwd(s["proj"], row(lbs, l), row(p["hg_norm"], l), s["o"], d_on, seq)
        (dp_a, g["w_r"][l], g["w_i"][l], g["b_r"][l], g["b_i"][l], g["lam"][l], g["conv_w"][l],
         g["conv_b"][l]) = _mixer_a_bwd(s["proj"], s["hs"], d_ya, w["conv_w"], row(p["conv_b"], l), w["w_r"],
                                        row(p["b_r"], l), w["w_i"], row(p["b_i"], l), row(p["lam"], l), seq)
        dx, dxb, g["norm_mix"][l] = _inproj_bwd(dx_mid, dp_a, dp_b, dp_c, w["w_in"], s["x_in"],
                                                row(p["norm_mix"], l))
        hb = s["h"][None]
        g["w_in"][l] = jnp.concatenate([_wgrad(hb, dp_a, "wgrad_in_pair"), _wgrad(hb, dp_b, "wgrad_in_triple"),
                                        _wgrad(hb, dp_c, "wgrad_in_triple")], axis=0)
    grads = {k: jnp.stack(v) for k, v in g.items()}
    for k in ("norm_mix", "conv_b", "b_r", "b_i", "lam", "hg_norm", "norm_mlp"):
        grads[k] = grads[k][:, 0]
    grads["lb_logits"] = _lower_bounds_bwd(p["lb_logits"], jnp.concatenate(d_lbs, axis=0))
    grads["norm_final"] = g_norm_final[0]
    return loss8[0, 0], dx.reshape(bl, seq, d), grads


def _update(p, mom1, mom2, grads):
    depth, d, _ = p["w_in"].shape
    d8, f8 = d // N_DEV, p["w_up"].shape[2]
    n_blk, rb = d // RG_BLOCK_W, RG_BLOCK_W // N_DEV
    out = {}

    to_dev = dict(
        w_in=grads["w_in"].transpose(1, 0, 2, 3).reshape(N_DEV, depth * d, d),
        w_out=grads["w_out"].reshape(depth, N_DEV, d8, d).transpose(1, 0, 2, 3).reshape(N_DEV, depth * d8, d),
        w_up=grads["w_up"].transpose(1, 0, 2, 3).reshape(N_DEV, depth * d, f8),
        w_down=grads["w_down"].transpose(1, 0, 2, 3).reshape(N_DEV, depth * f8, d))
    for k in LARGE_SHARDED:
        shp = p[k].shape
        flat = lambda a: a.reshape(shp[0] * shp[1], shp[2])
        parts = _exchange(to_dev[k], "scatter_grad_" + k, gather=False)
        res = _reduce_adamw(parts, flat(p[k]), flat(mom1[k]), flat(mom2[k]), "adamw_" + k)
        out[k] = [r.reshape(shp) for r in res]

    small_to_dev = [
        grads["conv_w"].reshape(depth, CONV_W, N_DEV, d8).transpose(2, 0, 1, 3),
        grads["w_r"].reshape(depth, n_blk, N_DEV, rb, RG_BLOCK_W).transpose(2, 0, 1, 3, 4),
        grads["w_i"].reshape(depth, n_blk, N_DEV, rb, RG_BLOCK_W).transpose(2, 0, 1, 3, 4)]
    parts = _exchange(_pack(small_to_dev, lead=1), "scatter_grad_mixer", gather=False)
    res = _reduce_adamw(parts, *[_pack([src[k] for k in SMALL_SHARDED]) for src in (p, mom1, mom2)],
                        "adamw_mixer")
    shapes = [p[k].shape for k in SMALL_SHARDED]
    for i, vals in enumerate(zip(*[_unpack(r, shapes) for r in res])):
        out[SMALL_SHARDED[i]] = list(vals)

    parts = _exchange(_pack([grads[k] for k in REPLICATED]), "gather_grad_replicated", gather=True)
    res = _reduce_adamw(parts, *[_pack([src[k] for k in REPLICATED]) for src in (p, mom1, mom2)],
                        "adamw_replicated")
    shapes = [p[k].shape for k in REPLICATED]
    for i, vals in enumerate(zip(*[_unpack(r, shapes) for r in res])):
        out[REPLICATED[i]] = list(vals)

    return tuple(out[k][i] for i in range(4) for k in WEIGHTS)


def kernel(x, lb_logits, norm_mix, w_in, conv_w, conv_b, w_r, b_r, w_i, b_i, lam, hg_norm, w_out, norm_mlp, w_up, w_down, norm_final, loss_target, m_lb_logits, m_norm_mix, m_w_in, m_conv_w, m_conv_b, m_w_r, m_b_r, m_w_i, m_b_i, m_lam, m_hg_norm, m_w_out, m_norm_mlp, m_w_up, m_w_down, m_norm_final, v_lb_logits, v_norm_mix, v_w_in, v_conv_w, v_conv_b, v_w_r, v_b_r, v_w_i, v_b_i, v_lam, v_hg_norm, v_w_out, v_norm_mlp, v_w_up, v_w_down, v_norm_final):
    p = dict(lb_logits=lb_logits, norm_mix=norm_mix, w_in=w_in, conv_w=conv_w, conv_b=conv_b, w_r=w_r, b_r=b_r,
             w_i=w_i, b_i=b_i, lam=lam, hg_norm=hg_norm, w_out=w_out, norm_mlp=norm_mlp, w_up=w_up,
             w_down=w_down, norm_final=norm_final)
    mom1 = dict(lb_logits=m_lb_logits, norm_mix=m_norm_mix, w_in=m_w_in, conv_w=m_conv_w, conv_b=m_conv_b,
                w_r=m_w_r, b_r=m_b_r, w_i=m_w_i, b_i=m_b_i, lam=m_lam, hg_norm=m_hg_norm, w_out=m_w_out,
                norm_mlp=m_norm_mlp, w_up=m_w_up, w_down=m_w_down, norm_final=m_norm_final)
    mom2 = dict(lb_logits=v_lb_logits, norm_mix=v_norm_mix, w_in=v_w_in, conv_w=v_conv_w, conv_b=v_conv_b,
                w_r=v_w_r, b_r=v_b_r, w_i=v_w_i, b_i=v_b_i, lam=v_lam, hg_norm=v_hg_norm, w_out=v_w_out,
                norm_mlp=v_norm_mlp, w_up=v_w_up, w_down=v_w_down, norm_final=v_norm_final)
    layers = _gather_weights(p)
    loss, grad_x, grads = _local_step(x, loss_target, p, layers)
    loss = lax.psum(loss, ("x", "y", "c"))
    return (loss, grad_x) + _update(p, mom1, mom2, grads)
```

```python
import numpy as np

import jax
import jax.numpy as jnp
from jax import lax
from jax.experimental import pallas as pl
from jax.experimental.pallas import tpu as pltpu

F32 = jnp.float32
BF16 = jnp.bfloat16
MESH_ID = pl.DeviceIdType.MESH

N_DEV = 8
NORM_EPS = 1e-6
RG_C = 8.0
RG_BLOCK_W = 256
CONV_W = 4
HG_DK = 128
F_MIN = 1e-30
HG_CHUNK = 16
SUBLANES = 8
LANES = 128
PACK_ROWS = 16
ROW_CHUNK = 256
VMEM_LIMIT_V7X = 56 * 1024 * 1024

ADAM_LR = 0.001
ADAM_B1 = 0.9
ADAM_B2 = 0.999
ADAM_EPS = 1e-08
ADAM_WD = 0.01
ADAM_STEP = 10

GELU_C = 0.7978845608028654
GELU_K = 0.044715


def _cp(*sem):
    return pltpu.CompilerParams(dimension_semantics=sem, vmem_limit_bytes=VMEM_LIMIT_V7X)


def _row_tile(n, cap):
    if n <= cap:
        return n
    t = cap - cap % 16
    while n % t:
        t -= 16
    return t


def _dot(a, b):
    return jnp.dot(a, b, preferred_element_type=F32)


def _dot_nt(a, b):
    return lax.dot_general(a, b, (((1,), (1,)), ((), ())), preferred_element_type=F32)


def _dot_tn(a, b):
    return lax.dot_general(a, b, (((0,), (0,)), ((), ())), preferred_element_type=F32)


def _sigmoid(x):
    return jax.nn.sigmoid(x)


def _log1p_pos(y):
    return jnp.where(y < 0.01, y * (1.0 - y * (0.5 - y * (1.0 / 3.0))), jnp.log(1.0 + y))


def _softplus(x):
    return jnp.maximum(x, 0.0) + _log1p_pos(jnp.exp(-jnp.abs(x)))


def _one_minus_exp(x):
    series = -x * (1.0 + x * 0.5 * (1.0 + x * (1.0 / 3.0) * (1.0 + x * 0.25 * (1.0 + x * 0.2))))
    return jnp.where(x > -0.1, series, 1.0 - jnp.exp(x))


def _gelu_and_grad(x):
    x2 = x * x
    t = jnp.tanh(GELU_C * x * (1.0 + GELU_K * x2))
    g = 0.5 * x * (1.0 + t)
    dg = 0.5 * (1.0 + t) + 0.5 * x * (1.0 - t * t) * GELU_C * (1.0 + 3.0 * GELU_K * x2)
    return g, dg


def _silu_and_grad(x):
    s = _sigmoid(x)
    return x * s, s * (1.0 + x * (1.0 - s))


def _rstd(x):
    return lax.rsqrt(jnp.mean(x * x, axis=-1, keepdims=True) + NORM_EPS)


def _rms_bwd(dh, x, g):
    rstd = _rstd(x)
    xh = x * rstd
    dxh = dh * g
    dx = rstd * (dxh - xh * jnp.mean(dxh * xh, axis=-1, keepdims=True))
    return dx, jnp.sum(dh * xh, axis=0, keepdims=True)


def _shift_rows(x, k):
    n = x.shape[0]
    k = k % n
    return x if k == 0 else pltpu.roll(x, k, axis=0)


def _seg_cumsum(x, seg, reverse=False):
    n = x.shape[0]
    rid = lax.broadcasted_iota(jnp.int32, x.shape, 0) & (seg - 1)
    d = 1
    while d < seg:
        if reverse:
            x = jnp.where(rid < seg - d, x + _shift_rows(x, n - d), x)
        else:
            x = jnp.where(rid >= d, x + _shift_rows(x, d), x)
        d *= 2
    return x


def _scan_rows(a_ref, b_ref, out_ref, n_rows, width, reverse):
    rid = lax.broadcasted_iota(jnp.int32, (SUBLANES, width), 0)
    n_groups = n_rows // SUBLANES

    def group(i, carry):
        g = n_groups - 1 - i if reverse else i
        r0 = pl.multiple_of(g * SUBLANES, SUBLANES)
        a = a_ref[pl.ds(r0, SUBLANES), :]
        b = b_ref[pl.ds(r0, SUBLANES), :]
        for d in (1, 2, 4):
            if reverse:
                keep = rid < SUBLANES - d
                a_sh, b_sh = _shift_rows(a, SUBLANES - d), _shift_rows(b, SUBLANES - d)
            else:
                keep = rid >= d
                a_sh, b_sh = _shift_rows(a, d), _shift_rows(b, d)
            b = jnp.where(keep, a * b_sh + b, b)
            a = jnp.where(keep, a * a_sh, a)
        out = a * carry + b
        out_ref[pl.ds(r0, SUBLANES), :] = out
        edge = out[0:1, :] if reverse else out[SUBLANES - 1:SUBLANES, :]
        return jnp.broadcast_to(edge, (SUBLANES, width))

    lax.fori_loop(0, n_groups, group, jnp.zeros((SUBLANES, width), F32), unroll=4)


def _lb_softmax_rows(x_ref, depth):
    rows = [x_ref[pl.ds(l, 1), :] for l in range(depth)]
    top = rows[0]
    for r in rows[1:]:
        top = jnp.maximum(top, r)
    e = [jnp.exp(r - top) for r in rows]
    tot = e[0]
    for r in e[1:]:
        tot = tot + r
    return [r / tot for r in e]


def _lower_bounds_fwd(lb_logits):
    depth, d = lb_logits.shape

    def body(x_ref, o_ref):
        sm = _lb_softmax_rows(x_ref, depth)
        cum = jnp.zeros((1, d), F32)
        for l in range(depth):
            cum = cum + sm[l]
            o_ref[pl.ds(l, 1), :] = jnp.clip(cum - sm[0], 0.0, 1.0)

    return pl.pallas_call(body, name="lower_bounds_fwd",
                          out_shape=jax.ShapeDtypeStruct((depth, d), F32))(lb_logits)


def _lower_bounds_bwd(lb_logits, d_lbs):
    depth, d = lb_logits.shape

    def body(x_ref, g_ref, o_ref):
        sm = _lb_softmax_rows(x_ref, depth)
        cum = jnp.zeros((1, d), F32)
        d_cum = []
        for l in range(depth):
            cum = cum + sm[l]
            v = cum - sm[0]
            d_cum.append(jnp.where((v > 0.0) & (v < 1.0), g_ref[pl.ds(l, 1), :], 0.0))
        d_sm = []
        tail = jnp.zeros((1, d), F32)
        for l in reversed(range(depth)):
            tail = tail + d_cum[l]
            d_sm.append(tail)
        d_sm = d_sm[::-1]
        d_sm[0] = d_sm[0] - tail
        inner = jnp.zeros((1, d), F32)
        for l in range(depth):
            inner = inner + sm[l] * d_sm[l]
        for l in range(depth):
            o_ref[pl.ds(l, 1), :] = sm[l] * (d_sm[l] - inner)

    return pl.pallas_call(body, name="lower_bounds_bwd",
                          out_shape=jax.ShapeDtypeStruct((depth, d), F32))(lb_logits, d_lbs)


def _inproj_fwd(x, gain, w_seg):
    t_rows, d = x.shape
    tm = _row_tile(t_rows, 512)

    def body(x_ref, g_ref, w_ref, proj_ref, h_ref):
        @pl.when(pl.program_id(1) == 0)
        def _():
            xv = x_ref[...]
            h_ref[...] = (xv * _rstd(xv) * g_ref[...]).astype(BF16)

        proj_ref[...] = _dot(h_ref[...], w_ref[...])

    return pl.pallas_call(
        body, name="inproj_fwd", grid=(t_rows // tm, N_DEV),
        in_specs=[pl.BlockSpec((tm, d), lambda i, j: (i, 0)),
                  pl.BlockSpec((1, d), lambda i, j: (0, 0)),
                  pl.BlockSpec((None, d, d), lambda i, j: (j, 0, 0))],
        out_specs=[pl.BlockSpec((None, tm, d), lambda i, j: (j, i, 0)),
                   pl.BlockSpec((tm, d), lambda i, j: (i, 0))],
        out_shape=[jax.ShapeDtypeStruct((N_DEV, t_rows, d), F32),
                   jax.ShapeDtypeStruct((t_rows, d), BF16)],
        compiler_params=_cp("parallel", "arbitrary"))(x, gain, w_seg)


def _merge_out_fwd(proj, y_a, o_n, x, w_out):
    t_rows, d = x.shape
    tm = _row_tile(t_rows, 256)

    def body(g_ref, ma_ref, mb_ref, ya_ref, on_ref, x_ref, w_ref, xmid_ref, y_ref):
        g = g_ref[...]
        y = _sigmoid(ma_ref[...]) * ya_ref[...] + _sigmoid(mb_ref[...]) * (on_ref[...] * (g * _sigmoid(g)))
        yb = y.astype(BF16)
        y_ref[...] = yb
        xmid_ref[...] = x_ref[...] + _dot(yb, w_ref[...])

    seg = lambda k: pl.BlockSpec((None, tm, d), lambda i, k=k: (k, i, 0))
    row = pl.BlockSpec((tm, d), lambda i: (i, 0))
    return pl.pallas_call(
        body, name="merge_out_fwd", grid=(t_rows // tm,),
        in_specs=[seg(5), seg(6), seg(7), row, row, row, pl.BlockSpec((d, d), lambda i: (0, 0))],
        out_specs=[row, row],
        out_shape=[jax.ShapeDtypeStruct((t_rows, d), F32), jax.ShapeDtypeStruct((t_rows, d), BF16)],
        compiler_params=_cp("parallel"))(proj, proj, proj, y_a, o_n, x, w_out)


def _mlp_fwd(x_mid, gain, w_up, w_down):
    t_rows, d = x_mid.shape
    f8 = w_up.shape[2]
    tm = _row_tile(t_rows, 512)

    def body(x_ref, g_ref, wu_ref, wd_ref, out_ref, u_ref, h_ref):
        @pl.when(pl.program_id(1) == 0)
        def _():
            xv = x_ref[...]
            h_ref[...] = (xv * _rstd(xv) * g_ref[...]).astype(BF16)
            out_ref[...] = xv

        u = _dot(h_ref[...], wu_ref[...])
        u_ref[...] = u
        r = jnp.maximum(u, 0.0)
        out_ref[...] += _dot((r * r).astype(BF16), wd_ref[...])

    row = pl.BlockSpec((tm, d), lambda i, j: (i, 0))
    return pl.pallas_call(
        body, name="mlp_fwd", grid=(t_rows // tm, N_DEV),
        in_specs=[row, pl.BlockSpec((1, d), lambda i, j: (0, 0)),
                  pl.BlockSpec((None, d, f8), lambda i, j: (j, 0, 0)),
                  pl.BlockSpec((None, f8, d), lambda i, j: (j, 0, 0))],
        out_specs=[row, pl.BlockSpec((None, tm, f8), lambda i, j: (j, i, 0)), row],
        out_shape=[jax.ShapeDtypeStruct((t_rows, d), F32),
                   jax.ShapeDtypeStruct((N_DEV, t_rows, f8), F32),
                   jax.ShapeDtypeStruct((t_rows, d), BF16)],
        compiler_params=_cp("parallel", "arbitrary"))(x_mid, gain, w_up, w_down)


def _loss_head(x, gain, target):
    t_rows, d = x.shape
    tm = _row_tile(t_rows, 512)

    def body(x_ref, g_ref, t_ref, loss_ref, dx_ref, dxb_ref, dg_ref):
        @pl.when(pl.program_id(0) == 0)
        def _():
            loss_ref[...] = jnp.zeros_like(loss_ref)
            dg_ref[...] = jnp.zeros_like(dg_ref)

        xv = x_ref[...]
        g = g_ref[...]
        err = xv * _rstd(xv) * g - t_ref[...]
        loss_ref[...] += (0.5 / d) * jnp.sum(err * err)
        dx, dg = _rms_bwd(err * (1.0 / d), xv, g)
        dx_ref[...] = dx
        dxb_ref[...] = dx.astype(BF16)
        dg_ref[...] += dg

    row = pl.BlockSpec((tm, d), lambda i: (i, 0))
    vec = pl.BlockSpec((1, d), lambda i: (0, 0))
    return pl.pallas_call(
        body, name="loss_head", grid=(t_rows // tm,),
        in_specs=[row, vec, row],
        out_specs=[pl.BlockSpec((SUBLANES, LANES), lambda i: (0, 0)), row, row, vec],
        out_shape=[jax.ShapeDtypeStruct((SUBLANES, LANES), F32),
                   jax.ShapeDtypeStruct((t_rows, d), F32),
                   jax.ShapeDtypeStruct((t_rows, d), BF16),
                   jax.ShapeDtypeStruct((1, d), F32)],
        compiler_params=_cp("arbitrary"))(x, gain, target)


def _mlp_bwd(d_out, u, x_mid, gain, w_up, w_down):
    t_rows, d = x_mid.shape
    f8 = w_up.shape[2]
    tm = _row_tile(t_rows, 512)

    def body(do_ref, u_ref, x_ref, g_ref, wu_ref, wd_ref, dx_ref, dxb_ref, du_ref, act_ref, dg_ref, acc_ref):
        j = pl.program_id(1)

        @pl.when((pl.program_id(0) == 0) & (j == 0))
        def _():
            dg_ref[...] = jnp.zeros_like(dg_ref)

        @pl.when(j == 0)
        def _():
            acc_ref[...] = jnp.zeros_like(acc_ref)

        r = jnp.maximum(u_ref[...], 0.0)
        act_ref[...] = (r * r).astype(BF16)
        du = (_dot_nt(do_ref[...].astype(BF16), wd_ref[...]) * (2.0 * r)).astype(BF16)
        du_ref[...] = du
        acc_ref[...] += _dot_nt(du, wu_ref[...])

        @pl.when(j == N_DEV - 1)
        def _():
            dx, dg = _rms_bwd(acc_ref[...], x_ref[...], g_ref[...])
            dx = dx + do_ref[...]
            dx_ref[...] = dx
            dxb_ref[...] = dx.astype(BF16)
            dg_ref[...] += dg

    row = pl.BlockSpec((tm, d), lambda i, j: (i, 0))
    vec = pl.BlockSpec((1, d), lambda i, j: (0, 0))
    hid = pl.BlockSpec((None, tm, f8), lambda i, j: (j, i, 0))
    return pl.pallas_call(
        body, name="mlp_bwd", grid=(t_rows // tm, N_DEV),
        in_specs=[row, hid, row, vec,
                  pl.BlockSpec((None, d, f8), lambda i, j: (j, 0, 0)),
                  pl.BlockSpec((None, f8, d), lambda i, j: (j, 0, 0))],
        out_specs=[row, row, hid, hid, vec],
        out_shape=[jax.ShapeDtypeStruct((t_rows, d), F32),
                   jax.ShapeDtypeStruct((t_rows, d), BF16),
                   jax.ShapeDtypeStruct((N_DEV, t_rows, f8), BF16),
                   jax.ShapeDtypeStruct((N_DEV, t_rows, f8), BF16),
                   jax.ShapeDtypeStruct((1, d), F32)],
        scratch_shapes=[pltpu.VMEM((tm, d), F32)],
        compiler_params=_cp("arbitrary", "arbitrary"))(d_out, u, x_mid, gain, w_up, w_down)


def _outproj_bwd(dx_mid_b, w_out, proj, y_a, o_n):
    t_rows, d = y_a.shape
    tm = _row_tile(t_rows, 256)

    def body(dx_ref, w_ref, g_ref, ma_ref, mb_ref, ya_ref, on_ref, dya_ref, don_ref, dp_ref):
        dy = _dot_nt(dx_ref[...], w_ref[...])
        sa = _sigmoid(ma_ref[...])
        sb = _sigmoid(mb_ref[...])
        sg, dsg = _silu_and_grad(g_ref[...])
        ya = ya_ref[...]
        on = on_ref[...]
        dya_ref[...] = dy * sa
        t = dy * sb
        don_ref[...] = t * sg
        dp_ref[0] = (t * on * dsg).astype(BF16)
        dp_ref[1] = (dy * ya * sa * (1.0 - sa)).astype(BF16)
        dp_ref[2] = (dy * on * sg * sb * (1.0 - sb)).astype(BF16)

    seg = lambda k: pl.BlockSpec((None, tm, d), lambda i, k=k: (k, i, 0))
    row = pl.BlockSpec((tm, d), lambda i: (i, 0))
    return pl.pallas_call(
        body, name="outproj_bwd", grid=(t_rows // tm,),
        in_specs=[row, pl.BlockSpec((d, d), lambda i: (0, 0)), seg(5), seg(6), seg(7), row, row],
        out_specs=[row, row, pl.BlockSpec((3, tm, d), lambda i: (0, i, 0))],
        out_shape=[jax.ShapeDtypeStruct((t_rows, d), F32),
                   jax.ShapeDtypeStruct((t_rows, d), F32),
                   jax.ShapeDtypeStruct((3, t_rows, d), BF16)],
        compiler_params=_cp("parallel"))(dx_mid_b, w_out, proj, proj, proj, y_a, o_n)


def _inproj_bwd(dx_mid, dp_a, dp_b, dp_c, w_seg, x_in, gain):
    t_rows, d = x_in.shape
    tm = _row_tile(t_rows, 512)
    n_a, n_b = dp_a.shape[0], dp_b.shape[0]

    def body(dxm_ref, a_ref, b_ref, c_ref, w_ref, x_ref, g_ref, dx_ref, dxb_ref, dg_ref, acc_ref):
        j = pl.program_id(1)

        @pl.when((pl.program_id(0) == 0) & (j == 0))
        def _():
            dg_ref[...] = jnp.zeros_like(dg_ref)

        @pl.when(j == 0)
        def _():
            acc_ref[...] = jnp.zeros_like(acc_ref)

        @pl.when(j < n_a)
        def _():
            acc_ref[...] += _dot_nt(a_ref[...], w_ref[...])

        @pl.when((j >= n_a) & (j < n_a + n_b))
        def _():
            acc_ref[...] += _dot_nt(b_ref[...], w_ref[...])

        @pl.when(j >= n_a + n_b)
        def _():
            acc_ref[...] += _dot_nt(c_ref[...], w_ref[...])

        @pl.when(j == N_DEV - 1)
        def _():
            dx, dg = _rms_bwd(acc_ref[...], x_ref[...], g_ref[...])
            dx = dx + dxm_ref[...]
            dx_ref[...] = dx
            dxb_ref[...] = dx.astype(BF16)
            dg_ref[...] += dg

    def part(first, n):
        return pl.BlockSpec((None, tm, d), lambda i, j: (jnp.clip(j - first, 0, n - 1), i, 0))

    row = pl.BlockSpec((tm, d), lambda i, j: (i, 0))
    vec = pl.BlockSpec((1, d), lambda i, j: (0, 0))
    return pl.pallas_call(
        body, name="inproj_bwd", grid=(t_rows // tm, N_DEV),
        in_specs=[row, part(0, n_a), part(n_a, n_b), part(n_a + n_b, dp_c.shape[0]),
                  pl.BlockSpec((None, d, d), lambda i, j: (j, 0, 0)), row, vec],
        out_specs=[row, row, vec],
        out_shape=[jax.ShapeDtypeStruct((t_rows, d), F32),
                   jax.ShapeDtypeStruct((t_rows, d), BF16),
                   jax.ShapeDtypeStruct((1, d), F32)],
        scratch_shapes=[pltpu.VMEM((tm, d), F32)],
        compiler_params=_cp("arbitrary", "arbitrary"))(dx_mid, dp_a, dp_b, dp_c, w_seg, x_in, gain)


def _wgrad(a3, b3, name):
    n_a, t_rows, k_a = a3.shape
    n_b, _, n_cols = b3.shape
    n = max(n_a, n_b)
    bk = _row_tile(k_a, 1024)
    bn = n_cols if n_cols <= 1024 else 1024
    tt = _row_tile(t_rows, 512)
    n_t = t_rows // tt

    def body(a_ref, b_ref, o_ref, acc_ref):
        @pl.when(pl.program_id(3) == 0)
        def _():
            acc_ref[...] = jnp.zeros_like(acc_ref)

        acc_ref[...] += _dot_tn(a_ref[...], b_ref[...])

        @pl.when(pl.program_id(3) == n_t - 1)
        def _():
            o_ref[...] = acc_ref[...].astype(BF16)

    return pl.pallas_call(
        body, name=name, grid=(n, k_a // bk, n_cols // bn, n_t),
        in_specs=[pl.BlockSpec((None, tt, bk), lambda j, p, q, t: (j if n_a > 1 else 0, t, p)),
                  pl.BlockSpec((None, tt, bn), lambda j, p, q, t: (j if n_b > 1 else 0, t, q))],
        out_specs=pl.BlockSpec((None, bk, bn), lambda j, p, q, t: (j, p, q)),
        out_shape=jax.ShapeDtypeStruct((n, k_a, n_cols), BF16),
        scratch_shapes=[pltpu.VMEM((bk, bn), F32)],
        compiler_params=_cp("parallel", "parallel", "parallel", "arbitrary"))(a3, b3)


def _conv_taps(xe, n):
    return [_shift_rows(xe, CONV_W - 1 - j)[SUBLANES:SUBLANES + n, :] for j in range(CONV_W)]


def _rg_gates(xc, w_r, b_r, w_i, b_i, sp8):
    xb = xc.astype(BF16)
    r = _sigmoid(_dot(xb, w_r) + b_r)
    i = _sigmoid(_dot(xb, w_i) + b_i)
    return r, i


def _mixer_a_fwd(proj, conv_w, conv_b, w_r, b_r, w_i, b_i, lam, seq):
    _, t_rows, d = proj.shape
    n_seq, n_blk = t_rows // seq, d // RG_BLOCK_W
    wb = RG_BLOCK_W
    ch = _row_tile(seq, ROW_CHUNK)

    def body(xa_ref, ga_ref, cw_ref, cb_ref, wr_ref, br_ref, wi_ref, bi_ref, lam_ref, h_ref, ya_ref,
             xpad, a_s, u_s):
        xpad[0:SUBLANES, :] = jnp.zeros((SUBLANES, wb), F32)
        xpad[SUBLANES:, :] = xa_ref[...]
        sp8 = RG_C * _softplus(-lam_ref[...])

        def gates(c, _):
            r0 = pl.multiple_of(c * ch, ch)
            taps = _conv_taps(xpad[pl.ds(r0, ch + SUBLANES), :], ch)
            xc = cb_ref[...] + sum(cw_ref[pl.ds(j, 1), :] * taps[j] for j in range(CONV_W))
            r, i = _rg_gates(xc, wr_ref[...], br_ref[...], wi_ref[...], bi_ref[...], sp8)
            log_a = -(r * sp8)
            a_s[pl.ds(r0, ch), :] = jnp.exp(log_a)
            u_s[pl.ds(r0, ch), :] = jnp.sqrt(jnp.maximum(_one_minus_exp(2.0 * log_a), 0.0)) * (i * xc)
            return 0

        lax.fori_loop(0, seq // ch, gates, 0)
        _scan_rows(a_s, u_s, h_ref, seq, wb, reverse=False)

        def gate_out(c, _):
            r0 = pl.multiple_of(c * ch, ch)
            gl, _ = _gelu_and_grad(ga_ref[pl.ds(r0, ch), :])
            ya_ref[pl.ds(r0, ch), :] = h_ref[pl.ds(r0, ch), :] * gl
            return 0

        lax.fori_loop(0, seq // ch, gate_out, 0)

    seg = lambda k: pl.BlockSpec((None, seq, wb), lambda s, b, k=k: (k, s, b))
    blk = pl.BlockSpec((seq, wb), lambda s, b: (s, b))
    vec = pl.BlockSpec((1, wb), lambda s, b: (0, b))
    wsp = pl.BlockSpec((None, wb, wb), lambda s, b: (b, 0, 0))
    return pl.pallas_call(
        body, name="mixer_a_fwd", grid=(n_seq, n_blk),
        in_specs=[seg(0), seg(1), pl.BlockSpec((CONV_W, wb), lambda s, b: (0, b)), vec, wsp, vec, wsp, vec, vec],
        out_specs=[blk, blk],
        out_shape=[jax.ShapeDtypeStruct((t_rows, d), F32), jax.ShapeDtypeStruct((t_rows, d), F32)],
        scratch_shapes=[pltpu.VMEM((seq + SUBLANES, wb), F32), pltpu.VMEM((seq, wb), F32),
                        pltpu.VMEM((seq, wb), F32)],
        compiler_params=_cp("parallel", "parallel"))(proj, proj, conv_w, conv_b, w_r, b_r, w_i, b_i, lam)


def _mixer_a_bwd(proj, h, d_ya, conv_w, conv_b, w_r, b_r, w_i, b_i, lam, seq):
    _, t_rows, d = proj.shape
    n_seq, n_blk = t_rows // seq, d // RG_BLOCK_W
    wb = RG_BLOCK_W
    ch = _row_tile(seq, ROW_CHUNK)
    n_ch = seq // ch

    def body(xa_ref, ga_ref, h_ref, dya_ref, cw_ref, cb_ref, wr_ref, br_ref, wi_ref, bi_ref, lam_ref,
             dp_ref, dwr_ref, dwi_ref, dbr_ref, dbi_ref, dlam_ref, dcw_ref, dcb_ref,
             xpad, hpad, a_s, e_pad, g_s, xc_s, r_s, i_s, dxc_pad):
        @pl.when(pl.program_id(1) == 0)
        def _():
            for ref in (dwr_ref, dwi_ref, dbr_ref, dbi_ref, dlam_ref, dcw_ref, dcb_ref):
                ref[...] = jnp.zeros_like(ref)

        zeros8 = jnp.zeros((SUBLANES, wb), F32)
        xpad[0:SUBLANES, :] = zeros8
        xpad[SUBLANES:, :] = xa_ref[...]
        hpad[0:SUBLANES, :] = zeros8
        hpad[SUBLANES:, :] = h_ref[...]
        e_pad[seq:, :] = zeros8
        dxc_pad[seq:, :] = zeros8
        lam_v = lam_ref[...]
        sp8 = RG_C * _softplus(-lam_v)

        def recompute(c, _):
            r0 = pl.multiple_of(c * ch, ch)
            rows = pl.ds(r0, ch)
            taps = _conv_taps(xpad[pl.ds(r0, ch + SUBLANES), :], ch)
            xc = cb_ref[...] + sum(cw_ref[pl.ds(j, 1), :] * taps[j] for j in range(CONV_W))
            r, i = _rg_gates(xc, wr_ref[...], br_ref[...], wi_ref[...], bi_ref[...], sp8)
            a = jnp.exp(-(r * sp8))
            gl, dgl = _gelu_and_grad(ga_ref[rows, :])
            dya = dya_ref[rows, :]
            g = dya * gl
            dp_ref[1, rows, :] = (dya * h_ref[rows, :] * dgl).astype(BF16)
            a_s[rows, :] = a
            e_pad[rows, :] = a * g
            g_s[rows, :] = g
            xc_s[rows, :] = xc
            r_s[rows, :] = r
            i_s[rows, :] = i
            return 0

        lax.fori_loop(0, n_ch, recompute, 0)
        _scan_rows(a_s, e_pad, e_pad, seq, wb, reverse=True)

        def grads(c, _):
            r0 = pl.multiple_of(c * ch, ch)
            rows = pl.ds(r0, ch)
            halo = pl.ds(r0, ch + SUBLANES)
            dh = g_s[rows, :] + _shift_rows(e_pad[halo, :], ch + SUBLANES - 1)[0:ch, :]
            h_prev = _shift_rows(hpad[halo, :], 1)[SUBLANES:, :]
            xc, r, i = xc_s[rows, :], r_s[rows, :], i_s[rows, :]
            log_a = -(r * sp8)
            a = jnp.exp(log_a)
            om = _one_minus_exp(2.0 * log_a)
            sq = jnp.sqrt(jnp.maximum(om, 0.0))
            t1 = dh * xc
            d_i = t1 * sq
            d_la = dh * h_prev * a + jnp.where(om > 0.0, -(t1 * i) * (1.0 - om) / sq, 0.0)
            dpr = -(d_la * sp8) * r * (1.0 - r)
            dpi = d_i * i * (1.0 - i)
            dprb, dpib, xb = dpr.astype(BF16), dpi.astype(BF16), xc.astype(BF16)
            dxc = dh * sq * i + _dot_nt(dprb, wr_ref[...]) + _dot_nt(dpib, wi_ref[...])
            dwr_ref[...] += _dot_tn(xb, dprb)
            dwi_ref[...] += _dot_tn(xb, dpib)
            dbr_ref[...] += jnp.sum(dpr, axis=0, keepdims=True)
            dbi_ref[...] += jnp.sum(dpi, axis=0, keepdims=True)
            dlam_ref[...] += jnp.sum(d_la * r, axis=0, keepdims=True) * (RG_C * _sigmoid(-lam_v))
            dcb_ref[...] += jnp.sum(dxc, axis=0, keepdims=True)
            taps = _conv_taps(xpad[halo, :], ch)
            for j in range(CONV_W):
                dcw_ref[pl.ds(j, 1), :] += jnp.sum(dxc * taps[j], axis=0, keepdims=True)
            dxc_pad[rows, :] = dxc
            return 0

        lax.fori_loop(0, n_ch, grads, 0)

        def conv_bwd(c, _):
            r0 = pl.multiple_of(c * ch, ch)
            de = dxc_pad[pl.ds(r0, ch + SUBLANES), :]
            dxa = sum(cw_ref[pl.ds(j, 1), :] * _shift_rows(de, ch + SUBLANES - (CONV_W - 1 - j))[0:ch, :]
                      for j in range(CONV_W))
            dp_ref[0, pl.ds(r0, ch), :] = dxa.astype(BF16)
            return 0

        lax.fori_loop(0, n_ch, conv_bwd, 0)

    seg = lambda k: pl.BlockSpec((None, seq, wb), lambda b, s, k=k: (k, s, b))
    blk = pl.BlockSpec((seq, wb), lambda b, s: (s, b))
    vec = pl.BlockSpec((1, wb), lambda b, s: (0, b))
    taps = pl.BlockSpec((CONV_W, wb), lambda b, s: (0, b))
    wsp = pl.BlockSpec((None, wb, wb), lambda b, s: (b, 0, 0))
    vec_shape = jax.ShapeDtypeStruct((1, d), F32)
    w_shape = jax.ShapeDtypeStruct((n_blk, wb, wb), F32)
    pad = pltpu.VMEM((seq + SUBLANES, wb), F32)
    full = pltpu.VMEM((seq, wb), F32)
    return pl.pallas_call(
        body, name="mixer_a_bwd", grid=(n_blk, n_seq),
        in_specs=[seg(0), seg(1), blk, blk, taps, vec, wsp, vec, wsp, vec, vec],
        out_specs=[pl.BlockSpec((2, seq, wb), lambda b, s: (0, s, b)), wsp, wsp, vec, vec, vec, taps, vec],
        out_shape=[jax.ShapeDtypeStruct((2, t_rows, d), BF16), w_shape, w_shape, vec_shape, vec_shape,
                   vec_shape, jax.ShapeDtypeStruct((CONV_W, d), F32), vec_shape],
        scratch_shapes=[pad, pad, full, pad, full, full, full, full, pad],
        compiler_params=_cp("parallel", "arbitrary"))(
            proj, proj, h, d_ya, conv_w, conv_b, w_r, b_r, w_i, b_i, lam)


def _hg_prepare(q_ref, z_ref, lb, rows):
    z = z_ref[rows, :]
    sig = _sigmoid(z)
    fg = lb + (1.0 - lb) * sig
    log_f = jnp.log(jnp.maximum(fg, F_MIN))
    key = (1.0 - lb) * _sigmoid(-z)
    qs, _ = _silu_and_grad(q_ref[rows, :])
    return qs, key, log_f, sig, fg


def _hg_pair_terms(g_ref, r0, gc, s):
    rid = lax.broadcasted_iota(jnp.int32, (HG_CHUNK, HG_DK), 0)
    gs = g_ref[pl.ds(r0 + s, 1), :]
    return jnp.where(rid >= s, jnp.exp(gc - gs), 0.0)


def _chunk_row_sums():
    row = lax.broadcasted_iota(jnp.int32, (HG_CHUNK, HG_CHUNK * HG_CHUNK), 0)
    col = lax.broadcasted_iota(jnp.int32, (HG_CHUNK, HG_CHUNK * HG_CHUNK), 1)
    lo = row * HG_CHUNK
    return jnp.where((col >= lo) & (col < lo + HG_CHUNK), 1.0, 0.0).astype(BF16)


def _for_chunks(n, unroll, *stages):
    unroll = min(unroll, n)
    assert n % unroll == 0

    def trip(i, _):
        chunks = [i * unroll + u for u in range(unroll)]
        carried = [stages[0](c) for c in chunks]
        for stage in stages[1:]:
            carried = [stage(c, x) for c, x in zip(chunks, carried)]
        return 0

    lax.fori_loop(0, n // unroll, trip, 0)


def _hg_state_terms(v_ref, k_ref, g_ref, states, n_chunks):
    def one(c):
        rows = pl.ds(pl.multiple_of(c * HG_CHUNK, HG_CHUNK), HG_CHUNK)
        gc = g_ref[rows, :]
        k_end = k_ref[rows, :] * jnp.exp(gc[HG_CHUNK - 1:HG_CHUNK, :] - gc)
        states[c] = _dot_tn(v_ref[rows, :].astype(BF16), k_end.astype(BF16))

    _for_chunks(n_chunks, 8, one)


def _hg_state_chain(states, g_ref, carry_ref, n_chunks, reverse):
    unroll = min(8, n_chunks)
    assert n_chunks % unroll == 0
    carry_ref[...] = jnp.zeros_like(carry_ref)

    def trip(i, _):
        st = carry_ref[...]
        for u in range(unroll):
            k = i * unroll + u
            c = n_chunks - 1 - k if reverse else k
            term = states[c]
            states[c] = st
            st = st * jnp.exp(g_ref[pl.ds(c * HG_CHUNK + HG_CHUNK - 1, 1), :]) + term
        carry_ref[...] = st
        return 0

    lax.fori_loop(0, n_chunks // unroll, trip, 0)


def _hgrn_fwd(proj, lower_bound, hg_gain, seq, carry=None):
    _, t_rows, d = proj.shape
    n_seq, n_head = t_rows // seq, d // HG_DK
    ch = _row_tile(seq, ROW_CHUNK)
    n_chunks = seq // HG_CHUNK

    def body(q_ref, z_ref, v_ref, lb_ref, gain_ref, o_ref, on_ref, qs_s, k_s, g_s, states, st_ref):
        lb = lb_ref[...]

        def prepare(c, _):
            rows = pl.ds(pl.multiple_of(c * ch, ch), ch)
            qs, key, log_f, _, _ = _hg_prepare(q_ref, z_ref, lb, rows)
            qs_s[rows, :] = qs
            k_s[rows, :] = key
            g_s[rows, :] = _seg_cumsum(log_f, HG_CHUNK)
            return 0

        lax.fori_loop(0, seq // ch, prepare, 0)
        _hg_state_terms(v_ref, k_s, g_s, states, n_chunks)
        _hg_state_chain(states, g_s, st_ref, n_chunks, reverse=False)
        ones = jnp.ones((HG_DK, HG_DK), BF16)

        def issue(c):
            r0 = pl.multiple_of(c * HG_CHUNK, HG_CHUNK)
            rows = pl.ds(r0, HG_CHUNK)
            qc, gc = qs_s[rows, :], g_s[rows, :]
            o = _dot_nt((qc * jnp.exp(gc)).astype(BF16), states[c].astype(BF16))
            pairs = [(qc * _hg_pair_terms(g_s, r0, gc, s) * k_s[pl.ds(r0 + s, 1), :]).astype(BF16)
                     for s in range(HG_CHUNK)]
            score = _dot(jnp.concatenate(pairs, axis=0), ones)
            return o, score

        def combine(c, issued):
            o, score = issued
            r0 = pl.multiple_of(c * HG_CHUNK, HG_CHUNK)
            for s in range(HG_CHUNK):
                o = o + score[s * HG_CHUNK:(s + 1) * HG_CHUNK, :] * v_ref[pl.ds(r0 + s, 1), :]
            o_ref[pl.ds(r0, HG_CHUNK), :] = o

        _for_chunks(n_chunks, 8, issue, combine)

        def norm(c, _):
            rows = pl.ds(pl.multiple_of(c * ch, ch), ch)
            o = o_ref[rows, :]
            on_ref[rows, :] = o * _rstd(o) * gain_ref[...]
            return 0

        lax.fori_loop(0, seq // ch, norm, 0)

    seg = lambda k: pl.BlockSpec((None, seq, HG_DK), lambda s, h, k=k: (k, s, h))
    blk = pl.BlockSpec((seq, HG_DK), lambda s, h: (s, h))
    full = pltpu.VMEM((seq, HG_DK), F32)
    return _call_carrying(
        body, carry, name="hgrn_fwd", grid=(n_seq, n_head),
        in_specs=[seg(2), seg(3), seg(4), pl.BlockSpec((1, HG_DK), lambda s, h: (0, h)),
                  pl.BlockSpec((1, HG_DK), lambda s, h: (0, 0))],
        out_specs=[blk, blk],
        out_shape=[jax.ShapeDtypeStruct((t_rows, d), F32), jax.ShapeDtypeStruct((t_rows, d), F32)],
        scratch_shapes=[full, full, full, pltpu.VMEM((n_chunks, HG_DK, HG_DK), F32),
                        pltpu.VMEM((HG_DK, HG_DK), F32)],
        semantics=("parallel", "parallel"), args=(proj, proj, proj, lower_bound, hg_gain))


def _hgrn_bwd(proj, lower_bound, hg_gain, o, d_on, seq, carry=None):
    _, t_rows, d = proj.shape
    n_seq, n_head = t_rows // seq, d // HG_DK
    ch = _row_tile(seq, ROW_CHUNK)
    n_chunks = seq // HG_CHUNK
    cc = HG_CHUNK

    def body(q_ref, z_ref, v_ref, lb_ref, gain_ref, o_ref, don_ref, dp_ref, dlb_ref, dgain_ref,
             qs_s, k_s, g_s, do_s, dqs_s, dk_s, dlf_s, states, dstates, carry_ref):
        hh, ss = pl.program_id(0), pl.program_id(1)
        lb = lb_ref[...]

        @pl.when(ss == 0)
        def _():
            dlb_ref[...] = jnp.zeros_like(dlb_ref)

        @pl.when((ss == 0) & (hh == 0))
        def _():
            dgain_ref[...] = jnp.zeros_like(dgain_ref)

        def prepare(c, _):
            rows = pl.ds(pl.multiple_of(c * ch, ch), ch)
            qs, key, log_f, _, _ = _hg_prepare(q_ref, z_ref, lb, rows)
            qs_s[rows, :] = qs
            k_s[rows, :] = key
            g_s[rows, :] = _seg_cumsum(log_f, cc)
            do, dgain = _rms_bwd(don_ref[rows, :], o_ref[rows, :], gain_ref[...])
            do_s[rows, :] = do
            dgain_ref[...] += dgain
            return 0

        lax.fori_loop(0, seq // ch, prepare, 0)

        _hg_state_terms(v_ref, k_s, g_s, states, n_chunks)
        _hg_state_chain(states, g_s, carry_ref, n_chunks, reverse=False)

        def query_term(c):
            rows = pl.ds(pl.multiple_of(c * cc, cc), cc)
            q_in = qs_s[rows, :] * jnp.exp(g_s[rows, :])
            dstates[c] = _dot_tn(do_s[rows, :].astype(BF16), q_in.astype(BF16))

        _for_chunks(n_chunks, 8, query_term)
        _hg_state_chain(dstates, g_s, carry_ref, n_chunks, reverse=True)
        ones = jnp.ones((HG_DK, HG_DK), BF16)
        row_sums = _chunk_row_sums()

        def chunk_rows(c):
            r0 = pl.multiple_of(c * cc, cc)
            return r0, pl.ds(r0, cc)

        def through_state(c):
            r0, rows = chunk_rows(c)
            kc, gc, vc, doc = k_s[rows, :], g_s[rows, :], v_ref[rows, :], do_s[rows, :]
            st, dst = states[c], dstates[c]
            g_last = gc[cc - 1:cc, :]
            e_last, e_end = jnp.exp(g_last), jnp.exp(g_last - gc)
            dob, dstb = doc.astype(BF16), dst.astype(BF16)
            dqs = _dot(dob, st.astype(BF16))
            dk_state = _dot(vc.astype(BF16), dstb)
            dv = _dot_nt((kc * e_end).astype(BF16), dstb)
            cots = [(doc * v_ref[pl.ds(r0 + s, 1), :]).astype(BF16) for s in range(cc)]
            d_score = _dot(jnp.concatenate(cots, axis=0), ones)
            return dqs, dk_state, dv, d_score, e_last * jnp.sum(dst * st, axis=0, keepdims=True)

        def pair_terms(c, x):
            dqs, dk_state, dv, d_score, d_glast = x
            r0, rows = chunk_rows(c)
            qc, kc, gc = qs_s[rows, :], k_s[rows, :], g_s[rows, :]
            dqs = dqs * jnp.exp(gc)
            dk_state = dk_state * jnp.exp(gc[cc - 1:cc, :] - gc)
            d_glast = d_glast + jnp.sum(kc * dk_state, axis=0, keepdims=True)
            pairs, dk_terms = [], []
            for s in range(cc):
                decay = _hg_pair_terms(g_s, r0, gc, s)
                ks = k_s[pl.ds(r0 + s, 1), :]
                da_decay = d_score[s * cc:(s + 1) * cc, :] * decay
                pairs.append((qc * decay * ks).astype(BF16))
                dk_terms.append((da_decay * qc).astype(BF16))
                dqs = dqs + da_decay * ks
            score = _dot(jnp.concatenate(pairs, axis=0), ones)
            dk = dk_state + _dot(row_sums, jnp.concatenate(dk_terms, axis=0))
            return dqs, dk, dv, score, d_glast

        def value_terms(c, x):
            dqs, dk, dv, score, d_glast = x
            _, rows = chunk_rows(c)
            doc = do_s[rows, :]
            dv_terms = [(score[s * cc:(s + 1) * cc, :] * doc).astype(BF16) for s in range(cc)]
            return dqs, dk, dv + _dot(row_sums, jnp.concatenate(dv_terms, axis=0)), d_glast

        def store(c, x):
            dqs, dk, dv, d_glast = x
            _, rows = chunk_rows(c)
            d_g = qs_s[rows, :] * dqs - k_s[rows, :] * dk
            dlf_s[rows, :] = _seg_cumsum(d_g, cc, reverse=True) + d_glast
            dqs_s[rows, :] = dqs
            dk_s[rows, :] = dk
            dp_ref[2, rows, :] = dv.astype(BF16)

        _for_chunks(n_chunks, 8, through_state, pair_terms, value_terms, store)

        def finish(c, _):
            rows = pl.ds(pl.multiple_of(c * ch, ch), ch)
            z = z_ref[rows, :]
            sig = _sigmoid(z)
            nsig = _sigmoid(-z)
            fg = lb + (1.0 - lb) * sig
            _, dsilu = _silu_and_grad(q_ref[rows, :])
            dp_ref[0, rows, :] = (dqs_s[rows, :] * dsilu).astype(BF16)
            dfg = jnp.where(fg > F_MIN, dlf_s[rows, :] / fg, 0.0)
            dk = dk_s[rows, :]
            dp_ref[1, rows, :] = ((dfg - dk) * (1.0 - lb) * sig * nsig).astype(BF16)
            dlb_ref[...] += jnp.sum((dfg - dk) * nsig, axis=0, keepdims=True)
            return 0

        lax.fori_loop(0, seq // ch, finish, 0)

    seg = lambda k: pl.BlockSpec((None, seq, HG_DK), lambda h, s, k=k: (k, s, h))
    blk = pl.BlockSpec((seq, HG_DK), lambda h, s: (s, h))
    full = pltpu.VMEM((seq, HG_DK), F32)
    return _call_carrying(
        body, carry, name="hgrn_bwd", grid=(n_head, n_seq),
        in_specs=[seg(2), seg(3), seg(4), pl.BlockSpec((1, HG_DK), lambda h, s: (0, h)),
                  pl.BlockSpec((1, HG_DK), lambda h, s: (0, 0)), blk, blk],
        out_specs=[pl.BlockSpec((3, seq, HG_DK), lambda h, s: (0, s, h)),
                   pl.BlockSpec((1, HG_DK), lambda h, s: (0, h)),
                   pl.BlockSpec((1, HG_DK), lambda h, s: (0, 0))],
        out_shape=[jax.ShapeDtypeStruct((3, t_rows, d), BF16), jax.ShapeDtypeStruct((1, d), F32),
                   jax.ShapeDtypeStruct((1, HG_DK), F32)],
        scratch_shapes=[full, full, full, full, full, full, full,
                        pltpu.VMEM((n_chunks, HG_DK, HG_DK), F32), pltpu.VMEM((n_chunks, HG_DK, HG_DK), F32),
                        pltpu.VMEM((HG_DK, HG_DK), F32)],
        semantics=("arbitrary", "arbitrary"), args=(proj, proj, proj, lower_bound, hg_gain, o, d_on))


def _mesh_place():
    x, y, c = lax.axis_index("x"), lax.axis_index("y"), lax.axis_index("c")
    return x, y, c


def _peer(place, k):
    x, y, c = place
    px = 1 - x if k & 4 else x
    py = 1 - y if k & 2 else y
    pc = 1 - c if k & 1 else c
    return (px, py, pc), 4 * px + 2 * py + pc


class _Exchange:
    def __init__(self, srcs, gather):
        self.n = len(srcs)
        self.gather = gather
        self.out_shape = [jax.ShapeDtypeStruct((N_DEV,) + tuple(s.shape if gather else s.shape[1:]), s.dtype)
                          for s in srcs]
        self.scratch = [pltpu.SemaphoreType.DMA((self.n * (N_DEV - 1),)),
                        pltpu.SemaphoreType.DMA((self.n * (N_DEV - 1),)),
                        pltpu.SemaphoreType.DMA((self.n,))]

    def _copies(self, src_refs, out_refs, sems):
        send_sems, recv_sems, local_sems = sems
        place = _mesh_place()
        me = 4 * place[0] + 2 * place[1] + place[2]
        local, sends, recvs = [], [], []
        for a, (src, out) in enumerate(zip(src_refs, out_refs)):
            outgoing = (lambda idx, src=src: src) if self.gather else (lambda idx, src=src: src.at[idx])
            local.append(pltpu.make_async_copy(outgoing(me), out.at[me], local_sems.at[a]))
            for k in range(1, N_DEV):
                peer, peer_idx = _peer(place, k)
                sem = a * (N_DEV - 1) + k - 1
                sends.append(pltpu.make_async_remote_copy(
                    src_ref=outgoing(peer_idx), dst_ref=out.at[me], send_sem=send_sems.at[sem],
                    recv_sem=recv_sems.at[sem], device_id=peer, device_id_type=MESH_ID))
                recvs.append(pltpu.make_async_remote_copy(
                    src_ref=outgoing(peer_idx), dst_ref=out.at[peer_idx], send_sem=send_sems.at[sem],
                    recv_sem=recv_sems.at[sem], device_id=peer, device_id_type=MESH_ID))
        return local, sends, recvs

    def start(self, src_refs, out_refs, sems):
        local, sends, _ = self._copies(src_refs, out_refs, sems)
        for cp in local + sends:
            cp.start()

    def wait(self, src_refs, out_refs, sems):
        local, sends, recvs = self._copies(src_refs, out_refs, sems)
        for cp in recvs:
            cp.wait_recv()
        for cp in sends:
            cp.wait_send()
        for cp in local:
            cp.wait()


def _call_carrying(body, carry, *, name, grid, in_specs, out_specs, out_shape, scratch_shapes, semantics, args):
    if carry is None:
        outs = pl.pallas_call(body, name=name, grid=grid, in_specs=in_specs, out_specs=out_specs,
                              out_shape=out_shape, scratch_shapes=scratch_shapes,
                              compiler_params=_cp(*semantics))(*args)
        return outs, []
    srcs, gather = carry
    ex = _Exchange(srcs, gather)
    n, n_in, n_out, n_scr = ex.n, len(in_specs), len(out_specs), len(scratch_shapes)

    def wrapped(*refs):
        ins, refs = refs[:n_in], refs[n_in:]
        src_refs, refs = refs[:n], refs[n:]
        outs, refs = refs[:n_out], refs[n_out:]
        dst_refs, refs = refs[:n], refs[n:]
        scratch, sems = refs[:n_scr], refs[n_scr:]
        first, last = None, None
        for axis, size in enumerate(grid):
            i = pl.program_id(axis)
            first = (i == 0) if first is None else first & (i == 0)
            last = (i == size - 1) if last is None else last & (i == size - 1)

        @pl.when(first)
        def _():
            ex.start(src_refs, dst_refs, sems)

        body(*ins, *outs, *scratch)

        @pl.when(last)
        def _():
            ex.wait(src_refs, dst_refs, sems)

    any_space = pl.BlockSpec(memory_space=pl.ANY)
    res = pl.pallas_call(
        wrapped, name=name + "_carrying", grid=grid, in_specs=list(in_specs) + [any_space] * n,
        out_specs=list(out_specs) + [any_space] * n, out_shape=list(out_shape) + ex.out_shape,
        scratch_shapes=list(scratch_shapes) + ex.scratch,
        compiler_params=_cp(*(["arbitrary"] * len(grid))))(*args, *srcs)
    return res[:n_out], res[n_out:]


def _exchange(srcs, name, gather):
    ex = _Exchange(srcs, gather)
    n = ex.n

    def body(*refs):
        src_refs, out_refs, sems = refs[:n], refs[n:2 * n], refs[2 * n:]
        ex.start(src_refs, out_refs, sems)
        ex.wait(src_refs, out_refs, sems)

    any_space = pl.BlockSpec(memory_space=pl.ANY)
    return pl.pallas_call(
        body, name=name, in_specs=[any_space] * n, out_specs=[any_space] * n,
        out_shape=ex.out_shape, scratch_shapes=ex.scratch)(*srcs)


def _reduce_adamw(parts, w, m, v, name):
    rows, cols = w.shape
    tr = _row_tile(rows, 128)
    c1 = np.float32(1.0 - ADAM_B1 ** ADAM_STEP)
    c2 = np.float32(1.0 - ADAM_B2 ** ADAM_STEP)

    def body(p_ref, w_ref, m_ref, v_ref, g_ref, d_ref, nm_ref, nv_ref):
        g = p_ref[0].astype(F32)
        for k in range(1, N_DEV):
            g = g + p_ref[k].astype(F32)
        nm = ADAM_B1 * m_ref[...] + (1.0 - ADAM_B1) * g
        nv = ADAM_B2 * v_ref[...] + (1.0 - ADAM_B2) * (g * g)
        g_ref[...] = g
        nm_ref[...] = nm
        nv_ref[...] = nv
        d_ref[...] = -ADAM_LR * ((nm / c1) / (jnp.sqrt(nv / c2) + ADAM_EPS) + ADAM_WD * w_ref[...])

    blk = pl.BlockSpec((tr, cols), lambda i: (i, 0))
    shp = jax.ShapeDtypeStruct((rows, cols), F32)
    return pl.pallas_call(
        body, name=name, grid=(rows // tr,),
        in_specs=[pl.BlockSpec((N_DEV, tr, cols), lambda i: (0, i, 0)), blk, blk, blk],
        out_specs=[blk, blk, blk, blk], out_shape=[shp, shp, shp, shp],
        compiler_params=_cp("parallel"))(parts, w, m, v)


def _pack(arrays, lead=0):
    parts = []
    for a in arrays:
        f = a.reshape(a.shape[:lead] + (-1, LANES))
        pad = -f.shape[lead] % PACK_ROWS
        if pad:
            f = jnp.pad(f, [(0, 0)] * lead + [(0, pad), (0, 0)])
        parts.append(f)
    return jnp.concatenate(parts, axis=lead)


def _unpack(buf, shapes, lead=0):
    out, r = [], 0
    for shp in shapes:
        n = int(np.prod(shp)) // LANES
        part = lax.slice_in_dim(buf, r, r + n, axis=lead)
        out.append(part.reshape(buf.shape[:lead] + tuple(shp)))
        r += n + (-n % PACK_ROWS)
    return out


REPLICATED = ("lb_logits", "norm_mix", "conv_b", "b_r", "b_i", "lam", "hg_norm", "norm_mlp", "norm_final")
SMALL_SHARDED = ("conv_w", "w_r", "w_i")
LARGE_SHARDED = ("w_in", "w_out", "w_up", "w_down")
WEIGHTS = ("lb_logits", "norm_mix", "w_in", "conv_w", "conv_b", "w_r", "b_r", "w_i", "b_i", "lam", "hg_norm",
           "w_out", "norm_mlp", "w_up", "w_down", "norm_final")


def _matmul_weight_shards(p):
    depth, d, _ = p["w_in"].shape
    f8 = p["w_up"].shape[2]
    parts = [p["w_in"], p["w_out"], p["w_up"].reshape(depth, f8, d), p["w_down"]]
    stacked = jnp.concatenate([a.astype(BF16) for a in parts], axis=1)
    return [stacked[l] for l in range(depth)]


def _split_matmul_weights(g, f8):
    d = g.shape[2]
    d8 = d // N_DEV
    return dict(w_in=g[:, :d],
                w_out=g[:, d:d + d8].reshape(d, d),
                w_up=g[:, d + d8:d + d8 + f8].reshape(N_DEV, d, f8),
                w_down=g[:, d + d8 + f8:])


def _gather_mixer_weights(p):
    depth, d, _ = p["w_in"].shape
    n_blk = d // RG_BLOCK_W
    small = _exchange([_pack([p["conv_w"], p["w_r"], p["w_i"]])], "gather_mixer_weights", gather=True)[0]
    conv_w, w_r, w_i = _unpack(small, [p["conv_w"].shape, p["w_r"].shape, p["w_i"].shape], lead=1)
    conv_w = conv_w.transpose(1, 2, 0, 3).reshape(depth, CONV_W, d)
    w_r = w_r.transpose(1, 2, 0, 3, 4).reshape(depth, n_blk, RG_BLOCK_W, RG_BLOCK_W).astype(BF16)
    w_i = w_i.transpose(1, 2, 0, 3, 4).reshape(depth, n_blk, RG_BLOCK_W, RG_BLOCK_W).astype(BF16)
    return conv_w, w_r, w_i


def _local_step(x, target, p):
    bl, seq, d = x.shape
    depth, f8 = p["w_in"].shape[0], p["w_up"].shape[2]
    t_rows = bl * seq
    row = lambda a, l: a[l:l + 1]
    lbs = _lower_bounds_fwd(p["lb_logits"])
    shards = _matmul_weight_shards(p)
    conv_w, w_r, w_i = _gather_mixer_weights(p)
    gathered = _exchange([shards[0]], "gather_matmul_weights", gather=True)
    cur = x.reshape(t_rows, d)
    saved, layers = [], []
    for l in range(depth):
        w = dict(_split_matmul_weights(gathered[0], f8), conv_w=conv_w[l], w_r=w_r[l], w_i=w_i[l])
        layers.append(w)
        proj, h = _inproj_fwd(cur, row(p["norm_mix"], l), w["w_in"])
        hs, y_a = _mixer_a_fwd(proj, w["conv_w"], row(p["conv_b"], l), w["w_r"], row(p["b_r"], l), w["w_i"],
                               row(p["b_i"], l), row(p["lam"], l), seq)
        (o, o_n), gathered = _hgrn_fwd(proj, row(lbs, l), row(p["hg_norm"], l), seq,
                                       carry=([shards[l + 1]], True) if l + 1 < depth else None)
        x_mid, y = _merge_out_fwd(proj, y_a, o_n, cur, w["w_out"])
        x_out, u, h2 = _mlp_fwd(x_mid, row(p["norm_mlp"], l), w["w_up"], w["w_down"])
        saved.append(dict(x_in=cur, proj=proj, h=h, hs=hs, y_a=y_a, o=o, o_n=o_n, x_mid=x_mid, y=y, u=u, h2=h2))
        cur = x_out
    loss8, dx, dxb, g_norm_final = _loss_head(cur, p["norm_final"].reshape(1, d), target.reshape(t_rows, d))
    small = ("norm_mix", "conv_w", "conv_b", "w_r", "b_r", "w_i", "b_i", "lam", "hg_norm", "norm_mlp")
    g = {k: [None] * depth for k in small}
    d_lbs, received = [None] * depth, [None] * depth
    pending = None
    for l in reversed(range(depth)):
        s, w = saved[l], layers[l]
        dx_mid, dx_mid_b, du, act, g["norm_mlp"][l] = _mlp_bwd(dx, s["u"], s["x_mid"], row(p["norm_mlp"], l),
                                                               w["w_up"], w["w_down"])
        g_w_down = _wgrad(act, dxb[None], "wgrad_down")
        g_w_up = _wgrad(s["h2"][None], du, "wgrad_up")
        d_ya, d_on, dp_c = _outproj_bwd(dx_mid_b, w["w_out"], s["proj"], s["y_a"], s["o_n"])
        g_w_out = _wgrad(s["y"][None], dx_mid_b[None], "wgrad_out")
        (dp_b, d_lbs[l], g["hg_norm"][l]), got = _hgrn_bwd(
            s["proj"], row(lbs, l), row(p["hg_norm"], l), s["o"], d_on, seq,
            carry=(pending, False) if pending is not None else None)
        if pending is not None:
            received[l + 1] = got
        (dp_a, g["w_r"][l], g["w_i"][l], g["b_r"][l], g["b_i"][l], g["lam"][l], g["conv_w"][l],
         g["conv_b"][l]) = _mixer_a_bwd(s["proj"], s["hs"], d_ya, w["conv_w"], row(p["conv_b"], l), w["w_r"],
                                        row(p["b_r"], l), w["w_i"], row(p["b_i"], l), row(p["lam"], l), seq)
        dx, dxb, g["norm_mix"][l] = _inproj_bwd(dx_mid, dp_a, dp_b, dp_c, w["w_in"], s["x_in"],
                                                row(p["norm_mix"], l))
        hb = s["h"][None]
        g_w_in = jnp.concatenate([_wgrad(hb, dp_a, "wgrad_in_pair"), _wgrad(hb, dp_b, "wgrad_in_triple"),
                                  _wgrad(hb, dp_c, "wgrad_in_triple")], axis=0)
        pending = [g_w_in, g_w_out.reshape(N_DEV, d // N_DEV, d), g_w_up, g_w_down]
    received[0] = _exchange(pending, "scatter_grad_matmul_weights", gather=False)
    grads = {k: jnp.stack(v) for k, v in g.items()}
    for k in ("norm_mix", "conv_b", "b_r", "b_i", "lam", "hg_norm", "norm_mlp"):
        grads[k] = grads[k][:, 0]
    grads["lb_logits"] = _lower_bounds_bwd(p["lb_logits"], jnp.concatenate(d_lbs, axis=0))
    grads["norm_final"] = g_norm_final[0]
    return loss8[0, 0], dx.reshape(bl, seq, d), grads, received


def _update(p, mom1, mom2, grads, received):
    depth, d, _ = p["w_in"].shape
    d8 = d // N_DEV
    n_blk, rb = d // RG_BLOCK_W, RG_BLOCK_W // N_DEV
    out = {}

    for i, k in enumerate(LARGE_SHARDED):
        shp = p[k].shape
        flat = lambda a: a.reshape(shp[0] * shp[1], shp[2])
        parts = jnp.concatenate([received[l][i] for l in range(depth)], axis=1)
        res = _reduce_adamw(parts, flat(p[k]), flat(mom1[k]), flat(mom2[k]), "adamw_" + k)
        out[k] = [r.reshape(shp) for r in res]

    small_to_dev = [
        grads["conv_w"].reshape(depth, CONV_W, N_DEV, d8).transpose(2, 0, 1, 3),
        grads["w_r"].reshape(depth, n_blk, N_DEV, rb, RG_BLOCK_W).transpose(2, 0, 1, 3, 4),
        grads["w_i"].reshape(depth, n_blk, N_DEV, rb, RG_BLOCK_W).transpose(2, 0, 1, 3, 4)]
    parts = _exchange([_pack(small_to_dev, lead=1)], "scatter_grad_mixer", gather=False)[0]
    res = _reduce_adamw(parts, *[_pack([src[k] for k in SMALL_SHARDED]) for src in (p, mom1, mom2)],
                        "adamw_mixer")
    shapes = [p[k].shape for k in SMALL_SHARDED]
    for i, vals in enumerate(zip(*[_unpack(r, shapes) for r in res])):
        out[SMALL_SHARDED[i]] = list(vals)

    parts = _exchange([_pack([grads[k] for k in REPLICATED])], "gather_grad_replicated", gather=True)[0]
    res = _reduce_adamw(parts, *[_pack([src[k] for k in REPLICATED]) for src in (p, mom1, mom2)],
                        "adamw_replicated")
    shapes = [p[k].shape for k in REPLICATED]
    for i, vals in enumerate(zip(*[_unpack(r, shapes) for r in res])):
        out[REPLICATED[i]] = list(vals)

    return tuple(out[k][i] for i in range(4) for k in WEIGHTS)


def kernel(x, lb_logits, norm_mix, w_in, conv_w, conv_b, w_r, b_r, w_i, b_i, lam, hg_norm, w_out, norm_mlp, w_up, w_down, norm_final, loss_target, m_lb_logits, m_norm_mix, m_w_in, m_conv_w, m_conv_b, m_w_r, m_b_r, m_w_i, m_b_i, m_lam, m_hg_norm, m_w_out, m_norm_mlp, m_w_up, m_w_down, m_norm_final, v_lb_logits, v_norm_mix, v_w_in, v_conv_w, v_conv_b, v_w_r, v_b_r, v_w_i, v_b_i, v_lam, v_hg_norm, v_w_out, v_norm_mlp, v_w_up, v_w_down, v_norm_final):
    p = dict(lb_logits=lb_logits, norm_mix=norm_mix, w_in=w_in, conv_w=conv_w, conv_b=conv_b, w_r=w_r, b_r=b_r,
             w_i=w_i, b_i=b_i, lam=lam, hg_norm=hg_norm, w_out=w_out, norm_mlp=norm_mlp, w_up=w_up,
             w_down=w_down, norm_final=norm_final)
    mom1 = dict(lb_logits=m_lb_logits, norm_mix=m_norm_mix, w_in=m_w_in, conv_w=m_conv_w, conv_b=m_conv_b,
                w_r=m_w_r, b_r=m_b_r, w_i=m_w_i, b_i=m_b_i, lam=m_lam, hg_norm=m_hg_norm, w_out=m_w_out,
                norm_mlp=m_norm_mlp, w_up=m_w_up, w_down=m_w_down, norm_final=m_norm_final)
    mom2 = dict(lb_logits=v_lb_logits, norm_mix=v_norm_mix, w_in=v_w_in, conv_w=v_conv_w, conv_b=v_conv_b,
                w_r=v_w_r, b_r=v_b_r, w_i=v_w_i, b_i=v_b_i, lam=v_lam, hg_norm=v_hg_norm, w_out=v_w_out,
                norm_mlp=v_norm_mlp, w_up=v_w_up, w_down=v_w_down, norm_final=v_norm_final)
    loss, grad_x, grads, received = _local_step(x, loss_target, p)
    loss = lax.psum(loss, ("x", "y", "c"))
    return (loss, grad_x) + _update(p, mom1, mom2, grads, received)
```

```python
import numpy as np

import jax
import jax.numpy as jnp
from jax import lax
from jax.experimental import pallas as pl
from jax.experimental.pallas import tpu as pltpu

F32 = jnp.float32
BF16 = jnp.bfloat16
MESH_ID = pl.DeviceIdType.MESH

N_DEV = 8
NORM_EPS = 1e-6
RG_C = 8.0
RG_BLOCK_W = 256
CONV_W = 4
HG_DK = 128
F_MIN = 1e-30
HG_CHUNK = 16
SUBLANES = 8
LANES = 128
PACK_ROWS = 16
ROW_CHUNK = 256
ROW_TILE_WEIGHT_STREAM = 1024
WGRAD_TOKEN_TILE = 2048
VMEM_LIMIT_V7X = 56 * 1024 * 1024

ADAM_LR = 0.001
ADAM_B1 = 0.9
ADAM_B2 = 0.999
ADAM_EPS = 1e-08
ADAM_WD = 0.01
ADAM_STEP = 10

GELU_C = 0.7978845608028654
GELU_K = 0.044715


def _cp(*sem):
    return pltpu.CompilerParams(dimension_semantics=sem, vmem_limit_bytes=VMEM_LIMIT_V7X)


def _row_tile(n, cap):
    if n <= cap:
        return n
    t = cap - cap % 16
    while n % t:
        t -= 16
    return t


def _dot(a, b):
    return jnp.dot(a, b, preferred_element_type=F32)


def _dot_nt(a, b):
    return lax.dot_general(a, b, (((1,), (1,)), ((), ())), preferred_element_type=F32)


def _dot_tn(a, b):
    return lax.dot_general(a, b, (((0,), (0,)), ((), ())), preferred_element_type=F32)


def _sigmoid(x):
    return jax.nn.sigmoid(x)


def _log1p_pos(y):
    return jnp.where(y < 0.01, y * (1.0 - y * (0.5 - y * (1.0 / 3.0))), jnp.log(1.0 + y))


def _softplus(x):
    return jnp.maximum(x, 0.0) + _log1p_pos(jnp.exp(-jnp.abs(x)))


def _one_minus_exp(x):
    series = -x * (1.0 + x * 0.5 * (1.0 + x * (1.0 / 3.0) * (1.0 + x * 0.25 * (1.0 + x * 0.2))))
    return jnp.where(x > -0.1, series, 1.0 - jnp.exp(x))


def _gelu_and_grad(x):
    x2 = x * x
    t = jnp.tanh(GELU_C * x * (1.0 + GELU_K * x2))
    g = 0.5 * x * (1.0 + t)
    dg = 0.5 * (1.0 + t) + 0.5 * x * (1.0 - t * t) * GELU_C * (1.0 + 3.0 * GELU_K * x2)
    return g, dg


def _silu_and_grad(x):
    s = _sigmoid(x)
    return x * s, s * (1.0 + x * (1.0 - s))


def _rstd(x):
    return lax.rsqrt(jnp.mean(x * x, axis=-1, keepdims=True) + NORM_EPS)


def _rms_bwd(dh, x, g):
    rstd = _rstd(x)
    xh = x * rstd
    dxh = dh * g
    dx = rstd * (dxh - xh * jnp.mean(dxh * xh, axis=-1, keepdims=True))
    return dx, jnp.sum(dh * xh, axis=0, keepdims=True)


def _shift_rows(x, k):
    n = x.shape[0]
    k = k % n
    return x if k == 0 else pltpu.roll(x, k, axis=0)


def _seg_cumsum(x, seg, reverse=False):
    n = x.shape[0]
    rid = lax.broadcasted_iota(jnp.int32, x.shape, 0) & (seg - 1)
    d = 1
    while d < seg:
        if reverse:
            x = jnp.where(rid < seg - d, x + _shift_rows(x, n - d), x)
        else:
            x = jnp.where(rid >= d, x + _shift_rows(x, d), x)
        d *= 2
    return x


def _scan_rows(a_ref, b_ref, out_ref, n_rows, width, reverse):
    rid = lax.broadcasted_iota(jnp.int32, (SUBLANES, width), 0)
    n_groups = n_rows // SUBLANES

    def group(i, carry):
        g = n_groups - 1 - i if reverse else i
        r0 = pl.multiple_of(g * SUBLANES, SUBLANES)
        a = a_ref[pl.ds(r0, SUBLANES), :]
        b = b_ref[pl.ds(r0, SUBLANES), :]
        for d in (1, 2, 4):
            if reverse:
                keep = rid < SUBLANES - d
                a_sh, b_sh = _shift_rows(a, SUBLANES - d), _shift_rows(b, SUBLANES - d)
            else:
                keep = rid >= d
                a_sh, b_sh = _shift_rows(a, d), _shift_rows(b, d)
            b = jnp.where(keep, a * b_sh + b, b)
            a = jnp.where(keep, a * a_sh, a)
        out = a * carry + b
        out_ref[pl.ds(r0, SUBLANES), :] = out
        edge = out[0:1, :] if reverse else out[SUBLANES - 1:SUBLANES, :]
        return jnp.broadcast_to(edge, (SUBLANES, width))

    lax.fori_loop(0, n_groups, group, jnp.zeros((SUBLANES, width), F32), unroll=4)


def _lb_softmax_rows(x_ref, depth):
    rows = [x_ref[pl.ds(l, 1), :] for l in range(depth)]
    top = rows[0]
    for r in rows[1:]:
        top = jnp.maximum(top, r)
    e = [jnp.exp(r - top) for r in rows]
    tot = e[0]
    for r in e[1:]:
        tot = tot + r
    return [r / tot for r in e]


def _lower_bounds_fwd(lb_logits):
    depth, d = lb_logits.shape

    def body(x_ref, o_ref):
        sm = _lb_softmax_rows(x_ref, depth)
        cum = jnp.zeros((1, d), F32)
        for l in range(depth):
            cum = cum + sm[l]
            o_ref[pl.ds(l, 1), :] = jnp.clip(cum - sm[0], 0.0, 1.0)

    return pl.pallas_call(body, name="lower_bounds_fwd",
                          out_shape=jax.ShapeDtypeStruct((depth, d), F32))(lb_logits)


def _lower_bounds_bwd(lb_logits, d_lbs):
    depth, d = lb_logits.shape

    def body(x_ref, g_ref, o_ref):
        sm = _lb_softmax_rows(x_ref, depth)
        cum = jnp.zeros((1, d), F32)
        d_cum = []
        for l in range(depth):
            cum = cum + sm[l]
            v = cum - sm[0]
            d_cum.append(jnp.where((v > 0.0) & (v < 1.0), g_ref[pl.ds(l, 1), :], 0.0))
        d_sm = []
        tail = jnp.zeros((1, d), F32)
        for l in reversed(range(depth)):
            tail = tail + d_cum[l]
            d_sm.append(tail)
        d_sm = d_sm[::-1]
        d_sm[0] = d_sm[0] - tail
        inner = jnp.zeros((1, d), F32)
        for l in range(depth):
            inner = inner + sm[l] * d_sm[l]
        for l in range(depth):
            o_ref[pl.ds(l, 1), :] = sm[l] * (d_sm[l] - inner)

    return pl.pallas_call(body, name="lower_bounds_bwd",
                          out_shape=jax.ShapeDtypeStruct((depth, d), F32))(lb_logits, d_lbs)


def _inproj_fwd(x, gain, w_seg, carry=None):
    t_rows, d = x.shape
    tm = _row_tile(t_rows, ROW_TILE_WEIGHT_STREAM)

    def body(x_ref, g_ref, w_ref, proj_ref, h_ref):
        @pl.when(pl.program_id(1) == 0)
        def _():
            xv = x_ref[...]
            h_ref[...] = (xv * _rstd(xv) * g_ref[...]).astype(BF16)

        proj_ref[...] = _dot(h_ref[...], w_ref[...])

    return _call_carrying(
        body, carry, name="inproj_fwd", grid=(t_rows // tm, N_DEV),
        in_specs=[pl.BlockSpec((tm, d), lambda i, j: (i, 0)),
                  pl.BlockSpec((1, d), lambda i, j: (0, 0)),
                  pl.BlockSpec((None, d, d), lambda i, j: (j, 0, 0))],
        out_specs=[pl.BlockSpec((None, tm, d), lambda i, j: (j, i, 0)),
                   pl.BlockSpec((tm, d), lambda i, j: (i, 0))],
        out_shape=[jax.ShapeDtypeStruct((N_DEV, t_rows, d), F32),
                   jax.ShapeDtypeStruct((t_rows, d), BF16)],
        scratch_shapes=[], semantics=("parallel", "arbitrary"), args=(x, gain, w_seg))


def _merge_out_fwd(proj, y_a, o_n, x, w_out):
    t_rows, d = x.shape
    tm = _row_tile(t_rows, 256)

    def body(g_ref, ma_ref, mb_ref, ya_ref, on_ref, x_ref, w_ref, xmid_ref, y_ref):
        g = g_ref[...]
        y = _sigmoid(ma_ref[...]) * ya_ref[...] + _sigmoid(mb_ref[...]) * (on_ref[...] * (g * _sigmoid(g)))
        yb = y.astype(BF16)
        y_ref[...] = yb
        xmid_ref[...] = x_ref[...] + _dot(yb, w_ref[...])

    seg = lambda k: pl.BlockSpec((None, tm, d), lambda i, k=k: (k, i, 0))
    row = pl.BlockSpec((tm, d), lambda i: (i, 0))
    return pl.pallas_call(
        body, name="merge_out_fwd", grid=(t_rows // tm,),
        in_specs=[seg(5), seg(6), seg(7), row, row, row, pl.BlockSpec((d, d), lambda i: (0, 0))],
        out_specs=[row, row],
        out_shape=[jax.ShapeDtypeStruct((t_rows, d), F32), jax.ShapeDtypeStruct((t_rows, d), BF16)],
        compiler_params=_cp("parallel"))(proj, proj, proj, y_a, o_n, x, w_out)


def _mlp_fwd(x_mid, gain, w_up, w_down):
    t_rows, d = x_mid.shape
    f8 = w_up.shape[2]
    tm = _row_tile(t_rows, ROW_TILE_WEIGHT_STREAM)

    def body(x_ref, g_ref, wu_ref, wd_ref, out_ref, u_ref, h_ref):
        @pl.when(pl.program_id(1) == 0)
        def _():
            xv = x_ref[...]
            h_ref[...] = (xv * _rstd(xv) * g_ref[...]).astype(BF16)
            out_ref[...] = xv

        u = _dot(h_ref[...], wu_ref[...])
        u_ref[...] = u
        r = jnp.maximum(u, 0.0)
        out_ref[...] += _dot((r * r).astype(BF16), wd_ref[...])

    row = pl.BlockSpec((tm, d), lambda i, j: (i, 0))
    return pl.pallas_call(
        body, name="mlp_fwd", grid=(t_rows // tm, N_DEV),
        in_specs=[row, pl.BlockSpec((1, d), lambda i, j: (0, 0)),
                  pl.BlockSpec((None, d, f8), lambda i, j: (j, 0, 0)),
                  pl.BlockSpec((None, f8, d), lambda i, j: (j, 0, 0))],
        out_specs=[row, pl.BlockSpec((None, tm, f8), lambda i, j: (j, i, 0)), row],
        out_shape=[jax.ShapeDtypeStruct((t_rows, d), F32),
                   jax.ShapeDtypeStruct((N_DEV, t_rows, f8), F32),
                   jax.ShapeDtypeStruct((t_rows, d), BF16)],
        compiler_params=_cp("parallel", "arbitrary"))(x_mid, gain, w_up, w_down)


def _loss_head(x, gain, target):
    t_rows, d = x.shape
    tm = _row_tile(t_rows, 512)

    def body(x_ref, g_ref, t_ref, loss_ref, dx_ref, dxb_ref, dg_ref):
        @pl.when(pl.program_id(0) == 0)
        def _():
            loss_ref[...] = jnp.zeros_like(loss_ref)
            dg_ref[...] = jnp.zeros_like(dg_ref)

        xv = x_ref[...]
        g = g_ref[...]
        err = xv * _rstd(xv) * g - t_ref[...]
        loss_ref[...] += (0.5 / d) * jnp.sum(err * err)
        dx, dg = _rms_bwd(err * (1.0 / d), xv, g)
        dx_ref[...] = dx
        dxb_ref[...] = dx.astype(BF16)
        dg_ref[...] += dg

    row = pl.BlockSpec((tm, d), lambda i: (i, 0))
    vec = pl.BlockSpec((1, d), lambda i: (0, 0))
    return pl.pallas_call(
        body, name="loss_head", grid=(t_rows // tm,),
        in_specs=[row, vec, row],
        out_specs=[pl.BlockSpec((SUBLANES, LANES), lambda i: (0, 0)), row, row, vec],
        out_shape=[jax.ShapeDtypeStruct((SUBLANES, LANES), F32),
                   jax.ShapeDtypeStruct((t_rows, d), F32),
                   jax.ShapeDtypeStruct((t_rows, d), BF16),
                   jax.ShapeDtypeStruct((1, d), F32)],
        compiler_params=_cp("arbitrary"))(x, gain, target)


def _mlp_bwd(d_out, d_out_b, u, x_mid, gain, w_up, w_down):
    t_rows, d = x_mid.shape
    f8 = w_up.shape[2]
    tm = _row_tile(t_rows, 512)

    def body(do_ref, dob_ref, u_ref, x_ref, g_ref, wu_ref, wd_ref, dx_ref, dxb_ref, du_ref, act_ref, dg_ref,
             acc_ref):
        j = pl.program_id(1)

        @pl.when((pl.program_id(0) == 0) & (j == 0))
        def _():
            dg_ref[...] = jnp.zeros_like(dg_ref)

        @pl.when(j == 0)
        def _():
            acc_ref[...] = jnp.zeros_like(acc_ref)

        r = jnp.maximum(u_ref[...], 0.0)
        act_ref[...] = (r * r).astype(BF16)
        du = (_dot_nt(dob_ref[...], wd_ref[...]) * (2.0 * r)).astype(BF16)
        du_ref[...] = du
        acc_ref[...] += _dot_nt(du, wu_ref[...])

        @pl.when(j == N_DEV - 1)
        def _():
            dx, dg = _rms_bwd(acc_ref[...], x_ref[...], g_ref[...])
            dx = dx + do_ref[...]
            dx_ref[...] = dx
            dxb_ref[...] = dx.astype(BF16)
            dg_ref[...] += dg

    row = pl.BlockSpec((tm, d), lambda i, j: (i, 0))
    vec = pl.BlockSpec((1, d), lambda i, j: (0, 0))
    hid = pl.BlockSpec((None, tm, f8), lambda i, j: (j, i, 0))
    return pl.pallas_call(
        body, name="mlp_bwd", grid=(t_rows // tm, N_DEV),
        in_specs=[row, row, hid, row, vec,
                  pl.BlockSpec((None, d, f8), lambda i, j: (j, 0, 0)),
                  pl.BlockSpec((None, f8, d), lambda i, j: (j, 0, 0))],
        out_specs=[row, row, hid, hid, vec],
        out_shape=[jax.ShapeDtypeStruct((t_rows, d), F32),
                   jax.ShapeDtypeStruct((t_rows, d), BF16),
                   jax.ShapeDtypeStruct((N_DEV, t_rows, f8), BF16),
                   jax.ShapeDtypeStruct((N_DEV, t_rows, f8), BF16),
                   jax.ShapeDtypeStruct((1, d), F32)],
        scratch_shapes=[pltpu.VMEM((tm, d), F32)],
        compiler_params=_cp("arbitrary", "arbitrary"))(d_out, d_out_b, u, x_mid, gain, w_up, w_down)


def _outproj_bwd(dx_mid_b, w_out, proj, y_a, o_n):
    t_rows, d = y_a.shape
    tm = _row_tile(t_rows, 256)

    def body(dx_ref, w_ref, g_ref, ma_ref, mb_ref, ya_ref, on_ref, dya_ref, don_ref, dp_ref):
        dy = _dot_nt(dx_ref[...], w_ref[...])
        sa = _sigmoid(ma_ref[...])
        sb = _sigmoid(mb_ref[...])
        sg, dsg = _silu_and_grad(g_ref[...])
        ya = ya_ref[...]
        on = on_ref[...]
        dya_ref[...] = dy * sa
        t = dy * sb
        don_ref[...] = t * sg
        dp_ref[0] = (t * on * dsg).astype(BF16)
        dp_ref[1] = (dy * ya * sa * (1.0 - sa)).astype(BF16)
        dp_ref[2] = (dy * on * sg * sb * (1.0 - sb)).astype(BF16)

    seg = lambda k: pl.BlockSpec((None, tm, d), lambda i, k=k: (k, i, 0))
    row = pl.BlockSpec((tm, d), lambda i: (i, 0))
    return pl.pallas_call(
        body, name="outproj_bwd", grid=(t_rows // tm,),
        in_specs=[row, pl.BlockSpec((d, d), lambda i: (0, 0)), seg(5), seg(6), seg(7), row, row],
        out_specs=[row, row, pl.BlockSpec((3, tm, d), lambda i: (0, i, 0))],
        out_shape=[jax.ShapeDtypeStruct((t_rows, d), F32),
                   jax.ShapeDtypeStruct((t_rows, d), F32),
                   jax.ShapeDtypeStruct((3, t_rows, d), BF16)],
        compiler_params=_cp("parallel"))(dx_mid_b, w_out, proj, proj, proj, y_a, o_n)


def _inproj_bwd(dx_mid, dp_a, dp_b, dp_c, w_seg, x_in, gain):
    t_rows, d = x_in.shape
    tm = _row_tile(t_rows, 512)
    n_a, n_b = dp_a.shape[0], dp_b.shape[0]

    def body(dxm_ref, a_ref, b_ref, c_ref, w_ref, x_ref, g_ref, dx_ref, dxb_ref, dg_ref, acc_ref):
        j = pl.program_id(1)

        @pl.when((pl.program_id(0) == 0) & (j == 0))
        def _():
            dg_ref[...] = jnp.zeros_like(dg_ref)

        @pl.when(j == 0)
        def _():
            acc_ref[...] = jnp.zeros_like(acc_ref)

        @pl.when(j < n_a)
        def _():
            acc_ref[...] += _dot_nt(a_ref[...], w_ref[...])

        @pl.when((j >= n_a) & (j < n_a + n_b))
        def _():
            acc_ref[...] += _dot_nt(b_ref[...], w_ref[...])

        @pl.when(j >= n_a + n_b)
        def _():
            acc_ref[...] += _dot_nt(c_ref[...], w_ref[...])

        @pl.when(j == N_DEV - 1)
        def _():
            dx, dg = _rms_bwd(acc_ref[...], x_ref[...], g_ref[...])
            dx = dx + dxm_ref[...]
            dx_ref[...] = dx
            dxb_ref[...] = dx.astype(BF16)
            dg_ref[...] += dg

    def part(first, n):
        return pl.BlockSpec((None, tm, d), lambda i, j: (jnp.clip(j - first, 0, n - 1), i, 0))

    row = pl.BlockSpec((tm, d), lambda i, j: (i, 0))
    vec = pl.BlockSpec((1, d), lambda i, j: (0, 0))
    return pl.pallas_call(
        body, name="inproj_bwd", grid=(t_rows // tm, N_DEV),
        in_specs=[row, part(0, n_a), part(n_a, n_b), part(n_a + n_b, dp_c.shape[0]),
                  pl.BlockSpec((None, d, d), lambda i, j: (j, 0, 0)), row, vec],
        out_specs=[row, row, vec],
        out_shape=[jax.ShapeDtypeStruct((t_rows, d), F32),
                   jax.ShapeDtypeStruct((t_rows, d), BF16),
                   jax.ShapeDtypeStruct((1, d), F32)],
        scratch_shapes=[pltpu.VMEM((tm, d), F32)],
        compiler_params=_cp("arbitrary", "arbitrary"))(dx_mid, dp_a, dp_b, dp_c, w_seg, x_in, gain)


def _wgrad(a3, b3, name):
    n_a, t_rows, k_a = a3.shape
    n_b, _, n_cols = b3.shape
    n = max(n_a, n_b)
    bk = _row_tile(k_a, 1024)
    bn = n_cols if n_cols <= 1024 else 1024
    tt = _row_tile(t_rows, WGRAD_TOKEN_TILE)
    n_t = t_rows // tt

    def body(a_ref, b_ref, o_ref, acc_ref):
        @pl.when(pl.program_id(3) == 0)
        def _():
            acc_ref[...] = jnp.zeros_like(acc_ref)

        acc_ref[...] += _dot_tn(a_ref[...], b_ref[...])

        @pl.when(pl.program_id(3) == n_t - 1)
        def _():
            o_ref[...] = acc_ref[...].astype(BF16)

    return pl.pallas_call(
        body, name=name, grid=(n, k_a // bk, n_cols // bn, n_t),
        in_specs=[pl.BlockSpec((None, tt, bk), lambda j, p, q, t: (j if n_a > 1 else 0, t, p)),
                  pl.BlockSpec((None, tt, bn), lambda j, p, q, t: (j if n_b > 1 else 0, t, q))],
        out_specs=pl.BlockSpec((None, bk, bn), lambda j, p, q, t: (j, p, q)),
        out_shape=jax.ShapeDtypeStruct((n, k_a, n_cols), BF16),
        scratch_shapes=[pltpu.VMEM((bk, bn), F32)],
        compiler_params=_cp("parallel", "parallel", "parallel", "arbitrary"))(a3, b3)


def _conv_taps(xe, n):
    return [_shift_rows(xe, CONV_W - 1 - j)[SUBLANES:SUBLANES + n, :] for j in range(CONV_W)]


def _rg_gates(xc, w_r, b_r, w_i, b_i, sp8):
    xb = xc.astype(BF16)
    r = _sigmoid(_dot(xb, w_r) + b_r)
    i = _sigmoid(_dot(xb, w_i) + b_i)
    return r, i


def _mixer_a_fwd(proj, conv_w, conv_b, w_r, b_r, w_i, b_i, lam, seq):
    _, t_rows, d = proj.shape
    n_seq, n_blk = t_rows // seq, d // RG_BLOCK_W
    wb = RG_BLOCK_W
    ch = _row_tile(seq, ROW_CHUNK)

    def body(xa_ref, ga_ref, cw_ref, cb_ref, wr_ref, br_ref, wi_ref, bi_ref, lam_ref, h_ref, ya_ref,
             xpad, a_s, u_s):
        xpad[0:SUBLANES, :] = jnp.zeros((SUBLANES, wb), F32)
        xpad[SUBLANES:, :] = xa_ref[...]
        sp8 = RG_C * _softplus(-lam_ref[...])

        def gates(c, _):
            r0 = pl.multiple_of(c * ch, ch)
            taps = _conv_taps(xpad[pl.ds(r0, ch + SUBLANES), :], ch)
            xc = cb_ref[...] + sum(cw_ref[pl.ds(j, 1), :] * taps[j] for j in range(CONV_W))
            r, i = _rg_gates(xc, wr_ref[...], br_ref[...], wi_ref[...], bi_ref[...], sp8)
            log_a = -(r * sp8)
            a_s[pl.ds(r0, ch), :] = jnp.exp(log_a)
            u_s[pl.ds(r0, ch), :] = jnp.sqrt(jnp.maximum(_one_minus_exp(2.0 * log_a), 0.0)) * (i * xc)
            return 0

        lax.fori_loop(0, seq // ch, gates, 0)
        _scan_rows(a_s, u_s, h_ref, seq, wb, reverse=False)

        def gate_out(c, _):
            r0 = pl.multiple_of(c * ch, ch)
            gl, _ = _gelu_and_grad(ga_ref[pl.ds(r0, ch), :])
            ya_ref[pl.ds(r0, ch), :] = h_ref[pl.ds(r0, ch), :] * gl
            return 0

        lax.fori_loop(0, seq // ch, gate_out, 0)

    seg = lambda k: pl.BlockSpec((None, seq, wb), lambda s, b, k=k: (k, s, b))
    blk = pl.BlockSpec((seq, wb), lambda s, b: (s, b))
    vec = pl.BlockSpec((1, wb), lambda s, b: (0, b))
    wsp = pl.BlockSpec((None, wb, wb), lambda s, b: (b, 0, 0))
    return pl.pallas_call(
        body, name="mixer_a_fwd", grid=(n_seq, n_blk),
        in_specs=[seg(0), seg(1), pl.BlockSpec((CONV_W, wb), lambda s, b: (0, b)), vec, wsp, vec, wsp, vec, vec],
        out_specs=[blk, blk],
        out_shape=[jax.ShapeDtypeStruct((t_rows, d), F32), jax.ShapeDtypeStruct((t_rows, d), F32)],
        scratch_shapes=[pltpu.VMEM((seq + SUBLANES, wb), F32), pltpu.VMEM((seq, wb), F32),
                        pltpu.VMEM((seq, wb), F32)],
        compiler_params=_cp("parallel", "parallel"))(proj, proj, conv_w, conv_b, w_r, b_r, w_i, b_i, lam)


def _mixer_a_bwd(proj, h, d_ya, conv_w, conv_b, w_r, b_r, w_i, b_i, lam, seq):
    _, t_rows, d = proj.shape
    n_seq, n_blk = t_rows // seq, d // RG_BLOCK_W
    wb = RG_BLOCK_W
    ch = _row_tile(seq, ROW_CHUNK)
    n_ch = seq // ch

    def body(xa_ref, ga_ref, h_ref, dya_ref, cw_ref, cb_ref, wr_ref, br_ref, wi_ref, bi_ref, lam_ref,
             dp_ref, dwr_ref, dwi_ref, dbr_ref, dbi_ref, dlam_ref, dcw_ref, dcb_ref,
             xpad, hpad, a_s, e_pad, g_s, xc_s, r_s, i_s, dxc_pad):
        @pl.when(pl.program_id(1) == 0)
        def _():
            for ref in (dwr_ref, dwi_ref, dbr_ref, dbi_ref, dlam_ref, dcw_ref, dcb_ref):
                ref[...] = jnp.zeros_like(ref)

        zeros8 = jnp.zeros((SUBLANES, wb), F32)
        xpad[0:SUBLANES, :] = zeros8
        xpad[SUBLANES:, :] = xa_ref[...]
        hpad[0:SUBLANES, :] = zeros8
        hpad[SUBLANES:, :] = h_ref[...]
        e_pad[seq:, :] = zeros8
        dxc_pad[seq:, :] = zeros8
        lam_v = lam_ref[...]
        sp8 = RG_C * _softplus(-lam_v)

        def recompute(c, _):
            r0 = pl.multiple_of(c * ch, ch)
            rows = pl.ds(r0, ch)
            taps = _conv_taps(xpad[pl.ds(r0, ch + SUBLANES), :], ch)
            xc = cb_ref[...] + sum(cw_ref[pl.ds(j, 1), :] * taps[j] for j in range(CONV_W))
            r, i = _rg_gates(xc, wr_ref[...], br_ref[...], wi_ref[...], bi_ref[...], sp8)
            a = jnp.exp(-(r * sp8))
            gl, dgl = _gelu_and_grad(ga_ref[rows, :])
            dya = dya_ref[rows, :]
            g = dya * gl
            dp_ref[1, rows, :] = (dya * h_ref[rows, :] * dgl).astype(BF16)
            a_s[rows, :] = a
            e_pad[rows, :] = a * g
            g_s[rows, :] = g
            xc_s[rows, :] = xc
            r_s[rows, :] = r
            i_s[rows, :] = i
            return 0

        lax.fori_loop(0, n_ch, recompute, 0)
        _scan_rows(a_s, e_pad, e_pad, seq, wb, reverse=True)

        def grads(c, _):
            r0 = pl.multiple_of(c * ch, ch)
            rows = pl.ds(r0, ch)
            halo = pl.ds(r0, ch + SUBLANES)
            dh = g_s[rows, :] + _shift_rows(e_pad[halo, :], ch + SUBLANES - 1)[0:ch, :]
            h_prev = _shift_rows(hpad[halo, :], 1)[SUBLANES:, :]
            xc, r, i = xc_s[rows, :], r_s[rows, :], i_s[rows, :]
            log_a = -(r * sp8)
            a = jnp.exp(log_a)
            om = _one_minus_exp(2.0 * log_a)
            sq = jnp.sqrt(jnp.maximum(om, 0.0))
            t1 = dh * xc
            d_i = t1 * sq
            d_la = dh * h_prev * a + jnp.where(om > 0.0, -(t1 * i) * (1.0 - om) / sq, 0.0)
            dpr = -(d_la * sp8) * r * (1.0 - r)
            dpi = d_i * i * (1.0 - i)
            dprb, dpib, xb = dpr.astype(BF16), dpi.astype(BF16), xc.astype(BF16)
            dxc = dh * sq * i + _dot_nt(dprb, wr_ref[...]) + _dot_nt(dpib, wi_ref[...])
            dwr_ref[...] += _dot_tn(xb, dprb)
            dwi_ref[...] += _dot_tn(xb, dpib)
            dbr_ref[...] += jnp.sum(dpr, axis=0, keepdims=True)
            dbi_ref[...] += jnp.sum(dpi, axis=0, keepdims=True)
            dlam_ref[...] += jnp.sum(d_la * r, axis=0, keepdims=True) * (RG_C * _sigmoid(-lam_v))
            dcb_ref[...] += jnp.sum(dxc, axis=0, keepdims=True)
            taps = _conv_taps(xpad[halo, :], ch)
            for j in range(CONV_W):
                dcw_ref[pl.ds(j, 1), :] += jnp.sum(dxc * taps[j], axis=0, keepdims=True)
            dxc_pad[rows, :] = dxc
            return 0

        lax.fori_loop(0, n_ch, grads, 0)

        def conv_bwd(c, _):
            r0 = pl.multiple_of(c * ch, ch)
            de = dxc_pad[pl.ds(r0, ch + SUBLANES), :]
            dxa = sum(cw_ref[pl.ds(j, 1), :] * _shift_rows(de, ch + SUBLANES - (CONV_W - 1 - j))[0:ch, :]
                      for j in range(CONV_W))
            dp_ref[0, pl.ds(r0, ch), :] = dxa.astype(BF16)
            return 0

        lax.fori_loop(0, n_ch, conv_bwd, 0)

    seg = lambda k: pl.BlockSpec((None, seq, wb), lambda b, s, k=k: (k, s, b))
    blk = pl.BlockSpec((seq, wb), lambda b, s: (s, b))
    vec = pl.BlockSpec((1, wb), lambda b, s: (0, b))
    taps = pl.BlockSpec((CONV_W, wb), lambda b, s: (0, b))
    wsp = pl.BlockSpec((None, wb, wb), lambda b, s: (b, 0, 0))
    vec_shape = jax.ShapeDtypeStruct((1, d), F32)
    w_shape = jax.ShapeDtypeStruct((n_blk, wb, wb), F32)
    pad = pltpu.VMEM((seq + SUBLANES, wb), F32)
    full = pltpu.VMEM((seq, wb), F32)
    return pl.pallas_call(
        body, name="mixer_a_bwd", grid=(n_blk, n_seq),
        in_specs=[seg(0), seg(1), blk, blk, taps, vec, wsp, vec, wsp, vec, vec],
        out_specs=[pl.BlockSpec((2, seq, wb), lambda b, s: (0, s, b)), wsp, wsp, vec, vec, vec, taps, vec],
        out_shape=[jax.ShapeDtypeStruct((2, t_rows, d), BF16), w_shape, w_shape, vec_shape, vec_shape,
                   vec_shape, jax.ShapeDtypeStruct((CONV_W, d), F32), vec_shape],
        scratch_shapes=[pad, pad, full, pad, full, full, full, full, pad],
        compiler_params=_cp("parallel", "arbitrary"))(
            proj, proj, h, d_ya, conv_w, conv_b, w_r, b_r, w_i, b_i, lam)


def _hg_prepare(q_ref, z_ref, lb, rows):
    z = z_ref[rows, :]
    sig = _sigmoid(z)
    fg = lb + (1.0 - lb) * sig
    log_f = jnp.log(jnp.maximum(fg, F_MIN))
    key = (1.0 - lb) * _sigmoid(-z)
    qs, _ = _silu_and_grad(q_ref[rows, :])
    return qs, key, log_f, sig, fg


def _hg_pair_terms(g_ref, r0, gc, s):
    rid = lax.broadcasted_iota(jnp.int32, (HG_CHUNK, HG_DK), 0)
    gs = g_ref[pl.ds(r0 + s, 1), :]
    return jnp.where(rid >= s, jnp.exp(gc - gs), 0.0)


def _chunk_row_sums():
    row = lax.broadcasted_iota(jnp.int32, (HG_CHUNK, HG_CHUNK * HG_CHUNK), 0)
    col = lax.broadcasted_iota(jnp.int32, (HG_CHUNK, HG_CHUNK * HG_CHUNK), 1)
    lo = row * HG_CHUNK
    return jnp.where((col >= lo) & (col < lo + HG_CHUNK), 1.0, 0.0).astype(BF16)


def _for_chunks(n, unroll, *stages):
    unroll = min(unroll, n)
    assert n % unroll == 0

    def trip(i, _):
        chunks = [i * unroll + u for u in range(unroll)]
        carried = [stages[0](c) for c in chunks]
        for stage in stages[1:]:
            carried = [stage(c, x) for c, x in zip(chunks, carried)]
        return 0

    lax.fori_loop(0, n // unroll, trip, 0)


def _hg_state_terms(v_ref, k_ref, g_ref, states, n_chunks):
    def one(c):
        rows = pl.ds(pl.multiple_of(c * HG_CHUNK, HG_CHUNK), HG_CHUNK)
        gc = g_ref[rows, :]
        k_end = k_ref[rows, :] * jnp.exp(gc[HG_CHUNK - 1:HG_CHUNK, :] - gc)
        states[c] = _dot_tn(v_ref[rows, :].astype(BF16), k_end.astype(BF16))

    _for_chunks(n_chunks, 8, one)


def _hg_state_chain(states, g_ref, carry_ref, n_chunks, reverse):
    unroll = min(8, n_chunks)
    assert n_chunks % unroll == 0
    carry_ref[...] = jnp.zeros_like(carry_ref)

    def trip(i, _):
        st = carry_ref[...]
        for u in range(unroll):
            k = i * unroll + u
            c = n_chunks - 1 - k if reverse else k
            term = states[c]
            states[c] = st
            st = st * jnp.exp(g_ref[pl.ds(c * HG_CHUNK + HG_CHUNK - 1, 1), :]) + term
        carry_ref[...] = st
        return 0

    lax.fori_loop(0, n_chunks // unroll, trip, 0)


def _hgrn_fwd(proj, lower_bound, hg_gain, seq, carry=None):
    _, t_rows, d = proj.shape
    n_seq, n_head = t_rows // seq, d // HG_DK
    ch = _row_tile(seq, ROW_CHUNK)
    n_chunks = seq // HG_CHUNK

    def body(q_ref, z_ref, v_ref, lb_ref, gain_ref, o_ref, on_ref, qs_s, k_s, g_s, states, st_ref):
        lb = lb_ref[...]

        def prepare(c, _):
            rows = pl.ds(pl.multiple_of(c * ch, ch), ch)
            qs, key, log_f, _, _ = _hg_prepare(q_ref, z_ref, lb, rows)
            qs_s[rows, :] = qs
            k_s[rows, :] = key
            g_s[rows, :] = _seg_cumsum(log_f, HG_CHUNK)
            return 0

        lax.fori_loop(0, seq // ch, prepare, 0)
        _hg_state_terms(v_ref, k_s, g_s, states, n_chunks)
        _hg_state_chain(states, g_s, st_ref, n_chunks, reverse=False)
        ones = jnp.ones((HG_DK, HG_DK), BF16)

        def issue(c):
            r0 = pl.multiple_of(c * HG_CHUNK, HG_CHUNK)
            rows = pl.ds(r0, HG_CHUNK)
            qc, gc = qs_s[rows, :], g_s[rows, :]
            o = _dot_nt((qc * jnp.exp(gc)).astype(BF16), states[c].astype(BF16))
            pairs = [(qc * _hg_pair_terms(g_s, r0, gc, s) * k_s[pl.ds(r0 + s, 1), :]).astype(BF16)
                     for s in range(HG_CHUNK)]
            score = _dot(jnp.concatenate(pairs, axis=0), ones)
            return o, score

        def combine(c, issued):
            o, score = issued
            r0 = pl.multiple_of(c * HG_CHUNK, HG_CHUNK)
            for s in range(HG_CHUNK):
                o = o + score[s * HG_CHUNK:(s + 1) * HG_CHUNK, :] * v_ref[pl.ds(r0 + s, 1), :]
            o_ref[pl.ds(r0, HG_CHUNK), :] = o

        _for_chunks(n_chunks, 8, issue, combine)

        def norm(c, _):
            rows = pl.ds(pl.multiple_of(c * ch, ch), ch)
            o = o_ref[rows, :]
            on_ref[rows, :] = o * _rstd(o) * gain_ref[...]
            return 0

        lax.fori_loop(0, seq // ch, norm, 0)

    seg = lambda k: pl.BlockSpec((None, seq, HG_DK), lambda s, h, k=k: (k, s, h))
    blk = pl.BlockSpec((seq, HG_DK), lambda s, h: (s, h))
    full = pltpu.VMEM((seq, HG_DK), F32)
    return _call_carrying(
        body, carry, name="hgrn_fwd", grid=(n_seq, n_head),
        in_specs=[seg(2), seg(3), seg(4), pl.BlockSpec((1, HG_DK), lambda s, h: (0, h)),
                  pl.BlockSpec((1, HG_DK), lambda s, h: (0, 0))],
        out_specs=[blk, blk],
        out_shape=[jax.ShapeDtypeStruct((t_rows, d), F32), jax.ShapeDtypeStruct((t_rows, d), F32)],
        scratch_shapes=[full, full, full, pltpu.VMEM((n_chunks, HG_DK, HG_DK), F32),
                        pltpu.VMEM((HG_DK, HG_DK), F32)],
        semantics=("parallel", "parallel"), args=(proj, proj, proj, lower_bound, hg_gain))


def _hgrn_bwd(proj, lower_bound, hg_gain, o, d_on, seq, carry=None):
    _, t_rows, d = proj.shape
    n_seq, n_head = t_rows // seq, d // HG_DK
    ch = _row_tile(seq, ROW_CHUNK)
    n_chunks = seq // HG_CHUNK
    cc = HG_CHUNK

    def body(q_ref, z_ref, v_ref, lb_ref, gain_ref, o_ref, don_ref, dp_ref, dlb_ref, dgain_ref,
             qs_s, k_s, g_s, do_s, dqs_s, dk_s, dlf_s, states, dstates, carry_ref):
        hh, ss = pl.program_id(0), pl.program_id(1)
        lb = lb_ref[...]

        @pl.when(ss == 0)
        def _():
            dlb_ref[...] = jnp.zeros_like(dlb_ref)

        @pl.when((ss == 0) & (hh == 0))
        def _():
            dgain_ref[...] = jnp.zeros_like(dgain_ref)

        def prepare(c, _):
            rows = pl.ds(pl.multiple_of(c * ch, ch), ch)
            qs, key, log_f, _, _ = _hg_prepare(q_ref, z_ref, lb, rows)
            qs_s[rows, :] = qs
            k_s[rows, :] = key
            g_s[rows, :] = _seg_cumsum(log_f, cc)
            do, dgain = _rms_bwd(don_ref[rows, :], o_ref[rows, :], gain_ref[...])
            do_s[rows, :] = do
            dgain_ref[...] += dgain
            return 0

        lax.fori_loop(0, seq // ch, prepare, 0)

        _hg_state_terms(v_ref, k_s, g_s, states, n_chunks)
        _hg_state_chain(states, g_s, carry_ref, n_chunks, reverse=False)

        def query_term(c):
            rows = pl.ds(pl.multiple_of(c * cc, cc), cc)
            q_in = qs_s[rows, :] * jnp.exp(g_s[rows, :])
            dstates[c] = _dot_tn(do_s[rows, :].astype(BF16), q_in.astype(BF16))

        _for_chunks(n_chunks, 8, query_term)
        _hg_state_chain(dstates, g_s, carry_ref, n_chunks, reverse=True)
        ones = jnp.ones((HG_DK, HG_DK), BF16)
        row_sums = _chunk_row_sums()

        def chunk_rows(c):
            r0 = pl.multiple_of(c * cc, cc)
            return r0, pl.ds(r0, cc)

        def through_state(c):
            r0, rows = chunk_rows(c)
            kc, gc, vc, doc = k_s[rows, :], g_s[rows, :], v_ref[rows, :], do_s[rows, :]
            st, dst = states[c], dstates[c]
            g_last = gc[cc - 1:cc, :]
            e_last, e_end = jnp.exp(g_last), jnp.exp(g_last - gc)
            dob, dstb = doc.astype(BF16), dst.astype(BF16)
            dqs = _dot(dob, st.astype(BF16))
            dk_state = _dot(vc.astype(BF16), dstb)
            dv = _dot_nt((kc * e_end).astype(BF16), dstb)
            cots = [(doc * v_ref[pl.ds(r0 + s, 1), :]).astype(BF16) for s in range(cc)]
            d_score = _dot(jnp.concatenate(cots, axis=0), ones)
            return dqs, dk_state, dv, d_score, e_last * jnp.sum(dst * st, axis=0, keepdims=True)

        def pair_terms(c, x):
            dqs, dk_state, dv, d_score, d_glast = x
            r0, rows = chunk_rows(c)
            qc, kc, gc = qs_s[rows, :], k_s[rows, :], g_s[rows, :]
            dqs = dqs * jnp.exp(gc)
            dk_state = dk_state * jnp.exp(gc[cc - 1:cc, :] - gc)
            d_glast = d_glast + jnp.sum(kc * dk_state, axis=0, keepdims=True)
            pairs, dk_terms = [], []
            for s in range(cc):
                decay = _hg_pair_terms(g_s, r0, gc, s)
                ks = k_s[pl.ds(r0 + s, 1), :]
                da_decay = d_score[s * cc:(s + 1) * cc, :] * decay
                pairs.append((qc * decay * ks).astype(BF16))
                dk_terms.append((da_decay * qc).astype(BF16))
                dqs = dqs + da_decay * ks
            score = _dot(jnp.concatenate(pairs, axis=0), ones)
            dk = dk_state + _dot(row_sums, jnp.concatenate(dk_terms, axis=0))
            return dqs, dk, dv, score, d_glast

        def value_terms(c, x):
            dqs, dk, dv, score, d_glast = x
            _, rows = chunk_rows(c)
            doc = do_s[rows, :]
            dv_terms = [(score[s * cc:(s + 1) * cc, :] * doc).astype(BF16) for s in range(cc)]
            return dqs, dk, dv + _dot(row_sums, jnp.concatenate(dv_terms, axis=0)), d_glast

        def store(c, x):
            dqs, dk, dv, d_glast = x
            _, rows = chunk_rows(c)
            d_g = qs_s[rows, :] * dqs - k_s[rows, :] * dk
            dlf_s[rows, :] = _seg_cumsum(d_g, cc, reverse=True) + d_glast
            dqs_s[rows, :] = dqs
            dk_s[rows, :] = dk
            dp_ref[2, rows, :] = dv.astype(BF16)

        _for_chunks(n_chunks, 8, through_state, pair_terms, value_terms, store)

        def finish(c, _):
            rows = pl.ds(pl.multiple_of(c * ch, ch), ch)
            z = z_ref[rows, :]
            sig = _sigmoid(z)
            nsig = _sigmoid(-z)
            fg = lb + (1.0 - lb) * sig
            _, dsilu = _silu_and_grad(q_ref[rows, :])
            dp_ref[0, rows, :] = (dqs_s[rows, :] * dsilu).astype(BF16)
            dfg = jnp.where(fg > F_MIN, dlf_s[rows, :] / fg, 0.0)
            dk = dk_s[rows, :]
            dp_ref[1, rows, :] = ((dfg - dk) * (1.0 - lb) * sig * nsig).astype(BF16)
            dlb_ref[...] += jnp.sum((dfg - dk) * nsig, axis=0, keepdims=True)
            return 0

        lax.fori_loop(0, seq // ch, finish, 0)

    seg = lambda k: pl.BlockSpec((None, seq, HG_DK), lambda h, s, k=k: (k, s, h))
    blk = pl.BlockSpec((seq, HG_DK), lambda h, s: (s, h))
    full = pltpu.VMEM((seq, HG_DK), F32)
    return _call_carrying(
        body, carry, name="hgrn_bwd", grid=(n_head, n_seq),
        in_specs=[seg(2), seg(3), seg(4), pl.BlockSpec((1, HG_DK), lambda h, s: (0, h)),
                  pl.BlockSpec((1, HG_DK), lambda h, s: (0, 0)), blk, blk],
        out_specs=[pl.BlockSpec((3, seq, HG_DK), lambda h, s: (0, s, h)),
                   pl.BlockSpec((1, HG_DK), lambda h, s: (0, h)),
                   pl.BlockSpec((1, HG_DK), lambda h, s: (0, 0))],
        out_shape=[jax.ShapeDtypeStruct((3, t_rows, d), BF16), jax.ShapeDtypeStruct((1, d), F32),
                   jax.ShapeDtypeStruct((1, HG_DK), F32)],
        scratch_shapes=[full, full, full, full, full, full, full,
                        pltpu.VMEM((n_chunks, HG_DK, HG_DK), F32), pltpu.VMEM((n_chunks, HG_DK, HG_DK), F32),
                        pltpu.VMEM((HG_DK, HG_DK), F32)],
        semantics=("arbitrary", "arbitrary"), args=(proj, proj, proj, lower_bound, hg_gain, o, d_on))


def _mesh_place():
    x, y, c = lax.axis_index("x"), lax.axis_index("y"), lax.axis_index("c")
    return x, y, c


def _peer(place, k):
    x, y, c = place
    px = 1 - x if k & 4 else x
    py = 1 - y if k & 2 else y
    pc = 1 - c if k & 1 else c
    return (px, py, pc), 4 * px + 2 * py + pc


class _Exchange:
    def __init__(self, srcs, gather):
        self.n = len(srcs)
        self.gather = gather
        self.out_shape = [jax.ShapeDtypeStruct((N_DEV,) + tuple(s.shape if gather else s.shape[1:]), s.dtype)
                          for s in srcs]
        self.scratch = [pltpu.SemaphoreType.DMA((self.n * (N_DEV - 1),)),
                        pltpu.SemaphoreType.DMA((self.n * (N_DEV - 1),)),
                        pltpu.SemaphoreType.DMA((self.n,))]

    def _copies(self, src_refs, out_refs, sems):
        send_sems, recv_sems, local_sems = sems
        place = _mesh_place()
        me = 4 * place[0] + 2 * place[1] + place[2]
        local, sends, recvs = [], [], []
        for a, (src, out) in enumerate(zip(src_refs, out_refs)):
            outgoing = (lambda idx, src=src: src) if self.gather else (lambda idx, src=src: src.at[idx])
            local.append(pltpu.make_async_copy(outgoing(me), out.at[me], local_sems.at[a]))
            for k in range(1, N_DEV):
                peer, peer_idx = _peer(place, k)
                sem = a * (N_DEV - 1) + k - 1
                sends.append(pltpu.make_async_remote_copy(
                    src_ref=outgoing(peer_idx), dst_ref=out.at[me], send_sem=send_sems.at[sem],
                    recv_sem=recv_sems.at[sem], device_id=peer, device_id_type=MESH_ID))
                recvs.append(pltpu.make_async_remote_copy(
                    src_ref=outgoing(peer_idx), dst_ref=out.at[peer_idx], send_sem=send_sems.at[sem],
                    recv_sem=recv_sems.at[sem], device_id=peer, device_id_type=MESH_ID))
        return local, sends, recvs

    def start(self, src_refs, out_refs, sems):
        local, sends, _ = self._copies(src_refs, out_refs, sems)
        for cp in local + sends:
            cp.start()

    def wait(self, src_refs, out_refs, sems):
        local, sends, recvs = self._copies(src_refs, out_refs, sems)
        for cp in recvs:
            cp.wait_recv()
        for cp in sends:
            cp.wait_send()
        for cp in local:
            cp.wait()


def _call_carrying(body, carry, *, name, grid, in_specs, out_specs, out_shape, scratch_shapes, semantics, args):
    if carry is None:
        outs = pl.pallas_call(body, name=name, grid=grid, in_specs=in_specs, out_specs=out_specs,
                              out_shape=out_shape, scratch_shapes=scratch_shapes,
                              compiler_params=_cp(*semantics))(*args)
        return outs, []
    srcs, gather = carry
    ex = _Exchange(srcs, gather)
    n, n_in, n_out, n_scr = ex.n, len(in_specs), len(out_specs), len(scratch_shapes)

    def wrapped(*refs):
        ins, refs = refs[:n_in], refs[n_in:]
        src_refs, refs = refs[:n], refs[n:]
        outs, refs = refs[:n_out], refs[n_out:]
        dst_refs, refs = refs[:n], refs[n:]
        scratch, sems = refs[:n_scr], refs[n_scr:]
        first, last = None, None
        for axis, size in enumerate(grid):
            i = pl.program_id(axis)
            first = (i == 0) if first is None else first & (i == 0)
            last = (i == size - 1) if last is None else last & (i == size - 1)

        @pl.when(first)
        def _():
            ex.start(src_refs, dst_refs, sems)

        body(*ins, *outs, *scratch)

        @pl.when(last)
        def _():
            ex.wait(src_refs, dst_refs, sems)

    any_space = pl.BlockSpec(memory_space=pl.ANY)
    res = pl.pallas_call(
        wrapped, name=name + "_carrying", grid=grid, in_specs=list(in_specs) + [any_space] * n,
        out_specs=list(out_specs) + [any_space] * n, out_shape=list(out_shape) + ex.out_shape,
        scratch_shapes=list(scratch_shapes) + ex.scratch,
        compiler_params=_cp(*(["arbitrary"] * len(grid))))(*args, *srcs)
    return res[:n_out], res[n_out:]


def _exchange(srcs, name, gather):
    ex = _Exchange(srcs, gather)
    n = ex.n

    def body(*refs):
        src_refs, out_refs, sems = refs[:n], refs[n:2 * n], refs[2 * n:]
        ex.start(src_refs, out_refs, sems)
        ex.wait(src_refs, out_refs, sems)

    any_space = pl.BlockSpec(memory_space=pl.ANY)
    return pl.pallas_call(
        body, name=name, in_specs=[any_space] * n, out_specs=[any_space] * n,
        out_shape=ex.out_shape, scratch_shapes=ex.scratch)(*srcs)


def _reduce_adamw(parts, w, m, v, name):
    rows, cols = w.shape
    tr = _row_tile(rows, 128)
    c1 = np.float32(1.0 - ADAM_B1 ** ADAM_STEP)
    c2 = np.float32(1.0 - ADAM_B2 ** ADAM_STEP)

    def body(p_ref, w_ref, m_ref, v_ref, g_ref, d_ref, nm_ref, nv_ref):
        g = p_ref[0].astype(F32)
        for k in range(1, N_DEV):
            g = g + p_ref[k].astype(F32)
        nm = ADAM_B1 * m_ref[...] + (1.0 - ADAM_B1) * g
        nv = ADAM_B2 * v_ref[...] + (1.0 - ADAM_B2) * (g * g)
        g_ref[...] = g
        nm_ref[...] = nm
        nv_ref[...] = nv
        d_ref[...] = -ADAM_LR * ((nm / c1) / (jnp.sqrt(nv / c2) + ADAM_EPS) + ADAM_WD * w_ref[...])

    blk = pl.BlockSpec((tr, cols), lambda i: (i, 0))
    shp = jax.ShapeDtypeStruct((rows, cols), F32)
    return pl.pallas_call(
        body, name=name, grid=(rows // tr,),
        in_specs=[pl.BlockSpec((N_DEV, tr, cols), lambda i: (0, i, 0)), blk, blk, blk],
        out_specs=[blk, blk, blk, blk], out_shape=[shp, shp, shp, shp],
        compiler_params=_cp("parallel"))(parts, w, m, v)


def _pack(arrays, lead=0):
    parts = []
    for a in arrays:
        f = a.reshape(a.shape[:lead] + (-1, LANES))
        pad = -f.shape[lead] % PACK_ROWS
        if pad:
            f = jnp.pad(f, [(0, 0)] * lead + [(0, pad), (0, 0)])
        parts.append(f)
    return jnp.concatenate(parts, axis=lead)


def _unpack(buf, shapes, lead=0):
    out, r = [], 0
    for shp in shapes:
        n = int(np.prod(shp)) // LANES
        part = lax.slice_in_dim(buf, r, r + n, axis=lead)
        out.append(part.reshape(buf.shape[:lead] + tuple(shp)))
        r += n + (-n % PACK_ROWS)
    return out


REPLICATED = ("lb_logits", "norm_mix", "conv_b", "b_r", "b_i", "lam", "hg_norm", "norm_mlp", "norm_final")
SMALL_SHARDED = ("conv_w", "w_r", "w_i")
LARGE_SHARDED = ("w_in", "w_out", "w_up", "w_down")
WEIGHTS = ("lb_logits", "norm_mix", "w_in", "conv_w", "conv_b", "w_r", "b_r", "w_i", "b_i", "lam", "hg_norm",
           "w_out", "norm_mlp", "w_up", "w_down", "norm_final")


def _matmul_weight_shards(p):
    depth, d, _ = p["w_in"].shape
    f8 = p["w_up"].shape[2]
    first = p["w_in"].astype(BF16)
    rest = jnp.concatenate([a.astype(BF16) for a in (p["w_out"], p["w_up"].reshape(depth, f8, d), p["w_down"])],
                           axis=1)
    return [first[l] for l in range(depth)], [rest[l] for l in range(depth)]


def _split_rest(g, f8):
    d = g.shape[2]
    d8 = d // N_DEV
    return dict(w_out=g[:, :d8].reshape(d, d),
                w_up=g[:, d8:d8 + f8].reshape(N_DEV, d, f8),
                w_down=g[:, d8 + f8:])


def _unpack_mixer_weights(small, p):
    depth, d, _ = p["w_in"].shape
    n_blk = d // RG_BLOCK_W
    conv_w, w_r, w_i = _unpack(small, [p["conv_w"].shape, p["w_r"].shape, p["w_i"].shape], lead=1)
    conv_w = conv_w.transpose(1, 2, 0, 3).reshape(depth, CONV_W, d)
    w_r = w_r.transpose(1, 2, 0, 3, 4).reshape(depth, n_blk, RG_BLOCK_W, RG_BLOCK_W).astype(BF16)
    w_i = w_i.transpose(1, 2, 0, 3, 4).reshape(depth, n_blk, RG_BLOCK_W, RG_BLOCK_W).astype(BF16)
    return conv_w, w_r, w_i


def _local_step(x, target, p):
    bl, seq, d = x.shape
    depth, f8 = p["w_in"].shape[0], p["w_up"].shape[2]
    t_rows = bl * seq
    row = lambda a, l: a[l:l + 1]
    lbs = _lower_bounds_fwd(p["lb_logits"])
    shard_in, shard_rest = _matmul_weight_shards(p)
    w_in = _exchange([shard_in[0]], "gather_w_in", gather=True)[0]
    cur = x.reshape(t_rows, d)
    saved, layers = [], []
    for l in range(depth):
        if l == 0:
            (proj, h), small = _inproj_fwd(cur, row(p["norm_mix"], l), w_in,
                                           carry=([_pack([p["conv_w"], p["w_r"], p["w_i"]])], True))
            conv_w, w_r, w_i = _unpack_mixer_weights(small[0], p)
        else:
            (proj, h), _ = _inproj_fwd(cur, row(p["norm_mix"], l), w_in)
        w = dict(w_in=w_in, conv_w=conv_w[l], w_r=w_r[l], w_i=w_i[l])
        hs, y_a = _mixer_a_fwd(proj, w["conv_w"], row(p["conv_b"], l), w["w_r"], row(p["b_r"], l), w["w_i"],
                               row(p["b_i"], l), row(p["lam"], l), seq)
        (o, o_n), got = _hgrn_fwd(proj, row(lbs, l), row(p["hg_norm"], l), seq,
                                  carry=([shard_rest[l]] + ([shard_in[l + 1]] if l + 1 < depth else []), True))
        w.update(_split_rest(got[0], f8))
        w_in = got[1] if l + 1 < depth else None
        layers.append(w)
        x_mid, y = _merge_out_fwd(proj, y_a, o_n, cur, w["w_out"])
        x_out, u, h2 = _mlp_fwd(x_mid, row(p["norm_mlp"], l), w["w_up"], w["w_down"])
        saved.append(dict(x_in=cur, proj=proj, h=h, hs=hs, y_a=y_a, o=o, o_n=o_n, x_mid=x_mid, y=y, u=u, h2=h2))
        cur = x_out
    loss8, dx, dxb, g_norm_final = _loss_head(cur, p["norm_final"].reshape(1, d), target.reshape(t_rows, d))
    small = ("norm_mix", "conv_w", "conv_b", "w_r", "b_r", "w_i", "b_i", "lam", "hg_norm", "norm_mlp")
    g = {k: [None] * depth for k in small}
    d_lbs, received = [None] * depth, [None] * depth
    g_w_in = None
    for l in reversed(range(depth)):
        s, w = saved[l], layers[l]
        dx_mid, dx_mid_b, du, act, g["norm_mlp"][l] = _mlp_bwd(dx, dxb, s["u"], s["x_mid"], row(p["norm_mlp"], l),
                                                               w["w_up"], w["w_down"])
        g_w_down = _wgrad(act, dxb[None], "wgrad_down")
        g_w_up = _wgrad(s["h2"][None], du, "wgrad_up")
        d_ya, d_on, dp_c = _outproj_bwd(dx_mid_b, w["w_out"], s["proj"], s["y_a"], s["o_n"])
        g_w_out = _wgrad(s["y"][None], dx_mid_b[None], "wgrad_out").reshape(N_DEV, d // N_DEV, d)
        (dp_b, d_lbs[l], g["hg_norm"][l]), got = _hgrn_bwd(
            s["proj"], row(lbs, l), row(p["hg_norm"], l), s["o"], d_on, seq,
            carry=([g_w_out, g_w_up, g_w_down] + ([g_w_in] if g_w_in is not None else []), False))
        received[l] = [None] + list(got[:3])
        if g_w_in is not None:
            received[l + 1][0] = got[3]
        (dp_a, g["w_r"][l], g["w_i"][l], g["b_r"][l], g["b_i"][l], g["lam"][l], g["conv_w"][l],
         g["conv_b"][l]) = _mixer_a_bwd(s["proj"], s["hs"], d_ya, w["conv_w"], row(p["conv_b"], l), w["w_r"],
                                        row(p["b_r"], l), w["w_i"], row(p["b_i"], l), row(p["lam"], l), seq)
        dx, dxb, g["norm_mix"][l] = _inproj_bwd(dx_mid, dp_a, dp_b, dp_c, w["w_in"], s["x_in"],
                                                row(p["norm_mix"], l))
        hb = s["h"][None]
        g_w_in = jnp.concatenate([_wgrad(hb, dp_a, "wgrad_in_pair"), _wgrad(hb, dp_b, "wgrad_in_triple"),
                                  _wgrad(hb, dp_c, "wgrad_in_triple")], axis=0)
    received[0][0] = _exchange([g_w_in], "scatter_grad_w_in", gather=False)[0]
    grads = {k: jnp.stack(v) for k, v in g.items()}
    for k in ("norm_mix", "conv_b", "b_r", "b_i", "lam", "hg_norm", "norm_mlp"):
        grads[k] = grads[k][:, 0]
    grads["lb_logits"] = _lower_bounds_bwd(p["lb_logits"], jnp.concatenate(d_lbs, axis=0))
    grads["norm_final"] = g_norm_final[0]
    return loss8[0, 0], dx.reshape(bl, seq, d), grads, received


def _update(p, mom1, mom2, grads, received):
    depth, d, _ = p["w_in"].shape
    d8 = d // N_DEV
    n_blk, rb = d // RG_BLOCK_W, RG_BLOCK_W // N_DEV
    out = {}

    for i, k in enumerate(LARGE_SHARDED):
        shp = p[k].shape
        flat = lambda a: a.reshape(shp[0] * shp[1], shp[2])
        parts = jnp.concatenate([received[l][i] for l in range(depth)], axis=1)
        res = _reduce_adamw(parts, flat(p[k]), flat(mom1[k]), flat(mom2[k]), "adamw_" + k)
        out[k] = [r.reshape(shp) for r in res]

    small_to_dev = [
        grads["conv_w"].reshape(depth, CONV_W, N_DEV, d8).transpose(2, 0, 1, 3),
        grads["w_r"].reshape(depth, n_blk, N_DEV, rb, RG_BLOCK_W).transpose(2, 0, 1, 3, 4),
        grads["w_i"].reshape(depth, n_blk, N_DEV, rb, RG_BLOCK_W).transpose(2, 0, 1, 3, 4)]
    parts = _exchange([_pack(small_to_dev, lead=1)], "scatter_grad_mixer", gather=False)[0]
    res = _reduce_adamw(parts, *[_pack([src[k] for k in SMALL_SHARDED]) for src in (p, mom1, mom2)],
                        "adamw_mixer")
    shapes = [p[k].shape for k in SMALL_SHARDED]
    for i, vals in enumerate(zip(*[_unpack(r, shapes) for r in res])):
        out[SMALL_SHARDED[i]] = list(vals)

    parts = _exchange([_pack([grads[k] for k in REPLICATED])], "gather_grad_replicated", gather=True)[0]
    res = _reduce_adamw(parts, *[_pack([src[k] for k in REPLICATED]) for src in (p, mom1, mom2)],
                        "adamw_replicated")
    shapes = [p[k].shape for k in REPLICATED]
    for i, vals in enumerate(zip(*[_unpack(r, shapes) for r in res])):
        out[REPLICATED[i]] = list(vals)

    return tuple(out[k][i] for i in range(4) for k in WEIGHTS)


def kernel(x, lb_logits, norm_mix, w_in, conv_w, conv_b, w_r, b_r, w_i, b_i, lam, hg_norm, w_out, norm_mlp, w_up, w_down, norm_final, loss_target, m_lb_logits, m_norm_mix, m_w_in, m_conv_w, m_conv_b, m_w_r, m_b_r, m_w_i, m_b_i, m_lam, m_hg_norm, m_w_out, m_norm_mlp, m_w_up, m_w_down, m_norm_final, v_lb_logits, v_norm_mix, v_w_in, v_conv_w, v_conv_b, v_w_r, v_b_r, v_w_i, v_b_i, v_lam, v_hg_norm, v_w_out, v_norm_mlp, v_w_up, v_w_down, v_norm_final):
    p = dict(lb_logits=lb_logits, norm_mix=norm_mix, w_in=w_in, conv_w=conv_w, conv_b=conv_b, w_r=w_r, b_r=b_r,
             w_i=w_i, b_i=b_i, lam=lam, hg_norm=hg_norm, w_out=w_out, norm_mlp=norm_mlp, w_up=w_up,
             w_down=w_down, norm_final=norm_final)
    mom1 = dict(lb_logits=m_lb_logits, norm_mix=m_norm_mix, w_in=m_w_in, conv_w=m_conv_w, conv_b=m_conv_b,
                w_r=m_w_r, b_r=m_b_r, w_i=m_w_i, b_i=m_b_i, lam=m_lam, hg_norm=m_hg_norm, w_out=m_w_out,
                norm_mlp=m_norm_mlp, w_up=m_w_up, w_down=m_w_down, norm_final=m_norm_final)
    mom2 = dict(lb_logits=v_lb_logits, norm_mix=v_norm_mix, w_in=v_w_in, conv_w=v_conv_w, conv_b=v_conv_b,
                w_r=v_w_r, b_r=v_b_r, w_i=v_w_i, b_i=v_b_i, lam=v_lam, hg_norm=v_hg_norm, w_out=v_w_out,
                norm_mlp=v_norm_mlp, w_up=v_w_up, w_down=v_w_down, norm_final=v_norm_final)
    loss, grad_x, grads, received = _local_step(x, loss_target, p)
    loss = lax.psum(loss, ("x", "y", "c"))
    return (loss, grad_x) + _update(p, mom1, mom2, grads, received)
```

```python
import numpy as np

import jax
import jax.numpy as jnp
from jax import lax
from jax.experimental import pallas as pl
from jax.experimental.pallas import tpu as pltpu

F32 = jnp.float32
BF16 = jnp.bfloat16
MESH_ID = pl.DeviceIdType.MESH

N_DEV = 8
NORM_EPS = 1e-6
RG_C = 8.0
RG_BLOCK_W = 256
CONV_W = 4
HG_DK = 128
F_MIN = 1e-30
HG_CHUNK = 16
SUBLANES = 8
LANES = 128
PACK_ROWS = 16
ROW_CHUNK = 256
ROW_TILE_WEIGHT_STREAM = 1024
WGRAD_TOKEN_TILE = 2048
VMEM_LIMIT_V7X = 56 * 1024 * 1024

ADAM_LR = 0.001
ADAM_B1 = 0.9
ADAM_B2 = 0.999
ADAM_EPS = 1e-08
ADAM_WD = 0.01
ADAM_STEP = 10

GELU_C = 0.7978845608028654
GELU_K = 0.044715


def _cp(*sem):
    return pltpu.CompilerParams(dimension_semantics=sem, vmem_limit_bytes=VMEM_LIMIT_V7X)


def _row_tile(n, cap):
    if n <= cap:
        return n
    t = cap - cap % 16
    while n % t:
        t -= 16
    return t


def _dot(a, b):
    return jnp.dot(a, b, preferred_element_type=F32)


def _dot_nt(a, b):
    return lax.dot_general(a, b, (((1,), (1,)), ((), ())), preferred_element_type=F32)


def _dot_tn(a, b):
    return lax.dot_general(a, b, (((0,), (0,)), ((), ())), preferred_element_type=F32)


def _sigmoid(x):
    return jax.nn.sigmoid(x)


def _log1p_pos(y):
    return jnp.where(y < 0.01, y * (1.0 - y * (0.5 - y * (1.0 / 3.0))), jnp.log(1.0 + y))


def _softplus(x):
    return jnp.maximum(x, 0.0) + _log1p_pos(jnp.exp(-jnp.abs(x)))


def _one_minus_exp(x):
    series = -x * (1.0 + x * 0.5 * (1.0 + x * (1.0 / 3.0) * (1.0 + x * 0.25 * (1.0 + x * 0.2))))
    return jnp.where(x > -0.1, series, 1.0 - jnp.exp(x))


def _gelu_and_grad(x):
    x2 = x * x
    t = jnp.tanh(GELU_C * x * (1.0 + GELU_K * x2))
    g = 0.5 * x * (1.0 + t)
    dg = 0.5 * (1.0 + t) + 0.5 * x * (1.0 - t * t) * GELU_C * (1.0 + 3.0 * GELU_K * x2)
    return g, dg


def _silu_and_grad(x):
    s = _sigmoid(x)
    return x * s, s * (1.0 + x * (1.0 - s))


def _rstd(x):
    return lax.rsqrt(jnp.mean(x * x, axis=-1, keepdims=True) + NORM_EPS)


def _rms_bwd(dh, x, g):
    rstd = _rstd(x)
    xh = x * rstd
    dxh = dh * g
    dx = rstd * (dxh - xh * jnp.mean(dxh * xh, axis=-1, keepdims=True))
    return dx, jnp.sum(dh * xh, axis=0, keepdims=True)


def _shift_rows(x, k):
    n = x.shape[0]
    k = k % n
    return x if k == 0 else pltpu.roll(x, k, axis=0)


def _seg_cumsum(x, seg, reverse=False):
    n = x.shape[0]
    rid = lax.broadcasted_iota(jnp.int32, x.shape, 0) & (seg - 1)
    d = 1
    while d < seg:
        if reverse:
            x = jnp.where(rid < seg - d, x + _shift_rows(x, n - d), x)
        else:
            x = jnp.where(rid >= d, x + _shift_rows(x, d), x)
        d *= 2
    return x


def _scan_rows(a_ref, b_ref, out_ref, n_rows, width, reverse):
    rid = lax.broadcasted_iota(jnp.int32, (SUBLANES, width), 0)
    n_groups = n_rows // SUBLANES

    def group(i, carry):
        g = n_groups - 1 - i if reverse else i
        r0 = pl.multiple_of(g * SUBLANES, SUBLANES)
        a = a_ref[pl.ds(r0, SUBLANES), :]
        b = b_ref[pl.ds(r0, SUBLANES), :]
        for d in (1, 2, 4):
            if reverse:
                keep = rid < SUBLANES - d
                a_sh, b_sh = _shift_rows(a, SUBLANES - d), _shift_rows(b, SUBLANES - d)
            else:
                keep = rid >= d
                a_sh, b_sh = _shift_rows(a, d), _shift_rows(b, d)
            b = jnp.where(keep, a * b_sh + b, b)
            a = jnp.where(keep, a * a_sh, a)
        out = a * carry + b
        out_ref[pl.ds(r0, SUBLANES), :] = out
        edge = out[0:1, :] if reverse else out[SUBLANES - 1:SUBLANES, :]
        return jnp.broadcast_to(edge, (SUBLANES, width))

    lax.fori_loop(0, n_groups, group, jnp.zeros((SUBLANES, width), F32), unroll=4)


def _lb_softmax_rows(x_ref, depth):
    rows = [x_ref[pl.ds(l, 1), :] for l in range(depth)]
    top = rows[0]
    for r in rows[1:]:
        top = jnp.maximum(top, r)
    e = [jnp.exp(r - top) for r in rows]
    tot = e[0]
    for r in e[1:]:
        tot = tot + r
    return [r / tot for r in e]


def _lower_bounds_fwd(lb_logits):
    depth, d = lb_logits.shape

    def body(x_ref, o_ref):
        sm = _lb_softmax_rows(x_ref, depth)
        cum = jnp.zeros((1, d), F32)
        for l in range(depth):
            cum = cum + sm[l]
            o_ref[pl.ds(l, 1), :] = jnp.clip(cum - sm[0], 0.0, 1.0)

    return pl.pallas_call(body, name="lower_bounds_fwd",
                          out_shape=jax.ShapeDtypeStruct((depth, d), F32))(lb_logits)


def _lower_bounds_bwd(lb_logits, d_lbs):
    depth, d = lb_logits.shape

    def body(x_ref, g_ref, o_ref):
        sm = _lb_softmax_rows(x_ref, depth)
        cum = jnp.zeros((1, d), F32)
        d_cum = []
        for l in range(depth):
            cum = cum + sm[l]
            v = cum - sm[0]
            d_cum.append(jnp.where((v > 0.0) & (v < 1.0), g_ref[pl.ds(l, 1), :], 0.0))
        d_sm = []
        tail = jnp.zeros((1, d), F32)
        for l in reversed(range(depth)):
            tail = tail + d_cum[l]
            d_sm.append(tail)
        d_sm = d_sm[::-1]
        d_sm[0] = d_sm[0] - tail
        inner = jnp.zeros((1, d), F32)
        for l in range(depth):
            inner = inner + sm[l] * d_sm[l]
        for l in range(depth):
            o_ref[pl.ds(l, 1), :] = sm[l] * (d_sm[l] - inner)

    return pl.pallas_call(body, name="lower_bounds_bwd",
                          out_shape=jax.ShapeDtypeStruct((depth, d), F32))(lb_logits, d_lbs)


def _inproj_fwd(x, gain, w_seg, carry=None):
    t_rows, d = x.shape
    tm = _row_tile(t_rows, ROW_TILE_WEIGHT_STREAM)

    def body(x_ref, g_ref, w_ref, proj_ref, h_ref):
        @pl.when(pl.program_id(1) == 0)
        def _():
            xv = x_ref[...]
            h_ref[...] = (xv * _rstd(xv) * g_ref[...]).astype(BF16)

        proj_ref[...] = _dot(h_ref[...], w_ref[...])

    return _call_carrying(
        body, carry, name="inproj_fwd", grid=(t_rows // tm, N_DEV),
        in_specs=[pl.BlockSpec((tm, d), lambda i, j: (i, 0)),
                  pl.BlockSpec((1, d), lambda i, j: (0, 0)),
                  pl.BlockSpec((None, d, d), lambda i, j: (j, 0, 0))],
        out_specs=[pl.BlockSpec((None, tm, d), lambda i, j: (j, i, 0)),
                   pl.BlockSpec((tm, d), lambda i, j: (i, 0))],
        out_shape=[jax.ShapeDtypeStruct((N_DEV, t_rows, d), F32),
                   jax.ShapeDtypeStruct((t_rows, d), BF16)],
        scratch_shapes=[], semantics=("parallel", "arbitrary"), args=(x, gain, w_seg))


def _merge_out_fwd(proj, y_a, o_n, x, w_out):
    t_rows, d = x.shape
    tm = _row_tile(t_rows, 256)

    def body(g_ref, ma_ref, mb_ref, ya_ref, on_ref, x_ref, w_ref, xmid_ref, y_ref):
        g = g_ref[...]
        y = _sigmoid(ma_ref[...]) * ya_ref[...] + _sigmoid(mb_ref[...]) * (on_ref[...] * (g * _sigmoid(g)))
        yb = y.astype(BF16)
        y_ref[...] = yb
        xmid_ref[...] = x_ref[...] + _dot(yb, w_ref[...])

    seg = lambda k: pl.BlockSpec((None, tm, d), lambda i, k=k: (k, i, 0))
    row = pl.BlockSpec((tm, d), lambda i: (i, 0))
    return pl.pallas_call(
        body, name="merge_out_fwd", grid=(t_rows // tm,),
        in_specs=[seg(5), seg(6), seg(7), row, row, row, pl.BlockSpec((d, d), lambda i: (0, 0))],
        out_specs=[row, row],
        out_shape=[jax.ShapeDtypeStruct((t_rows, d), F32), jax.ShapeDtypeStruct((t_rows, d), BF16)],
        compiler_params=_cp("parallel"))(proj, proj, proj, y_a, o_n, x, w_out)


def _mlp_fwd(x_mid, gain, w_up, w_down):
    t_rows, d = x_mid.shape
    f8 = w_up.shape[2]
    tm = _row_tile(t_rows, ROW_TILE_WEIGHT_STREAM)

    def body(x_ref, g_ref, wu_ref, wd_ref, out_ref, u_ref, h_ref):
        @pl.when(pl.program_id(1) == 0)
        def _():
            xv = x_ref[...]
            h_ref[...] = (xv * _rstd(xv) * g_ref[...]).astype(BF16)
            out_ref[...] = xv

        u = _dot(h_ref[...], wu_ref[...])
        u_ref[...] = u
        r = jnp.maximum(u, 0.0)
        out_ref[...] += _dot((r * r).astype(BF16), wd_ref[...])

    row = pl.BlockSpec((tm, d), lambda i, j: (i, 0))
    return pl.pallas_call(
        body, name="mlp_fwd", grid=(t_rows // tm, N_DEV),
        in_specs=[row, pl.BlockSpec((1, d), lambda i, j: (0, 0)),
                  pl.BlockSpec((None, d, f8), lambda i, j: (j, 0, 0)),
                  pl.BlockSpec((None, f8, d), lambda i, j: (j, 0, 0))],
        out_specs=[row, pl.BlockSpec((None, tm, f8), lambda i, j: (j, i, 0)), row],
        out_shape=[jax.ShapeDtypeStruct((t_rows, d), F32),
                   jax.ShapeDtypeStruct((N_DEV, t_rows, f8), F32),
                   jax.ShapeDtypeStruct((t_rows, d), BF16)],
        compiler_params=_cp("parallel", "arbitrary"))(x_mid, gain, w_up, w_down)


def _loss_head(x, gain, target):
    t_rows, d = x.shape
    tm = _row_tile(t_rows, 512)

    def body(x_ref, g_ref, t_ref, loss_ref, dx_ref, dxb_ref, dg_ref):
        @pl.when(pl.program_id(0) == 0)
        def _():
            loss_ref[...] = jnp.zeros_like(loss_ref)
            dg_ref[...] = jnp.zeros_like(dg_ref)

        xv = x_ref[...]
        g = g_ref[...]
        err = xv * _rstd(xv) * g - t_ref[...]
        loss_ref[...] += (0.5 / d) * jnp.sum(err * err)
        dx, dg = _rms_bwd(err * (1.0 / d), xv, g)
        dx_ref[...] = dx
        dxb_ref[...] = dx.astype(BF16)
        dg_ref[...] += dg

    row = pl.BlockSpec((tm, d), lambda i: (i, 0))
    vec = pl.BlockSpec((1, d), lambda i: (0, 0))
    return pl.pallas_call(
        body, name="loss_head", grid=(t_rows // tm,),
        in_specs=[row, vec, row],
        out_specs=[pl.BlockSpec((SUBLANES, LANES), lambda i: (0, 0)), row, row, vec],
        out_shape=[jax.ShapeDtypeStruct((SUBLANES, LANES), F32),
                   jax.ShapeDtypeStruct((t_rows, d), F32),
                   jax.ShapeDtypeStruct((t_rows, d), BF16),
                   jax.ShapeDtypeStruct((1, d), F32)],
        compiler_params=_cp("arbitrary"))(x, gain, target)


def _mlp_bwd(d_out, d_out_b, u, x_mid, gain, w_up, w_down):
    t_rows, d = x_mid.shape
    f8 = w_up.shape[2]
    tm = _row_tile(t_rows, 512)

    def body(do_ref, dob_ref, u_ref, x_ref, g_ref, wu_ref, wd_ref, dx_ref, dxb_ref, du_ref, act_ref, dg_ref,
             acc_ref):
        j = pl.program_id(1)

        @pl.when((pl.program_id(0) == 0) & (j == 0))
        def _():
            dg_ref[...] = jnp.zeros_like(dg_ref)

        @pl.when(j == 0)
        def _():
            acc_ref[...] = jnp.zeros_like(acc_ref)

        r = jnp.maximum(u_ref[...], 0.0)
        act_ref[...] = (r * r).astype(BF16)
        du = (_dot_nt(dob_ref[...], wd_ref[...]) * (2.0 * r)).astype(BF16)
        du_ref[...] = du
        acc_ref[...] += _dot_nt(du, wu_ref[...])

        @pl.when(j == N_DEV - 1)
        def _():
            dx, dg = _rms_bwd(acc_ref[...], x_ref[...], g_ref[...])
            dx = dx + do_ref[...]
            dx_ref[...] = dx
            dxb_ref[...] = dx.astype(BF16)
            dg_ref[...] += dg

    row = pl.BlockSpec((tm, d), lambda i, j: (i, 0))
    vec = pl.BlockSpec((1, d), lambda i, j: (0, 0))
    hid = pl.BlockSpec((None, tm, f8), lambda i, j: (j, i, 0))
    return pl.pallas_call(
        body, name="mlp_bwd", grid=(t_rows // tm, N_DEV),
        in_specs=[row, row, hid, row, vec,
                  pl.BlockSpec((None, d, f8), lambda i, j: (j, 0, 0)),
                  pl.BlockSpec((None, f8, d), lambda i, j: (j, 0, 0))],
        out_specs=[row, row, hid, hid, vec],
        out_shape=[jax.ShapeDtypeStruct((t_rows, d), F32),
                   jax.ShapeDtypeStruct((t_rows, d), BF16),
                   jax.ShapeDtypeStruct((N_DEV, t_rows, f8), BF16),
                   jax.ShapeDtypeStruct((N_DEV, t_rows, f8), BF16),
                   jax.ShapeDtypeStruct((1, d), F32)],
        scratch_shapes=[pltpu.VMEM((tm, d), F32)],
        compiler_params=_cp("arbitrary", "arbitrary"))(d_out, d_out_b, u, x_mid, gain, w_up, w_down)


def _outproj_bwd(dx_mid_b, w_out, proj, y_a, o_n):
    t_rows, d = y_a.shape
    tm = _row_tile(t_rows, 256)

    def body(dx_ref, w_ref, g_ref, ma_ref, mb_ref, ya_ref, on_ref, dya_ref, don_ref, dp_ref):
        dy = _dot_nt(dx_ref[...], w_ref[...])
        sa = _sigmoid(ma_ref[...])
        sb = _sigmoid(mb_ref[...])
        sg, dsg = _silu_and_grad(g_ref[...])
        ya = ya_ref[...]
        on = on_ref[...]
        dya_ref[...] = dy * sa
        t = dy * sb
        don_ref[...] = t * sg
        dp_ref[0] = (t * on * dsg).astype(BF16)
        dp_ref[1] = (dy * ya * sa * (1.0 - sa)).astype(BF16)
        dp_ref[2] = (dy * on * sg * sb * (1.0 - sb)).astype(BF16)

    seg = lambda k: pl.BlockSpec((None, tm, d), lambda i, k=k: (k, i, 0))
    row = pl.BlockSpec((tm, d), lambda i: (i, 0))
    return pl.pallas_call(
        body, name="outproj_bwd", grid=(t_rows // tm,),
        in_specs=[row, pl.BlockSpec((d, d), lambda i: (0, 0)), seg(5), seg(6), seg(7), row, row],
        out_specs=[row, row, pl.BlockSpec((3, tm, d), lambda i: (0, i, 0))],
        out_shape=[jax.ShapeDtypeStruct((t_rows, d), F32),
                   jax.ShapeDtypeStruct((t_rows, d), F32),
                   jax.ShapeDtypeStruct((3, t_rows, d), BF16)],
        compiler_params=_cp("parallel"))(dx_mid_b, w_out, proj, proj, proj, y_a, o_n)


def _inproj_bwd(dx_mid, dp_a, dp_b, dp_c, w_seg, x_in, gain):
    t_rows, d = x_in.shape
    tm = _row_tile(t_rows, 512)
    n_a, n_b = dp_a.shape[0], dp_b.shape[0]

    def body(dxm_ref, a_ref, b_ref, c_ref, w_ref, x_ref, g_ref, dx_ref, dxb_ref, dg_ref, acc_ref):
        j = pl.program_id(1)

        @pl.when((pl.program_id(0) == 0) & (j == 0))
        def _():
            dg_ref[...] = jnp.zeros_like(dg_ref)

        @pl.when(j == 0)
        def _():
            acc_ref[...] = jnp.zeros_like(acc_ref)

        @pl.when(j < n_a)
        def _():
            acc_ref[...] += _dot_nt(a_ref[...], w_ref[...])

        @pl.when((j >= n_a) & (j < n_a + n_b))
        def _():
            acc_ref[...] += _dot_nt(b_ref[...], w_ref[...])

        @pl.when(j >= n_a + n_b)
        def _():
            acc_ref[...] += _dot_nt(c_ref[...], w_ref[...])

        @pl.when(j == N_DEV - 1)
        def _():
            dx, dg = _rms_bwd(acc_ref[...], x_ref[...], g_ref[...])
            dx = dx + dxm_ref[...]
            dx_ref[...] = dx
            dxb_ref[...] = dx.astype(BF16)
            dg_ref[...] += dg

    def part(first, n):
        return pl.BlockSpec((None, tm, d), lambda i, j: (jnp.clip(j - first, 0, n - 1), i, 0))

    row = pl.BlockSpec((tm, d), lambda i, j: (i, 0))
    vec = pl.BlockSpec((1, d), lambda i, j: (0, 0))
    return pl.pallas_call(
        body, name="inproj_bwd", grid=(t_rows // tm, N_DEV),
        in_specs=[row, part(0, n_a), part(n_a, n_b), part(n_a + n_b, dp_c.shape[0]),
                  pl.BlockSpec((None, d, d), lambda i, j: (j, 0, 0)), row, vec],
        out_specs=[row, row, vec],
        out_shape=[jax.ShapeDtypeStruct((t_rows, d), F32),
                   jax.ShapeDtypeStruct((t_rows, d), BF16),
                   jax.ShapeDtypeStruct((1, d), F32)],
        scratch_shapes=[pltpu.VMEM((tm, d), F32)],
        compiler_params=_cp("arbitrary", "arbitrary"))(dx_mid, dp_a, dp_b, dp_c, w_seg, x_in, gain)


def _wgrad(a3, b3, name):
    n_a, t_rows, k_a = a3.shape
    n_b, _, n_cols = b3.shape
    n = max(n_a, n_b)
    bk = _row_tile(k_a, 1024)
    bn = n_cols if n_cols <= 1024 else 1024
    tt = _row_tile(t_rows, WGRAD_TOKEN_TILE)
    n_t = t_rows // tt

    def body(a_ref, b_ref, o_ref, acc_ref):
        @pl.when(pl.program_id(3) == 0)
        def _():
            acc_ref[...] = jnp.zeros_like(acc_ref)

        acc_ref[...] += _dot_tn(a_ref[...], b_ref[...])

        @pl.when(pl.program_id(3) == n_t - 1)
        def _():
            o_ref[...] = acc_ref[...].astype(BF16)

    return pl.pallas_call(
        body, name=name, grid=(n, k_a // bk, n_cols // bn, n_t),
        in_specs=[pl.BlockSpec((None, tt, bk), lambda j, p, q, t: (j if n_a > 1 else 0, t, p)),
                  pl.BlockSpec((None, tt, bn), lambda j, p, q, t: (j if n_b > 1 else 0, t, q))],
        out_specs=pl.BlockSpec((None, bk, bn), lambda j, p, q, t: (j, p, q)),
        out_shape=jax.ShapeDtypeStruct((n, k_a, n_cols), BF16),
        scratch_shapes=[pltpu.VMEM((bk, bn), F32)],
        compiler_params=_cp("parallel", "parallel", "parallel", "arbitrary"))(a3, b3)


def _conv_taps(xe, n):
    return [_shift_rows(xe, CONV_W - 1 - j)[SUBLANES:SUBLANES + n, :] for j in range(CONV_W)]


def _rg_gates(xc, w_r, b_r, w_i, b_i, sp8):
    xb = xc.astype(BF16)
    r = _sigmoid(_dot(xb, w_r) + b_r)
    i = _sigmoid(_dot(xb, w_i) + b_i)
    return r, i


def _mixer_a_fwd(proj, conv_w, conv_b, w_r, b_r, w_i, b_i, lam, seq):
    _, t_rows, d = proj.shape
    n_seq, n_blk = t_rows // seq, d // RG_BLOCK_W
    wb = RG_BLOCK_W
    ch = _row_tile(seq, ROW_CHUNK)

    def body(xa_ref, ga_ref, cw_ref, cb_ref, wr_ref, br_ref, wi_ref, bi_ref, lam_ref, h_ref, ya_ref,
             xpad, a_s, u_s):
        xpad[0:SUBLANES, :] = jnp.zeros((SUBLANES, wb), F32)
        xpad[SUBLANES:, :] = xa_ref[...]
        sp8 = RG_C * _softplus(-lam_ref[...])

        def gates(c, _):
            r0 = pl.multiple_of(c * ch, ch)
            taps = _conv_taps(xpad[pl.ds(r0, ch + SUBLANES), :], ch)
            xc = cb_ref[...] + sum(cw_ref[pl.ds(j, 1), :] * taps[j] for j in range(CONV_W))
            r, i = _rg_gates(xc, wr_ref[...], br_ref[...], wi_ref[...], bi_ref[...], sp8)
            log_a = -(r * sp8)
            a_s[pl.ds(r0, ch), :] = jnp.exp(log_a)
            u_s[pl.ds(r0, ch), :] = jnp.sqrt(jnp.maximum(_one_minus_exp(2.0 * log_a), 0.0)) * (i * xc)
            return 0

        lax.fori_loop(0, seq // ch, gates, 0)
        _scan_rows(a_s, u_s, h_ref, seq, wb, reverse=False)

        def gate_out(c, _):
            r0 = pl.multiple_of(c * ch, ch)
            gl, _ = _gelu_and_grad(ga_ref[pl.ds(r0, ch), :])
            ya_ref[pl.ds(r0, ch), :] = h_ref[pl.ds(r0, ch), :] * gl
            return 0

        lax.fori_loop(0, seq // ch, gate_out, 0)

    seg = lambda k: pl.BlockSpec((None, seq, wb), lambda s, b, k=k: (k, s, b))
    blk = pl.BlockSpec((seq, wb), lambda s, b: (s, b))
    vec = pl.BlockSpec((1, wb), lambda s, b: (0, b))
    wsp = pl.BlockSpec((None, wb, wb), lambda s, b: (b, 0, 0))
    return pl.pallas_call(
        body, name="mixer_a_fwd", grid=(n_seq, n_blk),
        in_specs=[seg(0), seg(1), pl.BlockSpec((CONV_W, wb), lambda s, b: (0, b)), vec, wsp, vec, wsp, vec, vec],
        out_specs=[blk, blk],
        out_shape=[jax.ShapeDtypeStruct((t_rows, d), F32), jax.ShapeDtypeStruct((t_rows, d), F32)],
        scratch_shapes=[pltpu.VMEM((seq + SUBLANES, wb), F32), pltpu.VMEM((seq, wb), F32),
                        pltpu.VMEM((seq, wb), F32)],
        compiler_params=_cp("parallel", "parallel"))(proj, proj, conv_w, conv_b, w_r, b_r, w_i, b_i, lam)


def _mixer_a_bwd(proj, h, d_ya, conv_w, conv_b, w_r, b_r, w_i, b_i, lam, seq):
    _, t_rows, d = proj.shape
    n_seq, n_blk = t_rows // seq, d // RG_BLOCK_W
    wb = RG_BLOCK_W
    ch = _row_tile(seq, ROW_CHUNK)
    n_ch = seq // ch

    def body(xa_ref, ga_ref, h_ref, dya_ref, cw_ref, cb_ref, wr_ref, br_ref, wi_ref, bi_ref, lam_ref,
             dp_ref, dwr_ref, dwi_ref, dbr_ref, dbi_ref, dlam_ref, dcw_ref, dcb_ref,
             xpad, hpad, a_s, e_pad, g_s, xc_s, r_s, i_s, dxc_pad):
        @pl.when(pl.program_id(1) == 0)
        def _():
            for ref in (dwr_ref, dwi_ref, dbr_ref, dbi_ref, dlam_ref, dcw_ref, dcb_ref):
                ref[...] = jnp.zeros_like(ref)

        zeros8 = jnp.zeros((SUBLANES, wb), F32)
        xpad[0:SUBLANES, :] = zeros8
        xpad[SUBLANES:, :] = xa_ref[...]
        hpad[0:SUBLANES, :] = zeros8
        hpad[SUBLANES:, :] = h_ref[...]
        e_pad[seq:, :] = zeros8
        dxc_pad[seq:, :] = zeros8
        lam_v = lam_ref[...]
        sp8 = RG_C * _softplus(-lam_v)

        def recompute(c, _):
            r0 = pl.multiple_of(c * ch, ch)
            rows = pl.ds(r0, ch)
            taps = _conv_taps(xpad[pl.ds(r0, ch + SUBLANES), :], ch)
            xc = cb_ref[...] + sum(cw_ref[pl.ds(j, 1), :] * taps[j] for j in range(CONV_W))
            r, i = _rg_gates(xc, wr_ref[...], br_ref[...], wi_ref[...], bi_ref[...], sp8)
            a = jnp.exp(-(r * sp8))
            gl, dgl = _gelu_and_grad(ga_ref[rows, :])
            dya = dya_ref[rows, :]
            g = dya * gl
            dp_ref[1, rows, :] = (dya * h_ref[rows, :] * dgl).astype(BF16)
            a_s[rows, :] = a
            e_pad[rows, :] = a * g
            g_s[rows, :] = g
            xc_s[rows, :] = xc
            r_s[rows, :] = r
            i_s[rows, :] = i
            return 0

        lax.fori_loop(0, n_ch, recompute, 0)
        _scan_rows(a_s, e_pad, e_pad, seq, wb, reverse=True)

        def grads(c, _):
            r0 = pl.multiple_of(c * ch, ch)
            rows = pl.ds(r0, ch)
            halo = pl.ds(r0, ch + SUBLANES)
            dh = g_s[rows, :] + _shift_rows(e_pad[halo, :], ch + SUBLANES - 1)[0:ch, :]
            h_prev = _shift_rows(hpad[halo, :], 1)[SUBLANES:, :]
            xc, r, i = xc_s[rows, :], r_s[rows, :], i_s[rows, :]
            log_a = -(r * sp8)
            a = jnp.exp(log_a)
            om = _one_minus_exp(2.0 * log_a)
            sq = jnp.sqrt(jnp.maximum(om, 0.0))
            t1 = dh * xc
            d_i = t1 * sq
            d_la = dh * h_prev * a + jnp.where(om > 0.0, -(t1 * i) * (1.0 - om) / sq, 0.0)
            dpr = -(d_la * sp8) * r * (1.0 - r)
            dpi = d_i * i * (1.0 - i)
            dprb, dpib, xb = dpr.astype(BF16), dpi.astype(BF16), xc.astype(BF16)
            dxc = dh * sq * i + _dot_nt(dprb, wr_ref[...]) + _dot_nt(dpib, wi_ref[...])
            dwr_ref[...] += _dot_tn(xb, dprb)
            dwi_ref[...] += _dot_tn(xb, dpib)
            dbr_ref[...] += jnp.sum(dpr, axis=0, keepdims=True)
            dbi_ref[...] += jnp.sum(dpi, axis=0, keepdims=True)
            dlam_ref[...] += jnp.sum(d_la * r, axis=0, keepdims=True) * (RG_C * _sigmoid(-lam_v))
            dcb_ref[...] += jnp.sum(dxc, axis=0, keepdims=True)
            taps = _conv_taps(xpad[halo, :], ch)
            for j in range(CONV_W):
                dcw_ref[pl.ds(j, 1), :] += jnp.sum(dxc * taps[j], axis=0, keepdims=True)
            dxc_pad[rows, :] = dxc
            return 0

        lax.fori_loop(0, n_ch, grads, 0)

        def conv_bwd(c, _):
            r0 = pl.multiple_of(c * ch, ch)
            de = dxc_pad[pl.ds(r0, ch + SUBLANES), :]
            dxa = sum(cw_ref[pl.ds(j, 1), :] * _shift_rows(de, ch + SUBLANES - (CONV_W - 1 - j))[0:ch, :]
                      for j in range(CONV_W))
            dp_ref[0, pl.ds(r0, ch), :] = dxa.astype(BF16)
            return 0

        lax.fori_loop(0, n_ch, conv_bwd, 0)

    seg = lambda k: pl.BlockSpec((None, seq, wb), lambda b, s, k=k: (k, s, b))
    blk = pl.BlockSpec((seq, wb), lambda b, s: (s, b))
    vec = pl.BlockSpec((1, wb), lambda b, s: (0, b))
    taps = pl.BlockSpec((CONV_W, wb), lambda b, s: (0, b))
    wsp = pl.BlockSpec((None, wb, wb), lambda b, s: (b, 0, 0))
    vec_shape = jax.ShapeDtypeStruct((1, d), F32)
    w_shape = jax.ShapeDtypeStruct((n_blk, wb, wb), F32)
    pad = pltpu.VMEM((seq + SUBLANES, wb), F32)
    full = pltpu.VMEM((seq, wb), F32)
    return pl.pallas_call(
        body, name="mixer_a_bwd", grid=(n_blk, n_seq),
        in_specs=[seg(0), seg(1), blk, blk, taps, vec, wsp, vec, wsp, vec, vec],
        out_specs=[pl.BlockSpec((2, seq, wb), lambda b, s: (0, s, b)), wsp, wsp, vec, vec, vec, taps, vec],
        out_shape=[jax.ShapeDtypeStruct((2, t_rows, d), BF16), w_shape, w_shape, vec_shape, vec_shape,
                   vec_shape, jax.ShapeDtypeStruct((CONV_W, d), F32), vec_shape],
        scratch_shapes=[pad, pad, full, pad, full, full, full, full, pad],
        compiler_params=_cp("parallel", "arbitrary"))(
            proj, proj, h, d_ya, conv_w, conv_b, w_r, b_r, w_i, b_i, lam)


def _hg_prepare(q_ref, z_ref, lb, rows):
    z = z_ref[rows, :]
    sig = _sigmoid(z)
    fg = lb + (1.0 - lb) * sig
    log_f = jnp.log(jnp.maximum(fg, F_MIN))
    key = (1.0 - lb) * _sigmoid(-z)
    qs, _ = _silu_and_grad(q_ref[rows, :])
    return qs, key, log_f, sig, fg


HG_UNROLL_TERMS = 32
HG_UNROLL_FWD = 16
HG_UNROLL_BWD = 16
HG_HALF = HG_CHUNK // 2
HG_STACK = HG_HALF * HG_CHUNK + HG_HALF * HG_HALF
assert HG_HALF == SUBLANES


def _slab(s):
    if s < HG_HALF:
        return s * HG_CHUNK, HG_CHUNK
    return HG_HALF * HG_CHUNK + (s - HG_HALF) * HG_HALF, HG_HALF


def _rows_seeing(x, s):
    return x if s < HG_HALF else x[HG_HALF:, :]


def _hg_decay(g_ref, r0, g_rows, s):
    n = g_rows.shape[0]
    rid = lax.broadcasted_iota(jnp.int32, (n, HG_DK), 0) + (HG_CHUNK - n)
    gs = g_ref[pl.ds(r0 + s, 1), :]
    return jnp.where(rid >= s, jnp.exp(g_rows - gs), 0.0)


def _stack(slabs):
    return jnp.concatenate(slabs, axis=0).astype(BF16)


def _slab_row_sums():
    row = lax.broadcasted_iota(jnp.int32, (HG_CHUNK, HG_STACK), 0)
    col = lax.broadcasted_iota(jnp.int32, (HG_CHUNK, HG_STACK), 1)
    lo = jnp.where(row < HG_HALF, row * HG_CHUNK, HG_HALF * HG_CHUNK + (row - HG_HALF) * HG_HALF)
    n = jnp.where(row < HG_HALF, HG_CHUNK, HG_HALF)
    return jnp.where((col >= lo) & (col < lo + n), 1.0, 0.0).astype(BF16)


def _for_chunks(n, unroll, *stages):
    unroll = min(unroll, n)
    assert n % unroll == 0

    def trip(i, _):
        chunks = [i * unroll + u for u in range(unroll)]
        carried = [stages[0](c) for c in chunks]
        for stage in stages[1:]:
            carried = [stage(c, x) for c, x in zip(chunks, carried)]
        return 0

    lax.fori_loop(0, n // unroll, trip, 0)


def _hg_state_terms(v_ref, k_ref, g_ref, states, n_chunks):
    def issue(c):
        rows = pl.ds(pl.multiple_of(c * HG_CHUNK, HG_CHUNK), HG_CHUNK)
        gc = g_ref[rows, :]
        k_end = k_ref[rows, :] * jnp.exp(gc[HG_CHUNK - 1:HG_CHUNK, :] - gc)
        return _dot_tn(v_ref[rows, :].astype(BF16), k_end.astype(BF16))

    def store(c, term):
        states[c] = term

    _for_chunks(n_chunks, HG_UNROLL_TERMS, issue, store)


def _hg_state_chain(states, g_ref, carry_ref, n_chunks, reverse):
    unroll = min(8, n_chunks)
    assert n_chunks % unroll == 0
    carry_ref[...] = jnp.zeros_like(carry_ref)

    def trip(i, _):
        st = carry_ref[...]
        for u in range(unroll):
            k = i * unroll + u
            c = n_chunks - 1 - k if reverse else k
            term = states[c]
            states[c] = st
            st = st * jnp.exp(g_ref[pl.ds(c * HG_CHUNK + HG_CHUNK - 1, 1), :]) + term
        carry_ref[...] = st
        return 0

    lax.fori_loop(0, n_chunks // unroll, trip, 0)


def _hgrn_fwd(proj, lower_bound, hg_gain, seq, carry=None):
    _, t_rows, d = proj.shape
    n_seq, n_head = t_rows // seq, d // HG_DK
    ch = _row_tile(seq, ROW_CHUNK)
    n_chunks = seq // HG_CHUNK

    def body(q_ref, z_ref, v_ref, lb_ref, gain_ref, o_ref, on_ref, qs_s, k_s, g_s, states, st_ref):
        lb = lb_ref[...]

        def prepare(c, _):
            rows = pl.ds(pl.multiple_of(c * ch, ch), ch)
            qs, key, log_f, _, _ = _hg_prepare(q_ref, z_ref, lb, rows)
            qs_s[rows, :] = qs
            k_s[rows, :] = key
            g_s[rows, :] = _seg_cumsum(log_f, HG_CHUNK)
            return 0

        lax.fori_loop(0, seq // ch, prepare, 0)
        _hg_state_terms(v_ref, k_s, g_s, states, n_chunks)
        _hg_state_chain(states, g_s, st_ref, n_chunks, reverse=False)
        ones = jnp.ones((HG_DK, HG_DK), BF16)

        def issue(c):
            r0 = pl.multiple_of(c * HG_CHUNK, HG_CHUNK)
            rows = pl.ds(r0, HG_CHUNK)
            qc, gc = qs_s[rows, :], g_s[rows, :]
            o = _dot_nt((qc * jnp.exp(gc)).astype(BF16), states[c].astype(BF16))
            pairs = [_rows_seeing(qc, s) * _hg_decay(g_s, r0, _rows_seeing(gc, s), s) * k_s[pl.ds(r0 + s, 1), :]
                     for s in range(HG_CHUNK)]
            score = _dot(_stack(pairs), ones)
            return o, score

        def combine(c, issued):
            o, score = issued
            r0 = pl.multiple_of(c * HG_CHUNK, HG_CHUNK)
            o_lo, o_hi = o[:HG_HALF, :], o[HG_HALF:, :]
            for s in range(HG_CHUNK):
                first, n = _slab(s)
                vs = v_ref[pl.ds(r0 + s, 1), :]
                if n == HG_CHUNK:
                    o_lo = o_lo + score[first:first + HG_HALF, :] * vs
                o_hi = o_hi + score[first + n - HG_HALF:first + n, :] * vs
            o_ref[pl.ds(r0, HG_CHUNK), :] = jnp.concatenate([o_lo, o_hi], axis=0)

        _for_chunks(n_chunks, HG_UNROLL_FWD, issue, combine)

        def norm(c, _):
            rows = pl.ds(pl.multiple_of(c * ch, ch), ch)
            o = o_ref[rows, :]
            on_ref[rows, :] = o * _rstd(o) * gain_ref[...]
            return 0

        lax.fori_loop(0, seq // ch, norm, 0)

    seg = lambda k: pl.BlockSpec((None, seq, HG_DK), lambda s, h, k=k: (k, s, h))
    blk = pl.BlockSpec((seq, HG_DK), lambda s, h: (s, h))
    full = pltpu.VMEM((seq, HG_DK), F32)
    return _call_carrying(
        body, carry, name="hgrn_fwd", grid=(n_seq, n_head),
        in_specs=[seg(2), seg(3), seg(4), pl.BlockSpec((1, HG_DK), lambda s, h: (0, h)),
                  pl.BlockSpec((1, HG_DK), lambda s, h: (0, 0))],
        out_specs=[blk, blk],
        out_shape=[jax.ShapeDtypeStruct((t_rows, d), F32), jax.ShapeDtypeStruct((t_rows, d), F32)],
        scratch_shapes=[full, full, full, pltpu.VMEM((n_chunks, HG_DK, HG_DK), F32),
                        pltpu.VMEM((HG_DK, HG_DK), F32)],
        semantics=("parallel", "parallel"), args=(proj, proj, proj, lower_bound, hg_gain))


def _hgrn_bwd(proj, lower_bound, hg_gain, o, d_on, seq, carry=None):
    _, t_rows, d = proj.shape
    n_seq, n_head = t_rows // seq, d // HG_DK
    ch = _row_tile(seq, ROW_CHUNK)
    n_chunks = seq // HG_CHUNK
    cc = HG_CHUNK

    def body(q_ref, z_ref, v_ref, lb_ref, gain_ref, o_ref, don_ref, dp_ref, dlb_ref, dgain_ref,
             qs_s, k_s, g_s, do_s, dqs_s, dk_s, dlf_s, states, dstates, carry_ref):
        hh, ss = pl.program_id(0), pl.program_id(1)
        lb = lb_ref[...]

        @pl.when(ss == 0)
        def _():
            dlb_ref[...] = jnp.zeros_like(dlb_ref)

        @pl.when((ss == 0) & (hh == 0))
        def _():
            dgain_ref[...] = jnp.zeros_like(dgain_ref)

        def prepare(c, _):
            rows = pl.ds(pl.multiple_of(c * ch, ch), ch)
            qs, key, log_f, _, _ = _hg_prepare(q_ref, z_ref, lb, rows)
            qs_s[rows, :] = qs
            k_s[rows, :] = key
            g_s[rows, :] = _seg_cumsum(log_f, cc)
            do, dgain = _rms_bwd(don_ref[rows, :], o_ref[rows, :], gain_ref[...])
            do_s[rows, :] = do
            dgain_ref[...] += dgain
            return 0

        lax.fori_loop(0, seq // ch, prepare, 0)

        _hg_state_terms(v_ref, k_s, g_s, states, n_chunks)
        _hg_state_chain(states, g_s, carry_ref, n_chunks, reverse=False)

        def query_term(c):
            rows = pl.ds(pl.multiple_of(c * cc, cc), cc)
            q_in = qs_s[rows, :] * jnp.exp(g_s[rows, :])
            return _dot_tn(do_s[rows, :].astype(BF16), q_in.astype(BF16))

        def store_query_term(c, term):
            dstates[c] = term

        _for_chunks(n_chunks, HG_UNROLL_TERMS, query_term, store_query_term)
        _hg_state_chain(dstates, g_s, carry_ref, n_chunks, reverse=True)
        ones = jnp.ones((HG_DK, HG_DK), BF16)
        row_sums = _slab_row_sums()

        def chunk_rows(c):
            r0 = pl.multiple_of(c * cc, cc)
            return r0, pl.ds(r0, cc)

        def through_state(c):
            r0, rows = chunk_rows(c)
            kc, gc, vc, doc = k_s[rows, :], g_s[rows, :], v_ref[rows, :], do_s[rows, :]
            st, dst = states[c], dstates[c]
            g_last = gc[cc - 1:cc, :]
            e_last, e_end = jnp.exp(g_last), jnp.exp(g_last - gc)
            dob, dstb = doc.astype(BF16), dst.astype(BF16)
            dqs = _dot(dob, st.astype(BF16))
            dk_state = _dot(vc.astype(BF16), dstb)
            dv = _dot_nt((kc * e_end).astype(BF16), dstb)
            cots = [_rows_seeing(doc, s) * v_ref[pl.ds(r0 + s, 1), :] for s in range(cc)]
            d_score = _dot(_stack(cots), ones)
            return dqs, dk_state, dv, d_score, e_last * jnp.sum(dst * st, axis=0, keepdims=True)

        def pair_terms(c, x):
            dqs, dk_state, dv, d_score, d_glast = x
            r0, rows = chunk_rows(c)
            qc, kc, gc = qs_s[rows, :], k_s[rows, :], g_s[rows, :]
            dqs = dqs * jnp.exp(gc)
            dk_state = dk_state * jnp.exp(gc[cc - 1:cc, :] - gc)
            d_glast = d_glast + jnp.sum(kc * dk_state, axis=0, keepdims=True)
            dqs_lo, dqs_hi = dqs[:HG_HALF, :], dqs[HG_HALF:, :]
            pairs, dk_terms = [], []
            for s in range(cc):
                first, n = _slab(s)
                qv = _rows_seeing(qc, s)
                decay = _hg_decay(g_s, r0, _rows_seeing(gc, s), s)
                ks = k_s[pl.ds(r0 + s, 1), :]
                da_decay = d_score[first:first + n, :] * decay
                pairs.append(qv * decay * ks)
                dk_terms.append(da_decay * qv)
                dq_term = da_decay * ks
                if n == cc:
                    dqs_lo = dqs_lo + dq_term[:HG_HALF, :]
                dqs_hi = dqs_hi + dq_term[n - HG_HALF:, :]
            score = _dot(_stack(pairs), ones)
            dk = dk_state + _dot(row_sums, _stack(dk_terms))
            return jnp.concatenate([dqs_lo, dqs_hi], axis=0), dk, dv, score, d_glast

        def value_terms(c, x):
            dqs, dk, dv, score, d_glast = x
            _, rows = chunk_rows(c)
            doc = do_s[rows, :]
            dv_terms = [score[_slab(s)[0]:sum(_slab(s)), :] * _rows_seeing(doc, s) for s in range(cc)]
            return dqs, dk, dv + _dot(row_sums, _stack(dv_terms)), d_glast

        def store(c, x):
            dqs, dk, dv, d_glast = x
            _, rows = chunk_rows(c)
            d_g = qs_s[rows, :] * dqs - k_s[rows, :] * dk
            dlf_s[rows, :] = _seg_cumsum(d_g, cc, reverse=True) + d_glast
            dqs_s[rows, :] = dqs
            dk_s[rows, :] = dk
            dp_ref[2, rows, :] = dv.astype(BF16)

        _for_chunks(n_chunks, HG_UNROLL_BWD, through_state, pair_terms, value_terms, store)

        def finish(c, _):
            rows = pl.ds(pl.multiple_of(c * ch, ch), ch)
            z = z_ref[rows, :]
            sig = _sigmoid(z)
            nsig = _sigmoid(-z)
            fg = lb + (1.0 - lb) * sig
            _, dsilu = _silu_and_grad(q_ref[rows, :])
            dp_ref[0, rows, :] = (dqs_s[rows, :] * dsilu).astype(BF16)
            dfg = jnp.where(fg > F_MIN, dlf_s[rows, :] / fg, 0.0)
            dk = dk_s[rows, :]
            dp_ref[1, rows, :] = ((dfg - dk) * (1.0 - lb) * sig * nsig).astype(BF16)
            dlb_ref[...] += jnp.sum((dfg - dk) * nsig, axis=0, keepdims=True)
            return 0

        lax.fori_loop(0, seq // ch, finish, 0)

    seg = lambda k: pl.BlockSpec((None, seq, HG_DK), lambda h, s, k=k: (k, s, h))
    blk = pl.BlockSpec((seq, HG_DK), lambda h, s: (s, h))
    full = pltpu.VMEM((seq, HG_DK), F32)
    return _call_carrying(
        body, carry, name="hgrn_bwd", grid=(n_head, n_seq),
        in_specs=[seg(2), seg(3), seg(4), pl.BlockSpec((1, HG_DK), lambda h, s: (0, h)),
                  pl.BlockSpec((1, HG_DK), lambda h, s: (0, 0)), blk, blk],
        out_specs=[pl.BlockSpec((3, seq, HG_DK), lambda h, s: (0, s, h)),
                   pl.BlockSpec((1, HG_DK), lambda h, s: (0, h)),
                   pl.BlockSpec((1, HG_DK), lambda h, s: (0, 0))],
        out_shape=[jax.ShapeDtypeStruct((3, t_rows, d), BF16), jax.ShapeDtypeStruct((1, d), F32),
                   jax.ShapeDtypeStruct((1, HG_DK), F32)],
        scratch_shapes=[full, full, full, full, full, full, full,
                        pltpu.VMEM((n_chunks, HG_DK, HG_DK), F32), pltpu.VMEM((n_chunks, HG_DK, HG_DK), F32),
                        pltpu.VMEM((HG_DK, HG_DK), F32)],
        semantics=("arbitrary", "arbitrary"), args=(proj, proj, proj, lower_bound, hg_gain, o, d_on))


def _mesh_place():
    x, y, c = lax.axis_index("x"), lax.axis_index("y"), lax.axis_index("c")
    return x, y, c


def _peer(place, k):
    x, y, c = place
    px = 1 - x if k & 4 else x
    py = 1 - y if k & 2 else y
    pc = 1 - c if k & 1 else c
    return (px, py, pc), 4 * px + 2 * py + pc


class _Exchange:
    def __init__(self, srcs, gather):
        self.n = len(srcs)
        self.gather = gather
        self.out_shape = [jax.ShapeDtypeStruct((N_DEV,) + tuple(s.shape if gather else s.shape[1:]), s.dtype)
                          for s in srcs]
        self.scratch = [pltpu.SemaphoreType.DMA((self.n * (N_DEV - 1),)),
                        pltpu.SemaphoreType.DMA((self.n * (N_DEV - 1),)),
                        pltpu.SemaphoreType.DMA((self.n,))]

    def _copies(self, src_refs, out_refs, sems):
        send_sems, recv_sems, local_sems = sems
        place = _mesh_place()
        me = 4 * place[0] + 2 * place[1] + place[2]
        local, sends, recvs = [], [], []
        for a, (src, out) in enumerate(zip(src_refs, out_refs)):
            outgoing = (lambda idx, src=src: src) if self.gather else (lambda idx, src=src: src.at[idx])
            local.append(pltpu.make_async_copy(outgoing(me), out.at[me], local_sems.at[a]))
            for k in range(1, N_DEV):
                peer, peer_idx = _peer(place, k)
                sem = a * (N_DEV - 1) + k - 1
                sends.append(pltpu.make_async_remote_copy(
                    src_ref=outgoing(peer_idx), dst_ref=out.at[me], send_sem=send_sems.at[sem],
                    recv_sem=recv_sems.at[sem], device_id=peer, device_id_type=MESH_ID))
                recvs.append(pltpu.make_async_remote_copy(
                    src_ref=outgoing(peer_idx), dst_ref=out.at[peer_idx], send_sem=send_sems.at[sem],
                    recv_sem=recv_sems.at[sem], device_id=peer, device_id_type=MESH_ID))
        return local, sends, recvs

    def start(self, src_refs, out_refs, sems):
        local, sends, _ = self._copies(src_refs, out_refs, sems)
        for cp in local + sends:
            cp.start()

    def wait(self, src_refs, out_refs, sems):
        local, sends, recvs = self._copies(src_refs, out_refs, sems)
        for cp in recvs:
            cp.wait_recv()
        for cp in sends:
            cp.wait_send()
        for cp in local:
            cp.wait()


def _call_carrying(body, carry, *, name, grid, in_specs, out_specs, out_shape, scratch_shapes, semantics, args):
    if carry is None:
        outs = pl.pallas_call(body, name=name, grid=grid, in_specs=in_specs, out_specs=out_specs,
                              out_shape=out_shape, scratch_shapes=scratch_shapes,
                              compiler_params=_cp(*semantics))(*args)
        return outs, []
    srcs, gather = carry
    ex = _Exchange(srcs, gather)
    n, n_in, n_out, n_scr = ex.n, len(in_specs), len(out_specs), len(scratch_shapes)

    def wrapped(*refs):
        ins, refs = refs[:n_in], refs[n_in:]
        src_refs, refs = refs[:n], refs[n:]
        outs, refs = refs[:n_out], refs[n_out:]
        dst_refs, refs = refs[:n], refs[n:]
        scratch, sems = refs[:n_scr], refs[n_scr:]
        first, last = None, None
        for axis, size in enumerate(grid):
            i = pl.program_id(axis)
            first = (i == 0) if first is None else first & (i == 0)
            last = (i == size - 1) if last is None else last & (i == size - 1)

        @pl.when(first)
        def _():
            ex.start(src_refs, dst_refs, sems)

        body(*ins, *outs, *scratch)

        @pl.when(last)
        def _():
            ex.wait(src_refs, dst_refs, sems)

    any_space = pl.BlockSpec(memory_space=pl.ANY)
    res = pl.pallas_call(
        wrapped, name=name + "_carrying", grid=grid, in_specs=list(in_specs) + [any_space] * n,
        out_specs=list(out_specs) + [any_space] * n, out_shape=list(out_shape) + ex.out_shape,
        scratch_shapes=list(scratch_shapes) + ex.scratch,
        compiler_params=_cp(*(["arbitrary"] * len(grid))))(*args, *srcs)
    return res[:n_out], res[n_out:]


def _exchange(srcs, name, gather):
    ex = _Exchange(srcs, gather)
    n = ex.n

    def body(*refs):
        src_refs, out_refs, sems = refs[:n], refs[n:2 * n], refs[2 * n:]
        ex.start(src_refs, out_refs, sems)
        ex.wait(src_refs, out_refs, sems)

    any_space = pl.BlockSpec(memory_space=pl.ANY)
    return pl.pallas_call(
        body, name=name, in_specs=[any_space] * n, out_specs=[any_space] * n,
        out_shape=ex.out_shape, scratch_shapes=ex.scratch)(*srcs)


def _reduce_adamw(parts, w, m, v, name):
    rows, cols = w.shape
    tr = _row_tile(rows, 128)
    c1 = np.float32(1.0 - ADAM_B1 ** ADAM_STEP)
    c2 = np.float32(1.0 - ADAM_B2 ** ADAM_STEP)

    def body(p_ref, w_ref, m_ref, v_ref, g_ref, d_ref, nm_ref, nv_ref):
        g = p_ref[0].astype(F32)
        for k in range(1, N_DEV):
            g = g + p_ref[k].astype(F32)
        nm = ADAM_B1 * m_ref[...] + (1.0 - ADAM_B1) * g
        nv = ADAM_B2 * v_ref[...] + (1.0 - ADAM_B2) * (g * g)
        g_ref[...] = g
        nm_ref[...] = nm
        nv_ref[...] = nv
        d_ref[...] = -ADAM_LR * ((nm / c1) / (jnp.sqrt(nv / c2) + ADAM_EPS) + ADAM_WD * w_ref[...])

    blk = pl.BlockSpec((tr, cols), lambda i: (i, 0))
    shp = jax.ShapeDtypeStruct((rows, cols), F32)
    return pl.pallas_call(
        body, name=name, grid=(rows // tr,),
        in_specs=[pl.BlockSpec((N_DEV, tr, cols), lambda i: (0, i, 0)), blk, blk, blk],
        out_specs=[blk, blk, blk, blk], out_shape=[shp, shp, shp, shp],
        compiler_params=_cp("parallel"))(parts, w, m, v)


def _pack(arrays, lead=0):
    parts = []
    for a in arrays:
        f = a.reshape(a.shape[:lead] + (-1, LANES))
        pad = -f.shape[lead] % PACK_ROWS
        if pad:
            f = jnp.pad(f, [(0, 0)] * lead + [(0, pad), (0, 0)])
        parts.append(f)
    return jnp.concatenate(parts, axis=lead)


def _unpack(buf, shapes, lead=0):
    out, r = [], 0
    for shp in shapes:
        n = int(np.prod(shp)) // LANES
        part = lax.slice_in_dim(buf, r, r + n, axis=lead)
        out.append(part.reshape(buf.shape[:lead] + tuple(shp)))
        r += n + (-n % PACK_ROWS)
    return out


REPLICATED = ("lb_logits", "norm_mix", "conv_b", "b_r", "b_i", "lam", "hg_norm", "norm_mlp", "norm_final")
SMALL_SHARDED = ("conv_w", "w_r", "w_i")
LARGE_SHARDED = ("w_in", "w_out", "w_up", "w_down")
WEIGHTS = ("lb_logits", "norm_mix", "w_in", "conv_w", "conv_b", "w_r", "b_r", "w_i", "b_i", "lam", "hg_norm",
           "w_out", "norm_mlp", "w_up", "w_down", "norm_final")


def _matmul_weight_shards(p):
    depth, d, _ = p["w_in"].shape
    f8 = p["w_up"].shape[2]
    first = p["w_in"].astype(BF16)
    rest = jnp.concatenate([a.astype(BF16) for a in (p["w_out"], p["w_up"].reshape(depth, f8, d), p["w_down"])],
                           axis=1)
    return [first[l] for l in range(depth)], [rest[l] for l in range(depth)]


def _split_rest(g, f8):
    d = g.shape[2]
    d8 = d // N_DEV
    return dict(w_out=g[:, :d8].reshape(d, d),
                w_up=g[:, d8:d8 + f8].reshape(N_DEV, d, f8),
                w_down=g[:, d8 + f8:])


def _unpack_mixer_weights(small, p):
    depth, d, _ = p["w_in"].shape
    n_blk = d // RG_BLOCK_W
    conv_w, w_r, w_i = _unpack(small, [p["conv_w"].shape, p["w_r"].shape, p["w_i"].shape], lead=1)
    conv_w = conv_w.transpose(1, 2, 0, 3).reshape(depth, CONV_W, d)
    w_r = w_r.transpose(1, 2, 0, 3, 4).reshape(depth, n_blk, RG_BLOCK_W, RG_BLOCK_W).astype(BF16)
    w_i = w_i.transpose(1, 2, 0, 3, 4).reshape(depth, n_blk, RG_BLOCK_W, RG_BLOCK_W).astype(BF16)
    return conv_w, w_r, w_i


def _local_step(x, target, p):
    bl, seq, d = x.shape
    depth, f8 = p["w_in"].shape[0], p["w_up"].shape[2]
    t_rows = bl * seq
    row = lambda a, l: a[l:l + 1]
    lbs = _lower_bounds_fwd(p["lb_logits"])
    shard_in, shard_rest = _matmul_weight_shards(p)
    w_in = _exchange([shard_in[0]], "gather_w_in", gather=True)[0]
    cur = x.reshape(t_rows, d)
    saved, layers = [], []
    for l in range(depth):
        if l == 0:
            (proj, h), small = _inproj_fwd(cur, row(p["norm_mix"], l), w_in,
                                           carry=([_pack([p["conv_w"], p["w_r"], p["w_i"]])], True))
            conv_w, w_r, w_i = _unpack_mixer_weights(small[0], p)
        else:
            (proj, h), _ = _inproj_fwd(cur, row(p["norm_mix"], l), w_in)
        w = dict(w_in=w_in, conv_w=conv_w[l], w_r=w_r[l], w_i=w_i[l])
        hs, y_a = _mixer_a_fwd(proj, w["conv_w"], row(p["conv_b"], l), w["w_r"], row(p["b_r"], l), w["w_i"],
                               row(p["b_i"], l), row(p["lam"], l), seq)
        (o, o_n), got = _hgrn_fwd(proj, row(lbs, l), row(p["hg_norm"], l), seq,
                                  carry=([shard_rest[l]] + ([shard_in[l + 1]] if l + 1 < depth else []), True))
        w.update(_split_rest(got[0], f8))
        w_in = got[1] if l + 1 < depth else None
        layers.append(w)
        x_mid, y = _merge_out_fwd(proj, y_a, o_n, cur, w["w_out"])
        x_out, u, h2 = _mlp_fwd(x_mid, row(p["norm_mlp"], l), w["w_up"], w["w_down"])
        saved.append(dict(x_in=cur, proj=proj, h=h, hs=hs, y_a=y_a, o=o, o_n=o_n, x_mid=x_mid, y=y, u=u, h2=h2))
        cur = x_out
    loss8, dx, dxb, g_norm_final = _loss_head(cur, p["norm_final"].reshape(1, d), target.reshape(t_rows, d))
    small = ("norm_mix", "conv_w", "conv_b", "w_r", "b_r", "w_i", "b_i", "lam", "hg_norm", "norm_mlp")
    g = {k: [None] * depth for k in small}
    d_lbs, received = [None] * depth, [None] * depth
    g_w_in = None
    for l in reversed(range(depth)):
        s, w = saved[l], layers[l]
        dx_mid, dx_mid_b, du, act, g["norm_mlp"][l] = _mlp_bwd(dx, dxb, s["u"], s["x_mid"], row(p["norm_mlp"], l),
                                                               w["w_up"], w["w_down"])
        g_w_down = _wgrad(act, dxb[None], "wgrad_down")
        g_w_up = _wgrad(s["h2"][None], du, "wgrad_up")
        d_ya, d_on, dp_c = _outproj_bwd(dx_mid_b, w["w_out"], s["proj"], s["y_a"], s["o_n"])
        g_w_out = _wgrad(s["y"][None], dx_mid_b[None], "wgrad_out").reshape(N_DEV, d // N_DEV, d)
        (dp_b, d_lbs[l], g["hg_norm"][l]), got = _hgrn_bwd(
            s["proj"], row(lbs, l), row(p["hg_norm"], l), s["o"], d_on, seq,
            carry=([g_w_out, g_w_up, g_w_down] + ([g_w_in] if g_w_in is not None else []), False))
        received[l] = [None] + list(got[:3])
        if g_w_in is not None:
            received[l + 1][0] = got[3]
        (dp_a, g["w_r"][l], g["w_i"][l], g["b_r"][l], g["b_i"][l], g["lam"][l], g["conv_w"][l],
         g["conv_b"][l]) = _mixer_a_bwd(s["proj"], s["hs"], d_ya, w["conv_w"], row(p["conv_b"], l), w["w_r"],
                                        row(p["b_r"], l), w["w_i"], row(p["b_i"], l), row(p["lam"], l), seq)
        dx, dxb, g["norm_mix"][l] = _inproj_bwd(dx_mid, dp_a, dp_b, dp_c, w["w_in"], s["x_in"],
                                                row(p["norm_mix"], l))
        hb = s["h"][None]
        g_w_in = jnp.concatenate([_wgrad(hb, dp_a, "wgrad_in_pair"), _wgrad(hb, dp_b, "wgrad_in_triple"),
                                  _wgrad(hb, dp_c, "wgrad_in_triple")], axis=0)
    received[0][0] = _exchange([g_w_in], "scatter_grad_w_in", gather=False)[0]
    grads = {k: jnp.stack(v) for k, v in g.items()}
    for k in ("norm_mix", "conv_b", "b_r", "b_i", "lam", "hg_norm", "norm_mlp"):
        grads[k] = grads[k][:, 0]
    grads["lb_logits"] = _lower_bounds_bwd(p["lb_logits"], jnp.concatenate(d_lbs, axis=0))
    grads["norm_final"] = g_norm_final[0]
    return loss8[0, 0], dx.reshape(bl, seq, d), grads, received


def _update(p, mom1, mom2, grads, received):
    depth, d, _ = p["w_in"].shape
    d8 = d // N_DEV
    n_blk, rb = d // RG_BLOCK_W, RG_BLOCK_W // N_DEV
    out = {}

    for i, k in enumerate(LARGE_SHARDED):
        shp = p[k].shape
        flat = lambda a: a.reshape(shp[0] * shp[1], shp[2])
        parts = jnp.concatenate([received[l][i] for l in range(depth)], axis=1)
        res = _reduce_adamw(parts, flat(p[k]), flat(mom1[k]), flat(mom2[k]), "adamw_" + k)
        out[k] = [r.reshape(shp) for r in res]

    small_to_dev = [
        grads["conv_w"].reshape(depth, CONV_W, N_DEV, d8).transpose(2, 0, 1, 3),
        grads["w_r"].reshape(depth, n_blk, N_DEV, rb, RG_BLOCK_W).transpose(2, 0, 1, 3, 4),
        grads["w_i"].reshape(depth, n_blk, N_DEV, rb, RG_BLOCK_W).transpose(2, 0, 1, 3, 4)]
    parts = _exchange([_pack(small_to_dev, lead=1)], "scatter_grad_mixer", gather=False)[0]
    res = _reduce_adamw(parts, *[_pack([src[k] for k in SMALL_SHARDED]) for src in (p, mom1, mom2)],
                        "adamw_mixer")
    shapes = [p[k].shape for k in SMALL_SHARDED]
    for i, vals in enumerate(zip(*[_unpack(r, shapes) for r in res])):
        out[SMALL_SHARDED[i]] = list(vals)

    parts = _exchange([_pack([grads[k] for k in REPLICATED])], "gather_grad_replicated", gather=True)[0]
    res = _reduce_adamw(parts, *[_pack([src[k] for k in REPLICATED]) for src in (p, mom1, mom2)],
                        "adamw_replicated")
    shapes = [p[k].shape for k in REPLICATED]
    for i, vals in enumerate(zip(*[_unpack(r, shapes) for r in res])):
        out[REPLICATED[i]] = list(vals)

    return tuple(out[k][i] for i in range(4) for k in WEIGHTS)


def kernel(x, lb_logits, norm_mix, w_in, conv_w, conv_b, w_r, b_r, w_i, b_i, lam, hg_norm, w_out, norm_mlp, w_up, w_down, norm_final, loss_target, m_lb_logits, m_norm_mix, m_w_in, m_conv_w, m_conv_b, m_w_r, m_b_r, m_w_i, m_b_i, m_lam, m_hg_norm, m_w_out, m_norm_mlp, m_w_up, m_w_down, m_norm_final, v_lb_logits, v_norm_mix, v_w_in, v_conv_w, v_conv_b, v_w_r, v_b_r, v_w_i, v_b_i, v_lam, v_hg_norm, v_w_out, v_norm_mlp, v_w_up, v_w_down, v_norm_final):
    p = dict(lb_logits=lb_logits, norm_mix=norm_mix, w_in=w_in, conv_w=conv_w, conv_b=conv_b, w_r=w_r, b_r=b_r,
             w_i=w_i, b_i=b_i, lam=lam, hg_norm=hg_norm, w_out=w_out, norm_mlp=norm_mlp, w_up=w_up,
             w_down=w_down, norm_final=norm_final)
    mom1 = dict(lb_logits=m_lb_logits, norm_mix=m_norm_mix, w_in=m_w_in, conv_w=m_conv_w, conv_b=m_conv_b,
                w_r=m_w_r, b_r=m_b_r, w_i=m_w_i, b_i=m_b_i, lam=m_lam, hg_norm=m_hg_norm, w_out=m_w_out,
                norm_mlp=m_norm_mlp, w_up=m_w_up, w_down=m_w_down, norm_final=m_norm_final)
    mom2 = dict(lb_logits=v_lb_logits, norm_mix=v_norm_mix, w_in=v_w_in, conv_w=v_conv_w, conv_b=v_conv_b,
                w_r=v_w_r, b_r=v_b_r, w_i=v_w_i, b_i=v_b_i, lam=v_lam, hg_norm=v_hg_norm, w_out=v_w_out,
                norm_mlp=v_norm_mlp, w_up=v_w_up, w_down=v_w_down, norm_final=v_norm_final)
    loss, grad_x, grads, received = _local_step(x, loss_target, p)
    loss = lax.psum(loss, ("x", "y", "c"))
    return (loss, grad_x) + _update(p, mom1, mom2, grads, received)
```

```python
import numpy as np

import jax
import jax.numpy as jnp
from jax import lax
from jax.experimental import pallas as pl
from jax.experimental.pallas import tpu as pltpu

F32 = jnp.float32
BF16 = jnp.bfloat16
MESH_ID = pl.DeviceIdType.MESH

N_DEV = 8
NORM_EPS = 1e-6
RG_C = 8.0
RG_BLOCK_W = 256
CONV_W = 4
HG_DK = 128
F_MIN = 1e-30
HG_CHUNK = 16
SUBLANES = 8
LANES = 128
PACK_ROWS = 16
ROW_CHUNK = 256
ROW_TILE_WEIGHT_STREAM = 1024
WGRAD_TOKEN_TILE = 2048
VMEM_LIMIT_V7X = 56 * 1024 * 1024

ADAM_LR = 0.001
ADAM_B1 = 0.9
ADAM_B2 = 0.999
ADAM_EPS = 1e-08
ADAM_WD = 0.01
ADAM_STEP = 10

GELU_C = 0.7978845608028654
GELU_K = 0.044715


def _cp(*sem):
    return pltpu.CompilerParams(dimension_semantics=sem, vmem_limit_bytes=VMEM_LIMIT_V7X)


def _row_tile(n, cap):
    if n <= cap:
        return n
    t = cap - cap % 16
    while n % t:
        t -= 16
    return t


def _dot(a, b):
    return jnp.dot(a, b, preferred_element_type=F32)


def _dot_nt(a, b):
    return lax.dot_general(a, b, (((1,), (1,)), ((), ())), preferred_element_type=F32)


def _dot_tn(a, b):
    return lax.dot_general(a, b, (((0,), (0,)), ((), ())), preferred_element_type=F32)


def _sigmoid(x):
    return jax.nn.sigmoid(x)


def _log1p_pos(y):
    return jnp.where(y < 0.01, y * (1.0 - y * (0.5 - y * (1.0 / 3.0))), jnp.log(1.0 + y))


def _softplus(x):
    return jnp.maximum(x, 0.0) + _log1p_pos(jnp.exp(-jnp.abs(x)))


def _one_minus_exp(x):
    series = -x * (1.0 + x * 0.5 * (1.0 + x * (1.0 / 3.0) * (1.0 + x * 0.25 * (1.0 + x * 0.2))))
    return jnp.where(x > -0.1, series, 1.0 - jnp.exp(x))


def _gelu_and_grad(x):
    x2 = x * x
    t = jnp.tanh(GELU_C * x * (1.0 + GELU_K * x2))
    g = 0.5 * x * (1.0 + t)
    dg = 0.5 * (1.0 + t) + 0.5 * x * (1.0 - t * t) * GELU_C * (1.0 + 3.0 * GELU_K * x2)
    return g, dg


def _silu_and_grad(x):
    s = _sigmoid(x)
    return x * s, s * (1.0 + x * (1.0 - s))


def _rstd(x):
    return lax.rsqrt(jnp.mean(x * x, axis=-1, keepdims=True) + NORM_EPS)


def _rms_bwd(dh, x, g):
    rstd = _rstd(x)
    xh = x * rstd
    dxh = dh * g
    dx = rstd * (dxh - xh * jnp.mean(dxh * xh, axis=-1, keepdims=True))
    return dx, jnp.sum(dh * xh, axis=0, keepdims=True)


def _shift_rows(x, k):
    n = x.shape[0]
    k = k % n
    return x if k == 0 else pltpu.roll(x, k, axis=0)


def _seg_cumsum(x, seg, reverse=False):
    n = x.shape[0]
    rid = lax.broadcasted_iota(jnp.int32, x.shape, 0) & (seg - 1)
    d = 1
    while d < seg:
        if reverse:
            x = jnp.where(rid < seg - d, x + _shift_rows(x, n - d), x)
        else:
            x = jnp.where(rid >= d, x + _shift_rows(x, d), x)
        d *= 2
    return x


def _scan_rows(a_ref, b_ref, out_ref, n_rows, width, reverse):
    rid = lax.broadcasted_iota(jnp.int32, (SUBLANES, width), 0)
    n_groups = n_rows // SUBLANES

    def group(i, carry):
        g = n_groups - 1 - i if reverse else i
        r0 = pl.multiple_of(g * SUBLANES, SUBLANES)
        a = a_ref[pl.ds(r0, SUBLANES), :]
        b = b_ref[pl.ds(r0, SUBLANES), :]
        for d in (1, 2, 4):
            if reverse:
                keep = rid < SUBLANES - d
                a_sh, b_sh = _shift_rows(a, SUBLANES - d), _shift_rows(b, SUBLANES - d)
            else:
                keep = rid >= d
                a_sh, b_sh = _shift_rows(a, d), _shift_rows(b, d)
            b = jnp.where(keep, a * b_sh + b, b)
            a = jnp.where(keep, a * a_sh, a)
        out = a * carry + b
        out_ref[pl.ds(r0, SUBLANES), :] = out
        edge = out[0:1, :] if reverse else out[SUBLANES - 1:SUBLANES, :]
        return jnp.broadcast_to(edge, (SUBLANES, width))

    lax.fori_loop(0, n_groups, group, jnp.zeros((SUBLANES, width), F32), unroll=4)


def _lb_softmax_rows(x_ref, depth):
    rows = [x_ref[pl.ds(l, 1), :] for l in range(depth)]
    top = rows[0]
    for r in rows[1:]:
        top = jnp.maximum(top, r)
    e = [jnp.exp(r - top) for r in rows]
    tot = e[0]
    for r in e[1:]:
        tot = tot + r
    return [r / tot for r in e]


def _lower_bounds_fwd(lb_logits):
    depth, d = lb_logits.shape

    def body(x_ref, o_ref):
        sm = _lb_softmax_rows(x_ref, depth)
        cum = jnp.zeros((1, d), F32)
        for l in range(depth):
            cum = cum + sm[l]
            o_ref[pl.ds(l, 1), :] = jnp.clip(cum - sm[0], 0.0, 1.0)

    return pl.pallas_call(body, name="lower_bounds_fwd",
                          out_shape=jax.ShapeDtypeStruct((depth, d), F32))(lb_logits)


def _lower_bounds_bwd(lb_logits, d_lbs):
    depth, d = lb_logits.shape

    def body(x_ref, g_ref, o_ref):
        sm = _lb_softmax_rows(x_ref, depth)
        cum = jnp.zeros((1, d), F32)
        d_cum = []
        for l in range(depth):
            cum = cum + sm[l]
            v = cum - sm[0]
            d_cum.append(jnp.where((v > 0.0) & (v < 1.0), g_ref[pl.ds(l, 1), :], 0.0))
        d_sm = []
        tail = jnp.zeros((1, d), F32)
        for l in reversed(range(depth)):
            tail = tail + d_cum[l]
            d_sm.append(tail)
        d_sm = d_sm[::-1]
        d_sm[0] = d_sm[0] - tail
        inner = jnp.zeros((1, d), F32)
        for l in range(depth):
            inner = inner + sm[l] * d_sm[l]
        for l in range(depth):
            o_ref[pl.ds(l, 1), :] = sm[l] * (d_sm[l] - inner)

    return pl.pallas_call(body, name="lower_bounds_bwd",
                          out_shape=jax.ShapeDtypeStruct((depth, d), F32))(lb_logits, d_lbs)


def _inproj_fwd(x, gain, w_seg, carry=None):
    t_rows, d = x.shape
    tm = _row_tile(t_rows, ROW_TILE_WEIGHT_STREAM)

    def body(x_ref, g_ref, w_ref, proj_ref, h_ref):
        @pl.when(pl.program_id(1) == 0)
        def _():
            xv = x_ref[...]
            h_ref[...] = (xv * _rstd(xv) * g_ref[...]).astype(BF16)

        proj_ref[...] = _dot(h_ref[...], w_ref[...])

    return _call_carrying(
        body, carry, name="inproj_fwd", grid=(t_rows // tm, N_DEV),
        in_specs=[pl.BlockSpec((tm, d), lambda i, j: (i, 0)),
                  pl.BlockSpec((1, d), lambda i, j: (0, 0)),
                  pl.BlockSpec((None, d, d), lambda i, j: (j, 0, 0))],
        out_specs=[pl.BlockSpec((None, tm, d), lambda i, j: (j, i, 0)),
                   pl.BlockSpec((tm, d), lambda i, j: (i, 0))],
        out_shape=[jax.ShapeDtypeStruct((N_DEV, t_rows, d), F32),
                   jax.ShapeDtypeStruct((t_rows, d), BF16)],
        scratch_shapes=[], semantics=("parallel", "arbitrary"), args=(x, gain, w_seg))


def _merge_out_fwd(proj, y_a, o_n, x, w_out):
    t_rows, d = x.shape
    tm = _row_tile(t_rows, 256)

    def body(g_ref, ma_ref, mb_ref, ya_ref, on_ref, x_ref, w_ref, xmid_ref, y_ref):
        g = g_ref[...]
        y = _sigmoid(ma_ref[...]) * ya_ref[...] + _sigmoid(mb_ref[...]) * (on_ref[...] * (g * _sigmoid(g)))
        yb = y.astype(BF16)
        y_ref[...] = yb
        xmid_ref[...] = x_ref[...] + _dot(yb, w_ref[...])

    seg = lambda k: pl.BlockSpec((None, tm, d), lambda i, k=k: (k, i, 0))
    row = pl.BlockSpec((tm, d), lambda i: (i, 0))
    return pl.pallas_call(
        body, name="merge_out_fwd", grid=(t_rows // tm,),
        in_specs=[seg(5), seg(6), seg(7), row, row, row, pl.BlockSpec((d, d), lambda i: (0, 0))],
        out_specs=[row, row],
        out_shape=[jax.ShapeDtypeStruct((t_rows, d), F32), jax.ShapeDtypeStruct((t_rows, d), BF16)],
        compiler_params=_cp("parallel"))(proj, proj, proj, y_a, o_n, x, w_out)


def _mlp_fwd(x_mid, gain, w_up, w_down):
    t_rows, d = x_mid.shape
    f8 = w_up.shape[2]
    tm = _row_tile(t_rows, ROW_TILE_WEIGHT_STREAM)

    def body(x_ref, g_ref, wu_ref, wd_ref, out_ref, u_ref, h_ref):
        @pl.when(pl.program_id(1) == 0)
        def _():
            xv = x_ref[...]
            h_ref[...] = (xv * _rstd(xv) * g_ref[...]).astype(BF16)
            out_ref[...] = xv

        u = _dot(h_ref[...], wu_ref[...])
        u_ref[...] = u
        r = jnp.maximum(u, 0.0)
        out_ref[...] += _dot((r * r).astype(BF16), wd_ref[...])

    row = pl.BlockSpec((tm, d), lambda i, j: (i, 0))
    return pl.pallas_call(
        body, name="mlp_fwd", grid=(t_rows // tm, N_DEV),
        in_specs=[row, pl.BlockSpec((1, d), lambda i, j: (0, 0)),
                  pl.BlockSpec((None, d, f8), lambda i, j: (j, 0, 0)),
                  pl.BlockSpec((None, f8, d), lambda i, j: (j, 0, 0))],
        out_specs=[row, pl.BlockSpec((None, tm, f8), lambda i, j: (j, i, 0)), row],
        out_shape=[jax.ShapeDtypeStruct((t_rows, d), F32),
                   jax.ShapeDtypeStruct((N_DEV, t_rows, f8), F32),
                   jax.ShapeDtypeStruct((t_rows, d), BF16)],
        compiler_params=_cp("parallel", "arbitrary"))(x_mid, gain, w_up, w_down)


def _loss_head(x, gain, target):
    t_rows, d = x.shape
    tm = _row_tile(t_rows, 512)

    def body(x_ref, g_ref, t_ref, loss_ref, dx_ref, dxb_ref, dg_ref):
        @pl.when(pl.program_id(0) == 0)
        def _():
            loss_ref[...] = jnp.zeros_like(loss_ref)
            dg_ref[...] = jnp.zeros_like(dg_ref)

        xv = x_ref[...]
        g = g_ref[...]
        err = xv * _rstd(xv) * g - t_ref[...]
        loss_ref[...] += (0.5 / d) * jnp.sum(err * err)
        dx, dg = _rms_bwd(err * (1.0 / d), xv, g)
        dx_ref[...] = dx
        dxb_ref[...] = dx.astype(BF16)
        dg_ref[...] += dg

    row = pl.BlockSpec((tm, d), lambda i: (i, 0))
    vec = pl.BlockSpec((1, d), lambda i: (0, 0))
    return pl.pallas_call(
        body, name="loss_head", grid=(t_rows // tm,),
        in_specs=[row, vec, row],
        out_specs=[pl.BlockSpec((SUBLANES, LANES), lambda i: (0, 0)), row, row, vec],
        out_shape=[jax.ShapeDtypeStruct((SUBLANES, LANES), F32),
                   jax.ShapeDtypeStruct((t_rows, d), F32),
                   jax.ShapeDtypeStruct((t_rows, d), BF16),
                   jax.ShapeDtypeStruct((1, d), F32)],
        compiler_params=_cp("arbitrary"))(x, gain, target)


def _mlp_bwd(d_out, d_out_b, u, x_mid, gain, w_up, w_down):
    t_rows, d = x_mid.shape
    f8 = w_up.shape[2]
    tm = _row_tile(t_rows, ROW_TILE_WEIGHT_STREAM)
    sub = _row_tile(tm, ROW_CHUNK)

    def body(do_ref, dob_ref, u_ref, x_ref, g_ref, wu_ref, wd_ref, dx_ref, dxb_ref, du_ref, act_ref, dg_ref):
        j = pl.program_id(1)

        @pl.when((pl.program_id(0) == 0) & (j == 0))
        def _():
            dg_ref[...] = jnp.zeros_like(dg_ref)

        @pl.when(j == 0)
        def _():
            dx_ref[...] = jnp.zeros_like(dx_ref)

        r = jnp.maximum(u_ref[...], 0.0)
        act_ref[...] = (r * r).astype(BF16)
        du = (_dot_nt(dob_ref[...], wd_ref[...]) * (2.0 * r)).astype(BF16)
        du_ref[...] = du
        dx_ref[...] += _dot_nt(du, wu_ref[...])

        @pl.when(j == N_DEV - 1)
        def _():
            def finish(c, _):
                rows = pl.ds(pl.multiple_of(c * sub, sub), sub)
                dx, dg = _rms_bwd(dx_ref[rows, :], x_ref[rows, :], g_ref[...])
                dx = dx + do_ref[rows, :]
                dx_ref[rows, :] = dx
                dxb_ref[rows, :] = dx.astype(BF16)
                dg_ref[...] += dg
                return 0

            lax.fori_loop(0, tm // sub, finish, 0)

    row = pl.BlockSpec((tm, d), lambda i, j: (i, 0))
    vec = pl.BlockSpec((1, d), lambda i, j: (0, 0))
    hid = pl.BlockSpec((None, tm, f8), lambda i, j: (j, i, 0))
    return pl.pallas_call(
        body, name="mlp_bwd", grid=(t_rows // tm, N_DEV),
        in_specs=[row, row, hid, row, vec,
                  pl.BlockSpec((None, d, f8), lambda i, j: (j, 0, 0)),
                  pl.BlockSpec((None, f8, d), lambda i, j: (j, 0, 0))],
        out_specs=[row, row, hid, hid, vec],
        out_shape=[jax.ShapeDtypeStruct((t_rows, d), F32),
                   jax.ShapeDtypeStruct((t_rows, d), BF16),
                   jax.ShapeDtypeStruct((N_DEV, t_rows, f8), BF16),
                   jax.ShapeDtypeStruct((N_DEV, t_rows, f8), BF16),
                   jax.ShapeDtypeStruct((1, d), F32)],
        compiler_params=_cp("arbitrary", "arbitrary"))(d_out, d_out_b, u, x_mid, gain, w_up, w_down)


def _outproj_bwd(dx_mid_b, w_out, proj, y_a, o_n):
    t_rows, d = y_a.shape
    tm = _row_tile(t_rows, 256)

    def body(dx_ref, w_ref, g_ref, ma_ref, mb_ref, ya_ref, on_ref, dya_ref, don_ref, dp_ref):
        dy = _dot_nt(dx_ref[...], w_ref[...])
        sa = _sigmoid(ma_ref[...])
        sb = _sigmoid(mb_ref[...])
        sg, dsg = _silu_and_grad(g_ref[...])
        ya = ya_ref[...]
        on = on_ref[...]
        dya_ref[...] = dy * sa
        t = dy * sb
        don_ref[...] = t * sg
        dp_ref[0] = (t * on * dsg).astype(BF16)
        dp_ref[1] = (dy * ya * sa * (1.0 - sa)).astype(BF16)
        dp_ref[2] = (dy * on * sg * sb * (1.0 - sb)).astype(BF16)

    seg = lambda k: pl.BlockSpec((None, tm, d), lambda i, k=k: (k, i, 0))
    row = pl.BlockSpec((tm, d), lambda i: (i, 0))
    return pl.pallas_call(
        body, name="outproj_bwd", grid=(t_rows // tm,),
        in_specs=[row, pl.BlockSpec((d, d), lambda i: (0, 0)), seg(5), seg(6), seg(7), row, row],
        out_specs=[row, row, pl.BlockSpec((3, tm, d), lambda i: (0, i, 0))],
        out_shape=[jax.ShapeDtypeStruct((t_rows, d), F32),
                   jax.ShapeDtypeStruct((t_rows, d), F32),
                   jax.ShapeDtypeStruct((3, t_rows, d), BF16)],
        compiler_params=_cp("parallel"))(dx_mid_b, w_out, proj, proj, proj, y_a, o_n)


def _inproj_bwd(dx_mid, dp_a, dp_b, dp_c, w_seg, x_in, gain):
    t_rows, d = x_in.shape
    tm = _row_tile(t_rows, ROW_TILE_WEIGHT_STREAM)
    sub = _row_tile(tm, ROW_CHUNK)
    n_a, n_b = dp_a.shape[0], dp_b.shape[0]

    def body(dxm_ref, a_ref, b_ref, c_ref, w_ref, x_ref, g_ref, dx_ref, dxb_ref, dg_ref):
        j = pl.program_id(1)

        @pl.when((pl.program_id(0) == 0) & (j == 0))
        def _():
            dg_ref[...] = jnp.zeros_like(dg_ref)

        @pl.when(j == 0)
        def _():
            dx_ref[...] = jnp.zeros_like(dx_ref)

        @pl.when(j < n_a)
        def _():
            dx_ref[...] += _dot_nt(a_ref[...], w_ref[...])

        @pl.when((j >= n_a) & (j < n_a + n_b))
        def _():
            dx_ref[...] += _dot_nt(b_ref[...], w_ref[...])

        @pl.when(j >= n_a + n_b)
        def _():
            dx_ref[...] += _dot_nt(c_ref[...], w_ref[...])

        @pl.when(j == N_DEV - 1)
        def _():
            def finish(c, _):
                rows = pl.ds(pl.multiple_of(c * sub, sub), sub)
                dx, dg = _rms_bwd(dx_ref[rows, :], x_ref[rows, :], g_ref[...])
                dx = dx + dxm_ref[rows, :]
                dx_ref[rows, :] = dx
                dxb_ref[rows, :] = dx.astype(BF16)
                dg_ref[...] += dg
                return 0

            lax.fori_loop(0, tm // sub, finish, 0)

    def part(first, n):
        return pl.BlockSpec((None, tm, d), lambda i, j: (jnp.clip(j - first, 0, n - 1), i, 0))

    row = pl.BlockSpec((tm, d), lambda i, j: (i, 0))
    vec = pl.BlockSpec((1, d), lambda i, j: (0, 0))
    return pl.pallas_call(
        body, name="inproj_bwd", grid=(t_rows // tm, N_DEV),
        in_specs=[row, part(0, n_a), part(n_a, n_b), part(n_a + n_b, dp_c.shape[0]),
                  pl.BlockSpec((None, d, d), lambda i, j: (j, 0, 0)), row, vec],
        out_specs=[row, row, vec],
        out_shape=[jax.ShapeDtypeStruct((t_rows, d), F32),
                   jax.ShapeDtypeStruct((t_rows, d), BF16),
                   jax.ShapeDtypeStruct((1, d), F32)],
        compiler_params=_cp("arbitrary", "arbitrary"))(dx_mid, dp_a, dp_b, dp_c, w_seg, x_in, gain)


def _wgrad(a3, b3, name):
    n_a, t_rows, k_a = a3.shape
    n_b, _, n_cols = b3.shape
    n = max(n_a, n_b)
    bk = _row_tile(k_a, 1024)
    bn = n_cols if n_cols <= 1024 else 1024
    tt = _row_tile(t_rows, WGRAD_TOKEN_TILE)
    n_t = t_rows // tt

    def body(a_ref, b_ref, o_ref, acc_ref):
        @pl.when(pl.program_id(3) == 0)
        def _():
            acc_ref[...] = jnp.zeros_like(acc_ref)

        acc_ref[...] += _dot_tn(a_ref[...], b_ref[...])

        @pl.when(pl.program_id(3) == n_t - 1)
        def _():
            o_ref[...] = acc_ref[...].astype(BF16)

    return pl.pallas_call(
        body, name=name, grid=(n, k_a // bk, n_cols // bn, n_t),
        in_specs=[pl.BlockSpec((None, tt, bk), lambda j, p, q, t: (j if n_a > 1 else 0, t, p)),
                  pl.BlockSpec((None, tt, bn), lambda j, p, q, t: (j if n_b > 1 else 0, t, q))],
        out_specs=pl.BlockSpec((None, bk, bn), lambda j, p, q, t: (j, p, q)),
        out_shape=jax.ShapeDtypeStruct((n, k_a, n_cols), BF16),
        scratch_shapes=[pltpu.VMEM((bk, bn), F32)],
        compiler_params=_cp("parallel", "parallel", "parallel", "arbitrary"))(a3, b3)


def _conv_taps(xe, n):
    return [_shift_rows(xe, CONV_W - 1 - j)[SUBLANES:SUBLANES + n, :] for j in range(CONV_W)]


def _rg_gates(xc, w_r, b_r, w_i, b_i, sp8):
    xb = xc.astype(BF16)
    r = _sigmoid(_dot(xb, w_r) + b_r)
    i = _sigmoid(_dot(xb, w_i) + b_i)
    return r, i


def _mixer_a_fwd(proj, conv_w, conv_b, w_r, b_r, w_i, b_i, lam, seq):
    _, t_rows, d = proj.shape
    n_seq, n_blk = t_rows // seq, d // RG_BLOCK_W
    wb = RG_BLOCK_W
    ch = _row_tile(seq, ROW_CHUNK)

    def body(xa_ref, ga_ref, cw_ref, cb_ref, wr_ref, br_ref, wi_ref, bi_ref, lam_ref, h_ref, ya_ref,
             xpad, a_s, u_s):
        xpad[0:SUBLANES, :] = jnp.zeros((SUBLANES, wb), F32)
        xpad[SUBLANES:, :] = xa_ref[...]
        sp8 = RG_C * _softplus(-lam_ref[...])

        def gates(c, _):
            r0 = pl.multiple_of(c * ch, ch)
            taps = _conv_taps(xpad[pl.ds(r0, ch + SUBLANES), :], ch)
            xc = cb_ref[...] + sum(cw_ref[pl.ds(j, 1), :] * taps[j] for j in range(CONV_W))
            r, i = _rg_gates(xc, wr_ref[...], br_ref[...], wi_ref[...], bi_ref[...], sp8)
            log_a = -(r * sp8)
            a_s[pl.ds(r0, ch), :] = jnp.exp(log_a)
            u_s[pl.ds(r0, ch), :] = jnp.sqrt(jnp.maximum(_one_minus_exp(2.0 * log_a), 0.0)) * (i * xc)
            return 0

        lax.fori_loop(0, seq // ch, gates, 0)
        _scan_rows(a_s, u_s, h_ref, seq, wb, reverse=False)

        def gate_out(c, _):
            r0 = pl.multiple_of(c * ch, ch)
            gl, _ = _gelu_and_grad(ga_ref[pl.ds(r0, ch), :])
            ya_ref[pl.ds(r0, ch), :] = h_ref[pl.ds(r0, ch), :] * gl
            return 0

        lax.fori_loop(0, seq // ch, gate_out, 0)

    seg = lambda k: pl.BlockSpec((None, seq, wb), lambda s, b, k=k: (k, s, b))
    blk = pl.BlockSpec((seq, wb), lambda s, b: (s, b))
    vec = pl.BlockSpec((1, wb), lambda s, b: (0, b))
    wsp = pl.BlockSpec((None, wb, wb), lambda s, b: (b, 0, 0))
    return pl.pallas_call(
        body, name="mixer_a_fwd", grid=(n_seq, n_blk),
        in_specs=[seg(0), seg(1), pl.BlockSpec((CONV_W, wb), lambda s, b: (0, b)), vec, wsp, vec, wsp, vec, vec],
        out_specs=[blk, blk],
        out_shape=[jax.ShapeDtypeStruct((t_rows, d), F32), jax.ShapeDtypeStruct((t_rows, d), F32)],
        scratch_shapes=[pltpu.VMEM((seq + SUBLANES, wb), F32), pltpu.VMEM((seq, wb), F32),
                        pltpu.VMEM((seq, wb), F32)],
        compiler_params=_cp("parallel", "parallel"))(proj, proj, conv_w, conv_b, w_r, b_r, w_i, b_i, lam)


def _mixer_a_bwd(proj, h, d_ya, conv_w, conv_b, w_r, b_r, w_i, b_i, lam, seq):
    _, t_rows, d = proj.shape
    n_seq, n_blk = t_rows // seq, d // RG_BLOCK_W
    wb = RG_BLOCK_W
    ch = _row_tile(seq, ROW_CHUNK)
    n_ch = seq // ch

    def body(xa_ref, ga_ref, h_ref, dya_ref, cw_ref, cb_ref, wr_ref, br_ref, wi_ref, bi_ref, lam_ref,
             dp_ref, dwr_ref, dwi_ref, dbr_ref, dbi_ref, dlam_ref, dcw_ref, dcb_ref,
             xpad, hpad, a_s, e_pad, g_s, xc_s, r_s, i_s, dxc_pad):
        @pl.when(pl.program_id(1) == 0)
        def _():
            for ref in (dwr_ref, dwi_ref, dbr_ref, dbi_ref, dlam_ref, dcw_ref, dcb_ref):
                ref[...] = jnp.zeros_like(ref)

        zeros8 = jnp.zeros((SUBLANES, wb), F32)
        xpad[0:SUBLANES, :] = zeros8
        xpad[SUBLANES:, :] = xa_ref[...]
        hpad[0:SUBLANES, :] = zeros8
        hpad[SUBLANES:, :] = h_ref[...]
        e_pad[seq:, :] = zeros8
        dxc_pad[seq:, :] = zeros8
        lam_v = lam_ref[...]
        sp8 = RG_C * _softplus(-lam_v)

        def recompute(c, _):
            r0 = pl.multiple_of(c * ch, ch)
            rows = pl.ds(r0, ch)
            taps = _conv_taps(xpad[pl.ds(r0, ch + SUBLANES), :], ch)
            xc = cb_ref[...] + sum(cw_ref[pl.ds(j, 1), :] * taps[j] for j in range(CONV_W))
            r, i = _rg_gates(xc, wr_ref[...], br_ref[...], wi_ref[...], bi_ref[...], sp8)
            a = jnp.exp(-(r * sp8))
            gl, dgl = _gelu_and_grad(ga_ref[rows, :])
            dya = dya_ref[rows, :]
            g = dya * gl
            dp_ref[1, rows, :] = (dya * h_ref[rows, :] * dgl).astype(BF16)
            a_s[rows, :] = a
            e_pad[rows, :] = a * g
            g_s[rows, :] = g
            xc_s[rows, :] = xc
            r_s[rows, :] = r
            i_s[rows, :] = i
            return 0

        lax.fori_loop(0, n_ch, recompute, 0)
        _scan_rows(a_s, e_pad, e_pad, seq, wb, reverse=True)

        def grads(c, _):
            r0 = pl.multiple_of(c * ch, ch)
            rows = pl.ds(r0, ch)
            halo = pl.ds(r0, ch + SUBLANES)
            dh = g_s[rows, :] + _shift_rows(e_pad[halo, :], ch + SUBLANES - 1)[0:ch, :]
            h_prev = _shift_rows(hpad[halo, :], 1)[SUBLANES:, :]
            xc, r, i = xc_s[rows, :], r_s[rows, :], i_s[rows, :]
            log_a = -(r * sp8)
            a = jnp.exp(log_a)
            om = _one_minus_exp(2.0 * log_a)
            sq = jnp.sqrt(jnp.maximum(om, 0.0))
            t1 = dh * xc
            d_i = t1 * sq
            d_la = dh * h_prev * a + jnp.where(om > 0.0, -(t1 * i) * (1.0 - om) / sq, 0.0)
            dpr = -(d_la * sp8) * r * (1.0 - r)
            dpi = d_i * i * (1.0 - i)
            dprb, dpib, xb = dpr.astype(BF16), dpi.astype(BF16), xc.astype(BF16)
            dxc = dh * sq * i + _dot_nt(dprb, wr_ref[...]) + _dot_nt(dpib, wi_ref[...])
            dwr_ref[...] += _dot_tn(xb, dprb)
            dwi_ref[...] += _dot_tn(xb, dpib)
            dbr_ref[...] += jnp.sum(dpr, axis=0, keepdims=True)
            dbi_ref[...] += jnp.sum(dpi, axis=0, keepdims=True)
            dlam_ref[...] += jnp.sum(d_la * r, axis=0, keepdims=True) * (RG_C * _sigmoid(-lam_v))
            dcb_ref[...] += jnp.sum(dxc, axis=0, keepdims=True)
            taps = _conv_taps(xpad[halo, :], ch)
            for j in range(CONV_W):
                dcw_ref[pl.ds(j, 1), :] += jnp.sum(dxc * taps[j], axis=0, keepdims=True)
            dxc_pad[rows, :] = dxc
            return 0

        lax.fori_loop(0, n_ch, grads, 0)

        def conv_bwd(c, _):
            r0 = pl.multiple_of(c * ch, ch)
            de = dxc_pad[pl.ds(r0, ch + SUBLANES), :]
            dxa = sum(cw_ref[pl.ds(j, 1), :] * _shift_rows(de, ch + SUBLANES - (CONV_W - 1 - j))[0:ch, :]
                      for j in range(CONV_W))
            dp_ref[0, pl.ds(r0, ch), :] = dxa.astype(BF16)
            return 0

        lax.fori_loop(0, n_ch, conv_bwd, 0)

    seg = lambda k: pl.BlockSpec((None, seq, wb), lambda b, s, k=k: (k, s, b))
    blk = pl.BlockSpec((seq, wb), lambda b, s: (s, b))
    vec = pl.BlockSpec((1, wb), lambda b, s: (0, b))
    taps = pl.BlockSpec((CONV_W, wb), lambda b, s: (0, b))
    wsp = pl.BlockSpec((None, wb, wb), lambda b, s: (b, 0, 0))
    vec_shape = jax.ShapeDtypeStruct((1, d), F32)
    w_shape = jax.ShapeDtypeStruct((n_blk, wb, wb), F32)
    pad = pltpu.VMEM((seq + SUBLANES, wb), F32)
    full = pltpu.VMEM((seq, wb), F32)
    return pl.pallas_call(
        body, name="mixer_a_bwd", grid=(n_blk, n_seq),
        in_specs=[seg(0), seg(1), blk, blk, taps, vec, wsp, vec, wsp, vec, vec],
        out_specs=[pl.BlockSpec((2, seq, wb), lambda b, s: (0, s, b)), wsp, wsp, vec, vec, vec, taps, vec],
        out_shape=[jax.ShapeDtypeStruct((2, t_rows, d), BF16), w_shape, w_shape, vec_shape, vec_shape,
                   vec_shape, jax.ShapeDtypeStruct((CONV_W, d), F32), vec_shape],
        scratch_shapes=[pad, pad, full, pad, full, full, full, full, pad],
        compiler_params=_cp("parallel", "arbitrary"))(
            proj, proj, h, d_ya, conv_w, conv_b, w_r, b_r, w_i, b_i, lam)


def _hg_prepare(q_ref, z_ref, lb, rows):
    z = z_ref[rows, :]
    sig = _sigmoid(z)
    fg = lb + (1.0 - lb) * sig
    log_f = jnp.log(jnp.maximum(fg, F_MIN))
    key = (1.0 - lb) * _sigmoid(-z)
    qs, _ = _silu_and_grad(q_ref[rows, :])
    return qs, key, log_f, sig, fg


HG_UNROLL_TERMS = 32
HG_UNROLL_FWD = 16
HG_UNROLL_BWD = 16
HG_HALF = HG_CHUNK // 2
HG_STACK = HG_HALF * HG_CHUNK + HG_HALF * HG_HALF
assert HG_HALF == SUBLANES


def _slab(s):
    if s < HG_HALF:
        return s * HG_CHUNK, HG_CHUNK
    return HG_HALF * HG_CHUNK + (s - HG_HALF) * HG_HALF, HG_HALF


def _rows_seeing(x, s):
    return x if s < HG_HALF else x[HG_HALF:, :]


def _hg_decay(g_ref, r0, g_rows, s):
    n = g_rows.shape[0]
    rid = lax.broadcasted_iota(jnp.int32, (n, HG_DK), 0) + (HG_CHUNK - n)
    gs = g_ref[pl.ds(r0 + s, 1), :]
    return jnp.where(rid >= s, jnp.exp(g_rows - gs), 0.0)


def _stack(slabs):
    return jnp.concatenate(slabs, axis=0).astype(BF16)


def _slab_row_sums():
    row = lax.broadcasted_iota(jnp.int32, (HG_CHUNK, HG_STACK), 0)
    col = lax.broadcasted_iota(jnp.int32, (HG_CHUNK, HG_STACK), 1)
    lo = jnp.where(row < HG_HALF, row * HG_CHUNK, HG_HALF * HG_CHUNK + (row - HG_HALF) * HG_HALF)
    n = jnp.where(row < HG_HALF, HG_CHUNK, HG_HALF)
    return jnp.where((col >= lo) & (col < lo + n), 1.0, 0.0).astype(BF16)


def _for_chunks(n, unroll, *stages):
    unroll = min(unroll, n)
    assert n % unroll == 0

    def trip(i, _):
        chunks = [i * unroll + u for u in range(unroll)]
        carried = [stages[0](c) for c in chunks]
        for stage in stages[1:]:
            carried = [stage(c, x) for c, x in zip(chunks, carried)]
        return 0

    lax.fori_loop(0, n // unroll, trip, 0)


def _hg_state_terms(v_ref, k_ref, g_ref, states, n_chunks):
    def issue(c):
        rows = pl.ds(pl.multiple_of(c * HG_CHUNK, HG_CHUNK), HG_CHUNK)
        gc = g_ref[rows, :]
        k_end = k_ref[rows, :] * jnp.exp(gc[HG_CHUNK - 1:HG_CHUNK, :] - gc)
        return _dot_tn(v_ref[rows, :].astype(BF16), k_end.astype(BF16))

    def store(c, term):
        states[c] = term

    _for_chunks(n_chunks, HG_UNROLL_TERMS, issue, store)


def _hg_state_chain(states, g_ref, carry_ref, n_chunks, reverse):
    unroll = min(8, n_chunks)
    assert n_chunks % unroll == 0
    carry_ref[...] = jnp.zeros_like(carry_ref)

    def trip(i, _):
        st = carry_ref[...]
        for u in range(unroll):
            k = i * unroll + u
            c = n_chunks - 1 - k if reverse else k
            term = states[c]
            states[c] = st
            st = st * jnp.exp(g_ref[pl.ds(c * HG_CHUNK + HG_CHUNK - 1, 1), :]) + term
        carry_ref[...] = st
        return 0

    lax.fori_loop(0, n_chunks // unroll, trip, 0)


def _hgrn_fwd(proj, lower_bound, hg_gain, seq, carry=None):
    _, t_rows, d = proj.shape
    n_seq, n_head = t_rows // seq, d // HG_DK
    ch = _row_tile(seq, ROW_CHUNK)
    n_chunks = seq // HG_CHUNK

    def body(q_ref, z_ref, v_ref, lb_ref, gain_ref, o_ref, on_ref, qs_s, k_s, g_s, states, st_ref):
        lb = lb_ref[...]

        def prepare(c, _):
            rows = pl.ds(pl.multiple_of(c * ch, ch), ch)
            qs, key, log_f, _, _ = _hg_prepare(q_ref, z_ref, lb, rows)
            qs_s[rows, :] = qs
            k_s[rows, :] = key
            g_s[rows, :] = _seg_cumsum(log_f, HG_CHUNK)
            return 0

        lax.fori_loop(0, seq // ch, prepare, 0)
        _hg_state_terms(v_ref, k_s, g_s, states, n_chunks)
        _hg_state_chain(states, g_s, st_ref, n_chunks, reverse=False)
        ones = jnp.ones((HG_DK, HG_DK), BF16)

        def issue(c):
            r0 = pl.multiple_of(c * HG_CHUNK, HG_CHUNK)
            rows = pl.ds(r0, HG_CHUNK)
            qc, gc = qs_s[rows, :], g_s[rows, :]
            o = _dot_nt((qc * jnp.exp(gc)).astype(BF16), states[c].astype(BF16))
            pairs = [_rows_seeing(qc, s) * _hg_decay(g_s, r0, _rows_seeing(gc, s), s) * k_s[pl.ds(r0 + s, 1), :]
                     for s in range(HG_CHUNK)]
            score = _dot(_stack(pairs), ones)
            return o, score

        def combine(c, issued):
            o, score = issued
            r0 = pl.multiple_of(c * HG_CHUNK, HG_CHUNK)
            o_lo, o_hi = o[:HG_HALF, :], o[HG_HALF:, :]
            for s in range(HG_CHUNK):
                first, n = _slab(s)
                vs = v_ref[pl.ds(r0 + s, 1), :]
                if n == HG_CHUNK:
                    o_lo = o_lo + score[first:first + HG_HALF, :] * vs
                o_hi = o_hi + score[first + n - HG_HALF:first + n, :] * vs
            o_ref[pl.ds(r0, HG_CHUNK), :] = jnp.concatenate([o_lo, o_hi], axis=0)

        _for_chunks(n_chunks, HG_UNROLL_FWD, issue, combine)

        def norm(c, _):
            rows = pl.ds(pl.multiple_of(c * ch, ch), ch)
            o = o_ref[rows, :]
            on_ref[rows, :] = o * _rstd(o) * gain_ref[...]
            return 0

        lax.fori_loop(0, seq // ch, norm, 0)

    seg = lambda k: pl.BlockSpec((None, seq, HG_DK), lambda s, h, k=k: (k, s, h))
    blk = pl.BlockSpec((seq, HG_DK), lambda s, h: (s, h))
    full = pltpu.VMEM((seq, HG_DK), F32)
    return _call_carrying(
        body, carry, name="hgrn_fwd", grid=(n_seq, n_head),
        in_specs=[seg(2), seg(3), seg(4), pl.BlockSpec((1, HG_DK), lambda s, h: (0, h)),
                  pl.BlockSpec((1, HG_DK), lambda s, h: (0, 0))],
        out_specs=[blk, blk],
        out_shape=[jax.ShapeDtypeStruct((t_rows, d), F32), jax.ShapeDtypeStruct((t_rows, d), F32)],
        scratch_shapes=[full, full, full, pltpu.VMEM((n_chunks, HG_DK, HG_DK), F32),
                        pltpu.VMEM((HG_DK, HG_DK), F32)],
        semantics=("parallel", "parallel"), args=(proj, proj, proj, lower_bound, hg_gain))


def _hgrn_bwd(proj, lower_bound, hg_gain, o, d_on, seq, carry=None):
    _, t_rows, d = proj.shape
    n_seq, n_head = t_rows // seq, d // HG_DK
    ch = _row_tile(seq, ROW_CHUNK)
    n_chunks = seq // HG_CHUNK
    cc = HG_CHUNK

    def body(q_ref, z_ref, v_ref, lb_ref, gain_ref, o_ref, don_ref, dp_ref, dlb_ref, dgain_ref,
             qs_s, k_s, g_s, do_s, dqs_s, dk_s, dlf_s, states, dstates, carry_ref):
        hh, ss = pl.program_id(0), pl.program_id(1)
        lb = lb_ref[...]

        @pl.when(ss == 0)
        def _():
            dlb_ref[...] = jnp.zeros_like(dlb_ref)

        @pl.when((ss == 0) & (hh == 0))
        def _():
            dgain_ref[...] = jnp.zeros_like(dgain_ref)

        def prepare(c, _):
            rows = pl.ds(pl.multiple_of(c * ch, ch), ch)
            qs, key, log_f, _, _ = _hg_prepare(q_ref, z_ref, lb, rows)
            qs_s[rows, :] = qs
            k_s[rows, :] = key
            g_s[rows, :] = _seg_cumsum(log_f, cc)
            do, dgain = _rms_bwd(don_ref[rows, :], o_ref[rows, :], gain_ref[...])
            do_s[rows, :] = do
            dgain_ref[...] += dgain
            return 0

        lax.fori_loop(0, seq // ch, prepare, 0)

        _hg_state_terms(v_ref, k_s, g_s, states, n_chunks)
        _hg_state_chain(states, g_s, carry_ref, n_chunks, reverse=False)

        def query_term(c):
            rows = pl.ds(pl.multiple_of(c * cc, cc), cc)
            q_in = qs_s[rows, :] * jnp.exp(g_s[rows, :])
            return _dot_tn(do_s[rows, :].astype(BF16), q_in.astype(BF16))

        def store_query_term(c, term):
            dstates[c] = term

        _for_chunks(n_chunks, HG_UNROLL_TERMS, query_term, store_query_term)
        _hg_state_chain(dstates, g_s, carry_ref, n_chunks, reverse=True)
        ones = jnp.ones((HG_DK, HG_DK), BF16)
        row_sums = _slab_row_sums()

        def chunk_rows(c):
            r0 = pl.multiple_of(c * cc, cc)
            return r0, pl.ds(r0, cc)

        def through_state(c):
            r0, rows = chunk_rows(c)
            kc, gc, vc, doc = k_s[rows, :], g_s[rows, :], v_ref[rows, :], do_s[rows, :]
            st, dst = states[c], dstates[c]
            g_last = gc[cc - 1:cc, :]
            e_last, e_end = jnp.exp(g_last), jnp.exp(g_last - gc)
            dob, dstb = doc.astype(BF16), dst.astype(BF16)
            dqs = _dot(dob, st.astype(BF16))
            dk_state = _dot(vc.astype(BF16), dstb)
            dv = _dot_nt((kc * e_end).astype(BF16), dstb)
            cots = [_rows_seeing(doc, s) * v_ref[pl.ds(r0 + s, 1), :] for s in range(cc)]
            d_score = _dot(_stack(cots), ones)
            return dqs, dk_state, dv, d_score, e_last * jnp.sum(dst * st, axis=0, keepdims=True)

        def pair_terms(c, x):
            dqs, dk_state, dv, d_score, d_glast = x
            r0, rows = chunk_rows(c)
            qc, kc, gc = qs_s[rows, :], k_s[rows, :], g_s[rows, :]
            dqs = dqs * jnp.exp(gc)
            dk_state = dk_state * jnp.exp(gc[cc - 1:cc, :] - gc)
            d_glast = d_glast + jnp.sum(kc * dk_state, axis=0, keepdims=True)
            dqs_lo, dqs_hi = dqs[:HG_HALF, :], dqs[HG_HALF:, :]
            pairs, dk_terms = [], []
            for s in range(cc):
                first, n = _slab(s)
                qv = _rows_seeing(qc, s)
                decay = _hg_decay(g_s, r0, _rows_seeing(gc, s), s)
                ks = k_s[pl.ds(r0 + s, 1), :]
                da_decay = d_score[first:first + n, :] * decay
                pairs.append(qv * decay * ks)
                dk_terms.append(da_decay * qv)
                dq_term = da_decay * ks
                if n == cc:
                    dqs_lo = dqs_lo + dq_term[:HG_HALF, :]
                dqs_hi = dqs_hi + dq_term[n - HG_HALF:, :]
            score = _dot(_stack(pairs), ones)
            dk = dk_state + _dot(row_sums, _stack(dk_terms))
            return jnp.concatenate([dqs_lo, dqs_hi], axis=0), dk, dv, score, d_glast

        def value_terms(c, x):
            dqs, dk, dv, score, d_glast = x
            _, rows = chunk_rows(c)
            doc = do_s[rows, :]
            dv_terms = [score[_slab(s)[0]:sum(_slab(s)), :] * _rows_seeing(doc, s) for s in range(cc)]
            return dqs, dk, dv + _dot(row_sums, _stack(dv_terms)), d_glast

        def store(c, x):
            dqs, dk, dv, d_glast = x
            _, rows = chunk_rows(c)
            d_g = qs_s[rows, :] * dqs - k_s[rows, :] * dk
            dlf_s[rows, :] = _seg_cumsum(d_g, cc, reverse=True) + d_glast
            dqs_s[rows, :] = dqs
            dk_s[rows, :] = dk
            dp_ref[2, rows, :] = dv.astype(BF16)

        _for_chunks(n_chunks, HG_UNROLL_BWD, through_state, pair_terms, value_terms, store)

        def finish(c, _):
            rows = pl.ds(pl.multiple_of(c * ch, ch), ch)
            z = z_ref[rows, :]
            sig = _sigmoid(z)
            nsig = _sigmoid(-z)
            fg = lb + (1.0 - lb) * sig
            _, dsilu = _silu_and_grad(q_ref[rows, :])
            dp_ref[0, rows, :] = (dqs_s[rows, :] * dsilu).astype(BF16)
            dfg = jnp.where(fg > F_MIN, dlf_s[rows, :] / fg, 0.0)
            dk = dk_s[rows, :]
            dp_ref[1, rows, :] = ((dfg - dk) * (1.0 - lb) * sig * nsig).astype(BF16)
            dlb_ref[...] += jnp.sum((dfg - dk) * nsig, axis=0, keepdims=True)
            return 0

        lax.fori_loop(0, seq // ch, finish, 0)

    seg = lambda k: pl.BlockSpec((None, seq, HG_DK), lambda h, s, k=k: (k, s, h))
    blk = pl.BlockSpec((seq, HG_DK), lambda h, s: (s, h))
    full = pltpu.VMEM((seq, HG_DK), F32)
    return _call_carrying(
        body, carry, name="hgrn_bwd", grid=(n_head, n_seq),
        in_specs=[seg(2), seg(3), seg(4), pl.BlockSpec((1, HG_DK), lambda h, s: (0, h)),
                  pl.BlockSpec((1, HG_DK), lambda h, s: (0, 0)), blk, blk],
        out_specs=[pl.BlockSpec((3, seq, HG_DK), lambda h, s: (0, s, h)),
                   pl.BlockSpec((1, HG_DK), lambda h, s: (0, h)),
                   pl.BlockSpec((1, HG_DK), lambda h, s: (0, 0))],
        out_shape=[jax.ShapeDtypeStruct((3, t_rows, d), BF16), jax.ShapeDtypeStruct((1, d), F32),
                   jax.ShapeDtypeStruct((1, HG_DK), F32)],
        scratch_shapes=[full, full, full, full, full, full, full,
                        pltpu.VMEM((n_chunks, HG_DK, HG_DK), F32), pltpu.VMEM((n_chunks, HG_DK, HG_DK), F32),
                        pltpu.VMEM((HG_DK, HG_DK), F32)],
        semantics=("arbitrary", "arbitrary"), args=(proj, proj, proj, lower_bound, hg_gain, o, d_on))


def _mesh_place():
    x, y, c = lax.axis_index("x"), lax.axis_index("y"), lax.axis_index("c")
    return x, y, c


def _peer(place, k):
    x, y, c = place
    px = 1 - x if k & 4 else x
    py = 1 - y if k & 2 else y
    pc = 1 - c if k & 1 else c
    return (px, py, pc), 4 * px + 2 * py + pc


class _Exchange:
    def __init__(self, srcs, gather):
        self.n = len(srcs)
        self.gather = gather
        self.out_shape = [jax.ShapeDtypeStruct((N_DEV,) + tuple(s.shape if gather else s.shape[1:]), s.dtype)
                          for s in srcs]
        self.scratch = [pltpu.SemaphoreType.DMA((self.n * (N_DEV - 1),)),
                        pltpu.SemaphoreType.DMA((self.n * (N_DEV - 1),)),
                        pltpu.SemaphoreType.DMA((self.n,))]

    def _copies(self, src_refs, out_refs, sems):
        send_sems, recv_sems, local_sems = sems
        place = _mesh_place()
        me = 4 * place[0] + 2 * place[1] + place[2]
        local, sends, recvs = [], [], []
        for a, (src, out) in enumerate(zip(src_refs, out_refs)):
            outgoing = (lambda idx, src=src: src) if self.gather else (lambda idx, src=src: src.at[idx])
            local.append(pltpu.make_async_copy(outgoing(me), out.at[me], local_sems.at[a]))
            for k in range(1, N_DEV):
                peer, peer_idx = _peer(place, k)
                sem = a * (N_DEV - 1) + k - 1
                sends.append(pltpu.make_async_remote_copy(
                    src_ref=outgoing(peer_idx), dst_ref=out.at[me], send_sem=send_sems.at[sem],
                    recv_sem=recv_sems.at[sem], device_id=peer, device_id_type=MESH_ID))
                recvs.append(pltpu.make_async_remote_copy(
                    src_ref=outgoing(peer_idx), dst_ref=out.at[peer_idx], send_sem=send_sems.at[sem],
                    recv_sem=recv_sems.at[sem], device_id=peer, device_id_type=MESH_ID))
        return local, sends, recvs

    def start(self, src_refs, out_refs, sems):
        local, sends, _ = self._copies(src_refs, out_refs, sems)
        for cp in local + sends:
            cp.start()

    def wait(self, src_refs, out_refs, sems):
        local, sends, recvs = self._copies(src_refs, out_refs, sems)
        for cp in recvs:
            cp.wait_recv()
        for cp in sends:
            cp.wait_send()
        for cp in local:
            cp.wait()


def _call_carrying(body, carry, *, name, grid, in_specs, out_specs, out_shape, scratch_shapes, semantics, args):
    if carry is None:
        outs = pl.pallas_call(body, name=name, grid=grid, in_specs=in_specs, out_specs=out_specs,
                              out_shape=out_shape, scratch_shapes=scratch_shapes,
                              compiler_params=_cp(*semantics))(*args)
        return outs, []
    srcs, gather = carry
    ex = _Exchange(srcs, gather)
    n, n_in, n_out, n_scr = ex.n, len(in_specs), len(out_specs), len(scratch_shapes)

    def wrapped(*refs):
        ins, refs = refs[:n_in], refs[n_in:]
        src_refs, refs = refs[:n], refs[n:]
        outs, refs = refs[:n_out], refs[n_out:]
        dst_refs, refs = refs[:n], refs[n:]
        scratch, sems = refs[:n_scr], refs[n_scr:]
        first, last = None, None
        for axis, size in enumerate(grid):
            i = pl.program_id(axis)
            first = (i == 0) if first is None else first & (i == 0)
            last = (i == size - 1) if last is None else last & (i == size - 1)

        @pl.when(first)
        def _():
            ex.start(src_refs, dst_refs, sems)

        body(*ins, *outs, *scratch)

        @pl.when(last)
        def _():
            ex.wait(src_refs, dst_refs, sems)

    any_space = pl.BlockSpec(memory_space=pl.ANY)
    res = pl.pallas_call(
        wrapped, name=name + "_carrying", grid=grid, in_specs=list(in_specs) + [any_space] * n,
        out_specs=list(out_specs) + [any_space] * n, out_shape=list(out_shape) + ex.out_shape,
        scratch_shapes=list(scratch_shapes) + ex.scratch,
        compiler_params=_cp(*(["arbitrary"] * len(grid))))(*args, *srcs)
    return res[:n_out], res[n_out:]


def _exchange(srcs, name, gather):
    ex = _Exchange(srcs, gather)
    n = ex.n

    def body(*refs):
        src_refs, out_refs, sems = refs[:n], refs[n:2 * n], refs[2 * n:]
        ex.start(src_refs, out_refs, sems)
        ex.wait(src_refs, out_refs, sems)

    any_space = pl.BlockSpec(memory_space=pl.ANY)
    return pl.pallas_call(
        body, name=name, in_specs=[any_space] * n, out_specs=[any_space] * n,
        out_shape=ex.out_shape, scratch_shapes=ex.scratch)(*srcs)


def _reduce_adamw(parts, w, m, v, name):
    rows, cols = w.shape
    n_seg = len(parts)
    seg_rows = rows // n_seg
    tr = _row_tile(seg_rows, 128)
    per_seg = seg_rows // tr
    c1 = np.float32(1.0 - ADAM_B1 ** ADAM_STEP)
    c2 = np.float32(1.0 - ADAM_B2 ** ADAM_STEP)

    def body(*refs):
        p_refs = refs[:n_seg]
        w_ref, m_ref, v_ref, g_ref, d_ref, nm_ref, nv_ref = refs[n_seg:]
        seg = pl.program_id(0)
        for k, p_ref in enumerate(p_refs):
            @pl.when(seg == k)
            def _(p_ref=p_ref):
                g = p_ref[0].astype(F32)
                for dev in range(1, N_DEV):
                    g = g + p_ref[dev].astype(F32)
                g_ref[...] = g

        g = g_ref[...]
        nm = ADAM_B1 * m_ref[...] + (1.0 - ADAM_B1) * g
        nv = ADAM_B2 * v_ref[...] + (1.0 - ADAM_B2) * (g * g)
        nm_ref[...] = nm
        nv_ref[...] = nv
        d_ref[...] = -ADAM_LR * ((nm / c1) / (jnp.sqrt(nv / c2) + ADAM_EPS) + ADAM_WD * w_ref[...])

    def part_spec(k):
        return pl.BlockSpec((N_DEV, tr, cols), lambda s, i, k=k: (0, jnp.where(s == k, i, 0), 0))

    blk = pl.BlockSpec((tr, cols), lambda s, i: (s * per_seg + i, 0))
    shp = jax.ShapeDtypeStruct((rows, cols), F32)
    return pl.pallas_call(
        body, name=name, grid=(n_seg, per_seg),
        in_specs=[part_spec(k) for k in range(n_seg)] + [blk, blk, blk],
        out_specs=[blk, blk, blk, blk], out_shape=[shp, shp, shp, shp],
        compiler_params=_cp("arbitrary", "arbitrary"))(*parts, w, m, v)


def _pack(arrays, lead=0):
    parts = []
    for a in arrays:
        f = a.reshape(a.shape[:lead] + (-1, LANES))
        pad = -f.shape[lead] % PACK_ROWS
        if pad:
            f = jnp.pad(f, [(0, 0)] * lead + [(0, pad), (0, 0)])
        parts.append(f)
    return jnp.concatenate(parts, axis=lead)


def _unpack(buf, shapes, lead=0):
    out, r = [], 0
    for shp in shapes:
        n = int(np.prod(shp)) // LANES
        part = lax.slice_in_dim(buf, r, r + n, axis=lead)
        out.append(part.reshape(buf.shape[:lead] + tuple(shp)))
        r += n + (-n % PACK_ROWS)
    return out


REPLICATED = ("lb_logits", "norm_mix", "conv_b", "b_r", "b_i", "lam", "hg_norm", "norm_mlp", "norm_final")
SMALL_SHARDED = ("conv_w", "w_r", "w_i")
LARGE_SHARDED = ("w_in", "w_out", "w_up", "w_down")
WEIGHTS = ("lb_logits", "norm_mix", "w_in", "conv_w", "conv_b", "w_r", "b_r", "w_i", "b_i", "lam", "hg_norm",
           "w_out", "norm_mlp", "w_up", "w_down", "norm_final")


def _matmul_weight_shards(p):
    depth = p["w_in"].shape[0]
    cast = {k: p[k].astype(BF16) for k in LARGE_SHARDED}
    return ([cast["w_in"][l] for l in range(depth)],
            [[cast[k][l] for k in ("w_out", "w_up", "w_down")] for l in range(depth)])


def _gathered_rest(got):
    w_out, w_up, w_down = got
    d = w_out.shape[2]
    return dict(w_out=w_out.reshape(d, d), w_up=w_up, w_down=w_down)


def _unpack_mixer_weights(small, p):
    depth, d, _ = p["w_in"].shape
    n_blk = d // RG_BLOCK_W
    conv_w, w_r, w_i = _unpack(small, [p["conv_w"].shape, p["w_r"].shape, p["w_i"].shape], lead=1)
    conv_w = conv_w.transpose(1, 2, 0, 3).reshape(depth, CONV_W, d)
    w_r = w_r.transpose(1, 2, 0, 3, 4).reshape(depth, n_blk, RG_BLOCK_W, RG_BLOCK_W).astype(BF16)
    w_i = w_i.transpose(1, 2, 0, 3, 4).reshape(depth, n_blk, RG_BLOCK_W, RG_BLOCK_W).astype(BF16)
    return conv_w, w_r, w_i


def _local_step(x, target, p):
    bl, seq, d = x.shape
    depth = p["w_in"].shape[0]
    t_rows = bl * seq
    row = lambda a, l: a[l:l + 1]
    lbs = _lower_bounds_fwd(p["lb_logits"])
    shard_in, shard_rest = _matmul_weight_shards(p)
    w_in = _exchange([shard_in[0]], "gather_w_in", gather=True)[0]
    cur = x.reshape(t_rows, d)
    saved, layers = [], []
    for l in range(depth):
        if l == 0:
            (proj, h), small = _inproj_fwd(cur, row(p["norm_mix"], l), w_in,
                                           carry=([_pack([p["conv_w"], p["w_r"], p["w_i"]])], True))
            conv_w, w_r, w_i = _unpack_mixer_weights(small[0], p)
        else:
            (proj, h), _ = _inproj_fwd(cur, row(p["norm_mix"], l), w_in)
        w = dict(w_in=w_in, conv_w=conv_w[l], w_r=w_r[l], w_i=w_i[l])
        hs, y_a = _mixer_a_fwd(proj, w["conv_w"], row(p["conv_b"], l), w["w_r"], row(p["b_r"], l), w["w_i"],
                               row(p["b_i"], l), row(p["lam"], l), seq)
        (o, o_n), got = _hgrn_fwd(proj, row(lbs, l), row(p["hg_norm"], l), seq,
                                  carry=(shard_rest[l] + ([shard_in[l + 1]] if l + 1 < depth else []), True))
        w.update(_gathered_rest(got[:3]))
        w_in = got[3] if l + 1 < depth else None
        layers.append(w)
        x_mid, y = _merge_out_fwd(proj, y_a, o_n, cur, w["w_out"])
        x_out, u, h2 = _mlp_fwd(x_mid, row(p["norm_mlp"], l), w["w_up"], w["w_down"])
        saved.append(dict(x_in=cur, proj=proj, h=h, hs=hs, y_a=y_a, o=o, o_n=o_n, x_mid=x_mid, y=y, u=u, h2=h2))
        cur = x_out
    loss8, dx, dxb, g_norm_final = _loss_head(cur, p["norm_final"].reshape(1, d), target.reshape(t_rows, d))
    small = ("norm_mix", "conv_w", "conv_b", "w_r", "b_r", "w_i", "b_i", "lam", "hg_norm", "norm_mlp")
    g = {k: [None] * depth for k in small}
    d_lbs, received = [None] * depth, [None] * depth
    g_w_in = None
    for l in reversed(range(depth)):
        s, w = saved[l], layers[l]
        dx_mid, dx_mid_b, du, act, g["norm_mlp"][l] = _mlp_bwd(dx, dxb, s["u"], s["x_mid"], row(p["norm_mlp"], l),
                                                               w["w_up"], w["w_down"])
        g_w_down = _wgrad(act, dxb[None], "wgrad_down")
        g_w_up = _wgrad(s["h2"][None], du, "wgrad_up")
        d_ya, d_on, dp_c = _outproj_bwd(dx_mid_b, w["w_out"], s["proj"], s["y_a"], s["o_n"])
        g_w_out = _wgrad(s["y"][None], dx_mid_b[None], "wgrad_out").reshape(N_DEV, d // N_DEV, d)
        (dp_b, d_lbs[l], g["hg_norm"][l]), got = _hgrn_bwd(
            s["proj"], row(lbs, l), row(p["hg_norm"], l), s["o"], d_on, seq,
            carry=([g_w_out, g_w_up, g_w_down] + ([g_w_in] if g_w_in is not None else []), False))
        received[l] = [None] + list(got[:3])
        if g_w_in is not None:
            received[l + 1][0] = got[3]
        (dp_a, g["w_r"][l], g["w_i"][l], g["b_r"][l], g["b_i"][l], g["lam"][l], g["conv_w"][l],
         g["conv_b"][l]) = _mixer_a_bwd(s["proj"], s["hs"], d_ya, w["conv_w"], row(p["conv_b"], l), w["w_r"],
                                        row(p["b_r"], l), w["w_i"], row(p["b_i"], l), row(p["lam"], l), seq)
        dx, dxb, g["norm_mix"][l] = _inproj_bwd(dx_mid, dp_a, dp_b, dp_c, w["w_in"], s["x_in"],
                                                row(p["norm_mix"], l))
        hb = s["h"][None]
        g_w_in = jnp.concatenate([_wgrad(hb, dp_a, "wgrad_in_pair"), _wgrad(hb, dp_b, "wgrad_in_triple"),
                                  _wgrad(hb, dp_c, "wgrad_in_triple")], axis=0)
    received[0][0] = _exchange([g_w_in], "scatter_grad_w_in", gather=False)[0]
    grads = {k: jnp.stack(v) for k, v in g.items()}
    for k in ("norm_mix", "conv_b", "b_r", "b_i", "lam", "hg_norm", "norm_mlp"):
        grads[k] = grads[k][:, 0]
    grads["lb_logits"] = _lower_bounds_bwd(p["lb_logits"], jnp.concatenate(d_lbs, axis=0))
    grads["norm_final"] = g_norm_final[0]
    return loss8[0, 0], dx.reshape(bl, seq, d), grads, received


def _update(p, mom1, mom2, grads, received):
    depth, d, _ = p["w_in"].shape
    d8 = d // N_DEV
    n_blk, rb = d // RG_BLOCK_W, RG_BLOCK_W // N_DEV
    out = {}

    for i, k in enumerate(LARGE_SHARDED):
        shp = p[k].shape
        flat = lambda a: a.reshape(shp[0] * shp[1], shp[2])
        parts = [received[l][i] for l in range(depth)]
        res = _reduce_adamw(parts, flat(p[k]), flat(mom1[k]), flat(mom2[k]), "adamw_" + k)
        out[k] = [r.reshape(shp) for r in res]

    small_to_dev = [
        grads["conv_w"].reshape(depth, CONV_W, N_DEV, d8).transpose(2, 0, 1, 3),
        grads["w_r"].reshape(depth, n_blk, N_DEV, rb, RG_BLOCK_W).transpose(2, 0, 1, 3, 4),
        grads["w_i"].reshape(depth, n_blk, N_DEV, rb, RG_BLOCK_W).transpose(2, 0, 1, 3, 4)]
    parts = _exchange([_pack(small_to_dev, lead=1)], "scatter_grad_mixer", gather=False)
    res = _reduce_adamw(parts, *[_pack([src[k] for k in SMALL_SHARDED]) for src in (p, mom1, mom2)],
                        "adamw_mixer")
    shapes = [p[k].shape for k in SMALL_SHARDED]
    for i, vals in enumerate(zip(*[_unpack(r, shapes) for r in res])):
        out[SMALL_SHARDED[i]] = list(vals)

    parts = _exchange([_pack([grads[k] for k in REPLICATED])], "gather_grad_replicated", gather=True)
    res = _reduce_adamw(parts, *[_pack([src[k] for k in REPLICATED]) for src in (p, mom1, mom2)],
                        "adamw_replicated")
    shapes = [p[k].shape for k in REPLICATED]
    for i, vals in enumerate(zip(*[_unpack(r, shapes) for r in res])):
        out[REPLICATED[i]] = list(vals)

    return tuple(out[k][i] for i in range(4) for k in WEIGHTS)


def kernel(x, lb_logits, norm_mix, w_in, conv_w, conv_b, w_r, b_r, w_i, b_i, lam, hg_norm, w_out, norm_mlp, w_up, w_down, norm_final, loss_target, m_lb_logits, m_norm_mix, m_w_in, m_conv_w, m_conv_b, m_w_r, m_b_r, m_w_i, m_b_i, m_lam, m_hg_norm, m_w_out, m_norm_mlp, m_w_up, m_w_down, m_norm_final, v_lb_logits, v_norm_mix, v_w_in, v_conv_w, v_conv_b, v_w_r, v_b_r, v_w_i, v_b_i, v_lam, v_hg_norm, v_w_out, v_norm_mlp, v_w_up, v_w_down, v_norm_final):
    p = dict(lb_logits=lb_logits, norm_mix=norm_mix, w_in=w_in, conv_w=conv_w, conv_b=conv_b, w_r=w_r, b_r=b_r,
             w_i=w_i, b_i=b_i, lam=lam, hg_norm=hg_norm, w_out=w_out, norm_mlp=norm_mlp, w_up=w_up,
             w_down=w_down, norm_final=norm_final)
    mom1 = dict(lb_logits=m_lb_logits, norm_mix=m_norm_mix, w_in=m_w_in, conv_w=m_conv_w, conv_b=m_conv_b,
                w_r=m_w_r, b_r=m_b_r, w_i=m_w_i, b_i=m_b_i, lam=m_lam, hg_norm=m_hg_norm, w_out=m_w_out,
                norm_mlp=m_norm_mlp, w_up=m_w_up, w_down=m_w_down, norm_final=m_norm_final)
    mom2 = dict(lb_logits=v_lb_logits, norm_mix=v_norm_mix, w_in=v_w_in, conv_w=v_conv_w, conv_b=v_conv_b,
                w_r=v_w_r, b_r=v_b_r, w_i=v_w_i, b_i=v_b_i, lam=v_lam, hg_norm=v_hg_norm, w_out=v_w_out,
                norm_mlp=v_norm_mlp, w_up=v_w_up, w_down=v_w_down, norm_final=v_norm_final)
    loss, grad_x, grads, received = _local_step(x, loss_target, p)
    loss = lax.psum(loss, ("x", "y", "c"))
    return (loss, grad_x) + _update(p, mom1, mom2, grads, received)
```

```python
import numpy as np

import jax
import jax.numpy as jnp
from jax import lax
from jax.experimental import pallas as pl
from jax.experimental.pallas import tpu as pltpu

F32 = jnp.float32
BF16 = jnp.bfloat16
MESH_ID = pl.DeviceIdType.MESH

N_DEV = 8
NORM_EPS = 1e-6
RG_C = 8.0
RG_BLOCK_W = 256
CONV_W = 4
HG_DK = 128
F_MIN = 1e-30
HG_CHUNK = 16
SUBLANES = 8
LANES = 128
PACK_ROWS = 16
ROW_CHUNK = 256
ROW_TILE_WEIGHT_STREAM = 1024
WGRAD_TOKEN_TILE = 2048
VMEM_LIMIT_V7X = 56 * 1024 * 1024

ADAM_LR = 0.001
ADAM_B1 = 0.9
ADAM_B2 = 0.999
ADAM_EPS = 1e-08
ADAM_WD = 0.01
ADAM_STEP = 10

GELU_C = 0.7978845608028654
GELU_K = 0.044715


def _cp(*sem):
    return pltpu.CompilerParams(dimension_semantics=sem, vmem_limit_bytes=VMEM_LIMIT_V7X)


def _row_tile(n, cap):
    if n <= cap:
        return n
    t = cap - cap % 16
    while n % t:
        t -= 16
    return t


def _dot(a, b):
    return jnp.dot(a, b, preferred_element_type=F32)


def _dot_nt(a, b):
    return lax.dot_general(a, b, (((1,), (1,)), ((), ())), preferred_element_type=F32)


def _dot_tn(a, b):
    return lax.dot_general(a, b, (((0,), (0,)), ((), ())), preferred_element_type=F32)


def _sigmoid(x):
    return jax.nn.sigmoid(x)


def _sigmoid_pair(x):
    e = jnp.exp(-jnp.abs(x))
    r = 1.0 / (1.0 + e)
    er = e * r
    pos = x >= 0.0
    return jnp.where(pos, r, er), jnp.where(pos, er, r)


def _log1p_pos(y):
    return jnp.where(y < 0.01, y * (1.0 - y * (0.5 - y * (1.0 / 3.0))), jnp.log(1.0 + y))


def _softplus(x):
    return jnp.maximum(x, 0.0) + _log1p_pos(jnp.exp(-jnp.abs(x)))


def _one_minus_exp(x):
    series = -x * (1.0 + x * 0.5 * (1.0 + x * (1.0 / 3.0) * (1.0 + x * 0.25 * (1.0 + x * 0.2))))
    return jnp.where(x > -0.1, series, 1.0 - jnp.exp(x))


def _gelu_and_grad(x):
    x2 = x * x
    t = jnp.tanh(GELU_C * x * (1.0 + GELU_K * x2))
    g = 0.5 * x * (1.0 + t)
    dg = 0.5 * (1.0 + t) + 0.5 * x * (1.0 - t * t) * GELU_C * (1.0 + 3.0 * GELU_K * x2)
    return g, dg


def _silu_and_grad(x):
    s = _sigmoid(x)
    return x * s, s * (1.0 + x * (1.0 - s))


def _rstd(x):
    return lax.rsqrt(jnp.mean(x * x, axis=-1, keepdims=True) + NORM_EPS)


def _rms_bwd(dh, x, g):
    rstd = _rstd(x)
    xh = x * rstd
    dxh = dh * g
    dx = rstd * (dxh - xh * jnp.mean(dxh * xh, axis=-1, keepdims=True))
    return dx, jnp.sum(dh * xh, axis=0, keepdims=True)


def _shift_rows(x, k):
    n = x.shape[0]
    k = k % n
    return x if k == 0 else pltpu.roll(x, k, axis=0)


def _seg_cumsum(x, seg, reverse=False):
    n = x.shape[0]
    rid = lax.broadcasted_iota(jnp.int32, x.shape, 0) & (seg - 1)
    d = 1
    while d < seg:
        if reverse:
            x = jnp.where(rid < seg - d, x + _shift_rows(x, n - d), x)
        else:
            x = jnp.where(rid >= d, x + _shift_rows(x, d), x)
        d *= 2
    return x


def _group_cumsum_matrix(n, seg):
    row = lax.broadcasted_iota(jnp.int32, (n, n), 0)
    col = lax.broadcasted_iota(jnp.int32, (n, n), 1)
    same_group = (row & ~(seg - 1)) == (col & ~(seg - 1))
    return jnp.where(same_group & (col <= row), 1.0, 0.0).astype(BF16)


def _group_cumsum_mxu(x, tri):
    hi = x.astype(BF16)
    lo = (x - hi.astype(F32)).astype(BF16)
    return _dot(tri, hi) + _dot(tri, lo)


def _scan_rows(a_ref, b_ref, out_ref, n_rows, width, reverse):
    rid = lax.broadcasted_iota(jnp.int32, (SUBLANES, width), 0)
    n_groups = n_rows // SUBLANES

    def group(i, carry):
        g = n_groups - 1 - i if reverse else i
        r0 = pl.multiple_of(g * SUBLANES, SUBLANES)
        a = a_ref[pl.ds(r0, SUBLANES), :]
        b = b_ref[pl.ds(r0, SUBLANES), :]
        for d in (1, 2, 4):
            if reverse:
                keep = rid < SUBLANES - d
                a_sh, b_sh = _shift_rows(a, SUBLANES - d), _shift_rows(b, SUBLANES - d)
            else:
                keep = rid >= d
                a_sh, b_sh = _shift_rows(a, d), _shift_rows(b, d)
            b = jnp.where(keep, a * b_sh + b, b)
            a = jnp.where(keep, a * a_sh, a)
        out = a * carry + b
        out_ref[pl.ds(r0, SUBLANES), :] = out
        edge = out[0:1, :] if reverse else out[SUBLANES - 1:SUBLANES, :]
        return jnp.broadcast_to(edge, (SUBLANES, width))

    lax.fori_loop(0, n_groups, group, jnp.zeros((SUBLANES, width), F32), unroll=4)


def _lb_softmax_rows(x_ref, depth):
    rows = [x_ref[pl.ds(l, 1), :] for l in range(depth)]
    top = rows[0]
    for r in rows[1:]:
        top = jnp.maximum(top, r)
    e = [jnp.exp(r - top) for r in rows]
    tot = e[0]
    for r in e[1:]:
        tot = tot + r
    return [r / tot for r in e]


def _lower_bounds_fwd(lb_logits):
    depth, d = lb_logits.shape

    def body(x_ref, o_ref):
        sm = _lb_softmax_rows(x_ref, depth)
        cum = jnp.zeros((1, d), F32)
        for l in range(depth):
            cum = cum + sm[l]
            o_ref[pl.ds(l, 1), :] = jnp.clip(cum - sm[0], 0.0, 1.0)

    return pl.pallas_call(body, name="lower_bounds_fwd",
                          out_shape=jax.ShapeDtypeStruct((depth, d), F32))(lb_logits)


def _lower_bounds_bwd(lb_logits, d_lbs):
    depth, d = lb_logits.shape

    def body(x_ref, g_ref, o_ref):
        sm = _lb_softmax_rows(x_ref, depth)
        cum = jnp.zeros((1, d), F32)
        d_cum = []
        for l in range(depth):
            cum = cum + sm[l]
            v = cum - sm[0]
            d_cum.append(jnp.where((v > 0.0) & (v < 1.0), g_ref[pl.ds(l, 1), :], 0.0))
        d_sm = []
        tail = jnp.zeros((1, d), F32)
        for l in reversed(range(depth)):
            tail = tail + d_cum[l]
            d_sm.append(tail)
        d_sm = d_sm[::-1]
        d_sm[0] = d_sm[0] - tail
        inner = jnp.zeros((1, d), F32)
        for l in range(depth):
            inner = inner + sm[l] * d_sm[l]
        for l in range(depth):
            o_ref[pl.ds(l, 1), :] = sm[l] * (d_sm[l] - inner)

    return pl.pallas_call(body, name="lower_bounds_bwd",
                          out_shape=jax.ShapeDtypeStruct((depth, d), F32))(lb_logits, d_lbs)


def _inproj_fwd(x, gain, w_seg, carry=None):
    t_rows, d = x.shape
    tm = _row_tile(t_rows, ROW_TILE_WEIGHT_STREAM)

    def body(x_ref, g_ref, w_ref, proj_ref, h_ref):
        @pl.when(pl.program_id(1) == 0)
        def _():
            xv = x_ref[...]
            h_ref[...] = (xv * _rstd(xv) * g_ref[...]).astype(BF16)

        proj_ref[...] = _dot(h_ref[...], w_ref[...])

    return _call_carrying(
        body, carry, name="inproj_fwd", grid=(t_rows // tm, N_DEV),
        in_specs=[pl.BlockSpec((tm, d), lambda i, j: (i, 0)),
                  pl.BlockSpec((1, d), lambda i, j: (0, 0)),
                  pl.BlockSpec((None, d, d), lambda i, j: (j, 0, 0))],
        out_specs=[pl.BlockSpec((None, tm, d), lambda i, j: (j, i, 0)),
                   pl.BlockSpec((tm, d), lambda i, j: (i, 0))],
        out_shape=[jax.ShapeDtypeStruct((N_DEV, t_rows, d), F32),
                   jax.ShapeDtypeStruct((t_rows, d), BF16)],
        scratch_shapes=[], semantics=("parallel", "arbitrary"), args=(x, gain, w_seg))


def _merge_out_fwd(proj, y_a, o_n, x, w_out):
    t_rows, d = x.shape
    tm = _row_tile(t_rows, 256)

    def body(g_ref, ma_ref, mb_ref, ya_ref, on_ref, x_ref, w_ref, xmid_ref, y_ref):
        g = g_ref[...]
        y = _sigmoid(ma_ref[...]) * ya_ref[...] + _sigmoid(mb_ref[...]) * (on_ref[...] * (g * _sigmoid(g)))
        yb = y.astype(BF16)
        y_ref[...] = yb
        xmid_ref[...] = x_ref[...] + _dot(yb, w_ref[...])

    seg = lambda k: pl.BlockSpec((None, tm, d), lambda i, k=k: (k, i, 0))
    row = pl.BlockSpec((tm, d), lambda i: (i, 0))
    return pl.pallas_call(
        body, name="merge_out_fwd", grid=(t_rows // tm,),
        in_specs=[seg(5), seg(6), seg(7), row, row, row, pl.BlockSpec((d, d), lambda i: (0, 0))],
        out_specs=[row, row],
        out_shape=[jax.ShapeDtypeStruct((t_rows, d), F32), jax.ShapeDtypeStruct((t_rows, d), BF16)],
        compiler_params=_cp("parallel"))(proj, proj, proj, y_a, o_n, x, w_out)


def _mlp_fwd(x_mid, gain, w_up, w_down):
    t_rows, d = x_mid.shape
    f8 = w_up.shape[2]
    tm = _row_tile(t_rows, ROW_TILE_WEIGHT_STREAM)

    def body(x_ref, g_ref, wu_ref, wd_ref, out_ref, u_ref, h_ref):
        @pl.when(pl.program_id(1) == 0)
        def _():
            xv = x_ref[...]
            h_ref[...] = (xv * _rstd(xv) * g_ref[...]).astype(BF16)
            out_ref[...] = xv

        u = _dot(h_ref[...], wu_ref[...])
        u_ref[...] = u
        r = jnp.maximum(u, 0.0)
        out_ref[...] += _dot((r * r).astype(BF16), wd_ref[...])

    row = pl.BlockSpec((tm, d), lambda i, j: (i, 0))
    return pl.pallas_call(
        body, name="mlp_fwd", grid=(t_rows // tm, N_DEV),
        in_specs=[row, pl.BlockSpec((1, d), lambda i, j: (0, 0)),
                  pl.BlockSpec((None, d, f8), lambda i, j: (j, 0, 0)),
                  pl.BlockSpec((None, f8, d), lambda i, j: (j, 0, 0))],
        out_specs=[row, pl.BlockSpec((None, tm, f8), lambda i, j: (j, i, 0)), row],
        out_shape=[jax.ShapeDtypeStruct((t_rows, d), F32),
                   jax.ShapeDtypeStruct((N_DEV, t_rows, f8), F32),
                   jax.ShapeDtypeStruct((t_rows, d), BF16)],
        compiler_params=_cp("parallel", "arbitrary"))(x_mid, gain, w_up, w_down)


def _loss_head(x, gain, target):
    t_rows, d = x.shape
    tm = _row_tile(t_rows, 512)

    def body(x_ref, g_ref, t_ref, loss_ref, dx_ref, dxb_ref, dg_ref):
        @pl.when(pl.program_id(0) == 0)
        def _():
            loss_ref[...] = jnp.zeros_like(loss_ref)
            dg_ref[...] = jnp.zeros_like(dg_ref)

        xv = x_ref[...]
        g = g_ref[...]
        err = xv * _rstd(xv) * g - t_ref[...]
        loss_ref[...] += (0.5 / d) * jnp.sum(err * err)
        dx, dg = _rms_bwd(err * (1.0 / d), xv, g)
        dx_ref[...] = dx
        dxb_ref[...] = dx.astype(BF16)
        dg_ref[...] += dg

    row = pl.BlockSpec((tm, d), lambda i: (i, 0))
    vec = pl.BlockSpec((1, d), lambda i: (0, 0))
    return pl.pallas_call(
        body, name="loss_head", grid=(t_rows // tm,),
        in_specs=[row, vec, row],
        out_specs=[pl.BlockSpec((SUBLANES, LANES), lambda i: (0, 0)), row, row, vec],
        out_shape=[jax.ShapeDtypeStruct((SUBLANES, LANES), F32),
                   jax.ShapeDtypeStruct((t_rows, d), F32),
                   jax.ShapeDtypeStruct((t_rows, d), BF16),
                   jax.ShapeDtypeStruct((1, d), F32)],
        compiler_params=_cp("arbitrary"))(x, gain, target)


def _mlp_bwd(d_out, d_out_b, u, x_mid, gain, w_up, w_down):
    t_rows, d = x_mid.shape
    f8 = w_up.shape[2]
    tm = _row_tile(t_rows, ROW_TILE_WEIGHT_STREAM)
    sub = _row_tile(tm, ROW_CHUNK)

    def body(do_ref, dob_ref, u_ref, x_ref, g_ref, wu_ref, wd_ref, dx_ref, dxb_ref, du_ref, act_ref, dg_ref):
        j = pl.program_id(1)

        @pl.when((pl.program_id(0) == 0) & (j == 0))
        def _():
            dg_ref[...] = jnp.zeros_like(dg_ref)

        @pl.when(j == 0)
        def _():
            dx_ref[...] = jnp.zeros_like(dx_ref)

        r = jnp.maximum(u_ref[...], 0.0)
        act_ref[...] = (r * r).astype(BF16)
        du = (_dot_nt(dob_ref[...], wd_ref[...]) * (2.0 * r)).astype(BF16)
        du_ref[...] = du
        dx_ref[...] += _dot_nt(du, wu_ref[...])

        @pl.when(j == N_DEV - 1)
        def _():
            def finish(c, _):
                rows = pl.ds(pl.multiple_of(c * sub, sub), sub)
                dx, dg = _rms_bwd(dx_ref[rows, :], x_ref[rows, :], g_ref[...])
                dx = dx + do_ref[rows, :]
                dx_ref[rows, :] = dx
                dxb_ref[rows, :] = dx.astype(BF16)
                dg_ref[...] += dg
                return 0

            lax.fori_loop(0, tm // sub, finish, 0)

    row = pl.BlockSpec((tm, d), lambda i, j: (i, 0))
    vec = pl.BlockSpec((1, d), lambda i, j: (0, 0))
    hid = pl.BlockSpec((None, tm, f8), lambda i, j: (j, i, 0))
    return pl.pallas_call(
        body, name="mlp_bwd", grid=(t_rows // tm, N_DEV),
        in_specs=[row, row, hid, row, vec,
                  pl.BlockSpec((None, d, f8), lambda i, j: (j, 0, 0)),
                  pl.BlockSpec((None, f8, d), lambda i, j: (j, 0, 0))],
        out_specs=[row, row, hid, hid, vec],
        out_shape=[jax.ShapeDtypeStruct((t_rows, d), F32),
                   jax.ShapeDtypeStruct((t_rows, d), BF16),
                   jax.ShapeDtypeStruct((N_DEV, t_rows, f8), BF16),
                   jax.ShapeDtypeStruct((N_DEV, t_rows, f8), BF16),
                   jax.ShapeDtypeStruct((1, d), F32)],
        compiler_params=_cp("arbitrary", "arbitrary"))(d_out, d_out_b, u, x_mid, gain, w_up, w_down)


def _outproj_bwd(dx_mid_b, w_out, proj, y_a, o_n):
    t_rows, d = y_a.shape
    tm = _row_tile(t_rows, 256)

    def body(dx_ref, w_ref, g_ref, ma_ref, mb_ref, ya_ref, on_ref, dya_ref, don_ref, dp_ref):
        dy = _dot_nt(dx_ref[...], w_ref[...])
        sa = _sigmoid(ma_ref[...])
        sb = _sigmoid(mb_ref[...])
        sg, dsg = _silu_and_grad(g_ref[...])
        ya = ya_ref[...]
        on = on_ref[...]
        dya_ref[...] = dy * sa
        t = dy * sb
        don_ref[...] = t * sg
        dp_ref[0] = (t * on * dsg).astype(BF16)
        dp_ref[1] = (dy * ya * sa * (1.0 - sa)).astype(BF16)
        dp_ref[2] = (dy * on * sg * sb * (1.0 - sb)).astype(BF16)

    seg = lambda k: pl.BlockSpec((None, tm, d), lambda i, k=k: (k, i, 0))
    row = pl.BlockSpec((tm, d), lambda i: (i, 0))
    return pl.pallas_call(
        body, name="outproj_bwd", grid=(t_rows // tm,),
        in_specs=[row, pl.BlockSpec((d, d), lambda i: (0, 0)), seg(5), seg(6), seg(7), row, row],
        out_specs=[row, row, pl.BlockSpec((3, tm, d), lambda i: (0, i, 0))],
        out_shape=[jax.ShapeDtypeStruct((t_rows, d), F32),
                   jax.ShapeDtypeStruct((t_rows, d), F32),
                   jax.ShapeDtypeStruct((3, t_rows, d), BF16)],
        compiler_params=_cp("parallel"))(dx_mid_b, w_out, proj, proj, proj, y_a, o_n)


def _inproj_bwd(dx_mid, dp_a, dp_b, dp_c, w_seg, x_in, gain):
    t_rows, d = x_in.shape
    tm = _row_tile(t_rows, ROW_TILE_WEIGHT_STREAM)
    sub = _row_tile(tm, ROW_CHUNK)
    n_a, n_b = dp_a.shape[0], dp_b.shape[0]

    def body(dxm_ref, a_ref, b_ref, c_ref, w_ref, x_ref, g_ref, dx_ref, dxb_ref, dg_ref):
        j = pl.program_id(1)

        @pl.when((pl.program_id(0) == 0) & (j == 0))
        def _():
            dg_ref[...] = jnp.zeros_like(dg_ref)

        @pl.when(j == 0)
        def _():
            dx_ref[...] = jnp.zeros_like(dx_ref)

        @pl.when(j < n_a)
        def _():
            dx_ref[...] += _dot_nt(a_ref[...], w_ref[...])

        @pl.when((j >= n_a) & (j < n_a + n_b))
        def _():
            dx_ref[...] += _dot_nt(b_ref[...], w_ref[...])

        @pl.when(j >= n_a + n_b)
        def _():
            dx_ref[...] += _dot_nt(c_ref[...], w_ref[...])

        @pl.when(j == N_DEV - 1)
        def _():
            def finish(c, _):
                rows = pl.ds(pl.multiple_of(c * sub, sub), sub)
                dx, dg = _rms_bwd(dx_ref[rows, :], x_ref[rows, :], g_ref[...])
                dx = dx + dxm_ref[rows, :]
                dx_ref[rows, :] = dx
                dxb_ref[rows, :] = dx.astype(BF16)
                dg_ref[...] += dg
                return 0

            lax.fori_loop(0, tm // sub, finish, 0)

    def part(first, n):
        return pl.BlockSpec((None, tm, d), lambda i, j: (jnp.clip(j - first, 0, n - 1), i, 0))

    row = pl.BlockSpec((tm, d), lambda i, j: (i, 0))
    vec = pl.BlockSpec((1, d), lambda i, j: (0, 0))
    return pl.pallas_call(
        body, name="inproj_bwd", grid=(t_rows // tm, N_DEV),
        in_specs=[row, part(0, n_a), part(n_a, n_b), part(n_a + n_b, dp_c.shape[0]),
                  pl.BlockSpec((None, d, d), lambda i, j: (j, 0, 0)), row, vec],
        out_specs=[row, row, vec],
        out_shape=[jax.ShapeDtypeStruct((t_rows, d), F32),
                   jax.ShapeDtypeStruct((t_rows, d), BF16),
                   jax.ShapeDtypeStruct((1, d), F32)],
        compiler_params=_cp("arbitrary", "arbitrary"))(dx_mid, dp_a, dp_b, dp_c, w_seg, x_in, gain)


def _wgrad(a3, b3, name):
    n_a, t_rows, k_a = a3.shape
    n_b, _, n_cols = b3.shape
    n = max(n_a, n_b)
    bk = _row_tile(k_a, 1024)
    bn = n_cols if n_cols <= 1024 else 1024
    tt = _row_tile(t_rows, WGRAD_TOKEN_TILE)
    n_t = t_rows // tt

    def body(a_ref, b_ref, o_ref, acc_ref):
        t, j = pl.program_id(2), pl.program_id(3)
        part = _dot_tn(a_ref[...], b_ref[...])

        @pl.when(t == 0)
        def _():
            acc_ref[j] = part

        @pl.when(t > 0)
        def _():
            acc_ref[j] += part

        @pl.when(t == n_t - 1)
        def _():
            o_ref[...] = acc_ref[j].astype(BF16)

    def out_map(p, q, t, j):
        return (jnp.where(t == n_t - 1, j, 0), p, q)

    return pl.pallas_call(
        body, name=name, grid=(k_a // bk, n_cols // bn, n_t, n),
        in_specs=[pl.BlockSpec((None, tt, bk), lambda p, q, t, j: (j if n_a > 1 else 0, t, p)),
                  pl.BlockSpec((None, tt, bn), lambda p, q, t, j: (j if n_b > 1 else 0, t, q))],
        out_specs=pl.BlockSpec((None, bk, bn), out_map),
        out_shape=jax.ShapeDtypeStruct((n, k_a, n_cols), BF16),
        scratch_shapes=[pltpu.VMEM((n, bk, bn), F32)],
        compiler_params=_cp("parallel", "parallel", "arbitrary", "arbitrary"))(a3, b3)


def _conv_taps(xe, n):
    return [_shift_rows(xe, CONV_W - 1 - j)[SUBLANES:SUBLANES + n, :] for j in range(CONV_W)]


def _rg_gates(xc, w_r, b_r, w_i, b_i, sp8):
    xb = xc.astype(BF16)
    r = _sigmoid(_dot(xb, w_r) + b_r)
    i = _sigmoid(_dot(xb, w_i) + b_i)
    return r, i


def _mixer_a_fwd(proj, conv_w, conv_b, w_r, b_r, w_i, b_i, lam, seq):
    _, t_rows, d = proj.shape
    n_seq, n_blk = t_rows // seq, d // RG_BLOCK_W
    wb = RG_BLOCK_W
    ch = _row_tile(seq, ROW_CHUNK)

    def body(xa_ref, ga_ref, cw_ref, cb_ref, wr_ref, br_ref, wi_ref, bi_ref, lam_ref, h_ref, ya_ref,
             xpad, a_s, u_s):
        xpad[0:SUBLANES, :] = jnp.zeros((SUBLANES, wb), F32)
        xpad[SUBLANES:, :] = xa_ref[...]
        sp8 = RG_C * _softplus(-lam_ref[...])

        def gates(c, _):
            r0 = pl.multiple_of(c * ch, ch)
            taps = _conv_taps(xpad[pl.ds(r0, ch + SUBLANES), :], ch)
            xc = cb_ref[...] + sum(cw_ref[pl.ds(j, 1), :] * taps[j] for j in range(CONV_W))
            r, i = _rg_gates(xc, wr_ref[...], br_ref[...], wi_ref[...], bi_ref[...], sp8)
            log_a = -(r * sp8)
            a_s[pl.ds(r0, ch), :] = jnp.exp(log_a)
            u_s[pl.ds(r0, ch), :] = jnp.sqrt(jnp.maximum(_one_minus_exp(2.0 * log_a), 0.0)) * (i * xc)
            return 0

        lax.fori_loop(0, seq // ch, gates, 0)
        _scan_rows(a_s, u_s, h_ref, seq, wb, reverse=False)

        def gate_out(c, _):
            r0 = pl.multiple_of(c * ch, ch)
            gl, _ = _gelu_and_grad(ga_ref[pl.ds(r0, ch), :])
            ya_ref[pl.ds(r0, ch), :] = h_ref[pl.ds(r0, ch), :] * gl
            return 0

        lax.fori_loop(0, seq // ch, gate_out, 0)

    seg = lambda k: pl.BlockSpec((None, seq, wb), lambda s, b, k=k: (k, s, b))
    blk = pl.BlockSpec((seq, wb), lambda s, b: (s, b))
    vec = pl.BlockSpec((1, wb), lambda s, b: (0, b))
    wsp = pl.BlockSpec((None, wb, wb), lambda s, b: (b, 0, 0))
    return pl.pallas_call(
        body, name="mixer_a_fwd", grid=(n_seq, n_blk),
        in_specs=[seg(0), seg(1), pl.BlockSpec((CONV_W, wb), lambda s, b: (0, b)), vec, wsp, vec, wsp, vec, vec],
        out_specs=[blk, blk],
        out_shape=[jax.ShapeDtypeStruct((t_rows, d), F32), jax.ShapeDtypeStruct((t_rows, d), F32)],
        scratch_shapes=[pltpu.VMEM((seq + SUBLANES, wb), F32), pltpu.VMEM((seq, wb), F32),
                        pltpu.VMEM((seq, wb), F32)],
        compiler_params=_cp("parallel", "parallel"))(proj, proj, conv_w, conv_b, w_r, b_r, w_i, b_i, lam)


def _mixer_a_bwd(proj, h, d_ya, conv_w, conv_b, w_r, b_r, w_i, b_i, lam, seq):
    _, t_rows, d = proj.shape
    n_seq, n_blk = t_rows // seq, d // RG_BLOCK_W
    wb = RG_BLOCK_W
    ch = _row_tile(seq, ROW_CHUNK)
    n_ch = seq // ch

    def body(xa_ref, ga_ref, h_ref, dya_ref, cw_ref, cb_ref, wr_ref, br_ref, wi_ref, bi_ref, lam_ref,
             dp_ref, dwr_ref, dwi_ref, dbr_ref, dbi_ref, dlam_ref, dcw_ref, dcb_ref,
             xpad, hpad, a_s, e_pad, g_s, xc_s, r_s, i_s, dxc_pad):
        @pl.when(pl.program_id(1) == 0)
        def _():
            for ref in (dwr_ref, dwi_ref, dbr_ref, dbi_ref, dlam_ref, dcw_ref, dcb_ref):
                ref[...] = jnp.zeros_like(ref)

        zeros8 = jnp.zeros((SUBLANES, wb), F32)
        xpad[0:SUBLANES, :] = zeros8
        xpad[SUBLANES:, :] = xa_ref[...]
        hpad[0:SUBLANES, :] = zeros8
        hpad[SUBLANES:, :] = h_ref[...]
        e_pad[seq:, :] = zeros8
        dxc_pad[seq:, :] = zeros8
        lam_v = lam_ref[...]
        sp8 = RG_C * _softplus(-lam_v)

        def recompute(c, _):
            r0 = pl.multiple_of(c * ch, ch)
            rows = pl.ds(r0, ch)
            taps = _conv_taps(xpad[pl.ds(r0, ch + SUBLANES), :], ch)
            xc = cb_ref[...] + sum(cw_ref[pl.ds(j, 1), :] * taps[j] for j in range(CONV_W))
            r, i = _rg_gates(xc, wr_ref[...], br_ref[...], wi_ref[...], bi_ref[...], sp8)
            a = jnp.exp(-(r * sp8))
            gl, dgl = _gelu_and_grad(ga_ref[rows, :])
            dya = dya_ref[rows, :]
            g = dya * gl
            dp_ref[1, rows, :] = (dya * h_ref[rows, :] * dgl).astype(BF16)
            a_s[rows, :] = a
            e_pad[rows, :] = a * g
            g_s[rows, :] = g
            xc_s[rows, :] = xc
            r_s[rows, :] = r
            i_s[rows, :] = i
            return 0

        lax.fori_loop(0, n_ch, recompute, 0)
        _scan_rows(a_s, e_pad, e_pad, seq, wb, reverse=True)

        def grads(c, _):
            r0 = pl.multiple_of(c * ch, ch)
            rows = pl.ds(r0, ch)
            halo = pl.ds(r0, ch + SUBLANES)
            dh = g_s[rows, :] + _shift_rows(e_pad[halo, :], ch + SUBLANES - 1)[0:ch, :]
            h_prev = _shift_rows(hpad[halo, :], 1)[SUBLANES:, :]
            xc, r, i = xc_s[rows, :], r_s[rows, :], i_s[rows, :]
            log_a = -(r * sp8)
            a = jnp.exp(log_a)
            om = _one_minus_exp(2.0 * log_a)
            sq = jnp.sqrt(jnp.maximum(om, 0.0))
            t1 = dh * xc
            d_i = t1 * sq
            d_la = dh * h_prev * a + jnp.where(om > 0.0, -(t1 * i) * (1.0 - om) / sq, 0.0)
            dpr = -(d_la * sp8) * r * (1.0 - r)
            dpi = d_i * i * (1.0 - i)
            dprb, dpib, xb = dpr.astype(BF16), dpi.astype(BF16), xc.astype(BF16)
            dxc = dh * sq * i + _dot_nt(dprb, wr_ref[...]) + _dot_nt(dpib, wi_ref[...])
            dwr_ref[...] += _dot_tn(xb, dprb)
            dwi_ref[...] += _dot_tn(xb, dpib)
            dbr_ref[...] += jnp.sum(dpr, axis=0, keepdims=True)
            dbi_ref[...] += jnp.sum(dpi, axis=0, keepdims=True)
            dlam_ref[...] += jnp.sum(d_la * r, axis=0, keepdims=True) * (RG_C * _sigmoid(-lam_v))
            dcb_ref[...] += jnp.sum(dxc, axis=0, keepdims=True)
            taps = _conv_taps(xpad[halo, :], ch)
            for j in range(CONV_W):
                dcw_ref[pl.ds(j, 1), :] += jnp.sum(dxc * taps[j], axis=0, keepdims=True)
            dxc_pad[rows, :] = dxc
            return 0

        lax.fori_loop(0, n_ch, grads, 0)

        def conv_bwd(c, _):
            r0 = pl.multiple_of(c * ch, ch)
            de = dxc_pad[pl.ds(r0, ch + SUBLANES), :]
            dxa = sum(cw_ref[pl.ds(j, 1), :] * _shift_rows(de, ch + SUBLANES - (CONV_W - 1 - j))[0:ch, :]
                      for j in range(CONV_W))
            dp_ref[0, pl.ds(r0, ch), :] = dxa.astype(BF16)
            return 0

        lax.fori_loop(0, n_ch, conv_bwd, 0)

    seg = lambda k: pl.BlockSpec((None, seq, wb), lambda b, s, k=k: (k, s, b))
    blk = pl.BlockSpec((seq, wb), lambda b, s: (s, b))
    vec = pl.BlockSpec((1, wb), lambda b, s: (0, b))
    taps = pl.BlockSpec((CONV_W, wb), lambda b, s: (0, b))
    wsp = pl.BlockSpec((None, wb, wb), lambda b, s: (b, 0, 0))
    vec_shape = jax.ShapeDtypeStruct((1, d), F32)
    w_shape = jax.ShapeDtypeStruct((n_blk, wb, wb), F32)
    pad = pltpu.VMEM((seq + SUBLANES, wb), F32)
    full = pltpu.VMEM((seq, wb), F32)
    return pl.pallas_call(
        body, name="mixer_a_bwd", grid=(n_blk, n_seq),
        in_specs=[seg(0), seg(1), blk, blk, taps, vec, wsp, vec, wsp, vec, vec],
        out_specs=[pl.BlockSpec((2, seq, wb), lambda b, s: (0, s, b)), wsp, wsp, vec, vec, vec, taps, vec],
        out_shape=[jax.ShapeDtypeStruct((2, t_rows, d), BF16), w_shape, w_shape, vec_shape, vec_shape,
                   vec_shape, jax.ShapeDtypeStruct((CONV_W, d), F32), vec_shape],
        scratch_shapes=[pad, pad, full, pad, full, full, full, full, pad],
        compiler_params=_cp("parallel", "arbitrary"))(
            proj, proj, h, d_ya, conv_w, conv_b, w_r, b_r, w_i, b_i, lam)


def _hg_prepare(q_ref, z_ref, lb, rows):
    z = z_ref[rows, :]
    sig, nsig = _sigmoid_pair(z)
    fg = lb + (1.0 - lb) * sig
    log_f = jnp.log(jnp.maximum(fg, F_MIN))
    key = (1.0 - lb) * nsig
    qs, _ = _silu_and_grad(q_ref[rows, :])
    return qs, key, log_f, sig, fg


HG_UNROLL_TERMS = 32
HG_UNROLL_FWD = 16
HG_UNROLL_BWD = 16
HG_HALF = HG_CHUNK // 2
HG_STACK = HG_HALF * HG_CHUNK + HG_HALF * HG_HALF
assert HG_HALF == SUBLANES


def _slab(s):
    if s < HG_HALF:
        return s * HG_CHUNK, HG_CHUNK
    return HG_HALF * HG_CHUNK + (s - HG_HALF) * HG_HALF, HG_HALF


def _rows_seeing(x, s):
    return x if s < HG_HALF else x[HG_HALF:, :]


def _hg_decay(g_ref, r0, g_rows, s):
    n = g_rows.shape[0]
    rid = lax.broadcasted_iota(jnp.int32, (n, HG_DK), 0) + (HG_CHUNK - n)
    gs = g_ref[pl.ds(r0 + s, 1), :]
    return jnp.where(rid >= s, jnp.exp(g_rows - gs), 0.0)


def _stack(slabs):
    return jnp.concatenate(slabs, axis=0).astype(BF16)


def _slab_row_sums():
    row = lax.broadcasted_iota(jnp.int32, (HG_CHUNK, HG_STACK), 0)
    col = lax.broadcasted_iota(jnp.int32, (HG_CHUNK, HG_STACK), 1)
    lo = jnp.where(row < HG_HALF, row * HG_CHUNK, HG_HALF * HG_CHUNK + (row - HG_HALF) * HG_HALF)
    n = jnp.where(row < HG_HALF, HG_CHUNK, HG_HALF)
    return jnp.where((col >= lo) & (col < lo + n), 1.0, 0.0).astype(BF16)


def _for_chunks(n, unroll, *stages):
    unroll = min(unroll, n)
    assert n % unroll == 0

    def trip(i, _):
        chunks = [i * unroll + u for u in range(unroll)]
        carried = [stages[0](c) for c in chunks]
        for stage in stages[1:]:
            carried = [stage(c, x) for c, x in zip(chunks, carried)]
        return 0

    lax.fori_loop(0, n // unroll, trip, 0)


def _hg_state_terms(v_ref, k_ref, g_ref, states, n_chunks):
    def issue(c):
        rows = pl.ds(pl.multiple_of(c * HG_CHUNK, HG_CHUNK), HG_CHUNK)
        gc = g_ref[rows, :]
        k_end = k_ref[rows, :] * jnp.exp(gc[HG_CHUNK - 1:HG_CHUNK, :] - gc)
        return _dot_tn(v_ref[rows, :].astype(BF16), k_end.astype(BF16))

    def store(c, term):
        states[c] = term

    _for_chunks(n_chunks, HG_UNROLL_TERMS, issue, store)


def _hg_state_chain(states, g_ref, carry_ref, n_chunks, reverse):
    unroll = min(8, n_chunks)
    assert n_chunks % unroll == 0
    carry_ref[...] = jnp.zeros_like(carry_ref)

    def trip(i, _):
        st = carry_ref[...]
        for u in range(unroll):
            k = i * unroll + u
            c = n_chunks - 1 - k if reverse else k
            term = states[c]
            states[c] = st
            st = st * jnp.exp(g_ref[pl.ds(c * HG_CHUNK + HG_CHUNK - 1, 1), :]) + term
        carry_ref[...] = st
        return 0

    lax.fori_loop(0, n_chunks // unroll, trip, 0)


def _hgrn_fwd(proj, lower_bound, hg_gain, seq, carry=None):
    _, t_rows, d = proj.shape
    n_seq, n_head = t_rows // seq, d // HG_DK
    ch = _row_tile(seq, ROW_CHUNK)
    n_chunks = seq // HG_CHUNK

    def body(q_ref, z_ref, v_ref, lb_ref, gain_ref, o_ref, on_ref, qs_s, k_s, g_s, states, st_ref):
        lb = lb_ref[...]
        tri = _group_cumsum_matrix(ch, HG_CHUNK)

        def prepare(c):
            rows = pl.ds(pl.multiple_of(c * ch, ch), ch)
            qs, key, log_f, _, _ = _hg_prepare(q_ref, z_ref, lb, rows)
            qs_s[rows, :] = qs
            k_s[rows, :] = key
            return _group_cumsum_mxu(log_f, tri)

        def store_cumsum(c, g):
            g_s[pl.ds(pl.multiple_of(c * ch, ch), ch), :] = g

        _for_chunks(seq // ch, 4, prepare, store_cumsum)
        _hg_state_terms(v_ref, k_s, g_s, states, n_chunks)
        _hg_state_chain(states, g_s, st_ref, n_chunks, reverse=False)
        ones = jnp.ones((HG_DK, HG_DK), BF16)

        def issue(c):
            r0 = pl.multiple_of(c * HG_CHUNK, HG_CHUNK)
            rows = pl.ds(r0, HG_CHUNK)
            qc, gc = qs_s[rows, :], g_s[rows, :]
            o = _dot_nt((qc * jnp.exp(gc)).astype(BF16), states[c].astype(BF16))
            pairs = [_rows_seeing(qc, s) * _hg_decay(g_s, r0, _rows_seeing(gc, s), s) * k_s[pl.ds(r0 + s, 1), :]
                     for s in range(HG_CHUNK)]
            score = _dot(_stack(pairs), ones)
            return o, score

        def combine(c, issued):
            o, score = issued
            r0 = pl.multiple_of(c * HG_CHUNK, HG_CHUNK)
            o_lo, o_hi = o[:HG_HALF, :], o[HG_HALF:, :]
            for s in range(HG_CHUNK):
                first, n = _slab(s)
                vs = v_ref[pl.ds(r0 + s, 1), :]
                if n == HG_CHUNK:
                    o_lo = o_lo + score[first:first + HG_HALF, :] * vs
                o_hi = o_hi + score[first + n - HG_HALF:first + n, :] * vs
            o_ref[pl.ds(r0, HG_CHUNK), :] = jnp.concatenate([o_lo, o_hi], axis=0)

        _for_chunks(n_chunks, HG_UNROLL_FWD, issue, combine)

        def norm(c, _):
            rows = pl.ds(pl.multiple_of(c * ch, ch), ch)
            o = o_ref[rows, :]
            on_ref[rows, :] = o * _rstd(o) * gain_ref[...]
            return 0

        lax.fori_loop(0, seq // ch, norm, 0)

    seg = lambda k: pl.BlockSpec((None, seq, HG_DK), lambda s, h, k=k: (k, s, h))
    blk = pl.BlockSpec((seq, HG_DK), lambda s, h: (s, h))
    full = pltpu.VMEM((seq, HG_DK), F32)
    return _call_carrying(
        body, carry, name="hgrn_fwd", grid=(n_seq, n_head),
        in_specs=[seg(2), seg(3), seg(4), pl.BlockSpec((1, HG_DK), lambda s, h: (0, h)),
                  pl.BlockSpec((1, HG_DK), lambda s, h: (0, 0))],
        out_specs=[blk, blk],
        out_shape=[jax.ShapeDtypeStruct((t_rows, d), F32), jax.ShapeDtypeStruct((t_rows, d), F32)],
        scratch_shapes=[full, full, full, pltpu.VMEM((n_chunks, HG_DK, HG_DK), F32),
                        pltpu.VMEM((HG_DK, HG_DK), F32)],
        semantics=("parallel", "parallel"), args=(proj, proj, proj, lower_bound, hg_gain))


def _hgrn_bwd(proj, lower_bound, hg_gain, o, d_on, seq, carry=None):
    _, t_rows, d = proj.shape
    n_seq, n_head = t_rows // seq, d // HG_DK
    ch = _row_tile(seq, ROW_CHUNK)
    n_chunks = seq // HG_CHUNK
    cc = HG_CHUNK

    def body(q_ref, z_ref, v_ref, lb_ref, gain_ref, o_ref, don_ref, dp_ref, dlb_ref, dgain_ref,
             qs_s, k_s, g_s, do_s, dqs_s, dk_s, dlf_s, states, dstates, carry_ref):
        hh, ss = pl.program_id(0), pl.program_id(1)
        lb = lb_ref[...]

        @pl.when(ss == 0)
        def _():
            dlb_ref[...] = jnp.zeros_like(dlb_ref)

        @pl.when((ss == 0) & (hh == 0))
        def _():
            dgain_ref[...] = jnp.zeros_like(dgain_ref)

        tri = _group_cumsum_matrix(ch, cc)

        def prepare(c):
            rows = pl.ds(pl.multiple_of(c * ch, ch), ch)
            qs, key, log_f, _, _ = _hg_prepare(q_ref, z_ref, lb, rows)
            qs_s[rows, :] = qs
            k_s[rows, :] = key
            do, dgain = _rms_bwd(don_ref[rows, :], o_ref[rows, :], gain_ref[...])
            do_s[rows, :] = do
            dgain_ref[...] += dgain
            return _group_cumsum_mxu(log_f, tri)

        def store_cumsum(c, g):
            g_s[pl.ds(pl.multiple_of(c * ch, ch), ch), :] = g

        _for_chunks(seq // ch, 4, prepare, store_cumsum)

        _hg_state_terms(v_ref, k_s, g_s, states, n_chunks)
        _hg_state_chain(states, g_s, carry_ref, n_chunks, reverse=False)

        def query_term(c):
            rows = pl.ds(pl.multiple_of(c * cc, cc), cc)
            q_in = qs_s[rows, :] * jnp.exp(g_s[rows, :])
            return _dot_tn(do_s[rows, :].astype(BF16), q_in.astype(BF16))

        def store_query_term(c, term):
            dstates[c] = term

        _for_chunks(n_chunks, HG_UNROLL_TERMS, query_term, store_query_term)
        _hg_state_chain(dstates, g_s, carry_ref, n_chunks, reverse=True)
        ones = jnp.ones((HG_DK, HG_DK), BF16)
        row_sums = _slab_row_sums()

        def chunk_rows(c):
            r0 = pl.multiple_of(c * cc, cc)
            return r0, pl.ds(r0, cc)

        def through_state(c):
            r0, rows = chunk_rows(c)
            kc, gc, vc, doc = k_s[rows, :], g_s[rows, :], v_ref[rows, :], do_s[rows, :]
            st, dst = states[c], dstates[c]
            g_last = gc[cc - 1:cc, :]
            e_last, e_end = jnp.exp(g_last), jnp.exp(g_last - gc)
            dob, dstb = doc.astype(BF16), dst.astype(BF16)
            dqs = _dot(dob, st.astype(BF16))
            dk_state = _dot(vc.astype(BF16), dstb)
            dv = _dot_nt((kc * e_end).astype(BF16), dstb)
            cots = [_rows_seeing(doc, s) * v_ref[pl.ds(r0 + s, 1), :] for s in range(cc)]
            d_score = _dot(_stack(cots), ones)
            return dqs, dk_state, dv, d_score, e_last * jnp.sum(dst * st, axis=0, keepdims=True)

        def pair_terms(c, x):
            dqs, dk_state, dv, d_score, d_glast = x
            r0, rows = chunk_rows(c)
            qc, kc, gc = qs_s[rows, :], k_s[rows, :], g_s[rows, :]
            dqs = dqs * jnp.exp(gc)
            dk_state = dk_state * jnp.exp(gc[cc - 1:cc, :] - gc)
            d_glast = d_glast + jnp.sum(kc * dk_state, axis=0, keepdims=True)
            dqs_lo, dqs_hi = dqs[:HG_HALF, :], dqs[HG_HALF:, :]
            pairs, dk_terms = [], []
            for s in range(cc):
                first, n = _slab(s)
                qv = _rows_seeing(qc, s)
                decay = _hg_decay(g_s, r0, _rows_seeing(gc, s), s)
                ks = k_s[pl.ds(r0 + s, 1), :]
                da_decay = d_score[first:first + n, :] * decay
                pairs.append(qv * decay * ks)
                dk_terms.append(da_decay * qv)
                dq_term = da_decay * ks
                if n == cc:
                    dqs_lo = dqs_lo + dq_term[:HG_HALF, :]
                dqs_hi = dqs_hi + dq_term[n - HG_HALF:, :]
            score = _dot(_stack(pairs), ones)
            dk = dk_state + _dot(row_sums, _stack(dk_terms))
            return jnp.concatenate([dqs_lo, dqs_hi], axis=0), dk, dv, score, d_glast

        def value_terms(c, x):
            dqs, dk, dv, score, d_glast = x
            _, rows = chunk_rows(c)
            doc = do_s[rows, :]
            dv_terms = [score[_slab(s)[0]:sum(_slab(s)), :] * _rows_seeing(doc, s) for s in range(cc)]
            return dqs, dk, dv + _dot(row_sums, _stack(dv_terms)), d_glast

        def store(c, x):
            dqs, dk, dv, d_glast = x
            _, rows = chunk_rows(c)
            d_g = qs_s[rows, :] * dqs - k_s[rows, :] * dk
            dlf_s[rows, :] = _seg_cumsum(d_g, cc, reverse=True) + d_glast
            dqs_s[rows, :] = dqs
            dk_s[rows, :] = dk
            dp_ref[2, rows, :] = dv.astype(BF16)

        _for_chunks(n_chunks, HG_UNROLL_BWD, through_state, pair_terms, value_terms, store)

        def finish(c, _):
            rows = pl.ds(pl.multiple_of(c * ch, ch), ch)
            sig, nsig = _sigmoid_pair(z_ref[rows, :])
            fg = lb + (1.0 - lb) * sig
            _, dsilu = _silu_and_grad(q_ref[rows, :])
            dp_ref[0, rows, :] = (dqs_s[rows, :] * dsilu).astype(BF16)
            dfg = jnp.where(fg > F_MIN, dlf_s[rows, :] / fg, 0.0)
            dk = dk_s[rows, :]
            dp_ref[1, rows, :] = ((dfg - dk) * (1.0 - lb) * sig * nsig).astype(BF16)
            dlb_ref[...] += jnp.sum((dfg - dk) * nsig, axis=0, keepdims=True)
            return 0

        lax.fori_loop(0, seq // ch, finish, 0)

    seg = lambda k: pl.BlockSpec((None, seq, HG_DK), lambda h, s, k=k: (k, s, h))
    blk = pl.BlockSpec((seq, HG_DK), lambda h, s: (s, h))
    full = pltpu.VMEM((seq, HG_DK), F32)
    return _call_carrying(
        body, carry, name="hgrn_bwd", grid=(n_head, n_seq),
        in_specs=[seg(2), seg(3), seg(4), pl.BlockSpec((1, HG_DK), lambda h, s: (0, h)),
                  pl.BlockSpec((1, HG_DK), lambda h, s: (0, 0)), blk, blk],
        out_specs=[pl.BlockSpec((3, seq, HG_DK), lambda h, s: (0, s, h)),
                   pl.BlockSpec((1, HG_DK), lambda h, s: (0, h)),
                   pl.BlockSpec((1, HG_DK), lambda h, s: (0, 0))],
        out_shape=[jax.ShapeDtypeStruct((3, t_rows, d), BF16), jax.ShapeDtypeStruct((1, d), F32),
                   jax.ShapeDtypeStruct((1, HG_DK), F32)],
        scratch_shapes=[full, full, full, full, full, full, full,
                        pltpu.VMEM((n_chunks, HG_DK, HG_DK), F32), pltpu.VMEM((n_chunks, HG_DK, HG_DK), F32),
                        pltpu.VMEM((HG_DK, HG_DK), F32)],
        semantics=("arbitrary", "arbitrary"), args=(proj, proj, proj, lower_bound, hg_gain, o, d_on))


def _mesh_place():
    x, y, c = lax.axis_index("x"), lax.axis_index("y"), lax.axis_index("c")
    return x, y, c


def _peer(place, k):
    x, y, c = place
    px = 1 - x if k & 4 else x
    py = 1 - y if k & 2 else y
    pc = 1 - c if k & 1 else c
    return (px, py, pc), 4 * px + 2 * py + pc


class _Exchange:
    def __init__(self, srcs, gather):
        self.n = len(srcs)
        self.gather = gather
        self.out_shape = [jax.ShapeDtypeStruct((N_DEV,) + tuple(s.shape if gather else s.shape[1:]), s.dtype)
                          for s in srcs]
        self.scratch = [pltpu.SemaphoreType.DMA((self.n * (N_DEV - 1),)),
                        pltpu.SemaphoreType.DMA((self.n * (N_DEV - 1),)),
                        pltpu.SemaphoreType.DMA((self.n,))]

    def _copies(self, src_refs, out_refs, sems):
        send_sems, recv_sems, local_sems = sems
        place = _mesh_place()
        me = 4 * place[0] + 2 * place[1] + place[2]
        local, sends, recvs = [], [], []
        for a, (src, out) in enumerate(zip(src_refs, out_refs)):
            outgoing = (lambda idx, src=src: src) if self.gather else (lambda idx, src=src: src.at[idx])
            local.append(pltpu.make_async_copy(outgoing(me), out.at[me], local_sems.at[a]))
            for k in range(1, N_DEV):
                peer, peer_idx = _peer(place, k)
                sem = a * (N_DEV - 1) + k - 1
                sends.append(pltpu.make_async_remote_copy(
                    src_ref=outgoing(peer_idx), dst_ref=out.at[me], send_sem=send_sems.at[sem],
                    recv_sem=recv_sems.at[sem], device_id=peer, device_id_type=MESH_ID))
                recvs.append(pltpu.make_async_remote_copy(
                    src_ref=outgoing(peer_idx), dst_ref=out.at[peer_idx], send_sem=send_sems.at[sem],
                    recv_sem=recv_sems.at[sem], device_id=peer, device_id_type=MESH_ID))
        return local, sends, recvs

    def start(self, src_refs, out_refs, sems):
        local, sends, _ = self._copies(src_refs, out_refs, sems)
        for cp in local + sends:
            cp.start()

    def wait(self, src_refs, out_refs, sems):
        local, sends, recvs = self._copies(src_refs, out_refs, sems)
        for cp in recvs:
            cp.wait_recv()
        for cp in sends:
            cp.wait_send()
        for cp in local:
            cp.wait()


def _call_carrying(body, carry, *, name, grid, in_specs, out_specs, out_shape, scratch_shapes, semantics, args):
    if carry is None:
        outs = pl.pallas_call(body, name=name, grid=grid, in_specs=in_specs, out_specs=out_specs,
                              out_shape=out_shape, scratch_shapes=scratch_shapes,
                              compiler_params=_cp(*semantics))(*args)
        return outs, []
    srcs, gather = carry
    ex = _Exchange(srcs, gather)
    n, n_in, n_out, n_scr = ex.n, len(in_specs), len(out_specs), len(scratch_shapes)

    def wrapped(*refs):
        ins, refs = refs[:n_in], refs[n_in:]
        src_refs, refs = refs[:n], refs[n:]
        outs, refs = refs[:n_out], refs[n_out:]
        dst_refs, refs = refs[:n], refs[n:]
        scratch, sems = refs[:n_scr], refs[n_scr:]
        first, last = None, None
        for axis, size in enumerate(grid):
            i = pl.program_id(axis)
            first = (i == 0) if first is None else first & (i == 0)
            last = (i == size - 1) if last is None else last & (i == size - 1)

        @pl.when(first)
        def _():
            ex.start(src_refs, dst_refs, sems)

        body(*ins, *outs, *scratch)

        @pl.when(last)
        def _():
            ex.wait(src_refs, dst_refs, sems)

    any_space = pl.BlockSpec(memory_space=pl.ANY)
    res = pl.pallas_call(
        wrapped, name=name + "_carrying", grid=grid, in_specs=list(in_specs) + [any_space] * n,
        out_specs=list(out_specs) + [any_space] * n, out_shape=list(out_shape) + ex.out_shape,
        scratch_shapes=list(scratch_shapes) + ex.scratch,
        compiler_params=_cp(*(["arbitrary"] * len(grid))))(*args, *srcs)
    return res[:n_out], res[n_out:]


def _exchange(srcs, name, gather):
    ex = _Exchange(srcs, gather)
    n = ex.n

    def body(*refs):
        src_refs, out_refs, sems = refs[:n], refs[n:2 * n], refs[2 * n:]
        ex.start(src_refs, out_refs, sems)
        ex.wait(src_refs, out_refs, sems)

    any_space = pl.BlockSpec(memory_space=pl.ANY)
    return pl.pallas_call(
        body, name=name, in_specs=[any_space] * n, out_specs=[any_space] * n,
        out_shape=ex.out_shape, scratch_shapes=ex.scratch)(*srcs)


def _reduce_adamw(parts, w, m, v, name):
    rows, cols = w.shape
    n_seg = len(parts)
    seg_rows = rows // n_seg
    tr = _row_tile(seg_rows, 128)
    per_seg = seg_rows // tr
    c1 = np.float32(1.0 - ADAM_B1 ** ADAM_STEP)
    c2 = np.float32(1.0 - ADAM_B2 ** ADAM_STEP)

    def body(*refs):
        p_refs = refs[:n_seg]
        w_ref, m_ref, v_ref, g_ref, d_ref, nm_ref, nv_ref = refs[n_seg:]
        seg = pl.program_id(0)
        for k, p_ref in enumerate(p_refs):
            @pl.when(seg == k)
            def _(p_ref=p_ref):
                g = p_ref[0].astype(F32)
                for dev in range(1, N_DEV):
                    g = g + p_ref[dev].astype(F32)
                g_ref[...] = g

        g = g_ref[...]
        nm = ADAM_B1 * m_ref[...] + (1.0 - ADAM_B1) * g
        nv = ADAM_B2 * v_ref[...] + (1.0 - ADAM_B2) * (g * g)
        nm_ref[...] = nm
        nv_ref[...] = nv
        d_ref[...] = -ADAM_LR * ((nm / c1) / (jnp.sqrt(nv / c2) + ADAM_EPS) + ADAM_WD * w_ref[...])

    def part_spec(k):
        return pl.BlockSpec((N_DEV, tr, cols), lambda s, i, k=k: (0, jnp.where(s == k, i, 0), 0))

    blk = pl.BlockSpec((tr, cols), lambda s, i: (s * per_seg + i, 0))
    shp = jax.ShapeDtypeStruct((rows, cols), F32)
    return pl.pallas_call(
        body, name=name, grid=(n_seg, per_seg),
        in_specs=[part_spec(k) for k in range(n_seg)] + [blk, blk, blk],
        out_specs=[blk, blk, blk, blk], out_shape=[shp, shp, shp, shp],
        compiler_params=_cp("arbitrary", "arbitrary"))(*parts, w, m, v)


def _pack(arrays, lead=0):
    parts = []
    for a in arrays:
        f = a.reshape(a.shape[:lead] + (-1, LANES))
        pad = -f.shape[lead] % PACK_ROWS
        if pad:
            f = jnp.pad(f, [(0, 0)] * lead + [(0, pad), (0, 0)])
        parts.append(f)
    return jnp.concatenate(parts, axis=lead)


def _unpack(buf, shapes, lead=0):
    out, r = [], 0
    for shp in shapes:
        n = int(np.prod(shp)) // LANES
        part = lax.slice_in_dim(buf, r, r + n, axis=lead)
        out.append(part.reshape(buf.shape[:lead] + tuple(shp)))
        r += n + (-n % PACK_ROWS)
    return out


REPLICATED = ("lb_logits", "norm_mix", "conv_b", "b_r", "b_i", "lam", "hg_norm", "norm_mlp", "norm_final")
SMALL_SHARDED = ("conv_w", "w_r", "w_i")
LARGE_SHARDED = ("w_in", "w_out", "w_up", "w_down")
WEIGHTS = ("lb_logits", "norm_mix", "w_in", "conv_w", "conv_b", "w_r", "b_r", "w_i", "b_i", "lam", "hg_norm",
           "w_out", "norm_mlp", "w_up", "w_down", "norm_final")


def _matmul_weight_shards(p):
    depth = p["w_in"].shape[0]
    cast = {k: p[k].astype(BF16) for k in LARGE_SHARDED}
    return ([cast["w_in"][l] for l in range(depth)],
            [[cast[k][l] for k in ("w_out", "w_up", "w_down")] for l in range(depth)])


def _gathered_rest(got):
    w_out, w_up, w_down = got
    d = w_out.shape[2]
    return dict(w_out=w_out.reshape(d, d), w_up=w_up, w_down=w_down)


def _unpack_mixer_weights(small, p):
    depth, d, _ = p["w_in"].shape
    n_blk = d // RG_BLOCK_W
    conv_w, w_r, w_i = _unpack(small, [p["conv_w"].shape, p["w_r"].shape, p["w_i"].shape], lead=1)
    conv_w = conv_w.transpose(1, 2, 0, 3).reshape(depth, CONV_W, d)
    w_r = w_r.transpose(1, 2, 0, 3, 4).reshape(depth, n_blk, RG_BLOCK_W, RG_BLOCK_W).astype(BF16)
    w_i = w_i.transpose(1, 2, 0, 3, 4).reshape(depth, n_blk, RG_BLOCK_W, RG_BLOCK_W).astype(BF16)
    return conv_w, w_r, w_i


def _local_step(x, target, p):
    bl, seq, d = x.shape
    depth = p["w_in"].shape[0]
    t_rows = bl * seq
    row = lambda a, l: a[l:l + 1]
    lbs = _lower_bounds_fwd(p["lb_logits"])
    shard_in, shard_rest = _matmul_weight_shards(p)
    w_in = _exchange([shard_in[0]], "gather_w_in", gather=True)[0]
    cur = x.reshape(t_rows, d)
    saved, layers = [], []
    for l in range(depth):
        if l == 0:
            (proj, h), small = _inproj_fwd(cur, row(p["norm_mix"], l), w_in,
                                           carry=([_pack([p["conv_w"], p["w_r"], p["w_i"]])], True))
            conv_w, w_r, w_i = _unpack_mixer_weights(small[0], p)
        else:
            (proj, h), _ = _inproj_fwd(cur, row(p["norm_mix"], l), w_in)
        w = dict(w_in=w_in, conv_w=conv_w[l], w_r=w_r[l], w_i=w_i[l])
        hs, y_a = _mixer_a_fwd(proj, w["conv_w"], row(p["conv_b"], l), w["w_r"], row(p["b_r"], l), w["w_i"],
                               row(p["b_i"], l), row(p["lam"], l), seq)
        (o, o_n), got = _hgrn_fwd(proj, row(lbs, l), row(p["hg_norm"], l), seq,
                                  carry=(shard_rest[l] + ([shard_in[l + 1]] if l + 1 < depth else []), True))
        w.update(_gathered_rest(got[:3]))
        w_in = got[3] if l + 1 < depth else None
        layers.append(w)
        x_mid, y = _merge_out_fwd(proj, y_a, o_n, cur, w["w_out"])
        x_out, u, h2 = _mlp_fwd(x_mid, row(p["norm_mlp"], l), w["w_up"], w["w_down"])
        saved.append(dict(x_in=cur, proj=proj, h=h, hs=hs, y_a=y_a, o=o, o_n=o_n, x_mid=x_mid, y=y, u=u, h2=h2))
        cur = x_out
    loss8, dx, dxb, g_norm_final = _loss_head(cur, p["norm_final"].reshape(1, d), target.reshape(t_rows, d))
    small = ("norm_mix", "conv_w", "conv_b", "w_r", "b_r", "w_i", "b_i", "lam", "hg_norm", "norm_mlp")
    g = {k: [None] * depth for k in small}
    d_lbs, received = [None] * depth, [None] * depth
    g_w_in = None
    for l in reversed(range(depth)):
        s, w = saved[l], layers[l]
        dx_mid, dx_mid_b, du, act, g["norm_mlp"][l] = _mlp_bwd(dx, dxb, s["u"], s["x_mid"], row(p["norm_mlp"], l),
                                                               w["w_up"], w["w_down"])
        g_w_down = _wgrad(act, dxb[None], "wgrad_down")
        g_w_up = _wgrad(s["h2"][None], du, "wgrad_up")
        d_ya, d_on, dp_c = _outproj_bwd(dx_mid_b, w["w_out"], s["proj"], s["y_a"], s["o_n"])
        g_w_out = _wgrad(s["y"][None], dx_mid_b[None], "wgrad_out").reshape(N_DEV, d // N_DEV, d)
        (dp_b, d_lbs[l], g["hg_norm"][l]), got = _hgrn_bwd(
            s["proj"], row(lbs, l), row(p["hg_norm"], l), s["o"], d_on, seq,
            carry=([g_w_out, g_w_up, g_w_down] + ([g_w_in] if g_w_in is not None else []), False))
        received[l] = [None] + list(got[:3])
        if g_w_in is not None:
            received[l + 1][0] = got[3]
        (dp_a, g["w_r"][l], g["w_i"][l], g["b_r"][l], g["b_i"][l], g["lam"][l], g["conv_w"][l],
         g["conv_b"][l]) = _mixer_a_bwd(s["proj"], s["hs"], d_ya, w["conv_w"], row(p["conv_b"], l), w["w_r"],
                                        row(p["b_r"], l), w["w_i"], row(p["b_i"], l), row(p["lam"], l), seq)
        dx, dxb, g["norm_mix"][l] = _inproj_bwd(dx_mid, dp_a, dp_b, dp_c, w["w_in"], s["x_in"],
                                                row(p["norm_mix"], l))
        hb = s["h"][None]
        g_w_in = jnp.concatenate([_wgrad(hb, dp_a, "wgrad_in_pair"), _wgrad(hb, dp_b, "wgrad_in_triple"),
                                  _wgrad(hb, dp_c, "wgrad_in_triple")], axis=0)
    received[0][0] = _exchange([g_w_in], "scatter_grad_w_in", gather=False)[0]
    grads = {k: jnp.stack(v) for k, v in g.items()}
    for k in ("norm_mix", "conv_b", "b_r", "b_i", "lam", "hg_norm", "norm_mlp"):
        grads[k] = grads[k][:, 0]
    grads["lb_logits"] = _lower_bounds_bwd(p["lb_logits"], jnp.concatenate(d_lbs, axis=0))
    grads["norm_final"] = g_norm_final[0]
    return loss8[0, 0], dx.reshape(bl, seq, d), grads, received


def _update(p, mom1, mom2, grads, received):
    depth, d, _ = p["w_in"].shape
    d8 = d // N_DEV
    n_blk, rb = d // RG_BLOCK_W, RG_BLOCK_W // N_DEV
    out = {}

    for i, k in enumerate(LARGE_SHARDED):
        shp = p[k].shape
        flat = lambda a: a.reshape(shp[0] * shp[1], shp[2])
        parts = [received[l][i] for l in range(depth)]
        res = _reduce_adamw(parts, flat(p[k]), flat(mom1[k]), flat(mom2[k]), "adamw_" + k)
        out[k] = [r.reshape(shp) for r in res]

    small_to_dev = [
        grads["conv_w"].reshape(depth, CONV_W, N_DEV, d8).transpose(2, 0, 1, 3),
        grads["w_r"].reshape(depth, n_blk, N_DEV, rb, RG_BLOCK_W).transpose(2, 0, 1, 3, 4),
        grads["w_i"].reshape(depth, n_blk, N_DEV, rb, RG_BLOCK_W).transpose(2, 0, 1, 3, 4)]
    parts = _exchange([_pack(small_to_dev, lead=1)], "scatter_grad_mixer", gather=False)
    res = _reduce_adamw(parts, *[_pack([src[k] for k in SMALL_SHARDED]) for src in (p, mom1, mom2)],
                        "adamw_mixer")
    shapes = [p[k].shape for k in SMALL_SHARDED]
    for i, vals in enumerate(zip(*[_unpack(r, shapes) for r in res])):
        out[SMALL_SHARDED[i]] = list(vals)

    parts = _exchange([_pack([grads[k] for k in REPLICATED])], "gather_grad_replicated", gather=True)
    res = _reduce_adamw(parts, *[_pack([src[k] for k in REPLICATED]) for src in (p, mom1, mom2)],
                        "adamw_replicated")
    shapes = [p[k].shape for k in REPLICATED]
    for i, vals in enumerate(zip(*[_unpack(r, shapes) for r in res])):
        out[REPLICATED[i]] = list(vals)

    return tuple(out[k][i] for i in range(4) for k in WEIGHTS)


def kernel(x, lb_logits, norm_mix, w_in, conv_w, conv_b, w_r, b_r, w_i, b_i, lam, hg_norm, w_out, norm_mlp, w_up, w_down, norm_final, loss_target, m_lb_logits, m_norm_mix, m_w_in, m_conv_w, m_conv_b, m_w_r, m_b_r, m_w_i, m_b_i, m_lam, m_hg_norm, m_w_out, m_norm_mlp, m_w_up, m_w_down, m_norm_final, v_lb_logits, v_norm_mix, v_w_in, v_conv_w, v_conv_b, v_w_r, v_b_r, v_w_i, v_b_i, v_lam, v_hg_norm, v_w_out, v_norm_mlp, v_w_up, v_w_down, v_norm_final):
    p = dict(lb_logits=lb_logits, norm_mix=norm_mix, w_in=w_in, conv_w=conv_w, conv_b=conv_b, w_r=w_r, b_r=b_r,
             w_i=w_i, b_i=b_i, lam=lam, hg_norm=hg_norm, w_out=w_out, norm_mlp=norm_mlp, w_up=w_up,
             w_down=w_down, norm_final=norm_final)
    mom1 = dict(lb_logits=m_lb_logits, norm_mix=m_norm_mix, w_in=m_w_in, conv_w=m_conv_w, conv_b=m_conv_b,
                w_r=m_w_r, b_r=m_b_r, w_i=m_w_i, b_i=m_b_i, lam=m_lam, hg_norm=m_hg_norm, w_out=m_w_out,
                norm_mlp=m_norm_mlp, w_up=m_w_up, w_down=m_w_down, norm_final=m_norm_final)
    mom2 = dict(lb_logits=v_lb_logits, norm_mix=v_norm_mix, w_in=v_w_in, conv_w=v_conv_w, conv_b=v_conv_b,
                w_r=v_w_r, b_r=v_b_r, w_i=v_w_i, b_i=v_b_i, lam=v_lam, hg_norm=v_hg_norm, w_out=v_w_out,
                norm_mlp=v_norm_mlp, w_up=v_w_up, w_down=v_w_down, norm_final=v_norm_final)
    loss, grad_x, grads, received = _local_step(x, loss_target, p)
    loss = lax.psum(loss, ("x", "y", "c"))
    return (loss, grad_x) + _update(p, mom1, mom2, grads, received)
```

```python
import numpy as np

import jax
import jax.numpy as jnp
from jax import lax
from jax.experimental import pallas as pl
from jax.experimental.pallas import tpu as pltpu

F32 = jnp.float32
BF16 = jnp.bfloat16
MESH_ID = pl.DeviceIdType.MESH

N_DEV = 8
NORM_EPS = 1e-6
RG_C = 8.0
RG_BLOCK_W = 256
CONV_W = 4
HG_DK = 128
F_MIN = 1e-30
HG_CHUNK = 16
SUBLANES = 8
LANES = 128
PACK_ROWS = 16
SCAN_GROUP = 16
ROW_CHUNK = 256
ROW_TILE_WEIGHT_STREAM = 1024
WGRAD_TOKEN_TILE = 2048
VMEM_LIMIT_V7X = 56 * 1024 * 1024

ADAM_LR = 0.001
ADAM_B1 = 0.9
ADAM_B2 = 0.999
ADAM_EPS = 1e-08
ADAM_WD = 0.01
ADAM_STEP = 10

GELU_C = 0.7978845608028654
GELU_K = 0.044715


def _cp(*sem):
    return pltpu.CompilerParams(dimension_semantics=sem, vmem_limit_bytes=VMEM_LIMIT_V7X)


def _row_tile(n, cap):
    if n <= cap:
        return n
    t = cap - cap % 16
    while n % t:
        t -= 16
    return t


def _dot(a, b):
    return jnp.dot(a, b, preferred_element_type=F32)


def _dot_nt(a, b):
    return lax.dot_general(a, b, (((1,), (1,)), ((), ())), preferred_element_type=F32)


def _dot_tn(a, b):
    return lax.dot_general(a, b, (((0,), (0,)), ((), ())), preferred_element_type=F32)


def _sigmoid(x):
    return jax.nn.sigmoid(x)


def _sigmoid_pair(x):
    e = jnp.exp(-jnp.abs(x))
    r = 1.0 / (1.0 + e)
    er = e * r
    pos = x >= 0.0
    return jnp.where(pos, r, er), jnp.where(pos, er, r)


def _log1p_pos(y):
    return jnp.where(y < 0.01, y * (1.0 - y * (0.5 - y * (1.0 / 3.0))), jnp.log(1.0 + y))


def _softplus(x):
    return jnp.maximum(x, 0.0) + _log1p_pos(jnp.exp(-jnp.abs(x)))


def _one_minus_exp(x):
    series = -x * (1.0 + x * 0.5 * (1.0 + x * (1.0 / 3.0) * (1.0 + x * 0.25 * (1.0 + x * 0.2))))
    return jnp.where(x > -0.1, series, 1.0 - jnp.exp(x))


def _gelu_and_grad(x):
    x2 = x * x
    t = jnp.tanh(GELU_C * x * (1.0 + GELU_K * x2))
    g = 0.5 * x * (1.0 + t)
    dg = 0.5 * (1.0 + t) + 0.5 * x * (1.0 - t * t) * GELU_C * (1.0 + 3.0 * GELU_K * x2)
    return g, dg


def _silu_and_grad(x):
    s = _sigmoid(x)
    return x * s, s * (1.0 + x * (1.0 - s))


def _rstd(x):
    return lax.rsqrt(jnp.mean(x * x, axis=-1, keepdims=True) + NORM_EPS)


def _rms_bwd(dh, x, g):
    rstd = _rstd(x)
    xh = x * rstd
    dxh = dh * g
    dx = rstd * (dxh - xh * jnp.mean(dxh * xh, axis=-1, keepdims=True))
    return dx, jnp.sum(dh * xh, axis=0, keepdims=True)


def _shift_rows(x, k):
    n = x.shape[0]
    k = k % n
    return x if k == 0 else pltpu.roll(x, k, axis=0)


def _seg_cumsum(x, seg, reverse=False):
    n = x.shape[0]
    rid = lax.broadcasted_iota(jnp.int32, x.shape, 0) & (seg - 1)
    d = 1
    while d < seg:
        if reverse:
            x = jnp.where(rid < seg - d, x + _shift_rows(x, n - d), x)
        else:
            x = jnp.where(rid >= d, x + _shift_rows(x, d), x)
        d *= 2
    return x


def _group_cumsum_matrix(n, seg):
    row = lax.broadcasted_iota(jnp.int32, (n, n), 0)
    col = lax.broadcasted_iota(jnp.int32, (n, n), 1)
    same_group = (row & ~(seg - 1)) == (col & ~(seg - 1))
    return jnp.where(same_group & (col <= row), 1.0, 0.0).astype(BF16)


def _group_cumsum_mxu(x, tri):
    hi = x.astype(BF16)
    lo = (x - hi.astype(F32)).astype(BF16)
    return _dot(tri, hi) + _dot(tri, lo)


def _scan_rows(a_ref, b_ref, out_ref, n_rows, width, reverse):
    gr = min(SCAN_GROUP, n_rows)
    rid = lax.broadcasted_iota(jnp.int32, (gr, width), 0)
    n_groups = n_rows // gr
    per_trip = min(4, n_groups)
    assert n_groups % per_trip == 0

    def local_scan(g):
        r0 = pl.multiple_of(g * gr, gr)
        a = a_ref[pl.ds(r0, gr), :]
        b = b_ref[pl.ds(r0, gr), :]
        d = 1
        while d < gr:
            if reverse:
                keep = rid < gr - d
                a_sh, b_sh = _shift_rows(a, gr - d), _shift_rows(b, gr - d)
            else:
                keep = rid >= d
                a_sh, b_sh = _shift_rows(a, d), _shift_rows(b, d)
            b = jnp.where(keep, a * b_sh + b, b)
            a = jnp.where(keep, a * a_sh, a)
            d *= 2
        return r0, a, b

    def trip(i, carry):
        first = i * per_trip
        groups = [n_groups - 1 - (first + u) if reverse else first + u for u in range(per_trip)]
        for r0, a, b in [local_scan(g) for g in groups]:
            out = a * carry + b
            out_ref[pl.ds(r0, gr), :] = out
            edge = out[0:1, :] if reverse else out[gr - 1:gr, :]
            carry = jnp.broadcast_to(edge, (gr, width))
        return carry

    lax.fori_loop(0, n_groups // per_trip, trip, jnp.zeros((gr, width), F32))


def _lb_softmax_rows(x_ref, depth):
    rows = [x_ref[pl.ds(l, 1), :] for l in range(depth)]
    top = rows[0]
    for r in rows[1:]:
        top = jnp.maximum(top, r)
    e = [jnp.exp(r - top) for r in rows]
    tot = e[0]
    for r in e[1:]:
        tot = tot + r
    return [r / tot for r in e]


def _lower_bounds_fwd(lb_logits):
    depth, d = lb_logits.shape

    def body(x_ref, o_ref):
        sm = _lb_softmax_rows(x_ref, depth)
        cum = jnp.zeros((1, d), F32)
        for l in range(depth):
            cum = cum + sm[l]
            o_ref[pl.ds(l, 1), :] = jnp.clip(cum - sm[0], 0.0, 1.0)

    return pl.pallas_call(body, name="lower_bounds_fwd",
                          out_shape=jax.ShapeDtypeStruct((depth, d), F32))(lb_logits)


def _lower_bounds_bwd(lb_logits, d_lbs):
    depth, d = lb_logits.shape

    def body(x_ref, g_ref, o_ref):
        sm = _lb_softmax_rows(x_ref, depth)
        cum = jnp.zeros((1, d), F32)
        d_cum = []
        for l in range(depth):
            cum = cum + sm[l]
            v = cum - sm[0]
            d_cum.append(jnp.where((v > 0.0) & (v < 1.0), g_ref[pl.ds(l, 1), :], 0.0))
        d_sm = []
        tail = jnp.zeros((1, d), F32)
        for l in reversed(range(depth)):
            tail = tail + d_cum[l]
            d_sm.append(tail)
        d_sm = d_sm[::-1]
        d_sm[0] = d_sm[0] - tail
        inner = jnp.zeros((1, d), F32)
        for l in range(depth):
            inner = inner + sm[l] * d_sm[l]
        for l in range(depth):
            o_ref[pl.ds(l, 1), :] = sm[l] * (d_sm[l] - inner)

    return pl.pallas_call(body, name="lower_bounds_bwd",
                          out_shape=jax.ShapeDtypeStruct((depth, d), F32))(lb_logits, d_lbs)


def _inproj_fwd(x, gain, w_seg, carry=None):
    t_rows, d = x.shape
    tm = _row_tile(t_rows, ROW_TILE_WEIGHT_STREAM)

    def body(x_ref, g_ref, w_ref, proj_ref, h_ref):
        @pl.when(pl.program_id(1) == 0)
        def _():
            xv = x_ref[...]
            h_ref[...] = (xv * _rstd(xv) * g_ref[...]).astype(BF16)

        proj_ref[...] = _dot(h_ref[...], w_ref[...])

    return _call_carrying(
        body, carry, name="inproj_fwd", grid=(t_rows // tm, N_DEV),
        in_specs=[pl.BlockSpec((tm, d), lambda i, j: (i, 0)),
                  pl.BlockSpec((1, d), lambda i, j: (0, 0)),
                  pl.BlockSpec((None, d, d), lambda i, j: (j, 0, 0))],
        out_specs=[pl.BlockSpec((None, tm, d), lambda i, j: (j, i, 0)),
                   pl.BlockSpec((tm, d), lambda i, j: (i, 0))],
        out_shape=[jax.ShapeDtypeStruct((N_DEV, t_rows, d), F32),
                   jax.ShapeDtypeStruct((t_rows, d), BF16)],
        scratch_shapes=[], semantics=("parallel", "arbitrary"), args=(x, gain, w_seg))


def _merge_out_fwd(proj, y_a, o_n, x, w_out):
    t_rows, d = x.shape
    tm = _row_tile(t_rows, 256)

    def body(g_ref, ma_ref, mb_ref, ya_ref, on_ref, x_ref, w_ref, xmid_ref, y_ref):
        g = g_ref[...]
        y = _sigmoid(ma_ref[...]) * ya_ref[...] + _sigmoid(mb_ref[...]) * (on_ref[...] * (g * _sigmoid(g)))
        yb = y.astype(BF16)
        y_ref[...] = yb
        xmid_ref[...] = x_ref[...] + _dot(yb, w_ref[...])

    seg = lambda k: pl.BlockSpec((None, tm, d), lambda i, k=k: (k, i, 0))
    row = pl.BlockSpec((tm, d), lambda i: (i, 0))
    return pl.pallas_call(
        body, name="merge_out_fwd", grid=(t_rows // tm,),
        in_specs=[seg(5), seg(6), seg(7), row, row, row, pl.BlockSpec((d, d), lambda i: (0, 0))],
        out_specs=[row, row],
        out_shape=[jax.ShapeDtypeStruct((t_rows, d), F32), jax.ShapeDtypeStruct((t_rows, d), BF16)],
        compiler_params=_cp("parallel"))(proj, proj, proj, y_a, o_n, x, w_out)


def _mlp_fwd(x_mid, gain, w_up, w_down):
    t_rows, d = x_mid.shape
    f8 = w_up.shape[2]
    tm = _row_tile(t_rows, ROW_TILE_WEIGHT_STREAM)

    def body(x_ref, g_ref, wu_ref, wd_ref, out_ref, u_ref, h_ref):
        @pl.when(pl.program_id(1) == 0)
        def _():
            xv = x_ref[...]
            h_ref[...] = (xv * _rstd(xv) * g_ref[...]).astype(BF16)
            out_ref[...] = xv

        u = _dot(h_ref[...], wu_ref[...])
        u_ref[...] = u
        r = jnp.maximum(u, 0.0)
        out_ref[...] += _dot((r * r).astype(BF16), wd_ref[...])

    row = pl.BlockSpec((tm, d), lambda i, j: (i, 0))
    return pl.pallas_call(
        body, name="mlp_fwd", grid=(t_rows // tm, N_DEV),
        in_specs=[row, pl.BlockSpec((1, d), lambda i, j: (0, 0)),
                  pl.BlockSpec((None, d, f8), lambda i, j: (j, 0, 0)),
                  pl.BlockSpec((None, f8, d), lambda i, j: (j, 0, 0))],
        out_specs=[row, pl.BlockSpec((None, tm, f8), lambda i, j: (j, i, 0)), row],
        out_shape=[jax.ShapeDtypeStruct((t_rows, d), F32),
                   jax.ShapeDtypeStruct((N_DEV, t_rows, f8), F32),
                   jax.ShapeDtypeStruct((t_rows, d), BF16)],
        compiler_params=_cp("parallel", "arbitrary"))(x_mid, gain, w_up, w_down)


def _loss_head(x, gain, target):
    t_rows, d = x.shape
    tm = _row_tile(t_rows, 512)

    def body(x_ref, g_ref, t_ref, loss_ref, dx_ref, dxb_ref, dg_ref):
        @pl.when(pl.program_id(0) == 0)
        def _():
            loss_ref[...] = jnp.zeros_like(loss_ref)
            dg_ref[...] = jnp.zeros_like(dg_ref)

        xv = x_ref[...]
        g = g_ref[...]
        err = xv * _rstd(xv) * g - t_ref[...]
        loss_ref[...] += (0.5 / d) * jnp.sum(err * err)
        dx, dg = _rms_bwd(err * (1.0 / d), xv, g)
        dx_ref[...] = dx
        dxb_ref[...] = dx.astype(BF16)
        dg_ref[...] += dg

    row = pl.BlockSpec((tm, d), lambda i: (i, 0))
    vec = pl.BlockSpec((1, d), lambda i: (0, 0))
    return pl.pallas_call(
        body, name="loss_head", grid=(t_rows // tm,),
        in_specs=[row, vec, row],
        out_specs=[pl.BlockSpec((SUBLANES, LANES), lambda i: (0, 0)), row, row, vec],
        out_shape=[jax.ShapeDtypeStruct((SUBLANES, LANES), F32),
                   jax.ShapeDtypeStruct((t_rows, d), F32),
                   jax.ShapeDtypeStruct((t_rows, d), BF16),
                   jax.ShapeDtypeStruct((1, d), F32)],
        compiler_params=_cp("arbitrary"))(x, gain, target)


def _mlp_bwd(d_out, d_out_b, u, x_mid, gain, w_up, w_down):
    t_rows, d = x_mid.shape
    f8 = w_up.shape[2]
    tm = _row_tile(t_rows, ROW_TILE_WEIGHT_STREAM)
    sub = _row_tile(tm, ROW_CHUNK)

    def body(do_ref, dob_ref, u_ref, x_ref, g_ref, wu_ref, wd_ref, dx_ref, dxb_ref, du_ref, act_ref, dg_ref):
        j = pl.program_id(1)

        @pl.when((pl.program_id(0) == 0) & (j == 0))
        def _():
            dg_ref[...] = jnp.zeros_like(dg_ref)

        @pl.when(j == 0)
        def _():
            dx_ref[...] = jnp.zeros_like(dx_ref)

        r = jnp.maximum(u_ref[...], 0.0)
        act_ref[...] = (r * r).astype(BF16)
        du = (_dot_nt(dob_ref[...], wd_ref[...]) * (2.0 * r)).astype(BF16)
        du_ref[...] = du
        dx_ref[...] += _dot_nt(du, wu_ref[...])

        @pl.when(j == N_DEV - 1)
        def _():
            def finish(c, _):
                rows = pl.ds(pl.multiple_of(c * sub, sub), sub)
                dx, dg = _rms_bwd(dx_ref[rows, :], x_ref[rows, :], g_ref[...])
                dx = dx + do_ref[rows, :]
                dx_ref[rows, :] = dx
                dxb_ref[rows, :] = dx.astype(BF16)
                dg_ref[...] += dg
                return 0

            lax.fori_loop(0, tm // sub, finish, 0)

    row = pl.BlockSpec((tm, d), lambda i, j: (i, 0))
    vec = pl.BlockSpec((1, d), lambda i, j: (0, 0))
    hid = pl.BlockSpec((None, tm, f8), lambda i, j: (j, i, 0))
    return pl.pallas_call(
        body, name="mlp_bwd", grid=(t_rows // tm, N_DEV),
        in_specs=[row, row, hid, row, vec,
                  pl.BlockSpec((None, d, f8), lambda i, j: (j, 0, 0)),
                  pl.BlockSpec((None, f8, d), lambda i, j: (j, 0, 0))],
        out_specs=[row, row, hid, hid, vec],
        out_shape=[jax.ShapeDtypeStruct((t_rows, d), F32),
                   jax.ShapeDtypeStruct((t_rows, d), BF16),
                   jax.ShapeDtypeStruct((N_DEV, t_rows, f8), BF16),
                   jax.ShapeDtypeStruct((N_DEV, t_rows, f8), BF16),
                   jax.ShapeDtypeStruct((1, d), F32)],
        compiler_params=_cp("arbitrary", "arbitrary"))(d_out, d_out_b, u, x_mid, gain, w_up, w_down)


def _outproj_bwd(dx_mid_b, w_out, proj, y_a, o_n):
    t_rows, d = y_a.shape
    tm = _row_tile(t_rows, 256)

    def body(dx_ref, w_ref, g_ref, ma_ref, mb_ref, ya_ref, on_ref, dya_ref, don_ref, dp_ref):
        dy = _dot_nt(dx_ref[...], w_ref[...])
        sa = _sigmoid(ma_ref[...])
        sb = _sigmoid(mb_ref[...])
        sg, dsg = _silu_and_grad(g_ref[...])
        ya = ya_ref[...]
        on = on_ref[...]
        dya_ref[...] = dy * sa
        t = dy * sb
        don_ref[...] = t * sg
        dp_ref[0] = (t * on * dsg).astype(BF16)
        dp_ref[1] = (dy * ya * sa * (1.0 - sa)).astype(BF16)
        dp_ref[2] = (dy * on * sg * sb * (1.0 - sb)).astype(BF16)

    seg = lambda k: pl.BlockSpec((None, tm, d), lambda i, k=k: (k, i, 0))
    row = pl.BlockSpec((tm, d), lambda i: (i, 0))
    return pl.pallas_call(
        body, name="outproj_bwd", grid=(t_rows // tm,),
        in_specs=[row, pl.BlockSpec((d, d), lambda i: (0, 0)), seg(5), seg(6), seg(7), row, row],
        out_specs=[row, row, pl.BlockSpec((3, tm, d), lambda i: (0, i, 0))],
        out_shape=[jax.ShapeDtypeStruct((t_rows, d), F32),
                   jax.ShapeDtypeStruct((t_rows, d), F32),
                   jax.ShapeDtypeStruct((3, t_rows, d), BF16)],
        compiler_params=_cp("parallel"))(dx_mid_b, w_out, proj, proj, proj, y_a, o_n)


def _inproj_bwd(dx_mid, dp_a, dp_b, dp_c, w_seg, x_in, gain, carry=None):
    t_rows, d = x_in.shape
    tm = _row_tile(t_rows, ROW_TILE_WEIGHT_STREAM)
    sub = _row_tile(tm, ROW_CHUNK)
    n_a, n_b = dp_a.shape[0], dp_b.shape[0]

    def body(dxm_ref, a_ref, b_ref, c_ref, w_ref, x_ref, g_ref, dx_ref, dxb_ref, dg_ref):
        j = pl.program_id(1)

        @pl.when((pl.program_id(0) == 0) & (j == 0))
        def _():
            dg_ref[...] = jnp.zeros_like(dg_ref)

        @pl.when(j == 0)
        def _():
            dx_ref[...] = jnp.zeros_like(dx_ref)

        @pl.when(j < n_a)
        def _():
            dx_ref[...] += _dot_nt(a_ref[...], w_ref[...])

        @pl.when((j >= n_a) & (j < n_a + n_b))
        def _():
            dx_ref[...] += _dot_nt(b_ref[...], w_ref[...])

        @pl.when(j >= n_a + n_b)
        def _():
            dx_ref[...] += _dot_nt(c_ref[...], w_ref[...])

        @pl.when(j == N_DEV - 1)
        def _():
            def finish(c, _):
                rows = pl.ds(pl.multiple_of(c * sub, sub), sub)
                dx, dg = _rms_bwd(dx_ref[rows, :], x_ref[rows, :], g_ref[...])
                dx = dx + dxm_ref[rows, :]
                dx_ref[rows, :] = dx
                dxb_ref[rows, :] = dx.astype(BF16)
                dg_ref[...] += dg
                return 0

            lax.fori_loop(0, tm // sub, finish, 0)

    def part(first, n):
        return pl.BlockSpec((None, tm, d), lambda i, j: (jnp.clip(j - first, 0, n - 1), i, 0))

    row = pl.BlockSpec((tm, d), lambda i, j: (i, 0))
    vec = pl.BlockSpec((1, d), lambda i, j: (0, 0))
    return _call_carrying(
        body, carry, name="inproj_bwd", grid=(t_rows // tm, N_DEV),
        in_specs=[row, part(0, n_a), part(n_a, n_b), part(n_a + n_b, dp_c.shape[0]),
                  pl.BlockSpec((None, d, d), lambda i, j: (j, 0, 0)), row, vec],
        out_specs=[row, row, vec],
        out_shape=[jax.ShapeDtypeStruct((t_rows, d), F32),
                   jax.ShapeDtypeStruct((t_rows, d), BF16),
                   jax.ShapeDtypeStruct((1, d), F32)],
        scratch_shapes=[], semantics=("arbitrary", "arbitrary"),
        args=(dx_mid, dp_a, dp_b, dp_c, w_seg, x_in, gain))


def _wgrad(a3, b3, name):
    n_a, t_rows, k_a = a3.shape
    n_b, _, n_cols = b3.shape
    n = max(n_a, n_b)
    bk = _row_tile(k_a, 1024)
    bn = n_cols if n_cols <= 1024 else 1024
    tt = _row_tile(t_rows, WGRAD_TOKEN_TILE)
    n_t = t_rows // tt

    def body(a_ref, b_ref, o_ref, acc_ref):
        t, j = pl.program_id(2), pl.program_id(3)
        part = _dot_tn(a_ref[...], b_ref[...])

        @pl.when(t == 0)
        def _():
            acc_ref[j] = part

        @pl.when(t > 0)
        def _():
            acc_ref[j] += part

        @pl.when(t == n_t - 1)
        def _():
            o_ref[...] = acc_ref[j].astype(BF16)

    def out_map(p, q, t, j):
        return (jnp.where(t == n_t - 1, j, 0), p, q)

    return pl.pallas_call(
        body, name=name, grid=(k_a // bk, n_cols // bn, n_t, n),
        in_specs=[pl.BlockSpec((None, tt, bk), lambda p, q, t, j: (j if n_a > 1 else 0, t, p)),
                  pl.BlockSpec((None, tt, bn), lambda p, q, t, j: (j if n_b > 1 else 0, t, q))],
        out_specs=pl.BlockSpec((None, bk, bn), out_map),
        out_shape=jax.ShapeDtypeStruct((n, k_a, n_cols), BF16),
        scratch_shapes=[pltpu.VMEM((n, bk, bn), F32)],
        compiler_params=_cp("parallel", "parallel", "arbitrary", "arbitrary"))(a3, b3)


def _conv_taps(xe, n):
    return [_shift_rows(xe, CONV_W - 1 - j)[SUBLANES:SUBLANES + n, :] for j in range(CONV_W)]


def _rg_gates(xc, w_r, b_r, w_i, b_i, sp8):
    xb = xc.astype(BF16)
    r = _sigmoid(_dot(xb, w_r) + b_r)
    i = _sigmoid(_dot(xb, w_i) + b_i)
    return r, i


def _mixer_a_fwd(proj, conv_w, conv_b, w_r, b_r, w_i, b_i, lam, seq):
    _, t_rows, d = proj.shape
    n_seq, n_blk = t_rows // seq, d // RG_BLOCK_W
    wb = RG_BLOCK_W
    ch = _row_tile(seq, ROW_CHUNK)

    def body(xa_ref, ga_ref, cw_ref, cb_ref, wr_ref, br_ref, wi_ref, bi_ref, lam_ref, h_ref, ya_ref,
             xpad, a_s, u_s):
        xpad[0:SUBLANES, :] = jnp.zeros((SUBLANES, wb), F32)
        xpad[SUBLANES:, :] = xa_ref[...]
        sp8 = RG_C * _softplus(-lam_ref[...])

        def gates(c, _):
            r0 = pl.multiple_of(c * ch, ch)
            taps = _conv_taps(xpad[pl.ds(r0, ch + SUBLANES), :], ch)
            xc = cb_ref[...] + sum(cw_ref[pl.ds(j, 1), :] * taps[j] for j in range(CONV_W))
            r, i = _rg_gates(xc, wr_ref[...], br_ref[...], wi_ref[...], bi_ref[...], sp8)
            log_a = -(r * sp8)
            a_s[pl.ds(r0, ch), :] = jnp.exp(log_a)
            u_s[pl.ds(r0, ch), :] = jnp.sqrt(jnp.maximum(_one_minus_exp(2.0 * log_a), 0.0)) * (i * xc)
            return 0

        lax.fori_loop(0, seq // ch, gates, 0)
        _scan_rows(a_s, u_s, h_ref, seq, wb, reverse=False)

        def gate_out(c, _):
            r0 = pl.multiple_of(c * ch, ch)
            gl, _ = _gelu_and_grad(ga_ref[pl.ds(r0, ch), :])
            ya_ref[pl.ds(r0, ch), :] = h_ref[pl.ds(r0, ch), :] * gl
            return 0

        lax.fori_loop(0, seq // ch, gate_out, 0)

    seg = lambda k: pl.BlockSpec((None, seq, wb), lambda s, b, k=k: (k, s, b))
    blk = pl.BlockSpec((seq, wb), lambda s, b: (s, b))
    vec = pl.BlockSpec((1, wb), lambda s, b: (0, b))
    wsp = pl.BlockSpec((None, wb, wb), lambda s, b: (b, 0, 0))
    return pl.pallas_call(
        body, name="mixer_a_fwd", grid=(n_seq, n_blk),
        in_specs=[seg(0), seg(1), pl.BlockSpec((CONV_W, wb), lambda s, b: (0, b)), vec, wsp, vec, wsp, vec, vec],
        out_specs=[blk, blk],
        out_shape=[jax.ShapeDtypeStruct((t_rows, d), F32), jax.ShapeDtypeStruct((t_rows, d), F32)],
        scratch_shapes=[pltpu.VMEM((seq + SUBLANES, wb), F32), pltpu.VMEM((seq, wb), F32),
                        pltpu.VMEM((seq, wb), F32)],
        compiler_params=_cp("parallel", "parallel"))(proj, proj, conv_w, conv_b, w_r, b_r, w_i, b_i, lam)


def _mixer_a_bwd(proj, h, d_ya, conv_w, conv_b, w_r, b_r, w_i, b_i, lam, seq):
    _, t_rows, d = proj.shape
    n_seq, n_blk = t_rows // seq, d // RG_BLOCK_W
    wb = RG_BLOCK_W
    ch = _row_tile(seq, ROW_CHUNK)
    n_ch = seq // ch

    def body(xa_ref, ga_ref, h_ref, dya_ref, cw_ref, cb_ref, wr_ref, br_ref, wi_ref, bi_ref, lam_ref,
             dp_ref, dwr_ref, dwi_ref, dbr_ref, dbi_ref, dlam_ref, dcw_ref, dcb_ref,
             xpad, hpad, a_s, e_pad, g_s, xc_s, r_s, i_s, dxc_pad):
        @pl.when(pl.program_id(1) == 0)
        def _():
            for ref in (dwr_ref, dwi_ref, dbr_ref, dbi_ref, dlam_ref, dcw_ref, dcb_ref):
                ref[...] = jnp.zeros_like(ref)

        zeros8 = jnp.zeros((SUBLANES, wb), F32)
        xpad[0:SUBLANES, :] = zeros8
        xpad[SUBLANES:, :] = xa_ref[...]
        hpad[0:SUBLANES, :] = zeros8
        hpad[SUBLANES:, :] = h_ref[...]
        e_pad[seq:, :] = zeros8
        dxc_pad[seq:, :] = zeros8
        lam_v = lam_ref[...]
        sp8 = RG_C * _softplus(-lam_v)

        def recompute(c, _):
            r0 = pl.multiple_of(c * ch, ch)
            rows = pl.ds(r0, ch)
            taps = _conv_taps(xpad[pl.ds(r0, ch + SUBLANES), :], ch)
            xc = cb_ref[...] + sum(cw_ref[pl.ds(j, 1), :] * taps[j] for j in range(CONV_W))
            r, i = _rg_gates(xc, wr_ref[...], br_ref[...], wi_ref[...], bi_ref[...], sp8)
            a = jnp.exp(-(r * sp8))
            gl, dgl = _gelu_and_grad(ga_ref[rows, :])
            dya = dya_ref[rows, :]
            g = dya * gl
            dp_ref[1, rows, :] = (dya * h_ref[rows, :] * dgl).astype(BF16)
            a_s[rows, :] = a
            e_pad[rows, :] = a * g
            g_s[rows, :] = g
            xc_s[rows, :] = xc
            r_s[rows, :] = r
            i_s[rows, :] = i
            return 0

        lax.fori_loop(0, n_ch, recompute, 0)
        _scan_rows(a_s, e_pad, e_pad, seq, wb, reverse=True)

        def grads(c, _):
            r0 = pl.multiple_of(c * ch, ch)
            rows = pl.ds(r0, ch)
            halo = pl.ds(r0, ch + SUBLANES)
            dh = g_s[rows, :] + _shift_rows(e_pad[halo, :], ch + SUBLANES - 1)[0:ch, :]
            h_prev = _shift_rows(hpad[halo, :], 1)[SUBLANES:, :]
            xc, r, i = xc_s[rows, :], r_s[rows, :], i_s[rows, :]
            log_a = -(r * sp8)
            a = jnp.exp(log_a)
            om = _one_minus_exp(2.0 * log_a)
            sq = jnp.sqrt(jnp.maximum(om, 0.0))
            t1 = dh * xc
            d_i = t1 * sq
            d_la = dh * h_prev * a + jnp.where(om > 0.0, -(t1 * i) * (1.0 - om) / sq, 0.0)
            dpr = -(d_la * sp8) * r * (1.0 - r)
            dpi = d_i * i * (1.0 - i)
            dprb, dpib, xb = dpr.astype(BF16), dpi.astype(BF16), xc.astype(BF16)
            dxc = dh * sq * i + _dot_nt(dprb, wr_ref[...]) + _dot_nt(dpib, wi_ref[...])
            dwr_ref[...] += _dot_tn(xb, dprb)
            dwi_ref[...] += _dot_tn(xb, dpib)
            dbr_ref[...] += jnp.sum(dpr, axis=0, keepdims=True)
            dbi_ref[...] += jnp.sum(dpi, axis=0, keepdims=True)
            dlam_ref[...] += jnp.sum(d_la * r, axis=0, keepdims=True) * (RG_C * _sigmoid(-lam_v))
            dcb_ref[...] += jnp.sum(dxc, axis=0, keepdims=True)
            taps = _conv_taps(xpad[halo, :], ch)
            for j in range(CONV_W):
                dcw_ref[pl.ds(j, 1), :] += jnp.sum(dxc * taps[j], axis=0, keepdims=True)
            dxc_pad[rows, :] = dxc
            return 0

        lax.fori_loop(0, n_ch, grads, 0)

        def conv_bwd(c, _):
            r0 = pl.multiple_of(c * ch, ch)
            de = dxc_pad[pl.ds(r0, ch + SUBLANES), :]
            dxa = sum(cw_ref[pl.ds(j, 1), :] * _shift_rows(de, ch + SUBLANES - (CONV_W - 1 - j))[0:ch, :]
                      for j in range(CONV_W))
            dp_ref[0, pl.ds(r0, ch), :] = dxa.astype(BF16)
            return 0

        lax.fori_loop(0, n_ch, conv_bwd, 0)

    seg = lambda k: pl.BlockSpec((None, seq, wb), lambda b, s, k=k: (k, s, b))
    blk = pl.BlockSpec((seq, wb), lambda b, s: (s, b))
    vec = pl.BlockSpec((1, wb), lambda b, s: (0, b))
    taps = pl.BlockSpec((CONV_W, wb), lambda b, s: (0, b))
    wsp = pl.BlockSpec((None, wb, wb), lambda b, s: (b, 0, 0))
    vec_shape = jax.ShapeDtypeStruct((1, d), F32)
    w_shape = jax.ShapeDtypeStruct((n_blk, wb, wb), F32)
    pad = pltpu.VMEM((seq + SUBLANES, wb), F32)
    full = pltpu.VMEM((seq, wb), F32)
    return pl.pallas_call(
        body, name="mixer_a_bwd", grid=(n_blk, n_seq),
        in_specs=[seg(0), seg(1), blk, blk, taps, vec, wsp, vec, wsp, vec, vec],
        out_specs=[pl.BlockSpec((2, seq, wb), lambda b, s: (0, s, b)), wsp, wsp, vec, vec, vec, taps, vec],
        out_shape=[jax.ShapeDtypeStruct((2, t_rows, d), BF16), w_shape, w_shape, vec_shape, vec_shape,
                   vec_shape, jax.ShapeDtypeStruct((CONV_W, d), F32), vec_shape],
        scratch_shapes=[pad, pad, full, pad, full, full, full, full, pad],
        compiler_params=_cp("parallel", "arbitrary"))(
            proj, proj, h, d_ya, conv_w, conv_b, w_r, b_r, w_i, b_i, lam)


def _hg_prepare(q_ref, z_ref, lb, rows):
    z = z_ref[rows, :]
    sig, nsig = _sigmoid_pair(z)
    fg = lb + (1.0 - lb) * sig
    log_f = jnp.log(jnp.maximum(fg, F_MIN))
    key = (1.0 - lb) * nsig
    qs, _ = _silu_and_grad(q_ref[rows, :])
    return qs, key, log_f, sig, fg


HG_UNROLL_TERMS = 32
HG_UNROLL_FWD = 16
HG_UNROLL_BWD = 16
HG_HALF = HG_CHUNK // 2
HG_STACK = HG_HALF * HG_CHUNK + HG_HALF * HG_HALF
assert HG_HALF == SUBLANES


def _slab(s):
    if s < HG_HALF:
        return s * HG_CHUNK, HG_CHUNK
    return HG_HALF * HG_CHUNK + (s - HG_HALF) * HG_HALF, HG_HALF


def _rows_seeing(x, s):
    return x if s < HG_HALF else x[HG_HALF:, :]


def _hg_decay(g_ref, r0, g_rows, s):
    n = g_rows.shape[0]
    rid = lax.broadcasted_iota(jnp.int32, (n, HG_DK), 0) + (HG_CHUNK - n)
    gs = g_ref[pl.ds(r0 + s, 1), :]
    return jnp.where(rid >= s, jnp.exp(g_rows - gs), 0.0)


def _stack(slabs):
    return jnp.concatenate(slabs, axis=0).astype(BF16)


def _slab_row_sums():
    row = lax.broadcasted_iota(jnp.int32, (HG_CHUNK, HG_STACK), 0)
    col = lax.broadcasted_iota(jnp.int32, (HG_CHUNK, HG_STACK), 1)
    lo = jnp.where(row < HG_HALF, row * HG_CHUNK, HG_HALF * HG_CHUNK + (row - HG_HALF) * HG_HALF)
    n = jnp.where(row < HG_HALF, HG_CHUNK, HG_HALF)
    return jnp.where((col >= lo) & (col < lo + n), 1.0, 0.0).astype(BF16)


def _for_chunks(n, unroll, *stages):
    unroll = min(unroll, n)
    assert n % unroll == 0

    def trip(i, _):
        chunks = [i * unroll + u for u in range(unroll)]
        carried = [stages[0](c) for c in chunks]
        for stage in stages[1:]:
            carried = [stage(c, x) for c, x in zip(chunks, carried)]
        return 0

    lax.fori_loop(0, n // unroll, trip, 0)


def _hg_state_terms(v_ref, k_ref, g_ref, states, n_chunks):
    def issue(c):
        rows = pl.ds(pl.multiple_of(c * HG_CHUNK, HG_CHUNK), HG_CHUNK)
        gc = g_ref[rows, :]
        k_end = k_ref[rows, :] * jnp.exp(gc[HG_CHUNK - 1:HG_CHUNK, :] - gc)
        return _dot_tn(v_ref[rows, :].astype(BF16), k_end.astype(BF16))

    def store(c, term):
        states[c] = term

    _for_chunks(n_chunks, HG_UNROLL_TERMS, issue, store)


def _hg_state_chain(states, g_ref, carry_ref, n_chunks, reverse):
    unroll = min(8, n_chunks)
    assert n_chunks % unroll == 0
    carry_ref[...] = jnp.zeros_like(carry_ref)

    def trip(i, _):
        st = carry_ref[...]
        for u in range(unroll):
            k = i * unroll + u
            c = n_chunks - 1 - k if reverse else k
            term = states[c]
            states[c] = st
            st = st * jnp.exp(g_ref[pl.ds(c * HG_CHUNK + HG_CHUNK - 1, 1), :]) + term
        carry_ref[...] = st
        return 0

    lax.fori_loop(0, n_chunks // unroll, trip, 0)


def _hgrn_fwd(proj, lower_bound, hg_gain, seq, carry=None):
    _, t_rows, d = proj.shape
    n_seq, n_head = t_rows // seq, d // HG_DK
    ch = _row_tile(seq, ROW_CHUNK)
    n_chunks = seq // HG_CHUNK

    def body(q_ref, z_ref, v_ref, lb_ref, gain_ref, o_ref, on_ref, qs_s, k_s, g_s, states, st_ref):
        lb = lb_ref[...]
        tri = _group_cumsum_matrix(ch, HG_CHUNK)

        def prepare(c):
            rows = pl.ds(pl.multiple_of(c * ch, ch), ch)
            qs, key, log_f, _, _ = _hg_prepare(q_ref, z_ref, lb, rows)
            qs_s[rows, :] = qs
            k_s[rows, :] = key
            return _group_cumsum_mxu(log_f, tri)

        def store_cumsum(c, g):
            g_s[pl.ds(pl.multiple_of(c * ch, ch), ch), :] = g

        _for_chunks(seq // ch, 4, prepare, store_cumsum)
        _hg_state_terms(v_ref, k_s, g_s, states, n_chunks)
        _hg_state_chain(states, g_s, st_ref, n_chunks, reverse=False)
        ones = jnp.ones((HG_DK, HG_DK), BF16)

        def issue(c):
            r0 = pl.multiple_of(c * HG_CHUNK, HG_CHUNK)
            rows = pl.ds(r0, HG_CHUNK)
            qc, gc = qs_s[rows, :], g_s[rows, :]
            o = _dot_nt((qc * jnp.exp(gc)).astype(BF16), states[c].astype(BF16))
            pairs = [_rows_seeing(qc, s) * _hg_decay(g_s, r0, _rows_seeing(gc, s), s) * k_s[pl.ds(r0 + s, 1), :]
                     for s in range(HG_CHUNK)]
            score = _dot(_stack(pairs), ones)
            return o, score

        def combine(c, issued):
            o, score = issued
            r0 = pl.multiple_of(c * HG_CHUNK, HG_CHUNK)
            o_lo, o_hi = o[:HG_HALF, :], o[HG_HALF:, :]
            for s in range(HG_CHUNK):
                first, n = _slab(s)
                vs = v_ref[pl.ds(r0 + s, 1), :]
                if n == HG_CHUNK:
                    o_lo = o_lo + score[first:first + HG_HALF, :] * vs
                o_hi = o_hi + score[first + n - HG_HALF:first + n, :] * vs
            o_ref[pl.ds(r0, HG_CHUNK), :] = jnp.concatenate([o_lo, o_hi], axis=0)

        _for_chunks(n_chunks, HG_UNROLL_FWD, issue, combine)

        def norm(c, _):
            rows = pl.ds(pl.multiple_of(c * ch, ch), ch)
            o = o_ref[rows, :]
            on_ref[rows, :] = o * _rstd(o) * gain_ref[...]
            return 0

        lax.fori_loop(0, seq // ch, norm, 0)

    seg = lambda k: pl.BlockSpec((None, seq, HG_DK), lambda s, h, k=k: (k, s, h))
    blk = pl.BlockSpec((seq, HG_DK), lambda s, h: (s, h))
    full = pltpu.VMEM((seq, HG_DK), F32)
    return _call_carrying(
        body, carry, name="hgrn_fwd", grid=(n_seq, n_head),
        in_specs=[seg(2), seg(3), seg(4), pl.BlockSpec((1, HG_DK), lambda s, h: (0, h)),
                  pl.BlockSpec((1, HG_DK), lambda s, h: (0, 0))],
        out_specs=[blk, blk],
        out_shape=[jax.ShapeDtypeStruct((t_rows, d), F32), jax.ShapeDtypeStruct((t_rows, d), F32)],
        scratch_shapes=[full, full, full, pltpu.VMEM((n_chunks, HG_DK, HG_DK), F32),
                        pltpu.VMEM((HG_DK, HG_DK), F32)],
        semantics=("parallel", "parallel"), args=(proj, proj, proj, lower_bound, hg_gain))


def _hgrn_bwd(proj, lower_bound, hg_gain, o, d_on, seq, carry=None):
    _, t_rows, d = proj.shape
    n_seq, n_head = t_rows // seq, d // HG_DK
    ch = _row_tile(seq, ROW_CHUNK)
    n_chunks = seq // HG_CHUNK
    cc = HG_CHUNK

    def body(q_ref, z_ref, v_ref, lb_ref, gain_ref, o_ref, don_ref, dp_ref, dlb_ref, dgain_ref,
             qs_s, k_s, g_s, do_s, dqs_s, dk_s, dlf_s, states, dstates, carry_ref):
        hh, ss = pl.program_id(0), pl.program_id(1)
        lb = lb_ref[...]

        @pl.when(ss == 0)
        def _():
            dlb_ref[...] = jnp.zeros_like(dlb_ref)

        @pl.when((ss == 0) & (hh == 0))
        def _():
            dgain_ref[...] = jnp.zeros_like(dgain_ref)

        tri = _group_cumsum_matrix(ch, cc)

        def prepare(c):
            rows = pl.ds(pl.multiple_of(c * ch, ch), ch)
            qs, key, log_f, _, _ = _hg_prepare(q_ref, z_ref, lb, rows)
            qs_s[rows, :] = qs
            k_s[rows, :] = key
            do, dgain = _rms_bwd(don_ref[rows, :], o_ref[rows, :], gain_ref[...])
            do_s[rows, :] = do
            dgain_ref[...] += dgain
            return _group_cumsum_mxu(log_f, tri)

        def store_cumsum(c, g):
            g_s[pl.ds(pl.multiple_of(c * ch, ch), ch), :] = g

        _for_chunks(seq // ch, 4, prepare, store_cumsum)

        _hg_state_terms(v_ref, k_s, g_s, states, n_chunks)
        _hg_state_chain(states, g_s, carry_ref, n_chunks, reverse=False)

        def query_term(c):
            rows = pl.ds(pl.multiple_of(c * cc, cc), cc)
            q_in = qs_s[rows, :] * jnp.exp(g_s[rows, :])
            return _dot_tn(do_s[rows, :].astype(BF16), q_in.astype(BF16))

        def store_query_term(c, term):
            dstates[c] = term

        _for_chunks(n_chunks, HG_UNROLL_TERMS, query_term, store_query_term)
        _hg_state_chain(dstates, g_s, carry_ref, n_chunks, reverse=True)
        ones = jnp.ones((HG_DK, HG_DK), BF16)
        row_sums = _slab_row_sums()

        def chunk_rows(c):
            r0 = pl.multiple_of(c * cc, cc)
            return r0, pl.ds(r0, cc)

        def through_state(c):
            r0, rows = chunk_rows(c)
            kc, gc, vc, doc = k_s[rows, :], g_s[rows, :], v_ref[rows, :], do_s[rows, :]
            st, dst = states[c], dstates[c]
            g_last = gc[cc - 1:cc, :]
            e_last, e_end = jnp.exp(g_last), jnp.exp(g_last - gc)
            dob, dstb = doc.astype(BF16), dst.astype(BF16)
            dqs = _dot(dob, st.astype(BF16))
            dk_state = _dot(vc.astype(BF16), dstb)
            dv = _dot_nt((kc * e_end).astype(BF16), dstb)
            cots = [_rows_seeing(doc, s) * v_ref[pl.ds(r0 + s, 1), :] for s in range(cc)]
            d_score = _dot(_stack(cots), ones)
            return dqs, dk_state, dv, d_score, e_last * jnp.sum(dst * st, axis=0, keepdims=True)

        def pair_terms(c, x):
            dqs, dk_state, dv, d_score, d_glast = x
            r0, rows = chunk_rows(c)
            qc, kc, gc = qs_s[rows, :], k_s[rows, :], g_s[rows, :]
            dqs = dqs * jnp.exp(gc)
            dk_state = dk_state * jnp.exp(gc[cc - 1:cc, :] - gc)
            d_glast = d_glast + jnp.sum(kc * dk_state, axis=0, keepdims=True)
            dqs_lo, dqs_hi = dqs[:HG_HALF, :], dqs[HG_HALF:, :]
            pairs, dk_terms = [], []
            for s in range(cc):
                first, n = _slab(s)
                qv = _rows_seeing(qc, s)
                decay = _hg_decay(g_s, r0, _rows_seeing(gc, s), s)
                ks = k_s[pl.ds(r0 + s, 1), :]
                da_decay = d_score[first:first + n, :] * decay
                pairs.append(qv * decay * ks)
                dk_terms.append(da_decay * qv)
                dq_term = da_decay * ks
                if n == cc:
                    dqs_lo = dqs_lo + dq_term[:HG_HALF, :]
                dqs_hi = dqs_hi + dq_term[n - HG_HALF:, :]
            score = _dot(_stack(pairs), ones)
            dk = dk_state + _dot(row_sums, _stack(dk_terms))
            return jnp.concatenate([dqs_lo, dqs_hi], axis=0), dk, dv, score, d_glast

        def value_terms(c, x):
            dqs, dk, dv, score, d_glast = x
            _, rows = chunk_rows(c)
            doc = do_s[rows, :]
            dv_terms = [score[_slab(s)[0]:sum(_slab(s)), :] * _rows_seeing(doc, s) for s in range(cc)]
            return dqs, dk, dv + _dot(row_sums, _stack(dv_terms)), d_glast

        def store(c, x):
            dqs, dk, dv, d_glast = x
            _, rows = chunk_rows(c)
            d_g = qs_s[rows, :] * dqs - k_s[rows, :] * dk
            dlf_s[rows, :] = _seg_cumsum(d_g, cc, reverse=True) + d_glast
            dqs_s[rows, :] = dqs
            dk_s[rows, :] = dk
            dp_ref[2, rows, :] = dv.astype(BF16)

        _for_chunks(n_chunks, HG_UNROLL_BWD, through_state, pair_terms, value_terms, store)

        def finish(c, _):
            rows = pl.ds(pl.multiple_of(c * ch, ch), ch)
            sig, nsig = _sigmoid_pair(z_ref[rows, :])
            fg = lb + (1.0 - lb) * sig
            _, dsilu = _silu_and_grad(q_ref[rows, :])
            dp_ref[0, rows, :] = (dqs_s[rows, :] * dsilu).astype(BF16)
            dfg = jnp.where(fg > F_MIN, dlf_s[rows, :] / fg, 0.0)
            dk = dk_s[rows, :]
            dp_ref[1, rows, :] = ((dfg - dk) * (1.0 - lb) * sig * nsig).astype(BF16)
            dlb_ref[...] += jnp.sum((dfg - dk) * nsig, axis=0, keepdims=True)
            return 0

        lax.fori_loop(0, seq // ch, finish, 0)

    seg = lambda k: pl.BlockSpec((None, seq, HG_DK), lambda h, s, k=k: (k, s, h))
    blk = pl.BlockSpec((seq, HG_DK), lambda h, s: (s, h))
    full = pltpu.VMEM((seq, HG_DK), F32)
    return _call_carrying(
        body, carry, name="hgrn_bwd", grid=(n_head, n_seq),
        in_specs=[seg(2), seg(3), seg(4), pl.BlockSpec((1, HG_DK), lambda h, s: (0, h)),
                  pl.BlockSpec((1, HG_DK), lambda h, s: (0, 0)), blk, blk],
        out_specs=[pl.BlockSpec((3, seq, HG_DK), lambda h, s: (0, s, h)),
                   pl.BlockSpec((1, HG_DK), lambda h, s: (0, h)),
                   pl.BlockSpec((1, HG_DK), lambda h, s: (0, 0))],
        out_shape=[jax.ShapeDtypeStruct((3, t_rows, d), BF16), jax.ShapeDtypeStruct((1, d), F32),
                   jax.ShapeDtypeStruct((1, HG_DK), F32)],
        scratch_shapes=[full, full, full, full, full, full, full,
                        pltpu.VMEM((n_chunks, HG_DK, HG_DK), F32), pltpu.VMEM((n_chunks, HG_DK, HG_DK), F32),
                        pltpu.VMEM((HG_DK, HG_DK), F32)],
        semantics=("arbitrary", "arbitrary"), args=(proj, proj, proj, lower_bound, hg_gain, o, d_on))


def _mesh_place():
    x, y, c = lax.axis_index("x"), lax.axis_index("y"), lax.axis_index("c")
    return x, y, c


def _peer(place, k):
    x, y, c = place
    px = 1 - x if k & 4 else x
    py = 1 - y if k & 2 else y
    pc = 1 - c if k & 1 else c
    return (px, py, pc), 4 * px + 2 * py + pc


class _Exchange:
    def __init__(self, srcs, gather):
        self.n = len(srcs)
        self.gather = gather
        self.out_shape = [jax.ShapeDtypeStruct((N_DEV,) + tuple(s.shape if gather else s.shape[1:]), s.dtype)
                          for s in srcs]
        self.scratch = [pltpu.SemaphoreType.DMA((self.n * (N_DEV - 1),)),
                        pltpu.SemaphoreType.DMA((self.n * (N_DEV - 1),)),
                        pltpu.SemaphoreType.DMA((self.n,))]

    def _copies(self, src_refs, out_refs, sems):
        send_sems, recv_sems, local_sems = sems
        place = _mesh_place()
        me = 4 * place[0] + 2 * place[1] + place[2]
        local, sends, recvs = [], [], []
        for a, (src, out) in enumerate(zip(src_refs, out_refs)):
            outgoing = (lambda idx, src=src: src) if self.gather else (lambda idx, src=src: src.at[idx])
            local.append(pltpu.make_async_copy(outgoing(me), out.at[me], local_sems.at[a]))
            for k in range(1, N_DEV):
                peer, peer_idx = _peer(place, k)
                sem = a * (N_DEV - 1) + k - 1
                sends.append(pltpu.make_async_remote_copy(
                    src_ref=outgoing(peer_idx), dst_ref=out.at[me], send_sem=send_sems.at[sem],
                    recv_sem=recv_sems.at[sem], device_id=peer, device_id_type=MESH_ID))
                recvs.append(pltpu.make_async_remote_copy(
                    src_ref=outgoing(peer_idx), dst_ref=out.at[peer_idx], send_sem=send_sems.at[sem],
                    recv_sem=recv_sems.at[sem], device_id=peer, device_id_type=MESH_ID))
        return local, sends, recvs

    def start(self, src_refs, out_refs, sems):
        local, sends, _ = self._copies(src_refs, out_refs, sems)
        for cp in local + sends:
            cp.start()

    def wait(self, src_refs, out_refs, sems):
        local, sends, recvs = self._copies(src_refs, out_refs, sems)
        for cp in recvs:
            cp.wait_recv()
        for cp in sends:
            cp.wait_send()
        for cp in local:
            cp.wait()


def _call_carrying(body, carry, *, name, grid, in_specs, out_specs, out_shape, scratch_shapes, semantics, args):
    if carry is None:
        outs = pl.pallas_call(body, name=name, grid=grid, in_specs=in_specs, out_specs=out_specs,
                              out_shape=out_shape, scratch_shapes=scratch_shapes,
                              compiler_params=_cp(*semantics))(*args)
        return outs, []
    srcs, gather = carry
    ex = _Exchange(srcs, gather)
    n, n_in, n_out, n_scr = ex.n, len(in_specs), len(out_specs), len(scratch_shapes)

    def wrapped(*refs):
        ins, refs = refs[:n_in], refs[n_in:]
        src_refs, refs = refs[:n], refs[n:]
        outs, refs = refs[:n_out], refs[n_out:]
        dst_refs, refs = refs[:n], refs[n:]
        scratch, sems = refs[:n_scr], refs[n_scr:]
        first, last = None, None
        for axis, size in enumerate(grid):
            i = pl.program_id(axis)
            first = (i == 0) if first is None else first & (i == 0)
            last = (i == size - 1) if last is None else last & (i == size - 1)

        @pl.when(first)
        def _():
            ex.start(src_refs, dst_refs, sems)

        body(*ins, *outs, *scratch)

        @pl.when(last)
        def _():
            ex.wait(src_refs, dst_refs, sems)

    any_space = pl.BlockSpec(memory_space=pl.ANY)
    res = pl.pallas_call(
        wrapped, name=name + "_carrying", grid=grid, in_specs=list(in_specs) + [any_space] * n,
        out_specs=list(out_specs) + [any_space] * n, out_shape=list(out_shape) + ex.out_shape,
        scratch_shapes=list(scratch_shapes) + ex.scratch,
        compiler_params=_cp(*(["arbitrary"] * len(grid))))(*args, *srcs)
    return res[:n_out], res[n_out:]


def _exchange(srcs, name, gather):
    ex = _Exchange(srcs, gather)
    n = ex.n

    def body(*refs):
        src_refs, out_refs, sems = refs[:n], refs[n:2 * n], refs[2 * n:]
        ex.start(src_refs, out_refs, sems)
        ex.wait(src_refs, out_refs, sems)

    any_space = pl.BlockSpec(memory_space=pl.ANY)
    return pl.pallas_call(
        body, name=name, in_specs=[any_space] * n, out_specs=[any_space] * n,
        out_shape=ex.out_shape, scratch_shapes=ex.scratch)(*srcs)


def _reduce_adamw(parts, w, m, v, name):
    rows, cols = w.shape
    n_seg = len(parts)
    seg_rows = rows // n_seg
    tr = _row_tile(seg_rows, 128)
    per_seg = seg_rows // tr
    c1 = np.float32(1.0 - ADAM_B1 ** ADAM_STEP)
    c2 = np.float32(1.0 - ADAM_B2 ** ADAM_STEP)

    def body(*refs):
        p_refs = refs[:n_seg]
        w_ref, m_ref, v_ref, g_ref, d_ref, nm_ref, nv_ref = refs[n_seg:]
        seg = pl.program_id(0)
        for k, p_ref in enumerate(p_refs):
            @pl.when(seg == k)
            def _(p_ref=p_ref):
                g = p_ref[0].astype(F32)
                for dev in range(1, N_DEV):
                    g = g + p_ref[dev].astype(F32)
                g_ref[...] = g

        g = g_ref[...]
        nm = ADAM_B1 * m_ref[...] + (1.0 - ADAM_B1) * g
        nv = ADAM_B2 * v_ref[...] + (1.0 - ADAM_B2) * (g * g)
        nm_ref[...] = nm
        nv_ref[...] = nv
        d_ref[...] = -ADAM_LR * ((nm / c1) / (jnp.sqrt(nv / c2) + ADAM_EPS) + ADAM_WD * w_ref[...])

    def part_spec(k):
        return pl.BlockSpec((N_DEV, tr, cols), lambda s, i, k=k: (0, jnp.where(s == k, i, 0), 0))

    blk = pl.BlockSpec((tr, cols), lambda s, i: (s * per_seg + i, 0))
    shp = jax.ShapeDtypeStruct((rows, cols), F32)
    return pl.pallas_call(
        body, name=name, grid=(n_seg, per_seg),
        in_specs=[part_spec(k) for k in range(n_seg)] + [blk, blk, blk],
        out_specs=[blk, blk, blk, blk], out_shape=[shp, shp, shp, shp],
        compiler_params=_cp("arbitrary", "arbitrary"))(*parts, w, m, v)


def _pack(arrays, lead=0):
    parts = []
    for a in arrays:
        f = a.reshape(a.shape[:lead] + (-1, LANES))
        pad = -f.shape[lead] % PACK_ROWS
        if pad:
            f = jnp.pad(f, [(0, 0)] * lead + [(0, pad), (0, 0)])
        parts.append(f)
    return jnp.concatenate(parts, axis=lead)


def _unpack(buf, shapes, lead=0):
    out, r = [], 0
    for shp in shapes:
        n = int(np.prod(shp)) // LANES
        part = lax.slice_in_dim(buf, r, r + n, axis=lead)
        out.append(part.reshape(buf.shape[:lead] + tuple(shp)))
        r += n + (-n % PACK_ROWS)
    return out


REPLICATED = ("lb_logits", "norm_mix", "conv_b", "b_r", "b_i", "lam", "hg_norm", "norm_mlp", "norm_final")
SMALL_SHARDED = ("conv_w", "w_r", "w_i")
LARGE_SHARDED = ("w_in", "w_out", "w_up", "w_down")
WEIGHTS = ("lb_logits", "norm_mix", "w_in", "conv_w", "conv_b", "w_r", "b_r", "w_i", "b_i", "lam", "hg_norm",
           "w_out", "norm_mlp", "w_up", "w_down", "norm_final")


def _matmul_weight_shards(p):
    depth = p["w_in"].shape[0]
    cast = {k: p[k].astype(BF16) for k in LARGE_SHARDED}
    return ([cast["w_in"][l] for l in range(depth)],
            [[cast[k][l] for k in ("w_out", "w_up", "w_down")] for l in range(depth)])


def _gathered_rest(got):
    w_out, w_up, w_down = got
    d = w_out.shape[2]
    return dict(w_out=w_out.reshape(d, d), w_up=w_up, w_down=w_down)


def _unpack_mixer_weights(small, p):
    depth, d, _ = p["w_in"].shape
    n_blk = d // RG_BLOCK_W
    conv_w, w_r, w_i = _unpack(small, [p["conv_w"].shape, p["w_r"].shape, p["w_i"].shape], lead=1)
    conv_w = conv_w.transpose(1, 2, 0, 3).reshape(depth, CONV_W, d)
    w_r = w_r.transpose(1, 2, 0, 3, 4).reshape(depth, n_blk, RG_BLOCK_W, RG_BLOCK_W).astype(BF16)
    w_i = w_i.transpose(1, 2, 0, 3, 4).reshape(depth, n_blk, RG_BLOCK_W, RG_BLOCK_W).astype(BF16)
    return conv_w, w_r, w_i


def _local_step(x, target, p):
    bl, seq, d = x.shape
    depth = p["w_in"].shape[0]
    t_rows = bl * seq
    row = lambda a, l: a[l:l + 1]
    lbs = _lower_bounds_fwd(p["lb_logits"])
    shard_in, shard_rest = _matmul_weight_shards(p)
    w_in = _exchange([shard_in[0]], "gather_w_in", gather=True)[0]
    cur = x.reshape(t_rows, d)
    saved, layers = [], []
    for l in range(depth):
        if l == 0:
            (proj, h), small = _inproj_fwd(cur, row(p["norm_mix"], l), w_in,
                                           carry=([_pack([p["conv_w"], p["w_r"], p["w_i"]])], True))
            conv_w, w_r, w_i = _unpack_mixer_weights(small[0], p)
        else:
            (proj, h), _ = _inproj_fwd(cur, row(p["norm_mix"], l), w_in)
        w = dict(w_in=w_in, conv_w=conv_w[l], w_r=w_r[l], w_i=w_i[l])
        hs, y_a = _mixer_a_fwd(proj, w["conv_w"], row(p["conv_b"], l), w["w_r"], row(p["b_r"], l), w["w_i"],
                               row(p["b_i"], l), row(p["lam"], l), seq)
        (o, o_n), got = _hgrn_fwd(proj, row(lbs, l), row(p["hg_norm"], l), seq,
                                  carry=(shard_rest[l] + ([shard_in[l + 1]] if l + 1 < depth else []), True))
        w.update(_gathered_rest(got[:3]))
        w_in = got[3] if l + 1 < depth else None
        layers.append(w)
        x_mid, y = _merge_out_fwd(proj, y_a, o_n, cur, w["w_out"])
        x_out, u, h2 = _mlp_fwd(x_mid, row(p["norm_mlp"], l), w["w_up"], w["w_down"])
        saved.append(dict(x_in=cur, proj=proj, h=h, hs=hs, y_a=y_a, o=o, o_n=o_n, x_mid=x_mid, y=y, u=u, h2=h2))
        cur = x_out
    loss8, dx, dxb, g_norm_final = _loss_head(cur, p["norm_final"].reshape(1, d), target.reshape(t_rows, d))
    small = ("norm_mix", "conv_w", "conv_b", "w_r", "b_r", "w_i", "b_i", "lam", "hg_norm", "norm_mlp")
    g = {k: [None] * depth for k in small}
    d_lbs, received = [None] * depth, [None] * depth
    g_w_in = None
    for l in reversed(range(depth)):
        s, w = saved[l], layers[l]
        dx_mid, dx_mid_b, du, act, g["norm_mlp"][l] = _mlp_bwd(dx, dxb, s["u"], s["x_mid"], row(p["norm_mlp"], l),
                                                               w["w_up"], w["w_down"])
        g_w_down = _wgrad(act, dxb[None], "wgrad_down")
        g_w_up = _wgrad(s["h2"][None], du, "wgrad_up")
        d_ya, d_on, dp_c = _outproj_bwd(dx_mid_b, w["w_out"], s["proj"], s["y_a"], s["o_n"])
        g_w_out = _wgrad(s["y"][None], dx_mid_b[None], "wgrad_out").reshape(N_DEV, d // N_DEV, d)
        (dp_b, d_lbs[l], g["hg_norm"][l]), got = _hgrn_bwd(
            s["proj"], row(lbs, l), row(p["hg_norm"], l), s["o"], d_on, seq,
            carry=([g_w_out, g_w_up, g_w_down] + ([g_w_in] if g_w_in is not None else []), False))
        received[l] = [None] + list(got[:3])
        if g_w_in is not None:
            received[l + 1][0] = got[3]
        (dp_a, g["w_r"][l], g["w_i"][l], g["b_r"][l], g["b_i"][l], g["lam"][l], g["conv_w"][l],
         g["conv_b"][l]) = _mixer_a_bwd(s["proj"], s["hs"], d_ya, w["conv_w"], row(p["conv_b"], l), w["w_r"],
                                        row(p["b_r"], l), w["w_i"], row(p["b_i"], l), row(p["lam"], l), seq)
        hb = s["h"][None]
        g_w_in = jnp.concatenate([_wgrad(hb, dp_a, "wgrad_in_pair"), _wgrad(hb, dp_b, "wgrad_in_triple"),
                                  _wgrad(hb, dp_c, "wgrad_in_triple")], axis=0)
        carry = None
        if l == 0:
            carry = ([g_w_in, _mixer_grads_by_owner(g, d)], False)
        (dx, dxb, g["norm_mix"][l]), got = _inproj_bwd(dx_mid, dp_a, dp_b, dp_c, w["w_in"], s["x_in"],
                                                       row(p["norm_mix"], l), carry=carry)
    received[0][0], received_mixer = got
    grads = {k: jnp.stack(v) for k, v in g.items()}
    for k in ("norm_mix", "conv_b", "b_r", "b_i", "lam", "hg_norm", "norm_mlp"):
        grads[k] = grads[k][:, 0]
    grads["lb_logits"] = _lower_bounds_bwd(p["lb_logits"], jnp.concatenate(d_lbs, axis=0))
    grads["norm_final"] = g_norm_final[0]
    return loss8[0, 0], dx.reshape(bl, seq, d), grads, received, received_mixer


def _mixer_grads_by_owner(g, d):
    d8, n_blk, rb = d // N_DEV, d // RG_BLOCK_W, RG_BLOCK_W // N_DEV
    depth = len(g["conv_w"])
    conv_w, w_r, w_i = (jnp.stack(g[k]) for k in SMALL_SHARDED)
    return _pack([conv_w.reshape(depth, CONV_W, N_DEV, d8).transpose(2, 0, 1, 3),
                  w_r.reshape(depth, n_blk, N_DEV, rb, RG_BLOCK_W).transpose(2, 0, 1, 3, 4),
                  w_i.reshape(depth, n_blk, N_DEV, rb, RG_BLOCK_W).transpose(2, 0, 1, 3, 4)], lead=1)


def _update(p, mom1, mom2, grads, received, received_mixer):
    depth = p["w_in"].shape[0]
    out = {}

    for i, k in enumerate(LARGE_SHARDED):
        shp = p[k].shape
        flat = lambda a: a.reshape(shp[0] * shp[1], shp[2])
        parts = [received[l][i] for l in range(depth)]
        res = _reduce_adamw(parts, flat(p[k]), flat(mom1[k]), flat(mom2[k]), "adamw_" + k)
        out[k] = [r.reshape(shp) for r in res]

    res = _reduce_adamw([received_mixer], *[_pack([src[k] for k in SMALL_SHARDED]) for src in (p, mom1, mom2)],
                        "adamw_mixer")
    shapes = [p[k].shape for k in SMALL_SHARDED]
    for i, vals in enumerate(zip(*[_unpack(r, shapes) for r in res])):
        out[SMALL_SHARDED[i]] = list(vals)

    parts = _exchange([_pack([grads[k] for k in REPLICATED])], "gather_grad_replicated", gather=True)
    res = _reduce_adamw(parts, *[_pack([src[k] for k in REPLICATED]) for src in (p, mom1, mom2)],
                        "adamw_replicated")
    shapes = [p[k].shape for k in REPLICATED]
    for i, vals in enumerate(zip(*[_unpack(r, shapes) for r in res])):
        out[REPLICATED[i]] = list(vals)

    return tuple(out[k][i] for i in range(4) for k in WEIGHTS)


def kernel(x, lb_logits, norm_mix, w_in, conv_w, conv_b, w_r, b_r, w_i, b_i, lam, hg_norm, w_out, norm_mlp, w_up, w_down, norm_final, loss_target, m_lb_logits, m_norm_mix, m_w_in, m_conv_w, m_conv_b, m_w_r, m_b_r, m_w_i, m_b_i, m_lam, m_hg_norm, m_w_out, m_norm_mlp, m_w_up, m_w_down, m_norm_final, v_lb_logits, v_norm_mix, v_w_in, v_conv_w, v_conv_b, v_w_r, v_b_r, v_w_i, v_b_i, v_lam, v_hg_norm, v_w_out, v_norm_mlp, v_w_up, v_w_down, v_norm_final):
    p = dict(lb_logits=lb_logits, norm_mix=norm_mix, w_in=w_in, conv_w=conv_w, conv_b=conv_b, w_r=w_r, b_r=b_r,
             w_i=w_i, b_i=b_i, lam=lam, hg_norm=hg_norm, w_out=w_out, norm_mlp=norm_mlp, w_up=w_up,
             w_down=w_down, norm_final=norm_final)
    mom1 = dict(lb_logits=m_lb_logits, norm_mix=m_norm_mix, w_in=m_w_in, conv_w=m_conv_w, conv_b=m_conv_b,
                w_r=m_w_r, b_r=m_b_r, w_i=m_w_i, b_i=m_b_i, lam=m_lam, hg_norm=m_hg_norm, w_out=m_w_out,
                norm_mlp=m_norm_mlp, w_up=m_w_up, w_down=m_w_down, norm_final=m_norm_final)
    mom2 = dict(lb_logits=v_lb_logits, norm_mix=v_norm_mix, w_in=v_w_in, conv_w=v_conv_w, conv_b=v_conv_b,
                w_r=v_w_r, b_r=v_b_r, w_i=v_w_i, b_i=v_b_i, lam=v_lam, hg_norm=v_hg_norm, w_out=v_w_out,
                norm_mlp=v_norm_mlp, w_up=v_w_up, w_down=v_w_down, norm_final=v_norm_final)
    loss, grad_x, grads, received, received_mixer = _local_step(x, loss_target, p)
    loss = lax.psum(loss, ("x", "y", "c"))
    return (loss, grad_x) + _update(p, mom1, mom2, grads, received, received_mixer)
```

```python
import numpy as np

import jax
import jax.numpy as jnp
from jax import lax
from jax.experimental import pallas as pl
from jax.experimental.pallas import tpu as pltpu

F32 = jnp.float32
BF16 = jnp.bfloat16
MESH_ID = pl.DeviceIdType.MESH

N_DEV = 8
NORM_EPS = 1e-6
RG_C = 8.0
RG_BLOCK_W = 256
CONV_W = 4
HG_DK = 128
F_MIN = 1e-30
HG_CHUNK = 16
SUBLANES = 8
LANES = 128
PACK_ROWS = 16
SCAN_GROUP = 16
ROW_CHUNK = 256
ROW_TILE_WEIGHT_STREAM = 1024
WGRAD_TOKEN_TILE = 2048
VMEM_LIMIT_V7X = 56 * 1024 * 1024

ADAM_LR = 0.001
ADAM_B1 = 0.9
ADAM_B2 = 0.999
ADAM_EPS = 1e-08
ADAM_WD = 0.01
ADAM_STEP = 10

GELU_C = 0.7978845608028654
GELU_K = 0.044715


def _cp(*sem):
    return pltpu.CompilerParams(dimension_semantics=sem, vmem_limit_bytes=VMEM_LIMIT_V7X)


def _row_tile(n, cap):
    if n <= cap:
        return n
    t = cap - cap % 16
    while n % t:
        t -= 16
    return t


def _dot(a, b):
    return jnp.dot(a, b, preferred_element_type=F32)


def _dot_nt(a, b):
    return lax.dot_general(a, b, (((1,), (1,)), ((), ())), preferred_element_type=F32)


def _dot_tn(a, b):
    return lax.dot_general(a, b, (((0,), (0,)), ((), ())), preferred_element_type=F32)


def _sigmoid(x):
    return jax.nn.sigmoid(x)


def _sigmoid_pair(x):
    e = jnp.exp(-jnp.abs(x))
    r = 1.0 / (1.0 + e)
    er = e * r
    pos = x >= 0.0
    return jnp.where(pos, r, er), jnp.where(pos, er, r)


def _log1p_pos(y):
    return jnp.where(y < 0.01, y * (1.0 - y * (0.5 - y * (1.0 / 3.0))), jnp.log(1.0 + y))


def _softplus(x):
    return jnp.maximum(x, 0.0) + _log1p_pos(jnp.exp(-jnp.abs(x)))


def _one_minus_exp(x):
    series = -x * (1.0 + x * 0.5 * (1.0 + x * (1.0 / 3.0) * (1.0 + x * 0.25 * (1.0 + x * 0.2))))
    return jnp.where(x > -0.1, series, 1.0 - jnp.exp(x))


def _gelu_and_grad(x):
    x2 = x * x
    t = jnp.tanh(GELU_C * x * (1.0 + GELU_K * x2))
    g = 0.5 * x * (1.0 + t)
    dg = 0.5 * (1.0 + t) + 0.5 * x * (1.0 - t * t) * GELU_C * (1.0 + 3.0 * GELU_K * x2)
    return g, dg


def _silu_and_grad(x):
    s = _sigmoid(x)
    return x * s, s * (1.0 + x * (1.0 - s))


def _rstd(x):
    return lax.rsqrt(jnp.mean(x * x, axis=-1, keepdims=True) + NORM_EPS)


def _rms_bwd(dh, x, g):
    rstd = _rstd(x)
    xh = x * rstd
    dxh = dh * g
    dx = rstd * (dxh - xh * jnp.mean(dxh * xh, axis=-1, keepdims=True))
    return dx, jnp.sum(dh * xh, axis=0, keepdims=True)


def _shift_rows(x, k):
    n = x.shape[0]
    k = k % n
    return x if k == 0 else pltpu.roll(x, k, axis=0)


def _seg_cumsum(x, seg, reverse=False):
    n = x.shape[0]
    rid = lax.broadcasted_iota(jnp.int32, x.shape, 0) & (seg - 1)
    d = 1
    while d < seg:
        if reverse:
            x = jnp.where(rid < seg - d, x + _shift_rows(x, n - d), x)
        else:
            x = jnp.where(rid >= d, x + _shift_rows(x, d), x)
        d *= 2
    return x


def _group_cumsum_matrix(n, seg):
    row = lax.broadcasted_iota(jnp.int32, (n, n), 0)
    col = lax.broadcasted_iota(jnp.int32, (n, n), 1)
    same_group = (row & ~(seg - 1)) == (col & ~(seg - 1))
    return jnp.where(same_group & (col <= row), 1.0, 0.0).astype(BF16)


def _group_cumsum_mxu(x, tri):
    hi = x.astype(BF16)
    lo = (x - hi.astype(F32)).astype(BF16)
    return _dot(tri, hi) + _dot(tri, lo)


def _scan_rows(a_ref, b_ref, out_ref, n_rows, width, reverse):
    gr = min(SCAN_GROUP, n_rows)
    rid = lax.broadcasted_iota(jnp.int32, (gr, width), 0)
    n_groups = n_rows // gr
    per_trip = min(4, n_groups)
    assert n_groups % per_trip == 0

    def local_scan(g):
        r0 = pl.multiple_of(g * gr, gr)
        a = a_ref[pl.ds(r0, gr), :]
        b = b_ref[pl.ds(r0, gr), :]
        d = 1
        while d < gr:
            if reverse:
                keep = rid < gr - d
                a_sh, b_sh = _shift_rows(a, gr - d), _shift_rows(b, gr - d)
            else:
                keep = rid >= d
                a_sh, b_sh = _shift_rows(a, d), _shift_rows(b, d)
            b = jnp.where(keep, a * b_sh + b, b)
            a = jnp.where(keep, a * a_sh, a)
            d *= 2
        return r0, a, b

    def trip(i, carry):
        first = i * per_trip
        groups = [n_groups - 1 - (first + u) if reverse else first + u for u in range(per_trip)]
        for r0, a, b in [local_scan(g) for g in groups]:
            out = a * carry + b
            out_ref[pl.ds(r0, gr), :] = out
            edge = out[0:1, :] if reverse else out[gr - 1:gr, :]
            carry = jnp.broadcast_to(edge, (gr, width))
        return carry

    lax.fori_loop(0, n_groups // per_trip, trip, jnp.zeros((gr, width), F32))


def _lb_softmax_rows(x_ref, depth):
    rows = [x_ref[pl.ds(l, 1), :] for l in range(depth)]
    top = rows[0]
    for r in rows[1:]:
        top = jnp.maximum(top, r)
    e = [jnp.exp(r - top) for r in rows]
    tot = e[0]
    for r in e[1:]:
        tot = tot + r
    return [r / tot for r in e]


def _lower_bounds_fwd(lb_logits):
    depth, d = lb_logits.shape

    def body(x_ref, o_ref):
        sm = _lb_softmax_rows(x_ref, depth)
        cum = jnp.zeros((1, d), F32)
        for l in range(depth):
            cum = cum + sm[l]
            o_ref[pl.ds(l, 1), :] = jnp.clip(cum - sm[0], 0.0, 1.0)

    return pl.pallas_call(body, name="lower_bounds_fwd",
                          out_shape=jax.ShapeDtypeStruct((depth, d), F32))(lb_logits)


def _lower_bounds_bwd(lb_logits, d_lbs):
    depth, d = lb_logits.shape

    def body(x_ref, g_ref, o_ref):
        sm = _lb_softmax_rows(x_ref, depth)
        cum = jnp.zeros((1, d), F32)
        d_cum = []
        for l in range(depth):
            cum = cum + sm[l]
            v = cum - sm[0]
            d_cum.append(jnp.where((v > 0.0) & (v < 1.0), g_ref[pl.ds(l, 1), :], 0.0))
        d_sm = []
        tail = jnp.zeros((1, d), F32)
        for l in reversed(range(depth)):
            tail = tail + d_cum[l]
            d_sm.append(tail)
        d_sm = d_sm[::-1]
        d_sm[0] = d_sm[0] - tail
        inner = jnp.zeros((1, d), F32)
        for l in range(depth):
            inner = inner + sm[l] * d_sm[l]
        for l in range(depth):
            o_ref[pl.ds(l, 1), :] = sm[l] * (d_sm[l] - inner)

    return pl.pallas_call(body, name="lower_bounds_bwd",
                          out_shape=jax.ShapeDtypeStruct((depth, d), F32))(lb_logits, d_lbs)


def _inproj_fwd(x, gain, w_seg, carry=None):
    t_rows, d = x.shape
    tm = _row_tile(t_rows, ROW_TILE_WEIGHT_STREAM)

    def body(x_ref, g_ref, w_ref, proj_ref, h_ref):
        @pl.when(pl.program_id(1) == 0)
        def _():
            xv = x_ref[...]
            h_ref[...] = (xv * _rstd(xv) * g_ref[...]).astype(BF16)

        proj_ref[...] = _dot(h_ref[...], w_ref[...])

    return _call_carrying(
        body, carry, name="inproj_fwd", grid=(t_rows // tm, N_DEV),
        in_specs=[pl.BlockSpec((tm, d), lambda i, j: (i, 0)),
                  pl.BlockSpec((1, d), lambda i, j: (0, 0)),
                  pl.BlockSpec((None, d, d), lambda i, j: (j, 0, 0))],
        out_specs=[pl.BlockSpec((None, tm, d), lambda i, j: (j, i, 0)),
                   pl.BlockSpec((tm, d), lambda i, j: (i, 0))],
        out_shape=[jax.ShapeDtypeStruct((N_DEV, t_rows, d), F32),
                   jax.ShapeDtypeStruct((t_rows, d), BF16)],
        scratch_shapes=[], semantics=("parallel", "arbitrary"), args=(x, gain, w_seg))


def _merge_out_fwd(proj, y_a, o_n, x, w_out):
    t_rows, d = x.shape
    tm = _row_tile(t_rows, 256)

    def body(g_ref, ma_ref, mb_ref, ya_ref, on_ref, x_ref, w_ref, xmid_ref, y_ref):
        g = g_ref[...]
        ya, on = ya_ref[...].astype(F32), on_ref[...].astype(F32)
        y = _sigmoid(ma_ref[...]) * ya + _sigmoid(mb_ref[...]) * (on * (g * _sigmoid(g)))
        yb = y.astype(BF16)
        y_ref[...] = yb
        xmid_ref[...] = x_ref[...] + _dot(yb, w_ref[...])

    seg = lambda k: pl.BlockSpec((None, tm, d), lambda i, k=k: (k, i, 0))
    row = pl.BlockSpec((tm, d), lambda i: (i, 0))
    return pl.pallas_call(
        body, name="merge_out_fwd", grid=(t_rows // tm,),
        in_specs=[seg(5), seg(6), seg(7), row, row, row, pl.BlockSpec((d, d), lambda i: (0, 0))],
        out_specs=[row, row],
        out_shape=[jax.ShapeDtypeStruct((t_rows, d), F32), jax.ShapeDtypeStruct((t_rows, d), BF16)],
        compiler_params=_cp("parallel"))(proj, proj, proj, y_a, o_n, x, w_out)


def _mlp_fwd(x_mid, gain, w_up, w_down):
    t_rows, d = x_mid.shape
    f8 = w_up.shape[2]
    tm = _row_tile(t_rows, ROW_TILE_WEIGHT_STREAM)

    def body(x_ref, g_ref, wu_ref, wd_ref, out_ref, u_ref, h_ref):
        @pl.when(pl.program_id(1) == 0)
        def _():
            xv = x_ref[...]
            h_ref[...] = (xv * _rstd(xv) * g_ref[...]).astype(BF16)
            out_ref[...] = xv

        u = _dot(h_ref[...], wu_ref[...])
        u_ref[...] = u
        r = jnp.maximum(u, 0.0)
        out_ref[...] += _dot((r * r).astype(BF16), wd_ref[...])

    row = pl.BlockSpec((tm, d), lambda i, j: (i, 0))
    return pl.pallas_call(
        body, name="mlp_fwd", grid=(t_rows // tm, N_DEV),
        in_specs=[row, pl.BlockSpec((1, d), lambda i, j: (0, 0)),
                  pl.BlockSpec((None, d, f8), lambda i, j: (j, 0, 0)),
                  pl.BlockSpec((None, f8, d), lambda i, j: (j, 0, 0))],
        out_specs=[row, pl.BlockSpec((None, tm, f8), lambda i, j: (j, i, 0)), row],
        out_shape=[jax.ShapeDtypeStruct((t_rows, d), F32),
                   jax.ShapeDtypeStruct((N_DEV, t_rows, f8), F32),
                   jax.ShapeDtypeStruct((t_rows, d), BF16)],
        compiler_params=_cp("parallel", "arbitrary"))(x_mid, gain, w_up, w_down)


def _loss_head(x, gain, target):
    t_rows, d = x.shape
    tm = _row_tile(t_rows, 512)

    def body(x_ref, g_ref, t_ref, loss_ref, dx_ref, dxb_ref, dg_ref):
        @pl.when(pl.program_id(0) == 0)
        def _():
            loss_ref[...] = jnp.zeros_like(loss_ref)
            dg_ref[...] = jnp.zeros_like(dg_ref)

        xv = x_ref[...]
        g = g_ref[...]
        err = xv * _rstd(xv) * g - t_ref[...]
        loss_ref[...] += (0.5 / d) * jnp.sum(err * err)
        dx, dg = _rms_bwd(err * (1.0 / d), xv, g)
        dx_ref[...] = dx
        dxb_ref[...] = dx.astype(BF16)
        dg_ref[...] += dg

    row = pl.BlockSpec((tm, d), lambda i: (i, 0))
    vec = pl.BlockSpec((1, d), lambda i: (0, 0))
    return pl.pallas_call(
        body, name="loss_head", grid=(t_rows // tm,),
        in_specs=[row, vec, row],
        out_specs=[pl.BlockSpec((SUBLANES, LANES), lambda i: (0, 0)), row, row, vec],
        out_shape=[jax.ShapeDtypeStruct((SUBLANES, LANES), F32),
                   jax.ShapeDtypeStruct((t_rows, d), F32),
                   jax.ShapeDtypeStruct((t_rows, d), BF16),
                   jax.ShapeDtypeStruct((1, d), F32)],
        compiler_params=_cp("arbitrary"))(x, gain, target)


def _mlp_bwd(d_out, d_out_b, u, x_mid, gain, w_up, w_down):
    t_rows, d = x_mid.shape
    f8 = w_up.shape[2]
    tm = _row_tile(t_rows, ROW_TILE_WEIGHT_STREAM)
    sub = _row_tile(tm, ROW_CHUNK)

    def body(do_ref, dob_ref, u_ref, x_ref, g_ref, wu_ref, wd_ref, dx_ref, dxb_ref, du_ref, act_ref, dg_ref):
        j = pl.program_id(1)

        @pl.when((pl.program_id(0) == 0) & (j == 0))
        def _():
            dg_ref[...] = jnp.zeros_like(dg_ref)

        @pl.when(j == 0)
        def _():
            dx_ref[...] = jnp.zeros_like(dx_ref)

        r = jnp.maximum(u_ref[...], 0.0)
        act_ref[...] = (r * r).astype(BF16)
        du = (_dot_nt(dob_ref[...], wd_ref[...]) * (2.0 * r)).astype(BF16)
        du_ref[...] = du
        dx_ref[...] += _dot_nt(du, wu_ref[...])

        @pl.when(j == N_DEV - 1)
        def _():
            def finish(c, _):
                rows = pl.ds(pl.multiple_of(c * sub, sub), sub)
                dx, dg = _rms_bwd(dx_ref[rows, :], x_ref[rows, :], g_ref[...])
                dx = dx + do_ref[rows, :]
                dx_ref[rows, :] = dx
                dxb_ref[rows, :] = dx.astype(BF16)
                dg_ref[...] += dg
                return 0

            lax.fori_loop(0, tm // sub, finish, 0)

    row = pl.BlockSpec((tm, d), lambda i, j: (i, 0))
    vec = pl.BlockSpec((1, d), lambda i, j: (0, 0))
    hid = pl.BlockSpec((None, tm, f8), lambda i, j: (j, i, 0))
    return pl.pallas_call(
        body, name="mlp_bwd", grid=(t_rows // tm, N_DEV),
        in_specs=[row, row, hid, row, vec,
                  pl.BlockSpec((None, d, f8), lambda i, j: (j, 0, 0)),
                  pl.BlockSpec((None, f8, d), lambda i, j: (j, 0, 0))],
        out_specs=[row, row, hid, hid, vec],
        out_shape=[jax.ShapeDtypeStruct((t_rows, d), F32),
                   jax.ShapeDtypeStruct((t_rows, d), BF16),
                   jax.ShapeDtypeStruct((N_DEV, t_rows, f8), BF16),
                   jax.ShapeDtypeStruct((N_DEV, t_rows, f8), BF16),
                   jax.ShapeDtypeStruct((1, d), F32)],
        compiler_params=_cp("arbitrary", "arbitrary"))(d_out, d_out_b, u, x_mid, gain, w_up, w_down)


def _outproj_bwd(dx_mid_b, w_out, proj, y_a, o_n):
    t_rows, d = y_a.shape
    tm = _row_tile(t_rows, 256)

    def body(dx_ref, w_ref, g_ref, ma_ref, mb_ref, ya_ref, on_ref, dya_ref, don_ref, dp_ref):
        dy = _dot_nt(dx_ref[...], w_ref[...])
        sa = _sigmoid(ma_ref[...])
        sb = _sigmoid(mb_ref[...])
        sg, dsg = _silu_and_grad(g_ref[...])
        ya = ya_ref[...].astype(F32)
        on = on_ref[...].astype(F32)
        dya_ref[...] = dy * sa
        t = dy * sb
        don_ref[...] = t * sg
        dp_ref[0] = (t * on * dsg).astype(BF16)
        dp_ref[1] = (dy * ya * sa * (1.0 - sa)).astype(BF16)
        dp_ref[2] = (dy * on * sg * sb * (1.0 - sb)).astype(BF16)

    seg = lambda k: pl.BlockSpec((None, tm, d), lambda i, k=k: (k, i, 0))
    row = pl.BlockSpec((tm, d), lambda i: (i, 0))
    return pl.pallas_call(
        body, name="outproj_bwd", grid=(t_rows // tm,),
        in_specs=[row, pl.BlockSpec((d, d), lambda i: (0, 0)), seg(5), seg(6), seg(7), row, row],
        out_specs=[row, row, pl.BlockSpec((3, tm, d), lambda i: (0, i, 0))],
        out_shape=[jax.ShapeDtypeStruct((t_rows, d), F32),
                   jax.ShapeDtypeStruct((t_rows, d), F32),
                   jax.ShapeDtypeStruct((3, t_rows, d), BF16)],
        compiler_params=_cp("parallel"))(dx_mid_b, w_out, proj, proj, proj, y_a, o_n)


def _inproj_bwd(dx_mid, dp_a, dp_b, dp_c, w_seg, x_in, gain, carry=None):
    t_rows, d = x_in.shape
    tm = _row_tile(t_rows, ROW_TILE_WEIGHT_STREAM)
    sub = _row_tile(tm, ROW_CHUNK)
    n_a, n_b = dp_a.shape[0], dp_b.shape[0]

    def body(dxm_ref, a_ref, b_ref, c_ref, w_ref, x_ref, g_ref, dx_ref, dxb_ref, dg_ref):
        j = pl.program_id(1)

        @pl.when((pl.program_id(0) == 0) & (j == 0))
        def _():
            dg_ref[...] = jnp.zeros_like(dg_ref)

        @pl.when(j == 0)
        def _():
            dx_ref[...] = jnp.zeros_like(dx_ref)

        @pl.when(j < n_a)
        def _():
            dx_ref[...] += _dot_nt(a_ref[...], w_ref[...])

        @pl.when((j >= n_a) & (j < n_a + n_b))
        def _():
            dx_ref[...] += _dot_nt(b_ref[...], w_ref[...])

        @pl.when(j >= n_a + n_b)
        def _():
            dx_ref[...] += _dot_nt(c_ref[...], w_ref[...])

        @pl.when(j == N_DEV - 1)
        def _():
            def finish(c, _):
                rows = pl.ds(pl.multiple_of(c * sub, sub), sub)
                dx, dg = _rms_bwd(dx_ref[rows, :], x_ref[rows, :], g_ref[...])
                dx = dx + dxm_ref[rows, :]
                dx_ref[rows, :] = dx
                dxb_ref[rows, :] = dx.astype(BF16)
                dg_ref[...] += dg
                return 0

            lax.fori_loop(0, tm // sub, finish, 0)

    def part(first, n):
        return pl.BlockSpec((None, tm, d), lambda i, j: (jnp.clip(j - first, 0, n - 1), i, 0))

    row = pl.BlockSpec((tm, d), lambda i, j: (i, 0))
    vec = pl.BlockSpec((1, d), lambda i, j: (0, 0))
    return _call_carrying(
        body, carry, name="inproj_bwd", grid=(t_rows // tm, N_DEV),
        in_specs=[row, part(0, n_a), part(n_a, n_b), part(n_a + n_b, dp_c.shape[0]),
                  pl.BlockSpec((None, d, d), lambda i, j: (j, 0, 0)), row, vec],
        out_specs=[row, row, vec],
        out_shape=[jax.ShapeDtypeStruct((t_rows, d), F32),
                   jax.ShapeDtypeStruct((t_rows, d), BF16),
                   jax.ShapeDtypeStruct((1, d), F32)],
        scratch_shapes=[], semantics=("arbitrary", "arbitrary"),
        args=(dx_mid, dp_a, dp_b, dp_c, w_seg, x_in, gain))


def _wgrad(a3, b3, name):
    n_a, t_rows, k_a = a3.shape
    n_b, _, n_cols = b3.shape
    n = max(n_a, n_b)
    bk = _row_tile(k_a, 1024)
    bn = n_cols if n_cols <= 1024 else 1024
    tt = _row_tile(t_rows, WGRAD_TOKEN_TILE)
    n_t = t_rows // tt

    def body(a_ref, b_ref, o_ref, acc_ref):
        t, j = pl.program_id(2), pl.program_id(3)
        part = _dot_tn(a_ref[...], b_ref[...])

        @pl.when(t == 0)
        def _():
            acc_ref[j] = part

        @pl.when(t > 0)
        def _():
            acc_ref[j] += part

        @pl.when(t == n_t - 1)
        def _():
            o_ref[...] = acc_ref[j].astype(BF16)

    def out_map(p, q, t, j):
        return (jnp.where(t == n_t - 1, j, 0), p, q)

    return pl.pallas_call(
        body, name=name, grid=(k_a // bk, n_cols // bn, n_t, n),
        in_specs=[pl.BlockSpec((None, tt, bk), lambda p, q, t, j: (j if n_a > 1 else 0, t, p)),
                  pl.BlockSpec((None, tt, bn), lambda p, q, t, j: (j if n_b > 1 else 0, t, q))],
        out_specs=pl.BlockSpec((None, bk, bn), out_map),
        out_shape=jax.ShapeDtypeStruct((n, k_a, n_cols), BF16),
        scratch_shapes=[pltpu.VMEM((n, bk, bn), F32)],
        compiler_params=_cp("parallel", "parallel", "arbitrary", "arbitrary"))(a3, b3)


def _conv_taps(xe, n):
    return [_shift_rows(xe, CONV_W - 1 - j)[SUBLANES:SUBLANES + n, :] for j in range(CONV_W)]


def _rg_gates(xc, w_r, b_r, w_i, b_i, sp8):
    xb = xc.astype(BF16)
    r = _sigmoid(_dot(xb, w_r) + b_r)
    i = _sigmoid(_dot(xb, w_i) + b_i)
    return r, i


def _mixer_a_fwd(proj, conv_w, conv_b, w_r, b_r, w_i, b_i, lam, seq):
    _, t_rows, d = proj.shape
    n_seq, n_blk = t_rows // seq, d // RG_BLOCK_W
    wb = RG_BLOCK_W
    ch = _row_tile(seq, ROW_CHUNK)

    def body(xa_ref, ga_ref, cw_ref, cb_ref, wr_ref, br_ref, wi_ref, bi_ref, lam_ref, h_ref, ya_ref,
             xpad, a_s, u_s):
        xpad[0:SUBLANES, :] = jnp.zeros((SUBLANES, wb), F32)
        xpad[SUBLANES:, :] = xa_ref[...]
        sp8 = RG_C * _softplus(-lam_ref[...])

        def gates(c, _):
            r0 = pl.multiple_of(c * ch, ch)
            taps = _conv_taps(xpad[pl.ds(r0, ch + SUBLANES), :], ch)
            xc = cb_ref[...] + sum(cw_ref[pl.ds(j, 1), :] * taps[j] for j in range(CONV_W))
            r, i = _rg_gates(xc, wr_ref[...], br_ref[...], wi_ref[...], bi_ref[...], sp8)
            log_a = -(r * sp8)
            a_s[pl.ds(r0, ch), :] = jnp.exp(log_a)
            u_s[pl.ds(r0, ch), :] = jnp.sqrt(jnp.maximum(_one_minus_exp(2.0 * log_a), 0.0)) * (i * xc)
            return 0

        lax.fori_loop(0, seq // ch, gates, 0)
        _scan_rows(a_s, u_s, h_ref, seq, wb, reverse=False)

        def gate_out(c, _):
            r0 = pl.multiple_of(c * ch, ch)
            gl, _ = _gelu_and_grad(ga_ref[pl.ds(r0, ch), :])
            ya_ref[pl.ds(r0, ch), :] = (h_ref[pl.ds(r0, ch), :] * gl).astype(BF16)
            return 0

        lax.fori_loop(0, seq // ch, gate_out, 0)

    seg = lambda k: pl.BlockSpec((None, seq, wb), lambda s, b, k=k: (k, s, b))
    blk = pl.BlockSpec((seq, wb), lambda s, b: (s, b))
    vec = pl.BlockSpec((1, wb), lambda s, b: (0, b))
    wsp = pl.BlockSpec((None, wb, wb), lambda s, b: (b, 0, 0))
    return pl.pallas_call(
        body, name="mixer_a_fwd", grid=(n_seq, n_blk),
        in_specs=[seg(0), seg(1), pl.BlockSpec((CONV_W, wb), lambda s, b: (0, b)), vec, wsp, vec, wsp, vec, vec],
        out_specs=[blk, blk],
        out_shape=[jax.ShapeDtypeStruct((t_rows, d), F32), jax.ShapeDtypeStruct((t_rows, d), BF16)],
        scratch_shapes=[pltpu.VMEM((seq + SUBLANES, wb), F32), pltpu.VMEM((seq, wb), F32),
                        pltpu.VMEM((seq, wb), F32)],
        compiler_params=_cp("parallel", "parallel"))(proj, proj, conv_w, conv_b, w_r, b_r, w_i, b_i, lam)


def _mixer_a_bwd(proj, h, d_ya, conv_w, conv_b, w_r, b_r, w_i, b_i, lam, seq):
    _, t_rows, d = proj.shape
    n_seq, n_blk = t_rows // seq, d // RG_BLOCK_W
    wb = RG_BLOCK_W
    ch = _row_tile(seq, ROW_CHUNK)
    n_ch = seq // ch

    def body(xa_ref, ga_ref, h_ref, dya_ref, cw_ref, cb_ref, wr_ref, br_ref, wi_ref, bi_ref, lam_ref,
             dp_ref, dwr_ref, dwi_ref, dbr_ref, dbi_ref, dlam_ref, dcw_ref, dcb_ref,
             xpad, hpad, a_s, e_pad, g_s, xc_s, r_s, i_s, dxc_pad):
        @pl.when(pl.program_id(1) == 0)
        def _():
            for ref in (dwr_ref, dwi_ref, dbr_ref, dbi_ref, dlam_ref, dcw_ref, dcb_ref):
                ref[...] = jnp.zeros_like(ref)

        zeros8 = jnp.zeros((SUBLANES, wb), F32)
        xpad[0:SUBLANES, :] = zeros8
        xpad[SUBLANES:, :] = xa_ref[...]
        hpad[0:SUBLANES, :] = zeros8
        hpad[SUBLANES:, :] = h_ref[...]
        e_pad[seq:, :] = zeros8
        dxc_pad[seq:, :] = zeros8
        lam_v = lam_ref[...]
        sp8 = RG_C * _softplus(-lam_v)

        def recompute(c, _):
            r0 = pl.multiple_of(c * ch, ch)
            rows = pl.ds(r0, ch)
            taps = _conv_taps(xpad[pl.ds(r0, ch + SUBLANES), :], ch)
            xc = cb_ref[...] + sum(cw_ref[pl.ds(j, 1), :] * taps[j] for j in range(CONV_W))
            r, i = _rg_gates(xc, wr_ref[...], br_ref[...], wi_ref[...], bi_ref[...], sp8)
            a = jnp.exp(-(r * sp8))
            gl, dgl = _gelu_and_grad(ga_ref[rows, :])
            dya = dya_ref[rows, :]
            g = dya * gl
            dp_ref[1, rows, :] = (dya * h_ref[rows, :] * dgl).astype(BF16)
            a_s[rows, :] = a
            e_pad[rows, :] = a * g
            g_s[rows, :] = g
            xc_s[rows, :] = xc
            r_s[rows, :] = r
            i_s[rows, :] = i
            return 0

        lax.fori_loop(0, n_ch, recompute, 0)
        _scan_rows(a_s, e_pad, e_pad, seq, wb, reverse=True)

        def grads(c, _):
            r0 = pl.multiple_of(c * ch, ch)
            rows = pl.ds(r0, ch)
            halo = pl.ds(r0, ch + SUBLANES)
            dh = g_s[rows, :] + _shift_rows(e_pad[halo, :], ch + SUBLANES - 1)[0:ch, :]
            h_prev = _shift_rows(hpad[halo, :], 1)[SUBLANES:, :]
            xc, r, i = xc_s[rows, :], r_s[rows, :], i_s[rows, :]
            log_a = -(r * sp8)
            a = jnp.exp(log_a)
            om = _one_minus_exp(2.0 * log_a)
            sq = jnp.sqrt(jnp.maximum(om, 0.0))
            t1 = dh * xc
            d_i = t1 * sq
            d_la = dh * h_prev * a + jnp.where(om > 0.0, -(t1 * i) * (1.0 - om) / sq, 0.0)
            dpr = -(d_la * sp8) * r * (1.0 - r)
            dpi = d_i * i * (1.0 - i)
            dprb, dpib, xb = dpr.astype(BF16), dpi.astype(BF16), xc.astype(BF16)
            dxc = dh * sq * i + _dot_nt(dprb, wr_ref[...]) + _dot_nt(dpib, wi_ref[...])
            dwr_ref[...] += _dot_tn(xb, dprb)
            dwi_ref[...] += _dot_tn(xb, dpib)
            dbr_ref[...] += jnp.sum(dpr, axis=0, keepdims=True)
            dbi_ref[...] += jnp.sum(dpi, axis=0, keepdims=True)
            dlam_ref[...] += jnp.sum(d_la * r, axis=0, keepdims=True) * (RG_C * _sigmoid(-lam_v))
            dcb_ref[...] += jnp.sum(dxc, axis=0, keepdims=True)
            taps = _conv_taps(xpad[halo, :], ch)
            for j in range(CONV_W):
                dcw_ref[pl.ds(j, 1), :] += jnp.sum(dxc * taps[j], axis=0, keepdims=True)
            dxc_pad[rows, :] = dxc
            return 0

        lax.fori_loop(0, n_ch, grads, 0)

        def conv_bwd(c, _):
            r0 = pl.multiple_of(c * ch, ch)
            de = dxc_pad[pl.ds(r0, ch + SUBLANES), :]
            dxa = sum(cw_ref[pl.ds(j, 1), :] * _shift_rows(de, ch + SUBLANES - (CONV_W - 1 - j))[0:ch, :]
                      for j in range(CONV_W))
            dp_ref[0, pl.ds(r0, ch), :] = dxa.astype(BF16)
            return 0

        lax.fori_loop(0, n_ch, conv_bwd, 0)

    seg = lambda k: pl.BlockSpec((None, seq, wb), lambda b, s, k=k: (k, s, b))
    blk = pl.BlockSpec((seq, wb), lambda b, s: (s, b))
    vec = pl.BlockSpec((1, wb), lambda b, s: (0, b))
    taps = pl.BlockSpec((CONV_W, wb), lambda b, s: (0, b))
    wsp = pl.BlockSpec((None, wb, wb), lambda b, s: (b, 0, 0))
    vec_shape = jax.ShapeDtypeStruct((1, d), F32)
    w_shape = jax.ShapeDtypeStruct((n_blk, wb, wb), F32)
    pad = pltpu.VMEM((seq + SUBLANES, wb), F32)
    full = pltpu.VMEM((seq, wb), F32)
    return pl.pallas_call(
        body, name="mixer_a_bwd", grid=(n_blk, n_seq),
        in_specs=[seg(0), seg(1), blk, blk, taps, vec, wsp, vec, wsp, vec, vec],
        out_specs=[pl.BlockSpec((2, seq, wb), lambda b, s: (0, s, b)), wsp, wsp, vec, vec, vec, taps, vec],
        out_shape=[jax.ShapeDtypeStruct((2, t_rows, d), BF16), w_shape, w_shape, vec_shape, vec_shape,
                   vec_shape, jax.ShapeDtypeStruct((CONV_W, d), F32), vec_shape],
        scratch_shapes=[pad, pad, full, pad, full, full, full, full, pad],
        compiler_params=_cp("parallel", "arbitrary"))(
            proj, proj, h, d_ya, conv_w, conv_b, w_r, b_r, w_i, b_i, lam)


def _hg_prepare(q_ref, z_ref, lb, rows):
    z = z_ref[rows, :]
    sig, nsig = _sigmoid_pair(z)
    fg = lb + (1.0 - lb) * sig
    log_f = jnp.log(jnp.maximum(fg, F_MIN))
    key = (1.0 - lb) * nsig
    qs, _ = _silu_and_grad(q_ref[rows, :])
    return qs, key, log_f, sig, fg


HG_UNROLL_TERMS = 32
HG_UNROLL_FWD = 16
HG_UNROLL_BWD = 16
HG_HALF = HG_CHUNK // 2
HG_STACK = HG_CHUNK * HG_HALF
assert HG_HALF == SUBLANES


def _half_of(x, s):
    return x[:HG_HALF, :] if s < HG_HALF else x[HG_HALF:, :]


def _hg_decay(g_ref, r0, g_rows, first_row, s):
    rid = lax.broadcasted_iota(jnp.int32, g_rows.shape, 0) + first_row
    gs = g_ref[pl.ds(r0 + s, 1), :]
    return jnp.where(rid >= s, jnp.exp(g_rows - gs), 0.0)


def _half_start(s):
    return 0 if s < HG_HALF else HG_HALF


def _hg_cross_decays(gc):
    rid = lax.broadcasted_iota(jnp.int32, (HG_CHUNK, HG_DK), 0)
    g_mid = gc[HG_HALF - 1:HG_HALF, :]
    e_hi = jnp.where(rid >= HG_HALF, jnp.exp(gc - g_mid), 0.0)
    e_lo = jnp.where(rid < HG_HALF, jnp.exp(g_mid - gc), 0.0)
    return e_hi, e_lo


def _hg_cross(qc, kc, gc):
    e_hi, e_lo = _hg_cross_decays(gc)
    return (qc * e_hi).astype(BF16), (kc * e_lo).astype(BF16)


def _stack(slabs):
    return jnp.concatenate(slabs, axis=0).astype(BF16)


def _slab_row_sums():
    row = lax.broadcasted_iota(jnp.int32, (HG_CHUNK, HG_STACK), 0)
    col = lax.broadcasted_iota(jnp.int32, (HG_CHUNK, HG_STACK), 1)
    lo = row * HG_HALF
    return jnp.where((col >= lo) & (col < lo + HG_HALF), 1.0, 0.0).astype(BF16)


def _for_chunks(n, unroll, *stages):
    unroll = min(unroll, n)
    assert n % unroll == 0

    def trip(i, _):
        chunks = [i * unroll + u for u in range(unroll)]
        carried = [stages[0](c) for c in chunks]
        for stage in stages[1:]:
            carried = [stage(c, x) for c, x in zip(chunks, carried)]
        return 0

    lax.fori_loop(0, n // unroll, trip, 0)


def _hg_state_terms(v_ref, k_ref, g_ref, states, n_chunks):
    def issue(c):
        rows = pl.ds(pl.multiple_of(c * HG_CHUNK, HG_CHUNK), HG_CHUNK)
        gc = g_ref[rows, :]
        k_end = k_ref[rows, :] * jnp.exp(gc[HG_CHUNK - 1:HG_CHUNK, :] - gc)
        return _dot_tn(v_ref[rows, :].astype(BF16), k_end.astype(BF16))

    def store(c, term):
        states[c] = term

    _for_chunks(n_chunks, HG_UNROLL_TERMS, issue, store)


def _hg_state_chain(states, g_ref, carry_ref, n_chunks, reverse):
    unroll = min(8, n_chunks)
    assert n_chunks % unroll == 0
    carry_ref[...] = jnp.zeros_like(carry_ref)

    def trip(i, _):
        st = carry_ref[...]
        for u in range(unroll):
            k = i * unroll + u
            c = n_chunks - 1 - k if reverse else k
            term = states[c]
            states[c] = st
            st = st * jnp.exp(g_ref[pl.ds(c * HG_CHUNK + HG_CHUNK - 1, 1), :]) + term
        carry_ref[...] = st
        return 0

    lax.fori_loop(0, n_chunks // unroll, trip, 0)


def _hgrn_fwd(proj, lower_bound, hg_gain, seq, carry=None):
    _, t_rows, d = proj.shape
    n_seq, n_head = t_rows // seq, d // HG_DK
    ch = _row_tile(seq, ROW_CHUNK)
    n_chunks = seq // HG_CHUNK

    def body(q_ref, z_ref, v_ref, lb_ref, gain_ref, o_ref, on_ref, qs_s, k_s, g_s, states, st_ref):
        lb = lb_ref[...]
        tri = _group_cumsum_matrix(ch, HG_CHUNK)

        def prepare(c):
            rows = pl.ds(pl.multiple_of(c * ch, ch), ch)
            qs, key, log_f, _, _ = _hg_prepare(q_ref, z_ref, lb, rows)
            qs_s[rows, :] = qs
            k_s[rows, :] = key
            return _group_cumsum_mxu(log_f, tri)

        def store_cumsum(c, g):
            g_s[pl.ds(pl.multiple_of(c * ch, ch), ch), :] = g

        _for_chunks(seq // ch, 4, prepare, store_cumsum)
        _hg_state_terms(v_ref, k_s, g_s, states, n_chunks)
        _hg_state_chain(states, g_s, st_ref, n_chunks, reverse=False)
        ones = jnp.ones((HG_DK, HG_DK), BF16)

        def issue(c):
            r0 = pl.multiple_of(c * HG_CHUNK, HG_CHUNK)
            rows = pl.ds(r0, HG_CHUNK)
            qc, gc = qs_s[rows, :], g_s[rows, :]
            o = _dot_nt((qc * jnp.exp(gc)).astype(BF16), states[c].astype(BF16))
            pairs = [qc[_half_start(s):, :] * _hg_decay(g_s, r0, gc[_half_start(s):, :], _half_start(s), s)
                     * k_s[pl.ds(r0 + s, 1), :] for s in range(HG_CHUNK)]
            score = _dot(_stack(pairs), ones)
            return o, score

        def combine(c, issued):
            o, score = issued
            r0 = pl.multiple_of(c * HG_CHUNK, HG_CHUNK)
            halves = [o[:HG_HALF, :], o[HG_HALF:, :]]
            first = 0
            for s in range(HG_CHUNK):
                vs = v_ref[pl.ds(r0 + s, 1), :]
                if s < HG_HALF:
                    halves[0] += score[first:first + HG_HALF, :] * vs
                    first += HG_HALF
                halves[1] += score[first:first + HG_HALF, :] * vs
                first += HG_HALF
            o_ref[pl.ds(r0, HG_CHUNK), :] = jnp.concatenate(halves, axis=0)

        _for_chunks(n_chunks, HG_UNROLL_FWD, issue, combine)

        def norm(c, _):
            rows = pl.ds(pl.multiple_of(c * ch, ch), ch)
            o = o_ref[rows, :]
            on_ref[rows, :] = (o * _rstd(o) * gain_ref[...]).astype(BF16)
            return 0

        lax.fori_loop(0, seq // ch, norm, 0)

    seg = lambda k: pl.BlockSpec((None, seq, HG_DK), lambda s, h, k=k: (k, s, h))
    blk = pl.BlockSpec((seq, HG_DK), lambda s, h: (s, h))
    full = pltpu.VMEM((seq, HG_DK), F32)
    return _call_carrying(
        body, carry, name="hgrn_fwd", grid=(n_seq, n_head),
        in_specs=[seg(2), seg(3), seg(4), pl.BlockSpec((1, HG_DK), lambda s, h: (0, h)),
                  pl.BlockSpec((1, HG_DK), lambda s, h: (0, 0))],
        out_specs=[blk, blk],
        out_shape=[jax.ShapeDtypeStruct((t_rows, d), F32), jax.ShapeDtypeStruct((t_rows, d), BF16)],
        scratch_shapes=[full, full, full, pltpu.VMEM((n_chunks, HG_DK, HG_DK), F32),
                        pltpu.VMEM((HG_DK, HG_DK), F32)],
        semantics=("parallel", "parallel"), args=(proj, proj, proj, lower_bound, hg_gain))


def _hgrn_bwd(proj, lower_bound, hg_gain, o, d_on, seq, carry=None):
    _, t_rows, d = proj.shape
    n_seq, n_head = t_rows // seq, d // HG_DK
    ch = _row_tile(seq, ROW_CHUNK)
    n_chunks = seq // HG_CHUNK
    cc = HG_CHUNK

    def body(q_ref, z_ref, v_ref, lb_ref, gain_ref, o_ref, don_ref, dp_ref, dlb_ref, dgain_ref,
             qs_s, k_s, g_s, do_s, dqs_s, dk_s, dlf_s, states, dstates, carry_ref):
        hh, ss = pl.program_id(0), pl.program_id(1)
        lb = lb_ref[...]

        @pl.when(ss == 0)
        def _():
            dlb_ref[...] = jnp.zeros_like(dlb_ref)

        @pl.when((ss == 0) & (hh == 0))
        def _():
            dgain_ref[...] = jnp.zeros_like(dgain_ref)

        tri = _group_cumsum_matrix(ch, cc)

        def prepare(c):
            rows = pl.ds(pl.multiple_of(c * ch, ch), ch)
            qs, key, log_f, _, _ = _hg_prepare(q_ref, z_ref, lb, rows)
            qs_s[rows, :] = qs
            k_s[rows, :] = key
            do, dgain = _rms_bwd(don_ref[rows, :], o_ref[rows, :], gain_ref[...])
            do_s[rows, :] = do
            dgain_ref[...] += dgain
            return _group_cumsum_mxu(log_f, tri)

        def store_cumsum(c, g):
            g_s[pl.ds(pl.multiple_of(c * ch, ch), ch), :] = g

        _for_chunks(seq // ch, 4, prepare, store_cumsum)

        _hg_state_terms(v_ref, k_s, g_s, states, n_chunks)
        _hg_state_chain(states, g_s, carry_ref, n_chunks, reverse=False)

        def query_term(c):
            rows = pl.ds(pl.multiple_of(c * cc, cc), cc)
            q_in = qs_s[rows, :] * jnp.exp(g_s[rows, :])
            return _dot_tn(do_s[rows, :].astype(BF16), q_in.astype(BF16))

        def store_query_term(c, term):
            dstates[c] = term

        _for_chunks(n_chunks, HG_UNROLL_TERMS, query_term, store_query_term)
        _hg_state_chain(dstates, g_s, carry_ref, n_chunks, reverse=True)
        ones = jnp.ones((HG_DK, HG_DK), BF16)
        row_sums = _slab_row_sums()

        def chunk_rows(c):
            r0 = pl.multiple_of(c * cc, cc)
            return r0, pl.ds(r0, cc)

        def through_state(c):
            r0, rows = chunk_rows(c)
            qc, kc, gc, vc, doc = qs_s[rows, :], k_s[rows, :], g_s[rows, :], v_ref[rows, :], do_s[rows, :]
            st, dst = states[c], dstates[c]
            g_last = gc[cc - 1:cc, :]
            e_last, e_end = jnp.exp(g_last), jnp.exp(g_last - gc)
            dob, vcb, dstb = doc.astype(BF16), vc.astype(BF16), dst.astype(BF16)
            dqs = _dot(dob, st.astype(BF16))
            dk_state = _dot(vcb, dstb)
            dv = _dot_nt((kc * e_end).astype(BF16), dstb)
            cots = [_half_of(doc, s) * v_ref[pl.ds(r0 + s, 1), :] for s in range(cc)]
            d_score = _dot(_stack(cots), ones)
            x, y = _hg_cross(qc, kc, gc)
            cross = (_dot_nt(dob, vcb), _dot_nt(vcb, dob), _dot_nt(y, x))
            return dqs, dk_state, dv, d_score, e_last * jnp.sum(dst * st, axis=0, keepdims=True), cross

        def pair_terms(c, carried):
            dqs, dk_state, dv, d_score, d_glast, (da_cross, da_cross_t, a_cross_t) = carried
            r0, rows = chunk_rows(c)
            qc, kc, gc = qs_s[rows, :], k_s[rows, :], g_s[rows, :]
            dqs = dqs * jnp.exp(gc)
            dk_state = dk_state * jnp.exp(gc[cc - 1:cc, :] - gc)
            d_glast = d_glast + jnp.sum(kc * dk_state, axis=0, keepdims=True)
            dqs_half = [dqs[:HG_HALF, :], dqs[HG_HALF:, :]]
            pairs, dk_terms = [], []
            for s in range(cc):
                qv = _half_of(qc, s)
                decay = _hg_decay(g_s, r0, _half_of(gc, s), _half_start(s), s)
                ks = k_s[pl.ds(r0 + s, 1), :]
                da_decay = d_score[s * HG_HALF:(s + 1) * HG_HALF, :] * decay
                pairs.append(qv * decay * ks)
                dk_terms.append(da_decay * qv)
                dqs_half[s // HG_HALF] += da_decay * ks
            score = _dot(_stack(pairs), ones)
            dk = dk_state + _dot(row_sums, _stack(dk_terms))
            x, y = _hg_cross(qc, kc, gc)
            cross = (_dot(da_cross.astype(BF16), y), _dot(da_cross_t.astype(BF16), x),
                     _dot(a_cross_t.astype(BF16), do_s[rows, :].astype(BF16)))
            return jnp.concatenate(dqs_half, axis=0), dk, dv, score, d_glast, cross

        def value_terms(c, carried):
            dqs, dk, dv, score, d_glast, (dx_cross, dy_cross, dv_cross) = carried
            _, rows = chunk_rows(c)
            doc, gc = do_s[rows, :], g_s[rows, :]
            dv_terms = [score[s * HG_HALF:(s + 1) * HG_HALF, :] * _half_of(doc, s) for s in range(cc)]
            e_hi, e_lo = _hg_cross_decays(gc)
            return (dqs + dx_cross * e_hi, dk + dy_cross * e_lo,
                    dv + dv_cross + _dot(row_sums, _stack(dv_terms)), d_glast)

        def store(c, x):
            dqs, dk, dv, d_glast = x
            _, rows = chunk_rows(c)
            d_g = qs_s[rows, :] * dqs - k_s[rows, :] * dk
            dlf_s[rows, :] = _seg_cumsum(d_g, cc, reverse=True) + d_glast
            dqs_s[rows, :] = dqs
            dk_s[rows, :] = dk
            dp_ref[2, rows, :] = dv.astype(BF16)

        _for_chunks(n_chunks, HG_UNROLL_BWD, through_state, pair_terms, value_terms, store)

        def finish(c, _):
            rows = pl.ds(pl.multiple_of(c * ch, ch), ch)
            sig, nsig = _sigmoid_pair(z_ref[rows, :])
            fg = lb + (1.0 - lb) * sig
            _, dsilu = _silu_and_grad(q_ref[rows, :])
            dp_ref[0, rows, :] = (dqs_s[rows, :] * dsilu).astype(BF16)
            dfg = jnp.where(fg > F_MIN, dlf_s[rows, :] / fg, 0.0)
            dk = dk_s[rows, :]
            dp_ref[1, rows, :] = ((dfg - dk) * (1.0 - lb) * sig * nsig).astype(BF16)
            dlb_ref[...] += jnp.sum((dfg - dk) * nsig, axis=0, keepdims=True)
            return 0

        lax.fori_loop(0, seq // ch, finish, 0)

    seg = lambda k: pl.BlockSpec((None, seq, HG_DK), lambda h, s, k=k: (k, s, h))
    blk = pl.BlockSpec((seq, HG_DK), lambda h, s: (s, h))
    full = pltpu.VMEM((seq, HG_DK), F32)
    return _call_carrying(
        body, carry, name="hgrn_bwd", grid=(n_head, n_seq),
        in_specs=[seg(2), seg(3), seg(4), pl.BlockSpec((1, HG_DK), lambda h, s: (0, h)),
                  pl.BlockSpec((1, HG_DK), lambda h, s: (0, 0)), blk, blk],
        out_specs=[pl.BlockSpec((3, seq, HG_DK), lambda h, s: (0, s, h)),
                   pl.BlockSpec((1, HG_DK), lambda h, s: (0, h)),
                   pl.BlockSpec((1, HG_DK), lambda h, s: (0, 0))],
        out_shape=[jax.ShapeDtypeStruct((3, t_rows, d), BF16), jax.ShapeDtypeStruct((1, d), F32),
                   jax.ShapeDtypeStruct((1, HG_DK), F32)],
        scratch_shapes=[full, full, full, full, full, full, full,
                        pltpu.VMEM((n_chunks, HG_DK, HG_DK), F32), pltpu.VMEM((n_chunks, HG_DK, HG_DK), F32),
                        pltpu.VMEM((HG_DK, HG_DK), F32)],
        semantics=("arbitrary", "arbitrary"), args=(proj, proj, proj, lower_bound, hg_gain, o, d_on))


def _mesh_place():
    x, y, c = lax.axis_index("x"), lax.axis_index("y"), lax.axis_index("c")
    return x, y, c


def _peer(place, k):
    x, y, c = place
    px = 1 - x if k & 4 else x
    py = 1 - y if k & 2 else y
    pc = 1 - c if k & 1 else c
    return (px, py, pc), 4 * px + 2 * py + pc


class _Exchange:
    def __init__(self, srcs, gather):
        self.n = len(srcs)
        self.gather = gather
        self.out_shape = [jax.ShapeDtypeStruct((N_DEV,) + tuple(s.shape if gather else s.shape[1:]), s.dtype)
                          for s in srcs]
        self.scratch = [pltpu.SemaphoreType.DMA((self.n * (N_DEV - 1),)),
                        pltpu.SemaphoreType.DMA((self.n * (N_DEV - 1),)),
                        pltpu.SemaphoreType.DMA((self.n,))]

    def _copies(self, src_refs, out_refs, sems):
        send_sems, recv_sems, local_sems = sems
        place = _mesh_place()
        me = 4 * place[0] + 2 * place[1] + place[2]
        local, sends, recvs = [], [], []
        for a, (src, out) in enumerate(zip(src_refs, out_refs)):
            outgoing = (lambda idx, src=src: src) if self.gather else (lambda idx, src=src: src.at[idx])
            local.append(pltpu.make_async_copy(outgoing(me), out.at[me], local_sems.at[a]))
            for k in range(1, N_DEV):
                peer, peer_idx = _peer(place, k)
                sem = a * (N_DEV - 1) + k - 1
                sends.append(pltpu.make_async_remote_copy(
                    src_ref=outgoing(peer_idx), dst_ref=out.at[me], send_sem=send_sems.at[sem],
                    recv_sem=recv_sems.at[sem], device_id=peer, device_id_type=MESH_ID))
                recvs.append(pltpu.make_async_remote_copy(
                    src_ref=outgoing(peer_idx), dst_ref=out.at[peer_idx], send_sem=send_sems.at[sem],
                    recv_sem=recv_sems.at[sem], device_id=peer, device_id_type=MESH_ID))
        return local, sends, recvs

    def start(self, src_refs, out_refs, sems):
        local, sends, _ = self._copies(src_refs, out_refs, sems)
        for cp in local + sends:
            cp.start()

    def wait(self, src_refs, out_refs, sems):
        local, sends, recvs = self._copies(src_refs, out_refs, sems)
        for cp in recvs:
            cp.wait_recv()
        for cp in sends:
            cp.wait_send()
        for cp in local:
            cp.wait()


def _call_carrying(body, carry, *, name, grid, in_specs, out_specs, out_shape, scratch_shapes, semantics, args):
    if carry is None:
        outs = pl.pallas_call(body, name=name, grid=grid, in_specs=in_specs, out_specs=out_specs,
                              out_shape=out_shape, scratch_shapes=scratch_shapes,
                              compiler_params=_cp(*semantics))(*args)
        return outs, []
    srcs, gather = carry
    ex = _Exchange(srcs, gather)
    n, n_in, n_out, n_scr = ex.n, len(in_specs), len(out_specs), len(scratch_shapes)

    def wrapped(*refs):
        ins, refs = refs[:n_in], refs[n_in:]
        src_refs, refs = refs[:n], refs[n:]
        outs, refs = refs[:n_out], refs[n_out:]
        dst_refs, refs = refs[:n], refs[n:]
        scratch, sems = refs[:n_scr], refs[n_scr:]
        first, last = None, None
        for axis, size in enumerate(grid):
            i = pl.program_id(axis)
            first = (i == 0) if first is None else first & (i == 0)
            last = (i == size - 1) if last is None else last & (i == size - 1)

        @pl.when(first)
        def _():
            ex.start(src_refs, dst_refs, sems)

        body(*ins, *outs, *scratch)

        @pl.when(last)
        def _():
            ex.wait(src_refs, dst_refs, sems)

    any_space = pl.BlockSpec(memory_space=pl.ANY)
    res = pl.pallas_call(
        wrapped, name=name + "_carrying", grid=grid, in_specs=list(in_specs) + [any_space] * n,
        out_specs=list(out_specs) + [any_space] * n, out_shape=list(out_shape) + ex.out_shape,
        scratch_shapes=list(scratch_shapes) + ex.scratch,
        compiler_params=_cp(*(["arbitrary"] * len(grid))))(*args, *srcs)
    return res[:n_out], res[n_out:]


def _exchange(srcs, name, gather):
    ex = _Exchange(srcs, gather)
    n = ex.n

    def body(*refs):
        src_refs, out_refs, sems = refs[:n], refs[n:2 * n], refs[2 * n:]
        ex.start(src_refs, out_refs, sems)
        ex.wait(src_refs, out_refs, sems)

    any_space = pl.BlockSpec(memory_space=pl.ANY)
    return pl.pallas_call(
        body, name=name, in_specs=[any_space] * n, out_specs=[any_space] * n,
        out_shape=ex.out_shape, scratch_shapes=ex.scratch)(*srcs)


def _reduce_adamw(parts, w, m, v, name):
    rows, cols = w.shape
    n_seg = len(parts)
    seg_rows = rows // n_seg
    tr = _row_tile(seg_rows, 128)
    per_seg = seg_rows // tr
    c1 = np.float32(1.0 - ADAM_B1 ** ADAM_STEP)
    c2 = np.float32(1.0 - ADAM_B2 ** ADAM_STEP)

    def body(*refs):
        p_refs = refs[:n_seg]
        w_ref, m_ref, v_ref, g_ref, d_ref, nm_ref, nv_ref = refs[n_seg:]
        seg = pl.program_id(0)
        for k, p_ref in enumerate(p_refs):
            @pl.when(seg == k)
            def _(p_ref=p_ref):
                g = p_ref[0].astype(F32)
                for dev in range(1, N_DEV):
                    g = g + p_ref[dev].astype(F32)
                g_ref[...] = g

        g = g_ref[...]
        nm = ADAM_B1 * m_ref[...] + (1.0 - ADAM_B1) * g
        nv = ADAM_B2 * v_ref[...] + (1.0 - ADAM_B2) * (g * g)
        nm_ref[...] = nm
        nv_ref[...] = nv
        d_ref[...] = -ADAM_LR * ((nm / c1) / (jnp.sqrt(nv / c2) + ADAM_EPS) + ADAM_WD * w_ref[...])

    def part_spec(k):
        return pl.BlockSpec((N_DEV, tr, cols), lambda s, i, k=k: (0, jnp.where(s == k, i, 0), 0))

    blk = pl.BlockSpec((tr, cols), lambda s, i: (s * per_seg + i, 0))
    shp = jax.ShapeDtypeStruct((rows, cols), F32)
    return pl.pallas_call(
        body, name=name, grid=(n_seg, per_seg),
        in_specs=[part_spec(k) for k in range(n_seg)] + [blk, blk, blk],
        out_specs=[blk, blk, blk, blk], out_shape=[shp, shp, shp, shp],
        compiler_params=_cp("arbitrary", "arbitrary"))(*parts, w, m, v)


def _pack(arrays, lead=0):
    parts = []
    for a in arrays:
        f = a.reshape(a.shape[:lead] + (-1, LANES))
        pad = -f.shape[lead] % PACK_ROWS
        if pad:
            f = jnp.pad(f, [(0, 0)] * lead + [(0, pad), (0, 0)])
        parts.append(f)
    return jnp.concatenate(parts, axis=lead)


def _unpack(buf, shapes, lead=0):
    out, r = [], 0
    for shp in shapes:
        n = int(np.prod(shp)) // LANES
        part = lax.slice_in_dim(buf, r, r + n, axis=lead)
        out.append(part.reshape(buf.shape[:lead] + tuple(shp)))
        r += n + (-n % PACK_ROWS)
    return out


REPLICATED = ("lb_logits", "norm_mix", "conv_b", "b_r", "b_i", "lam", "hg_norm", "norm_mlp", "norm_final")
SMALL_SHARDED = ("conv_w", "w_r", "w_i")
LARGE_SHARDED = ("w_in", "w_out", "w_up", "w_down")
WEIGHTS = ("lb_logits", "norm_mix", "w_in", "conv_w", "conv_b", "w_r", "b_r", "w_i", "b_i", "lam", "hg_norm",
           "w_out", "norm_mlp", "w_up", "w_down", "norm_final")


def _matmul_weight_shards(p):
    depth = p["w_in"].shape[0]
    cast = {k: p[k].astype(BF16) for k in LARGE_SHARDED}
    return ([cast["w_in"][l] for l in range(depth)],
            [[cast[k][l] for k in ("w_out", "w_up", "w_down")] for l in range(depth)])


def _gathered_rest(got):
    w_out, w_up, w_down = got
    d = w_out.shape[2]
    return dict(w_out=w_out.reshape(d, d), w_up=w_up, w_down=w_down)


def _unpack_mixer_weights(small, p):
    depth, d, _ = p["w_in"].shape
    n_blk = d // RG_BLOCK_W
    conv_w, w_r, w_i = _unpack(small, [p["conv_w"].shape, p["w_r"].shape, p["w_i"].shape], lead=1)
    conv_w = conv_w.transpose(1, 2, 0, 3).reshape(depth, CONV_W, d)
    w_r = w_r.transpose(1, 2, 0, 3, 4).reshape(depth, n_blk, RG_BLOCK_W, RG_BLOCK_W).astype(BF16)
    w_i = w_i.transpose(1, 2, 0, 3, 4).reshape(depth, n_blk, RG_BLOCK_W, RG_BLOCK_W).astype(BF16)
    return conv_w, w_r, w_i


def _local_step(x, target, p):
    bl, seq, d = x.shape
    depth = p["w_in"].shape[0]
    t_rows = bl * seq
    row = lambda a, l: a[l:l + 1]
    lbs = _lower_bounds_fwd(p["lb_logits"])
    shard_in, shard_rest = _matmul_weight_shards(p)
    w_in = _exchange([shard_in[0]], "gather_w_in", gather=True)[0]
    cur = x.reshape(t_rows, d)
    saved, layers = [], []
    for l in range(depth):
        if l == 0:
            (proj, h), small = _inproj_fwd(cur, row(p["norm_mix"], l), w_in,
                                           carry=([_pack([p["conv_w"], p["w_r"], p["w_i"]])], True))
            conv_w, w_r, w_i = _unpack_mixer_weights(small[0], p)
        else:
            (proj, h), _ = _inproj_fwd(cur, row(p["norm_mix"], l), w_in)
        w = dict(w_in=w_in, conv_w=conv_w[l], w_r=w_r[l], w_i=w_i[l])
        hs, y_a = _mixer_a_fwd(proj, w["conv_w"], row(p["conv_b"], l), w["w_r"], row(p["b_r"], l), w["w_i"],
                               row(p["b_i"], l), row(p["lam"], l), seq)
        (o, o_n), got = _hgrn_fwd(proj, row(lbs, l), row(p["hg_norm"], l), seq,
                                  carry=(shard_rest[l] + ([shard_in[l + 1]] if l + 1 < depth else []), True))
        w.update(_gathered_rest(got[:3]))
        w_in = got[3] if l + 1 < depth else None
        layers.append(w)
        x_mid, y = _merge_out_fwd(proj, y_a, o_n, cur, w["w_out"])
        x_out, u, h2 = _mlp_fwd(x_mid, row(p["norm_mlp"], l), w["w_up"], w["w_down"])
        saved.append(dict(x_in=cur, proj=proj, h=h, hs=hs, y_a=y_a, o=o, o_n=o_n, x_mid=x_mid, y=y, u=u, h2=h2))
        cur = x_out
    loss8, dx, dxb, g_norm_final = _loss_head(cur, p["norm_final"].reshape(1, d), target.reshape(t_rows, d))
    small = ("norm_mix", "conv_w", "conv_b", "w_r", "b_r", "w_i", "b_i", "lam", "hg_norm", "norm_mlp")
    g = {k: [None] * depth for k in small}
    d_lbs, received = [None] * depth, [None] * depth
    g_w_in = None
    for l in reversed(range(depth)):
        s, w = saved[l], layers[l]
        dx_mid, dx_mid_b, du, act, g["norm_mlp"][l] = _mlp_bwd(dx, dxb, s["u"], s["x_mid"], row(p["norm_mlp"], l),
                                                               w["w_up"], w["w_down"])
        g_w_down = _wgrad(act, dxb[None], "wgrad_down")
        g_w_up = _wgrad(s["h2"][None], du, "wgrad_up")
        d_ya, d_on, dp_c = _outproj_bwd(dx_mid_b, w["w_out"], s["proj"], s["y_a"], s["o_n"])
        g_w_out = _wgrad(s["y"][None], dx_mid_b[None], "wgrad_out").reshape(N_DEV, d // N_DEV, d)
        (dp_b, d_lbs[l], g["hg_norm"][l]), got = _hgrn_bwd(
            s["proj"], row(lbs, l), row(p["hg_norm"], l), s["o"], d_on, seq,
            carry=([g_w_out, g_w_up, g_w_down] + ([g_w_in] if g_w_in is not None else []), False))
        received[l] = [None] + list(got[:3])
        if g_w_in is not None:
            received[l + 1][0] = got[3]
        (dp_a, g["w_r"][l], g["w_i"][l], g["b_r"][l], g["b_i"][l], g["lam"][l], g["conv_w"][l],
         g["conv_b"][l]) = _mixer_a_bwd(s["proj"], s["hs"], d_ya, w["conv_w"], row(p["conv_b"], l), w["w_r"],
                                        row(p["b_r"], l), w["w_i"], row(p["b_i"], l), row(p["lam"], l), seq)
        hb = s["h"][None]
        g_w_in = jnp.concatenate([_wgrad(hb, dp_a, "wgrad_in_pair"), _wgrad(hb, dp_b, "wgrad_in_triple"),
                                  _wgrad(hb, dp_c, "wgrad_in_triple")], axis=0)
        carry = None
        if l == 0:
            carry = ([g_w_in, _mixer_grads_by_owner(g, d)], False)
        (dx, dxb, g["norm_mix"][l]), got = _inproj_bwd(dx_mid, dp_a, dp_b, dp_c, w["w_in"], s["x_in"],
                                                       row(p["norm_mix"], l), carry=carry)
    received[0][0], received_mixer = got
    grads = {k: jnp.stack(v) for k, v in g.items()}
    for k in ("norm_mix", "conv_b", "b_r", "b_i", "lam", "hg_norm", "norm_mlp"):
        grads[k] = grads[k][:, 0]
    grads["lb_logits"] = _lower_bounds_bwd(p["lb_logits"], jnp.concatenate(d_lbs, axis=0))
    grads["norm_final"] = g_norm_final[0]
    return loss8[0, 0], dx.reshape(bl, seq, d), grads, received, received_mixer


def _mixer_grads_by_owner(g, d):
    d8, n_blk, rb = d // N_DEV, d // RG_BLOCK_W, RG_BLOCK_W // N_DEV
    depth = len(g["conv_w"])
    conv_w, w_r, w_i = (jnp.stack(g[k]) for k in SMALL_SHARDED)
    return _pack([conv_w.reshape(depth, CONV_W, N_DEV, d8).transpose(2, 0, 1, 3),
                  w_r.reshape(depth, n_blk, N_DEV, rb, RG_BLOCK_W).transpose(2, 0, 1, 3, 4),
                  w_i.reshape(depth, n_blk, N_DEV, rb, RG_BLOCK_W).transpose(2, 0, 1, 3, 4)], lead=1)


def _update(p, mom1, mom2, grads, received, received_mixer):
    depth = p["w_in"].shape[0]
    out = {}

    for i, k in enumerate(LARGE_SHARDED):
        shp = p[k].shape
        flat = lambda a: a.reshape(shp[0] * shp[1], shp[2])
        parts = [received[l][i] for l in range(depth)]
        res = _reduce_adamw(parts, flat(p[k]), flat(mom1[k]), flat(mom2[k]), "adamw_" + k)
        out[k] = [r.reshape(shp) for r in res]

    res = _reduce_adamw([received_mixer], *[_pack([src[k] for k in SMALL_SHARDED]) for src in (p, mom1, mom2)],
                        "adamw_mixer")
    shapes = [p[k].shape for k in SMALL_SHARDED]
    for i, vals in enumerate(zip(*[_unpack(r, shapes) for r in res])):
        out[SMALL_SHARDED[i]] = list(vals)

    parts = _exchange([_pack([grads[k] for k in REPLICATED])], "gather_grad_replicated", gather=True)
    res = _reduce_adamw(parts, *[_pack([src[k] for k in REPLICATED]) for src in (p, mom1, mom2)],
                        "adamw_replicated")
    shapes = [p[k].shape for k in REPLICATED]
    for i, vals in enumerate(zip(*[_unpack(r, shapes) for r in res])):
        out[REPLICATED[i]] = list(vals)

    return tuple(out[k][i] for i in range(4) for k in WEIGHTS)


def kernel(x, lb_logits, norm_mix, w_in, conv_w, conv_b, w_r, b_r, w_i, b_i, lam, hg_norm, w_out, norm_mlp, w_up, w_down, norm_final, loss_target, m_lb_logits, m_norm_mix, m_w_in, m_conv_w, m_conv_b, m_w_r, m_b_r, m_w_i, m_b_i, m_lam, m_hg_norm, m_w_out, m_norm_mlp, m_w_up, m_w_down, m_norm_final, v_lb_logits, v_norm_mix, v_w_in, v_conv_w, v_conv_b, v_w_r, v_b_r, v_w_i, v_b_i, v_lam, v_hg_norm, v_w_out, v_norm_mlp, v_w_up, v_w_down, v_norm_final):
    p = dict(lb_logits=lb_logits, norm_mix=norm_mix, w_in=w_in, conv_w=conv_w, conv_b=conv_b, w_r=w_r, b_r=b_r,
             w_i=w_i, b_i=b_i, lam=lam, hg_norm=hg_norm, w_out=w_out, norm_mlp=norm_mlp, w_up=w_up,
             w_down=w_down, norm_final=norm_final)
    mom1 = dict(lb_logits=m_lb_logits, norm_mix=m_norm_mix, w_in=m_w_in, conv_w=m_conv_w, conv_b=m_conv_b,
                w_r=m_w_r, b_r=m_b_r, w_i=m_w_i, b_i=m_b_i, lam=m_lam, hg_norm=m_hg_norm, w_out=m_w_out,
                norm_mlp=m_norm_mlp, w_up=m_w_up, w_down=m_w_down, norm_final=m_norm_final)
    mom2 = dict(lb_logits=v_lb_logits, norm_mix=v_norm_mix, w_in=v_w_in, conv_w=v_conv_w, conv_b=v_conv_b,
                w_r=v_w_r, b_r=v_b_r, w_i=v_w_i, b_i=v_b_i, lam=v_lam, hg_norm=v_hg_norm, w_out=v_w_out,
                norm_mlp=v_norm_mlp, w_up=v_w_up, w_down=v_w_down, norm_final=v_norm_final)
    loss, grad_x, grads, received, received_mixer = _local_step(x, loss_target, p)
    loss = lax.psum(loss, ("x", "y", "c"))
    return (loss, grad_x) + _update(p, mom1, mom2, grads, received, received_mixer)
```

```python
import numpy as np

import jax
import jax.numpy as jnp
from jax import lax
from jax.experimental import pallas as pl
from jax.experimental.pallas import tpu as pltpu

F32 = jnp.float32
BF16 = jnp.bfloat16
MESH_ID = pl.DeviceIdType.MESH

N_DEV = 8
N_MIXER_SEGMENTS = 5
NORM_EPS = 1e-6
RG_C = 8.0
RG_BLOCK_W = 256
CONV_W = 4
HG_DK = 128
F_MIN = 1e-30
HG_CHUNK = 16
SUBLANES = 8
LANES = 128
PACK_ROWS = 16
SCAN_GROUP = 16
ROW_CHUNK = 256
ROW_TILE_WEIGHT_STREAM = 1024
WGRAD_TOKEN_TILE = 2048
VMEM_LIMIT_V7X = 56 * 1024 * 1024

ADAM_LR = 0.001
ADAM_B1 = 0.9
ADAM_B2 = 0.999
ADAM_EPS = 1e-08
ADAM_WD = 0.01
ADAM_STEP = 10

GELU_C = 0.7978845608028654
GELU_K = 0.044715


def _cp(*sem):
    return pltpu.CompilerParams(dimension_semantics=sem, vmem_limit_bytes=VMEM_LIMIT_V7X)


def _row_tile(n, cap):
    if n <= cap:
        return n
    t = cap - cap % 16
    while n % t:
        t -= 16
    return t


def _dot(a, b):
    return jnp.dot(a, b, preferred_element_type=F32)


def _dot_nt(a, b):
    return lax.dot_general(a, b, (((1,), (1,)), ((), ())), preferred_element_type=F32)


def _dot_tn(a, b):
    return lax.dot_general(a, b, (((0,), (0,)), ((), ())), preferred_element_type=F32)


def _sigmoid(x):
    return jax.nn.sigmoid(x)


def _sigmoid_pair(x):
    e = jnp.exp(-jnp.abs(x))
    r = 1.0 / (1.0 + e)
    er = e * r
    pos = x >= 0.0
    return jnp.where(pos, r, er), jnp.where(pos, er, r)


def _log1p_pos(y):
    return jnp.where(y < 0.01, y * (1.0 - y * (0.5 - y * (1.0 / 3.0))), jnp.log(1.0 + y))


def _softplus(x):
    return jnp.maximum(x, 0.0) + _log1p_pos(jnp.exp(-jnp.abs(x)))


def _one_minus_exp(x):
    series = -x * (1.0 + x * 0.5 * (1.0 + x * (1.0 / 3.0) * (1.0 + x * 0.25 * (1.0 + x * 0.2))))
    return jnp.where(x > -0.1, series, 1.0 - jnp.exp(x))


def _gelu_and_grad(x):
    x2 = x * x
    t = jnp.tanh(GELU_C * x * (1.0 + GELU_K * x2))
    g = 0.5 * x * (1.0 + t)
    dg = 0.5 * (1.0 + t) + 0.5 * x * (1.0 - t * t) * GELU_C * (1.0 + 3.0 * GELU_K * x2)
    return g, dg


def _silu_and_grad(x):
    s = _sigmoid(x)
    return x * s, s * (1.0 + x * (1.0 - s))


def _rstd(x):
    return lax.rsqrt(jnp.mean(x * x, axis=-1, keepdims=True) + NORM_EPS)


def _rms_bwd(dh, x, g):
    rstd = _rstd(x)
    xh = x * rstd
    dxh = dh * g
    dx = rstd * (dxh - xh * jnp.mean(dxh * xh, axis=-1, keepdims=True))
    return dx, jnp.sum(dh * xh, axis=0, keepdims=True)


def _shift_rows(x, k):
    n = x.shape[0]
    k = k % n
    return x if k == 0 else pltpu.roll(x, k, axis=0)


def _seg_cumsum(x, seg, reverse=False):
    n = x.shape[0]
    rid = lax.broadcasted_iota(jnp.int32, x.shape, 0) & (seg - 1)
    d = 1
    while d < seg:
        if reverse:
            x = jnp.where(rid < seg - d, x + _shift_rows(x, n - d), x)
        else:
            x = jnp.where(rid >= d, x + _shift_rows(x, d), x)
        d *= 2
    return x


def _group_cumsum_matrix(n, seg):
    row = lax.broadcasted_iota(jnp.int32, (n, n), 0)
    col = lax.broadcasted_iota(jnp.int32, (n, n), 1)
    same_group = (row & ~(seg - 1)) == (col & ~(seg - 1))
    return jnp.where(same_group & (col <= row), 1.0, 0.0).astype(BF16)


def _group_cumsum_mxu(x, tri):
    hi = x.astype(BF16)
    lo = (x - hi.astype(F32)).astype(BF16)
    return _dot(tri, hi) + _dot(tri, lo)


def _scan_rows(a_ref, b_ref, out_ref, n_rows, width, reverse):
    gr = min(SCAN_GROUP, n_rows)
    rid = lax.broadcasted_iota(jnp.int32, (gr, width), 0)
    n_groups = n_rows // gr
    per_trip = min(4, n_groups)
    assert n_groups % per_trip == 0

    def local_scan(g):
        r0 = pl.multiple_of(g * gr, gr)
        a = a_ref[pl.ds(r0, gr), :]
        b = b_ref[pl.ds(r0, gr), :]
        d = 1
        while d < gr:
            if reverse:
                keep = rid < gr - d
                a_sh, b_sh = _shift_rows(a, gr - d), _shift_rows(b, gr - d)
            else:
                keep = rid >= d
                a_sh, b_sh = _shift_rows(a, d), _shift_rows(b, d)
            b = jnp.where(keep, a * b_sh + b, b)
            a = jnp.where(keep, a * a_sh, a)
            d *= 2
        return r0, a, b

    def trip(i, carry):
        first = i * per_trip
        groups = [n_groups - 1 - (first + u) if reverse else first + u for u in range(per_trip)]
        for r0, a, b in [local_scan(g) for g in groups]:
            out = a * carry + b
            out_ref[pl.ds(r0, gr), :] = out
            edge = out[0:1, :] if reverse else out[gr - 1:gr, :]
            carry = jnp.broadcast_to(edge, (gr, width))
        return carry

    lax.fori_loop(0, n_groups // per_trip, trip, jnp.zeros((gr, width), F32))


def _lb_softmax_rows(x_ref, depth):
    rows = [x_ref[pl.ds(l, 1), :] for l in range(depth)]
    top = rows[0]
    for r in rows[1:]:
        top = jnp.maximum(top, r)
    e = [jnp.exp(r - top) for r in rows]
    tot = e[0]
    for r in e[1:]:
        tot = tot + r
    return [r / tot for r in e]


def _lower_bounds_fwd(lb_logits):
    depth, d = lb_logits.shape

    def body(x_ref, o_ref):
        sm = _lb_softmax_rows(x_ref, depth)
        cum = jnp.zeros((1, d), F32)
        for l in range(depth):
            cum = cum + sm[l]
            o_ref[pl.ds(l, 1), :] = jnp.clip(cum - sm[0], 0.0, 1.0)

    return pl.pallas_call(body, name="lower_bounds_fwd",
                          out_shape=jax.ShapeDtypeStruct((depth, d), F32))(lb_logits)


def _lower_bounds_bwd(lb_logits, d_lbs):
    depth, d = lb_logits.shape

    def body(x_ref, g_ref, o_ref):
        sm = _lb_softmax_rows(x_ref, depth)
        cum = jnp.zeros((1, d), F32)
        d_cum = []
        for l in range(depth):
            cum = cum + sm[l]
            v = cum - sm[0]
            d_cum.append(jnp.where((v > 0.0) & (v < 1.0), g_ref[pl.ds(l, 1), :], 0.0))
        d_sm = []
        tail = jnp.zeros((1, d), F32)
        for l in reversed(range(depth)):
            tail = tail + d_cum[l]
            d_sm.append(tail)
        d_sm = d_sm[::-1]
        d_sm[0] = d_sm[0] - tail
        inner = jnp.zeros((1, d), F32)
        for l in range(depth):
            inner = inner + sm[l] * d_sm[l]
        for l in range(depth):
            o_ref[pl.ds(l, 1), :] = sm[l] * (d_sm[l] - inner)

    return pl.pallas_call(body, name="lower_bounds_bwd",
                          out_shape=jax.ShapeDtypeStruct((depth, d), F32))(lb_logits, d_lbs)


def _inproj_fwd(x, gain, w_seg, carry=None):
    t_rows, d = x.shape
    tm = _row_tile(t_rows, ROW_TILE_WEIGHT_STREAM)
    n_gate = N_DEV - N_MIXER_SEGMENTS

    def body(x_ref, g_ref, w_ref, proj_ref, gates_ref, h_ref):
        j = pl.program_id(1)

        @pl.when(j == 0)
        def _():
            xv = x_ref[...]
            h_ref[...] = (xv * _rstd(xv) * g_ref[...]).astype(BF16)

        @pl.when(j < N_MIXER_SEGMENTS)
        def _():
            proj_ref[...] = _dot(h_ref[...], w_ref[...])

        @pl.when(j >= N_MIXER_SEGMENTS)
        def _():
            gates_ref[...] = _dot(h_ref[...], w_ref[...]).astype(BF16)

    return _call_carrying(
        body, carry, name="inproj_fwd", grid=(t_rows // tm, N_DEV),
        in_specs=[pl.BlockSpec((tm, d), lambda i, j: (i, 0)),
                  pl.BlockSpec((1, d), lambda i, j: (0, 0)),
                  pl.BlockSpec((None, d, d), lambda i, j: (j, 0, 0))],
        out_specs=[pl.BlockSpec((None, tm, d), lambda i, j: (jnp.minimum(j, N_MIXER_SEGMENTS - 1), i, 0)),
                   pl.BlockSpec((None, tm, d), lambda i, j: (jnp.maximum(j - N_MIXER_SEGMENTS, 0), i, 0)),
                   pl.BlockSpec((tm, d), lambda i, j: (i, 0))],
        out_shape=[jax.ShapeDtypeStruct((N_MIXER_SEGMENTS, t_rows, d), F32),
                   jax.ShapeDtypeStruct((n_gate, t_rows, d), BF16),
                   jax.ShapeDtypeStruct((t_rows, d), BF16)],
        scratch_shapes=[], semantics=("parallel", "arbitrary"), args=(x, gain, w_seg))


def _merge_out_fwd(gates, y_a, o_n, x, w_out):
    t_rows, d = x.shape
    tm = _row_tile(t_rows, 256)

    def body(g_ref, ma_ref, mb_ref, ya_ref, on_ref, x_ref, w_ref, xmid_ref, y_ref):
        g = g_ref[...].astype(F32)
        ya, on = ya_ref[...].astype(F32), on_ref[...].astype(F32)
        y = (_sigmoid(ma_ref[...].astype(F32)) * ya
             + _sigmoid(mb_ref[...].astype(F32)) * (on * (g * _sigmoid(g))))
        yb = y.astype(BF16)
        y_ref[...] = yb
        xmid_ref[...] = x_ref[...] + _dot(yb, w_ref[...])

    seg = lambda k: pl.BlockSpec((None, tm, d), lambda i, k=k: (k, i, 0))
    row = pl.BlockSpec((tm, d), lambda i: (i, 0))
    return pl.pallas_call(
        body, name="merge_out_fwd", grid=(t_rows // tm,),
        in_specs=[seg(0), seg(1), seg(2), row, row, row, pl.BlockSpec((d, d), lambda i: (0, 0))],
        out_specs=[row, row],
        out_shape=[jax.ShapeDtypeStruct((t_rows, d), F32), jax.ShapeDtypeStruct((t_rows, d), BF16)],
        compiler_params=_cp("parallel"))(gates, gates, gates, y_a, o_n, x, w_out)


def _mlp_fwd(x_mid, gain, w_up, w_down):
    t_rows, d = x_mid.shape
    f8 = w_up.shape[2]
    tm = _row_tile(t_rows, ROW_TILE_WEIGHT_STREAM)

    def body(x_ref, g_ref, wu_ref, wd_ref, out_ref, u_ref, h_ref):
        @pl.when(pl.program_id(1) == 0)
        def _():
            xv = x_ref[...]
            h_ref[...] = (xv * _rstd(xv) * g_ref[...]).astype(BF16)
            out_ref[...] = xv

        u = _dot(h_ref[...], wu_ref[...])
        u_ref[...] = u
        r = jnp.maximum(u, 0.0)
        out_ref[...] += _dot((r * r).astype(BF16), wd_ref[...])

    row = pl.BlockSpec((tm, d), lambda i, j: (i, 0))
    return pl.pallas_call(
        body, name="mlp_fwd", grid=(t_rows // tm, N_DEV),
        in_specs=[row, pl.BlockSpec((1, d), lambda i, j: (0, 0)),
                  pl.BlockSpec((None, d, f8), lambda i, j: (j, 0, 0)),
                  pl.BlockSpec((None, f8, d), lambda i, j: (j, 0, 0))],
        out_specs=[row, pl.BlockSpec((None, tm, f8), lambda i, j: (j, i, 0)), row],
        out_shape=[jax.ShapeDtypeStruct((t_rows, d), F32),
                   jax.ShapeDtypeStruct((N_DEV, t_rows, f8), F32),
                   jax.ShapeDtypeStruct((t_rows, d), BF16)],
        compiler_params=_cp("parallel", "arbitrary"))(x_mid, gain, w_up, w_down)


def _loss_head(x, gain, target):
    t_rows, d = x.shape
    tm = _row_tile(t_rows, 512)

    def body(x_ref, g_ref, t_ref, loss_ref, dx_ref, dxb_ref, dg_ref):
        @pl.when(pl.program_id(0) == 0)
        def _():
            loss_ref[...] = jnp.zeros_like(loss_ref)
            dg_ref[...] = jnp.zeros_like(dg_ref)

        xv = x_ref[...]
        g = g_ref[...]
        err = xv * _rstd(xv) * g - t_ref[...]
        loss_ref[...] += (0.5 / d) * jnp.sum(err * err)
        dx, dg = _rms_bwd(err * (1.0 / d), xv, g)
        dx_ref[...] = dx
        dxb_ref[...] = dx.astype(BF16)
        dg_ref[...] += dg

    row = pl.BlockSpec((tm, d), lambda i: (i, 0))
    vec = pl.BlockSpec((1, d), lambda i: (0, 0))
    return pl.pallas_call(
        body, name="loss_head", grid=(t_rows // tm,),
        in_specs=[row, vec, row],
        out_specs=[pl.BlockSpec((SUBLANES, LANES), lambda i: (0, 0)), row, row, vec],
        out_shape=[jax.ShapeDtypeStruct((SUBLANES, LANES), F32),
                   jax.ShapeDtypeStruct((t_rows, d), F32),
                   jax.ShapeDtypeStruct((t_rows, d), BF16),
                   jax.ShapeDtypeStruct((1, d), F32)],
        compiler_params=_cp("arbitrary"))(x, gain, target)


def _mlp_bwd(d_out, d_out_b, u, x_mid, gain, w_up, w_down):
    t_rows, d = x_mid.shape
    f8 = w_up.shape[2]
    tm = _row_tile(t_rows, ROW_TILE_WEIGHT_STREAM)
    sub = _row_tile(tm, ROW_CHUNK)

    def body(do_ref, dob_ref, u_ref, x_ref, g_ref, wu_ref, wd_ref, dx_ref, dxb_ref, du_ref, act_ref, dg_ref):
        j = pl.program_id(1)

        @pl.when((pl.program_id(0) == 0) & (j == 0))
        def _():
            dg_ref[...] = jnp.zeros_like(dg_ref)

        @pl.when(j == 0)
        def _():
            dx_ref[...] = jnp.zeros_like(dx_ref)

        r = jnp.maximum(u_ref[...], 0.0)
        act_ref[...] = (r * r).astype(BF16)
        du = (_dot_nt(dob_ref[...], wd_ref[...]) * (2.0 * r)).astype(BF16)
        du_ref[...] = du
        dx_ref[...] += _dot_nt(du, wu_ref[...])

        @pl.when(j == N_DEV - 1)
        def _():
            def finish(c, _):
                rows = pl.ds(pl.multiple_of(c * sub, sub), sub)
                dx, dg = _rms_bwd(dx_ref[rows, :], x_ref[rows, :], g_ref[...])
                dx = dx + do_ref[rows, :]
                dx_ref[rows, :] = dx
                dxb_ref[rows, :] = dx.astype(BF16)
                dg_ref[...] += dg
                return 0

            lax.fori_loop(0, tm // sub, finish, 0)

    row = pl.BlockSpec((tm, d), lambda i, j: (i, 0))
    vec = pl.BlockSpec((1, d), lambda i, j: (0, 0))
    hid = pl.BlockSpec((None, tm, f8), lambda i, j: (j, i, 0))
    return pl.pallas_call(
        body, name="mlp_bwd", grid=(t_rows // tm, N_DEV),
        in_specs=[row, row, hid, row, vec,
                  pl.BlockSpec((None, d, f8), lambda i, j: (j, 0, 0)),
                  pl.BlockSpec((None, f8, d), lambda i, j: (j, 0, 0))],
        out_specs=[row, row, hid, hid, vec],
        out_shape=[jax.ShapeDtypeStruct((t_rows, d), F32),
                   jax.ShapeDtypeStruct((t_rows, d), BF16),
                   jax.ShapeDtypeStruct((N_DEV, t_rows, f8), BF16),
                   jax.ShapeDtypeStruct((N_DEV, t_rows, f8), BF16),
                   jax.ShapeDtypeStruct((1, d), F32)],
        compiler_params=_cp("arbitrary", "arbitrary"))(d_out, d_out_b, u, x_mid, gain, w_up, w_down)


def _outproj_bwd(dx_mid_b, w_out, gates, y_a, o_n):
    t_rows, d = y_a.shape
    tm = _row_tile(t_rows, 256)

    def body(dx_ref, w_ref, g_ref, ma_ref, mb_ref, ya_ref, on_ref, dya_ref, don_ref, dp_ref):
        dy = _dot_nt(dx_ref[...], w_ref[...])
        sa = _sigmoid(ma_ref[...].astype(F32))
        sb = _sigmoid(mb_ref[...].astype(F32))
        sg, dsg = _silu_and_grad(g_ref[...].astype(F32))
        ya = ya_ref[...].astype(F32)
        on = on_ref[...].astype(F32)
        dya_ref[...] = dy * sa
        t = dy * sb
        don_ref[...] = t * sg
        dp_ref[0] = (t * on * dsg).astype(BF16)
        dp_ref[1] = (dy * ya * sa * (1.0 - sa)).astype(BF16)
        dp_ref[2] = (dy * on * sg * sb * (1.0 - sb)).astype(BF16)

    seg = lambda k: pl.BlockSpec((None, tm, d), lambda i, k=k: (k, i, 0))
    row = pl.BlockSpec((tm, d), lambda i: (i, 0))
    return pl.pallas_call(
        body, name="outproj_bwd", grid=(t_rows // tm,),
        in_specs=[row, pl.BlockSpec((d, d), lambda i: (0, 0)), seg(0), seg(1), seg(2), row, row],
        out_specs=[row, row, pl.BlockSpec((3, tm, d), lambda i: (0, i, 0))],
        out_shape=[jax.ShapeDtypeStruct((t_rows, d), F32),
                   jax.ShapeDtypeStruct((t_rows, d), F32),
                   jax.ShapeDtypeStruct((3, t_rows, d), BF16)],
        compiler_params=_cp("parallel"))(dx_mid_b, w_out, gates, gates, gates, y_a, o_n)


def _inproj_bwd(dx_mid, dp_a, dp_b, dp_c, w_seg, x_in, gain, carry=None):
    t_rows, d = x_in.shape
    tm = _row_tile(t_rows, ROW_TILE_WEIGHT_STREAM)
    sub = _row_tile(tm, ROW_CHUNK)
    n_a, n_b = dp_a.shape[0], dp_b.shape[0]

    def body(dxm_ref, a_ref, b_ref, c_ref, w_ref, x_ref, g_ref, dx_ref, dxb_ref, dg_ref):
        j = pl.program_id(1)

        @pl.when((pl.program_id(0) == 0) & (j == 0))
        def _():
            dg_ref[...] = jnp.zeros_like(dg_ref)

        @pl.when(j == 0)
        def _():
            dx_ref[...] = jnp.zeros_like(dx_ref)

        @pl.when(j < n_a)
        def _():
            dx_ref[...] += _dot_nt(a_ref[...], w_ref[...])

        @pl.when((j >= n_a) & (j < n_a + n_b))
        def _():
            dx_ref[...] += _dot_nt(b_ref[...], w_ref[...])

        @pl.when(j >= n_a + n_b)
        def _():
            dx_ref[...] += _dot_nt(c_ref[...], w_ref[...])

        @pl.when(j == N_DEV - 1)
        def _():
            def finish(c, _):
                rows = pl.ds(pl.multiple_of(c * sub, sub), sub)
                dx, dg = _rms_bwd(dx_ref[rows, :], x_ref[rows, :], g_ref[...])
                dx = dx + dxm_ref[rows, :]
                dx_ref[rows, :] = dx
                dxb_ref[rows, :] = dx.astype(BF16)
                dg_ref[...] += dg
                return 0

            lax.fori_loop(0, tm // sub, finish, 0)

    def part(first, n):
        return pl.BlockSpec((None, tm, d), lambda i, j: (jnp.clip(j - first, 0, n - 1), i, 0))

    row = pl.BlockSpec((tm, d), lambda i, j: (i, 0))
    vec = pl.BlockSpec((1, d), lambda i, j: (0, 0))
    return _call_carrying(
        body, carry, name="inproj_bwd", grid=(t_rows // tm, N_DEV),
        in_specs=[row, part(0, n_a), part(n_a, n_b), part(n_a + n_b, dp_c.shape[0]),
                  pl.BlockSpec((None, d, d), lambda i, j: (j, 0, 0)), row, vec],
        out_specs=[row, row, vec],
        out_shape=[jax.ShapeDtypeStruct((t_rows, d), F32),
                   jax.ShapeDtypeStruct((t_rows, d), BF16),
                   jax.ShapeDtypeStruct((1, d), F32)],
        scratch_shapes=[], semantics=("arbitrary", "arbitrary"),
        args=(dx_mid, dp_a, dp_b, dp_c, w_seg, x_in, gain))


def _wgrad(a3, b3, name):
    n_a, t_rows, k_a = a3.shape
    n_b, _, n_cols = b3.shape
    n = max(n_a, n_b)
    bk = _row_tile(k_a, 1024)
    bn = n_cols if n_cols <= 1024 else 1024
    tt = _row_tile(t_rows, WGRAD_TOKEN_TILE)
    n_t = t_rows // tt

    def body(a_ref, b_ref, o_ref, acc_ref):
        t, j = pl.program_id(2), pl.program_id(3)
        part = _dot_tn(a_ref[...], b_ref[...])

        @pl.when(t == 0)
        def _():
            acc_ref[j] = part

        @pl.when(t > 0)
        def _():
            acc_ref[j] += part

        @pl.when(t == n_t - 1)
        def _():
            o_ref[...] = acc_ref[j].astype(BF16)

    def out_map(p, q, t, j):
        return (jnp.where(t == n_t - 1, j, 0), p, q)

    return pl.pallas_call(
        body, name=name, grid=(k_a // bk, n_cols // bn, n_t, n),
        in_specs=[pl.BlockSpec((None, tt, bk), lambda p, q, t, j: (j if n_a > 1 else 0, t, p)),
                  pl.BlockSpec((None, tt, bn), lambda p, q, t, j: (j if n_b > 1 else 0, t, q))],
        out_specs=pl.BlockSpec((None, bk, bn), out_map),
        out_shape=jax.ShapeDtypeStruct((n, k_a, n_cols), BF16),
        scratch_shapes=[pltpu.VMEM((n, bk, bn), F32)],
        compiler_params=_cp("parallel", "parallel", "arbitrary", "arbitrary"))(a3, b3)


def _conv_taps(xe, n):
    return [_shift_rows(xe, CONV_W - 1 - j)[SUBLANES:SUBLANES + n, :] for j in range(CONV_W)]


def _rg_gates(xc, w_r, b_r, w_i, b_i, sp8):
    xb = xc.astype(BF16)
    r = _sigmoid(_dot(xb, w_r) + b_r)
    i = _sigmoid(_dot(xb, w_i) + b_i)
    return r, i


def _mixer_a_fwd(proj, conv_w, conv_b, w_r, b_r, w_i, b_i, lam, seq):
    _, t_rows, d = proj.shape
    n_seq, n_blk = t_rows // seq, d // RG_BLOCK_W
    wb = RG_BLOCK_W
    ch = _row_tile(seq, ROW_CHUNK)

    def body(xa_ref, ga_ref, cw_ref, cb_ref, wr_ref, br_ref, wi_ref, bi_ref, lam_ref, h_ref, ya_ref,
             xpad, a_s, u_s):
        xpad[0:SUBLANES, :] = jnp.zeros((SUBLANES, wb), F32)
        xpad[SUBLANES:, :] = xa_ref[...]
        sp8 = RG_C * _softplus(-lam_ref[...])

        def gates(c, _):
            r0 = pl.multiple_of(c * ch, ch)
            taps = _conv_taps(xpad[pl.ds(r0, ch + SUBLANES), :], ch)
            xc = cb_ref[...] + sum(cw_ref[pl.ds(j, 1), :] * taps[j] for j in range(CONV_W))
            r, i = _rg_gates(xc, wr_ref[...], br_ref[...], wi_ref[...], bi_ref[...], sp8)
            log_a = -(r * sp8)
            a_s[pl.ds(r0, ch), :] = jnp.exp(log_a)
            u_s[pl.ds(r0, ch), :] = jnp.sqrt(jnp.maximum(_one_minus_exp(2.0 * log_a), 0.0)) * (i * xc)
            return 0

        lax.fori_loop(0, seq // ch, gates, 0)
        _scan_rows(a_s, u_s, h_ref, seq, wb, reverse=False)

        def gate_out(c, _):
            r0 = pl.multiple_of(c * ch, ch)
            gl, _ = _gelu_and_grad(ga_ref[pl.ds(r0, ch), :])
            ya_ref[pl.ds(r0, ch), :] = (h_ref[pl.ds(r0, ch), :] * gl).astype(BF16)
            return 0

        lax.fori_loop(0, seq // ch, gate_out, 0)

    seg = lambda k: pl.BlockSpec((None, seq, wb), lambda s, b, k=k: (k, s, b))
    blk = pl.BlockSpec((seq, wb), lambda s, b: (s, b))
    vec = pl.BlockSpec((1, wb), lambda s, b: (0, b))
    wsp = pl.BlockSpec((None, wb, wb), lambda s, b: (b, 0, 0))
    return pl.pallas_call(
        body, name="mixer_a_fwd", grid=(n_seq, n_blk),
        in_specs=[seg(0), seg(1), pl.BlockSpec((CONV_W, wb), lambda s, b: (0, b)), vec, wsp, vec, wsp, vec, vec],
        out_specs=[blk, blk],
        out_shape=[jax.ShapeDtypeStruct((t_rows, d), F32), jax.ShapeDtypeStruct((t_rows, d), BF16)],
        scratch_shapes=[pltpu.VMEM((seq + SUBLANES, wb), F32), pltpu.VMEM((seq, wb), F32),
                        pltpu.VMEM((seq, wb), F32)],
        compiler_params=_cp("parallel", "parallel"))(proj, proj, conv_w, conv_b, w_r, b_r, w_i, b_i, lam)


def _mixer_a_bwd(proj, h, d_ya, conv_w, conv_b, w_r, b_r, w_i, b_i, lam, seq):
    _, t_rows, d = proj.shape
    n_seq, n_blk = t_rows // seq, d // RG_BLOCK_W
    wb = RG_BLOCK_W
    ch = _row_tile(seq, ROW_CHUNK)
    n_ch = seq // ch

    def body(xa_ref, ga_ref, h_ref, dya_ref, cw_ref, cb_ref, wr_ref, br_ref, wi_ref, bi_ref, lam_ref,
             dp_ref, dwr_ref, dwi_ref, dbr_ref, dbi_ref, dlam_ref, dcw_ref, dcb_ref,
             xpad, hpad, a_s, e_pad, g_s, xc_s, r_s, i_s, dxc_pad):
        @pl.when(pl.program_id(1) == 0)
        def _():
            for ref in (dwr_ref, dwi_ref, dbr_ref, dbi_ref, dlam_ref, dcw_ref, dcb_ref):
                ref[...] = jnp.zeros_like(ref)

        zeros8 = jnp.zeros((SUBLANES, wb), F32)
        xpad[0:SUBLANES, :] = zeros8
        xpad[SUBLANES:, :] = xa_ref[...]
        hpad[0:SUBLANES, :] = zeros8
        hpad[SUBLANES:, :] = h_ref[...]
        e_pad[seq:, :] = zeros8
        dxc_pad[seq:, :] = zeros8
        lam_v = lam_ref[...]
        sp8 = RG_C * _softplus(-lam_v)

        def recompute(c, _):
            r0 = pl.multiple_of(c * ch, ch)
            rows = pl.ds(r0, ch)
            taps = _conv_taps(xpad[pl.ds(r0, ch + SUBLANES), :], ch)
            xc = cb_ref[...] + sum(cw_ref[pl.ds(j, 1), :] * taps[j] for j in range(CONV_W))
            r, i = _rg_gates(xc, wr_ref[...], br_ref[...], wi_ref[...], bi_ref[...], sp8)
            a = jnp.exp(-(r * sp8))
            gl, dgl = _gelu_and_grad(ga_ref[rows, :])
            dya = dya_ref[rows, :]
            g = dya * gl
            dp_ref[1, rows, :] = (dya * h_ref[rows, :] * dgl).astype(BF16)
            a_s[rows, :] = a
            e_pad[rows, :] = a * g
            g_s[rows, :] = g
            xc_s[rows, :] = xc
            r_s[rows, :] = r
            i_s[rows, :] = i
            return 0

        lax.fori_loop(0, n_ch, recompute, 0)
        _scan_rows(a_s, e_pad, e_pad, seq, wb, reverse=True)

        def grads(c, _):
            r0 = pl.multiple_of(c * ch, ch)
            rows = pl.ds(r0, ch)
            halo = pl.ds(r0, ch + SUBLANES)
            dh = g_s[rows, :] + _shift_rows(e_pad[halo, :], ch + SUBLANES - 1)[0:ch, :]
            h_prev = _shift_rows(hpad[halo, :], 1)[SUBLANES:, :]
            xc, r, i = xc_s[rows, :], r_s[rows, :], i_s[rows, :]
            log_a = -(r * sp8)
            a = jnp.exp(log_a)
            om = _one_minus_exp(2.0 * log_a)
            sq = jnp.sqrt(jnp.maximum(om, 0.0))
            t1 = dh * xc
            d_i = t1 * sq
            d_la = dh * h_prev * a + jnp.where(om > 0.0, -(t1 * i) * (1.0 - om) / sq, 0.0)
            dpr = -(d_la * sp8) * r * (1.0 - r)
            dpi = d_i * i * (1.0 - i)
            dprb, dpib, xb = dpr.astype(BF16), dpi.astype(BF16), xc.astype(BF16)
            dxc = dh * sq * i + _dot_nt(dprb, wr_ref[...]) + _dot_nt(dpib, wi_ref[...])
            dwr_ref[...] += _dot_tn(xb, dprb)
            dwi_ref[...] += _dot_tn(xb, dpib)
            dbr_ref[...] += jnp.sum(dpr, axis=0, keepdims=True)
            dbi_ref[...] += jnp.sum(dpi, axis=0, keepdims=True)
            dlam_ref[...] += jnp.sum(d_la * r, axis=0, keepdims=True) * (RG_C * _sigmoid(-lam_v))
            dcb_ref[...] += jnp.sum(dxc, axis=0, keepdims=True)
            taps = _conv_taps(xpad[halo, :], ch)
            for j in range(CONV_W):
                dcw_ref[pl.ds(j, 1), :] += jnp.sum(dxc * taps[j], axis=0, keepdims=True)
            dxc_pad[rows, :] = dxc
            return 0

        lax.fori_loop(0, n_ch, grads, 0)

        def conv_bwd(c, _):
            r0 = pl.multiple_of(c * ch, ch)
            de = dxc_pad[pl.ds(r0, ch + SUBLANES), :]
            dxa = sum(cw_ref[pl.ds(j, 1), :] * _shift_rows(de, ch + SUBLANES - (CONV_W - 1 - j))[0:ch, :]
                      for j in range(CONV_W))
            dp_ref[0, pl.ds(r0, ch), :] = dxa.astype(BF16)
            return 0

        lax.fori_loop(0, n_ch, conv_bwd, 0)

    seg = lambda k: pl.BlockSpec((None, seq, wb), lambda b, s, k=k: (k, s, b))
    blk = pl.BlockSpec((seq, wb), lambda b, s: (s, b))
    vec = pl.BlockSpec((1, wb), lambda b, s: (0, b))
    taps = pl.BlockSpec((CONV_W, wb), lambda b, s: (0, b))
    wsp = pl.BlockSpec((None, wb, wb), lambda b, s: (b, 0, 0))
    vec_shape = jax.ShapeDtypeStruct((1, d), F32)
    w_shape = jax.ShapeDtypeStruct((n_blk, wb, wb), F32)
    pad = pltpu.VMEM((seq + SUBLANES, wb), F32)
    full = pltpu.VMEM((seq, wb), F32)
    return pl.pallas_call(
        body, name="mixer_a_bwd", grid=(n_blk, n_seq),
        in_specs=[seg(0), seg(1), blk, blk, taps, vec, wsp, vec, wsp, vec, vec],
        out_specs=[pl.BlockSpec((2, seq, wb), lambda b, s: (0, s, b)), wsp, wsp, vec, vec, vec, taps, vec],
        out_shape=[jax.ShapeDtypeStruct((2, t_rows, d), BF16), w_shape, w_shape, vec_shape, vec_shape,
                   vec_shape, jax.ShapeDtypeStruct((CONV_W, d), F32), vec_shape],
        scratch_shapes=[pad, pad, full, pad, full, full, full, full, pad],
        compiler_params=_cp("parallel", "arbitrary"))(
            proj, proj, h, d_ya, conv_w, conv_b, w_r, b_r, w_i, b_i, lam)


def _hg_prepare(q_ref, z_ref, lb, rows):
    z = z_ref[rows, :]
    sig, nsig = _sigmoid_pair(z)
    fg = lb + (1.0 - lb) * sig
    log_f = jnp.log(jnp.maximum(fg, F_MIN))
    key = (1.0 - lb) * nsig
    qs, _ = _silu_and_grad(q_ref[rows, :])
    return qs, key, log_f, sig, fg


HG_UNROLL_TERMS = 32
HG_UNROLL_FWD = 32
HG_UNROLL_BWD = 32
HG_HALF = HG_CHUNK // 2
HG_STACK = HG_CHUNK * HG_HALF
assert HG_HALF == SUBLANES


def _half_of(x, s):
    return x[:HG_HALF, :] if s < HG_HALF else x[HG_HALF:, :]


def _hg_decay(g_ref, r0, g_rows, first_row, s):
    rid = lax.broadcasted_iota(jnp.int32, g_rows.shape, 0) + first_row
    gs = g_ref[pl.ds(r0 + s, 1), :]
    return jnp.where(rid >= s, jnp.exp(g_rows - gs), 0.0)


def _half_start(s):
    return 0 if s < HG_HALF else HG_HALF


def _hg_cross_decays(gc):
    rid = lax.broadcasted_iota(jnp.int32, (HG_CHUNK, HG_DK), 0)
    g_mid = gc[HG_HALF - 1:HG_HALF, :]
    e_hi = jnp.where(rid >= HG_HALF, jnp.exp(gc - g_mid), 0.0)
    e_lo = jnp.where(rid < HG_HALF, jnp.exp(g_mid - gc), 0.0)
    return e_hi, e_lo


def _hg_cross(qc, kc, gc):
    e_hi, e_lo = _hg_cross_decays(gc)
    return (qc * e_hi).astype(BF16), (kc * e_lo).astype(BF16)


def _stack(slabs):
    return jnp.concatenate(slabs, axis=0).astype(BF16)


def _slab_row_sums():
    row = lax.broadcasted_iota(jnp.int32, (HG_CHUNK, HG_STACK), 0)
    col = lax.broadcasted_iota(jnp.int32, (HG_CHUNK, HG_STACK), 1)
    lo = row * HG_HALF
    return jnp.where((col >= lo) & (col < lo + HG_HALF), 1.0, 0.0).astype(BF16)


def _for_chunks(n, unroll, *stages):
    unroll = min(unroll, n)
    assert n % unroll == 0

    def trip(i, _):
        chunks = [i * unroll + u for u in range(unroll)]
        carried = [stages[0](c) for c in chunks]
        for stage in stages[1:]:
            carried = [stage(c, x) for c, x in zip(chunks, carried)]
        return 0

    lax.fori_loop(0, n // unroll, trip, 0)


def _hg_state_terms(v_ref, k_ref, g_ref, states, n_chunks):
    def issue(c):
        rows = pl.ds(pl.multiple_of(c * HG_CHUNK, HG_CHUNK), HG_CHUNK)
        gc = g_ref[rows, :]
        k_end = k_ref[rows, :] * jnp.exp(gc[HG_CHUNK - 1:HG_CHUNK, :] - gc)
        return _dot_tn(v_ref[rows, :].astype(BF16), k_end.astype(BF16))

    def store(c, term):
        states[c] = term

    _for_chunks(n_chunks, HG_UNROLL_TERMS, issue, store)


def _hg_state_chain(states, g_ref, carry_ref, n_chunks, reverse):
    unroll = min(8, n_chunks)
    assert n_chunks % unroll == 0
    carry_ref[...] = jnp.zeros_like(carry_ref)

    def trip(i, _):
        st = carry_ref[...]
        for u in range(unroll):
            k = i * unroll + u
            c = n_chunks - 1 - k if reverse else k
            term = states[c]
            states[c] = st
            st = st * jnp.exp(g_ref[pl.ds(c * HG_CHUNK + HG_CHUNK - 1, 1), :]) + term
        carry_ref[...] = st
        return 0

    lax.fori_loop(0, n_chunks // unroll, trip, 0)


def _hgrn_fwd(proj, lower_bound, hg_gain, seq, carry=None):
    _, t_rows, d = proj.shape
    n_seq, n_head = t_rows // seq, d // HG_DK
    ch = _row_tile(seq, ROW_CHUNK)
    n_chunks = seq // HG_CHUNK

    def body(q_ref, z_ref, v_ref, lb_ref, gain_ref, o_ref, on_ref, qs_s, k_s, g_s, states, st_ref):
        lb = lb_ref[...]
        tri = _group_cumsum_matrix(ch, HG_CHUNK)

        def prepare(c):
            rows = pl.ds(pl.multiple_of(c * ch, ch), ch)
            qs, key, log_f, _, _ = _hg_prepare(q_ref, z_ref, lb, rows)
            qs_s[rows, :] = qs
            k_s[rows, :] = key
            return _group_cumsum_mxu(log_f, tri)

        def store_cumsum(c, g):
            g_s[pl.ds(pl.multiple_of(c * ch, ch), ch), :] = g

        _for_chunks(seq // ch, 4, prepare, store_cumsum)
        _hg_state_terms(v_ref, k_s, g_s, states, n_chunks)
        _hg_state_chain(states, g_s, st_ref, n_chunks, reverse=False)
        ones = jnp.ones((HG_DK, HG_DK), BF16)

        def issue(c):
            r0 = pl.multiple_of(c * HG_CHUNK, HG_CHUNK)
            rows = pl.ds(r0, HG_CHUNK)
            qc, gc = qs_s[rows, :], g_s[rows, :]
            o = _dot_nt((qc * jnp.exp(gc)).astype(BF16), states[c].astype(BF16))
            pairs = [qc[_half_start(s):, :] * _hg_decay(g_s, r0, gc[_half_start(s):, :], _half_start(s), s)
                     * k_s[pl.ds(r0 + s, 1), :] for s in range(HG_CHUNK)]
            score = _dot(_stack(pairs), ones)
            return o, score

        def combine(c, issued):
            o, score = issued
            r0 = pl.multiple_of(c * HG_CHUNK, HG_CHUNK)
            halves = [o[:HG_HALF, :], o[HG_HALF:, :]]
            first = 0
            for s in range(HG_CHUNK):
                vs = v_ref[pl.ds(r0 + s, 1), :]
                if s < HG_HALF:
                    halves[0] += score[first:first + HG_HALF, :] * vs
                    first += HG_HALF
                halves[1] += score[first:first + HG_HALF, :] * vs
                first += HG_HALF
            o_ref[pl.ds(r0, HG_CHUNK), :] = jnp.concatenate(halves, axis=0)

        _for_chunks(n_chunks, HG_UNROLL_FWD, issue, combine)

        def norm(c, _):
            rows = pl.ds(pl.multiple_of(c * ch, ch), ch)
            o = o_ref[rows, :]
            on_ref[rows, :] = (o * _rstd(o) * gain_ref[...]).astype(BF16)
            return 0

        lax.fori_loop(0, seq // ch, norm, 0)

    seg = lambda k: pl.BlockSpec((None, seq, HG_DK), lambda s, h, k=k: (k, s, h))
    blk = pl.BlockSpec((seq, HG_DK), lambda s, h: (s, h))
    full = pltpu.VMEM((seq, HG_DK), F32)
    return _call_carrying(
        body, carry, name="hgrn_fwd", grid=(n_seq, n_head),
        in_specs=[seg(2), seg(3), seg(4), pl.BlockSpec((1, HG_DK), lambda s, h: (0, h)),
                  pl.BlockSpec((1, HG_DK), lambda s, h: (0, 0))],
        out_specs=[blk, blk],
        out_shape=[jax.ShapeDtypeStruct((t_rows, d), F32), jax.ShapeDtypeStruct((t_rows, d), BF16)],
        scratch_shapes=[full, full, full, pltpu.VMEM((n_chunks, HG_DK, HG_DK), F32),
                        pltpu.VMEM((HG_DK, HG_DK), F32)],
        semantics=("parallel", "parallel"), args=(proj, proj, proj, lower_bound, hg_gain))


def _hgrn_bwd(proj, lower_bound, hg_gain, o, d_on, seq, carry=None):
    _, t_rows, d = proj.shape
    n_seq, n_head = t_rows // seq, d // HG_DK
    ch = _row_tile(seq, ROW_CHUNK)
    n_chunks = seq // HG_CHUNK
    cc = HG_CHUNK

    def body(q_ref, z_ref, v_ref, lb_ref, gain_ref, o_ref, don_ref, dp_ref, dlb_ref, dgain_ref,
             qs_s, k_s, g_s, do_s, dqs_s, dk_s, dlf_s, states, dstates, carry_ref):
        hh, ss = pl.program_id(0), pl.program_id(1)
        lb = lb_ref[...]

        @pl.when(ss == 0)
        def _():
            dlb_ref[...] = jnp.zeros_like(dlb_ref)

        @pl.when((ss == 0) & (hh == 0))
        def _():
            dgain_ref[...] = jnp.zeros_like(dgain_ref)

        tri = _group_cumsum_matrix(ch, cc)

        def prepare(c):
            rows = pl.ds(pl.multiple_of(c * ch, ch), ch)
            qs, key, log_f, _, _ = _hg_prepare(q_ref, z_ref, lb, rows)
            qs_s[rows, :] = qs
            k_s[rows, :] = key
            do, dgain = _rms_bwd(don_ref[rows, :], o_ref[rows, :], gain_ref[...])
            do_s[rows, :] = do
            dgain_ref[...] += dgain
            return _group_cumsum_mxu(log_f, tri)

        def store_cumsum(c, g):
            g_s[pl.ds(pl.multiple_of(c * ch, ch), ch), :] = g

        _for_chunks(seq // ch, 4, prepare, store_cumsum)

        _hg_state_terms(v_ref, k_s, g_s, states, n_chunks)
        _hg_state_chain(states, g_s, carry_ref, n_chunks, reverse=False)

        def query_term(c):
            rows = pl.ds(pl.multiple_of(c * cc, cc), cc)
            q_in = qs_s[rows, :] * jnp.exp(g_s[rows, :])
            return _dot_tn(do_s[rows, :].astype(BF16), q_in.astype(BF16))

        def store_query_term(c, term):
            dstates[c] = term

        _for_chunks(n_chunks, HG_UNROLL_TERMS, query_term, store_query_term)
        _hg_state_chain(dstates, g_s, carry_ref, n_chunks, reverse=True)
        ones = jnp.ones((HG_DK, HG_DK), BF16)
        row_sums = _slab_row_sums()

        def chunk_rows(c):
            r0 = pl.multiple_of(c * cc, cc)
            return r0, pl.ds(r0, cc)

        def through_state(c):
            r0, rows = chunk_rows(c)
            qc, kc, gc, vc, doc = qs_s[rows, :], k_s[rows, :], g_s[rows, :], v_ref[rows, :], do_s[rows, :]
            st, dst = states[c], dstates[c]
            g_last = gc[cc - 1:cc, :]
            e_last, e_end = jnp.exp(g_last), jnp.exp(g_last - gc)
            dob, vcb, dstb = doc.astype(BF16), vc.astype(BF16), dst.astype(BF16)
            dqs = _dot(dob, st.astype(BF16))
            dk_state = _dot(vcb, dstb)
            dv = _dot_nt((kc * e_end).astype(BF16), dstb)
            cots = [_half_of(doc, s) * v_ref[pl.ds(r0 + s, 1), :] for s in range(cc)]
            d_score = _dot(_stack(cots), ones)
            x, y = _hg_cross(qc, kc, gc)
            cross = (_dot_nt(dob, vcb), _dot_nt(vcb, dob), _dot_nt(y, x))
            return dqs, dk_state, dv, d_score, e_last * jnp.sum(dst * st, axis=0, keepdims=True), cross

        def pair_terms(c, carried):
            dqs, dk_state, dv, d_score, d_glast, (da_cross, da_cross_t, a_cross_t) = carried
            r0, rows = chunk_rows(c)
            qc, kc, gc = qs_s[rows, :], k_s[rows, :], g_s[rows, :]
            dqs = dqs * jnp.exp(gc)
            dk_state = dk_state * jnp.exp(gc[cc - 1:cc, :] - gc)
            d_glast = d_glast + jnp.sum(kc * dk_state, axis=0, keepdims=True)
            dqs_half = [dqs[:HG_HALF, :], dqs[HG_HALF:, :]]
            pairs, dk_terms = [], []
            for s in range(cc):
                qv = _half_of(qc, s)
                decay = _hg_decay(g_s, r0, _half_of(gc, s), _half_start(s), s)
                ks = k_s[pl.ds(r0 + s, 1), :]
                da_decay = d_score[s * HG_HALF:(s + 1) * HG_HALF, :] * decay
                pairs.append(qv * decay * ks)
                dk_terms.append(da_decay * qv)
                dqs_half[s // HG_HALF] += da_decay * ks
            score = _dot(_stack(pairs), ones)
            dk = dk_state + _dot(row_sums, _stack(dk_terms))
            x, y = _hg_cross(qc, kc, gc)
            cross = (_dot(da_cross.astype(BF16), y), _dot(da_cross_t.astype(BF16), x),
                     _dot(a_cross_t.astype(BF16), do_s[rows, :].astype(BF16)))
            return jnp.concatenate(dqs_half, axis=0), dk, dv, score, d_glast, cross

        def value_terms(c, carried):
            dqs, dk, dv, score, d_glast, (dx_cross, dy_cross, dv_cross) = carried
            _, rows = chunk_rows(c)
            doc, gc = do_s[rows, :], g_s[rows, :]
            dv_terms = [score[s * HG_HALF:(s + 1) * HG_HALF, :] * _half_of(doc, s) for s in range(cc)]
            e_hi, e_lo = _hg_cross_decays(gc)
            return (dqs + dx_cross * e_hi, dk + dy_cross * e_lo,
                    dv + dv_cross + _dot(row_sums, _stack(dv_terms)), d_glast)

        def store(c, x):
            dqs, dk, dv, d_glast = x
            _, rows = chunk_rows(c)
            d_g = qs_s[rows, :] * dqs - k_s[rows, :] * dk
            dlf_s[rows, :] = _seg_cumsum(d_g, cc, reverse=True) + d_glast
            dqs_s[rows, :] = dqs
            dk_s[rows, :] = dk
            dp_ref[2, rows, :] = dv.astype(BF16)

        _for_chunks(n_chunks, HG_UNROLL_BWD, through_state, pair_terms, value_terms, store)

        def finish(c, _):
            rows = pl.ds(pl.multiple_of(c * ch, ch), ch)
            sig, nsig = _sigmoid_pair(z_ref[rows, :])
            fg = lb + (1.0 - lb) * sig
            _, dsilu = _silu_and_grad(q_ref[rows, :])
            dp_ref[0, rows, :] = (dqs_s[rows, :] * dsilu).astype(BF16)
            dfg = jnp.where(fg > F_MIN, dlf_s[rows, :] / fg, 0.0)
            dk = dk_s[rows, :]
            dp_ref[1, rows, :] = ((dfg - dk) * (1.0 - lb) * sig * nsig).astype(BF16)
            dlb_ref[...] += jnp.sum((dfg - dk) * nsig, axis=0, keepdims=True)
            return 0

        lax.fori_loop(0, seq // ch, finish, 0)

    seg = lambda k: pl.BlockSpec((None, seq, HG_DK), lambda h, s, k=k: (k, s, h))
    blk = pl.BlockSpec((seq, HG_DK), lambda h, s: (s, h))
    full = pltpu.VMEM((seq, HG_DK), F32)
    return _call_carrying(
        body, carry, name="hgrn_bwd", grid=(n_head, n_seq),
        in_specs=[seg(2), seg(3), seg(4), pl.BlockSpec((1, HG_DK), lambda h, s: (0, h)),
                  pl.BlockSpec((1, HG_DK), lambda h, s: (0, 0)), blk, blk],
        out_specs=[pl.BlockSpec((3, seq, HG_DK), lambda h, s: (0, s, h)),
                   pl.BlockSpec((1, HG_DK), lambda h, s: (0, h)),
                   pl.BlockSpec((1, HG_DK), lambda h, s: (0, 0))],
        out_shape=[jax.ShapeDtypeStruct((3, t_rows, d), BF16), jax.ShapeDtypeStruct((1, d), F32),
                   jax.ShapeDtypeStruct((1, HG_DK), F32)],
        scratch_shapes=[full, full, full, full, full, full, full,
                        pltpu.VMEM((n_chunks, HG_DK, HG_DK), F32), pltpu.VMEM((n_chunks, HG_DK, HG_DK), F32),
                        pltpu.VMEM((HG_DK, HG_DK), F32)],
        semantics=("arbitrary", "arbitrary"), args=(proj, proj, proj, lower_bound, hg_gain, o, d_on))


def _mesh_place():
    x, y, c = lax.axis_index("x"), lax.axis_index("y"), lax.axis_index("c")
    return x, y, c


def _peer(place, k):
    x, y, c = place
    px = 1 - x if k & 4 else x
    py = 1 - y if k & 2 else y
    pc = 1 - c if k & 1 else c
    return (px, py, pc), 4 * px + 2 * py + pc


class _Exchange:
    def __init__(self, srcs, gather):
        self.n = len(srcs)
        self.gather = gather
        self.out_shape = [jax.ShapeDtypeStruct((N_DEV,) + tuple(s.shape if gather else s.shape[1:]), s.dtype)
                          for s in srcs]
        self.scratch = [pltpu.SemaphoreType.DMA((self.n * (N_DEV - 1),)),
                        pltpu.SemaphoreType.DMA((self.n * (N_DEV - 1),)),
                        pltpu.SemaphoreType.DMA((self.n,))]

    def _copies(self, src_refs, out_refs, sems):
        send_sems, recv_sems, local_sems = sems
        place = _mesh_place()
        me = 4 * place[0] + 2 * place[1] + place[2]
        local, sends, recvs = [], [], []
        for a, (src, out) in enumerate(zip(src_refs, out_refs)):
            outgoing = (lambda idx, src=src: src) if self.gather else (lambda idx, src=src: src.at[idx])
            local.append(pltpu.make_async_copy(outgoing(me), out.at[me], local_sems.at[a]))
            for k in range(1, N_DEV):
                peer, peer_idx = _peer(place, k)
                sem = a * (N_DEV - 1) + k - 1
                sends.append(pltpu.make_async_remote_copy(
                    src_ref=outgoing(peer_idx), dst_ref=out.at[me], send_sem=send_sems.at[sem],
                    recv_sem=recv_sems.at[sem], device_id=peer, device_id_type=MESH_ID))
                recvs.append(pltpu.make_async_remote_copy(
                    src_ref=outgoing(peer_idx), dst_ref=out.at[peer_idx], send_sem=send_sems.at[sem],
                    recv_sem=recv_sems.at[sem], device_id=peer, device_id_type=MESH_ID))
        return local, sends, recvs

    def start(self, src_refs, out_refs, sems):
        local, sends, _ = self._copies(src_refs, out_refs, sems)
        for cp in local + sends:
            cp.start()

    def wait(self, src_refs, out_refs, sems):
        local, sends, recvs = self._copies(src_refs, out_refs, sems)
        for cp in recvs:
            cp.wait_recv()
        for cp in sends:
            cp.wait_send()
        for cp in local:
            cp.wait()


def _call_carrying(body, carry, *, name, grid, in_specs, out_specs, out_shape, scratch_shapes, semantics, args):
    if carry is None:
        outs = pl.pallas_call(body, name=name, grid=grid, in_specs=in_specs, out_specs=out_specs,
                              out_shape=out_shape, scratch_shapes=scratch_shapes,
                              compiler_params=_cp(*semantics))(*args)
        return outs, []
    srcs, gather = carry
    ex = _Exchange(srcs, gather)
    n, n_in, n_out, n_scr = ex.n, len(in_specs), len(out_specs), len(scratch_shapes)

    def wrapped(*refs):
        ins, refs = refs[:n_in], refs[n_in:]
        src_refs, refs = refs[:n], refs[n:]
        outs, refs = refs[:n_out], refs[n_out:]
        dst_refs, refs = refs[:n], refs[n:]
        scratch, sems = refs[:n_scr], refs[n_scr:]
        first, last = None, None
        for axis, size in enumerate(grid):
            i = pl.program_id(axis)
            first = (i == 0) if first is None else first & (i == 0)
            last = (i == size - 1) if last is None else last & (i == size - 1)

        @pl.when(first)
        def _():
            ex.start(src_refs, dst_refs, sems)

        body(*ins, *outs, *scratch)

        @pl.when(last)
        def _():
            ex.wait(src_refs, dst_refs, sems)

    any_space = pl.BlockSpec(memory_space=pl.ANY)
    res = pl.pallas_call(
        wrapped, name=name + "_carrying", grid=grid, in_specs=list(in_specs) + [any_space] * n,
        out_specs=list(out_specs) + [any_space] * n, out_shape=list(out_shape) + ex.out_shape,
        scratch_shapes=list(scratch_shapes) + ex.scratch,
        compiler_params=_cp(*(["arbitrary"] * len(grid))))(*args, *srcs)
    return res[:n_out], res[n_out:]


def _exchange(srcs, name, gather):
    ex = _Exchange(srcs, gather)
    n = ex.n

    def body(*refs):
        src_refs, out_refs, sems = refs[:n], refs[n:2 * n], refs[2 * n:]
        ex.start(src_refs, out_refs, sems)
        ex.wait(src_refs, out_refs, sems)

    any_space = pl.BlockSpec(memory_space=pl.ANY)
    return pl.pallas_call(
        body, name=name, in_specs=[any_space] * n, out_specs=[any_space] * n,
        out_shape=ex.out_shape, scratch_shapes=ex.scratch)(*srcs)


def _reduce_adamw(parts, w, m, v, name):
    rows, cols = w.shape
    n_seg = len(parts)
    seg_rows = rows // n_seg
    tr = _row_tile(seg_rows, 128)
    per_seg = seg_rows // tr
    c1 = np.float32(1.0 - ADAM_B1 ** ADAM_STEP)
    c2 = np.float32(1.0 - ADAM_B2 ** ADAM_STEP)

    def body(*refs):
        p_refs = refs[:n_seg]
        w_ref, m_ref, v_ref, g_ref, d_ref, nm_ref, nv_ref = refs[n_seg:]
        seg = pl.program_id(0)
        for k, p_ref in enumerate(p_refs):
            @pl.when(seg == k)
            def _(p_ref=p_ref):
                g = p_ref[0].astype(F32)
                for dev in range(1, N_DEV):
                    g = g + p_ref[dev].astype(F32)
                g_ref[...] = g

        g = g_ref[...]
        nm = ADAM_B1 * m_ref[...] + (1.0 - ADAM_B1) * g
        nv = ADAM_B2 * v_ref[...] + (1.0 - ADAM_B2) * (g * g)
        nm_ref[...] = nm
        nv_ref[...] = nv
        d_ref[...] = -ADAM_LR * ((nm / c1) / (jnp.sqrt(nv / c2) + ADAM_EPS) + ADAM_WD * w_ref[...])

    def part_spec(k):
        return pl.BlockSpec((N_DEV, tr, cols), lambda s, i, k=k: (0, jnp.where(s == k, i, 0), 0))

    blk = pl.BlockSpec((tr, cols), lambda s, i: (s * per_seg + i, 0))
    shp = jax.ShapeDtypeStruct((rows, cols), F32)
    return pl.pallas_call(
        body, name=name, grid=(n_seg, per_seg),
        in_specs=[part_spec(k) for k in range(n_seg)] + [blk, blk, blk],
        out_specs=[blk, blk, blk, blk], out_shape=[shp, shp, shp, shp],
        compiler_params=_cp("arbitrary", "arbitrary"))(*parts, w, m, v)


def _pack(arrays, lead=0):
    parts = []
    for a in arrays:
        f = a.reshape(a.shape[:lead] + (-1, LANES))
        pad = -f.shape[lead] % PACK_ROWS
        if pad:
            f = jnp.pad(f, [(0, 0)] * lead + [(0, pad), (0, 0)])
        parts.append(f)
    return jnp.concatenate(parts, axis=lead)


def _unpack(buf, shapes, lead=0):
    out, r = [], 0
    for shp in shapes:
        n = int(np.prod(shp)) // LANES
        part = lax.slice_in_dim(buf, r, r + n, axis=lead)
        out.append(part.reshape(buf.shape[:lead] + tuple(shp)))
        r += n + (-n % PACK_ROWS)
    return out


REPLICATED = ("lb_logits", "norm_mix", "conv_b", "b_r", "b_i", "lam", "hg_norm", "norm_mlp", "norm_final")
SMALL_SHARDED = ("conv_w", "w_r", "w_i")
LARGE_SHARDED = ("w_in", "w_out", "w_up", "w_down")
WEIGHTS = ("lb_logits", "norm_mix", "w_in", "conv_w", "conv_b", "w_r", "b_r", "w_i", "b_i", "lam", "hg_norm",
           "w_out", "norm_mlp", "w_up", "w_down", "norm_final")


def _matmul_weight_shards(p):
    depth = p["w_in"].shape[0]
    cast = {k: p[k].astype(BF16) for k in LARGE_SHARDED}
    return ([cast["w_in"][l] for l in range(depth)],
            [[cast[k][l] for k in ("w_out", "w_up", "w_down")] for l in range(depth)])


def _gathered_rest(got):
    w_out, w_up, w_down = got
    d = w_out.shape[2]
    return dict(w_out=w_out.reshape(d, d), w_up=w_up, w_down=w_down)


def _unpack_mixer_weights(small, p):
    depth, d, _ = p["w_in"].shape
    n_blk = d // RG_BLOCK_W
    conv_w, w_r, w_i = _unpack(small, [p["conv_w"].shape, p["w_r"].shape, p["w_i"].shape], lead=1)
    conv_w = conv_w.transpose(1, 2, 0, 3).reshape(depth, CONV_W, d)
    w_r = w_r.transpose(1, 2, 0, 3, 4).reshape(depth, n_blk, RG_BLOCK_W, RG_BLOCK_W).astype(BF16)
    w_i = w_i.transpose(1, 2, 0, 3, 4).reshape(depth, n_blk, RG_BLOCK_W, RG_BLOCK_W).astype(BF16)
    return conv_w, w_r, w_i


def _local_step(x, target, p):
    bl, seq, d = x.shape
    depth = p["w_in"].shape[0]
    t_rows = bl * seq
    row = lambda a, l: a[l:l + 1]
    lbs = _lower_bounds_fwd(p["lb_logits"])
    shard_in, shard_rest = _matmul_weight_shards(p)
    w_in = _exchange([shard_in[0]], "gather_w_in", gather=True)[0]
    cur = x.reshape(t_rows, d)
    saved, layers = [], []
    for l in range(depth):
        if l == 0:
            (proj, gates, h), small = _inproj_fwd(cur, row(p["norm_mix"], l), w_in,
                                                  carry=([_pack([p["conv_w"], p["w_r"], p["w_i"]])], True))
            conv_w, w_r, w_i = _unpack_mixer_weights(small[0], p)
        else:
            (proj, gates, h), _ = _inproj_fwd(cur, row(p["norm_mix"], l), w_in)
        w = dict(w_in=w_in, conv_w=conv_w[l], w_r=w_r[l], w_i=w_i[l])
        hs, y_a = _mixer_a_fwd(proj, w["conv_w"], row(p["conv_b"], l), w["w_r"], row(p["b_r"], l), w["w_i"],
                               row(p["b_i"], l), row(p["lam"], l), seq)
        (o, o_n), got = _hgrn_fwd(proj, row(lbs, l), row(p["hg_norm"], l), seq,
                                  carry=(shard_rest[l] + ([shard_in[l + 1]] if l + 1 < depth else []), True))
        w.update(_gathered_rest(got[:3]))
        w_in = got[3] if l + 1 < depth else None
        layers.append(w)
        x_mid, y = _merge_out_fwd(gates, y_a, o_n, cur, w["w_out"])
        x_out, u, h2 = _mlp_fwd(x_mid, row(p["norm_mlp"], l), w["w_up"], w["w_down"])
        saved.append(dict(x_in=cur, proj=proj, gates=gates, h=h, hs=hs, y_a=y_a, o=o, o_n=o_n, x_mid=x_mid, y=y, u=u, h2=h2))
        cur = x_out
    loss8, dx, dxb, g_norm_final = _loss_head(cur, p["norm_final"].reshape(1, d), target.reshape(t_rows, d))
    small = ("norm_mix", "conv_w", "conv_b", "w_r", "b_r", "w_i", "b_i", "lam", "hg_norm", "norm_mlp")
    g = {k: [None] * depth for k in small}
    d_lbs, received = [None] * depth, [None] * depth
    g_w_in = None
    for l in reversed(range(depth)):
        s, w = saved[l], layers[l]
        dx_mid, dx_mid_b, du, act, g["norm_mlp"][l] = _mlp_bwd(dx, dxb, s["u"], s["x_mid"], row(p["norm_mlp"], l),
                                                               w["w_up"], w["w_down"])
        g_w_down = _wgrad(act, dxb[None], "wgrad_down")
        g_w_up = _wgrad(s["h2"][None], du, "wgrad_up")
        d_ya, d_on, dp_c = _outproj_bwd(dx_mid_b, w["w_out"], s["gates"], s["y_a"], s["o_n"])
        g_w_out = _wgrad(s["y"][None], dx_mid_b[None], "wgrad_out").reshape(N_DEV, d // N_DEV, d)
        (dp_b, d_lbs[l], g["hg_norm"][l]), got = _hgrn_bwd(
            s["proj"], row(lbs, l), row(p["hg_norm"], l), s["o"], d_on, seq,
            carry=([g_w_out, g_w_up, g_w_down] + ([g_w_in] if g_w_in is not None else []), False))
        received[l] = [None] + list(got[:3])
        if g_w_in is not None:
            received[l + 1][0] = got[3]
        (dp_a, g["w_r"][l], g["w_i"][l], g["b_r"][l], g["b_i"][l], g["lam"][l], g["conv_w"][l],
         g["conv_b"][l]) = _mixer_a_bwd(s["proj"], s["hs"], d_ya, w["conv_w"], row(p["conv_b"], l), w["w_r"],
                                        row(p["b_r"], l), w["w_i"], row(p["b_i"], l), row(p["lam"], l), seq)
        hb = s["h"][None]
        g_w_in = jnp.concatenate([_wgrad(hb, dp_a, "wgrad_in_pair"), _wgrad(hb, dp_b, "wgrad_in_triple"),
                                  _wgrad(hb, dp_c, "wgrad_in_triple")], axis=0)
        carry = None
        if l == 0:
            carry = ([g_w_in, _mixer_grads_by_owner(g, d)], False)
        (dx, dxb, g["norm_mix"][l]), got = _inproj_bwd(dx_mid, dp_a, dp_b, dp_c, w["w_in"], s["x_in"],
                                                       row(p["norm_mix"], l), carry=carry)
    received[0][0], received_mixer = got
    grads = {k: jnp.stack(v) for k, v in g.items()}
    for k in ("norm_mix", "conv_b", "b_r", "b_i", "lam", "hg_norm", "norm_mlp"):
        grads[k] = grads[k][:, 0]
    grads["lb_logits"] = _lower_bounds_bwd(p["lb_logits"], jnp.concatenate(d_lbs, axis=0))
    grads["norm_final"] = g_norm_final[0]
    return loss8[0, 0], dx.reshape(bl, seq, d), grads, received, received_mixer


def _mixer_grads_by_owner(g, d):
    d8, n_blk, rb = d // N_DEV, d // RG_BLOCK_W, RG_BLOCK_W // N_DEV
    depth = len(g["conv_w"])
    conv_w, w_r, w_i = (jnp.stack(g[k]) for k in SMALL_SHARDED)
    return _pack([conv_w.reshape(depth, CONV_W, N_DEV, d8).transpose(2, 0, 1, 3),
                  w_r.reshape(depth, n_blk, N_DEV, rb, RG_BLOCK_W).transpose(2, 0, 1, 3, 4),
                  w_i.reshape(depth, n_blk, N_DEV, rb, RG_BLOCK_W).transpose(2, 0, 1, 3, 4)], lead=1)


def _update(p, mom1, mom2, grads, received, received_mixer):
    depth = p["w_in"].shape[0]
    out = {}

    for i, k in enumerate(LARGE_SHARDED):
        shp = p[k].shape
        flat = lambda a: a.reshape(shp[0] * shp[1], shp[2])
        parts = [received[l][i] for l in range(depth)]
        res = _reduce_adamw(parts, flat(p[k]), flat(mom1[k]), flat(mom2[k]), "adamw_" + k)
        out[k] = [r.reshape(shp) for r in res]

    res = _reduce_adamw([received_mixer], *[_pack([src[k] for k in SMALL_SHARDED]) for src in (p, mom1, mom2)],
                        "adamw_mixer")
    shapes = [p[k].shape for k in SMALL_SHARDED]
    for i, vals in enumerate(zip(*[_unpack(r, shapes) for r in res])):
        out[SMALL_SHARDED[i]] = list(vals)

    parts = _exchange([_pack([grads[k] for k in REPLICATED])], "gather_grad_replicated", gather=True)
    res = _reduce_adamw(parts, *[_pack([src[k] for k in REPLICATED]) for src in (p, mom1, mom2)],
                        "adamw_replicated")
    shapes = [p[k].shape for k in REPLICATED]
    for i, vals in enumerate(zip(*[_unpack(r, shapes) for r in res])):
        out[REPLICATED[i]] = list(vals)

    return tuple(out[k][i] for i in range(4) for k in WEIGHTS)


def kernel(x, lb_logits, norm_mix, w_in, conv_w, conv_b, w_r, b_r, w_i, b_i, lam, hg_norm, w_out, norm_mlp, w_up, w_down, norm_final, loss_target, m_lb_logits, m_norm_mix, m_w_in, m_conv_w, m_conv_b, m_w_r, m_b_r, m_w_i, m_b_i, m_lam, m_hg_norm, m_w_out, m_norm_mlp, m_w_up, m_w_down, m_norm_final, v_lb_logits, v_norm_mix, v_w_in, v_conv_w, v_conv_b, v_w_r, v_b_r, v_w_i, v_b_i, v_lam, v_hg_norm, v_w_out, v_norm_mlp, v_w_up, v_w_down, v_norm_final):
    p = dict(lb_logits=lb_logits, norm_mix=norm_mix, w_in=w_in, conv_w=conv_w, conv_b=conv_b, w_r=w_r, b_r=b_r,
             w_i=w_i, b_i=b_i, lam=lam, hg_norm=hg_norm, w_out=w_out, norm_mlp=norm_mlp, w_up=w_up,
             w_down=w_down, norm_final=norm_final)
    mom1 = dict(lb_logits=m_lb_logits, norm_mix=m_norm_mix, w_in=m_w_in, conv_w=m_conv_w, conv_b=m_conv_b,
                w_r=m_w_r, b_r=m_b_r, w_i=m_w_i, b_i=m_b_i, lam=m_lam, hg_norm=m_hg_norm, w_out=m_w_out,
                norm_mlp=m_norm_mlp, w_up=m_w_up, w_down=m_w_down, norm_final=m_norm_final)
    mom2 = dict(lb_logits=v_lb_logits, norm_mix=v_norm_mix, w_in=v_w_in, conv_w=v_conv_w, conv_b=v_conv_b,
                w_r=v_w_r, b_r=v_b_r, w_i=v_w_i, b_i=v_b_i, lam=v_lam, hg_norm=v_hg_norm, w_out=v_w_out,
                norm_mlp=v_norm_mlp, w_up=v_w_up, w_down=v_w_down, norm_final=v_norm_final)
    loss, grad_x, grads, received, received_mixer = _local_step(x, loss_target, p)
    loss = lax.psum(loss, ("x", "y", "c"))
    return (loss, grad_x) + _update(p, mom1, mom2, grads, received, received_mixer)
```

```python
import numpy as np

import jax
import jax.numpy as jnp
from jax import lax
from jax.experimental import pallas as pl
from jax.experimental.pallas import tpu as pltpu

F32 = jnp.float32
BF16 = jnp.bfloat16
MESH_ID = pl.DeviceIdType.MESH

N_DEV = 8
N_MIXER_SEGMENTS = 5
NORM_EPS = 1e-6
RG_C = 8.0
RG_BLOCK_W = 256
CONV_W = 4
HG_DK = 128
F_MIN = 1e-30
HG_CHUNK = 16
SUBLANES = 8
LANES = 128
PACK_ROWS = 16
SCAN_GROUP = 16
ROW_CHUNK = 256
ROW_TILE_WEIGHT_STREAM = 1024
WGRAD_TOKEN_TILE = 2048
VMEM_LIMIT_V7X = 56 * 1024 * 1024

ADAM_LR = 0.001
ADAM_B1 = 0.9
ADAM_B2 = 0.999
ADAM_EPS = 1e-08
ADAM_WD = 0.01
ADAM_STEP = 10

GELU_C = 0.7978845608028654
GELU_K = 0.044715


def _cp(*sem):
    return pltpu.CompilerParams(dimension_semantics=sem, vmem_limit_bytes=VMEM_LIMIT_V7X)


def _row_tile(n, cap):
    if n <= cap:
        return n
    t = cap - cap % 16
    while n % t:
        t -= 16
    return t


def _dot(a, b):
    return jnp.dot(a, b, preferred_element_type=F32)


def _dot_nt(a, b):
    return lax.dot_general(a, b, (((1,), (1,)), ((), ())), preferred_element_type=F32)


def _dot_tn(a, b):
    return lax.dot_general(a, b, (((0,), (0,)), ((), ())), preferred_element_type=F32)


def _sigmoid(x):
    return jax.nn.sigmoid(x)


def _sigmoid_pair(x):
    e = jnp.exp(-jnp.abs(x))
    r = 1.0 / (1.0 + e)
    er = e * r
    pos = x >= 0.0
    return jnp.where(pos, r, er), jnp.where(pos, er, r)


def _log1p_pos(y):
    return jnp.where(y < 0.01, y * (1.0 - y * (0.5 - y * (1.0 / 3.0))), jnp.log(1.0 + y))


def _softplus(x):
    return jnp.maximum(x, 0.0) + _log1p_pos(jnp.exp(-jnp.abs(x)))


def _one_minus_exp(x):
    series = -x * (1.0 + x * 0.5 * (1.0 + x * (1.0 / 3.0) * (1.0 + x * 0.25 * (1.0 + x * 0.2))))
    return jnp.where(x > -0.1, series, 1.0 - jnp.exp(x))


def _gelu_and_grad(x):
    x2 = x * x
    t = jnp.tanh(GELU_C * x * (1.0 + GELU_K * x2))
    g = 0.5 * x * (1.0 + t)
    dg = 0.5 * (1.0 + t) + 0.5 * x * (1.0 - t * t) * GELU_C * (1.0 + 3.0 * GELU_K * x2)
    return g, dg


def _silu_and_grad(x):
    s = _sigmoid(x)
    return x * s, s * (1.0 + x * (1.0 - s))


def _rstd(x):
    return lax.rsqrt(jnp.mean(x * x, axis=-1, keepdims=True) + NORM_EPS)


def _rms_bwd(dh, x, g):
    rstd = _rstd(x)
    xh = x * rstd
    dxh = dh * g
    dx = rstd * (dxh - xh * jnp.mean(dxh * xh, axis=-1, keepdims=True))
    return dx, jnp.sum(dh * xh, axis=0, keepdims=True)


def _shift_rows(x, k):
    n = x.shape[0]
    k = k % n
    return x if k == 0 else pltpu.roll(x, k, axis=0)


def _seg_cumsum(x, seg, reverse=False):
    n = x.shape[0]
    rid = lax.broadcasted_iota(jnp.int32, x.shape, 0) & (seg - 1)
    d = 1
    while d < seg:
        if reverse:
            x = jnp.where(rid < seg - d, x + _shift_rows(x, n - d), x)
        else:
            x = jnp.where(rid >= d, x + _shift_rows(x, d), x)
        d *= 2
    return x


def _group_cumsum_matrix(n, seg):
    row = lax.broadcasted_iota(jnp.int32, (n, n), 0)
    col = lax.broadcasted_iota(jnp.int32, (n, n), 1)
    same_group = (row & ~(seg - 1)) == (col & ~(seg - 1))
    return jnp.where(same_group & (col <= row), 1.0, 0.0).astype(BF16)


def _group_cumsum_mxu(x, tri):
    hi = x.astype(BF16)
    lo = (x - hi.astype(F32)).astype(BF16)
    return _dot(tri, hi) + _dot(tri, lo)


def _scan_rows(a_ref, b_ref, out_ref, n_rows, width, reverse):
    gr = min(SCAN_GROUP, n_rows)
    rid = lax.broadcasted_iota(jnp.int32, (gr, width), 0)
    n_groups = n_rows // gr
    per_trip = min(4, n_groups)
    assert n_groups % per_trip == 0

    def local_scan(g):
        r0 = pl.multiple_of(g * gr, gr)
        a = a_ref[pl.ds(r0, gr), :]
        b = b_ref[pl.ds(r0, gr), :]
        d = 1
        while d < gr:
            if reverse:
                keep = rid < gr - d
                a_sh, b_sh = _shift_rows(a, gr - d), _shift_rows(b, gr - d)
            else:
                keep = rid >= d
                a_sh, b_sh = _shift_rows(a, d), _shift_rows(b, d)
            b = jnp.where(keep, a * b_sh + b, b)
            a = jnp.where(keep, a * a_sh, a)
            d *= 2
        return r0, a, b

    def trip(i, carry):
        first = i * per_trip
        groups = [n_groups - 1 - (first + u) if reverse else first + u for u in range(per_trip)]
        for r0, a, b in [local_scan(g) for g in groups]:
            out = a * carry + b
            out_ref[pl.ds(r0, gr), :] = out
            edge = out[0:1, :] if reverse else out[gr - 1:gr, :]
            carry = jnp.broadcast_to(edge, (gr, width))
        return carry

    lax.fori_loop(0, n_groups // per_trip, trip, jnp.zeros((gr, width), F32))


def _lb_softmax_rows(x_ref, depth):
    rows = [x_ref[pl.ds(l, 1), :] for l in range(depth)]
    top = rows[0]
    for r in rows[1:]:
        top = jnp.maximum(top, r)
    e = [jnp.exp(r - top) for r in rows]
    tot = e[0]
    for r in e[1:]:
        tot = tot + r
    return [r / tot for r in e]


def _lower_bounds_fwd(lb_logits):
    depth, d = lb_logits.shape

    def body(x_ref, o_ref):
        sm = _lb_softmax_rows(x_ref, depth)
        cum = jnp.zeros((1, d), F32)
        for l in range(depth):
            cum = cum + sm[l]
            o_ref[pl.ds(l, 1), :] = jnp.clip(cum - sm[0], 0.0, 1.0)

    return pl.pallas_call(body, name="lower_bounds_fwd",
                          out_shape=jax.ShapeDtypeStruct((depth, d), F32))(lb_logits)


def _lower_bounds_bwd(lb_logits, d_lbs):
    depth, d = lb_logits.shape

    def body(x_ref, g_ref, o_ref):
        sm = _lb_softmax_rows(x_ref, depth)
        cum = jnp.zeros((1, d), F32)
        d_cum = []
        for l in range(depth):
            cum = cum + sm[l]
            v = cum - sm[0]
            d_cum.append(jnp.where((v > 0.0) & (v < 1.0), g_ref[pl.ds(l, 1), :], 0.0))
        d_sm = []
        tail = jnp.zeros((1, d), F32)
        for l in reversed(range(depth)):
            tail = tail + d_cum[l]
            d_sm.append(tail)
        d_sm = d_sm[::-1]
        d_sm[0] = d_sm[0] - tail
        inner = jnp.zeros((1, d), F32)
        for l in range(depth):
            inner = inner + sm[l] * d_sm[l]
        for l in range(depth):
            o_ref[pl.ds(l, 1), :] = sm[l] * (d_sm[l] - inner)

    return pl.pallas_call(body, name="lower_bounds_bwd",
                          out_shape=jax.ShapeDtypeStruct((depth, d), F32))(lb_logits, d_lbs)


def _inproj_fwd(x, gain, w_seg, carry=None):
    t_rows, d = x.shape
    tm = _row_tile(t_rows, ROW_TILE_WEIGHT_STREAM)
    n_gate = N_DEV - N_MIXER_SEGMENTS

    def body(x_ref, g_ref, w_ref, proj_ref, gates_ref, h_ref):
        j = pl.program_id(1)

        @pl.when(j == 0)
        def _():
            xv = x_ref[...]
            h_ref[...] = (xv * _rstd(xv) * g_ref[...]).astype(BF16)

        @pl.when(j < N_MIXER_SEGMENTS)
        def _():
            proj_ref[...] = _dot(h_ref[...], w_ref[...])

        @pl.when(j >= N_MIXER_SEGMENTS)
        def _():
            gates_ref[...] = _dot(h_ref[...], w_ref[...]).astype(BF16)

    return _call_carrying(
        body, carry, name="inproj_fwd", grid=(t_rows // tm, N_DEV),
        in_specs=[pl.BlockSpec((tm, d), lambda i, j: (i, 0)),
                  pl.BlockSpec((1, d), lambda i, j: (0, 0)),
                  pl.BlockSpec((None, d, d), lambda i, j: (j, 0, 0))],
        out_specs=[pl.BlockSpec((None, tm, d), lambda i, j: (jnp.minimum(j, N_MIXER_SEGMENTS - 1), i, 0)),
                   pl.BlockSpec((None, tm, d), lambda i, j: (jnp.maximum(j - N_MIXER_SEGMENTS, 0), i, 0)),
                   pl.BlockSpec((tm, d), lambda i, j: (i, 0))],
        out_shape=[jax.ShapeDtypeStruct((N_MIXER_SEGMENTS, t_rows, d), F32),
                   jax.ShapeDtypeStruct((n_gate, t_rows, d), BF16),
                   jax.ShapeDtypeStruct((t_rows, d), BF16)],
        scratch_shapes=[], semantics=("parallel", "arbitrary"), args=(x, gain, w_seg))


def _merge_out_fwd(gates, y_a, o_n, x, w_out):
    t_rows, d = x.shape
    tm = _row_tile(t_rows, 256)

    def body(g_ref, ma_ref, mb_ref, ya_ref, on_ref, x_ref, w_ref, xmid_ref, y_ref):
        g = g_ref[...].astype(F32)
        ya, on = ya_ref[...].astype(F32), on_ref[...].astype(F32)
        y = (_sigmoid(ma_ref[...].astype(F32)) * ya
             + _sigmoid(mb_ref[...].astype(F32)) * (on * (g * _sigmoid(g))))
        yb = y.astype(BF16)
        y_ref[...] = yb
        xmid_ref[...] = x_ref[...] + _dot(yb, w_ref[...])

    seg = lambda k: pl.BlockSpec((None, tm, d), lambda i, k=k: (k, i, 0))
    row = pl.BlockSpec((tm, d), lambda i: (i, 0))
    return pl.pallas_call(
        body, name="merge_out_fwd", grid=(t_rows // tm,),
        in_specs=[seg(0), seg(1), seg(2), row, row, row, pl.BlockSpec((d, d), lambda i: (0, 0))],
        out_specs=[row, row],
        out_shape=[jax.ShapeDtypeStruct((t_rows, d), F32), jax.ShapeDtypeStruct((t_rows, d), BF16)],
        compiler_params=_cp("parallel"))(gates, gates, gates, y_a, o_n, x, w_out)


def _mlp_fwd(x_mid, gain, w_up, w_down):
    t_rows, d = x_mid.shape
    f8 = w_up.shape[2]
    tm = _row_tile(t_rows, ROW_TILE_WEIGHT_STREAM)

    def body(x_ref, g_ref, wu_ref, wd_ref, out_ref, u_ref, h_ref):
        @pl.when(pl.program_id(1) == 0)
        def _():
            xv = x_ref[...]
            h_ref[...] = (xv * _rstd(xv) * g_ref[...]).astype(BF16)
            out_ref[...] = xv

        u = _dot(h_ref[...], wu_ref[...])
        u_ref[...] = u.astype(BF16)
        r = jnp.maximum(u, 0.0)
        out_ref[...] += _dot((r * r).astype(BF16), wd_ref[...])

    row = pl.BlockSpec((tm, d), lambda i, j: (i, 0))
    return pl.pallas_call(
        body, name="mlp_fwd", grid=(t_rows // tm, N_DEV),
        in_specs=[row, pl.BlockSpec((1, d), lambda i, j: (0, 0)),
                  pl.BlockSpec((None, d, f8), lambda i, j: (j, 0, 0)),
                  pl.BlockSpec((None, f8, d), lambda i, j: (j, 0, 0))],
        out_specs=[row, pl.BlockSpec((None, tm, f8), lambda i, j: (j, i, 0)), row],
        out_shape=[jax.ShapeDtypeStruct((t_rows, d), F32),
                   jax.ShapeDtypeStruct((N_DEV, t_rows, f8), BF16),
                   jax.ShapeDtypeStruct((t_rows, d), BF16)],
        compiler_params=_cp("parallel", "arbitrary"))(x_mid, gain, w_up, w_down)


def _loss_head(x, gain, target):
    t_rows, d = x.shape
    tm = _row_tile(t_rows, 512)

    def body(x_ref, g_ref, t_ref, loss_ref, dx_ref, dxb_ref, dg_ref):
        @pl.when(pl.program_id(0) == 0)
        def _():
            loss_ref[...] = jnp.zeros_like(loss_ref)
            dg_ref[...] = jnp.zeros_like(dg_ref)

        xv = x_ref[...]
        g = g_ref[...]
        err = xv * _rstd(xv) * g - t_ref[...]
        loss_ref[...] += (0.5 / d) * jnp.sum(err * err)
        dx, dg = _rms_bwd(err * (1.0 / d), xv, g)
        dx_ref[...] = dx
        dxb_ref[...] = dx.astype(BF16)
        dg_ref[...] += dg

    row = pl.BlockSpec((tm, d), lambda i: (i, 0))
    vec = pl.BlockSpec((1, d), lambda i: (0, 0))
    return pl.pallas_call(
        body, name="loss_head", grid=(t_rows // tm,),
        in_specs=[row, vec, row],
        out_specs=[pl.BlockSpec((SUBLANES, LANES), lambda i: (0, 0)), row, row, vec],
        out_shape=[jax.ShapeDtypeStruct((SUBLANES, LANES), F32),
                   jax.ShapeDtypeStruct((t_rows, d), F32),
                   jax.ShapeDtypeStruct((t_rows, d), BF16),
                   jax.ShapeDtypeStruct((1, d), F32)],
        compiler_params=_cp("arbitrary"))(x, gain, target)


def _mlp_bwd(d_out, d_out_b, u, x_mid, gain, w_up, w_down):
    t_rows, d = x_mid.shape
    f8 = w_up.shape[2]
    tm = _row_tile(t_rows, ROW_TILE_WEIGHT_STREAM)
    sub = _row_tile(tm, ROW_CHUNK)

    def body(do_ref, dob_ref, u_ref, x_ref, g_ref, wu_ref, wd_ref, dx_ref, dxb_ref, du_ref, act_ref, dg_ref):
        j = pl.program_id(1)

        @pl.when((pl.program_id(0) == 0) & (j == 0))
        def _():
            dg_ref[...] = jnp.zeros_like(dg_ref)

        @pl.when(j == 0)
        def _():
            dx_ref[...] = jnp.zeros_like(dx_ref)

        r = jnp.maximum(u_ref[...].astype(F32), 0.0)
        act_ref[...] = (r * r).astype(BF16)
        du = (_dot_nt(dob_ref[...], wd_ref[...]) * (2.0 * r)).astype(BF16)
        du_ref[...] = du
        dx_ref[...] += _dot_nt(du, wu_ref[...])

        @pl.when(j == N_DEV - 1)
        def _():
            def finish(c, _):
                rows = pl.ds(pl.multiple_of(c * sub, sub), sub)
                dx, dg = _rms_bwd(dx_ref[rows, :], x_ref[rows, :], g_ref[...])
                dx = dx + do_ref[rows, :]
                dx_ref[rows, :] = dx
                dxb_ref[rows, :] = dx.astype(BF16)
                dg_ref[...] += dg
                return 0

            lax.fori_loop(0, tm // sub, finish, 0)

    row = pl.BlockSpec((tm, d), lambda i, j: (i, 0))
    vec = pl.BlockSpec((1, d), lambda i, j: (0, 0))
    hid = pl.BlockSpec((None, tm, f8), lambda i, j: (j, i, 0))
    return pl.pallas_call(
        body, name="mlp_bwd", grid=(t_rows // tm, N_DEV),
        in_specs=[row, row, hid, row, vec,
                  pl.BlockSpec((None, d, f8), lambda i, j: (j, 0, 0)),
                  pl.BlockSpec((None, f8, d), lambda i, j: (j, 0, 0))],
        out_specs=[row, row, hid, hid, vec],
        out_shape=[jax.ShapeDtypeStruct((t_rows, d), F32),
                   jax.ShapeDtypeStruct((t_rows, d), BF16),
                   jax.ShapeDtypeStruct((N_DEV, t_rows, f8), BF16),
                   jax.ShapeDtypeStruct((N_DEV, t_rows, f8), BF16),
                   jax.ShapeDtypeStruct((1, d), F32)],
        compiler_params=_cp("arbitrary", "arbitrary"))(d_out, d_out_b, u, x_mid, gain, w_up, w_down)


def _outproj_bwd(dx_mid_b, w_out, gates, y_a, o_n):
    t_rows, d = y_a.shape
    tm = _row_tile(t_rows, 256)

    def body(dx_ref, w_ref, g_ref, ma_ref, mb_ref, ya_ref, on_ref, dya_ref, don_ref, dp_ref):
        dy = _dot_nt(dx_ref[...], w_ref[...])
        sa = _sigmoid(ma_ref[...].astype(F32))
        sb = _sigmoid(mb_ref[...].astype(F32))
        sg, dsg = _silu_and_grad(g_ref[...].astype(F32))
        ya = ya_ref[...].astype(F32)
        on = on_ref[...].astype(F32)
        dya_ref[...] = dy * sa
        t = dy * sb
        don_ref[...] = t * sg
        dp_ref[0] = (t * on * dsg).astype(BF16)
        dp_ref[1] = (dy * ya * sa * (1.0 - sa)).astype(BF16)
        dp_ref[2] = (dy * on * sg * sb * (1.0 - sb)).astype(BF16)

    seg = lambda k: pl.BlockSpec((None, tm, d), lambda i, k=k: (k, i, 0))
    row = pl.BlockSpec((tm, d), lambda i: (i, 0))
    return pl.pallas_call(
        body, name="outproj_bwd", grid=(t_rows // tm,),
        in_specs=[row, pl.BlockSpec((d, d), lambda i: (0, 0)), seg(0), seg(1), seg(2), row, row],
        out_specs=[row, row, pl.BlockSpec((3, tm, d), lambda i: (0, i, 0))],
        out_shape=[jax.ShapeDtypeStruct((t_rows, d), F32),
                   jax.ShapeDtypeStruct((t_rows, d), F32),
                   jax.ShapeDtypeStruct((3, t_rows, d), BF16)],
        compiler_params=_cp("parallel"))(dx_mid_b, w_out, gates, gates, gates, y_a, o_n)


def _inproj_bwd(dx_mid, dp_a, dp_b, dp_c, w_seg, x_in, gain, carry=None):
    t_rows, d = x_in.shape
    tm = _row_tile(t_rows, ROW_TILE_WEIGHT_STREAM)
    sub = _row_tile(tm, ROW_CHUNK)
    n_a, n_b = dp_a.shape[0], dp_b.shape[0]

    def body(dxm_ref, a_ref, b_ref, c_ref, w_ref, x_ref, g_ref, dx_ref, dxb_ref, dg_ref):
        j = pl.program_id(1)

        @pl.when((pl.program_id(0) == 0) & (j == 0))
        def _():
            dg_ref[...] = jnp.zeros_like(dg_ref)

        @pl.when(j == 0)
        def _():
            dx_ref[...] = jnp.zeros_like(dx_ref)

        @pl.when(j < n_a)
        def _():
            dx_ref[...] += _dot_nt(a_ref[...], w_ref[...])

        @pl.when((j >= n_a) & (j < n_a + n_b))
        def _():
            dx_ref[...] += _dot_nt(b_ref[...], w_ref[...])

        @pl.when(j >= n_a + n_b)
        def _():
            dx_ref[...] += _dot_nt(c_ref[...], w_ref[...])

        @pl.when(j == N_DEV - 1)
        def _():
            def finish(c, _):
                rows = pl.ds(pl.multiple_of(c * sub, sub), sub)
                dx, dg = _rms_bwd(dx_ref[rows, :], x_ref[rows, :], g_ref[...])
                dx = dx + dxm_ref[rows, :]
                dx_ref[rows, :] = dx
                dxb_ref[rows, :] = dx.astype(BF16)
                dg_ref[...] += dg
                return 0

            lax.fori_loop(0, tm // sub, finish, 0)

    def part(first, n):
        return pl.BlockSpec((None, tm, d), lambda i, j: (jnp.clip(j - first, 0, n - 1), i, 0))

    row = pl.BlockSpec((tm, d), lambda i, j: (i, 0))
    vec = pl.BlockSpec((1, d), lambda i, j: (0, 0))
    return _call_carrying(
        body, carry, name="inproj_bwd", grid=(t_rows // tm, N_DEV),
        in_specs=[row, part(0, n_a), part(n_a, n_b), part(n_a + n_b, dp_c.shape[0]),
                  pl.BlockSpec((None, d, d), lambda i, j: (j, 0, 0)), row, vec],
        out_specs=[row, row, vec],
        out_shape=[jax.ShapeDtypeStruct((t_rows, d), F32),
                   jax.ShapeDtypeStruct((t_rows, d), BF16),
                   jax.ShapeDtypeStruct((1, d), F32)],
        scratch_shapes=[], semantics=("arbitrary", "arbitrary"),
        args=(dx_mid, dp_a, dp_b, dp_c, w_seg, x_in, gain))


def _wgrad(a3, b3, name):
    n_a, t_rows, k_a = a3.shape
    n_b, _, n_cols = b3.shape
    n = max(n_a, n_b)
    bk = _row_tile(k_a, 1024)
    bn = n_cols if n_cols <= 1024 else 1024
    tt = _row_tile(t_rows, WGRAD_TOKEN_TILE)
    n_t = t_rows // tt

    def body(a_ref, b_ref, o_ref, acc_ref):
        t, j = pl.program_id(2), pl.program_id(3)
        part = _dot_tn(a_ref[...], b_ref[...])

        @pl.when(t == 0)
        def _():
            acc_ref[j] = part

        @pl.when(t > 0)
        def _():
            acc_ref[j] += part

        @pl.when(t == n_t - 1)
        def _():
            o_ref[...] = acc_ref[j].astype(BF16)

    def out_map(p, q, t, j):
        return (jnp.where(t == n_t - 1, j, 0), p, q)

    return pl.pallas_call(
        body, name=name, grid=(k_a // bk, n_cols // bn, n_t, n),
        in_specs=[pl.BlockSpec((None, tt, bk), lambda p, q, t, j: (j if n_a > 1 else 0, t, p)),
                  pl.BlockSpec((None, tt, bn), lambda p, q, t, j: (j if n_b > 1 else 0, t, q))],
        out_specs=pl.BlockSpec((None, bk, bn), out_map),
        out_shape=jax.ShapeDtypeStruct((n, k_a, n_cols), BF16),
        scratch_shapes=[pltpu.VMEM((n, bk, bn), F32)],
        compiler_params=_cp("parallel", "parallel", "arbitrary", "arbitrary"))(a3, b3)


def _conv_taps(xe, n):
    return [_shift_rows(xe, CONV_W - 1 - j)[SUBLANES:SUBLANES + n, :] for j in range(CONV_W)]


def _rg_gates(xc, w_r, b_r, w_i, b_i, sp8):
    xb = xc.astype(BF16)
    r = _sigmoid(_dot(xb, w_r) + b_r)
    i = _sigmoid(_dot(xb, w_i) + b_i)
    return r, i


def _mixer_a_fwd(proj, conv_w, conv_b, w_r, b_r, w_i, b_i, lam, seq):
    _, t_rows, d = proj.shape
    n_seq, n_blk = t_rows // seq, d // RG_BLOCK_W
    wb = RG_BLOCK_W
    ch = _row_tile(seq, ROW_CHUNK)

    def body(xa_ref, ga_ref, cw_ref, cb_ref, wr_ref, br_ref, wi_ref, bi_ref, lam_ref, h_ref, ya_ref,
             xpad, a_s, u_s):
        xpad[0:SUBLANES, :] = jnp.zeros((SUBLANES, wb), F32)
        xpad[SUBLANES:, :] = xa_ref[...]
        sp8 = RG_C * _softplus(-lam_ref[...])

        def gates(c, _):
            r0 = pl.multiple_of(c * ch, ch)
            taps = _conv_taps(xpad[pl.ds(r0, ch + SUBLANES), :], ch)
            xc = cb_ref[...] + sum(cw_ref[pl.ds(j, 1), :] * taps[j] for j in range(CONV_W))
            r, i = _rg_gates(xc, wr_ref[...], br_ref[...], wi_ref[...], bi_ref[...], sp8)
            log_a = -(r * sp8)
            a_s[pl.ds(r0, ch), :] = jnp.exp(log_a)
            u_s[pl.ds(r0, ch), :] = jnp.sqrt(jnp.maximum(_one_minus_exp(2.0 * log_a), 0.0)) * (i * xc)
            return 0

        lax.fori_loop(0, seq // ch, gates, 0)
        _scan_rows(a_s, u_s, h_ref, seq, wb, reverse=False)

        def gate_out(c, _):
            r0 = pl.multiple_of(c * ch, ch)
            gl, _ = _gelu_and_grad(ga_ref[pl.ds(r0, ch), :])
            ya_ref[pl.ds(r0, ch), :] = (h_ref[pl.ds(r0, ch), :] * gl).astype(BF16)
            return 0

        lax.fori_loop(0, seq // ch, gate_out, 0)

    seg = lambda k: pl.BlockSpec((None, seq, wb), lambda s, b, k=k: (k, s, b))
    blk = pl.BlockSpec((seq, wb), lambda s, b: (s, b))
    vec = pl.BlockSpec((1, wb), lambda s, b: (0, b))
    wsp = pl.BlockSpec((None, wb, wb), lambda s, b: (b, 0, 0))
    return pl.pallas_call(
        body, name="mixer_a_fwd", grid=(n_seq, n_blk),
        in_specs=[seg(0), seg(1), pl.BlockSpec((CONV_W, wb), lambda s, b: (0, b)), vec, wsp, vec, wsp, vec, vec],
        out_specs=[blk, blk],
        out_shape=[jax.ShapeDtypeStruct((t_rows, d), F32), jax.ShapeDtypeStruct((t_rows, d), BF16)],
        scratch_shapes=[pltpu.VMEM((seq + SUBLANES, wb), F32), pltpu.VMEM((seq, wb), F32),
                        pltpu.VMEM((seq, wb), F32)],
        compiler_params=_cp("parallel", "parallel"))(proj, proj, conv_w, conv_b, w_r, b_r, w_i, b_i, lam)


def _mixer_a_bwd(proj, h, d_ya, conv_w, conv_b, w_r, b_r, w_i, b_i, lam, seq):
    _, t_rows, d = proj.shape
    n_seq, n_blk = t_rows // seq, d // RG_BLOCK_W
    wb = RG_BLOCK_W
    ch = _row_tile(seq, ROW_CHUNK)
    n_ch = seq // ch

    def body(xa_ref, ga_ref, h_ref, dya_ref, cw_ref, cb_ref, wr_ref, br_ref, wi_ref, bi_ref, lam_ref,
             dp_ref, dwr_ref, dwi_ref, dbr_ref, dbi_ref, dlam_ref, dcw_ref, dcb_ref,
             xpad, hpad, a_s, e_pad, g_s, xc_s, r_s, i_s, dxc_pad):
        @pl.when(pl.program_id(1) == 0)
        def _():
            for ref in (dwr_ref, dwi_ref, dbr_ref, dbi_ref, dlam_ref, dcw_ref, dcb_ref):
                ref[...] = jnp.zeros_like(ref)

        zeros8 = jnp.zeros((SUBLANES, wb), F32)
        xpad[0:SUBLANES, :] = zeros8
        xpad[SUBLANES:, :] = xa_ref[...]
        hpad[0:SUBLANES, :] = zeros8
        hpad[SUBLANES:, :] = h_ref[...]
        e_pad[seq:, :] = zeros8
        dxc_pad[seq:, :] = zeros8
        lam_v = lam_ref[...]
        sp8 = RG_C * _softplus(-lam_v)

        def recompute(c, _):
            r0 = pl.multiple_of(c * ch, ch)
            rows = pl.ds(r0, ch)
            taps = _conv_taps(xpad[pl.ds(r0, ch + SUBLANES), :], ch)
            xc = cb_ref[...] + sum(cw_ref[pl.ds(j, 1), :] * taps[j] for j in range(CONV_W))
            r, i = _rg_gates(xc, wr_ref[...], br_ref[...], wi_ref[...], bi_ref[...], sp8)
            a = jnp.exp(-(r * sp8))
            gl, dgl = _gelu_and_grad(ga_ref[rows, :])
            dya = dya_ref[rows, :]
            g = dya * gl
            dp_ref[1, rows, :] = (dya * h_ref[rows, :] * dgl).astype(BF16)
            a_s[rows, :] = a
            e_pad[rows, :] = a * g
            g_s[rows, :] = g
            xc_s[rows, :] = xc
            r_s[rows, :] = r
            i_s[rows, :] = i
            return 0

        lax.fori_loop(0, n_ch, recompute, 0)
        _scan_rows(a_s, e_pad, e_pad, seq, wb, reverse=True)

        def grads(c, _):
            r0 = pl.multiple_of(c * ch, ch)
            rows = pl.ds(r0, ch)
            halo = pl.ds(r0, ch + SUBLANES)
            dh = g_s[rows, :] + _shift_rows(e_pad[halo, :], ch + SUBLANES - 1)[0:ch, :]
            h_prev = _shift_rows(hpad[halo, :], 1)[SUBLANES:, :]
            xc, r, i = xc_s[rows, :], r_s[rows, :], i_s[rows, :]
            log_a = -(r * sp8)
            a = jnp.exp(log_a)
            om = _one_minus_exp(2.0 * log_a)
            sq = jnp.sqrt(jnp.maximum(om, 0.0))
            t1 = dh * xc
            d_i = t1 * sq
            d_la = dh * h_prev * a + jnp.where(om > 0.0, -(t1 * i) * (1.0 - om) / sq, 0.0)
            dpr = -(d_la * sp8) * r * (1.0 - r)
            dpi = d_i * i * (1.0 - i)
            dprb, dpib, xb = dpr.astype(BF16), dpi.astype(BF16), xc.astype(BF16)
            dxc = dh * sq * i + _dot_nt(dprb, wr_ref[...]) + _dot_nt(dpib, wi_ref[...])
            dwr_ref[...] += _dot_tn(xb, dprb)
            dwi_ref[...] += _dot_tn(xb, dpib)
            dbr_ref[...] += jnp.sum(dpr, axis=0, keepdims=True)
            dbi_ref[...] += jnp.sum(dpi, axis=0, keepdims=True)
            dlam_ref[...] += jnp.sum(d_la * r, axis=0, keepdims=True) * (RG_C * _sigmoid(-lam_v))
            dcb_ref[...] += jnp.sum(dxc, axis=0, keepdims=True)
            taps = _conv_taps(xpad[halo, :], ch)
            for j in range(CONV_W):
                dcw_ref[pl.ds(j, 1), :] += jnp.sum(dxc * taps[j], axis=0, keepdims=True)
            dxc_pad[rows, :] = dxc
            return 0

        lax.fori_loop(0, n_ch, grads, 0)

        def conv_bwd(c, _):
            r0 = pl.multiple_of(c * ch, ch)
            de = dxc_pad[pl.ds(r0, ch + SUBLANES), :]
            dxa = sum(cw_ref[pl.ds(j, 1), :] * _shift_rows(de, ch + SUBLANES - (CONV_W - 1 - j))[0:ch, :]
                      for j in range(CONV_W))
            dp_ref[0, pl.ds(r0, ch), :] = dxa.astype(BF16)
            return 0

        lax.fori_loop(0, n_ch, conv_bwd, 0)

    seg = lambda k: pl.BlockSpec((None, seq, wb), lambda b, s, k=k: (k, s, b))
    blk = pl.BlockSpec((seq, wb), lambda b, s: (s, b))
    vec = pl.BlockSpec((1, wb), lambda b, s: (0, b))
    taps = pl.BlockSpec((CONV_W, wb), lambda b, s: (0, b))
    wsp = pl.BlockSpec((None, wb, wb), lambda b, s: (b, 0, 0))
    vec_shape = jax.ShapeDtypeStruct((1, d), F32)
    w_shape = jax.ShapeDtypeStruct((n_blk, wb, wb), F32)
    pad = pltpu.VMEM((seq + SUBLANES, wb), F32)
    full = pltpu.VMEM((seq, wb), F32)
    return pl.pallas_call(
        body, name="mixer_a_bwd", grid=(n_blk, n_seq),
        in_specs=[seg(0), seg(1), blk, blk, taps, vec, wsp, vec, wsp, vec, vec],
        out_specs=[pl.BlockSpec((2, seq, wb), lambda b, s: (0, s, b)), wsp, wsp, vec, vec, vec, taps, vec],
        out_shape=[jax.ShapeDtypeStruct((2, t_rows, d), BF16), w_shape, w_shape, vec_shape, vec_shape,
                   vec_shape, jax.ShapeDtypeStruct((CONV_W, d), F32), vec_shape],
        scratch_shapes=[pad, pad, full, pad, full, full, full, full, pad],
        compiler_params=_cp("parallel", "arbitrary"))(
            proj, proj, h, d_ya, conv_w, conv_b, w_r, b_r, w_i, b_i, lam)


def _hg_prepare(q_ref, z_ref, lb, rows):
    z = z_ref[rows, :]
    sig, nsig = _sigmoid_pair(z)
    fg = lb + (1.0 - lb) * sig
    log_f = jnp.log(jnp.maximum(fg, F_MIN))
    key = (1.0 - lb) * nsig
    qs, _ = _silu_and_grad(q_ref[rows, :])
    return qs, key, log_f, sig, fg


HG_UNROLL_TERMS = 32
HG_UNROLL_FWD = 32
HG_UNROLL_BWD = 32
HG_HALF = HG_CHUNK // 2
HG_STACK = HG_CHUNK * HG_HALF
assert HG_HALF == SUBLANES


def _half_of(x, s):
    return x[:HG_HALF, :] if s < HG_HALF else x[HG_HALF:, :]


def _hg_decay(g_ref, r0, g_rows, first_row, s):
    rid = lax.broadcasted_iota(jnp.int32, g_rows.shape, 0) + first_row
    gs = g_ref[pl.ds(r0 + s, 1), :]
    return jnp.where(rid >= s, jnp.exp(g_rows - gs), 0.0)


def _half_start(s):
    return 0 if s < HG_HALF else HG_HALF


def _hg_cross_decays(gc):
    rid = lax.broadcasted_iota(jnp.int32, (HG_CHUNK, HG_DK), 0)
    g_mid = gc[HG_HALF - 1:HG_HALF, :]
    e_hi = jnp.where(rid >= HG_HALF, jnp.exp(gc - g_mid), 0.0)
    e_lo = jnp.where(rid < HG_HALF, jnp.exp(g_mid - gc), 0.0)
    return e_hi, e_lo


def _hg_cross(qc, kc, gc):
    e_hi, e_lo = _hg_cross_decays(gc)
    return (qc * e_hi).astype(BF16), (kc * e_lo).astype(BF16)


def _stack(slabs):
    return jnp.concatenate(slabs, axis=0).astype(BF16)


def _slab_row_sums():
    row = lax.broadcasted_iota(jnp.int32, (HG_CHUNK, HG_STACK), 0)
    col = lax.broadcasted_iota(jnp.int32, (HG_CHUNK, HG_STACK), 1)
    lo = row * HG_HALF
    return jnp.where((col >= lo) & (col < lo + HG_HALF), 1.0, 0.0).astype(BF16)


def _for_chunks(n, unroll, *stages):
    unroll = min(unroll, n)
    assert n % unroll == 0

    def trip(i, _):
        chunks = [i * unroll + u for u in range(unroll)]
        carried = [stages[0](c) for c in chunks]
        for stage in stages[1:]:
            carried = [stage(c, x) for c, x in zip(chunks, carried)]
        return 0

    lax.fori_loop(0, n // unroll, trip, 0)


def _hg_state_terms(v_ref, k_ref, g_ref, states, n_chunks):
    def issue(c):
        rows = pl.ds(pl.multiple_of(c * HG_CHUNK, HG_CHUNK), HG_CHUNK)
        gc = g_ref[rows, :]
        k_end = k_ref[rows, :] * jnp.exp(gc[HG_CHUNK - 1:HG_CHUNK, :] - gc)
        return _dot_tn(v_ref[rows, :].astype(BF16), k_end.astype(BF16))

    def store(c, term):
        states[c] = term

    _for_chunks(n_chunks, HG_UNROLL_TERMS, issue, store)


def _hg_state_chain(states, g_ref, carry_ref, n_chunks, reverse):
    unroll = min(8, n_chunks)
    assert n_chunks % unroll == 0
    carry_ref[...] = jnp.zeros_like(carry_ref)

    def trip(i, _):
        st = carry_ref[...]
        for u in range(unroll):
            k = i * unroll + u
            c = n_chunks - 1 - k if reverse else k
            term = states[c]
            states[c] = st
            st = st * jnp.exp(g_ref[pl.ds(c * HG_CHUNK + HG_CHUNK - 1, 1), :]) + term
        carry_ref[...] = st
        return 0

    lax.fori_loop(0, n_chunks // unroll, trip, 0)


def _hgrn_fwd(proj, lower_bound, hg_gain, seq, carry=None):
    _, t_rows, d = proj.shape
    n_seq, n_head = t_rows // seq, d // HG_DK
    ch = _row_tile(seq, ROW_CHUNK)
    n_chunks = seq // HG_CHUNK

    def body(q_ref, z_ref, v_ref, lb_ref, gain_ref, o_ref, on_ref, qs_s, k_s, g_s, states, st_ref):
        lb = lb_ref[...]
        tri = _group_cumsum_matrix(ch, HG_CHUNK)

        def prepare(c):
            rows = pl.ds(pl.multiple_of(c * ch, ch), ch)
            qs, key, log_f, _, _ = _hg_prepare(q_ref, z_ref, lb, rows)
            qs_s[rows, :] = qs
            k_s[rows, :] = key
            return _group_cumsum_mxu(log_f, tri)

        def store_cumsum(c, g):
            g_s[pl.ds(pl.multiple_of(c * ch, ch), ch), :] = g

        _for_chunks(seq // ch, 4, prepare, store_cumsum)
        _hg_state_terms(v_ref, k_s, g_s, states, n_chunks)
        _hg_state_chain(states, g_s, st_ref, n_chunks, reverse=False)
        ones = jnp.ones((HG_DK, HG_DK), BF16)

        def issue(c):
            r0 = pl.multiple_of(c * HG_CHUNK, HG_CHUNK)
            rows = pl.ds(r0, HG_CHUNK)
            qc, gc = qs_s[rows, :], g_s[rows, :]
            o = _dot_nt((qc * jnp.exp(gc)).astype(BF16), states[c].astype(BF16))
            pairs = [qc[_half_start(s):, :] * _hg_decay(g_s, r0, gc[_half_start(s):, :], _half_start(s), s)
                     * k_s[pl.ds(r0 + s, 1), :] for s in range(HG_CHUNK)]
            score = _dot(_stack(pairs), ones)
            return o, score

        def combine(c, issued):
            o, score = issued
            r0 = pl.multiple_of(c * HG_CHUNK, HG_CHUNK)
            halves = [o[:HG_HALF, :], o[HG_HALF:, :]]
            first = 0
            for s in range(HG_CHUNK):
                vs = v_ref[pl.ds(r0 + s, 1), :]
                if s < HG_HALF:
                    halves[0] += score[first:first + HG_HALF, :] * vs
                    first += HG_HALF
                halves[1] += score[first:first + HG_HALF, :] * vs
                first += HG_HALF
            o = jnp.concatenate(halves, axis=0)
            o_ref[pl.ds(r0, HG_CHUNK), :] = o
            on_ref[pl.ds(r0, HG_CHUNK), :] = (o * _rstd(o) * gain_ref[...]).astype(BF16)

        _for_chunks(n_chunks, HG_UNROLL_FWD, issue, combine)

    seg = lambda k: pl.BlockSpec((None, seq, HG_DK), lambda s, h, k=k: (k, s, h))
    blk = pl.BlockSpec((seq, HG_DK), lambda s, h: (s, h))
    full = pltpu.VMEM((seq, HG_DK), F32)
    return _call_carrying(
        body, carry, name="hgrn_fwd", grid=(n_seq, n_head),
        in_specs=[seg(2), seg(3), seg(4), pl.BlockSpec((1, HG_DK), lambda s, h: (0, h)),
                  pl.BlockSpec((1, HG_DK), lambda s, h: (0, 0))],
        out_specs=[blk, blk],
        out_shape=[jax.ShapeDtypeStruct((t_rows, d), F32), jax.ShapeDtypeStruct((t_rows, d), BF16)],
        scratch_shapes=[full, full, full, pltpu.VMEM((n_chunks, HG_DK, HG_DK), F32),
                        pltpu.VMEM((HG_DK, HG_DK), F32)],
        semantics=("parallel", "parallel"), args=(proj, proj, proj, lower_bound, hg_gain))


def _hgrn_bwd(proj, lower_bound, hg_gain, o, d_on, seq, carry=None):
    _, t_rows, d = proj.shape
    n_seq, n_head = t_rows // seq, d // HG_DK
    ch = _row_tile(seq, ROW_CHUNK)
    n_chunks = seq // HG_CHUNK
    cc = HG_CHUNK

    def body(q_ref, z_ref, v_ref, lb_ref, gain_ref, o_ref, don_ref, dp_ref, dlb_ref, dgain_ref,
             qs_s, k_s, g_s, do_s, dqs_s, dk_s, dlf_s, states, dstates, carry_ref):
        hh, ss = pl.program_id(0), pl.program_id(1)
        lb = lb_ref[...]

        @pl.when(ss == 0)
        def _():
            dlb_ref[...] = jnp.zeros_like(dlb_ref)

        @pl.when((ss == 0) & (hh == 0))
        def _():
            dgain_ref[...] = jnp.zeros_like(dgain_ref)

        tri = _group_cumsum_matrix(ch, cc)

        def prepare(c):
            rows = pl.ds(pl.multiple_of(c * ch, ch), ch)
            qs, key, log_f, _, _ = _hg_prepare(q_ref, z_ref, lb, rows)
            qs_s[rows, :] = qs
            k_s[rows, :] = key
            do, dgain = _rms_bwd(don_ref[rows, :], o_ref[rows, :], gain_ref[...])
            do_s[rows, :] = do
            dgain_ref[...] += dgain
            return _group_cumsum_mxu(log_f, tri)

        def store_cumsum(c, g):
            g_s[pl.ds(pl.multiple_of(c * ch, ch), ch), :] = g

        _for_chunks(seq // ch, 4, prepare, store_cumsum)

        _hg_state_terms(v_ref, k_s, g_s, states, n_chunks)
        _hg_state_chain(states, g_s, carry_ref, n_chunks, reverse=False)

        def query_term(c):
            rows = pl.ds(pl.multiple_of(c * cc, cc), cc)
            q_in = qs_s[rows, :] * jnp.exp(g_s[rows, :])
            return _dot_tn(do_s[rows, :].astype(BF16), q_in.astype(BF16))

        def store_query_term(c, term):
            dstates[c] = term

        _for_chunks(n_chunks, HG_UNROLL_TERMS, query_term, store_query_term)
        _hg_state_chain(dstates, g_s, carry_ref, n_chunks, reverse=True)
        ones = jnp.ones((HG_DK, HG_DK), BF16)
        row_sums = _slab_row_sums()

        def chunk_rows(c):
            r0 = pl.multiple_of(c * cc, cc)
            return r0, pl.ds(r0, cc)

        def through_state(c):
            r0, rows = chunk_rows(c)
            qc, kc, gc, vc, doc = qs_s[rows, :], k_s[rows, :], g_s[rows, :], v_ref[rows, :], do_s[rows, :]
            st, dst = states[c], dstates[c]
            g_last = gc[cc - 1:cc, :]
            e_last, e_end = jnp.exp(g_last), jnp.exp(g_last - gc)
            dob, vcb, dstb = doc.astype(BF16), vc.astype(BF16), dst.astype(BF16)
            dqs = _dot(dob, st.astype(BF16))
            dk_state = _dot(vcb, dstb)
            dv = _dot_nt((kc * e_end).astype(BF16), dstb)
            cots = [_half_of(doc, s) * v_ref[pl.ds(r0 + s, 1), :] for s in range(cc)]
            d_score = _dot(_stack(cots), ones)
            x, y = _hg_cross(qc, kc, gc)
            cross = (_dot_nt(dob, vcb), _dot_nt(vcb, dob), _dot_nt(y, x))
            return dqs, dk_state, dv, d_score, e_last * jnp.sum(dst * st, axis=0, keepdims=True), cross

        def pair_terms(c, carried):
            dqs, dk_state, dv, d_score, d_glast, (da_cross, da_cross_t, a_cross_t) = carried
            r0, rows = chunk_rows(c)
            qc, kc, gc = qs_s[rows, :], k_s[rows, :], g_s[rows, :]
            dqs = dqs * jnp.exp(gc)
            dk_state = dk_state * jnp.exp(gc[cc - 1:cc, :] - gc)
            d_glast = d_glast + jnp.sum(kc * dk_state, axis=0, keepdims=True)
            dqs_half = [dqs[:HG_HALF, :], dqs[HG_HALF:, :]]
            pairs, dk_terms = [], []
            for s in range(cc):
                qv = _half_of(qc, s)
                decay = _hg_decay(g_s, r0, _half_of(gc, s), _half_start(s), s)
                ks = k_s[pl.ds(r0 + s, 1), :]
                da_decay = d_score[s * HG_HALF:(s + 1) * HG_HALF, :] * decay
                pairs.append(qv * decay * ks)
                dk_terms.append(da_decay * qv)
                dqs_half[s // HG_HALF] += da_decay * ks
            score = _dot(_stack(pairs), ones)
            dk = dk_state + _dot(row_sums, _stack(dk_terms))
            x, y = _hg_cross(qc, kc, gc)
            cross = (_dot(da_cross.astype(BF16), y), _dot(da_cross_t.astype(BF16), x),
                     _dot(a_cross_t.astype(BF16), do_s[rows, :].astype(BF16)))
            return jnp.concatenate(dqs_half, axis=0), dk, dv, score, d_glast, cross

        def value_terms(c, carried):
            dqs, dk, dv, score, d_glast, (dx_cross, dy_cross, dv_cross) = carried
            _, rows = chunk_rows(c)
            doc, gc = do_s[rows, :], g_s[rows, :]
            dv_terms = [score[s * HG_HALF:(s + 1) * HG_HALF, :] * _half_of(doc, s) for s in range(cc)]
            e_hi, e_lo = _hg_cross_decays(gc)
            return (dqs + dx_cross * e_hi, dk + dy_cross * e_lo,
                    dv + dv_cross + _dot(row_sums, _stack(dv_terms)), d_glast)

        def store(c, x):
            dqs, dk, dv, d_glast = x
            _, rows = chunk_rows(c)
            d_g = qs_s[rows, :] * dqs - k_s[rows, :] * dk
            dlf_s[rows, :] = _seg_cumsum(d_g, cc, reverse=True) + d_glast
            dqs_s[rows, :] = dqs
            dk_s[rows, :] = dk
            dp_ref[2, rows, :] = dv.astype(BF16)

        _for_chunks(n_chunks, HG_UNROLL_BWD, through_state, pair_terms, value_terms, store)

        def finish(c, _):
            rows = pl.ds(pl.multiple_of(c * ch, ch), ch)
            sig, nsig = _sigmoid_pair(z_ref[rows, :])
            fg = lb + (1.0 - lb) * sig
            _, dsilu = _silu_and_grad(q_ref[rows, :])
            dp_ref[0, rows, :] = (dqs_s[rows, :] * dsilu).astype(BF16)
            dfg = jnp.where(fg > F_MIN, dlf_s[rows, :] / fg, 0.0)
            dk = dk_s[rows, :]
            dp_ref[1, rows, :] = ((dfg - dk) * (1.0 - lb) * sig * nsig).astype(BF16)
            dlb_ref[...] += jnp.sum((dfg - dk) * nsig, axis=0, keepdims=True)
            return 0

        lax.fori_loop(0, seq // ch, finish, 0)

    seg = lambda k: pl.BlockSpec((None, seq, HG_DK), lambda h, s, k=k: (k, s, h))
    blk = pl.BlockSpec((seq, HG_DK), lambda h, s: (s, h))
    full = pltpu.VMEM((seq, HG_DK), F32)
    return _call_carrying(
        body, carry, name="hgrn_bwd", grid=(n_head, n_seq),
        in_specs=[seg(2), seg(3), seg(4), pl.BlockSpec((1, HG_DK), lambda h, s: (0, h)),
                  pl.BlockSpec((1, HG_DK), lambda h, s: (0, 0)), blk, blk],
        out_specs=[pl.BlockSpec((3, seq, HG_DK), lambda h, s: (0, s, h)),
                   pl.BlockSpec((1, HG_DK), lambda h, s: (0, h)),
                   pl.BlockSpec((1, HG_DK), lambda h, s: (0, 0))],
        out_shape=[jax.ShapeDtypeStruct((3, t_rows, d), BF16), jax.ShapeDtypeStruct((1, d), F32),
                   jax.ShapeDtypeStruct((1, HG_DK), F32)],
        scratch_shapes=[full, full, full, full, full, full, full,
                        pltpu.VMEM((n_chunks, HG_DK, HG_DK), F32), pltpu.VMEM((n_chunks, HG_DK, HG_DK), F32),
                        pltpu.VMEM((HG_DK, HG_DK), F32)],
        semantics=("arbitrary", "arbitrary"), args=(proj, proj, proj, lower_bound, hg_gain, o, d_on))


def _mesh_place():
    x, y, c = lax.axis_index("x"), lax.axis_index("y"), lax.axis_index("c")
    return x, y, c


def _peer(place, k):
    x, y, c = place
    px = 1 - x if k & 4 else x
    py = 1 - y if k & 2 else y
    pc = 1 - c if k & 1 else c
    return (px, py, pc), 4 * px + 2 * py + pc


class _Exchange:
    def __init__(self, srcs, gather):
        self.n = len(srcs)
        self.gather = gather
        self.out_shape = [jax.ShapeDtypeStruct((N_DEV,) + tuple(s.shape if gather else s.shape[1:]), s.dtype)
                          for s in srcs]
        self.scratch = [pltpu.SemaphoreType.DMA((self.n * (N_DEV - 1),)),
                        pltpu.SemaphoreType.DMA((self.n * (N_DEV - 1),)),
                        pltpu.SemaphoreType.DMA((self.n,))]

    def _copies(self, src_refs, out_refs, sems):
        send_sems, recv_sems, local_sems = sems
        place = _mesh_place()
        me = 4 * place[0] + 2 * place[1] + place[2]
        local, sends, recvs = [], [], []
        for a, (src, out) in enumerate(zip(src_refs, out_refs)):
            outgoing = (lambda idx, src=src: src) if self.gather else (lambda idx, src=src: src.at[idx])
            local.append(pltpu.make_async_copy(outgoing(me), out.at[me], local_sems.at[a]))
            for k in range(1, N_DEV):
                peer, peer_idx = _peer(place, k)
                sem = a * (N_DEV - 1) + k - 1
                sends.append(pltpu.make_async_remote_copy(
                    src_ref=outgoing(peer_idx), dst_ref=out.at[me], send_sem=send_sems.at[sem],
                    recv_sem=recv_sems.at[sem], device_id=peer, device_id_type=MESH_ID))
                recvs.append(pltpu.make_async_remote_copy(
                    src_ref=outgoing(peer_idx), dst_ref=out.at[peer_idx], send_sem=send_sems.at[sem],
                    recv_sem=recv_sems.at[sem], device_id=peer, device_id_type=MESH_ID))
        return local, sends, recvs

    def start(self, src_refs, out_refs, sems):
        local, sends, _ = self._copies(src_refs, out_refs, sems)
        for cp in local + sends:
            cp.start()

    def wait(self, src_refs, out_refs, sems):
        local, sends, recvs = self._copies(src_refs, out_refs, sems)
        for cp in recvs:
            cp.wait_recv()
        for cp in sends:
            cp.wait_send()
        for cp in local:
            cp.wait()


def _call_carrying(body, carry, *, name, grid, in_specs, out_specs, out_shape, scratch_shapes, semantics, args):
    if carry is None:
        outs = pl.pallas_call(body, name=name, grid=grid, in_specs=in_specs, out_specs=out_specs,
                              out_shape=out_shape, scratch_shapes=scratch_shapes,
                              compiler_params=_cp(*semantics))(*args)
        return outs, []
    srcs, gather = carry
    ex = _Exchange(srcs, gather)
    n, n_in, n_out, n_scr = ex.n, len(in_specs), len(out_specs), len(scratch_shapes)

    def wrapped(*refs):
        ins, refs = refs[:n_in], refs[n_in:]
        src_refs, refs = refs[:n], refs[n:]
        outs, refs = refs[:n_out], refs[n_out:]
        dst_refs, refs = refs[:n], refs[n:]
        scratch, sems = refs[:n_scr], refs[n_scr:]
        first, last = None, None
        for axis, size in enumerate(grid):
            i = pl.program_id(axis)
            first = (i == 0) if first is None else first & (i == 0)
            last = (i == size - 1) if last is None else last & (i == size - 1)

        @pl.when(first)
        def _():
            ex.start(src_refs, dst_refs, sems)

        body(*ins, *outs, *scratch)

        @pl.when(last)
        def _():
            ex.wait(src_refs, dst_refs, sems)

    any_space = pl.BlockSpec(memory_space=pl.ANY)
    res = pl.pallas_call(
        wrapped, name=name + "_carrying", grid=grid, in_specs=list(in_specs) + [any_space] * n,
        out_specs=list(out_specs) + [any_space] * n, out_shape=list(out_shape) + ex.out_shape,
        scratch_shapes=list(scratch_shapes) + ex.scratch,
        compiler_params=_cp(*(["arbitrary"] * len(grid))))(*args, *srcs)
    return res[:n_out], res[n_out:]


def _exchange(srcs, name, gather):
    ex = _Exchange(srcs, gather)
    n = ex.n

    def body(*refs):
        src_refs, out_refs, sems = refs[:n], refs[n:2 * n], refs[2 * n:]
        ex.start(src_refs, out_refs, sems)
        ex.wait(src_refs, out_refs, sems)

    any_space = pl.BlockSpec(memory_space=pl.ANY)
    return pl.pallas_call(
        body, name=name, in_specs=[any_space] * n, out_specs=[any_space] * n,
        out_shape=ex.out_shape, scratch_shapes=ex.scratch)(*srcs)


def _reduce_adamw(parts, w, m, v, name):
    rows, cols = w.shape
    n_seg = len(parts)
    seg_rows = rows // n_seg
    tr = _row_tile(seg_rows, 128)
    per_seg = seg_rows // tr
    c1 = np.float32(1.0 - ADAM_B1 ** ADAM_STEP)
    c2 = np.float32(1.0 - ADAM_B2 ** ADAM_STEP)

    def body(*refs):
        p_refs = refs[:n_seg]
        w_ref, m_ref, v_ref, g_ref, d_ref, nm_ref, nv_ref = refs[n_seg:]
        seg = pl.program_id(0)
        for k, p_ref in enumerate(p_refs):
            @pl.when(seg == k)
            def _(p_ref=p_ref):
                g = p_ref[0].astype(F32)
                for dev in range(1, N_DEV):
                    g = g + p_ref[dev].astype(F32)
                g_ref[...] = g

        g = g_ref[...]
        nm = ADAM_B1 * m_ref[...] + (1.0 - ADAM_B1) * g
        nv = ADAM_B2 * v_ref[...] + (1.0 - ADAM_B2) * (g * g)
        nm_ref[...] = nm
        nv_ref[...] = nv
        d_ref[...] = -ADAM_LR * ((nm / c1) / (jnp.sqrt(nv / c2) + ADAM_EPS) + ADAM_WD * w_ref[...])

    def part_spec(k):
        return pl.BlockSpec((N_DEV, tr, cols), lambda s, i, k=k: (0, jnp.where(s == k, i, 0), 0))

    blk = pl.BlockSpec((tr, cols), lambda s, i: (s * per_seg + i, 0))
    shp = jax.ShapeDtypeStruct((rows, cols), F32)
    return pl.pallas_call(
        body, name=name, grid=(n_seg, per_seg),
        in_specs=[part_spec(k) for k in range(n_seg)] + [blk, blk, blk],
        out_specs=[blk, blk, blk, blk], out_shape=[shp, shp, shp, shp],
        compiler_params=_cp("arbitrary", "arbitrary"))(*parts, w, m, v)


def _pack(arrays, lead=0):
    parts = []
    for a in arrays:
        f = a.reshape(a.shape[:lead] + (-1, LANES))
        pad = -f.shape[lead] % PACK_ROWS
        if pad:
            f = jnp.pad(f, [(0, 0)] * lead + [(0, pad), (0, 0)])
        parts.append(f)
    return jnp.concatenate(parts, axis=lead)


def _unpack(buf, shapes, lead=0):
    out, r = [], 0
    for shp in shapes:
        n = int(np.prod(shp)) // LANES
        part = lax.slice_in_dim(buf, r, r + n, axis=lead)
        out.append(part.reshape(buf.shape[:lead] + tuple(shp)))
        r += n + (-n % PACK_ROWS)
    return out


REPLICATED = ("lb_logits", "norm_mix", "conv_b", "b_r", "b_i", "lam", "hg_norm", "norm_mlp", "norm_final")
SMALL_SHARDED = ("conv_w", "w_r", "w_i")
LARGE_SHARDED = ("w_in", "w_out", "w_up", "w_down")
WEIGHTS = ("lb_logits", "norm_mix", "w_in", "conv_w", "conv_b", "w_r", "b_r", "w_i", "b_i", "lam", "hg_norm",
           "w_out", "norm_mlp", "w_up", "w_down", "norm_final")


def _matmul_weight_shards(p):
    depth = p["w_in"].shape[0]
    cast = {k: p[k].astype(BF16) for k in LARGE_SHARDED}
    return ([cast["w_in"][l] for l in range(depth)],
            [[cast[k][l] for k in ("w_out", "w_up", "w_down")] for l in range(depth)])


def _gathered_rest(got):
    w_out, w_up, w_down = got
    d = w_out.shape[2]
    return dict(w_out=w_out.reshape(d, d), w_up=w_up, w_down=w_down)


def _unpack_mixer_weights(small, p):
    depth, d, _ = p["w_in"].shape
    n_blk = d // RG_BLOCK_W
    conv_w, w_r, w_i = _unpack(small, [p["conv_w"].shape, p["w_r"].shape, p["w_i"].shape], lead=1)
    conv_w = conv_w.transpose(1, 2, 0, 3).reshape(depth, CONV_W, d)
    w_r = w_r.transpose(1, 2, 0, 3, 4).reshape(depth, n_blk, RG_BLOCK_W, RG_BLOCK_W).astype(BF16)
    w_i = w_i.transpose(1, 2, 0, 3, 4).reshape(depth, n_blk, RG_BLOCK_W, RG_BLOCK_W).astype(BF16)
    return conv_w, w_r, w_i


def _local_step(x, target, p):
    bl, seq, d = x.shape
    depth = p["w_in"].shape[0]
    t_rows = bl * seq
    row = lambda a, l: a[l:l + 1]
    lbs = _lower_bounds_fwd(p["lb_logits"])
    shard_in, shard_rest = _matmul_weight_shards(p)
    w_in = _exchange([shard_in[0]], "gather_w_in", gather=True)[0]
    cur = x.reshape(t_rows, d)
    saved, layers = [], []
    for l in range(depth):
        if l == 0:
            (proj, gates, h), small = _inproj_fwd(cur, row(p["norm_mix"], l), w_in,
                                                  carry=([_pack([p["conv_w"], p["w_r"], p["w_i"]])], True))
            conv_w, w_r, w_i = _unpack_mixer_weights(small[0], p)
        else:
            (proj, gates, h), _ = _inproj_fwd(cur, row(p["norm_mix"], l), w_in)
        w = dict(w_in=w_in, conv_w=conv_w[l], w_r=w_r[l], w_i=w_i[l])
        hs, y_a = _mixer_a_fwd(proj, w["conv_w"], row(p["conv_b"], l), w["w_r"], row(p["b_r"], l), w["w_i"],
                               row(p["b_i"], l), row(p["lam"], l), seq)
        (o, o_n), got = _hgrn_fwd(proj, row(lbs, l), row(p["hg_norm"], l), seq,
                                  carry=(shard_rest[l] + ([shard_in[l + 1]] if l + 1 < depth else []), True))
        w.update(_gathered_rest(got[:3]))
        w_in = got[3] if l + 1 < depth else None
        layers.append(w)
        x_mid, y = _merge_out_fwd(gates, y_a, o_n, cur, w["w_out"])
        x_out, u, h2 = _mlp_fwd(x_mid, row(p["norm_mlp"], l), w["w_up"], w["w_down"])
        saved.append(dict(x_in=cur, proj=proj, gates=gates, h=h, hs=hs, y_a=y_a, o=o, o_n=o_n, x_mid=x_mid, y=y, u=u, h2=h2))
        cur = x_out
    loss8, dx, dxb, g_norm_final = _loss_head(cur, p["norm_final"].reshape(1, d), target.reshape(t_rows, d))
    small = ("norm_mix", "conv_w", "conv_b", "w_r", "b_r", "w_i", "b_i", "lam", "hg_norm", "norm_mlp")
    g = {k: [None] * depth for k in small}
    d_lbs, received = [None] * depth, [None] * depth
    g_w_in = None
    for l in reversed(range(depth)):
        s, w = saved[l], layers[l]
        dx_mid, dx_mid_b, du, act, g["norm_mlp"][l] = _mlp_bwd(dx, dxb, s["u"], s["x_mid"], row(p["norm_mlp"], l),
                                                               w["w_up"], w["w_down"])
        g_w_down = _wgrad(act, dxb[None], "wgrad_down")
        g_w_up = _wgrad(s["h2"][None], du, "wgrad_up")
        d_ya, d_on, dp_c = _outproj_bwd(dx_mid_b, w["w_out"], s["gates"], s["y_a"], s["o_n"])
        g_w_out = _wgrad(s["y"][None], dx_mid_b[None], "wgrad_out").reshape(N_DEV, d // N_DEV, d)
        (dp_b, d_lbs[l], g["hg_norm"][l]), got = _hgrn_bwd(
            s["proj"], row(lbs, l), row(p["hg_norm"], l), s["o"], d_on, seq,
            carry=([g_w_out, g_w_up, g_w_down] + ([g_w_in] if g_w_in is not None else []), False))
        received[l] = [None] + list(got[:3])
        if g_w_in is not None:
            received[l + 1][0] = got[3]
        (dp_a, g["w_r"][l], g["w_i"][l], g["b_r"][l], g["b_i"][l], g["lam"][l], g["conv_w"][l],
         g["conv_b"][l]) = _mixer_a_bwd(s["proj"], s["hs"], d_ya, w["conv_w"], row(p["conv_b"], l), w["w_r"],
                                        row(p["b_r"], l), w["w_i"], row(p["b_i"], l), row(p["lam"], l), seq)
        hb = s["h"][None]
        g_w_in = jnp.concatenate([_wgrad(hb, dp_a, "wgrad_in_pair"), _wgrad(hb, dp_b, "wgrad_in_triple"),
                                  _wgrad(hb, dp_c, "wgrad_in_triple")], axis=0)
        carry = None
        if l == 0:
            carry = ([g_w_in, _mixer_grads_by_owner(g, d)], False)
        (dx, dxb, g["norm_mix"][l]), got = _inproj_bwd(dx_mid, dp_a, dp_b, dp_c, w["w_in"], s["x_in"],
                                                       row(p["norm_mix"], l), carry=carry)
    received[0][0], received_mixer = got
    grads = {k: jnp.stack(v) for k, v in g.items()}
    for k in ("norm_mix", "conv_b", "b_r", "b_i", "lam", "hg_norm", "norm_mlp"):
        grads[k] = grads[k][:, 0]
    grads["lb_logits"] = _lower_bounds_bwd(p["lb_logits"], jnp.concatenate(d_lbs, axis=0))
    grads["norm_final"] = g_norm_final[0]
    return loss8[0, 0], dx.reshape(bl, seq, d), grads, received, received_mixer


def _mixer_grads_by_owner(g, d):
    d8, n_blk, rb = d // N_DEV, d // RG_BLOCK_W, RG_BLOCK_W // N_DEV
    depth = len(g["conv_w"])
    conv_w, w_r, w_i = (jnp.stack(g[k]) for k in SMALL_SHARDED)
    return _pack([conv_w.reshape(depth, CONV_W, N_DEV, d8).transpose(2, 0, 1, 3),
                  w_r.reshape(depth, n_blk, N_DEV, rb, RG_BLOCK_W).transpose(2, 0, 1, 3, 4),
                  w_i.reshape(depth, n_blk, N_DEV, rb, RG_BLOCK_W).transpose(2, 0, 1, 3, 4)], lead=1)


def _update(p, mom1, mom2, grads, received, received_mixer):
    depth = p["w_in"].shape[0]
    out = {}

    for i, k in enumerate(LARGE_SHARDED):
        shp = p[k].shape
        flat = lambda a: a.reshape(shp[0] * shp[1], shp[2])
        parts = [received[l][i] for l in range(depth)]
        res = _reduce_adamw(parts, flat(p[k]), flat(mom1[k]), flat(mom2[k]), "adamw_" + k)
        out[k] = [r.reshape(shp) for r in res]

    res = _reduce_adamw([received_mixer], *[_pack([src[k] for k in SMALL_SHARDED]) for src in (p, mom1, mom2)],
                        "adamw_mixer")
    shapes = [p[k].shape for k in SMALL_SHARDED]
    for i, vals in enumerate(zip(*[_unpack(r, shapes) for r in res])):
        out[SMALL_SHARDED[i]] = list(vals)

    parts = _exchange([_pack([grads[k] for k in REPLICATED])], "gather_grad_replicated", gather=True)
    res = _reduce_adamw(parts, *[_pack([src[k] for k in REPLICATED]) for src in (p, mom1, mom2)],
                        "adamw_replicated")
    shapes = [p[k].shape for k in REPLICATED]
    for i, vals in enumerate(zip(*[_unpack(r, shapes) for r in res])):
        out[REPLICATED[i]] = list(vals)

    return tuple(out[k][i] for i in range(4) for k in WEIGHTS)


def kernel(x, lb_logits, norm_mix, w_in, conv_w, conv_b, w_r, b_r, w_i, b_i, lam, hg_norm, w_out, norm_mlp, w_up, w_down, norm_final, loss_target, m_lb_logits, m_norm_mix, m_w_in, m_conv_w, m_conv_b, m_w_r, m_b_r, m_w_i, m_b_i, m_lam, m_hg_norm, m_w_out, m_norm_mlp, m_w_up, m_w_down, m_norm_final, v_lb_logits, v_norm_mix, v_w_in, v_conv_w, v_conv_b, v_w_r, v_b_r, v_w_i, v_b_i, v_lam, v_hg_norm, v_w_out, v_norm_mlp, v_w_up, v_w_down, v_norm_final):
    p = dict(lb_logits=lb_logits, norm_mix=norm_mix, w_in=w_in, conv_w=conv_w, conv_b=conv_b, w_r=w_r, b_r=b_r,
             w_i=w_i, b_i=b_i, lam=lam, hg_norm=hg_norm, w_out=w_out, norm_mlp=norm_mlp, w_up=w_up,
             w_down=w_down, norm_final=norm_final)
    mom1 = dict(lb_logits=m_lb_logits, norm_mix=m_norm_mix, w_in=m_w_in, conv_w=m_conv_w, conv_b=m_conv_b,
                w_r=m_w_r, b_r=m_b_r, w_i=m_w_i, b_i=m_b_i, lam=m_lam, hg_norm=m_hg_norm, w_out=m_w_out,
                norm_mlp=m_norm_mlp, w_up=m_w_up, w_down=m_w_down, norm_final=m_norm_final)
    mom2 = dict(lb_logits=v_lb_logits, norm_mix=v_norm_mix, w_in=v_w_in, conv_w=v_conv_w, conv_b=v_conv_b,
                w_r=v_w_r, b_r=v_b_r, w_i=v_w_i, b_i=v_b_i, lam=v_lam, hg_norm=v_hg_norm, w_out=v_w_out,
                norm_mlp=v_norm_mlp, w_up=v_w_up, w_down=v_w_down, norm_final=v_norm_final)
    loss, grad_x, grads, received, received_mixer = _local_step(x, loss_target, p)
    loss = lax.psum(loss, ("x", "y", "c"))
    return (loss, grad_x) + _update(p, mom1, mom2, grads, received, received_mixer)
```

```python
import numpy as np

import jax
import jax.numpy as jnp
from jax import lax
from jax.experimental import pallas as pl
from jax.experimental.pallas import tpu as pltpu

F32 = jnp.float32
BF16 = jnp.bfloat16
MESH_ID = pl.DeviceIdType.MESH

N_DEV = 8
N_MIXER_SEGMENTS = 5
NORM_EPS = 1e-6
RG_C = 8.0
RG_BLOCK_W = 256
CONV_W = 4
HG_DK = 128
F_MIN = 1e-30
HG_CHUNK = 16
SUBLANES = 8
LANES = 128
PACK_ROWS = 16
SCAN_GROUP = 16
ROW_CHUNK = 256
ROW_TILE_WEIGHT_STREAM = 1024
WGRAD_TOKEN_TILE = 2048
VMEM_LIMIT_V7X = 56 * 1024 * 1024

ADAM_LR = 0.001
ADAM_B1 = 0.9
ADAM_B2 = 0.999
ADAM_EPS = 1e-08
ADAM_WD = 0.01
ADAM_STEP = 10

GELU_C = 0.7978845608028654
GELU_K = 0.044715


def _cp(*sem):
    return pltpu.CompilerParams(dimension_semantics=sem, vmem_limit_bytes=VMEM_LIMIT_V7X)


def _row_tile(n, cap):
    if n <= cap:
        return n
    t = cap - cap % 16
    while n % t:
        t -= 16
    return t


def _dot(a, b):
    return jnp.dot(a, b, preferred_element_type=F32)


def _dot_nt(a, b):
    return lax.dot_general(a, b, (((1,), (1,)), ((), ())), preferred_element_type=F32)


def _dot_tn(a, b):
    return lax.dot_general(a, b, (((0,), (0,)), ((), ())), preferred_element_type=F32)


def _sigmoid(x):
    return jax.nn.sigmoid(x)


def _sigmoid_pair(x):
    e = jnp.exp(-jnp.abs(x))
    r = 1.0 / (1.0 + e)
    er = e * r
    pos = x >= 0.0
    return jnp.where(pos, r, er), jnp.where(pos, er, r)


def _log1p_pos(y):
    return jnp.where(y < 0.01, y * (1.0 - y * (0.5 - y * (1.0 / 3.0))), jnp.log(1.0 + y))


def _softplus(x):
    return jnp.maximum(x, 0.0) + _log1p_pos(jnp.exp(-jnp.abs(x)))


def _one_minus_exp(x):
    series = -x * (1.0 + x * 0.5 * (1.0 + x * (1.0 / 3.0) * (1.0 + x * 0.25 * (1.0 + x * 0.2))))
    return jnp.where(x > -0.1, series, 1.0 - jnp.exp(x))


def _gelu_and_grad(x):
    x2 = x * x
    t = jnp.tanh(GELU_C * x * (1.0 + GELU_K * x2))
    g = 0.5 * x * (1.0 + t)
    dg = 0.5 * (1.0 + t) + 0.5 * x * (1.0 - t * t) * GELU_C * (1.0 + 3.0 * GELU_K * x2)
    return g, dg


def _silu_and_grad(x):
    s = _sigmoid(x)
    return x * s, s * (1.0 + x * (1.0 - s))


def _rstd(x):
    return lax.rsqrt(jnp.mean(x * x, axis=-1, keepdims=True) + NORM_EPS)


def _rms_bwd(dh, x, g):
    rstd = _rstd(x)
    xh = x * rstd
    dxh = dh * g
    dx = rstd * (dxh - xh * jnp.mean(dxh * xh, axis=-1, keepdims=True))
    return dx, jnp.sum(dh * xh, axis=0, keepdims=True)


def _shift_rows(x, k):
    n = x.shape[0]
    k = k % n
    return x if k == 0 else pltpu.roll(x, k, axis=0)


def _seg_cumsum(x, seg, reverse=False):
    n = x.shape[0]
    rid = lax.broadcasted_iota(jnp.int32, x.shape, 0) & (seg - 1)
    d = 1
    while d < seg:
        if reverse:
            x = jnp.where(rid < seg - d, x + _shift_rows(x, n - d), x)
        else:
            x = jnp.where(rid >= d, x + _shift_rows(x, d), x)
        d *= 2
    return x


def _group_cumsum_matrix(n, seg):
    row = lax.broadcasted_iota(jnp.int32, (n, n), 0)
    col = lax.broadcasted_iota(jnp.int32, (n, n), 1)
    same_group = (row & ~(seg - 1)) == (col & ~(seg - 1))
    return jnp.where(same_group & (col <= row), 1.0, 0.0).astype(BF16)


def _group_cumsum_mxu(x, tri):
    hi = x.astype(BF16)
    lo = (x - hi.astype(F32)).astype(BF16)
    return _dot(tri, hi) + _dot(tri, lo)


def _scan_rows(a_ref, b_ref, out_ref, n_rows, width, reverse):
    gr = min(SCAN_GROUP, n_rows)
    rid = lax.broadcasted_iota(jnp.int32, (gr, width), 0)
    n_groups = n_rows // gr
    per_trip = min(4, n_groups)
    assert n_groups % per_trip == 0

    def local_scan(g):
        r0 = pl.multiple_of(g * gr, gr)
        a = a_ref[pl.ds(r0, gr), :]
        b = b_ref[pl.ds(r0, gr), :]
        d = 1
        while d < gr:
            if reverse:
                keep = rid < gr - d
                a_sh, b_sh = _shift_rows(a, gr - d), _shift_rows(b, gr - d)
            else:
                keep = rid >= d
                a_sh, b_sh = _shift_rows(a, d), _shift_rows(b, d)
            b = jnp.where(keep, a * b_sh + b, b)
            a = jnp.where(keep, a * a_sh, a)
            d *= 2
        return r0, a, b

    def trip(i, carry):
        first = i * per_trip
        groups = [n_groups - 1 - (first + u) if reverse else first + u for u in range(per_trip)]
        for r0, a, b in [local_scan(g) for g in groups]:
            out = a * carry + b
            out_ref[pl.ds(r0, gr), :] = out
            edge = out[0:1, :] if reverse else out[gr - 1:gr, :]
            carry = jnp.broadcast_to(edge, (gr, width))
        return carry

    lax.fori_loop(0, n_groups // per_trip, trip, jnp.zeros((gr, width), F32))


def _lb_softmax_rows(x_ref, depth):
    rows = [x_ref[pl.ds(l, 1), :] for l in range(depth)]
    top = rows[0]
    for r in rows[1:]:
        top = jnp.maximum(top, r)
    e = [jnp.exp(r - top) for r in rows]
    tot = e[0]
    for r in e[1:]:
        tot = tot + r
    return [r / tot for r in e]


def _lower_bounds_fwd(lb_logits):
    depth, d = lb_logits.shape

    def body(x_ref, o_ref):
        sm = _lb_softmax_rows(x_ref, depth)
        cum = jnp.zeros((1, d), F32)
        for l in range(depth):
            cum = cum + sm[l]
            o_ref[pl.ds(l, 1), :] = jnp.clip(cum - sm[0], 0.0, 1.0)

    return pl.pallas_call(body, name="lower_bounds_fwd",
                          out_shape=jax.ShapeDtypeStruct((depth, d), F32))(lb_logits)


def _lower_bounds_bwd(lb_logits, d_lbs):
    depth, d = lb_logits.shape

    def body(x_ref, g_ref, o_ref):
        sm = _lb_softmax_rows(x_ref, depth)
        cum = jnp.zeros((1, d), F32)
        d_cum = []
        for l in range(depth):
            cum = cum + sm[l]
            v = cum - sm[0]
            d_cum.append(jnp.where((v > 0.0) & (v < 1.0), g_ref[pl.ds(l, 1), :], 0.0))
        d_sm = []
        tail = jnp.zeros((1, d), F32)
        for l in reversed(range(depth)):
            tail = tail + d_cum[l]
            d_sm.append(tail)
        d_sm = d_sm[::-1]
        d_sm[0] = d_sm[0] - tail
        inner = jnp.zeros((1, d), F32)
        for l in range(depth):
            inner = inner + sm[l] * d_sm[l]
        for l in range(depth):
            o_ref[pl.ds(l, 1), :] = sm[l] * (d_sm[l] - inner)

    return pl.pallas_call(body, name="lower_bounds_bwd",
                          out_shape=jax.ShapeDtypeStruct((depth, d), F32))(lb_logits, d_lbs)


def _inproj_fwd(x, gain, w_seg, carry=None):
    t_rows, d = x.shape
    tm = _row_tile(t_rows, ROW_TILE_WEIGHT_STREAM)
    n_gate = N_DEV - N_MIXER_SEGMENTS

    def body(x_ref, g_ref, w_ref, proj_ref, gates_ref, h_ref):
        j = pl.program_id(1)

        @pl.when(j == 0)
        def _():
            xv = x_ref[...]
            h_ref[...] = (xv * _rstd(xv) * g_ref[...]).astype(BF16)

        @pl.when(j < N_MIXER_SEGMENTS)
        def _():
            proj_ref[...] = _dot(h_ref[...], w_ref[...])

        @pl.when(j >= N_MIXER_SEGMENTS)
        def _():
            gates_ref[...] = _dot(h_ref[...], w_ref[...]).astype(BF16)

    return _call_carrying(
        body, carry, name="inproj_fwd", grid=(t_rows // tm, N_DEV),
        in_specs=[pl.BlockSpec((tm, d), lambda i, j: (i, 0)),
                  pl.BlockSpec((1, d), lambda i, j: (0, 0)),
                  pl.BlockSpec((None, d, d), lambda i, j: (j, 0, 0))],
        out_specs=[pl.BlockSpec((None, tm, d), lambda i, j: (jnp.minimum(j, N_MIXER_SEGMENTS - 1), i, 0)),
                   pl.BlockSpec((None, tm, d), lambda i, j: (jnp.maximum(j - N_MIXER_SEGMENTS, 0), i, 0)),
                   pl.BlockSpec((tm, d), lambda i, j: (i, 0))],
        out_shape=[jax.ShapeDtypeStruct((N_MIXER_SEGMENTS, t_rows, d), F32),
                   jax.ShapeDtypeStruct((n_gate, t_rows, d), BF16),
                   jax.ShapeDtypeStruct((t_rows, d), BF16)],
        scratch_shapes=[], semantics=("parallel", "arbitrary"), args=(x, gain, w_seg))


def _merge_out_fwd(gates, y_a, o_n, x, w_out):
    t_rows, d = x.shape
    tm = _row_tile(t_rows, 256)

    def body(g_ref, ma_ref, mb_ref, ya_ref, on_ref, x_ref, w_ref, xmid_ref, y_ref):
        g = g_ref[...].astype(F32)
        ya, on = ya_ref[...].astype(F32), on_ref[...].astype(F32)
        y = (_sigmoid(ma_ref[...].astype(F32)) * ya
             + _sigmoid(mb_ref[...].astype(F32)) * (on * (g * _sigmoid(g))))
        yb = y.astype(BF16)
        y_ref[...] = yb
        xmid_ref[...] = x_ref[...] + _dot(yb, w_ref[...])

    seg = lambda k: pl.BlockSpec((None, tm, d), lambda i, k=k: (k, i, 0))
    row = pl.BlockSpec((tm, d), lambda i: (i, 0))
    return pl.pallas_call(
        body, name="merge_out_fwd", grid=(t_rows // tm,),
        in_specs=[seg(0), seg(1), seg(2), row, row, row, pl.BlockSpec((d, d), lambda i: (0, 0))],
        out_specs=[row, row],
        out_shape=[jax.ShapeDtypeStruct((t_rows, d), F32), jax.ShapeDtypeStruct((t_rows, d), BF16)],
        compiler_params=_cp("parallel"))(gates, gates, gates, y_a, o_n, x, w_out)


def _mlp_fwd(x_mid, gain, w_up, w_down):
    t_rows, d = x_mid.shape
    f8 = w_up.shape[2]
    tm = _row_tile(t_rows, ROW_TILE_WEIGHT_STREAM)

    def body(x_ref, g_ref, wu_ref, wd_ref, out_ref, u_ref, h_ref):
        @pl.when(pl.program_id(1) == 0)
        def _():
            xv = x_ref[...]
            h_ref[...] = (xv * _rstd(xv) * g_ref[...]).astype(BF16)
            out_ref[...] = xv

        u = _dot(h_ref[...], wu_ref[...])
        u_ref[...] = u.astype(BF16)
        r = jnp.maximum(u, 0.0)
        out_ref[...] += _dot((r * r).astype(BF16), wd_ref[...])

    row = pl.BlockSpec((tm, d), lambda i, j: (i, 0))
    return pl.pallas_call(
        body, name="mlp_fwd", grid=(t_rows // tm, N_DEV),
        in_specs=[row, pl.BlockSpec((1, d), lambda i, j: (0, 0)),
                  pl.BlockSpec((None, d, f8), lambda i, j: (j, 0, 0)),
                  pl.BlockSpec((None, f8, d), lambda i, j: (j, 0, 0))],
        out_specs=[row, pl.BlockSpec((None, tm, f8), lambda i, j: (j, i, 0)), row],
        out_shape=[jax.ShapeDtypeStruct((t_rows, d), F32),
                   jax.ShapeDtypeStruct((N_DEV, t_rows, f8), BF16),
                   jax.ShapeDtypeStruct((t_rows, d), BF16)],
        compiler_params=_cp("parallel", "arbitrary"))(x_mid, gain, w_up, w_down)


def _loss_head(x, gain, target):
    t_rows, d = x.shape
    tm = _row_tile(t_rows, 512)

    def body(x_ref, g_ref, t_ref, loss_ref, dx_ref, dxb_ref, dg_ref):
        @pl.when(pl.program_id(0) == 0)
        def _():
            loss_ref[...] = jnp.zeros_like(loss_ref)
            dg_ref[...] = jnp.zeros_like(dg_ref)

        xv = x_ref[...]
        g = g_ref[...]
        err = xv * _rstd(xv) * g - t_ref[...]
        loss_ref[...] += (0.5 / d) * jnp.sum(err * err)
        dx, dg = _rms_bwd(err * (1.0 / d), xv, g)
        dx_ref[...] = dx
        dxb_ref[...] = dx.astype(BF16)
        dg_ref[...] += dg

    row = pl.BlockSpec((tm, d), lambda i: (i, 0))
    vec = pl.BlockSpec((1, d), lambda i: (0, 0))
    return pl.pallas_call(
        body, name="loss_head", grid=(t_rows // tm,),
        in_specs=[row, vec, row],
        out_specs=[pl.BlockSpec((SUBLANES, LANES), lambda i: (0, 0)), row, row, vec],
        out_shape=[jax.ShapeDtypeStruct((SUBLANES, LANES), F32),
                   jax.ShapeDtypeStruct((t_rows, d), F32),
                   jax.ShapeDtypeStruct((t_rows, d), BF16),
                   jax.ShapeDtypeStruct((1, d), F32)],
        compiler_params=_cp("arbitrary"))(x, gain, target)


def _mlp_bwd(d_out, d_out_b, u, x_mid, gain, w_up, w_down):
    t_rows, d = x_mid.shape
    f8 = w_up.shape[2]
    tm = _row_tile(t_rows, ROW_TILE_WEIGHT_STREAM)
    sub = _row_tile(tm, ROW_CHUNK)

    def body(do_ref, dob_ref, u_ref, x_ref, g_ref, wu_ref, wd_ref, dx_ref, dxb_ref, du_ref, act_ref, dg_ref):
        j = pl.program_id(1)

        @pl.when((pl.program_id(0) == 0) & (j == 0))
        def _():
            dg_ref[...] = jnp.zeros_like(dg_ref)

        @pl.when(j == 0)
        def _():
            dx_ref[...] = jnp.zeros_like(dx_ref)

        r = jnp.maximum(u_ref[...].astype(F32), 0.0)
        act_ref[...] = (r * r).astype(BF16)
        du = (_dot_nt(dob_ref[...], wd_ref[...]) * (2.0 * r)).astype(BF16)
        du_ref[...] = du
        dx_ref[...] += _dot_nt(du, wu_ref[...])

        @pl.when(j == N_DEV - 1)
        def _():
            def finish(c, _):
                rows = pl.ds(pl.multiple_of(c * sub, sub), sub)
                dx, dg = _rms_bwd(dx_ref[rows, :], x_ref[rows, :], g_ref[...])
                dx = dx + do_ref[rows, :]
                dx_ref[rows, :] = dx
                dxb_ref[rows, :] = dx.astype(BF16)
                dg_ref[...] += dg
                return 0

            lax.fori_loop(0, tm // sub, finish, 0)

    row = pl.BlockSpec((tm, d), lambda i, j: (i, 0))
    vec = pl.BlockSpec((1, d), lambda i, j: (0, 0))
    hid = pl.BlockSpec((None, tm, f8), lambda i, j: (j, i, 0))
    return pl.pallas_call(
        body, name="mlp_bwd", grid=(t_rows // tm, N_DEV),
        in_specs=[row, row, hid, row, vec,
                  pl.BlockSpec((None, d, f8), lambda i, j: (j, 0, 0)),
                  pl.BlockSpec((None, f8, d), lambda i, j: (j, 0, 0))],
        out_specs=[row, row, hid, hid, vec],
        out_shape=[jax.ShapeDtypeStruct((t_rows, d), F32),
                   jax.ShapeDtypeStruct((t_rows, d), BF16),
                   jax.ShapeDtypeStruct((N_DEV, t_rows, f8), BF16),
                   jax.ShapeDtypeStruct((N_DEV, t_rows, f8), BF16),
                   jax.ShapeDtypeStruct((1, d), F32)],
        compiler_params=_cp("arbitrary", "arbitrary"))(d_out, d_out_b, u, x_mid, gain, w_up, w_down)


def _outproj_bwd(dx_mid_b, w_out, gates, y_a, o_n):
    t_rows, d = y_a.shape
    tm = _row_tile(t_rows, 256)

    def body(dx_ref, w_ref, g_ref, ma_ref, mb_ref, ya_ref, on_ref, dya_ref, don_ref, dp_ref):
        dy = _dot_nt(dx_ref[...], w_ref[...])
        sa = _sigmoid(ma_ref[...].astype(F32))
        sb = _sigmoid(mb_ref[...].astype(F32))
        sg, dsg = _silu_and_grad(g_ref[...].astype(F32))
        ya = ya_ref[...].astype(F32)
        on = on_ref[...].astype(F32)
        dya_ref[...] = dy * sa
        t = dy * sb
        don_ref[...] = t * sg
        dp_ref[0] = (t * on * dsg).astype(BF16)
        dp_ref[1] = (dy * ya * sa * (1.0 - sa)).astype(BF16)
        dp_ref[2] = (dy * on * sg * sb * (1.0 - sb)).astype(BF16)

    seg = lambda k: pl.BlockSpec((None, tm, d), lambda i, k=k: (k, i, 0))
    row = pl.BlockSpec((tm, d), lambda i: (i, 0))
    return pl.pallas_call(
        body, name="outproj_bwd", grid=(t_rows // tm,),
        in_specs=[row, pl.BlockSpec((d, d), lambda i: (0, 0)), seg(0), seg(1), seg(2), row, row],
        out_specs=[row, row, pl.BlockSpec((3, tm, d), lambda i: (0, i, 0))],
        out_shape=[jax.ShapeDtypeStruct((t_rows, d), F32),
                   jax.ShapeDtypeStruct((t_rows, d), F32),
                   jax.ShapeDtypeStruct((3, t_rows, d), BF16)],
        compiler_params=_cp("parallel"))(dx_mid_b, w_out, gates, gates, gates, y_a, o_n)


def _inproj_bwd(dx_mid, dp_a, dp_b, dp_c, w_seg, x_in, gain, carry=None):
    t_rows, d = x_in.shape
    tm = _row_tile(t_rows, ROW_TILE_WEIGHT_STREAM)
    sub = _row_tile(tm, ROW_CHUNK)
    n_a, n_b = dp_a.shape[0], dp_b.shape[0]

    def body(dxm_ref, a_ref, b_ref, c_ref, w_ref, x_ref, g_ref, dx_ref, dxb_ref, dg_ref):
        j = pl.program_id(1)

        @pl.when((pl.program_id(0) == 0) & (j == 0))
        def _():
            dg_ref[...] = jnp.zeros_like(dg_ref)

        @pl.when(j == 0)
        def _():
            dx_ref[...] = jnp.zeros_like(dx_ref)

        @pl.when(j < n_a)
        def _():
            dx_ref[...] += _dot_nt(a_ref[...], w_ref[...])

        @pl.when((j >= n_a) & (j < n_a + n_b))
        def _():
            dx_ref[...] += _dot_nt(b_ref[...], w_ref[...])

        @pl.when(j >= n_a + n_b)
        def _():
            dx_ref[...] += _dot_nt(c_ref[...], w_ref[...])

        @pl.when(j == N_DEV - 1)
        def _():
            def finish(c, _):
                rows = pl.ds(pl.multiple_of(c * sub, sub), sub)
                dx, dg = _rms_bwd(dx_ref[rows, :], x_ref[rows, :], g_ref[...])
                dx = dx + dxm_ref[rows, :]
                dx_ref[rows, :] = dx
                dxb_ref[rows, :] = dx.astype(BF16)
                dg_ref[...] += dg
                return 0

            lax.fori_loop(0, tm // sub, finish, 0)

    def part(first, n):
        return pl.BlockSpec((None, tm, d), lambda i, j: (jnp.clip(j - first, 0, n - 1), i, 0))

    row = pl.BlockSpec((tm, d), lambda i, j: (i, 0))
    vec = pl.BlockSpec((1, d), lambda i, j: (0, 0))
    return _call_carrying(
        body, carry, name="inproj_bwd", grid=(t_rows // tm, N_DEV),
        in_specs=[row, part(0, n_a), part(n_a, n_b), part(n_a + n_b, dp_c.shape[0]),
                  pl.BlockSpec((None, d, d), lambda i, j: (j, 0, 0)), row, vec],
        out_specs=[row, row, vec],
        out_shape=[jax.ShapeDtypeStruct((t_rows, d), F32),
                   jax.ShapeDtypeStruct((t_rows, d), BF16),
                   jax.ShapeDtypeStruct((1, d), F32)],
        scratch_shapes=[], semantics=("arbitrary", "arbitrary"),
        args=(dx_mid, dp_a, dp_b, dp_c, w_seg, x_in, gain))


def _wgrad(a3, b3, name):
    n_a, t_rows, k_a = a3.shape
    n_b, _, n_cols = b3.shape
    n = max(n_a, n_b)
    bk = _row_tile(k_a, 1024)
    bn = n_cols if n_cols <= 1024 else 1024
    tt = _row_tile(t_rows, WGRAD_TOKEN_TILE)
    n_t = t_rows // tt

    def body(a_ref, b_ref, o_ref, acc_ref):
        t, j = pl.program_id(2), pl.program_id(3)
        part = _dot_tn(a_ref[...], b_ref[...])

        @pl.when(t == 0)
        def _():
            acc_ref[j] = part

        @pl.when(t > 0)
        def _():
            acc_ref[j] += part

        @pl.when(t == n_t - 1)
        def _():
            o_ref[...] = acc_ref[j].astype(BF16)

    def out_map(p, q, t, j):
        return (jnp.where(t == n_t - 1, j, 0), p, q)

    return pl.pallas_call(
        body, name=name, grid=(k_a // bk, n_cols // bn, n_t, n),
        in_specs=[pl.BlockSpec((None, tt, bk), lambda p, q, t, j: (j if n_a > 1 else 0, t, p)),
                  pl.BlockSpec((None, tt, bn), lambda p, q, t, j: (j if n_b > 1 else 0, t, q))],
        out_specs=pl.BlockSpec((None, bk, bn), out_map),
        out_shape=jax.ShapeDtypeStruct((n, k_a, n_cols), BF16),
        scratch_shapes=[pltpu.VMEM((n, bk, bn), F32)],
        compiler_params=_cp("parallel", "parallel", "arbitrary", "arbitrary"))(a3, b3)


def _conv_taps(xe, n):
    return [_shift_rows(xe, CONV_W - 1 - j)[SUBLANES:SUBLANES + n, :] for j in range(CONV_W)]


def _rg_gates(xc, w_r, b_r, w_i, b_i, sp8):
    xb = xc.astype(BF16)
    r = _sigmoid(_dot(xb, w_r) + b_r)
    i = _sigmoid(_dot(xb, w_i) + b_i)
    return r, i


def _mixer_a_fwd(proj, conv_w, conv_b, w_r, b_r, w_i, b_i, lam, seq):
    _, t_rows, d = proj.shape
    n_seq, n_blk = t_rows // seq, d // RG_BLOCK_W
    wb = RG_BLOCK_W
    ch = _row_tile(seq, ROW_CHUNK)

    def body(xa_ref, ga_ref, cw_ref, cb_ref, wr_ref, br_ref, wi_ref, bi_ref, lam_ref, h_ref, ya_ref,
             xpad, a_s, u_s):
        xpad[0:SUBLANES, :] = jnp.zeros((SUBLANES, wb), F32)
        xpad[SUBLANES:, :] = xa_ref[...]
        sp8 = RG_C * _softplus(-lam_ref[...])

        def gates(c, _):
            r0 = pl.multiple_of(c * ch, ch)
            taps = _conv_taps(xpad[pl.ds(r0, ch + SUBLANES), :], ch)
            xc = cb_ref[...] + sum(cw_ref[pl.ds(j, 1), :] * taps[j] for j in range(CONV_W))
            r, i = _rg_gates(xc, wr_ref[...], br_ref[...], wi_ref[...], bi_ref[...], sp8)
            log_a = -(r * sp8)
            a_s[pl.ds(r0, ch), :] = jnp.exp(log_a)
            u_s[pl.ds(r0, ch), :] = jnp.sqrt(jnp.maximum(_one_minus_exp(2.0 * log_a), 0.0)) * (i * xc)
            return 0

        lax.fori_loop(0, seq // ch, gates, 0)
        _scan_rows(a_s, u_s, h_ref, seq, wb, reverse=False)

        def gate_out(c, _):
            r0 = pl.multiple_of(c * ch, ch)
            gl, _ = _gelu_and_grad(ga_ref[pl.ds(r0, ch), :])
            ya_ref[pl.ds(r0, ch), :] = (h_ref[pl.ds(r0, ch), :] * gl).astype(BF16)
            return 0

        lax.fori_loop(0, seq // ch, gate_out, 0)

    seg = lambda k: pl.BlockSpec((None, seq, wb), lambda s, b, k=k: (k, s, b))
    blk = pl.BlockSpec((seq, wb), lambda s, b: (s, b))
    vec = pl.BlockSpec((1, wb), lambda s, b: (0, b))
    wsp = pl.BlockSpec((None, wb, wb), lambda s, b: (b, 0, 0))
    return pl.pallas_call(
        body, name="mixer_a_fwd", grid=(n_seq, n_blk),
        in_specs=[seg(0), seg(1), pl.BlockSpec((CONV_W, wb), lambda s, b: (0, b)), vec, wsp, vec, wsp, vec, vec],
        out_specs=[blk, blk],
        out_shape=[jax.ShapeDtypeStruct((t_rows, d), F32), jax.ShapeDtypeStruct((t_rows, d), BF16)],
        scratch_shapes=[pltpu.VMEM((seq + SUBLANES, wb), F32), pltpu.VMEM((seq, wb), F32),
                        pltpu.VMEM((seq, wb), F32)],
        compiler_params=_cp("parallel", "parallel"))(proj, proj, conv_w, conv_b, w_r, b_r, w_i, b_i, lam)


def _mixer_a_bwd(proj, h, d_ya, conv_w, conv_b, w_r, b_r, w_i, b_i, lam, seq):
    _, t_rows, d = proj.shape
    n_seq, n_blk = t_rows // seq, d // RG_BLOCK_W
    wb = RG_BLOCK_W
    ch = _row_tile(seq, ROW_CHUNK)
    n_ch = seq // ch

    def body(xa_ref, ga_ref, h_ref, dya_ref, cw_ref, cb_ref, wr_ref, br_ref, wi_ref, bi_ref, lam_ref,
             dp_ref, dwr_ref, dwi_ref, dbr_ref, dbi_ref, dlam_ref, dcw_ref, dcb_ref,
             xpad, hpad, a_s, e_pad, g_s, xc_s, r_s, i_s, dxc_pad):
        @pl.when(pl.program_id(1) == 0)
        def _():
            for ref in (dwr_ref, dwi_ref, dbr_ref, dbi_ref, dlam_ref, dcw_ref, dcb_ref):
                ref[...] = jnp.zeros_like(ref)

        zeros8 = jnp.zeros((SUBLANES, wb), F32)
        xpad[0:SUBLANES, :] = zeros8
        xpad[SUBLANES:, :] = xa_ref[...]
        hpad[0:SUBLANES, :] = zeros8
        hpad[SUBLANES:, :] = h_ref[...]
        e_pad[seq:, :] = zeros8
        dxc_pad[seq:, :] = zeros8
        lam_v = lam_ref[...]
        sp8 = RG_C * _softplus(-lam_v)

        def recompute(c, _):
            r0 = pl.multiple_of(c * ch, ch)
            rows = pl.ds(r0, ch)
            taps = _conv_taps(xpad[pl.ds(r0, ch + SUBLANES), :], ch)
            xc = cb_ref[...] + sum(cw_ref[pl.ds(j, 1), :] * taps[j] for j in range(CONV_W))
            r, i = _rg_gates(xc, wr_ref[...], br_ref[...], wi_ref[...], bi_ref[...], sp8)
            a = jnp.exp(-(r * sp8))
            gl, dgl = _gelu_and_grad(ga_ref[rows, :])
            dya = dya_ref[rows, :]
            g = dya * gl
            dp_ref[1, rows, :] = (dya * h_ref[rows, :] * dgl).astype(BF16)
            a_s[rows, :] = a
            e_pad[rows, :] = a * g
            g_s[rows, :] = g
            xc_s[rows, :] = xc
            r_s[rows, :] = r
            i_s[rows, :] = i
            return 0

        lax.fori_loop(0, n_ch, recompute, 0)
        _scan_rows(a_s, e_pad, e_pad, seq, wb, reverse=True)

        def grads(c, _):
            r0 = pl.multiple_of(c * ch, ch)
            rows = pl.ds(r0, ch)
            halo = pl.ds(r0, ch + SUBLANES)
            dh = g_s[rows, :] + _shift_rows(e_pad[halo, :], ch + SUBLANES - 1)[0:ch, :]
            h_prev = _shift_rows(hpad[halo, :], 1)[SUBLANES:, :]
            xc, r, i = xc_s[rows, :], r_s[rows, :], i_s[rows, :]
            log_a = -(r * sp8)
            a = jnp.exp(log_a)
            om = _one_minus_exp(2.0 * log_a)
            sq = jnp.sqrt(jnp.maximum(om, 0.0))
            t1 = dh * xc
            d_i = t1 * sq
            d_la = dh * h_prev * a + jnp.where(om > 0.0, -(t1 * i) * (1.0 - om) / sq, 0.0)
            dpr = -(d_la * sp8) * r * (1.0 - r)
            dpi = d_i * i * (1.0 - i)
            dprb, dpib, xb = dpr.astype(BF16), dpi.astype(BF16), xc.astype(BF16)
            dxc = dh * sq * i + _dot_nt(dprb, wr_ref[...]) + _dot_nt(dpib, wi_ref[...])
            dwr_ref[...] += _dot_tn(xb, dprb)
            dwi_ref[...] += _dot_tn(xb, dpib)
            dbr_ref[...] += jnp.sum(dpr, axis=0, keepdims=True)
            dbi_ref[...] += jnp.sum(dpi, axis=0, keepdims=True)
            dlam_ref[...] += jnp.sum(d_la * r, axis=0, keepdims=True) * (RG_C * _sigmoid(-lam_v))
            dcb_ref[...] += jnp.sum(dxc, axis=0, keepdims=True)
            taps = _conv_taps(xpad[halo, :], ch)
            for j in range(CONV_W):
                dcw_ref[pl.ds(j, 1), :] += jnp.sum(dxc * taps[j], axis=0, keepdims=True)
            dxc_pad[rows, :] = dxc
            return 0

        lax.fori_loop(0, n_ch, grads, 0)

        def conv_bwd(c, _):
            r0 = pl.multiple_of(c * ch, ch)
            de = dxc_pad[pl.ds(r0, ch + SUBLANES), :]
            dxa = sum(cw_ref[pl.ds(j, 1), :] * _shift_rows(de, ch + SUBLANES - (CONV_W - 1 - j))[0:ch, :]
                      for j in range(CONV_W))
            dp_ref[0, pl.ds(r0, ch), :] = dxa.astype(BF16)
            return 0

        lax.fori_loop(0, n_ch, conv_bwd, 0)

    seg = lambda k: pl.BlockSpec((None, seq, wb), lambda b, s, k=k: (k, s, b))
    blk = pl.BlockSpec((seq, wb), lambda b, s: (s, b))
    vec = pl.BlockSpec((1, wb), lambda b, s: (0, b))
    taps = pl.BlockSpec((CONV_W, wb), lambda b, s: (0, b))
    wsp = pl.BlockSpec((None, wb, wb), lambda b, s: (b, 0, 0))
    vec_shape = jax.ShapeDtypeStruct((1, d), F32)
    w_shape = jax.ShapeDtypeStruct((n_blk, wb, wb), F32)
    pad = pltpu.VMEM((seq + SUBLANES, wb), F32)
    full = pltpu.VMEM((seq, wb), F32)
    return pl.pallas_call(
        body, name="mixer_a_bwd", grid=(n_blk, n_seq),
        in_specs=[seg(0), seg(1), blk, blk, taps, vec, wsp, vec, wsp, vec, vec],
        out_specs=[pl.BlockSpec((2, seq, wb), lambda b, s: (0, s, b)), wsp, wsp, vec, vec, vec, taps, vec],
        out_shape=[jax.ShapeDtypeStruct((2, t_rows, d), BF16), w_shape, w_shape, vec_shape, vec_shape,
                   vec_shape, jax.ShapeDtypeStruct((CONV_W, d), F32), vec_shape],
        scratch_shapes=[pad, pad, full, pad, full, full, full, full, pad],
        compiler_params=_cp("parallel", "arbitrary"))(
            proj, proj, h, d_ya, conv_w, conv_b, w_r, b_r, w_i, b_i, lam)


def _hg_prepare(q_ref, z_ref, lb, rows):
    z = z_ref[rows, :]
    sig, nsig = _sigmoid_pair(z)
    fg = lb + (1.0 - lb) * sig
    log_f = jnp.log(jnp.maximum(fg, F_MIN))
    key = (1.0 - lb) * nsig
    qs, _ = _silu_and_grad(q_ref[rows, :])
    return qs, key, log_f, sig, fg


HG_UNROLL_TERMS = 32
HG_UNROLL_FWD = 32
HG_UNROLL_BWD = 32
HG_HALF = HG_CHUNK // 2
HG_STACK = HG_CHUNK * HG_HALF
assert HG_HALF == SUBLANES


def _half_of(x, s):
    return x[:HG_HALF, :] if s < HG_HALF else x[HG_HALF:, :]


def _hg_decay(g_ref, r0, g_rows, first_row, s):
    rid = lax.broadcasted_iota(jnp.int32, g_rows.shape, 0) + first_row
    gs = g_ref[pl.ds(r0 + s, 1), :]
    return jnp.where(rid >= s, jnp.exp(g_rows - gs), 0.0)


def _half_start(s):
    return 0 if s < HG_HALF else HG_HALF


def _hg_cross_decays(gc):
    rid = lax.broadcasted_iota(jnp.int32, (HG_CHUNK, HG_DK), 0)
    g_mid = gc[HG_HALF - 1:HG_HALF, :]
    e_hi = jnp.where(rid >= HG_HALF, jnp.exp(gc - g_mid), 0.0)
    e_lo = jnp.where(rid < HG_HALF, jnp.exp(g_mid - gc), 0.0)
    return e_hi, e_lo


def _hg_cross(qc, kc, gc):
    e_hi, e_lo = _hg_cross_decays(gc)
    return (qc * e_hi).astype(BF16), (kc * e_lo).astype(BF16)


def _stack(slabs):
    return jnp.concatenate(slabs, axis=0).astype(BF16)


def _slab_row_sums():
    row = lax.broadcasted_iota(jnp.int32, (HG_CHUNK, HG_STACK), 0)
    col = lax.broadcasted_iota(jnp.int32, (HG_CHUNK, HG_STACK), 1)
    lo = row * HG_HALF
    return jnp.where((col >= lo) & (col < lo + HG_HALF), 1.0, 0.0).astype(BF16)


def _for_chunks(n, unroll, *stages):
    unroll = min(unroll, n)
    assert n % unroll == 0

    def trip(i, _):
        chunks = [i * unroll + u for u in range(unroll)]
        carried = [stages[0](c) for c in chunks]
        for stage in stages[1:]:
            carried = [stage(c, x) for c, x in zip(chunks, carried)]
        return 0

    lax.fori_loop(0, n // unroll, trip, 0)


def _hg_state_chain(states, g_ref, carry_ref, n_chunks, reverse):
    unroll = min(8, n_chunks)
    assert n_chunks % unroll == 0
    carry_ref[...] = jnp.zeros_like(carry_ref)

    def trip(i, _):
        st = carry_ref[...]
        for u in range(unroll):
            k = i * unroll + u
            c = n_chunks - 1 - k if reverse else k
            term = states[c]
            states[c] = st
            st = st * jnp.exp(g_ref[pl.ds(c * HG_CHUNK + HG_CHUNK - 1, 1), :]) + term
        carry_ref[...] = st
        return 0

    lax.fori_loop(0, n_chunks // unroll, trip, 0)


def _hgrn_fwd(proj, lower_bound, hg_gain, seq, carry=None):
    _, t_rows, d = proj.shape
    n_seq, n_head = t_rows // seq, d // HG_DK
    ch = _row_tile(seq, ROW_CHUNK)
    n_chunks = seq // HG_CHUNK

    def body(q_ref, z_ref, v_ref, lb_ref, gain_ref, o_ref, on_ref, qs_s, k_s, g_s, states, st_ref):
        lb = lb_ref[...]
        tri = _group_cumsum_matrix(ch, HG_CHUNK)

        per_block = ch // HG_CHUNK

        def prepare(c):
            rows = pl.ds(pl.multiple_of(c * ch, ch), ch)
            qs, key, log_f, _, _ = _hg_prepare(q_ref, z_ref, lb, rows)
            qs_s[rows, :] = qs
            k_s[rows, :] = key
            return key, _group_cumsum_mxu(log_f, tri)

        def state_terms(c, carried):
            key, g = carried
            r0 = pl.multiple_of(c * ch, ch)
            g_s[pl.ds(r0, ch), :] = g
            terms = []
            for u in range(per_block):
                sl = slice(u * HG_CHUNK, (u + 1) * HG_CHUNK)
                k_end = key[sl, :] * jnp.exp(g[(u + 1) * HG_CHUNK - 1:(u + 1) * HG_CHUNK, :] - g[sl, :])
                vc = v_ref[pl.ds(r0 + u * HG_CHUNK, HG_CHUNK), :]
                terms.append(_dot_tn(vc.astype(BF16), k_end.astype(BF16)))
            return terms

        def store_terms(c, terms):
            for u, term in enumerate(terms):
                states[c * per_block + u] = term

        _for_chunks(seq // ch, 2, prepare, state_terms, store_terms)
        _hg_state_chain(states, g_s, st_ref, n_chunks, reverse=False)
        ones = jnp.ones((HG_DK, HG_DK), BF16)

        def issue(c):
            r0 = pl.multiple_of(c * HG_CHUNK, HG_CHUNK)
            rows = pl.ds(r0, HG_CHUNK)
            qc, gc = qs_s[rows, :], g_s[rows, :]
            o = _dot_nt((qc * jnp.exp(gc)).astype(BF16), states[c].astype(BF16))
            pairs = [qc[_half_start(s):, :] * _hg_decay(g_s, r0, gc[_half_start(s):, :], _half_start(s), s)
                     * k_s[pl.ds(r0 + s, 1), :] for s in range(HG_CHUNK)]
            score = _dot(_stack(pairs), ones)
            return o, score

        def combine(c, issued):
            o, score = issued
            r0 = pl.multiple_of(c * HG_CHUNK, HG_CHUNK)
            halves = [o[:HG_HALF, :], o[HG_HALF:, :]]
            first = 0
            for s in range(HG_CHUNK):
                vs = v_ref[pl.ds(r0 + s, 1), :]
                if s < HG_HALF:
                    halves[0] += score[first:first + HG_HALF, :] * vs
                    first += HG_HALF
                halves[1] += score[first:first + HG_HALF, :] * vs
                first += HG_HALF
            o = jnp.concatenate(halves, axis=0)
            o_ref[pl.ds(r0, HG_CHUNK), :] = o
            on_ref[pl.ds(r0, HG_CHUNK), :] = (o * _rstd(o) * gain_ref[...]).astype(BF16)

        _for_chunks(n_chunks, HG_UNROLL_FWD, issue, combine)

    seg = lambda k: pl.BlockSpec((None, seq, HG_DK), lambda s, h, k=k: (k, s, h))
    blk = pl.BlockSpec((seq, HG_DK), lambda s, h: (s, h))
    full = pltpu.VMEM((seq, HG_DK), F32)
    return _call_carrying(
        body, carry, name="hgrn_fwd", grid=(n_seq, n_head),
        in_specs=[seg(2), seg(3), seg(4), pl.BlockSpec((1, HG_DK), lambda s, h: (0, h)),
                  pl.BlockSpec((1, HG_DK), lambda s, h: (0, 0))],
        out_specs=[blk, blk],
        out_shape=[jax.ShapeDtypeStruct((t_rows, d), F32), jax.ShapeDtypeStruct((t_rows, d), BF16)],
        scratch_shapes=[full, full, full, pltpu.VMEM((n_chunks, HG_DK, HG_DK), F32),
                        pltpu.VMEM((HG_DK, HG_DK), F32)],
        semantics=("parallel", "parallel"), args=(proj, proj, proj, lower_bound, hg_gain))


def _hgrn_bwd(proj, lower_bound, hg_gain, o, d_on, seq, carry=None):
    _, t_rows, d = proj.shape
    n_seq, n_head = t_rows // seq, d // HG_DK
    ch = _row_tile(seq, ROW_CHUNK)
    n_chunks = seq // HG_CHUNK
    cc = HG_CHUNK

    def body(q_ref, z_ref, v_ref, lb_ref, gain_ref, o_ref, don_ref, dp_ref, dlb_ref, dgain_ref,
             qs_s, k_s, g_s, do_s, dqs_s, dk_s, dlf_s, states, dstates, carry_ref):
        hh, ss = pl.program_id(0), pl.program_id(1)
        lb = lb_ref[...]

        @pl.when(ss == 0)
        def _():
            dlb_ref[...] = jnp.zeros_like(dlb_ref)

        @pl.when((ss == 0) & (hh == 0))
        def _():
            dgain_ref[...] = jnp.zeros_like(dgain_ref)

        tri = _group_cumsum_matrix(ch, cc)

        def prepare(c):
            rows = pl.ds(pl.multiple_of(c * ch, ch), ch)
            qs, key, log_f, _, _ = _hg_prepare(q_ref, z_ref, lb, rows)
            qs_s[rows, :] = qs
            k_s[rows, :] = key
            do, dgain = _rms_bwd(don_ref[rows, :], o_ref[rows, :], gain_ref[...])
            do_s[rows, :] = do
            dgain_ref[...] += dgain
            return _group_cumsum_mxu(log_f, tri)

        def store_cumsum(c, g):
            g_s[pl.ds(pl.multiple_of(c * ch, ch), ch), :] = g

        _for_chunks(seq // ch, 4, prepare, store_cumsum)

        def chain_terms(c):
            rows = pl.ds(pl.multiple_of(c * cc, cc), cc)
            gc = g_s[rows, :]
            k_end = k_s[rows, :] * jnp.exp(gc[cc - 1:cc, :] - gc)
            q_in = qs_s[rows, :] * jnp.exp(gc)
            return (_dot_tn(v_ref[rows, :].astype(BF16), k_end.astype(BF16)),
                    _dot_tn(do_s[rows, :].astype(BF16), q_in.astype(BF16)))

        def store_terms(c, terms):
            states[c], dstates[c] = terms

        _for_chunks(n_chunks, HG_UNROLL_TERMS, chain_terms, store_terms)
        _hg_state_chain(states, g_s, carry_ref, n_chunks, reverse=False)
        _hg_state_chain(dstates, g_s, carry_ref, n_chunks, reverse=True)
        ones = jnp.ones((HG_DK, HG_DK), BF16)
        row_sums = _slab_row_sums()

        def chunk_rows(c):
            r0 = pl.multiple_of(c * cc, cc)
            return r0, pl.ds(r0, cc)

        def through_state(c):
            r0, rows = chunk_rows(c)
            qc, kc, gc, vc, doc = qs_s[rows, :], k_s[rows, :], g_s[rows, :], v_ref[rows, :], do_s[rows, :]
            st, dst = states[c], dstates[c]
            g_last = gc[cc - 1:cc, :]
            e_last, e_end = jnp.exp(g_last), jnp.exp(g_last - gc)
            dob, vcb, dstb = doc.astype(BF16), vc.astype(BF16), dst.astype(BF16)
            dqs = _dot(dob, st.astype(BF16))
            dk_state = _dot(vcb, dstb)
            dv = _dot_nt((kc * e_end).astype(BF16), dstb)
            cots = [_half_of(doc, s) * v_ref[pl.ds(r0 + s, 1), :] for s in range(cc)]
            d_score = _dot(_stack(cots), ones)
            x, y = _hg_cross(qc, kc, gc)
            cross = (_dot_nt(dob, vcb), _dot_nt(vcb, dob), _dot_nt(y, x))
            return dqs, dk_state, dv, d_score, e_last * jnp.sum(dst * st, axis=0, keepdims=True), cross

        def pair_terms(c, carried):
            dqs, dk_state, dv, d_score, d_glast, (da_cross, da_cross_t, a_cross_t) = carried
            r0, rows = chunk_rows(c)
            qc, kc, gc = qs_s[rows, :], k_s[rows, :], g_s[rows, :]
            dqs = dqs * jnp.exp(gc)
            dk_state = dk_state * jnp.exp(gc[cc - 1:cc, :] - gc)
            d_glast = d_glast + jnp.sum(kc * dk_state, axis=0, keepdims=True)
            dqs_half = [dqs[:HG_HALF, :], dqs[HG_HALF:, :]]
            pairs, dk_terms = [], []
            for s in range(cc):
                qv = _half_of(qc, s)
                decay = _hg_decay(g_s, r0, _half_of(gc, s), _half_start(s), s)
                ks = k_s[pl.ds(r0 + s, 1), :]
                da_decay = d_score[s * HG_HALF:(s + 1) * HG_HALF, :] * decay
                pairs.append(qv * decay * ks)
                dk_terms.append(da_decay * qv)
                dqs_half[s // HG_HALF] += da_decay * ks
            score = _dot(_stack(pairs), ones)
            dk = dk_state + _dot(row_sums, _stack(dk_terms))
            x, y = _hg_cross(qc, kc, gc)
            cross = (_dot(da_cross.astype(BF16), y), _dot(da_cross_t.astype(BF16), x),
                     _dot(a_cross_t.astype(BF16), do_s[rows, :].astype(BF16)))
            return jnp.concatenate(dqs_half, axis=0), dk, dv, score, d_glast, cross

        def value_terms(c, carried):
            dqs, dk, dv, score, d_glast, (dx_cross, dy_cross, dv_cross) = carried
            _, rows = chunk_rows(c)
            doc, gc = do_s[rows, :], g_s[rows, :]
            dv_terms = [score[s * HG_HALF:(s + 1) * HG_HALF, :] * _half_of(doc, s) for s in range(cc)]
            e_hi, e_lo = _hg_cross_decays(gc)
            return (dqs + dx_cross * e_hi, dk + dy_cross * e_lo,
                    dv + dv_cross + _dot(row_sums, _stack(dv_terms)), d_glast)

        def store(c, x):
            dqs, dk, dv, d_glast = x
            _, rows = chunk_rows(c)
            d_g = qs_s[rows, :] * dqs - k_s[rows, :] * dk
            dlf_s[rows, :] = _seg_cumsum(d_g, cc, reverse=True) + d_glast
            dqs_s[rows, :] = dqs
            dk_s[rows, :] = dk
            dp_ref[2, rows, :] = dv.astype(BF16)

        _for_chunks(n_chunks, HG_UNROLL_BWD, through_state, pair_terms, value_terms, store)

        def finish(c, _):
            rows = pl.ds(pl.multiple_of(c * ch, ch), ch)
            sig, nsig = _sigmoid_pair(z_ref[rows, :])
            fg = lb + (1.0 - lb) * sig
            _, dsilu = _silu_and_grad(q_ref[rows, :])
            dp_ref[0, rows, :] = (dqs_s[rows, :] * dsilu).astype(BF16)
            dfg = jnp.where(fg > F_MIN, dlf_s[rows, :] / fg, 0.0)
            dk = dk_s[rows, :]
            dp_ref[1, rows, :] = ((dfg - dk) * (1.0 - lb) * sig * nsig).astype(BF16)
            dlb_ref[...] += jnp.sum((dfg - dk) * nsig, axis=0, keepdims=True)
            return 0

        lax.fori_loop(0, seq // ch, finish, 0)

    seg = lambda k: pl.BlockSpec((None, seq, HG_DK), lambda h, s, k=k: (k, s, h))
    blk = pl.BlockSpec((seq, HG_DK), lambda h, s: (s, h))
    full = pltpu.VMEM((seq, HG_DK), F32)
    return _call_carrying(
        body, carry, name="hgrn_bwd", grid=(n_head, n_seq),
        in_specs=[seg(2), seg(3), seg(4), pl.BlockSpec((1, HG_DK), lambda h, s: (0, h)),
                  pl.BlockSpec((1, HG_DK), lambda h, s: (0, 0)), blk, blk],
        out_specs=[pl.BlockSpec((3, seq, HG_DK), lambda h, s: (0, s, h)),
                   pl.BlockSpec((1, HG_DK), lambda h, s: (0, h)),
                   pl.BlockSpec((1, HG_DK), lambda h, s: (0, 0))],
        out_shape=[jax.ShapeDtypeStruct((3, t_rows, d), BF16), jax.ShapeDtypeStruct((1, d), F32),
                   jax.ShapeDtypeStruct((1, HG_DK), F32)],
        scratch_shapes=[full, full, full, full, full, full, full,
                        pltpu.VMEM((n_chunks, HG_DK, HG_DK), F32), pltpu.VMEM((n_chunks, HG_DK, HG_DK), F32),
                        pltpu.VMEM((HG_DK, HG_DK), F32)],
        semantics=("arbitrary", "arbitrary"), args=(proj, proj, proj, lower_bound, hg_gain, o, d_on))


def _mesh_place():
    x, y, c = lax.axis_index("x"), lax.axis_index("y"), lax.axis_index("c")
    return x, y, c


def _peer(place, k):
    x, y, c = place
    px = 1 - x if k & 4 else x
    py = 1 - y if k & 2 else y
    pc = 1 - c if k & 1 else c
    return (px, py, pc), 4 * px + 2 * py + pc


class _Exchange:
    def __init__(self, srcs, gather):
        self.n = len(srcs)
        self.gather = gather
        self.out_shape = [jax.ShapeDtypeStruct((N_DEV,) + tuple(s.shape if gather else s.shape[1:]), s.dtype)
                          for s in srcs]
        self.scratch = [pltpu.SemaphoreType.DMA((self.n * (N_DEV - 1),)),
                        pltpu.SemaphoreType.DMA((self.n * (N_DEV - 1),)),
                        pltpu.SemaphoreType.DMA((self.n,))]

    def _copies(self, src_refs, out_refs, sems):
        send_sems, recv_sems, local_sems = sems
        place = _mesh_place()
        me = 4 * place[0] + 2 * place[1] + place[2]
        local, sends, recvs = [], [], []
        for a, (src, out) in enumerate(zip(src_refs, out_refs)):
            outgoing = (lambda idx, src=src: src) if self.gather else (lambda idx, src=src: src.at[idx])
            local.append(pltpu.make_async_copy(outgoing(me), out.at[me], local_sems.at[a]))
            for k in range(1, N_DEV):
                peer, peer_idx = _peer(place, k)
                sem = a * (N_DEV - 1) + k - 1
                sends.append(pltpu.make_async_remote_copy(
                    src_ref=outgoing(peer_idx), dst_ref=out.at[me], send_sem=send_sems.at[sem],
                    recv_sem=recv_sems.at[sem], device_id=peer, device_id_type=MESH_ID))
                recvs.append(pltpu.make_async_remote_copy(
                    src_ref=outgoing(peer_idx), dst_ref=out.at[peer_idx], send_sem=send_sems.at[sem],
                    recv_sem=recv_sems.at[sem], device_id=peer, device_id_type=MESH_ID))
        return local, sends, recvs

    def start(self, src_refs, out_refs, sems):
        local, sends, _ = self._copies(src_refs, out_refs, sems)
        for cp in local + sends:
            cp.start()

    def wait(self, src_refs, out_refs, sems):
        local, sends, recvs = self._copies(src_refs, out_refs, sems)
        for cp in recvs:
            cp.wait_recv()
        for cp in sends:
            cp.wait_send()
        for cp in local:
            cp.wait()


def _call_carrying(body, carry, *, name, grid, in_specs, out_specs, out_shape, scratch_shapes, semantics, args):
    if carry is None:
        outs = pl.pallas_call(body, name=name, grid=grid, in_specs=in_specs, out_specs=out_specs,
                              out_shape=out_shape, scratch_shapes=scratch_shapes,
                              compiler_params=_cp(*semantics))(*args)
        return outs, []
    srcs, gather = carry
    ex = _Exchange(srcs, gather)
    n, n_in, n_out, n_scr = ex.n, len(in_specs), len(out_specs), len(scratch_shapes)

    def wrapped(*refs):
        ins, refs = refs[:n_in], refs[n_in:]
        src_refs, refs = refs[:n], refs[n:]
        outs, refs = refs[:n_out], refs[n_out:]
        dst_refs, refs = refs[:n], refs[n:]
        scratch, sems = refs[:n_scr], refs[n_scr:]
        first, last = None, None
        for axis, size in enumerate(grid):
            i = pl.program_id(axis)
            first = (i == 0) if first is None else first & (i == 0)
            last = (i == size - 1) if last is None else last & (i == size - 1)

        @pl.when(first)
        def _():
            ex.start(src_refs, dst_refs, sems)

        body(*ins, *outs, *scratch)

        @pl.when(last)
        def _():
            ex.wait(src_refs, dst_refs, sems)

    any_space = pl.BlockSpec(memory_space=pl.ANY)
    res = pl.pallas_call(
        wrapped, name=name + "_carrying", grid=grid, in_specs=list(in_specs) + [any_space] * n,
        out_specs=list(out_specs) + [any_space] * n, out_shape=list(out_shape) + ex.out_shape,
        scratch_shapes=list(scratch_shapes) + ex.scratch,
        compiler_params=_cp(*(["arbitrary"] * len(grid))))(*args, *srcs)
    return res[:n_out], res[n_out:]


def _exchange(srcs, name, gather):
    ex = _Exchange(srcs, gather)
    n = ex.n

    def body(*refs):
        src_refs, out_refs, sems = refs[:n], refs[n:2 * n], refs[2 * n:]
        ex.start(src_refs, out_refs, sems)
        ex.wait(src_refs, out_refs, sems)

    any_space = pl.BlockSpec(memory_space=pl.ANY)
    return pl.pallas_call(
        body, name=name, in_specs=[any_space] * n, out_specs=[any_space] * n,
        out_shape=ex.out_shape, scratch_shapes=ex.scratch)(*srcs)


def _reduce_adamw(parts, w, m, v, name):
    rows, cols = w.shape
    n_seg = len(parts)
    seg_rows = rows // n_seg
    tr = _row_tile(seg_rows, 128)
    per_seg = seg_rows // tr
    c1 = np.float32(1.0 - ADAM_B1 ** ADAM_STEP)
    c2 = np.float32(1.0 - ADAM_B2 ** ADAM_STEP)

    def body(*refs):
        p_refs = refs[:n_seg]
        w_ref, m_ref, v_ref, g_ref, d_ref, nm_ref, nv_ref = refs[n_seg:]
        seg = pl.program_id(0)
        for k, p_ref in enumerate(p_refs):
            @pl.when(seg == k)
            def _(p_ref=p_ref):
                g = p_ref[0].astype(F32)
                for dev in range(1, N_DEV):
                    g = g + p_ref[dev].astype(F32)
                g_ref[...] = g

        g = g_ref[...]
        nm = ADAM_B1 * m_ref[...] + (1.0 - ADAM_B1) * g
        nv = ADAM_B2 * v_ref[...] + (1.0 - ADAM_B2) * (g * g)
        nm_ref[...] = nm
        nv_ref[...] = nv
        d_ref[...] = -ADAM_LR * ((nm / c1) / (jnp.sqrt(nv / c2) + ADAM_EPS) + ADAM_WD * w_ref[...])

    def part_spec(k):
        return pl.BlockSpec((N_DEV, tr, cols), lambda s, i, k=k: (0, jnp.where(s == k, i, 0), 0))

    blk = pl.BlockSpec((tr, cols), lambda s, i: (s * per_seg + i, 0))
    shp = jax.ShapeDtypeStruct((rows, cols), F32)
    return pl.pallas_call(
        body, name=name, grid=(n_seg, per_seg),
        in_specs=[part_spec(k) for k in range(n_seg)] + [blk, blk, blk],
        out_specs=[blk, blk, blk, blk], out_shape=[shp, shp, shp, shp],
        compiler_params=_cp("arbitrary", "arbitrary"))(*parts, w, m, v)


def _pack(arrays, lead=0):
    parts = []
    for a in arrays:
        f = a.reshape(a.shape[:lead] + (-1, LANES))
        pad = -f.shape[lead] % PACK_ROWS
        if pad:
            f = jnp.pad(f, [(0, 0)] * lead + [(0, pad), (0, 0)])
        parts.append(f)
    return jnp.concatenate(parts, axis=lead)


def _unpack(buf, shapes, lead=0):
    out, r = [], 0
    for shp in shapes:
        n = int(np.prod(shp)) // LANES
        part = lax.slice_in_dim(buf, r, r + n, axis=lead)
        out.append(part.reshape(buf.shape[:lead] + tuple(shp)))
        r += n + (-n % PACK_ROWS)
    return out


REPLICATED = ("lb_logits", "norm_mix", "conv_b", "b_r", "b_i", "lam", "hg_norm", "norm_mlp", "norm_final")
SMALL_SHARDED = ("conv_w", "w_r", "w_i")
LARGE_SHARDED = ("w_in", "w_out", "w_up", "w_down")
WEIGHTS = ("lb_logits", "norm_mix", "w_in", "conv_w", "conv_b", "w_r", "b_r", "w_i", "b_i", "lam", "hg_norm",
           "w_out", "norm_mlp", "w_up", "w_down", "norm_final")


def _matmul_weight_shards(p):
    depth = p["w_in"].shape[0]
    cast = {k: p[k].astype(BF16) for k in LARGE_SHARDED}
    return ([cast["w_in"][l] for l in range(depth)],
            [[cast[k][l] for k in ("w_out", "w_up", "w_down")] for l in range(depth)])


def _gathered_rest(got):
    w_out, w_up, w_down = got
    d = w_out.shape[2]
    return dict(w_out=w_out.reshape(d, d), w_up=w_up, w_down=w_down)


def _unpack_mixer_weights(small, p):
    depth, d, _ = p["w_in"].shape
    n_blk = d // RG_BLOCK_W
    conv_w, w_r, w_i = _unpack(small, [p["conv_w"].shape, p["w_r"].shape, p["w_i"].shape], lead=1)
    conv_w = conv_w.transpose(1, 2, 0, 3).reshape(depth, CONV_W, d)
    w_r = w_r.transpose(1, 2, 0, 3, 4).reshape(depth, n_blk, RG_BLOCK_W, RG_BLOCK_W).astype(BF16)
    w_i = w_i.transpose(1, 2, 0, 3, 4).reshape(depth, n_blk, RG_BLOCK_W, RG_BLOCK_W).astype(BF16)
    return conv_w, w_r, w_i


def _local_step(x, target, p):
    bl, seq, d = x.shape
    depth = p["w_in"].shape[0]
    t_rows = bl * seq
    row = lambda a, l: a[l:l + 1]
    lbs = _lower_bounds_fwd(p["lb_logits"])
    shard_in, shard_rest = _matmul_weight_shards(p)
    w_in = _exchange([shard_in[0]], "gather_w_in", gather=True)[0]
    cur = x.reshape(t_rows, d)
    saved, layers = [], []
    for l in range(depth):
        if l == 0:
            (proj, gates, h), small = _inproj_fwd(cur, row(p["norm_mix"], l), w_in,
                                                  carry=([_pack([p["conv_w"], p["w_r"], p["w_i"]])], True))
            conv_w, w_r, w_i = _unpack_mixer_weights(small[0], p)
        else:
            (proj, gates, h), _ = _inproj_fwd(cur, row(p["norm_mix"], l), w_in)
        w = dict(w_in=w_in, conv_w=conv_w[l], w_r=w_r[l], w_i=w_i[l])
        hs, y_a = _mixer_a_fwd(proj, w["conv_w"], row(p["conv_b"], l), w["w_r"], row(p["b_r"], l), w["w_i"],
                               row(p["b_i"], l), row(p["lam"], l), seq)
        (o, o_n), got = _hgrn_fwd(proj, row(lbs, l), row(p["hg_norm"], l), seq,
                                  carry=(shard_rest[l] + ([shard_in[l + 1]] if l + 1 < depth else []), True))
        w.update(_gathered_rest(got[:3]))
        w_in = got[3] if l + 1 < depth else None
        layers.append(w)
        x_mid, y = _merge_out_fwd(gates, y_a, o_n, cur, w["w_out"])
        x_out, u, h2 = _mlp_fwd(x_mid, row(p["norm_mlp"], l), w["w_up"], w["w_down"])
        saved.append(dict(x_in=cur, proj=proj, gates=gates, h=h, hs=hs, y_a=y_a, o=o, o_n=o_n, x_mid=x_mid, y=y, u=u, h2=h2))
        cur = x_out
    loss8, dx, dxb, g_norm_final = _loss_head(cur, p["norm_final"].reshape(1, d), target.reshape(t_rows, d))
    small = ("norm_mix", "conv_w", "conv_b", "w_r", "b_r", "w_i", "b_i", "lam", "hg_norm", "norm_mlp")
    g = {k: [None] * depth for k in small}
    d_lbs, received = [None] * depth, [None] * depth
    g_w_in = None
    for l in reversed(range(depth)):
        s, w = saved[l], layers[l]
        dx_mid, dx_mid_b, du, act, g["norm_mlp"][l] = _mlp_bwd(dx, dxb, s["u"], s["x_mid"], row(p["norm_mlp"], l),
                                                               w["w_up"], w["w_down"])
        g_w_down = _wgrad(act, dxb[None], "wgrad_down")
        g_w_up = _wgrad(s["h2"][None], du, "wgrad_up")
        d_ya, d_on, dp_c = _outproj_bwd(dx_mid_b, w["w_out"], s["gates"], s["y_a"], s["o_n"])
        g_w_out = _wgrad(s["y"][None], dx_mid_b[None], "wgrad_out").reshape(N_DEV, d // N_DEV, d)
        (dp_b, d_lbs[l], g["hg_norm"][l]), got = _hgrn_bwd(
            s["proj"], row(lbs, l), row(p["hg_norm"], l), s["o"], d_on, seq,
            carry=([g_w_out, g_w_up, g_w_down] + ([g_w_in] if g_w_in is not None else []), False))
        received[l] = [None] + list(got[:3])
        if g_w_in is not None:
            received[l + 1][0] = got[3]
        (dp_a, g["w_r"][l], g["w_i"][l], g["b_r"][l], g["b_i"][l], g["lam"][l], g["conv_w"][l],
         g["conv_b"][l]) = _mixer_a_bwd(s["proj"], s["hs"], d_ya, w["conv_w"], row(p["conv_b"], l), w["w_r"],
                                        row(p["b_r"], l), w["w_i"], row(p["b_i"], l), row(p["lam"], l), seq)
        hb = s["h"][None]
        g_w_in = jnp.concatenate([_wgrad(hb, dp_a, "wgrad_in_pair"), _wgrad(hb, dp_b, "wgrad_in_triple"),
                                  _wgrad(hb, dp_c, "wgrad_in_triple")], axis=0)
        carry = None
        if l == 0:
            carry = ([g_w_in, _mixer_grads_by_owner(g, d)], False)
        (dx, dxb, g["norm_mix"][l]), got = _inproj_bwd(dx_mid, dp_a, dp_b, dp_c, w["w_in"], s["x_in"],
                                                       row(p["norm_mix"], l), carry=carry)
    received[0][0], received_mixer = got
    grads = {k: jnp.stack(v) for k, v in g.items()}
    for k in ("norm_mix", "conv_b", "b_r", "b_i", "lam", "hg_norm", "norm_mlp"):
        grads[k] = grads[k][:, 0]
    grads["lb_logits"] = _lower_bounds_bwd(p["lb_logits"], jnp.concatenate(d_lbs, axis=0))
    grads["norm_final"] = g_norm_final[0]
    return loss8[0, 0], dx.reshape(bl, seq, d), grads, received, received_mixer


def _mixer_grads_by_owner(g, d):
    d8, n_blk, rb = d // N_DEV, d // RG_BLOCK_W, RG_BLOCK_W // N_DEV
    depth = len(g["conv_w"])
    conv_w, w_r, w_i = (jnp.stack(g[k]) for k in SMALL_SHARDED)
    return _pack([conv_w.reshape(depth, CONV_W, N_DEV, d8).transpose(2, 0, 1, 3),
                  w_r.reshape(depth, n_blk, N_DEV, rb, RG_BLOCK_W).transpose(2, 0, 1, 3, 4),
                  w_i.reshape(depth, n_blk, N_DEV, rb, RG_BLOCK_W).transpose(2, 0, 1, 3, 4)],
                 lead=1).astype(BF16)


def _update(p, mom1, mom2, grads, received, received_mixer):
    depth = p["w_in"].shape[0]
    out = {}

    for i, k in enumerate(LARGE_SHARDED):
        shp = p[k].shape
        flat = lambda a: a.reshape(shp[0] * shp[1], shp[2])
        parts = [received[l][i] for l in range(depth)]
        res = _reduce_adamw(parts, flat(p[k]), flat(mom1[k]), flat(mom2[k]), "adamw_" + k)
        out[k] = [r.reshape(shp) for r in res]

    res = _reduce_adamw([received_mixer], *[_pack([src[k] for k in SMALL_SHARDED]) for src in (p, mom1, mom2)],
                        "adamw_mixer")
    shapes = [p[k].shape for k in SMALL_SHARDED]
    for i, vals in enumerate(zip(*[_unpack(r, shapes) for r in res])):
        out[SMALL_SHARDED[i]] = list(vals)

    parts = _exchange([_pack([grads[k] for k in REPLICATED])], "gather_grad_replicated", gather=True)
    res = _reduce_adamw(parts, *[_pack([src[k] for k in REPLICATED]) for src in (p, mom1, mom2)],
                        "adamw_replicated")
    shapes = [p[k].shape for k in REPLICATED]
    for i, vals in enumerate(zip(*[_unpack(r, shapes) for r in res])):
        out[REPLICATED[i]] = list(vals)

    return tuple(out[k][i] for i in range(4) for k in WEIGHTS)


def kernel(x, lb_logits, norm_mix, w_in, conv_w, conv_b, w_r, b_r, w_i, b_i, lam, hg_norm, w_out, norm_mlp, w_up, w_down, norm_final, loss_target, m_lb_logits, m_norm_mix, m_w_in, m_conv_w, m_conv_b, m_w_r, m_b_r, m_w_i, m_b_i, m_lam, m_hg_norm, m_w_out, m_norm_mlp, m_w_up, m_w_down, m_norm_final, v_lb_logits, v_norm_mix, v_w_in, v_conv_w, v_conv_b, v_w_r, v_b_r, v_w_i, v_b_i, v_lam, v_hg_norm, v_w_out, v_norm_mlp, v_w_up, v_w_down, v_norm_final):
    p = dict(lb_logits=lb_logits, norm_mix=norm_mix, w_in=w_in, conv_w=conv_w, conv_b=conv_b, w_r=w_r, b_r=b_r,
             w_i=w_i, b_i=b_i, lam=lam, hg_norm=hg_norm, w_out=w_out, norm_mlp=norm_mlp, w_up=w_up,
             w_down=w_down, norm_final=norm_final)
    mom1 = dict(lb_logits=m_lb_logits, norm_mix=m_norm_mix, w_in=m_w_in, conv_w=m_conv_w, conv_b=m_conv_b,
                w_r=m_w_r, b_r=m_b_r, w_i=m_w_i, b_i=m_b_i, lam=m_lam, hg_norm=m_hg_norm, w_out=m_w_out,
                norm_mlp=m_norm_mlp, w_up=m_w_up, w_down=m_w_down, norm_final=m_norm_final)
    mom2 = dict(lb_logits=v_lb_logits, norm_mix=v_norm_mix, w_in=v_w_in, conv_w=v_conv_w, conv_b=v_conv_b,
                w_r=v_w_r, b_r=v_b_r, w_i=v_w_i, b_i=v_b_i, lam=v_lam, hg_norm=v_hg_norm, w_out=v_w_out,
                norm_mlp=v_norm_mlp, w_up=v_w_up, w_down=v_w_down, norm_final=v_norm_final)
    loss, grad_x, grads, received, received_mixer = _local_step(x, loss_target, p)
    loss = lax.psum(loss, ("x", "y", "c"))
    return (loss, grad_x) + _update(p, mom1, mom2, grads, received, received_mixer)
```

```python
import numpy as np

import jax
import jax.numpy as jnp
from jax import lax
from jax.experimental import pallas as pl
from jax.experimental.pallas import tpu as pltpu

F32 = jnp.float32
BF16 = jnp.bfloat16
MESH_ID = pl.DeviceIdType.MESH

N_DEV = 8
N_MIXER_SEGMENTS = 5
NORM_EPS = 1e-6
RG_C = 8.0
RG_BLOCK_W = 256
CONV_W = 4
HG_DK = 128
F_MIN = 1e-30
HG_CHUNK = 16
SUBLANES = 8
LANES = 128
PACK_ROWS = 16
SCAN_GROUP = 16
ROW_CHUNK = 256
ROW_TILE_WEIGHT_STREAM = 1024
WGRAD_TOKEN_TILE = 2048
VMEM_LIMIT_V7X = 56 * 1024 * 1024

ADAM_LR = 0.001
ADAM_B1 = 0.9
ADAM_B2 = 0.999
ADAM_EPS = 1e-08
ADAM_WD = 0.01
ADAM_STEP = 10

GELU_C = 0.7978845608028654
GELU_K = 0.044715


def _cp(*sem):
    return pltpu.CompilerParams(dimension_semantics=sem, vmem_limit_bytes=VMEM_LIMIT_V7X)


def _row_tile(n, cap):
    if n <= cap:
        return n
    t = cap - cap % 16
    while n % t:
        t -= 16
    return t


def _dot(a, b):
    return jnp.dot(a, b, preferred_element_type=F32)


def _dot_nt(a, b):
    return lax.dot_general(a, b, (((1,), (1,)), ((), ())), preferred_element_type=F32)


def _dot_tn(a, b):
    return lax.dot_general(a, b, (((0,), (0,)), ((), ())), preferred_element_type=F32)


def _sigmoid(x):
    return jax.nn.sigmoid(x)


def _sigmoid_pair(x):
    e = jnp.exp(-jnp.abs(x))
    r = 1.0 / (1.0 + e)
    er = e * r
    pos = x >= 0.0
    return jnp.where(pos, r, er), jnp.where(pos, er, r)


def _log1p_pos(y):
    return jnp.where(y < 0.01, y * (1.0 - y * (0.5 - y * (1.0 / 3.0))), jnp.log(1.0 + y))


def _softplus(x):
    return jnp.maximum(x, 0.0) + _log1p_pos(jnp.exp(-jnp.abs(x)))


def _one_minus_exp(x):
    series = -x * (1.0 + x * 0.5 * (1.0 + x * (1.0 / 3.0) * (1.0 + x * 0.25 * (1.0 + x * 0.2))))
    return jnp.where(x > -0.1, series, 1.0 - jnp.exp(x))


def _gelu_and_grad(x):
    x2 = x * x
    t = jnp.tanh(GELU_C * x * (1.0 + GELU_K * x2))
    g = 0.5 * x * (1.0 + t)
    dg = 0.5 * (1.0 + t) + 0.5 * x * (1.0 - t * t) * GELU_C * (1.0 + 3.0 * GELU_K * x2)
    return g, dg


def _silu_and_grad(x):
    s = _sigmoid(x)
    return x * s, s * (1.0 + x * (1.0 - s))


def _rstd(x):
    return lax.rsqrt(jnp.mean(x * x, axis=-1, keepdims=True) + NORM_EPS)


def _rms_bwd(dh, x, g):
    rstd = _rstd(x)
    xh = x * rstd
    dxh = dh * g
    dx = rstd * (dxh - xh * jnp.mean(dxh * xh, axis=-1, keepdims=True))
    return dx, jnp.sum(dh * xh, axis=0, keepdims=True)


def _shift_rows(x, k):
    n = x.shape[0]
    k = k % n
    return x if k == 0 else pltpu.roll(x, k, axis=0)


def _seg_cumsum(x, seg, reverse=False):
    n = x.shape[0]
    rid = lax.broadcasted_iota(jnp.int32, x.shape, 0) & (seg - 1)
    d = 1
    while d < seg:
        if reverse:
            x = jnp.where(rid < seg - d, x + _shift_rows(x, n - d), x)
        else:
            x = jnp.where(rid >= d, x + _shift_rows(x, d), x)
        d *= 2
    return x


def _group_cumsum_matrix(n, seg):
    row = lax.broadcasted_iota(jnp.int32, (n, n), 0)
    col = lax.broadcasted_iota(jnp.int32, (n, n), 1)
    same_group = (row & ~(seg - 1)) == (col & ~(seg - 1))
    return jnp.where(same_group & (col <= row), 1.0, 0.0).astype(BF16)


def _group_cumsum_mxu(x, tri):
    hi = x.astype(BF16)
    lo = (x - hi.astype(F32)).astype(BF16)
    return _dot(tri, hi) + _dot(tri, lo)


def _scan_rows(a_ref, b_ref, out_ref, n_rows, width, reverse):
    gr = min(SCAN_GROUP, n_rows)
    rid = lax.broadcasted_iota(jnp.int32, (gr, width), 0)
    n_groups = n_rows // gr
    per_trip = min(4, n_groups)
    assert n_groups % per_trip == 0

    def local_scan(g):
        r0 = pl.multiple_of(g * gr, gr)
        a = a_ref[pl.ds(r0, gr), :]
        b = b_ref[pl.ds(r0, gr), :]
        d = 1
        while d < gr:
            if reverse:
                keep = rid < gr - d
                a_sh, b_sh = _shift_rows(a, gr - d), _shift_rows(b, gr - d)
            else:
                keep = rid >= d
                a_sh, b_sh = _shift_rows(a, d), _shift_rows(b, d)
            b = jnp.where(keep, a * b_sh + b, b)
            a = jnp.where(keep, a * a_sh, a)
            d *= 2
        return r0, a, b

    def trip(i, carry):
        first = i * per_trip
        groups = [n_groups - 1 - (first + u) if reverse else first + u for u in range(per_trip)]
        for r0, a, b in [local_scan(g) for g in groups]:
            out = a * carry + b
            out_ref[pl.ds(r0, gr), :] = out
            edge = out[0:1, :] if reverse else out[gr - 1:gr, :]
            carry = jnp.broadcast_to(edge, (gr, width))
        return carry

    lax.fori_loop(0, n_groups // per_trip, trip, jnp.zeros((gr, width), F32))


def _lb_softmax_rows(x_ref, depth):
    rows = [x_ref[pl.ds(l, 1), :] for l in range(depth)]
    top = rows[0]
    for r in rows[1:]:
        top = jnp.maximum(top, r)
    e = [jnp.exp(r - top) for r in rows]
    tot = e[0]
    for r in e[1:]:
        tot = tot + r
    return [r / tot for r in e]


def _lower_bounds_fwd(lb_logits):
    depth, d = lb_logits.shape

    def body(x_ref, o_ref):
        sm = _lb_softmax_rows(x_ref, depth)
        cum = jnp.zeros((1, d), F32)
        for l in range(depth):
            cum = cum + sm[l]
            o_ref[pl.ds(l, 1), :] = jnp.clip(cum - sm[0], 0.0, 1.0)

    return pl.pallas_call(body, name="lower_bounds_fwd",
                          out_shape=jax.ShapeDtypeStruct((depth, d), F32))(lb_logits)


def _lower_bounds_bwd(lb_logits, d_lbs):
    depth, d = lb_logits.shape

    def body(x_ref, g_ref, o_ref):
        sm = _lb_softmax_rows(x_ref, depth)
        cum = jnp.zeros((1, d), F32)
        d_cum = []
        for l in range(depth):
            cum = cum + sm[l]
            v = cum - sm[0]
            d_cum.append(jnp.where((v > 0.0) & (v < 1.0), g_ref[pl.ds(l, 1), :], 0.0))
        d_sm = []
        tail = jnp.zeros((1, d), F32)
        for l in reversed(range(depth)):
            tail = tail + d_cum[l]
            d_sm.append(tail)
        d_sm = d_sm[::-1]
        d_sm[0] = d_sm[0] - tail
        inner = jnp.zeros((1, d), F32)
        for l in range(depth):
            inner = inner + sm[l] * d_sm[l]
        for l in range(depth):
            o_ref[pl.ds(l, 1), :] = sm[l] * (d_sm[l] - inner)

    return pl.pallas_call(body, name="lower_bounds_bwd",
                          out_shape=jax.ShapeDtypeStruct((depth, d), F32))(lb_logits, d_lbs)


def _inproj_fwd(x, gain, w_seg, carry=None):
    t_rows, d = x.shape
    tm = _row_tile(t_rows, ROW_TILE_WEIGHT_STREAM)
    n_gate = N_DEV - N_MIXER_SEGMENTS

    def body(x_ref, g_ref, w_ref, proj_ref, gates_ref, h_ref):
        j = pl.program_id(1)

        @pl.when(j == 0)
        def _():
            xv = x_ref[...]
            h_ref[...] = (xv * _rstd(xv) * g_ref[...]).astype(BF16)

        @pl.when(j < N_MIXER_SEGMENTS)
        def _():
            proj_ref[...] = _dot(h_ref[...], w_ref[...])

        @pl.when(j >= N_MIXER_SEGMENTS)
        def _():
            gates_ref[...] = _dot(h_ref[...], w_ref[...]).astype(BF16)

    return _call_carrying(
        body, carry, name="inproj_fwd", grid=(t_rows // tm, N_DEV),
        in_specs=[pl.BlockSpec((tm, d), lambda i, j: (i, 0)),
                  pl.BlockSpec((1, d), lambda i, j: (0, 0)),
                  pl.BlockSpec((None, d, d), lambda i, j: (j, 0, 0))],
        out_specs=[pl.BlockSpec((None, tm, d), lambda i, j: (jnp.minimum(j, N_MIXER_SEGMENTS - 1), i, 0)),
                   pl.BlockSpec((None, tm, d), lambda i, j: (jnp.maximum(j - N_MIXER_SEGMENTS, 0), i, 0)),
                   pl.BlockSpec((tm, d), lambda i, j: (i, 0))],
        out_shape=[jax.ShapeDtypeStruct((N_MIXER_SEGMENTS, t_rows, d), F32),
                   jax.ShapeDtypeStruct((n_gate, t_rows, d), BF16),
                   jax.ShapeDtypeStruct((t_rows, d), BF16)],
        scratch_shapes=[], semantics=("parallel", "arbitrary"), args=(x, gain, w_seg))


def _merge_out_fwd(gates, y_a, o_n, x, w_out):
    t_rows, d = x.shape
    tm = _row_tile(t_rows, 256)

    def body(g_ref, ma_ref, mb_ref, ya_ref, on_ref, x_ref, w_ref, xmid_ref, y_ref):
        g = g_ref[...].astype(F32)
        ya, on = ya_ref[...].astype(F32), on_ref[...].astype(F32)
        y = (_sigmoid(ma_ref[...].astype(F32)) * ya
             + _sigmoid(mb_ref[...].astype(F32)) * (on * (g * _sigmoid(g))))
        yb = y.astype(BF16)
        y_ref[...] = yb
        xmid_ref[...] = x_ref[...] + _dot(yb, w_ref[...])

    seg = lambda k: pl.BlockSpec((None, tm, d), lambda i, k=k: (k, i, 0))
    row = pl.BlockSpec((tm, d), lambda i: (i, 0))
    return pl.pallas_call(
        body, name="merge_out_fwd", grid=(t_rows // tm,),
        in_specs=[seg(0), seg(1), seg(2), row, row, row, pl.BlockSpec((d, d), lambda i: (0, 0))],
        out_specs=[row, row],
        out_shape=[jax.ShapeDtypeStruct((t_rows, d), F32), jax.ShapeDtypeStruct((t_rows, d), BF16)],
        compiler_params=_cp("parallel"))(gates, gates, gates, y_a, o_n, x, w_out)


def _mlp_fwd(x_mid, gain, w_up, w_down):
    t_rows, d = x_mid.shape
    f8 = w_up.shape[2]
    tm = _row_tile(t_rows, ROW_TILE_WEIGHT_STREAM)

    def body(x_ref, g_ref, wu_ref, wd_ref, out_ref, u_ref, h_ref):
        @pl.when(pl.program_id(1) == 0)
        def _():
            xv = x_ref[...]
            h_ref[...] = (xv * _rstd(xv) * g_ref[...]).astype(BF16)
            out_ref[...] = xv

        u = _dot(h_ref[...], wu_ref[...])
        u_ref[...] = u.astype(BF16)
        r = jnp.maximum(u, 0.0)
        out_ref[...] += _dot((r * r).astype(BF16), wd_ref[...])

    row = pl.BlockSpec((tm, d), lambda i, j: (i, 0))
    return pl.pallas_call(
        body, name="mlp_fwd", grid=(t_rows // tm, N_DEV),
        in_specs=[row, pl.BlockSpec((1, d), lambda i, j: (0, 0)),
                  pl.BlockSpec((None, d, f8), lambda i, j: (j, 0, 0)),
                  pl.BlockSpec((None, f8, d), lambda i, j: (j, 0, 0))],
        out_specs=[row, pl.BlockSpec((None, tm, f8), lambda i, j: (j, i, 0)), row],
        out_shape=[jax.ShapeDtypeStruct((t_rows, d), F32),
                   jax.ShapeDtypeStruct((N_DEV, t_rows, f8), BF16),
                   jax.ShapeDtypeStruct((t_rows, d), BF16)],
        compiler_params=_cp("parallel", "arbitrary"))(x_mid, gain, w_up, w_down)


def _loss_head(x, gain, target):
    t_rows, d = x.shape
    tm = _row_tile(t_rows, 512)

    def body(x_ref, g_ref, t_ref, loss_ref, dx_ref, dxb_ref, dg_ref):
        @pl.when(pl.program_id(0) == 0)
        def _():
            loss_ref[...] = jnp.zeros_like(loss_ref)
            dg_ref[...] = jnp.zeros_like(dg_ref)

        xv = x_ref[...]
        g = g_ref[...]
        err = xv * _rstd(xv) * g - t_ref[...]
        loss_ref[...] += (0.5 / d) * jnp.sum(err * err)
        dx, dg = _rms_bwd(err * (1.0 / d), xv, g)
        dx_ref[...] = dx
        dxb_ref[...] = dx.astype(BF16)
        dg_ref[...] += dg

    row = pl.BlockSpec((tm, d), lambda i: (i, 0))
    vec = pl.BlockSpec((1, d), lambda i: (0, 0))
    return pl.pallas_call(
        body, name="loss_head", grid=(t_rows // tm,),
        in_specs=[row, vec, row],
        out_specs=[pl.BlockSpec((SUBLANES, LANES), lambda i: (0, 0)), row, row, vec],
        out_shape=[jax.ShapeDtypeStruct((SUBLANES, LANES), F32),
                   jax.ShapeDtypeStruct((t_rows, d), F32),
                   jax.ShapeDtypeStruct((t_rows, d), BF16),
                   jax.ShapeDtypeStruct((1, d), F32)],
        compiler_params=_cp("arbitrary"))(x, gain, target)


def _mlp_bwd(d_out, d_out_b, u, x_mid, gain, w_up, w_down):
    t_rows, d = x_mid.shape
    f8 = w_up.shape[2]
    tm = _row_tile(t_rows, ROW_TILE_WEIGHT_STREAM)
    sub = _row_tile(tm, ROW_CHUNK)

    def body(do_ref, dob_ref, u_ref, x_ref, g_ref, wu_ref, wd_ref, dx_ref, dxb_ref, du_ref, act_ref, dg_ref):
        j = pl.program_id(1)

        @pl.when((pl.program_id(0) == 0) & (j == 0))
        def _():
            dg_ref[...] = jnp.zeros_like(dg_ref)

        @pl.when(j == 0)
        def _():
            dx_ref[...] = jnp.zeros_like(dx_ref)

        r = jnp.maximum(u_ref[...].astype(F32), 0.0)
        act_ref[...] = (r * r).astype(BF16)
        du = (_dot_nt(dob_ref[...], wd_ref[...]) * (2.0 * r)).astype(BF16)
        du_ref[...] = du
        dx_ref[...] += _dot_nt(du, wu_ref[...])

        @pl.when(j == N_DEV - 1)
        def _():
            def finish(c, _):
                rows = pl.ds(pl.multiple_of(c * sub, sub), sub)
                dx, dg = _rms_bwd(dx_ref[rows, :], x_ref[rows, :], g_ref[...])
                dx = dx + do_ref[rows, :]
                dx_ref[rows, :] = dx
                dxb_ref[rows, :] = dx.astype(BF16)
                dg_ref[...] += dg
                return 0

            lax.fori_loop(0, tm // sub, finish, 0)

    row = pl.BlockSpec((tm, d), lambda i, j: (i, 0))
    vec = pl.BlockSpec((1, d), lambda i, j: (0, 0))
    hid = pl.BlockSpec((None, tm, f8), lambda i, j: (j, i, 0))
    return pl.pallas_call(
        body, name="mlp_bwd", grid=(t_rows // tm, N_DEV),
        in_specs=[row, row, hid, row, vec,
                  pl.BlockSpec((None, d, f8), lambda i, j: (j, 0, 0)),
                  pl.BlockSpec((None, f8, d), lambda i, j: (j, 0, 0))],
        out_specs=[row, row, hid, hid, vec],
        out_shape=[jax.ShapeDtypeStruct((t_rows, d), F32),
                   jax.ShapeDtypeStruct((t_rows, d), BF16),
                   jax.ShapeDtypeStruct((N_DEV, t_rows, f8), BF16),
                   jax.ShapeDtypeStruct((N_DEV, t_rows, f8), BF16),
                   jax.ShapeDtypeStruct((1, d), F32)],
        compiler_params=_cp("arbitrary", "arbitrary"))(d_out, d_out_b, u, x_mid, gain, w_up, w_down)


def _outproj_bwd(dx_mid_b, w_out, gates, y_a, o_n):
    t_rows, d = y_a.shape
    tm = _row_tile(t_rows, 256)

    def body(dx_ref, w_ref, g_ref, ma_ref, mb_ref, ya_ref, on_ref, dya_ref, don_ref, dp_ref):
        dy = _dot_nt(dx_ref[...], w_ref[...])
        sa = _sigmoid(ma_ref[...].astype(F32))
        sb = _sigmoid(mb_ref[...].astype(F32))
        sg, dsg = _silu_and_grad(g_ref[...].astype(F32))
        ya = ya_ref[...].astype(F32)
        on = on_ref[...].astype(F32)
        dya_ref[...] = dy * sa
        t = dy * sb
        don_ref[...] = t * sg
        dp_ref[0] = (t * on * dsg).astype(BF16)
        dp_ref[1] = (dy * ya * sa * (1.0 - sa)).astype(BF16)
        dp_ref[2] = (dy * on * sg * sb * (1.0 - sb)).astype(BF16)

    seg = lambda k: pl.BlockSpec((None, tm, d), lambda i, k=k: (k, i, 0))
    row = pl.BlockSpec((tm, d), lambda i: (i, 0))
    return pl.pallas_call(
        body, name="outproj_bwd", grid=(t_rows // tm,),
        in_specs=[row, pl.BlockSpec((d, d), lambda i: (0, 0)), seg(0), seg(1), seg(2), row, row],
        out_specs=[row, row, pl.BlockSpec((3, tm, d), lambda i: (0, i, 0))],
        out_shape=[jax.ShapeDtypeStruct((t_rows, d), F32),
                   jax.ShapeDtypeStruct((t_rows, d), F32),
                   jax.ShapeDtypeStruct((3, t_rows, d), BF16)],
        compiler_params=_cp("parallel"))(dx_mid_b, w_out, gates, gates, gates, y_a, o_n)


def _inproj_bwd(dx_mid, dp_a, dp_b, dp_c, w_seg, x_in, gain, carry=None):
    t_rows, d = x_in.shape
    tm = _row_tile(t_rows, ROW_TILE_WEIGHT_STREAM)
    sub = _row_tile(tm, ROW_CHUNK)
    n_a, n_b = dp_a.shape[0], dp_b.shape[0]

    def body(dxm_ref, a_ref, b_ref, c_ref, w_ref, x_ref, g_ref, dx_ref, dxb_ref, dg_ref):
        j = pl.program_id(1)

        @pl.when((pl.program_id(0) == 0) & (j == 0))
        def _():
            dg_ref[...] = jnp.zeros_like(dg_ref)

        @pl.when(j == 0)
        def _():
            dx_ref[...] = jnp.zeros_like(dx_ref)

        @pl.when(j < n_a)
        def _():
            dx_ref[...] += _dot_nt(a_ref[...], w_ref[...])

        @pl.when((j >= n_a) & (j < n_a + n_b))
        def _():
            dx_ref[...] += _dot_nt(b_ref[...], w_ref[...])

        @pl.when(j >= n_a + n_b)
        def _():
            dx_ref[...] += _dot_nt(c_ref[...], w_ref[...])

        @pl.when(j == N_DEV - 1)
        def _():
            def finish(c, _):
                rows = pl.ds(pl.multiple_of(c * sub, sub), sub)
                dx, dg = _rms_bwd(dx_ref[rows, :], x_ref[rows, :], g_ref[...])
                dx = dx + dxm_ref[rows, :]
                dx_ref[rows, :] = dx
                dxb_ref[rows, :] = dx.astype(BF16)
                dg_ref[...] += dg
                return 0

            lax.fori_loop(0, tm // sub, finish, 0)

    def part(first, n):
        return pl.BlockSpec((None, tm, d), lambda i, j: (jnp.clip(j - first, 0, n - 1), i, 0))

    row = pl.BlockSpec((tm, d), lambda i, j: (i, 0))
    vec = pl.BlockSpec((1, d), lambda i, j: (0, 0))
    return _call_carrying(
        body, carry, name="inproj_bwd", grid=(t_rows // tm, N_DEV),
        in_specs=[row, part(0, n_a), part(n_a, n_b), part(n_a + n_b, dp_c.shape[0]),
                  pl.BlockSpec((None, d, d), lambda i, j: (j, 0, 0)), row, vec],
        out_specs=[row, row, vec],
        out_shape=[jax.ShapeDtypeStruct((t_rows, d), F32),
                   jax.ShapeDtypeStruct((t_rows, d), BF16),
                   jax.ShapeDtypeStruct((1, d), F32)],
        scratch_shapes=[], semantics=("arbitrary", "arbitrary"),
        args=(dx_mid, dp_a, dp_b, dp_c, w_seg, x_in, gain))


def _wgrad(a3, b3, name):
    n_a, t_rows, k_a = a3.shape
    n_b, _, n_cols = b3.shape
    n = max(n_a, n_b)
    bk = _row_tile(k_a, 1024)
    bn = n_cols if n_cols <= 1024 else 1024
    tt = _row_tile(t_rows, WGRAD_TOKEN_TILE)
    n_t = t_rows // tt

    def body(a_ref, b_ref, o_ref, acc_ref):
        t, j = pl.program_id(2), pl.program_id(3)
        part = _dot_tn(a_ref[...], b_ref[...])

        @pl.when(t == 0)
        def _():
            acc_ref[j] = part

        @pl.when(t > 0)
        def _():
            acc_ref[j] += part

        @pl.when(t == n_t - 1)
        def _():
            o_ref[...] = acc_ref[j].astype(BF16)

    def out_map(p, q, t, j):
        return (jnp.where(t == n_t - 1, j, 0), p, q)

    return pl.pallas_call(
        body, name=name, grid=(k_a // bk, n_cols // bn, n_t, n),
        in_specs=[pl.BlockSpec((None, tt, bk), lambda p, q, t, j: (j if n_a > 1 else 0, t, p)),
                  pl.BlockSpec((None, tt, bn), lambda p, q, t, j: (j if n_b > 1 else 0, t, q))],
        out_specs=pl.BlockSpec((None, bk, bn), out_map),
        out_shape=jax.ShapeDtypeStruct((n, k_a, n_cols), BF16),
        scratch_shapes=[pltpu.VMEM((n, bk, bn), F32)],
        compiler_params=_cp("parallel", "parallel", "arbitrary", "arbitrary"))(a3, b3)


def _conv_taps(xe, n):
    return [_shift_rows(xe, CONV_W - 1 - j)[SUBLANES:SUBLANES + n, :] for j in range(CONV_W)]


def _rg_gates(xc, w_r, b_r, w_i, b_i, sp8):
    xb = xc.astype(BF16)
    r = _sigmoid(_dot(xb, w_r) + b_r)
    i = _sigmoid(_dot(xb, w_i) + b_i)
    return r, i


def _mixer_a_fwd(proj, conv_w, conv_b, w_r, b_r, w_i, b_i, lam, seq):
    _, t_rows, d = proj.shape
    n_seq, n_blk = t_rows // seq, d // RG_BLOCK_W
    wb = RG_BLOCK_W
    ch = _row_tile(seq, ROW_CHUNK)

    def body(xa_ref, ga_ref, cw_ref, cb_ref, wr_ref, br_ref, wi_ref, bi_ref, lam_ref, h_ref, ya_ref,
             xpad, a_s, u_s):
        xpad[0:SUBLANES, :] = jnp.zeros((SUBLANES, wb), F32)
        xpad[SUBLANES:, :] = xa_ref[...]
        sp8 = RG_C * _softplus(-lam_ref[...])

        def gates(c, _):
            r0 = pl.multiple_of(c * ch, ch)
            taps = _conv_taps(xpad[pl.ds(r0, ch + SUBLANES), :], ch)
            xc = cb_ref[...] + sum(cw_ref[pl.ds(j, 1), :] * taps[j] for j in range(CONV_W))
            r, i = _rg_gates(xc, wr_ref[...], br_ref[...], wi_ref[...], bi_ref[...], sp8)
            log_a = -(r * sp8)
            a_s[pl.ds(r0, ch), :] = jnp.exp(log_a)
            u_s[pl.ds(r0, ch), :] = jnp.sqrt(jnp.maximum(_one_minus_exp(2.0 * log_a), 0.0)) * (i * xc)
            return 0

        lax.fori_loop(0, seq // ch, gates, 0)
        _scan_rows(a_s, u_s, h_ref, seq, wb, reverse=False)

        def gate_out(c, _):
            r0 = pl.multiple_of(c * ch, ch)
            gl, _ = _gelu_and_grad(ga_ref[pl.ds(r0, ch), :])
            ya_ref[pl.ds(r0, ch), :] = (h_ref[pl.ds(r0, ch), :] * gl).astype(BF16)
            return 0

        lax.fori_loop(0, seq // ch, gate_out, 0)

    seg = lambda k: pl.BlockSpec((None, seq, wb), lambda s, b, k=k: (k, s, b))
    blk = pl.BlockSpec((seq, wb), lambda s, b: (s, b))
    vec = pl.BlockSpec((1, wb), lambda s, b: (0, b))
    wsp = pl.BlockSpec((None, wb, wb), lambda s, b: (b, 0, 0))
    return pl.pallas_call(
        body, name="mixer_a_fwd", grid=(n_seq, n_blk),
        in_specs=[seg(0), seg(1), pl.BlockSpec((CONV_W, wb), lambda s, b: (0, b)), vec, wsp, vec, wsp, vec, vec],
        out_specs=[blk, blk],
        out_shape=[jax.ShapeDtypeStruct((t_rows, d), F32), jax.ShapeDtypeStruct((t_rows, d), BF16)],
        scratch_shapes=[pltpu.VMEM((seq + SUBLANES, wb), F32), pltpu.VMEM((seq, wb), F32),
                        pltpu.VMEM((seq, wb), F32)],
        compiler_params=_cp("parallel", "parallel"))(proj, proj, conv_w, conv_b, w_r, b_r, w_i, b_i, lam)


def _mixer_a_bwd(proj, h, d_ya, conv_w, conv_b, w_r, b_r, w_i, b_i, lam, seq):
    _, t_rows, d = proj.shape
    n_seq, n_blk = t_rows // seq, d // RG_BLOCK_W
    wb = RG_BLOCK_W
    ch = _row_tile(seq, ROW_CHUNK)
    n_ch = seq // ch

    def body(xa_ref, ga_ref, h_ref, dya_ref, cw_ref, cb_ref, wr_ref, br_ref, wi_ref, bi_ref, lam_ref,
             dp_ref, dwr_ref, dwi_ref, dbr_ref, dbi_ref, dlam_ref, dcw_ref, dcb_ref,
             xpad, hpad, a_s, e_pad, g_s, xc_s, r_s, i_s, dxc_pad):
        @pl.when(pl.program_id(1) == 0)
        def _():
            for ref in (dwr_ref, dwi_ref, dbr_ref, dbi_ref, dlam_ref, dcw_ref, dcb_ref):
                ref[...] = jnp.zeros_like(ref)

        zeros8 = jnp.zeros((SUBLANES, wb), F32)
        xpad[0:SUBLANES, :] = zeros8
        xpad[SUBLANES:, :] = xa_ref[...]
        hpad[0:SUBLANES, :] = zeros8
        hpad[SUBLANES:, :] = h_ref[...]
        e_pad[seq:, :] = zeros8
        dxc_pad[seq:, :] = zeros8
        lam_v = lam_ref[...]
        sp8 = RG_C * _softplus(-lam_v)

        def recompute(c, _):
            r0 = pl.multiple_of(c * ch, ch)
            rows = pl.ds(r0, ch)
            taps = _conv_taps(xpad[pl.ds(r0, ch + SUBLANES), :], ch)
            xc = cb_ref[...] + sum(cw_ref[pl.ds(j, 1), :] * taps[j] for j in range(CONV_W))
            r, i = _rg_gates(xc, wr_ref[...], br_ref[...], wi_ref[...], bi_ref[...], sp8)
            a = jnp.exp(-(r * sp8))
            gl, dgl = _gelu_and_grad(ga_ref[rows, :])
            dya = dya_ref[rows, :]
            g = dya * gl
            dp_ref[1, rows, :] = (dya * h_ref[rows, :] * dgl).astype(BF16)
            a_s[rows, :] = a
            e_pad[rows, :] = a * g
            g_s[rows, :] = g
            xc_s[rows, :] = xc
            r_s[rows, :] = r
            i_s[rows, :] = i
            return 0

        lax.fori_loop(0, n_ch, recompute, 0)
        _scan_rows(a_s, e_pad, e_pad, seq, wb, reverse=True)

        def grads(c, _):
            r0 = pl.multiple_of(c * ch, ch)
            rows = pl.ds(r0, ch)
            halo = pl.ds(r0, ch + SUBLANES)
            dh = g_s[rows, :] + _shift_rows(e_pad[halo, :], ch + SUBLANES - 1)[0:ch, :]
            h_prev = _shift_rows(hpad[halo, :], 1)[SUBLANES:, :]
            xc, r, i = xc_s[rows, :], r_s[rows, :], i_s[rows, :]
            log_a = -(r * sp8)
            a = jnp.exp(log_a)
            om = _one_minus_exp(2.0 * log_a)
            sq = jnp.sqrt(jnp.maximum(om, 0.0))
            t1 = dh * xc
            d_i = t1 * sq
            d_la = dh * h_prev * a + jnp.where(om > 0.0, -(t1 * i) * (1.0 - om) / sq, 0.0)
            dpr = -(d_la * sp8) * r * (1.0 - r)
            dpi = d_i * i * (1.0 - i)
            dprb, dpib, xb = dpr.astype(BF16), dpi.astype(BF16), xc.astype(BF16)
            dxc = dh * sq * i + _dot_nt(dprb, wr_ref[...]) + _dot_nt(dpib, wi_ref[...])
            dwr_ref[...] += _dot_tn(xb, dprb)
            dwi_ref[...] += _dot_tn(xb, dpib)
            dbr_ref[...] += jnp.sum(dpr, axis=0, keepdims=True)
            dbi_ref[...] += jnp.sum(dpi, axis=0, keepdims=True)
            dlam_ref[...] += jnp.sum(d_la * r, axis=0, keepdims=True) * (RG_C * _sigmoid(-lam_v))
            dcb_ref[...] += jnp.sum(dxc, axis=0, keepdims=True)
            taps = _conv_taps(xpad[halo, :], ch)
            for j in range(CONV_W):
                dcw_ref[pl.ds(j, 1), :] += jnp.sum(dxc * taps[j], axis=0, keepdims=True)
            dxc_pad[rows, :] = dxc
            return 0

        lax.fori_loop(0, n_ch, grads, 0)

        def conv_bwd(c, _):
            r0 = pl.multiple_of(c * ch, ch)
            de = dxc_pad[pl.ds(r0, ch + SUBLANES), :]
            dxa = sum(cw_ref[pl.ds(j, 1), :] * _shift_rows(de, ch + SUBLANES - (CONV_W - 1 - j))[0:ch, :]
                      for j in range(CONV_W))
            dp_ref[0, pl.ds(r0, ch), :] = dxa.astype(BF16)
            return 0

        lax.fori_loop(0, n_ch, conv_bwd, 0)

    seg = lambda k: pl.BlockSpec((None, seq, wb), lambda b, s, k=k: (k, s, b))
    blk = pl.BlockSpec((seq, wb), lambda b, s: (s, b))
    vec = pl.BlockSpec((1, wb), lambda b, s: (0, b))
    taps = pl.BlockSpec((CONV_W, wb), lambda b, s: (0, b))
    wsp = pl.BlockSpec((None, wb, wb), lambda b, s: (b, 0, 0))
    vec_shape = jax.ShapeDtypeStruct((1, d), F32)
    w_shape = jax.ShapeDtypeStruct((n_blk, wb, wb), F32)
    pad = pltpu.VMEM((seq + SUBLANES, wb), F32)
    full = pltpu.VMEM((seq, wb), F32)
    return pl.pallas_call(
        body, name="mixer_a_bwd", grid=(n_blk, n_seq),
        in_specs=[seg(0), seg(1), blk, blk, taps, vec, wsp, vec, wsp, vec, vec],
        out_specs=[pl.BlockSpec((2, seq, wb), lambda b, s: (0, s, b)), wsp, wsp, vec, vec, vec, taps, vec],
        out_shape=[jax.ShapeDtypeStruct((2, t_rows, d), BF16), w_shape, w_shape, vec_shape, vec_shape,
                   vec_shape, jax.ShapeDtypeStruct((CONV_W, d), F32), vec_shape],
        scratch_shapes=[pad, pad, full, pad, full, full, full, full, pad],
        compiler_params=_cp("parallel", "arbitrary"))(
            proj, proj, h, d_ya, conv_w, conv_b, w_r, b_r, w_i, b_i, lam)


def _hg_prepare(q_ref, z_ref, lb, rows):
    z = z_ref[rows, :]
    sig, nsig = _sigmoid_pair(z)
    fg = lb + (1.0 - lb) * sig
    log_f = jnp.log(jnp.maximum(fg, F_MIN))
    key = (1.0 - lb) * nsig
    qs, _ = _silu_and_grad(q_ref[rows, :])
    return qs, key, log_f, sig, fg


HG_UNROLL_TERMS = 32
HG_UNROLL_FWD = 32
HG_UNROLL_BWD = 32
HG_HALF = HG_CHUNK // 2
HG_STACK = HG_CHUNK * HG_HALF
assert HG_HALF == SUBLANES


def _half_of(x, s):
    return x[:HG_HALF, :] if s < HG_HALF else x[HG_HALF:, :]


def _hg_decay(g_ref, r0, g_rows, first_row, s):
    rid = lax.broadcasted_iota(jnp.int32, g_rows.shape, 0) + first_row
    gs = g_ref[pl.ds(r0 + s, 1), :]
    return jnp.where(rid >= s, jnp.exp(g_rows - gs), 0.0)


def _half_start(s):
    return 0 if s < HG_HALF else HG_HALF


def _hg_cross_decays(gc):
    rid = lax.broadcasted_iota(jnp.int32, (HG_CHUNK, HG_DK), 0)
    g_mid = gc[HG_HALF - 1:HG_HALF, :]
    e_hi = jnp.where(rid >= HG_HALF, jnp.exp(gc - g_mid), 0.0)
    e_lo = jnp.where(rid < HG_HALF, jnp.exp(g_mid - gc), 0.0)
    return e_hi, e_lo


def _hg_cross(qc, kc, gc):
    e_hi, e_lo = _hg_cross_decays(gc)
    return (qc * e_hi).astype(BF16), (kc * e_lo).astype(BF16)


def _stack(slabs):
    return jnp.concatenate(slabs, axis=0).astype(BF16)


def _slab_row_sums():
    row = lax.broadcasted_iota(jnp.int32, (HG_CHUNK, HG_STACK), 0)
    col = lax.broadcasted_iota(jnp.int32, (HG_CHUNK, HG_STACK), 1)
    lo = row * HG_HALF
    return jnp.where((col >= lo) & (col < lo + HG_HALF), 1.0, 0.0).astype(BF16)


def _for_chunks(n, unroll, *stages, after_trip=None):
    unroll = min(unroll, n)
    assert n % unroll == 0

    def trip(i, _):
        chunks = [i * unroll + u for u in range(unroll)]
        carried = [stages[0](c) for c in chunks]
        for stage in stages[1:]:
            carried = [stage(c, x) for c, x in zip(chunks, carried)]
        if after_trip is not None:
            after_trip(carried)
        return 0

    lax.fori_loop(0, n // unroll, trip, 0)


def _hg_state_chain(states, g_ref, carry_ref, n_chunks, reverse):
    unroll = min(8, n_chunks)
    assert n_chunks % unroll == 0
    carry_ref[...] = jnp.zeros_like(carry_ref)

    def trip(i, _):
        st = carry_ref[...]
        for u in range(unroll):
            k = i * unroll + u
            c = n_chunks - 1 - k if reverse else k
            term = states[c]
            states[c] = st
            st = st * jnp.exp(g_ref[pl.ds(c * HG_CHUNK + HG_CHUNK - 1, 1), :]) + term
        carry_ref[...] = st
        return 0

    lax.fori_loop(0, n_chunks // unroll, trip, 0)


def _hgrn_fwd(proj, lower_bound, hg_gain, seq, carry=None):
    _, t_rows, d = proj.shape
    n_seq, n_head = t_rows // seq, d // HG_DK
    ch = _row_tile(seq, ROW_CHUNK)
    n_chunks = seq // HG_CHUNK

    def body(q_ref, z_ref, v_ref, lb_ref, gain_ref, o_ref, on_ref, qs_s, k_s, g_s, states, st_ref):
        lb = lb_ref[...]
        tri = _group_cumsum_matrix(ch, HG_CHUNK)

        per_block = ch // HG_CHUNK

        def prepare(c):
            rows = pl.ds(pl.multiple_of(c * ch, ch), ch)
            qs, key, log_f, _, _ = _hg_prepare(q_ref, z_ref, lb, rows)
            qs_s[rows, :] = qs
            k_s[rows, :] = key
            return key, _group_cumsum_mxu(log_f, tri)

        def state_terms(c, carried):
            key, g = carried
            r0 = pl.multiple_of(c * ch, ch)
            g_s[pl.ds(r0, ch), :] = g
            terms = []
            for u in range(per_block):
                sl = slice(u * HG_CHUNK, (u + 1) * HG_CHUNK)
                k_end = key[sl, :] * jnp.exp(g[(u + 1) * HG_CHUNK - 1:(u + 1) * HG_CHUNK, :] - g[sl, :])
                vc = v_ref[pl.ds(r0 + u * HG_CHUNK, HG_CHUNK), :]
                terms.append(_dot_tn(vc.astype(BF16), k_end.astype(BF16)))
            return terms

        def store_terms(c, terms):
            for u, term in enumerate(terms):
                states[c * per_block + u] = term

        _for_chunks(seq // ch, 2, prepare, state_terms, store_terms)
        _hg_state_chain(states, g_s, st_ref, n_chunks, reverse=False)
        ones = jnp.ones((HG_DK, HG_DK), BF16)

        def issue(c):
            r0 = pl.multiple_of(c * HG_CHUNK, HG_CHUNK)
            rows = pl.ds(r0, HG_CHUNK)
            qc, gc = qs_s[rows, :], g_s[rows, :]
            o = _dot_nt((qc * jnp.exp(gc)).astype(BF16), states[c].astype(BF16))
            pairs = [qc[_half_start(s):, :] * _hg_decay(g_s, r0, gc[_half_start(s):, :], _half_start(s), s)
                     * k_s[pl.ds(r0 + s, 1), :] for s in range(HG_CHUNK)]
            score = _dot(_stack(pairs), ones)
            return o, score

        def combine(c, issued):
            o, score = issued
            r0 = pl.multiple_of(c * HG_CHUNK, HG_CHUNK)
            halves = [o[:HG_HALF, :], o[HG_HALF:, :]]
            first = 0
            for s in range(HG_CHUNK):
                vs = v_ref[pl.ds(r0 + s, 1), :]
                if s < HG_HALF:
                    halves[0] += score[first:first + HG_HALF, :] * vs
                    first += HG_HALF
                halves[1] += score[first:first + HG_HALF, :] * vs
                first += HG_HALF
            o = jnp.concatenate(halves, axis=0)
            o_ref[pl.ds(r0, HG_CHUNK), :] = o
            on_ref[pl.ds(r0, HG_CHUNK), :] = (o * _rstd(o) * gain_ref[...]).astype(BF16)

        _for_chunks(n_chunks, HG_UNROLL_FWD, issue, combine)

    seg = lambda k: pl.BlockSpec((None, seq, HG_DK), lambda s, h, k=k: (k, s, h))
    blk = pl.BlockSpec((seq, HG_DK), lambda s, h: (s, h))
    full = pltpu.VMEM((seq, HG_DK), F32)
    return _call_carrying(
        body, carry, name="hgrn_fwd", grid=(n_seq, n_head),
        in_specs=[seg(2), seg(3), seg(4), pl.BlockSpec((1, HG_DK), lambda s, h: (0, h)),
                  pl.BlockSpec((1, HG_DK), lambda s, h: (0, 0))],
        out_specs=[blk, blk],
        out_shape=[jax.ShapeDtypeStruct((t_rows, d), F32), jax.ShapeDtypeStruct((t_rows, d), BF16)],
        scratch_shapes=[full, full, full, pltpu.VMEM((n_chunks, HG_DK, HG_DK), F32),
                        pltpu.VMEM((HG_DK, HG_DK), F32)],
        semantics=("parallel", "parallel"), args=(proj, proj, proj, lower_bound, hg_gain))


def _hgrn_bwd(proj, lower_bound, hg_gain, o, d_on, seq, carry=None):
    _, t_rows, d = proj.shape
    n_seq, n_head = t_rows // seq, d // HG_DK
    ch = _row_tile(seq, ROW_CHUNK)
    n_chunks = seq // HG_CHUNK
    cc = HG_CHUNK

    def body(q_ref, z_ref, v_ref, lb_ref, gain_ref, o_ref, don_ref, dp_ref, dlb_ref, dgain_ref,
             qs_s, k_s, g_s, do_s, dlb_acc, states, dstates, carry_ref):
        hh, ss = pl.program_id(0), pl.program_id(1)
        lb = lb_ref[...]

        @pl.when(ss == 0)
        def _():
            dlb_ref[...] = jnp.zeros_like(dlb_ref)

        @pl.when((ss == 0) & (hh == 0))
        def _():
            dgain_ref[...] = jnp.zeros_like(dgain_ref)

        tri = _group_cumsum_matrix(ch, cc)

        def prepare(c):
            rows = pl.ds(pl.multiple_of(c * ch, ch), ch)
            qs, key, log_f, _, _ = _hg_prepare(q_ref, z_ref, lb, rows)
            qs_s[rows, :] = qs
            k_s[rows, :] = key
            do, dgain = _rms_bwd(don_ref[rows, :], o_ref[rows, :], gain_ref[...])
            do_s[rows, :] = do
            dgain_ref[...] += dgain
            return _group_cumsum_mxu(log_f, tri)

        def store_cumsum(c, g):
            g_s[pl.ds(pl.multiple_of(c * ch, ch), ch), :] = g

        _for_chunks(seq // ch, 4, prepare, store_cumsum)

        def chain_terms(c):
            rows = pl.ds(pl.multiple_of(c * cc, cc), cc)
            gc = g_s[rows, :]
            k_end = k_s[rows, :] * jnp.exp(gc[cc - 1:cc, :] - gc)
            q_in = qs_s[rows, :] * jnp.exp(gc)
            return (_dot_tn(v_ref[rows, :].astype(BF16), k_end.astype(BF16)),
                    _dot_tn(do_s[rows, :].astype(BF16), q_in.astype(BF16)))

        def store_terms(c, terms):
            states[c], dstates[c] = terms

        _for_chunks(n_chunks, HG_UNROLL_TERMS, chain_terms, store_terms)
        _hg_state_chain(states, g_s, carry_ref, n_chunks, reverse=False)
        _hg_state_chain(dstates, g_s, carry_ref, n_chunks, reverse=True)
        ones = jnp.ones((HG_DK, HG_DK), BF16)
        row_sums = _slab_row_sums()

        def chunk_rows(c):
            r0 = pl.multiple_of(c * cc, cc)
            return r0, pl.ds(r0, cc)

        def through_state(c):
            r0, rows = chunk_rows(c)
            qc, kc, gc, vc, doc = qs_s[rows, :], k_s[rows, :], g_s[rows, :], v_ref[rows, :], do_s[rows, :]
            st, dst = states[c], dstates[c]
            g_last = gc[cc - 1:cc, :]
            e_last, e_end = jnp.exp(g_last), jnp.exp(g_last - gc)
            dob, vcb, dstb = doc.astype(BF16), vc.astype(BF16), dst.astype(BF16)
            dqs = _dot(dob, st.astype(BF16))
            dk_state = _dot(vcb, dstb)
            dv = _dot_nt((kc * e_end).astype(BF16), dstb)
            cots = [_half_of(doc, s) * v_ref[pl.ds(r0 + s, 1), :] for s in range(cc)]
            d_score = _dot(_stack(cots), ones)
            x, y = _hg_cross(qc, kc, gc)
            cross = (_dot_nt(dob, vcb), _dot_nt(vcb, dob), _dot_nt(y, x))
            return dqs, dk_state, dv, d_score, e_last * jnp.sum(dst * st, axis=0, keepdims=True), cross

        def pair_terms(c, carried):
            dqs, dk_state, dv, d_score, d_glast, (da_cross, da_cross_t, a_cross_t) = carried
            r0, rows = chunk_rows(c)
            qc, kc, gc = qs_s[rows, :], k_s[rows, :], g_s[rows, :]
            dqs = dqs * jnp.exp(gc)
            dk_state = dk_state * jnp.exp(gc[cc - 1:cc, :] - gc)
            d_glast = d_glast + jnp.sum(kc * dk_state, axis=0, keepdims=True)
            dqs_half = [dqs[:HG_HALF, :], dqs[HG_HALF:, :]]
            pairs, dk_terms = [], []
            for s in range(cc):
                qv = _half_of(qc, s)
                decay = _hg_decay(g_s, r0, _half_of(gc, s), _half_start(s), s)
                ks = k_s[pl.ds(r0 + s, 1), :]
                da_decay = d_score[s * HG_HALF:(s + 1) * HG_HALF, :] * decay
                pairs.append(qv * decay * ks)
                dk_terms.append(da_decay * qv)
                dqs_half[s // HG_HALF] += da_decay * ks
            score = _dot(_stack(pairs), ones)
            dk = dk_state + _dot(row_sums, _stack(dk_terms))
            x, y = _hg_cross(qc, kc, gc)
            cross = (_dot(da_cross.astype(BF16), y), _dot(da_cross_t.astype(BF16), x),
                     _dot(a_cross_t.astype(BF16), do_s[rows, :].astype(BF16)))
            return jnp.concatenate(dqs_half, axis=0), dk, dv, score, d_glast, cross

        def value_terms(c, carried):
            dqs, dk, dv, score, d_glast, (dx_cross, dy_cross, dv_cross) = carried
            _, rows = chunk_rows(c)
            doc, gc = do_s[rows, :], g_s[rows, :]
            dv_terms = [score[s * HG_HALF:(s + 1) * HG_HALF, :] * _half_of(doc, s) for s in range(cc)]
            e_hi, e_lo = _hg_cross_decays(gc)
            return (dqs + dx_cross * e_hi, dk + dy_cross * e_lo,
                    dv + dv_cross + _dot(row_sums, _stack(dv_terms)), d_glast)

        dlb_acc[...] = jnp.zeros_like(dlb_acc)

        def store(c, x):
            dqs, dk, dv, d_glast = x
            _, rows = chunk_rows(c)
            d_g = qs_s[rows, :] * dqs - k_s[rows, :] * dk
            d_logf = _seg_cumsum(d_g, cc, reverse=True) + d_glast
            sig, nsig = _sigmoid_pair(z_ref[rows, :])
            fg = lb + (1.0 - lb) * sig
            _, dsilu = _silu_and_grad(q_ref[rows, :])
            d_gate = jnp.where(fg > F_MIN, d_logf / fg, 0.0) - dk
            dp_ref[0, rows, :] = (dqs * dsilu).astype(BF16)
            dp_ref[1, rows, :] = (d_gate * (1.0 - lb) * sig * nsig).astype(BF16)
            dp_ref[2, rows, :] = dv.astype(BF16)
            return d_gate * nsig

        def add_lower_bound_grads(per_chunk):
            while len(per_chunk) > 1:
                per_chunk = [a + b for a, b in zip(per_chunk[::2], per_chunk[1::2])]
            dlb_acc[...] += per_chunk[0]

        _for_chunks(n_chunks, HG_UNROLL_BWD, through_state, pair_terms, value_terms, store,
                    after_trip=add_lower_bound_grads)
        dlb_ref[...] += jnp.sum(dlb_acc[...], axis=0, keepdims=True)

    seg = lambda k: pl.BlockSpec((None, seq, HG_DK), lambda h, s, k=k: (k, s, h))
    blk = pl.BlockSpec((seq, HG_DK), lambda h, s: (s, h))
    full = pltpu.VMEM((seq, HG_DK), F32)
    return _call_carrying(
        body, carry, name="hgrn_bwd", grid=(n_head, n_seq),
        in_specs=[seg(2), seg(3), seg(4), pl.BlockSpec((1, HG_DK), lambda h, s: (0, h)),
                  pl.BlockSpec((1, HG_DK), lambda h, s: (0, 0)), blk, blk],
        out_specs=[pl.BlockSpec((3, seq, HG_DK), lambda h, s: (0, s, h)),
                   pl.BlockSpec((1, HG_DK), lambda h, s: (0, h)),
                   pl.BlockSpec((1, HG_DK), lambda h, s: (0, 0))],
        out_shape=[jax.ShapeDtypeStruct((3, t_rows, d), BF16), jax.ShapeDtypeStruct((1, d), F32),
                   jax.ShapeDtypeStruct((1, HG_DK), F32)],
        scratch_shapes=[full, full, full, full, pltpu.VMEM((cc, HG_DK), F32),
                        pltpu.VMEM((n_chunks, HG_DK, HG_DK), F32), pltpu.VMEM((n_chunks, HG_DK, HG_DK), F32),
                        pltpu.VMEM((HG_DK, HG_DK), F32)],
        semantics=("arbitrary", "arbitrary"), args=(proj, proj, proj, lower_bound, hg_gain, o, d_on))


def _mesh_place():
    x, y, c = lax.axis_index("x"), lax.axis_index("y"), lax.axis_index("c")
    return x, y, c


def _peer(place, k):
    x, y, c = place
    px = 1 - x if k & 4 else x
    py = 1 - y if k & 2 else y
    pc = 1 - c if k & 1 else c
    return (px, py, pc), 4 * px + 2 * py + pc


class _Exchange:
    def __init__(self, srcs, gather):
        self.n = len(srcs)
        self.gather = gather
        self.out_shape = [jax.ShapeDtypeStruct((N_DEV,) + tuple(s.shape if gather else s.shape[1:]), s.dtype)
                          for s in srcs]
        self.scratch = [pltpu.SemaphoreType.DMA((self.n * (N_DEV - 1),)),
                        pltpu.SemaphoreType.DMA((self.n * (N_DEV - 1),)),
                        pltpu.SemaphoreType.DMA((self.n,))]

    def _copies(self, src_refs, out_refs, sems):
        send_sems, recv_sems, local_sems = sems
        place = _mesh_place()
        me = 4 * place[0] + 2 * place[1] + place[2]
        local, sends, recvs = [], [], []
        for a, (src, out) in enumerate(zip(src_refs, out_refs)):
            outgoing = (lambda idx, src=src: src) if self.gather else (lambda idx, src=src: src.at[idx])
            local.append(pltpu.make_async_copy(outgoing(me), out.at[me], local_sems.at[a]))
            for k in range(1, N_DEV):
                peer, peer_idx = _peer(place, k)
                sem = a * (N_DEV - 1) + k - 1
                sends.append(pltpu.make_async_remote_copy(
                    src_ref=outgoing(peer_idx), dst_ref=out.at[me], send_sem=send_sems.at[sem],
                    recv_sem=recv_sems.at[sem], device_id=peer, device_id_type=MESH_ID))
                recvs.append(pltpu.make_async_remote_copy(
                    src_ref=outgoing(peer_idx), dst_ref=out.at[peer_idx], send_sem=send_sems.at[sem],
                    recv_sem=recv_sems.at[sem], device_id=peer, device_id_type=MESH_ID))
        return local, sends, recvs

    def start(self, src_refs, out_refs, sems):
        local, sends, _ = self._copies(src_refs, out_refs, sems)
        for cp in local + sends:
            cp.start()

    def wait(self, src_refs, out_refs, sems):
        local, sends, recvs = self._copies(src_refs, out_refs, sems)
        for cp in recvs:
            cp.wait_recv()
        for cp in sends:
            cp.wait_send()
        for cp in local:
            cp.wait()


def _call_carrying(body, carry, *, name, grid, in_specs, out_specs, out_shape, scratch_shapes, semantics, args):
    if carry is None:
        outs = pl.pallas_call(body, name=name, grid=grid, in_specs=in_specs, out_specs=out_specs,
                              out_shape=out_shape, scratch_shapes=scratch_shapes,
                              compiler_params=_cp(*semantics))(*args)
        return outs, []
    srcs, gather = carry
    ex = _Exchange(srcs, gather)
    n, n_in, n_out, n_scr = ex.n, len(in_specs), len(out_specs), len(scratch_shapes)

    def wrapped(*refs):
        ins, refs = refs[:n_in], refs[n_in:]
        src_refs, refs = refs[:n], refs[n:]
        outs, refs = refs[:n_out], refs[n_out:]
        dst_refs, refs = refs[:n], refs[n:]
        scratch, sems = refs[:n_scr], refs[n_scr:]
        first, last = None, None
        for axis, size in enumerate(grid):
            i = pl.program_id(axis)
            first = (i == 0) if first is None else first & (i == 0)
            last = (i == size - 1) if last is None else last & (i == size - 1)

        @pl.when(first)
        def _():
            ex.start(src_refs, dst_refs, sems)

        body(*ins, *outs, *scratch)

        @pl.when(last)
        def _():
            ex.wait(src_refs, dst_refs, sems)

    any_space = pl.BlockSpec(memory_space=pl.ANY)
    res = pl.pallas_call(
        wrapped, name=name + "_carrying", grid=grid, in_specs=list(in_specs) + [any_space] * n,
        out_specs=list(out_specs) + [any_space] * n, out_shape=list(out_shape) + ex.out_shape,
        scratch_shapes=list(scratch_shapes) + ex.scratch,
        compiler_params=_cp(*(["arbitrary"] * len(grid))))(*args, *srcs)
    return res[:n_out], res[n_out:]


def _exchange(srcs, name, gather):
    ex = _Exchange(srcs, gather)
    n = ex.n

    def body(*refs):
        src_refs, out_refs, sems = refs[:n], refs[n:2 * n], refs[2 * n:]
        ex.start(src_refs, out_refs, sems)
        ex.wait(src_refs, out_refs, sems)

    any_space = pl.BlockSpec(memory_space=pl.ANY)
    return pl.pallas_call(
        body, name=name, in_specs=[any_space] * n, out_specs=[any_space] * n,
        out_shape=ex.out_shape, scratch_shapes=ex.scratch)(*srcs)


def _reduce_adamw(parts, w, m, v, name):
    rows, cols = w.shape
    n_seg = len(parts)
    seg_rows = rows // n_seg
    tr = _row_tile(seg_rows, 128)
    per_seg = seg_rows // tr
    c1 = np.float32(1.0 - ADAM_B1 ** ADAM_STEP)
    c2 = np.float32(1.0 - ADAM_B2 ** ADAM_STEP)

    def body(*refs):
        p_refs = refs[:n_seg]
        w_ref, m_ref, v_ref, g_ref, d_ref, nm_ref, nv_ref = refs[n_seg:]
        seg = pl.program_id(0)
        for k, p_ref in enumerate(p_refs):
            @pl.when(seg == k)
            def _(p_ref=p_ref):
                g = p_ref[0].astype(F32)
                for dev in range(1, N_DEV):
                    g = g + p_ref[dev].astype(F32)
                g_ref[...] = g

        g = g_ref[...]
        nm = ADAM_B1 * m_ref[...] + (1.0 - ADAM_B1) * g
        nv = ADAM_B2 * v_ref[...] + (1.0 - ADAM_B2) * (g * g)
        nm_ref[...] = nm
        nv_ref[...] = nv
        d_ref[...] = -ADAM_LR * ((nm / c1) / (jnp.sqrt(nv / c2) + ADAM_EPS) + ADAM_WD * w_ref[...])

    def part_spec(k):
        return pl.BlockSpec((N_DEV, tr, cols), lambda s, i, k=k: (0, jnp.where(s == k, i, 0), 0))

    blk = pl.BlockSpec((tr, cols), lambda s, i: (s * per_seg + i, 0))
    shp = jax.ShapeDtypeStruct((rows, cols), F32)
    return pl.pallas_call(
        body, name=name, grid=(n_seg, per_seg),
        in_specs=[part_spec(k) for k in range(n_seg)] + [blk, blk, blk],
        out_specs=[blk, blk, blk, blk], out_shape=[shp, shp, shp, shp],
        compiler_params=_cp("arbitrary", "arbitrary"))(*parts, w, m, v)


def _pack(arrays, lead=0):
    parts = []
    for a in arrays:
        f = a.reshape(a.shape[:lead] + (-1, LANES))
        pad = -f.shape[lead] % PACK_ROWS
        if pad:
            f = jnp.pad(f, [(0, 0)] * lead + [(0, pad), (0, 0)])
        parts.append(f)
    return jnp.concatenate(parts, axis=lead)


def _unpack(buf, shapes, lead=0):
    out, r = [], 0
    for shp in shapes:
        n = int(np.prod(shp)) // LANES
        part = lax.slice_in_dim(buf, r, r + n, axis=lead)
        out.append(part.reshape(buf.shape[:lead] + tuple(shp)))
        r += n + (-n % PACK_ROWS)
    return out


REPLICATED = ("lb_logits", "norm_mix", "conv_b", "b_r", "b_i", "lam", "hg_norm", "norm_mlp", "norm_final")
SMALL_SHARDED = ("conv_w", "w_r", "w_i")
LARGE_SHARDED = ("w_in", "w_out", "w_up", "w_down")
WEIGHTS = ("lb_logits", "norm_mix", "w_in", "conv_w", "conv_b", "w_r", "b_r", "w_i", "b_i", "lam", "hg_norm",
           "w_out", "norm_mlp", "w_up", "w_down", "norm_final")


def _matmul_weight_shards(p):
    depth = p["w_in"].shape[0]
    cast = {k: p[k].astype(BF16) for k in LARGE_SHARDED}
    return ([cast["w_in"][l] for l in range(depth)],
            [[cast[k][l] for k in ("w_out", "w_up", "w_down")] for l in range(depth)])


def _gathered_rest(got):
    w_out, w_up, w_down = got
    d = w_out.shape[2]
    return dict(w_out=w_out.reshape(d, d), w_up=w_up, w_down=w_down)


def _unpack_mixer_weights(small, p):
    depth, d, _ = p["w_in"].shape
    n_blk = d // RG_BLOCK_W
    conv_w, w_r, w_i = _unpack(small, [p["conv_w"].shape, p["w_r"].shape, p["w_i"].shape], lead=1)
    conv_w = conv_w.transpose(1, 2, 0, 3).reshape(depth, CONV_W, d)
    w_r = w_r.transpose(1, 2, 0, 3, 4).reshape(depth, n_blk, RG_BLOCK_W, RG_BLOCK_W).astype(BF16)
    w_i = w_i.transpose(1, 2, 0, 3, 4).reshape(depth, n_blk, RG_BLOCK_W, RG_BLOCK_W).astype(BF16)
    return conv_w, w_r, w_i


def _local_step(x, target, p):
    bl, seq, d = x.shape
    depth = p["w_in"].shape[0]
    t_rows = bl * seq
    row = lambda a, l: a[l:l + 1]
    lbs = _lower_bounds_fwd(p["lb_logits"])
    shard_in, shard_rest = _matmul_weight_shards(p)
    w_in = _exchange([shard_in[0]], "gather_w_in", gather=True)[0]
    cur = x.reshape(t_rows, d)
    saved, layers = [], []
    for l in range(depth):
        if l == 0:
            (proj, gates, h), small = _inproj_fwd(cur, row(p["norm_mix"], l), w_in,
                                                  carry=([_pack([p["conv_w"], p["w_r"], p["w_i"]])], True))
            conv_w, w_r, w_i = _unpack_mixer_weights(small[0], p)
        else:
            (proj, gates, h), _ = _inproj_fwd(cur, row(p["norm_mix"], l), w_in)
        w = dict(w_in=w_in, conv_w=conv_w[l], w_r=w_r[l], w_i=w_i[l])
        hs, y_a = _mixer_a_fwd(proj, w["conv_w"], row(p["conv_b"], l), w["w_r"], row(p["b_r"], l), w["w_i"],
                               row(p["b_i"], l), row(p["lam"], l), seq)
        (o, o_n), got = _hgrn_fwd(proj, row(lbs, l), row(p["hg_norm"], l), seq,
                                  carry=(shard_rest[l] + ([shard_in[l + 1]] if l + 1 < depth else []), True))
        w.update(_gathered_rest(got[:3]))
        w_in = got[3] if l + 1 < depth else None
        layers.append(w)
        x_mid, y = _merge_out_fwd(gates, y_a, o_n, cur, w["w_out"])
        x_out, u, h2 = _mlp_fwd(x_mid, row(p["norm_mlp"], l), w["w_up"], w["w_down"])
        saved.append(dict(x_in=cur, proj=proj, gates=gates, h=h, hs=hs, y_a=y_a, o=o, o_n=o_n, x_mid=x_mid, y=y, u=u, h2=h2))
        cur = x_out
    loss8, dx, dxb, g_norm_final = _loss_head(cur, p["norm_final"].reshape(1, d), target.reshape(t_rows, d))
    small = ("norm_mix", "conv_w", "conv_b", "w_r", "b_r", "w_i", "b_i", "lam", "hg_norm", "norm_mlp")
    g = {k: [None] * depth for k in small}
    d_lbs, received = [None] * depth, [None] * depth
    g_w_in = None
    for l in reversed(range(depth)):
        s, w = saved[l], layers[l]
        dx_mid, dx_mid_b, du, act, g["norm_mlp"][l] = _mlp_bwd(dx, dxb, s["u"], s["x_mid"], row(p["norm_mlp"], l),
                                                               w["w_up"], w["w_down"])
        g_w_down = _wgrad(act, dxb[None], "wgrad_down")
        g_w_up = _wgrad(s["h2"][None], du, "wgrad_up")
        d_ya, d_on, dp_c = _outproj_bwd(dx_mid_b, w["w_out"], s["gates"], s["y_a"], s["o_n"])
        g_w_out = _wgrad(s["y"][None], dx_mid_b[None], "wgrad_out").reshape(N_DEV, d // N_DEV, d)
        (dp_b, d_lbs[l], g["hg_norm"][l]), got = _hgrn_bwd(
            s["proj"], row(lbs, l), row(p["hg_norm"], l), s["o"], d_on, seq,
            carry=([g_w_out, g_w_up, g_w_down] + ([g_w_in] if g_w_in is not None else []), False))
        received[l] = [None] + list(got[:3])
        if g_w_in is not None:
            received[l + 1][0] = got[3]
        (dp_a, g["w_r"][l], g["w_i"][l], g["b_r"][l], g["b_i"][l], g["lam"][l], g["conv_w"][l],
         g["conv_b"][l]) = _mixer_a_bwd(s["proj"], s["hs"], d_ya, w["conv_w"], row(p["conv_b"], l), w["w_r"],
                                        row(p["b_r"], l), w["w_i"], row(p["b_i"], l), row(p["lam"], l), seq)
        hb = s["h"][None]
        g_w_in = jnp.concatenate([_wgrad(hb, dp_a, "wgrad_in_pair"), _wgrad(hb, dp_b, "wgrad_in_triple"),
                                  _wgrad(hb, dp_c, "wgrad_in_triple")], axis=0)
        carry = None
        if l == 0:
            carry = ([g_w_in, _mixer_grads_by_owner(g, d)], False)
        (dx, dxb, g["norm_mix"][l]), got = _inproj_bwd(dx_mid, dp_a, dp_b, dp_c, w["w_in"], s["x_in"],
                                                       row(p["norm_mix"], l), carry=carry)
    received[0][0], received_mixer = got
    grads = {k: jnp.stack(v) for k, v in g.items()}
    for k in ("norm_mix", "conv_b", "b_r", "b_i", "lam", "hg_norm", "norm_mlp"):
        grads[k] = grads[k][:, 0]
    grads["lb_logits"] = _lower_bounds_bwd(p["lb_logits"], jnp.concatenate(d_lbs, axis=0))
    grads["norm_final"] = g_norm_final[0]
    return loss8[0, 0], dx.reshape(bl, seq, d), grads, received, received_mixer


def _mixer_grads_by_owner(g, d):
    d8, n_blk, rb = d // N_DEV, d // RG_BLOCK_W, RG_BLOCK_W // N_DEV
    depth = len(g["conv_w"])
    conv_w, w_r, w_i = (jnp.stack(g[k]) for k in SMALL_SHARDED)
    return _pack([conv_w.reshape(depth, CONV_W, N_DEV, d8).transpose(2, 0, 1, 3),
                  w_r.reshape(depth, n_blk, N_DEV, rb, RG_BLOCK_W).transpose(2, 0, 1, 3, 4),
                  w_i.reshape(depth, n_blk, N_DEV, rb, RG_BLOCK_W).transpose(2, 0, 1, 3, 4)],
                 lead=1).astype(BF16)


def _update(p, mom1, mom2, grads, received, received_mixer):
    depth = p["w_in"].shape[0]
    out = {}

    for i, k in enumerate(LARGE_SHARDED):
        shp = p[k].shape
        flat = lambda a: a.reshape(shp[0] * shp[1], shp[2])
        parts = [received[l][i] for l in range(depth)]
        res = _reduce_adamw(parts, flat(p[k]), flat(mom1[k]), flat(mom2[k]), "adamw_" + k)
        out[k] = [r.reshape(shp) for r in res]

    res = _reduce_adamw([received_mixer], *[_pack([src[k] for k in SMALL_SHARDED]) for src in (p, mom1, mom2)],
                        "adamw_mixer")
    shapes = [p[k].shape for k in SMALL_SHARDED]
    for i, vals in enumerate(zip(*[_unpack(r, shapes) for r in res])):
        out[SMALL_SHARDED[i]] = list(vals)

    parts = _exchange([_pack([grads[k] for k in REPLICATED])], "gather_grad_replicated", gather=True)
    res = _reduce_adamw(parts, *[_pack([src[k] for k in REPLICATED]) for src in (p, mom1, mom2)],
                        "adamw_replicated")
    shapes = [p[k].shape for k in REPLICATED]
    for i, vals in enumerate(zip(*[_unpack(r, shapes) for r in res])):
        out[REPLICATED[i]] = list(vals)

    return tuple(out[k][i] for i in range(4) for k in WEIGHTS)


def kernel(x, lb_logits, norm_mix, w_in, conv_w, conv_b, w_r, b_r, w_i, b_i, lam, hg_norm, w_out, norm_mlp, w_up, w_down, norm_final, loss_target, m_lb_logits, m_norm_mix, m_w_in, m_conv_w, m_conv_b, m_w_r, m_b_r, m_w_i, m_b_i, m_lam, m_hg_norm, m_w_out, m_norm_mlp, m_w_up, m_w_down, m_norm_final, v_lb_logits, v_norm_mix, v_w_in, v_conv_w, v_conv_b, v_w_r, v_b_r, v_w_i, v_b_i, v_lam, v_hg_norm, v_w_out, v_norm_mlp, v_w_up, v_w_down, v_norm_final):
    p = dict(lb_logits=lb_logits, norm_mix=norm_mix, w_in=w_in, conv_w=conv_w, conv_b=conv_b, w_r=w_r, b_r=b_r,
             w_i=w_i, b_i=b_i, lam=lam, hg_norm=hg_norm, w_out=w_out, norm_mlp=norm_mlp, w_up=w_up,
             w_down=w_down, norm_final=norm_final)
    mom1 = dict(lb_logits=m_lb_logits, norm_mix=m_norm_mix, w_in=m_w_in, conv_w=m_conv_w, conv_b=m_conv_b,
                w_r=m_w_r, b_r=m_b_r, w_i=m_w_i, b_i=m_b_i, lam=m_lam, hg_norm=m_hg_norm, w_out=m_w_out,
                norm_mlp=m_norm_mlp, w_up=m_w_up, w_down=m_w_down, norm_final=m_norm_final)
    mom2 = dict(lb_logits=v_lb_logits, norm_mix=v_norm_mix, w_in=v_w_in, conv_w=v_conv_w, conv_b=v_conv_b,
                w_r=v_w_r, b_r=v_b_r, w_i=v_w_i, b_i=v_b_i, lam=v_lam, hg_norm=v_hg_norm, w_out=v_w_out,
                norm_mlp=v_norm_mlp, w_up=v_w_up, w_down=v_w_down, norm_final=v_norm_final)
    loss, grad_x, grads, received, received_mixer = _local_step(x, loss_target, p)
    loss = lax.psum(loss, ("x", "y", "c"))
    return (loss, grad_x) + _update(p, mom1, mom2, grads, received, received_mixer)
```

```python
import numpy as np

import jax
import jax.numpy as jnp
from jax import lax
from jax.experimental import pallas as pl
from jax.experimental.pallas import tpu as pltpu

F32 = jnp.float32
BF16 = jnp.bfloat16
MESH_ID = pl.DeviceIdType.MESH

N_DEV = 8
N_MIXER_SEGMENTS = 5
NORM_EPS = 1e-6
RG_C = 8.0
RG_BLOCK_W = 256
CONV_W = 4
HG_DK = 128
F_MIN = 1e-30
HG_CHUNK = 16
SUBLANES = 8
LANES = 128
PACK_ROWS = 16
SCAN_GROUP = 16
ROW_CHUNK = 256
ROW_TILE_WEIGHT_STREAM = 1024
WGRAD_TOKEN_TILE = 2048
VMEM_LIMIT_V7X = 56 * 1024 * 1024

ADAM_LR = 0.001
ADAM_B1 = 0.9
ADAM_B2 = 0.999
ADAM_EPS = 1e-08
ADAM_WD = 0.01
ADAM_STEP = 10

GELU_C = 0.7978845608028654
GELU_K = 0.044715


def _cp(*sem):
    return pltpu.CompilerParams(dimension_semantics=sem, vmem_limit_bytes=VMEM_LIMIT_V7X)


def _row_tile(n, cap):
    if n <= cap:
        return n
    t = cap - cap % 16
    while n % t:
        t -= 16
    return t


def _dot(a, b):
    return jnp.dot(a, b, preferred_element_type=F32)


def _dot_nt(a, b):
    return lax.dot_general(a, b, (((1,), (1,)), ((), ())), preferred_element_type=F32)


def _dot_tn(a, b):
    return lax.dot_general(a, b, (((0,), (0,)), ((), ())), preferred_element_type=F32)


def _sigmoid(x):
    return jax.nn.sigmoid(x)


def _sigmoid_pair(x):
    e = jnp.exp(-jnp.abs(x))
    r = 1.0 / (1.0 + e)
    er = e * r
    pos = x >= 0.0
    return jnp.where(pos, r, er), jnp.where(pos, er, r)


def _log1p_pos(y):
    return jnp.where(y < 0.01, y * (1.0 - y * (0.5 - y * (1.0 / 3.0))), jnp.log(1.0 + y))


def _softplus(x):
    return jnp.maximum(x, 0.0) + _log1p_pos(jnp.exp(-jnp.abs(x)))


def _one_minus_exp(x):
    series = -x * (1.0 + x * 0.5 * (1.0 + x * (1.0 / 3.0) * (1.0 + x * 0.25 * (1.0 + x * 0.2))))
    return jnp.where(x > -0.1, series, 1.0 - jnp.exp(x))


def _gelu_and_grad(x):
    x2 = x * x
    t = jnp.tanh(GELU_C * x * (1.0 + GELU_K * x2))
    g = 0.5 * x * (1.0 + t)
    dg = 0.5 * (1.0 + t) + 0.5 * x * (1.0 - t * t) * GELU_C * (1.0 + 3.0 * GELU_K * x2)
    return g, dg


def _silu_and_grad(x):
    s = _sigmoid(x)
    return x * s, s * (1.0 + x * (1.0 - s))


def _rstd(x):
    return lax.rsqrt(jnp.mean(x * x, axis=-1, keepdims=True) + NORM_EPS)


def _rms_bwd(dh, x, g):
    rstd = _rstd(x)
    xh = x * rstd
    dxh = dh * g
    dx = rstd * (dxh - xh * jnp.mean(dxh * xh, axis=-1, keepdims=True))
    return dx, jnp.sum(dh * xh, axis=0, keepdims=True)


def _shift_rows(x, k):
    n = x.shape[0]
    k = k % n
    return x if k == 0 else pltpu.roll(x, k, axis=0)


def _seg_cumsum(x, seg, reverse=False):
    n = x.shape[0]
    rid = lax.broadcasted_iota(jnp.int32, x.shape, 0) & (seg - 1)
    d = 1
    while d < seg:
        if reverse:
            x = jnp.where(rid < seg - d, x + _shift_rows(x, n - d), x)
        else:
            x = jnp.where(rid >= d, x + _shift_rows(x, d), x)
        d *= 2
    return x


def _group_cumsum_matrix(n, seg):
    row = lax.broadcasted_iota(jnp.int32, (n, n), 0)
    col = lax.broadcasted_iota(jnp.int32, (n, n), 1)
    same_group = (row & ~(seg - 1)) == (col & ~(seg - 1))
    return jnp.where(same_group & (col <= row), 1.0, 0.0).astype(BF16)


def _group_cumsum_mxu(x, tri):
    hi = x.astype(BF16)
    lo = (x - hi.astype(F32)).astype(BF16)
    return _dot(tri, hi) + _dot(tri, lo)


def _scan_rows(a_ref, b_ref, out_ref, n_rows, width, reverse):
    gr = min(SCAN_GROUP, n_rows)
    rid = lax.broadcasted_iota(jnp.int32, (gr, width), 0)
    n_groups = n_rows // gr
    per_trip = min(4, n_groups)
    assert n_groups % per_trip == 0

    def local_scan(g):
        r0 = pl.multiple_of(g * gr, gr)
        a = a_ref[pl.ds(r0, gr), :]
        b = b_ref[pl.ds(r0, gr), :]
        d = 1
        while d < gr:
            if reverse:
                keep = rid < gr - d
                a_sh, b_sh = _shift_rows(a, gr - d), _shift_rows(b, gr - d)
            else:
                keep = rid >= d
                a_sh, b_sh = _shift_rows(a, d), _shift_rows(b, d)
            b = jnp.where(keep, a * b_sh + b, b)
            a = jnp.where(keep, a * a_sh, a)
            d *= 2
        return r0, a, b

    def trip(i, carry):
        first = i * per_trip
        groups = [n_groups - 1 - (first + u) if reverse else first + u for u in range(per_trip)]
        for r0, a, b in [local_scan(g) for g in groups]:
            out = a * carry + b
            out_ref[pl.ds(r0, gr), :] = out
            edge = out[0:1, :] if reverse else out[gr - 1:gr, :]
            carry = jnp.broadcast_to(edge, (gr, width))
        return carry

    lax.fori_loop(0, n_groups // per_trip, trip, jnp.zeros((gr, width), F32))


def _lb_softmax_rows(x_ref, depth):
    rows = [x_ref[pl.ds(l, 1), :] for l in range(depth)]
    top = rows[0]
    for r in rows[1:]:
        top = jnp.maximum(top, r)
    e = [jnp.exp(r - top) for r in rows]
    tot = e[0]
    for r in e[1:]:
        tot = tot + r
    return [r / tot for r in e]


def _lower_bounds_fwd(lb_logits):
    depth, d = lb_logits.shape

    def body(x_ref, o_ref):
        sm = _lb_softmax_rows(x_ref, depth)
        cum = jnp.zeros((1, d), F32)
        for l in range(depth):
            cum = cum + sm[l]
            o_ref[pl.ds(l, 1), :] = jnp.clip(cum - sm[0], 0.0, 1.0)

    return pl.pallas_call(body, name="lower_bounds_fwd",
                          out_shape=jax.ShapeDtypeStruct((depth, d), F32))(lb_logits)


def _lower_bounds_bwd(lb_logits, d_lbs):
    depth, d = lb_logits.shape

    def body(x_ref, g_ref, o_ref):
        sm = _lb_softmax_rows(x_ref, depth)
        cum = jnp.zeros((1, d), F32)
        d_cum = []
        for l in range(depth):
            cum = cum + sm[l]
            v = cum - sm[0]
            d_cum.append(jnp.where((v > 0.0) & (v < 1.0), g_ref[pl.ds(l, 1), :], 0.0))
        d_sm = []
        tail = jnp.zeros((1, d), F32)
        for l in reversed(range(depth)):
            tail = tail + d_cum[l]
            d_sm.append(tail)
        d_sm = d_sm[::-1]
        d_sm[0] = d_sm[0] - tail
        inner = jnp.zeros((1, d), F32)
        for l in range(depth):
            inner = inner + sm[l] * d_sm[l]
        for l in range(depth):
            o_ref[pl.ds(l, 1), :] = sm[l] * (d_sm[l] - inner)

    return pl.pallas_call(body, name="lower_bounds_bwd",
                          out_shape=jax.ShapeDtypeStruct((depth, d), F32))(lb_logits, d_lbs)


def _inproj_fwd(x, gain, w_seg, carry=None):
    t_rows, d = x.shape
    tm = _row_tile(t_rows, ROW_TILE_WEIGHT_STREAM)
    n_gate = N_DEV - N_MIXER_SEGMENTS

    def body(x_ref, g_ref, w_ref, proj_ref, gates_ref, h_ref):
        j = pl.program_id(1)

        @pl.when(j == 0)
        def _():
            xv = x_ref[...]
            h_ref[...] = (xv * _rstd(xv) * g_ref[...]).astype(BF16)

        @pl.when(j < N_MIXER_SEGMENTS)
        def _():
            proj_ref[...] = _dot(h_ref[...], w_ref[...])

        @pl.when(j >= N_MIXER_SEGMENTS)
        def _():
            gates_ref[...] = _dot(h_ref[...], w_ref[...]).astype(BF16)

    return _call_carrying(
        body, carry, name="inproj_fwd", grid=(t_rows // tm, N_DEV),
        in_specs=[pl.BlockSpec((tm, d), lambda i, j: (i, 0)),
                  pl.BlockSpec((1, d), lambda i, j: (0, 0)),
                  pl.BlockSpec((None, d, d), lambda i, j: (j, 0, 0))],
        out_specs=[pl.BlockSpec((None, tm, d), lambda i, j: (jnp.minimum(j, N_MIXER_SEGMENTS - 1), i, 0)),
                   pl.BlockSpec((None, tm, d), lambda i, j: (jnp.maximum(j - N_MIXER_SEGMENTS, 0), i, 0)),
                   pl.BlockSpec((tm, d), lambda i, j: (i, 0))],
        out_shape=[jax.ShapeDtypeStruct((N_MIXER_SEGMENTS, t_rows, d), F32),
                   jax.ShapeDtypeStruct((n_gate, t_rows, d), BF16),
                   jax.ShapeDtypeStruct((t_rows, d), BF16)],
        scratch_shapes=[], semantics=("parallel", "arbitrary"), args=(x, gain, w_seg))


def _merge_out_fwd(gates, y_a, o_n, x, w_out):
    t_rows, d = x.shape
    tm = _row_tile(t_rows, 256)

    def body(g_ref, ma_ref, mb_ref, ya_ref, on_ref, x_ref, w_ref, xmid_ref, y_ref):
        g = g_ref[...].astype(F32)
        ya, on = ya_ref[...].astype(F32), on_ref[...].astype(F32)
        y = (_sigmoid(ma_ref[...].astype(F32)) * ya
             + _sigmoid(mb_ref[...].astype(F32)) * (on * (g * _sigmoid(g))))
        yb = y.astype(BF16)
        y_ref[...] = yb
        xmid_ref[...] = x_ref[...] + _dot(yb, w_ref[...])

    seg = lambda k: pl.BlockSpec((None, tm, d), lambda i, k=k: (k, i, 0))
    row = pl.BlockSpec((tm, d), lambda i: (i, 0))
    return pl.pallas_call(
        body, name="merge_out_fwd", grid=(t_rows // tm,),
        in_specs=[seg(0), seg(1), seg(2), row, row, row, pl.BlockSpec((d, d), lambda i: (0, 0))],
        out_specs=[row, row],
        out_shape=[jax.ShapeDtypeStruct((t_rows, d), F32), jax.ShapeDtypeStruct((t_rows, d), BF16)],
        compiler_params=_cp("parallel"))(gates, gates, gates, y_a, o_n, x, w_out)


def _mlp_fwd(x_mid, gain, w_up, w_down):
    t_rows, d = x_mid.shape
    f8 = w_up.shape[2]
    tm = _row_tile(t_rows, ROW_TILE_WEIGHT_STREAM)
    per_step = 2

    def body(x_ref, g_ref, wu_ref, wd_ref, out_ref, u_ref, h_ref):
        @pl.when(pl.program_id(1) == 0)
        def _():
            xv = x_ref[...]
            h_ref[...] = (xv * _rstd(xv) * g_ref[...]).astype(BF16)
            out_ref[...] = xv

        h = h_ref[...]
        down = None
        for k in range(per_step):
            u = _dot(h, wu_ref[k])
            u_ref[k] = u.astype(BF16)
            r = jnp.maximum(u, 0.0)
            part = _dot((r * r).astype(BF16), wd_ref[k])
            down = part if down is None else down + part
        out_ref[...] += down

    row = pl.BlockSpec((tm, d), lambda i, j: (i, 0))
    return pl.pallas_call(
        body, name="mlp_fwd", grid=(t_rows // tm, N_DEV // per_step),
        in_specs=[row, pl.BlockSpec((1, d), lambda i, j: (0, 0)),
                  pl.BlockSpec((per_step, d, f8), lambda i, j: (j, 0, 0)),
                  pl.BlockSpec((per_step, f8, d), lambda i, j: (j, 0, 0))],
        out_specs=[row, pl.BlockSpec((per_step, tm, f8), lambda i, j: (j, i, 0)), row],
        out_shape=[jax.ShapeDtypeStruct((t_rows, d), F32),
                   jax.ShapeDtypeStruct((N_DEV, t_rows, f8), BF16),
                   jax.ShapeDtypeStruct((t_rows, d), BF16)],
        compiler_params=_cp("parallel", "arbitrary"))(x_mid, gain, w_up, w_down)


def _loss_head(x, gain, target):
    t_rows, d = x.shape
    tm = _row_tile(t_rows, 512)

    def body(x_ref, g_ref, t_ref, loss_ref, dx_ref, dxb_ref, dg_ref):
        @pl.when(pl.program_id(0) == 0)
        def _():
            loss_ref[...] = jnp.zeros_like(loss_ref)
            dg_ref[...] = jnp.zeros_like(dg_ref)

        xv = x_ref[...]
        g = g_ref[...]
        err = xv * _rstd(xv) * g - t_ref[...]
        loss_ref[...] += (0.5 / d) * jnp.sum(err * err)
        dx, dg = _rms_bwd(err * (1.0 / d), xv, g)
        dx_ref[...] = dx
        dxb_ref[...] = dx.astype(BF16)
        dg_ref[...] += dg

    row = pl.BlockSpec((tm, d), lambda i: (i, 0))
    vec = pl.BlockSpec((1, d), lambda i: (0, 0))
    return pl.pallas_call(
        body, name="loss_head", grid=(t_rows // tm,),
        in_specs=[row, vec, row],
        out_specs=[pl.BlockSpec((SUBLANES, LANES), lambda i: (0, 0)), row, row, vec],
        out_shape=[jax.ShapeDtypeStruct((SUBLANES, LANES), F32),
                   jax.ShapeDtypeStruct((t_rows, d), F32),
                   jax.ShapeDtypeStruct((t_rows, d), BF16),
                   jax.ShapeDtypeStruct((1, d), F32)],
        compiler_params=_cp("arbitrary"))(x, gain, target)


def _mlp_bwd(d_out, d_out_b, u, x_mid, gain, w_up, w_down):
    t_rows, d = x_mid.shape
    f8 = w_up.shape[2]
    tm = _row_tile(t_rows, ROW_TILE_WEIGHT_STREAM)
    sub = _row_tile(tm, ROW_CHUNK)

    def body(do_ref, dob_ref, u_ref, x_ref, g_ref, wu_ref, wd_ref, dx_ref, dxb_ref, du_ref, act_ref, dg_ref):
        j = pl.program_id(1)

        @pl.when((pl.program_id(0) == 0) & (j == 0))
        def _():
            dg_ref[...] = jnp.zeros_like(dg_ref)

        @pl.when(j == 0)
        def _():
            dx_ref[...] = jnp.zeros_like(dx_ref)

        r = jnp.maximum(u_ref[...].astype(F32), 0.0)
        act_ref[...] = (r * r).astype(BF16)
        du = (_dot_nt(dob_ref[...], wd_ref[...]) * (2.0 * r)).astype(BF16)
        du_ref[...] = du
        dx_ref[...] += _dot_nt(du, wu_ref[...])

        @pl.when(j == N_DEV - 1)
        def _():
            def finish(c, _):
                rows = pl.ds(pl.multiple_of(c * sub, sub), sub)
                dx, dg = _rms_bwd(dx_ref[rows, :], x_ref[rows, :], g_ref[...])
                dx = dx + do_ref[rows, :]
                dx_ref[rows, :] = dx
                dxb_ref[rows, :] = dx.astype(BF16)
                dg_ref[...] += dg
                return 0

            lax.fori_loop(0, tm // sub, finish, 0)

    row = pl.BlockSpec((tm, d), lambda i, j: (i, 0))
    vec = pl.BlockSpec((1, d), lambda i, j: (0, 0))
    hid = pl.BlockSpec((None, tm, f8), lambda i, j: (j, i, 0))
    return pl.pallas_call(
        body, name="mlp_bwd", grid=(t_rows // tm, N_DEV),
        in_specs=[row, row, hid, row, vec,
                  pl.BlockSpec((None, d, f8), lambda i, j: (j, 0, 0)),
                  pl.BlockSpec((None, f8, d), lambda i, j: (j, 0, 0))],
        out_specs=[row, row, hid, hid, vec],
        out_shape=[jax.ShapeDtypeStruct((t_rows, d), F32),
                   jax.ShapeDtypeStruct((t_rows, d), BF16),
                   jax.ShapeDtypeStruct((N_DEV, t_rows, f8), BF16),
                   jax.ShapeDtypeStruct((N_DEV, t_rows, f8), BF16),
                   jax.ShapeDtypeStruct((1, d), F32)],
        compiler_params=_cp("arbitrary", "arbitrary"))(d_out, d_out_b, u, x_mid, gain, w_up, w_down)


def _outproj_bwd(dx_mid_b, w_out, gates, y_a, o_n):
    t_rows, d = y_a.shape
    tm = _row_tile(t_rows, 256)

    def body(dx_ref, w_ref, g_ref, ma_ref, mb_ref, ya_ref, on_ref, dya_ref, don_ref, dp_ref):
        dy = _dot_nt(dx_ref[...], w_ref[...])
        sa = _sigmoid(ma_ref[...].astype(F32))
        sb = _sigmoid(mb_ref[...].astype(F32))
        sg, dsg = _silu_and_grad(g_ref[...].astype(F32))
        ya = ya_ref[...].astype(F32)
        on = on_ref[...].astype(F32)
        dya_ref[...] = dy * sa
        t = dy * sb
        don_ref[...] = t * sg
        dp_ref[0] = (t * on * dsg).astype(BF16)
        dp_ref[1] = (dy * ya * sa * (1.0 - sa)).astype(BF16)
        dp_ref[2] = (dy * on * sg * sb * (1.0 - sb)).astype(BF16)

    seg = lambda k: pl.BlockSpec((None, tm, d), lambda i, k=k: (k, i, 0))
    row = pl.BlockSpec((tm, d), lambda i: (i, 0))
    return pl.pallas_call(
        body, name="outproj_bwd", grid=(t_rows // tm,),
        in_specs=[row, pl.BlockSpec((d, d), lambda i: (0, 0)), seg(0), seg(1), seg(2), row, row],
        out_specs=[row, row, pl.BlockSpec((3, tm, d), lambda i: (0, i, 0))],
        out_shape=[jax.ShapeDtypeStruct((t_rows, d), F32),
                   jax.ShapeDtypeStruct((t_rows, d), F32),
                   jax.ShapeDtypeStruct((3, t_rows, d), BF16)],
        compiler_params=_cp("parallel"))(dx_mid_b, w_out, gates, gates, gates, y_a, o_n)


def _inproj_bwd(dx_mid, dp_a, dp_b, dp_c, w_seg, x_in, gain, carry=None):
    t_rows, d = x_in.shape
    tm = _row_tile(t_rows, ROW_TILE_WEIGHT_STREAM)
    sub = _row_tile(tm, ROW_CHUNK)
    n_a, n_b = dp_a.shape[0], dp_b.shape[0]

    def body(dxm_ref, a_ref, b_ref, c_ref, w_ref, x_ref, g_ref, dx_ref, dxb_ref, dg_ref):
        j = pl.program_id(1)

        @pl.when((pl.program_id(0) == 0) & (j == 0))
        def _():
            dg_ref[...] = jnp.zeros_like(dg_ref)

        @pl.when(j == 0)
        def _():
            dx_ref[...] = jnp.zeros_like(dx_ref)

        @pl.when(j < n_a)
        def _():
            dx_ref[...] += _dot_nt(a_ref[...], w_ref[...])

        @pl.when((j >= n_a) & (j < n_a + n_b))
        def _():
            dx_ref[...] += _dot_nt(b_ref[...], w_ref[...])

        @pl.when(j >= n_a + n_b)
        def _():
            dx_ref[...] += _dot_nt(c_ref[...], w_ref[...])

        @pl.when(j == N_DEV - 1)
        def _():
            def finish(c, _):
                rows = pl.ds(pl.multiple_of(c * sub, sub), sub)
                dx, dg = _rms_bwd(dx_ref[rows, :], x_ref[rows, :], g_ref[...])
                dx = dx + dxm_ref[rows, :]
                dx_ref[rows, :] = dx
                dxb_ref[rows, :] = dx.astype(BF16)
                dg_ref[...] += dg
                return 0

            lax.fori_loop(0, tm // sub, finish, 0)

    def part(first, n):
        return pl.BlockSpec((None, tm, d), lambda i, j: (jnp.clip(j - first, 0, n - 1), i, 0))

    row = pl.BlockSpec((tm, d), lambda i, j: (i, 0))
    vec = pl.BlockSpec((1, d), lambda i, j: (0, 0))
    return _call_carrying(
        body, carry, name="inproj_bwd", grid=(t_rows // tm, N_DEV),
        in_specs=[row, part(0, n_a), part(n_a, n_b), part(n_a + n_b, dp_c.shape[0]),
                  pl.BlockSpec((None, d, d), lambda i, j: (j, 0, 0)), row, vec],
        out_specs=[row, row, vec],
        out_shape=[jax.ShapeDtypeStruct((t_rows, d), F32),
                   jax.ShapeDtypeStruct((t_rows, d), BF16),
                   jax.ShapeDtypeStruct((1, d), F32)],
        scratch_shapes=[], semantics=("arbitrary", "arbitrary"),
        args=(dx_mid, dp_a, dp_b, dp_c, w_seg, x_in, gain))


def _wgrad(a3, b3, name):
    n_a, t_rows, k_a = a3.shape
    n_b, _, n_cols = b3.shape
    n = max(n_a, n_b)
    bk = _row_tile(k_a, 1024)
    bn = n_cols if n_cols <= 1024 else 1024
    tt = _row_tile(t_rows, WGRAD_TOKEN_TILE)
    n_t = t_rows // tt

    def body(a_ref, b_ref, o_ref, acc_ref):
        t, j = pl.program_id(2), pl.program_id(3)
        part = _dot_tn(a_ref[...], b_ref[...])

        @pl.when(t == 0)
        def _():
            acc_ref[j] = part

        @pl.when(t > 0)
        def _():
            acc_ref[j] += part

        @pl.when(t == n_t - 1)
        def _():
            o_ref[...] = acc_ref[j].astype(BF16)

    def out_map(p, q, t, j):
        return (jnp.where(t == n_t - 1, j, 0), p, q)

    return pl.pallas_call(
        body, name=name, grid=(k_a // bk, n_cols // bn, n_t, n),
        in_specs=[pl.BlockSpec((None, tt, bk), lambda p, q, t, j: (j if n_a > 1 else 0, t, p)),
                  pl.BlockSpec((None, tt, bn), lambda p, q, t, j: (j if n_b > 1 else 0, t, q))],
        out_specs=pl.BlockSpec((None, bk, bn), out_map),
        out_shape=jax.ShapeDtypeStruct((n, k_a, n_cols), BF16),
        scratch_shapes=[pltpu.VMEM((n, bk, bn), F32)],
        compiler_params=_cp("parallel", "parallel", "arbitrary", "arbitrary"))(a3, b3)


def _conv_taps(xe, n):
    return [_shift_rows(xe, CONV_W - 1 - j)[SUBLANES:SUBLANES + n, :] for j in range(CONV_W)]


def _rg_gates(xc, w_r, b_r, w_i, b_i, sp8):
    xb = xc.astype(BF16)
    r = _sigmoid(_dot(xb, w_r) + b_r)
    i = _sigmoid(_dot(xb, w_i) + b_i)
    return r, i


def _mixer_a_fwd(proj, conv_w, conv_b, w_r, b_r, w_i, b_i, lam, seq):
    _, t_rows, d = proj.shape
    n_seq, n_blk = t_rows // seq, d // RG_BLOCK_W
    wb = RG_BLOCK_W
    ch = _row_tile(seq, ROW_CHUNK)

    def body(xa_ref, ga_ref, cw_ref, cb_ref, wr_ref, br_ref, wi_ref, bi_ref, lam_ref, h_ref, ya_ref,
             xpad, a_s, u_s):
        xpad[0:SUBLANES, :] = jnp.zeros((SUBLANES, wb), F32)
        xpad[SUBLANES:, :] = xa_ref[...]
        sp8 = RG_C * _softplus(-lam_ref[...])

        def gates(c, _):
            r0 = pl.multiple_of(c * ch, ch)
            taps = _conv_taps(xpad[pl.ds(r0, ch + SUBLANES), :], ch)
            xc = cb_ref[...] + sum(cw_ref[pl.ds(j, 1), :] * taps[j] for j in range(CONV_W))
            r, i = _rg_gates(xc, wr_ref[...], br_ref[...], wi_ref[...], bi_ref[...], sp8)
            log_a = -(r * sp8)
            a_s[pl.ds(r0, ch), :] = jnp.exp(log_a)
            u_s[pl.ds(r0, ch), :] = jnp.sqrt(jnp.maximum(_one_minus_exp(2.0 * log_a), 0.0)) * (i * xc)
            return 0

        lax.fori_loop(0, seq // ch, gates, 0)
        _scan_rows(a_s, u_s, h_ref, seq, wb, reverse=False)

        def gate_out(c, _):
            r0 = pl.multiple_of(c * ch, ch)
            gl, _ = _gelu_and_grad(ga_ref[pl.ds(r0, ch), :])
            ya_ref[pl.ds(r0, ch), :] = (h_ref[pl.ds(r0, ch), :] * gl).astype(BF16)
            return 0

        lax.fori_loop(0, seq // ch, gate_out, 0)

    seg = lambda k: pl.BlockSpec((None, seq, wb), lambda s, b, k=k: (k, s, b))
    blk = pl.BlockSpec((seq, wb), lambda s, b: (s, b))
    vec = pl.BlockSpec((1, wb), lambda s, b: (0, b))
    wsp = pl.BlockSpec((None, wb, wb), lambda s, b: (b, 0, 0))
    return pl.pallas_call(
        body, name="mixer_a_fwd", grid=(n_seq, n_blk),
        in_specs=[seg(0), seg(1), pl.BlockSpec((CONV_W, wb), lambda s, b: (0, b)), vec, wsp, vec, wsp, vec, vec],
        out_specs=[blk, blk],
        out_shape=[jax.ShapeDtypeStruct((t_rows, d), F32), jax.ShapeDtypeStruct((t_rows, d), BF16)],
        scratch_shapes=[pltpu.VMEM((seq + SUBLANES, wb), F32), pltpu.VMEM((seq, wb), F32),
                        pltpu.VMEM((seq, wb), F32)],
        compiler_params=_cp("parallel", "parallel"))(proj, proj, conv_w, conv_b, w_r, b_r, w_i, b_i, lam)


def _mixer_a_bwd(proj, h, d_ya, conv_w, conv_b, w_r, b_r, w_i, b_i, lam, seq):
    _, t_rows, d = proj.shape
    n_seq, n_blk = t_rows // seq, d // RG_BLOCK_W
    wb = RG_BLOCK_W
    ch = _row_tile(seq, ROW_CHUNK)
    n_ch = seq // ch

    def body(xa_ref, ga_ref, h_ref, dya_ref, cw_ref, cb_ref, wr_ref, br_ref, wi_ref, bi_ref, lam_ref,
             dp_ref, dwr_ref, dwi_ref, dbr_ref, dbi_ref, dlam_ref, dcw_ref, dcb_ref,
             xpad, hpad, a_s, e_pad, g_s, xc_s, r_s, i_s, dxc_pad):
        @pl.when(pl.program_id(1) == 0)
        def _():
            for ref in (dwr_ref, dwi_ref, dbr_ref, dbi_ref, dlam_ref, dcw_ref, dcb_ref):
                ref[...] = jnp.zeros_like(ref)

        zeros8 = jnp.zeros((SUBLANES, wb), F32)
        xpad[0:SUBLANES, :] = zeros8
        xpad[SUBLANES:, :] = xa_ref[...]
        hpad[0:SUBLANES, :] = zeros8
        hpad[SUBLANES:, :] = h_ref[...]
        e_pad[seq:, :] = zeros8
        dxc_pad[seq:, :] = zeros8
        lam_v = lam_ref[...]
        sp8 = RG_C * _softplus(-lam_v)

        def recompute(c, _):
            r0 = pl.multiple_of(c * ch, ch)
            rows = pl.ds(r0, ch)
            taps = _conv_taps(xpad[pl.ds(r0, ch + SUBLANES), :], ch)
            xc = cb_ref[...] + sum(cw_ref[pl.ds(j, 1), :] * taps[j] for j in range(CONV_W))
            r, i = _rg_gates(xc, wr_ref[...], br_ref[...], wi_ref[...], bi_ref[...], sp8)
            a = jnp.exp(-(r * sp8))
            gl, dgl = _gelu_and_grad(ga_ref[rows, :])
            dya = dya_ref[rows, :]
            g = dya * gl
            dp_ref[1, rows, :] = (dya * h_ref[rows, :] * dgl).astype(BF16)
            a_s[rows, :] = a
            e_pad[rows, :] = a * g
            g_s[rows, :] = g
            xc_s[rows, :] = xc
            r_s[rows, :] = r
            i_s[rows, :] = i
            return 0

        lax.fori_loop(0, n_ch, recompute, 0)
        _scan_rows(a_s, e_pad, e_pad, seq, wb, reverse=True)

        def grads(c, _):
            r0 = pl.multiple_of(c * ch, ch)
            rows = pl.ds(r0, ch)
            halo = pl.ds(r0, ch + SUBLANES)
            dh = g_s[rows, :] + _shift_rows(e_pad[halo, :], ch + SUBLANES - 1)[0:ch, :]
            h_prev = _shift_rows(hpad[halo, :], 1)[SUBLANES:, :]
            xc, r, i = xc_s[rows, :], r_s[rows, :], i_s[rows, :]
            log_a = -(r * sp8)
            a = jnp.exp(log_a)
            om = _one_minus_exp(2.0 * log_a)
            sq = jnp.sqrt(jnp.maximum(om, 0.0))
            t1 = dh * xc
            d_i = t1 * sq
            d_la = dh * h_prev * a + jnp.where(om > 0.0, -(t1 * i) * (1.0 - om) / sq, 0.0)
            dpr = -(d_la * sp8) * r * (1.0 - r)
            dpi = d_i * i * (1.0 - i)
            dprb, dpib, xb = dpr.astype(BF16), dpi.astype(BF16), xc.astype(BF16)
            dxc = dh * sq * i + _dot_nt(dprb, wr_ref[...]) + _dot_nt(dpib, wi_ref[...])
            dwr_ref[...] += _dot_tn(xb, dprb)
            dwi_ref[...] += _dot_tn(xb, dpib)
            dbr_ref[...] += jnp.sum(dpr, axis=0, keepdims=True)
            dbi_ref[...] += jnp.sum(dpi, axis=0, keepdims=True)
            dlam_ref[...] += jnp.sum(d_la * r, axis=0, keepdims=True) * (RG_C * _sigmoid(-lam_v))
            dcb_ref[...] += jnp.sum(dxc, axis=0, keepdims=True)
            taps = _conv_taps(xpad[halo, :], ch)
            for j in range(CONV_W):
                dcw_ref[pl.ds(j, 1), :] += jnp.sum(dxc * taps[j], axis=0, keepdims=True)
            dxc_pad[rows, :] = dxc
            return 0

        lax.fori_loop(0, n_ch, grads, 0)

        def conv_bwd(c, _):
            r0 = pl.multiple_of(c * ch, ch)
            de = dxc_pad[pl.ds(r0, ch + SUBLANES), :]
            dxa = sum(cw_ref[pl.ds(j, 1), :] * _shift_rows(de, ch + SUBLANES - (CONV_W - 1 - j))[0:ch, :]
                      for j in range(CONV_W))
            dp_ref[0, pl.ds(r0, ch), :] = dxa.astype(BF16)
            return 0

        lax.fori_loop(0, n_ch, conv_bwd, 0)

    seg = lambda k: pl.BlockSpec((None, seq, wb), lambda b, s, k=k: (k, s, b))
    blk = pl.BlockSpec((seq, wb), lambda b, s: (s, b))
    vec = pl.BlockSpec((1, wb), lambda b, s: (0, b))
    taps = pl.BlockSpec((CONV_W, wb), lambda b, s: (0, b))
    wsp = pl.BlockSpec((None, wb, wb), lambda b, s: (b, 0, 0))
    vec_shape = jax.ShapeDtypeStruct((1, d), F32)
    w_shape = jax.ShapeDtypeStruct((n_blk, wb, wb), F32)
    pad = pltpu.VMEM((seq + SUBLANES, wb), F32)
    full = pltpu.VMEM((seq, wb), F32)
    return pl.pallas_call(
        body, name="mixer_a_bwd", grid=(n_blk, n_seq),
        in_specs=[seg(0), seg(1), blk, blk, taps, vec, wsp, vec, wsp, vec, vec],
        out_specs=[pl.BlockSpec((2, seq, wb), lambda b, s: (0, s, b)), wsp, wsp, vec, vec, vec, taps, vec],
        out_shape=[jax.ShapeDtypeStruct((2, t_rows, d), BF16), w_shape, w_shape, vec_shape, vec_shape,
                   vec_shape, jax.ShapeDtypeStruct((CONV_W, d), F32), vec_shape],
        scratch_shapes=[pad, pad, full, pad, full, full, full, full, pad],
        compiler_params=_cp("parallel", "arbitrary"))(
            proj, proj, h, d_ya, conv_w, conv_b, w_r, b_r, w_i, b_i, lam)


def _hg_prepare(q_ref, z_ref, lb, rows):
    z = z_ref[rows, :]
    sig, nsig = _sigmoid_pair(z)
    fg = lb + (1.0 - lb) * sig
    log_f = jnp.log(jnp.maximum(fg, F_MIN))
    key = (1.0 - lb) * nsig
    qs, _ = _silu_and_grad(q_ref[rows, :])
    return qs, key, log_f, sig, fg


HG_UNROLL_TERMS = 32
HG_UNROLL_FWD = 32
HG_UNROLL_BWD = 32
HG_HALF = HG_CHUNK // 2
HG_STACK = HG_CHUNK * HG_HALF
assert HG_HALF == SUBLANES


def _half_of(x, s):
    return x[:HG_HALF, :] if s < HG_HALF else x[HG_HALF:, :]


def _hg_decay(g_ref, r0, g_rows, first_row, s):
    rid = lax.broadcasted_iota(jnp.int32, g_rows.shape, 0) + first_row
    gs = g_ref[pl.ds(r0 + s, 1), :]
    return jnp.where(rid >= s, jnp.exp(g_rows - gs), 0.0)


def _half_start(s):
    return 0 if s < HG_HALF else HG_HALF


def _hg_cross_decays(gc):
    rid = lax.broadcasted_iota(jnp.int32, (HG_CHUNK, HG_DK), 0)
    g_mid = gc[HG_HALF - 1:HG_HALF, :]
    e_hi = jnp.where(rid >= HG_HALF, jnp.exp(gc - g_mid), 0.0)
    e_lo = jnp.where(rid < HG_HALF, jnp.exp(g_mid - gc), 0.0)
    return e_hi, e_lo


def _hg_cross(qc, kc, gc):
    e_hi, e_lo = _hg_cross_decays(gc)
    return (qc * e_hi).astype(BF16), (kc * e_lo).astype(BF16)


def _stack(slabs):
    return jnp.concatenate(slabs, axis=0).astype(BF16)


def _slab_row_sums():
    row = lax.broadcasted_iota(jnp.int32, (HG_CHUNK, HG_STACK), 0)
    col = lax.broadcasted_iota(jnp.int32, (HG_CHUNK, HG_STACK), 1)
    lo = row * HG_HALF
    return jnp.where((col >= lo) & (col < lo + HG_HALF), 1.0, 0.0).astype(BF16)


def _for_chunks(n, unroll, *stages, after_trip=None):
    unroll = min(unroll, n)
    assert n % unroll == 0

    def trip(i, _):
        chunks = [i * unroll + u for u in range(unroll)]
        carried = [stages[0](c) for c in chunks]
        for stage in stages[1:]:
            carried = [stage(c, x) for c, x in zip(chunks, carried)]
        if after_trip is not None:
            after_trip(carried)
        return 0

    lax.fori_loop(0, n // unroll, trip, 0)


def _hg_state_chain(states, g_ref, carry_ref, n_chunks, reverse):
    unroll = min(8, n_chunks)
    assert n_chunks % unroll == 0
    carry_ref[...] = jnp.zeros_like(carry_ref)

    def trip(i, _):
        st = carry_ref[...]
        for u in range(unroll):
            k = i * unroll + u
            c = n_chunks - 1 - k if reverse else k
            term = states[c]
            states[c] = st
            st = st * jnp.exp(g_ref[pl.ds(c * HG_CHUNK + HG_CHUNK - 1, 1), :]) + term
        carry_ref[...] = st
        return 0

    lax.fori_loop(0, n_chunks // unroll, trip, 0)


def _hgrn_fwd(proj, lower_bound, hg_gain, seq, carry=None):
    _, t_rows, d = proj.shape
    n_seq, n_head = t_rows // seq, d // HG_DK
    ch = _row_tile(seq, ROW_CHUNK)
    n_chunks = seq // HG_CHUNK

    def body(q_ref, z_ref, v_ref, lb_ref, gain_ref, o_ref, on_ref, qs_s, k_s, g_s, states, st_ref):
        lb = lb_ref[...]
        tri = _group_cumsum_matrix(ch, HG_CHUNK)

        per_block = ch // HG_CHUNK

        def prepare(c):
            rows = pl.ds(pl.multiple_of(c * ch, ch), ch)
            qs, key, log_f, _, _ = _hg_prepare(q_ref, z_ref, lb, rows)
            qs_s[rows, :] = qs
            k_s[rows, :] = key
            return key, _group_cumsum_mxu(log_f, tri)

        def state_terms(c, carried):
            key, g = carried
            r0 = pl.multiple_of(c * ch, ch)
            g_s[pl.ds(r0, ch), :] = g
            terms = []
            for u in range(per_block):
                sl = slice(u * HG_CHUNK, (u + 1) * HG_CHUNK)
                k_end = key[sl, :] * jnp.exp(g[(u + 1) * HG_CHUNK - 1:(u + 1) * HG_CHUNK, :] - g[sl, :])
                vc = v_ref[pl.ds(r0 + u * HG_CHUNK, HG_CHUNK), :]
                terms.append(_dot_tn(vc.astype(BF16), k_end.astype(BF16)))
            return terms

        def store_terms(c, terms):
            for u, term in enumerate(terms):
                states[c * per_block + u] = term

        _for_chunks(seq // ch, 2, prepare, state_terms, store_terms)
        _hg_state_chain(states, g_s, st_ref, n_chunks, reverse=False)
        ones = jnp.ones((HG_DK, HG_DK), BF16)

        def issue(c):
            r0 = pl.multiple_of(c * HG_CHUNK, HG_CHUNK)
            rows = pl.ds(r0, HG_CHUNK)
            qc, gc = qs_s[rows, :], g_s[rows, :]
            o = _dot_nt((qc * jnp.exp(gc)).astype(BF16), states[c].astype(BF16))
            pairs = [qc[_half_start(s):, :] * _hg_decay(g_s, r0, gc[_half_start(s):, :], _half_start(s), s)
                     * k_s[pl.ds(r0 + s, 1), :] for s in range(HG_CHUNK)]
            score = _dot(_stack(pairs), ones)
            return o, score

        def combine(c, issued):
            o, score = issued
            r0 = pl.multiple_of(c * HG_CHUNK, HG_CHUNK)
            halves = [o[:HG_HALF, :], o[HG_HALF:, :]]
            first = 0
            for s in range(HG_CHUNK):
                vs = v_ref[pl.ds(r0 + s, 1), :]
                if s < HG_HALF:
                    halves[0] += score[first:first + HG_HALF, :] * vs
                    first += HG_HALF
                halves[1] += score[first:first + HG_HALF, :] * vs
                first += HG_HALF
            o = jnp.concatenate(halves, axis=0)
            o_ref[pl.ds(r0, HG_CHUNK), :] = o
            on_ref[pl.ds(r0, HG_CHUNK), :] = (o * _rstd(o) * gain_ref[...]).astype(BF16)

        _for_chunks(n_chunks, HG_UNROLL_FWD, issue, combine)

    seg = lambda k: pl.BlockSpec((None, seq, HG_DK), lambda s, h, k=k: (k, s, h))
    blk = pl.BlockSpec((seq, HG_DK), lambda s, h: (s, h))
    full = pltpu.VMEM((seq, HG_DK), F32)
    return _call_carrying(
        body, carry, name="hgrn_fwd", grid=(n_seq, n_head),
        in_specs=[seg(2), seg(3), seg(4), pl.BlockSpec((1, HG_DK), lambda s, h: (0, h)),
                  pl.BlockSpec((1, HG_DK), lambda s, h: (0, 0))],
        out_specs=[blk, blk],
        out_shape=[jax.ShapeDtypeStruct((t_rows, d), F32), jax.ShapeDtypeStruct((t_rows, d), BF16)],
        scratch_shapes=[full, full, full, pltpu.VMEM((n_chunks, HG_DK, HG_DK), F32),
                        pltpu.VMEM((HG_DK, HG_DK), F32)],
        semantics=("parallel", "parallel"), args=(proj, proj, proj, lower_bound, hg_gain))


def _hgrn_bwd(proj, lower_bound, hg_gain, o, d_on, seq, carry=None):
    _, t_rows, d = proj.shape
    n_seq, n_head = t_rows // seq, d // HG_DK
    ch = _row_tile(seq, ROW_CHUNK)
    n_chunks = seq // HG_CHUNK
    cc = HG_CHUNK

    def body(q_ref, z_ref, v_ref, lb_ref, gain_ref, o_ref, don_ref, dp_ref, dlb_ref, dgain_ref,
             qs_s, k_s, g_s, do_s, dlb_acc, states, dstates, carry_ref):
        hh, ss = pl.program_id(0), pl.program_id(1)
        lb = lb_ref[...]

        @pl.when(ss == 0)
        def _():
            dlb_ref[...] = jnp.zeros_like(dlb_ref)

        @pl.when((ss == 0) & (hh == 0))
        def _():
            dgain_ref[...] = jnp.zeros_like(dgain_ref)

        tri = _group_cumsum_matrix(ch, cc)

        def prepare(c):
            rows = pl.ds(pl.multiple_of(c * ch, ch), ch)
            qs, key, log_f, _, _ = _hg_prepare(q_ref, z_ref, lb, rows)
            qs_s[rows, :] = qs
            k_s[rows, :] = key
            do, dgain = _rms_bwd(don_ref[rows, :], o_ref[rows, :], gain_ref[...])
            do_s[rows, :] = do
            dgain_ref[...] += dgain
            return _group_cumsum_mxu(log_f, tri)

        def store_cumsum(c, g):
            g_s[pl.ds(pl.multiple_of(c * ch, ch), ch), :] = g

        _for_chunks(seq // ch, 4, prepare, store_cumsum)

        def chain_terms(c):
            rows = pl.ds(pl.multiple_of(c * cc, cc), cc)
            gc = g_s[rows, :]
            k_end = k_s[rows, :] * jnp.exp(gc[cc - 1:cc, :] - gc)
            q_in = qs_s[rows, :] * jnp.exp(gc)
            return (_dot_tn(v_ref[rows, :].astype(BF16), k_end.astype(BF16)),
                    _dot_tn(do_s[rows, :].astype(BF16), q_in.astype(BF16)))

        def store_terms(c, terms):
            states[c], dstates[c] = terms

        _for_chunks(n_chunks, HG_UNROLL_TERMS, chain_terms, store_terms)
        _hg_state_chain(states, g_s, carry_ref, n_chunks, reverse=False)
        _hg_state_chain(dstates, g_s, carry_ref, n_chunks, reverse=True)
        ones = jnp.ones((HG_DK, HG_DK), BF16)
        row_sums = _slab_row_sums()

        def chunk_rows(c):
            r0 = pl.multiple_of(c * cc, cc)
            return r0, pl.ds(r0, cc)

        def through_state(c):
            r0, rows = chunk_rows(c)
            qc, kc, gc, vc, doc = qs_s[rows, :], k_s[rows, :], g_s[rows, :], v_ref[rows, :], do_s[rows, :]
            st, dst = states[c], dstates[c]
            g_last = gc[cc - 1:cc, :]
            e_last, e_end = jnp.exp(g_last), jnp.exp(g_last - gc)
            dob, vcb, dstb = doc.astype(BF16), vc.astype(BF16), dst.astype(BF16)
            dqs = _dot(dob, st.astype(BF16))
            dk_state = _dot(vcb, dstb)
            dv = _dot_nt((kc * e_end).astype(BF16), dstb)
            cots = [_half_of(doc, s) * v_ref[pl.ds(r0 + s, 1), :] for s in range(cc)]
            d_score = _dot(_stack(cots), ones)
            x, y = _hg_cross(qc, kc, gc)
            cross = (_dot_nt(dob, vcb), _dot_nt(vcb, dob), _dot_nt(y, x))
            return dqs, dk_state, dv, d_score, e_last * jnp.sum(dst * st, axis=0, keepdims=True), cross

        def pair_terms(c, carried):
            dqs, dk_state, dv, d_score, d_glast, (da_cross, da_cross_t, a_cross_t) = carried
            r0, rows = chunk_rows(c)
            qc, kc, gc = qs_s[rows, :], k_s[rows, :], g_s[rows, :]
            dqs = dqs * jnp.exp(gc)
            dk_state = dk_state * jnp.exp(gc[cc - 1:cc, :] - gc)
            d_glast = d_glast + jnp.sum(kc * dk_state, axis=0, keepdims=True)
            dqs_half = [dqs[:HG_HALF, :], dqs[HG_HALF:, :]]
            pairs, dk_terms = [], []
            for s in range(cc):
                qv = _half_of(qc, s)
                decay = _hg_decay(g_s, r0, _half_of(gc, s), _half_start(s), s)
                ks = k_s[pl.ds(r0 + s, 1), :]
                da_decay = d_score[s * HG_HALF:(s + 1) * HG_HALF, :] * decay
                pairs.append(qv * decay * ks)
                dk_terms.append(da_decay * qv)
                dqs_half[s // HG_HALF] += da_decay * ks
            score = _dot(_stack(pairs), ones)
            dk = dk_state + _dot(row_sums, _stack(dk_terms))
            x, y = _hg_cross(qc, kc, gc)
            cross = (_dot(da_cross.astype(BF16), y), _dot(da_cross_t.astype(BF16), x),
                     _dot(a_cross_t.astype(BF16), do_s[rows, :].astype(BF16)))
            return jnp.concatenate(dqs_half, axis=0), dk, dv, score, d_glast, cross

        def value_terms(c, carried):
            dqs, dk, dv, score, d_glast, (dx_cross, dy_cross, dv_cross) = carried
            _, rows = chunk_rows(c)
            doc, gc = do_s[rows, :], g_s[rows, :]
            dv_terms = [score[s * HG_HALF:(s + 1) * HG_HALF, :] * _half_of(doc, s) for s in range(cc)]
            e_hi, e_lo = _hg_cross_decays(gc)
            return (dqs + dx_cross * e_hi, dk + dy_cross * e_lo,
                    dv + dv_cross + _dot(row_sums, _stack(dv_terms)), d_glast)

        dlb_acc[...] = jnp.zeros_like(dlb_acc)

        def store(c, x):
            dqs, dk, dv, d_glast = x
            _, rows = chunk_rows(c)
            d_g = qs_s[rows, :] * dqs - k_s[rows, :] * dk
            d_logf = _seg_cumsum(d_g, cc, reverse=True) + d_glast
            sig, nsig = _sigmoid_pair(z_ref[rows, :])
            fg = lb + (1.0 - lb) * sig
            _, dsilu = _silu_and_grad(q_ref[rows, :])
            d_gate = jnp.where(fg > F_MIN, d_logf / fg, 0.0) - dk
            dp_ref[0, rows, :] = (dqs * dsilu).astype(BF16)
            dp_ref[1, rows, :] = (d_gate * (1.0 - lb) * sig * nsig).astype(BF16)
            dp_ref[2, rows, :] = dv.astype(BF16)
            return d_gate * nsig

        def add_lower_bound_grads(per_chunk):
            while len(per_chunk) > 1:
                per_chunk = [a + b for a, b in zip(per_chunk[::2], per_chunk[1::2])]
            dlb_acc[...] += per_chunk[0]

        _for_chunks(n_chunks, HG_UNROLL_BWD, through_state, pair_terms, value_terms, store,
                    after_trip=add_lower_bound_grads)
        dlb_ref[...] += jnp.sum(dlb_acc[...], axis=0, keepdims=True)

    seg = lambda k: pl.BlockSpec((None, seq, HG_DK), lambda h, s, k=k: (k, s, h))
    blk = pl.BlockSpec((seq, HG_DK), lambda h, s: (s, h))
    full = pltpu.VMEM((seq, HG_DK), F32)
    return _call_carrying(
        body, carry, name="hgrn_bwd", grid=(n_head, n_seq),
        in_specs=[seg(2), seg(3), seg(4), pl.BlockSpec((1, HG_DK), lambda h, s: (0, h)),
                  pl.BlockSpec((1, HG_DK), lambda h, s: (0, 0)), blk, blk],
        out_specs=[pl.BlockSpec((3, seq, HG_DK), lambda h, s: (0, s, h)),
                   pl.BlockSpec((1, HG_DK), lambda h, s: (0, h)),
                   pl.BlockSpec((1, HG_DK), lambda h, s: (0, 0))],
        out_shape=[jax.ShapeDtypeStruct((3, t_rows, d), BF16), jax.ShapeDtypeStruct((1, d), F32),
                   jax.ShapeDtypeStruct((1, HG_DK), F32)],
        scratch_shapes=[full, full, full, full, pltpu.VMEM((cc, HG_DK), F32),
                        pltpu.VMEM((n_chunks, HG_DK, HG_DK), F32), pltpu.VMEM((n_chunks, HG_DK, HG_DK), F32),
                        pltpu.VMEM((HG_DK, HG_DK), F32)],
        semantics=("arbitrary", "arbitrary"), args=(proj, proj, proj, lower_bound, hg_gain, o, d_on))


def _mesh_place():
    x, y, c = lax.axis_index("x"), lax.axis_index("y"), lax.axis_index("c")
    return x, y, c


def _peer(place, k):
    x, y, c = place
    px = 1 - x if k & 4 else x
    py = 1 - y if k & 2 else y
    pc = 1 - c if k & 1 else c
    return (px, py, pc), 4 * px + 2 * py + pc


class _Exchange:
    def __init__(self, srcs, gather):
        self.n = len(srcs)
        self.gather = gather
        self.out_shape = [jax.ShapeDtypeStruct((N_DEV,) + tuple(s.shape if gather else s.shape[1:]), s.dtype)
                          for s in srcs]
        self.scratch = [pltpu.SemaphoreType.DMA((self.n * (N_DEV - 1),)),
                        pltpu.SemaphoreType.DMA((self.n * (N_DEV - 1),)),
                        pltpu.SemaphoreType.DMA((self.n,))]

    def _copies(self, src_refs, out_refs, sems):
        send_sems, recv_sems, local_sems = sems
        place = _mesh_place()
        me = 4 * place[0] + 2 * place[1] + place[2]
        local, sends, recvs = [], [], []
        for a, (src, out) in enumerate(zip(src_refs, out_refs)):
            outgoing = (lambda idx, src=src: src) if self.gather else (lambda idx, src=src: src.at[idx])
            local.append(pltpu.make_async_copy(outgoing(me), out.at[me], local_sems.at[a]))
            for k in range(1, N_DEV):
                peer, peer_idx = _peer(place, k)
                sem = a * (N_DEV - 1) + k - 1
                sends.append(pltpu.make_async_remote_copy(
                    src_ref=outgoing(peer_idx), dst_ref=out.at[me], send_sem=send_sems.at[sem],
                    recv_sem=recv_sems.at[sem], device_id=peer, device_id_type=MESH_ID))
                recvs.append(pltpu.make_async_remote_copy(
                    src_ref=outgoing(peer_idx), dst_ref=out.at[peer_idx], send_sem=send_sems.at[sem],
                    recv_sem=recv_sems.at[sem], device_id=peer, device_id_type=MESH_ID))
        return local, sends, recvs

    def start(self, src_refs, out_refs, sems):
        local, sends, _ = self._copies(src_refs, out_refs, sems)
        for cp in local + sends:
            cp.start()

    def wait(self, src_refs, out_refs, sems):
        local, sends, recvs = self._copies(src_refs, out_refs, sems)
        for cp in recvs:
            cp.wait_recv()
        for cp in sends:
            cp.wait_send()
        for cp in local:
            cp.wait()


def _call_carrying(body, carry, *, name, grid, in_specs, out_specs, out_shape, scratch_shapes, semantics, args):
    if carry is None:
        outs = pl.pallas_call(body, name=name, grid=grid, in_specs=in_specs, out_specs=out_specs,
                              out_shape=out_shape, scratch_shapes=scratch_shapes,
                              compiler_params=_cp(*semantics))(*args)
        return outs, []
    srcs, gather = carry
    ex = _Exchange(srcs, gather)
    n, n_in, n_out, n_scr = ex.n, len(in_specs), len(out_specs), len(scratch_shapes)

    def wrapped(*refs):
        ins, refs = refs[:n_in], refs[n_in:]
        src_refs, refs = refs[:n], refs[n:]
        outs, refs = refs[:n_out], refs[n_out:]
        dst_refs, refs = refs[:n], refs[n:]
        scratch, sems = refs[:n_scr], refs[n_scr:]
        first, last = None, None
        for axis, size in enumerate(grid):
            i = pl.program_id(axis)
            first = (i == 0) if first is None else first & (i == 0)
            last = (i == size - 1) if last is None else last & (i == size - 1)

        @pl.when(first)
        def _():
            ex.start(src_refs, dst_refs, sems)

        body(*ins, *outs, *scratch)

        @pl.when(last)
        def _():
            ex.wait(src_refs, dst_refs, sems)

    any_space = pl.BlockSpec(memory_space=pl.ANY)
    res = pl.pallas_call(
        wrapped, name=name + "_carrying", grid=grid, in_specs=list(in_specs) + [any_space] * n,
        out_specs=list(out_specs) + [any_space] * n, out_shape=list(out_shape) + ex.out_shape,
        scratch_shapes=list(scratch_shapes) + ex.scratch,
        compiler_params=_cp(*(["arbitrary"] * len(grid))))(*args, *srcs)
    return res[:n_out], res[n_out:]


def _exchange(srcs, name, gather):
    ex = _Exchange(srcs, gather)
    n = ex.n

    def body(*refs):
        src_refs, out_refs, sems = refs[:n], refs[n:2 * n], refs[2 * n:]
        ex.start(src_refs, out_refs, sems)
        ex.wait(src_refs, out_refs, sems)

    any_space = pl.BlockSpec(memory_space=pl.ANY)
    return pl.pallas_call(
        body, name=name, in_specs=[any_space] * n, out_specs=[any_space] * n,
        out_shape=ex.out_shape, scratch_shapes=ex.scratch)(*srcs)


def _reduce_adamw(parts, w, m, v, name):
    rows, cols = w.shape
    n_seg = len(parts)
    seg_rows = rows // n_seg
    tr = _row_tile(seg_rows, 128)
    per_seg = seg_rows // tr
    c1 = np.float32(1.0 - ADAM_B1 ** ADAM_STEP)
    c2 = np.float32(1.0 - ADAM_B2 ** ADAM_STEP)

    def body(*refs):
        p_refs = refs[:n_seg]
        w_ref, m_ref, v_ref, g_ref, d_ref, nm_ref, nv_ref = refs[n_seg:]
        seg = pl.program_id(0)
        for k, p_ref in enumerate(p_refs):
            @pl.when(seg == k)
            def _(p_ref=p_ref):
                g = p_ref[0].astype(F32)
                for dev in range(1, N_DEV):
                    g = g + p_ref[dev].astype(F32)
                g_ref[...] = g

        g = g_ref[...]
        nm = ADAM_B1 * m_ref[...] + (1.0 - ADAM_B1) * g
        nv = ADAM_B2 * v_ref[...] + (1.0 - ADAM_B2) * (g * g)
        nm_ref[...] = nm
        nv_ref[...] = nv
        d_ref[...] = -ADAM_LR * ((nm / c1) / (jnp.sqrt(nv / c2) + ADAM_EPS) + ADAM_WD * w_ref[...])

    def part_spec(k):
        return pl.BlockSpec((N_DEV, tr, cols), lambda s, i, k=k: (0, jnp.where(s == k, i, 0), 0))

    blk = pl.BlockSpec((tr, cols), lambda s, i: (s * per_seg + i, 0))
    shp = jax.ShapeDtypeStruct((rows, cols), F32)
    return pl.pallas_call(
        body, name=name, grid=(n_seg, per_seg),
        in_specs=[part_spec(k) for k in range(n_seg)] + [blk, blk, blk],
        out_specs=[blk, blk, blk, blk], out_shape=[shp, shp, shp, shp],
        compiler_params=_cp("arbitrary", "arbitrary"))(*parts, w, m, v)


def _pack(arrays, lead=0):
    parts = []
    for a in arrays:
        f = a.reshape(a.shape[:lead] + (-1, LANES))
        pad = -f.shape[lead] % PACK_ROWS
        if pad:
            f = jnp.pad(f, [(0, 0)] * lead + [(0, pad), (0, 0)])
        parts.append(f)
    return jnp.concatenate(parts, axis=lead)


def _unpack(buf, shapes, lead=0):
    out, r = [], 0
    for shp in shapes:
        n = int(np.prod(shp)) // LANES
        part = lax.slice_in_dim(buf, r, r + n, axis=lead)
        out.append(part.reshape(buf.shape[:lead] + tuple(shp)))
        r += n + (-n % PACK_ROWS)
    return out


REPLICATED = ("lb_logits", "norm_mix", "conv_b", "b_r", "b_i", "lam", "hg_norm", "norm_mlp", "norm_final")
SMALL_SHARDED = ("conv_w", "w_r", "w_i")
LARGE_SHARDED = ("w_in", "w_out", "w_up", "w_down")
WEIGHTS = ("lb_logits", "norm_mix", "w_in", "conv_w", "conv_b", "w_r", "b_r", "w_i", "b_i", "lam", "hg_norm",
           "w_out", "norm_mlp", "w_up", "w_down", "norm_final")


def _matmul_weight_shards(p):
    depth = p["w_in"].shape[0]
    cast = {k: p[k].astype(BF16) for k in LARGE_SHARDED}
    return ([cast["w_in"][l] for l in range(depth)],
            [[cast[k][l] for k in ("w_out", "w_up", "w_down")] for l in range(depth)])


def _gathered_rest(got):
    w_out, w_up, w_down = got
    d = w_out.shape[2]
    return dict(w_out=w_out.reshape(d, d), w_up=w_up, w_down=w_down)


def _unpack_mixer_weights(small, p):
    depth, d, _ = p["w_in"].shape
    n_blk = d // RG_BLOCK_W
    conv_w, w_r, w_i = _unpack(small, [p["conv_w"].shape, p["w_r"].shape, p["w_i"].shape], lead=1)
    conv_w = conv_w.transpose(1, 2, 0, 3).reshape(depth, CONV_W, d)
    w_r = w_r.transpose(1, 2, 0, 3, 4).reshape(depth, n_blk, RG_BLOCK_W, RG_BLOCK_W).astype(BF16)
    w_i = w_i.transpose(1, 2, 0, 3, 4).reshape(depth, n_blk, RG_BLOCK_W, RG_BLOCK_W).astype(BF16)
    return conv_w, w_r, w_i


def _local_step(x, target, p):
    bl, seq, d = x.shape
    depth = p["w_in"].shape[0]
    t_rows = bl * seq
    row = lambda a, l: a[l:l + 1]
    lbs = _lower_bounds_fwd(p["lb_logits"])
    shard_in, shard_rest = _matmul_weight_shards(p)
    w_in = _exchange([shard_in[0]], "gather_w_in", gather=True)[0]
    cur = x.reshape(t_rows, d)
    saved, layers = [], []
    for l in range(depth):
        if l == 0:
            (proj, gates, h), small = _inproj_fwd(cur, row(p["norm_mix"], l), w_in,
                                                  carry=([_pack([p["conv_w"], p["w_r"], p["w_i"]])], True))
            conv_w, w_r, w_i = _unpack_mixer_weights(small[0], p)
        else:
            (proj, gates, h), _ = _inproj_fwd(cur, row(p["norm_mix"], l), w_in)
        w = dict(w_in=w_in, conv_w=conv_w[l], w_r=w_r[l], w_i=w_i[l])
        hs, y_a = _mixer_a_fwd(proj, w["conv_w"], row(p["conv_b"], l), w["w_r"], row(p["b_r"], l), w["w_i"],
                               row(p["b_i"], l), row(p["lam"], l), seq)
        (o, o_n), got = _hgrn_fwd(proj, row(lbs, l), row(p["hg_norm"], l), seq,
                                  carry=(shard_rest[l] + ([shard_in[l + 1]] if l + 1 < depth else []), True))
        w.update(_gathered_rest(got[:3]))
        w_in = got[3] if l + 1 < depth else None
        layers.append(w)
        x_mid, y = _merge_out_fwd(gates, y_a, o_n, cur, w["w_out"])
        x_out, u, h2 = _mlp_fwd(x_mid, row(p["norm_mlp"], l), w["w_up"], w["w_down"])
        saved.append(dict(x_in=cur, proj=proj, gates=gates, h=h, hs=hs, y_a=y_a, o=o, o_n=o_n, x_mid=x_mid, y=y, u=u, h2=h2))
        cur = x_out
    loss8, dx, dxb, g_norm_final = _loss_head(cur, p["norm_final"].reshape(1, d), target.reshape(t_rows, d))
    small = ("norm_mix", "conv_w", "conv_b", "w_r", "b_r", "w_i", "b_i", "lam", "hg_norm", "norm_mlp")
    g = {k: [None] * depth for k in small}
    d_lbs, received = [None] * depth, [None] * depth
    g_w_in = None
    for l in reversed(range(depth)):
        s, w = saved[l], layers[l]
        dx_mid, dx_mid_b, du, act, g["norm_mlp"][l] = _mlp_bwd(dx, dxb, s["u"], s["x_mid"], row(p["norm_mlp"], l),
                                                               w["w_up"], w["w_down"])
        g_w_down = _wgrad(act, dxb[None], "wgrad_down")
        g_w_up = _wgrad(s["h2"][None], du, "wgrad_up")
        d_ya, d_on, dp_c = _outproj_bwd(dx_mid_b, w["w_out"], s["gates"], s["y_a"], s["o_n"])
        g_w_out = _wgrad(s["y"][None], dx_mid_b[None], "wgrad_out").reshape(N_DEV, d // N_DEV, d)
        (dp_b, d_lbs[l], g["hg_norm"][l]), got = _hgrn_bwd(
            s["proj"], row(lbs, l), row(p["hg_norm"], l), s["o"], d_on, seq,
            carry=([g_w_out, g_w_up, g_w_down] + ([g_w_in] if g_w_in is not None else []), False))
        received[l] = [None] + list(got[:3])
        if g_w_in is not None:
            received[l + 1][0] = got[3]
        (dp_a, g["w_r"][l], g["w_i"][l], g["b_r"][l], g["b_i"][l], g["lam"][l], g["conv_w"][l],
         g["conv_b"][l]) = _mixer_a_bwd(s["proj"], s["hs"], d_ya, w["conv_w"], row(p["conv_b"], l), w["w_r"],
                                        row(p["b_r"], l), w["w_i"], row(p["b_i"], l), row(p["lam"], l), seq)
        hb = s["h"][None]
        g_w_in = jnp.concatenate([_wgrad(hb, dp_a, "wgrad_in_pair"), _wgrad(hb, dp_b, "wgrad_in_triple"),
                                  _wgrad(hb, dp_c, "wgrad_in_triple")], axis=0)
        carry = None
        if l == 0:
            carry = ([g_w_in, _mixer_grads_by_owner(g, d)], False)
        (dx, dxb, g["norm_mix"][l]), got = _inproj_bwd(dx_mid, dp_a, dp_b, dp_c, w["w_in"], s["x_in"],
                                                       row(p["norm_mix"], l), carry=carry)
    received[0][0], received_mixer = got
    grads = {k: jnp.stack(v) for k, v in g.items()}
    for k in ("norm_mix", "conv_b", "b_r", "b_i", "lam", "hg_norm", "norm_mlp"):
        grads[k] = grads[k][:, 0]
    grads["lb_logits"] = _lower_bounds_bwd(p["lb_logits"], jnp.concatenate(d_lbs, axis=0))
    grads["norm_final"] = g_norm_final[0]
    return loss8[0, 0], dx.reshape(bl, seq, d), grads, received, received_mixer


def _mixer_grads_by_owner(g, d):
    d8, n_blk, rb = d // N_DEV, d // RG_BLOCK_W, RG_BLOCK_W // N_DEV
    depth = len(g["conv_w"])
    conv_w, w_r, w_i = (jnp.stack(g[k]) for k in SMALL_SHARDED)
    return _pack([conv_w.reshape(depth, CONV_W, N_DEV, d8).transpose(2, 0, 1, 3),
                  w_r.reshape(depth, n_blk, N_DEV, rb, RG_BLOCK_W).transpose(2, 0, 1, 3, 4),
                  w_i.reshape(depth, n_blk, N_DEV, rb, RG_BLOCK_W).transpose(2, 0, 1, 3, 4)],
                 lead=1).astype(BF16)


def _update(p, mom1, mom2, grads, received, received_mixer):
    depth = p["w_in"].shape[0]
    out = {}

    for i, k in enumerate(LARGE_SHARDED):
        shp = p[k].shape
        flat = lambda a: a.reshape(shp[0] * shp[1], shp[2])
        parts = [received[l][i] for l in range(depth)]
        res = _reduce_adamw(parts, flat(p[k]), flat(mom1[k]), flat(mom2[k]), "adamw_" + k)
        out[k] = [r.reshape(shp) for r in res]

    res = _reduce_adamw([received_mixer], *[_pack([src[k] for k in SMALL_SHARDED]) for src in (p, mom1, mom2)],
                        "adamw_mixer")
    shapes = [p[k].shape for k in SMALL_SHARDED]
    for i, vals in enumerate(zip(*[_unpack(r, shapes) for r in res])):
        out[SMALL_SHARDED[i]] = list(vals)

    parts = _exchange([_pack([grads[k] for k in REPLICATED])], "gather_grad_replicated", gather=True)
    res = _reduce_adamw(parts, *[_pack([src[k] for k in REPLICATED]) for src in (p, mom1, mom2)],
                        "adamw_replicated")
    shapes = [p[k].shape for k in REPLICATED]
    for i, vals in enumerate(zip(*[_unpack(r, shapes) for r in res])):
        out[REPLICATED[i]] = list(vals)

    return tuple(out[k][i] for i in range(4) for k in WEIGHTS)


def kernel(x, lb_logits, norm_mix, w_in, conv_w, conv_b, w_r, b_r, w_i, b_i, lam, hg_norm, w_out, norm_mlp, w_up, w_down, norm_final, loss_target, m_lb_logits, m_norm_mix, m_w_in, m_conv_w, m_conv_b, m_w_r, m_b_r, m_w_i, m_b_i, m_lam, m_hg_norm, m_w_out, m_norm_mlp, m_w_up, m_w_down, m_norm_final, v_lb_logits, v_norm_mix, v_w_in, v_conv_w, v_conv_b, v_w_r, v_b_r, v_w_i, v_b_i, v_lam, v_hg_norm, v_w_out, v_norm_mlp, v_w_up, v_w_down, v_norm_final):
    p = dict(lb_logits=lb_logits, norm_mix=norm_mix, w_in=w_in, conv_w=conv_w, conv_b=conv_b, w_r=w_r, b_r=b_r,
             w_i=w_i, b_i=b_i, lam=lam, hg_norm=hg_norm, w_out=w_out, norm_mlp=norm_mlp, w_up=w_up,
             w_down=w_down, norm_final=norm_final)
    mom1 = dict(lb_logits=m_lb_logits, norm_mix=m_norm_mix, w_in=m_w_in, conv_w=m_conv_w, conv_b=m_conv_b,
                w_r=m_w_r, b_r=m_b_r, w_i=m_w_i, b_i=m_b_i, lam=m_lam, hg_norm=m_hg_norm, w_out=m_w_out,
                norm_mlp=m_norm_mlp, w_up=m_w_up, w_down=m_w_down, norm_final=m_norm_final)
    mom2 = dict(lb_logits=v_lb_logits, norm_mix=v_norm_mix, w_in=v_w_in, conv_w=v_conv_w, conv_b=v_conv_b,
                w_r=v_w_r, b_r=v_b_r, w_i=v_w_i, b_i=v_b_i, lam=v_lam, hg_norm=v_hg_norm, w_out=v_w_out,
                norm_mlp=v_norm_mlp, w_up=v_w_up, w_down=v_w_down, norm_final=v_norm_final)
    loss, grad_x, grads, received, received_mixer = _local_step(x, loss_target, p)
    loss = lax.psum(loss, ("x", "y", "c"))
    return (loss, grad_x) + _update(p, mom1, mom2, grads, received, received_mixer)
```

```python
import numpy as np

import jax
import jax.numpy as jnp
from jax import lax
from jax.experimental import pallas as pl
from jax.experimental.pallas import tpu as pltpu

F32 = jnp.float32
BF16 = jnp.bfloat16
MESH_ID = pl.DeviceIdType.MESH

N_DEV = 8
N_MIXER_SEGMENTS = 5
NORM_EPS = 1e-6
RG_C = 8.0
RG_BLOCK_W = 256
CONV_W = 4
HG_DK = 128
F_MIN = 1e-30
HG_CHUNK = 16
SUBLANES = 8
LANES = 128
PACK_ROWS = 16
SCAN_GROUP = 16
ROW_CHUNK = 256
ROW_TILE_WEIGHT_STREAM = 1024
WGRAD_TOKEN_TILE = 2048
VMEM_LIMIT_V7X = 56 * 1024 * 1024

ADAM_LR = 0.001
ADAM_B1 = 0.9
ADAM_B2 = 0.999
ADAM_EPS = 1e-08
ADAM_WD = 0.01
ADAM_STEP = 10

GELU_C = 0.7978845608028654
GELU_K = 0.044715


def _cp(*sem):
    return pltpu.CompilerParams(dimension_semantics=sem, vmem_limit_bytes=VMEM_LIMIT_V7X)


def _row_tile(n, cap):
    if n <= cap:
        return n
    t = cap - cap % 16
    while n % t:
        t -= 16
    return t


def _dot(a, b):
    return jnp.dot(a, b, preferred_element_type=F32)


def _dot_nt(a, b):
    return lax.dot_general(a, b, (((1,), (1,)), ((), ())), preferred_element_type=F32)


def _dot_tn(a, b):
    return lax.dot_general(a, b, (((0,), (0,)), ((), ())), preferred_element_type=F32)


def _sigmoid(x):
    return jax.nn.sigmoid(x)


def _sigmoid_pair(x):
    e = jnp.exp(-jnp.abs(x))
    r = 1.0 / (1.0 + e)
    er = e * r
    pos = x >= 0.0
    return jnp.where(pos, r, er), jnp.where(pos, er, r)


def _log1p_pos(y):
    return jnp.where(y < 0.01, y * (1.0 - y * (0.5 - y * (1.0 / 3.0))), jnp.log(1.0 + y))


def _softplus(x):
    return jnp.maximum(x, 0.0) + _log1p_pos(jnp.exp(-jnp.abs(x)))


def _one_minus_exp(x):
    series = -x * (1.0 + x * 0.5 * (1.0 + x * (1.0 / 3.0) * (1.0 + x * 0.25 * (1.0 + x * 0.2))))
    return jnp.where(x > -0.1, series, 1.0 - jnp.exp(x))


def _gelu_and_grad(x):
    x2 = x * x
    t = jnp.tanh(GELU_C * x * (1.0 + GELU_K * x2))
    g = 0.5 * x * (1.0 + t)
    dg = 0.5 * (1.0 + t) + 0.5 * x * (1.0 - t * t) * GELU_C * (1.0 + 3.0 * GELU_K * x2)
    return g, dg


def _silu_and_grad(x):
    s = _sigmoid(x)
    return x * s, s * (1.0 + x * (1.0 - s))


def _rstd(x):
    return lax.rsqrt(jnp.mean(x * x, axis=-1, keepdims=True) + NORM_EPS)


def _rms_bwd(dh, x, g):
    rstd = _rstd(x)
    xh = x * rstd
    dxh = dh * g
    dx = rstd * (dxh - xh * jnp.mean(dxh * xh, axis=-1, keepdims=True))
    return dx, jnp.sum(dh * xh, axis=0, keepdims=True)


def _shift_rows(x, k):
    n = x.shape[0]
    k = k % n
    return x if k == 0 else pltpu.roll(x, k, axis=0)


def _seg_cumsum(x, seg, reverse=False):
    n = x.shape[0]
    rid = lax.broadcasted_iota(jnp.int32, x.shape, 0) & (seg - 1)
    d = 1
    while d < seg:
        if reverse:
            x = jnp.where(rid < seg - d, x + _shift_rows(x, n - d), x)
        else:
            x = jnp.where(rid >= d, x + _shift_rows(x, d), x)
        d *= 2
    return x


def _group_cumsum_matrix(n, seg):
    row = lax.broadcasted_iota(jnp.int32, (n, n), 0)
    col = lax.broadcasted_iota(jnp.int32, (n, n), 1)
    same_group = (row & ~(seg - 1)) == (col & ~(seg - 1))
    return jnp.where(same_group & (col <= row), 1.0, 0.0).astype(BF16)


def _group_cumsum_mxu(x, tri):
    hi = x.astype(BF16)
    lo = (x - hi.astype(F32)).astype(BF16)
    return _dot(tri, hi) + _dot(tri, lo)


def _scan_rows(a_ref, b_ref, out_ref, n_rows, width, reverse):
    gr = min(SCAN_GROUP, n_rows)
    rid = lax.broadcasted_iota(jnp.int32, (gr, width), 0)
    n_groups = n_rows // gr
    per_trip = min(4, n_groups)
    assert n_groups % per_trip == 0

    def local_scan(g):
        r0 = pl.multiple_of(g * gr, gr)
        a = a_ref[pl.ds(r0, gr), :]
        b = b_ref[pl.ds(r0, gr), :]
        d = 1
        while d < gr:
            if reverse:
                keep = rid < gr - d
                a_sh, b_sh = _shift_rows(a, gr - d), _shift_rows(b, gr - d)
            else:
                keep = rid >= d
                a_sh, b_sh = _shift_rows(a, d), _shift_rows(b, d)
            b = jnp.where(keep, a * b_sh + b, b)
            a = jnp.where(keep, a * a_sh, a)
            d *= 2
        return r0, a, b

    def trip(i, carry):
        first = i * per_trip
        groups = [n_groups - 1 - (first + u) if reverse else first + u for u in range(per_trip)]
        for r0, a, b in [local_scan(g) for g in groups]:
            out = a * carry + b
            out_ref[pl.ds(r0, gr), :] = out
            edge = out[0:1, :] if reverse else out[gr - 1:gr, :]
            carry = jnp.broadcast_to(edge, (gr, width))
        return carry

    lax.fori_loop(0, n_groups // per_trip, trip, jnp.zeros((gr, width), F32))


def _lb_softmax_rows(x_ref, depth):
    rows = [x_ref[pl.ds(l, 1), :] for l in range(depth)]
    top = rows[0]
    for r in rows[1:]:
        top = jnp.maximum(top, r)
    e = [jnp.exp(r - top) for r in rows]
    tot = e[0]
    for r in e[1:]:
        tot = tot + r
    return [r / tot for r in e]


def _lower_bounds_fwd(lb_logits):
    depth, d = lb_logits.shape

    def body(x_ref, o_ref):
        sm = _lb_softmax_rows(x_ref, depth)
        cum = jnp.zeros((1, d), F32)
        for l in range(depth):
            cum = cum + sm[l]
            o_ref[pl.ds(l, 1), :] = jnp.clip(cum - sm[0], 0.0, 1.0)

    return pl.pallas_call(body, name="lower_bounds_fwd",
                          out_shape=jax.ShapeDtypeStruct((depth, d), F32))(lb_logits)


def _lower_bounds_bwd(lb_logits, d_lbs):
    depth, d = lb_logits.shape

    def body(x_ref, g_ref, o_ref):
        sm = _lb_softmax_rows(x_ref, depth)
        cum = jnp.zeros((1, d), F32)
        d_cum = []
        for l in range(depth):
            cum = cum + sm[l]
            v = cum - sm[0]
            d_cum.append(jnp.where((v > 0.0) & (v < 1.0), g_ref[pl.ds(l, 1), :], 0.0))
        d_sm = []
        tail = jnp.zeros((1, d), F32)
        for l in reversed(range(depth)):
            tail = tail + d_cum[l]
            d_sm.append(tail)
        d_sm = d_sm[::-1]
        d_sm[0] = d_sm[0] - tail
        inner = jnp.zeros((1, d), F32)
        for l in range(depth):
            inner = inner + sm[l] * d_sm[l]
        for l in range(depth):
            o_ref[pl.ds(l, 1), :] = sm[l] * (d_sm[l] - inner)

    return pl.pallas_call(body, name="lower_bounds_bwd",
                          out_shape=jax.ShapeDtypeStruct((depth, d), F32))(lb_logits, d_lbs)


def _inproj_fwd(x, gain, w_seg, carry=None):
    t_rows, d = x.shape
    tm = _row_tile(t_rows, ROW_TILE_WEIGHT_STREAM)
    n_gate = N_DEV - N_MIXER_SEGMENTS

    def body(x_ref, g_ref, w_ref, proj_ref, gates_ref, h_ref):
        j = pl.program_id(1)

        @pl.when(j == 0)
        def _():
            xv = x_ref[...]
            h_ref[...] = (xv * _rstd(xv) * g_ref[...]).astype(BF16)

        @pl.when(j < N_MIXER_SEGMENTS)
        def _():
            proj_ref[...] = _dot(h_ref[...], w_ref[...])

        @pl.when(j >= N_MIXER_SEGMENTS)
        def _():
            gates_ref[...] = _dot(h_ref[...], w_ref[...]).astype(BF16)

    return _call_carrying(
        body, carry, name="inproj_fwd", grid=(t_rows // tm, N_DEV),
        in_specs=[pl.BlockSpec((tm, d), lambda i, j: (i, 0)),
                  pl.BlockSpec((1, d), lambda i, j: (0, 0)),
                  pl.BlockSpec((None, d, d), lambda i, j: (j, 0, 0))],
        out_specs=[pl.BlockSpec((None, tm, d), lambda i, j: (jnp.minimum(j, N_MIXER_SEGMENTS - 1), i, 0)),
                   pl.BlockSpec((None, tm, d), lambda i, j: (jnp.maximum(j - N_MIXER_SEGMENTS, 0), i, 0)),
                   pl.BlockSpec((tm, d), lambda i, j: (i, 0))],
        out_shape=[jax.ShapeDtypeStruct((N_MIXER_SEGMENTS, t_rows, d), F32),
                   jax.ShapeDtypeStruct((n_gate, t_rows, d), BF16),
                   jax.ShapeDtypeStruct((t_rows, d), BF16)],
        scratch_shapes=[], semantics=("parallel", "arbitrary"), args=(x, gain, w_seg))


def _merge_out_fwd(gates, y_a, o_n, x, w_out):
    t_rows, d = x.shape
    tm = _row_tile(t_rows, 256)

    def body(g_ref, ma_ref, mb_ref, ya_ref, on_ref, x_ref, w_ref, xmid_ref, y_ref):
        g = g_ref[...].astype(F32)
        ya, on = ya_ref[...].astype(F32), on_ref[...].astype(F32)
        y = (_sigmoid(ma_ref[...].astype(F32)) * ya
             + _sigmoid(mb_ref[...].astype(F32)) * (on * (g * _sigmoid(g))))
        yb = y.astype(BF16)
        y_ref[...] = yb
        xmid_ref[...] = x_ref[...] + _dot(yb, w_ref[...])

    seg = lambda k: pl.BlockSpec((None, tm, d), lambda i, k=k: (k, i, 0))
    row = pl.BlockSpec((tm, d), lambda i: (i, 0))
    return pl.pallas_call(
        body, name="merge_out_fwd", grid=(t_rows // tm,),
        in_specs=[seg(0), seg(1), seg(2), row, row, row, pl.BlockSpec((d, d), lambda i: (0, 0))],
        out_specs=[row, row],
        out_shape=[jax.ShapeDtypeStruct((t_rows, d), F32), jax.ShapeDtypeStruct((t_rows, d), BF16)],
        compiler_params=_cp("parallel"))(gates, gates, gates, y_a, o_n, x, w_out)


def _mlp_fwd(x_mid, gain, w_up, w_down):
    t_rows, d = x_mid.shape
    f8 = w_up.shape[2]
    tm = _row_tile(t_rows, ROW_TILE_WEIGHT_STREAM)
    per_step = 2

    def body(x_ref, g_ref, wu_ref, wd_ref, out_ref, u_ref, h_ref):
        @pl.when(pl.program_id(1) == 0)
        def _():
            xv = x_ref[...]
            h_ref[...] = (xv * _rstd(xv) * g_ref[...]).astype(BF16)
            out_ref[...] = xv

        h = h_ref[...]
        down = None
        for k in range(per_step):
            u = _dot(h, wu_ref[k])
            u_ref[k] = u.astype(BF16)
            r = jnp.maximum(u, 0.0)
            part = _dot((r * r).astype(BF16), wd_ref[k])
            down = part if down is None else down + part
        out_ref[...] += down

    row = pl.BlockSpec((tm, d), lambda i, j: (i, 0))
    return pl.pallas_call(
        body, name="mlp_fwd", grid=(t_rows // tm, N_DEV // per_step),
        in_specs=[row, pl.BlockSpec((1, d), lambda i, j: (0, 0)),
                  pl.BlockSpec((per_step, d, f8), lambda i, j: (j, 0, 0)),
                  pl.BlockSpec((per_step, f8, d), lambda i, j: (j, 0, 0))],
        out_specs=[row, pl.BlockSpec((per_step, tm, f8), lambda i, j: (j, i, 0)), row],
        out_shape=[jax.ShapeDtypeStruct((t_rows, d), F32),
                   jax.ShapeDtypeStruct((N_DEV, t_rows, f8), BF16),
                   jax.ShapeDtypeStruct((t_rows, d), BF16)],
        compiler_params=_cp("parallel", "arbitrary"))(x_mid, gain, w_up, w_down)


def _loss_head(x, gain, target):
    t_rows, d = x.shape
    tm = _row_tile(t_rows, 512)

    def body(x_ref, g_ref, t_ref, loss_ref, dx_ref, dxb_ref, dg_ref):
        @pl.when(pl.program_id(0) == 0)
        def _():
            loss_ref[...] = jnp.zeros_like(loss_ref)
            dg_ref[...] = jnp.zeros_like(dg_ref)

        xv = x_ref[...]
        g = g_ref[...]
        err = xv * _rstd(xv) * g - t_ref[...]
        loss_ref[...] += (0.5 / d) * jnp.sum(err * err)
        dx, dg = _rms_bwd(err * (1.0 / d), xv, g)
        dx_ref[...] = dx
        dxb_ref[...] = dx.astype(BF16)
        dg_ref[...] += dg

    row = pl.BlockSpec((tm, d), lambda i: (i, 0))
    vec = pl.BlockSpec((1, d), lambda i: (0, 0))
    return pl.pallas_call(
        body, name="loss_head", grid=(t_rows // tm,),
        in_specs=[row, vec, row],
        out_specs=[pl.BlockSpec((SUBLANES, LANES), lambda i: (0, 0)), row, row, vec],
        out_shape=[jax.ShapeDtypeStruct((SUBLANES, LANES), F32),
                   jax.ShapeDtypeStruct((t_rows, d), F32),
                   jax.ShapeDtypeStruct((t_rows, d), BF16),
                   jax.ShapeDtypeStruct((1, d), F32)],
        compiler_params=_cp("arbitrary"))(x, gain, target)


def _mlp_bwd(d_out, d_out_b, u, x_mid, gain, w_up, w_down):
    t_rows, d = x_mid.shape
    f8 = w_up.shape[2]
    tm = _row_tile(t_rows, ROW_TILE_WEIGHT_STREAM)
    sub = _row_tile(tm, ROW_CHUNK)

    def body(do_ref, dob_ref, u_ref, x_ref, g_ref, wu_ref, wd_ref, dx_ref, dxb_ref, du_ref, act_ref, dg_ref):
        j = pl.program_id(1)

        @pl.when((pl.program_id(0) == 0) & (j == 0))
        def _():
            dg_ref[...] = jnp.zeros_like(dg_ref)

        @pl.when(j == 0)
        def _():
            dx_ref[...] = jnp.zeros_like(dx_ref)

        r = jnp.maximum(u_ref[...].astype(F32), 0.0)
        act_ref[...] = (r * r).astype(BF16)
        du = (_dot_nt(dob_ref[...], wd_ref[...]) * (2.0 * r)).astype(BF16)
        du_ref[...] = du
        dx_ref[...] += _dot_nt(du, wu_ref[...])

        @pl.when(j == N_DEV - 1)
        def _():
            def finish(c, _):
                rows = pl.ds(pl.multiple_of(c * sub, sub), sub)
                dx, dg = _rms_bwd(dx_ref[rows, :], x_ref[rows, :], g_ref[...])
                dx = dx + do_ref[rows, :]
                dx_ref[rows, :] = dx
                dxb_ref[rows, :] = dx.astype(BF16)
                dg_ref[...] += dg
                return 0

            lax.fori_loop(0, tm // sub, finish, 0)

    row = pl.BlockSpec((tm, d), lambda i, j: (i, 0))
    vec = pl.BlockSpec((1, d), lambda i, j: (0, 0))
    hid = pl.BlockSpec((None, tm, f8), lambda i, j: (j, i, 0))
    return pl.pallas_call(
        body, name="mlp_bwd", grid=(t_rows // tm, N_DEV),
        in_specs=[row, row, hid, row, vec,
                  pl.BlockSpec((None, d, f8), lambda i, j: (j, 0, 0)),
                  pl.BlockSpec((None, f8, d), lambda i, j: (j, 0, 0))],
        out_specs=[row, row, hid, hid, vec],
        out_shape=[jax.ShapeDtypeStruct((t_rows, d), F32),
                   jax.ShapeDtypeStruct((t_rows, d), BF16),
                   jax.ShapeDtypeStruct((N_DEV, t_rows, f8), BF16),
                   jax.ShapeDtypeStruct((N_DEV, t_rows, f8), BF16),
                   jax.ShapeDtypeStruct((1, d), F32)],
        compiler_params=_cp("arbitrary", "arbitrary"))(d_out, d_out_b, u, x_mid, gain, w_up, w_down)


def _outproj_bwd(dx_mid_b, w_out, gates, y_a, o_n):
    t_rows, d = y_a.shape
    tm = _row_tile(t_rows, 256)

    def body(dx_ref, w_ref, g_ref, ma_ref, mb_ref, ya_ref, on_ref, dya_ref, don_ref, dp_ref):
        dy = _dot_nt(dx_ref[...], w_ref[...])
        sa = _sigmoid(ma_ref[...].astype(F32))
        sb = _sigmoid(mb_ref[...].astype(F32))
        sg, dsg = _silu_and_grad(g_ref[...].astype(F32))
        ya = ya_ref[...].astype(F32)
        on = on_ref[...].astype(F32)
        dya_ref[...] = dy * sa
        t = dy * sb
        don_ref[...] = t * sg
        dp_ref[0] = (t * on * dsg).astype(BF16)
        dp_ref[1] = (dy * ya * sa * (1.0 - sa)).astype(BF16)
        dp_ref[2] = (dy * on * sg * sb * (1.0 - sb)).astype(BF16)

    seg = lambda k: pl.BlockSpec((None, tm, d), lambda i, k=k: (k, i, 0))
    row = pl.BlockSpec((tm, d), lambda i: (i, 0))
    return pl.pallas_call(
        body, name="outproj_bwd", grid=(t_rows // tm,),
        in_specs=[row, pl.BlockSpec((d, d), lambda i: (0, 0)), seg(0), seg(1), seg(2), row, row],
        out_specs=[row, row, pl.BlockSpec((3, tm, d), lambda i: (0, i, 0))],
        out_shape=[jax.ShapeDtypeStruct((t_rows, d), F32),
                   jax.ShapeDtypeStruct((t_rows, d), F32),
                   jax.ShapeDtypeStruct((3, t_rows, d), BF16)],
        compiler_params=_cp("parallel"))(dx_mid_b, w_out, gates, gates, gates, y_a, o_n)


def _inproj_bwd(dx_mid, dp_a, dp_b, dp_c, w_seg, x_in, gain, carry=None):
    t_rows, d = x_in.shape
    tm = _row_tile(t_rows, ROW_TILE_WEIGHT_STREAM)
    sub = _row_tile(tm, ROW_CHUNK)
    n_a, n_b = dp_a.shape[0], dp_b.shape[0]

    def body(dxm_ref, a_ref, b_ref, c_ref, w_ref, x_ref, g_ref, dx_ref, dxb_ref, dg_ref):
        j = pl.program_id(1)

        @pl.when((pl.program_id(0) == 0) & (j == 0))
        def _():
            dg_ref[...] = jnp.zeros_like(dg_ref)

        @pl.when(j == 0)
        def _():
            dx_ref[...] = jnp.zeros_like(dx_ref)

        @pl.when(j < n_a)
        def _():
            dx_ref[...] += _dot_nt(a_ref[...], w_ref[...])

        @pl.when((j >= n_a) & (j < n_a + n_b))
        def _():
            dx_ref[...] += _dot_nt(b_ref[...], w_ref[...])

        @pl.when(j >= n_a + n_b)
        def _():
            dx_ref[...] += _dot_nt(c_ref[...], w_ref[...])

        @pl.when(j == N_DEV - 1)
        def _():
            def finish(c, _):
                rows = pl.ds(pl.multiple_of(c * sub, sub), sub)
                dx, dg = _rms_bwd(dx_ref[rows, :], x_ref[rows, :], g_ref[...])
                dx = dx + dxm_ref[rows, :]
                dx_ref[rows, :] = dx
                dxb_ref[rows, :] = dx.astype(BF16)
                dg_ref[...] += dg
                return 0

            lax.fori_loop(0, tm // sub, finish, 0)

    def part(first, n):
        return pl.BlockSpec((None, tm, d), lambda i, j: (jnp.clip(j - first, 0, n - 1), i, 0))

    row = pl.BlockSpec((tm, d), lambda i, j: (i, 0))
    vec = pl.BlockSpec((1, d), lambda i, j: (0, 0))
    return _call_carrying(
        body, carry, name="inproj_bwd", grid=(t_rows // tm, N_DEV),
        in_specs=[row, part(0, n_a), part(n_a, n_b), part(n_a + n_b, dp_c.shape[0]),
                  pl.BlockSpec((None, d, d), lambda i, j: (j, 0, 0)), row, vec],
        out_specs=[row, row, vec],
        out_shape=[jax.ShapeDtypeStruct((t_rows, d), F32),
                   jax.ShapeDtypeStruct((t_rows, d), BF16),
                   jax.ShapeDtypeStruct((1, d), F32)],
        scratch_shapes=[], semantics=("arbitrary", "arbitrary"),
        args=(dx_mid, dp_a, dp_b, dp_c, w_seg, x_in, gain))


def _wgrad(a3, b3, name):
    n_a, t_rows, k_a = a3.shape
    n_b, _, n_cols = b3.shape
    n = max(n_a, n_b)
    bk = _row_tile(k_a, 1024)
    bn = n_cols if n_cols <= 1024 else 1024
    tt = _row_tile(t_rows, WGRAD_TOKEN_TILE)
    n_t = t_rows // tt

    def body(a_ref, b_ref, o_ref, acc_ref):
        t, j = pl.program_id(2), pl.program_id(3)
        part = _dot_tn(a_ref[...], b_ref[...])

        @pl.when(t == 0)
        def _():
            acc_ref[j] = part

        @pl.when(t > 0)
        def _():
            acc_ref[j] += part

        @pl.when(t == n_t - 1)
        def _():
            o_ref[...] = acc_ref[j].astype(BF16)

    def out_map(p, q, t, j):
        return (jnp.where(t == n_t - 1, j, 0), p, q)

    return pl.pallas_call(
        body, name=name, grid=(k_a // bk, n_cols // bn, n_t, n),
        in_specs=[pl.BlockSpec((None, tt, bk), lambda p, q, t, j: (j if n_a > 1 else 0, t, p)),
                  pl.BlockSpec((None, tt, bn), lambda p, q, t, j: (j if n_b > 1 else 0, t, q))],
        out_specs=pl.BlockSpec((None, bk, bn), out_map),
        out_shape=jax.ShapeDtypeStruct((n, k_a, n_cols), BF16),
        scratch_shapes=[pltpu.VMEM((n, bk, bn), F32)],
        compiler_params=_cp("parallel", "parallel", "arbitrary", "arbitrary"))(a3, b3)


def _conv_taps(xe, n):
    return [_shift_rows(xe, CONV_W - 1 - j)[SUBLANES:SUBLANES + n, :] for j in range(CONV_W)]


def _rg_gates(xc, w_r, b_r, w_i, b_i, sp8):
    xb = xc.astype(BF16)
    r = _sigmoid(_dot(xb, w_r) + b_r)
    i = _sigmoid(_dot(xb, w_i) + b_i)
    return r, i


def _mixer_a_fwd(proj, conv_w, conv_b, w_r, b_r, w_i, b_i, lam, seq):
    _, t_rows, d = proj.shape
    n_seq, n_blk = t_rows // seq, d // RG_BLOCK_W
    wb = RG_BLOCK_W
    ch = _row_tile(seq, ROW_CHUNK)

    def body(xa_ref, ga_ref, cw_ref, cb_ref, wr_ref, br_ref, wi_ref, bi_ref, lam_ref, h_ref, ya_ref,
             xpad, a_s, u_s):
        xpad[0:SUBLANES, :] = jnp.zeros((SUBLANES, wb), F32)
        xpad[SUBLANES:, :] = xa_ref[...]
        sp8 = RG_C * _softplus(-lam_ref[...])

        def gates(c, _):
            r0 = pl.multiple_of(c * ch, ch)
            taps = _conv_taps(xpad[pl.ds(r0, ch + SUBLANES), :], ch)
            xc = cb_ref[...] + sum(cw_ref[pl.ds(j, 1), :] * taps[j] for j in range(CONV_W))
            r, i = _rg_gates(xc, wr_ref[...], br_ref[...], wi_ref[...], bi_ref[...], sp8)
            log_a = -(r * sp8)
            a_s[pl.ds(r0, ch), :] = jnp.exp(log_a)
            u_s[pl.ds(r0, ch), :] = jnp.sqrt(jnp.maximum(_one_minus_exp(2.0 * log_a), 0.0)) * (i * xc)
            return 0

        lax.fori_loop(0, seq // ch, gates, 0)
        _scan_rows(a_s, u_s, h_ref, seq, wb, reverse=False)

        def gate_out(c, _):
            r0 = pl.multiple_of(c * ch, ch)
            gl, _ = _gelu_and_grad(ga_ref[pl.ds(r0, ch), :])
            ya_ref[pl.ds(r0, ch), :] = (h_ref[pl.ds(r0, ch), :] * gl).astype(BF16)
            return 0

        lax.fori_loop(0, seq // ch, gate_out, 0)

    seg = lambda k: pl.BlockSpec((None, seq, wb), lambda s, b, k=k: (k, s, b))
    blk = pl.BlockSpec((seq, wb), lambda s, b: (s, b))
    vec = pl.BlockSpec((1, wb), lambda s, b: (0, b))
    wsp = pl.BlockSpec((None, wb, wb), lambda s, b: (b, 0, 0))
    return pl.pallas_call(
        body, name="mixer_a_fwd", grid=(n_seq, n_blk),
        in_specs=[seg(0), seg(1), pl.BlockSpec((CONV_W, wb), lambda s, b: (0, b)), vec, wsp, vec, wsp, vec, vec],
        out_specs=[blk, blk],
        out_shape=[jax.ShapeDtypeStruct((t_rows, d), F32), jax.ShapeDtypeStruct((t_rows, d), BF16)],
        scratch_shapes=[pltpu.VMEM((seq + SUBLANES, wb), F32), pltpu.VMEM((seq, wb), F32),
                        pltpu.VMEM((seq, wb), F32)],
        compiler_params=_cp("parallel", "parallel"))(proj, proj, conv_w, conv_b, w_r, b_r, w_i, b_i, lam)


def _mixer_a_bwd(proj, h, d_ya, conv_w, conv_b, w_r, b_r, w_i, b_i, lam, seq):
    _, t_rows, d = proj.shape
    n_seq, n_blk = t_rows // seq, d // RG_BLOCK_W
    wb = RG_BLOCK_W
    ch = _row_tile(seq, ROW_CHUNK)
    n_ch = seq // ch

    def body(xa_ref, ga_ref, h_ref, dya_ref, cw_ref, cb_ref, wr_ref, br_ref, wi_ref, bi_ref, lam_ref,
             dp_ref, dwr_ref, dwi_ref, dbr_ref, dbi_ref, dlam_ref, dcw_ref, dcb_ref,
             xpad, hpad, a_s, e_pad, g_s, xc_s, r_s, i_s, dxc_pad):
        @pl.when(pl.program_id(1) == 0)
        def _():
            for ref in (dwr_ref, dwi_ref, dbr_ref, dbi_ref, dlam_ref, dcw_ref, dcb_ref):
                ref[...] = jnp.zeros_like(ref)

        zeros8 = jnp.zeros((SUBLANES, wb), F32)
        xpad[0:SUBLANES, :] = zeros8
        xpad[SUBLANES:, :] = xa_ref[...]
        hpad[0:SUBLANES, :] = zeros8
        hpad[SUBLANES:, :] = h_ref[...]
        e_pad[seq:, :] = zeros8
        dxc_pad[seq:, :] = zeros8
        lam_v = lam_ref[...]
        sp8 = RG_C * _softplus(-lam_v)

        def recompute(c, _):
            r0 = pl.multiple_of(c * ch, ch)
            rows = pl.ds(r0, ch)
            taps = _conv_taps(xpad[pl.ds(r0, ch + SUBLANES), :], ch)
            xc = cb_ref[...] + sum(cw_ref[pl.ds(j, 1), :] * taps[j] for j in range(CONV_W))
            r, i = _rg_gates(xc, wr_ref[...], br_ref[...], wi_ref[...], bi_ref[...], sp8)
            a = jnp.exp(-(r * sp8))
            gl, dgl = _gelu_and_grad(ga_ref[rows, :])
            dya = dya_ref[rows, :]
            g = dya * gl
            dp_ref[1, rows, :] = (dya * h_ref[rows, :] * dgl).astype(BF16)
            a_s[rows, :] = a
            e_pad[rows, :] = a * g
            g_s[rows, :] = g
            xc_s[rows, :] = xc
            r_s[rows, :] = r
            i_s[rows, :] = i
            return 0

        lax.fori_loop(0, n_ch, recompute, 0)
        _scan_rows(a_s, e_pad, e_pad, seq, wb, reverse=True)

        def grads(c, _):
            r0 = pl.multiple_of(c * ch, ch)
            rows = pl.ds(r0, ch)
            halo = pl.ds(r0, ch + SUBLANES)
            dh = g_s[rows, :] + _shift_rows(e_pad[halo, :], ch + SUBLANES - 1)[0:ch, :]
            h_prev = _shift_rows(hpad[halo, :], 1)[SUBLANES:, :]
            xc, r, i = xc_s[rows, :], r_s[rows, :], i_s[rows, :]
            log_a = -(r * sp8)
            a = jnp.exp(log_a)
            om = _one_minus_exp(2.0 * log_a)
            sq = jnp.sqrt(jnp.maximum(om, 0.0))
            t1 = dh * xc
            d_i = t1 * sq
            d_la = dh * h_prev * a + jnp.where(om > 0.0, -(t1 * i) * (1.0 - om) / sq, 0.0)
            dpr = -(d_la * sp8) * r * (1.0 - r)
            dpi = d_i * i * (1.0 - i)
            dprb, dpib, xb = dpr.astype(BF16), dpi.astype(BF16), xc.astype(BF16)
            dxc = dh * sq * i + _dot_nt(dprb, wr_ref[...]) + _dot_nt(dpib, wi_ref[...])
            dwr_ref[...] += _dot_tn(xb, dprb)
            dwi_ref[...] += _dot_tn(xb, dpib)
            dbr_ref[...] += jnp.sum(dpr, axis=0, keepdims=True)
            dbi_ref[...] += jnp.sum(dpi, axis=0, keepdims=True)
            dlam_ref[...] += jnp.sum(d_la * r, axis=0, keepdims=True) * (RG_C * _sigmoid(-lam_v))
            dcb_ref[...] += jnp.sum(dxc, axis=0, keepdims=True)
            taps = _conv_taps(xpad[halo, :], ch)
            for j in range(CONV_W):
                dcw_ref[pl.ds(j, 1), :] += jnp.sum(dxc * taps[j], axis=0, keepdims=True)
            dxc_pad[rows, :] = dxc
            return 0

        lax.fori_loop(0, n_ch, grads, 0)

        def conv_bwd(c, _):
            r0 = pl.multiple_of(c * ch, ch)
            de = dxc_pad[pl.ds(r0, ch + SUBLANES), :]
            dxa = sum(cw_ref[pl.ds(j, 1), :] * _shift_rows(de, ch + SUBLANES - (CONV_W - 1 - j))[0:ch, :]
                      for j in range(CONV_W))
            dp_ref[0, pl.ds(r0, ch), :] = dxa.astype(BF16)
            return 0

        lax.fori_loop(0, n_ch, conv_bwd, 0)

    seg = lambda k: pl.BlockSpec((None, seq, wb), lambda b, s, k=k: (k, s, b))
    blk = pl.BlockSpec((seq, wb), lambda b, s: (s, b))
    vec = pl.BlockSpec((1, wb), lambda b, s: (0, b))
    taps = pl.BlockSpec((CONV_W, wb), lambda b, s: (0, b))
    wsp = pl.BlockSpec((None, wb, wb), lambda b, s: (b, 0, 0))
    vec_shape = jax.ShapeDtypeStruct((1, d), F32)
    w_shape = jax.ShapeDtypeStruct((n_blk, wb, wb), F32)
    pad = pltpu.VMEM((seq + SUBLANES, wb), F32)
    full = pltpu.VMEM((seq, wb), F32)
    return pl.pallas_call(
        body, name="mixer_a_bwd", grid=(n_blk, n_seq),
        in_specs=[seg(0), seg(1), blk, blk, taps, vec, wsp, vec, wsp, vec, vec],
        out_specs=[pl.BlockSpec((2, seq, wb), lambda b, s: (0, s, b)), wsp, wsp, vec, vec, vec, taps, vec],
        out_shape=[jax.ShapeDtypeStruct((2, t_rows, d), BF16), w_shape, w_shape, vec_shape, vec_shape,
                   vec_shape, jax.ShapeDtypeStruct((CONV_W, d), F32), vec_shape],
        scratch_shapes=[pad, pad, full, pad, full, full, full, full, pad],
        compiler_params=_cp("parallel", "arbitrary"))(
            proj, proj, h, d_ya, conv_w, conv_b, w_r, b_r, w_i, b_i, lam)


def _hg_prepare(q_ref, z_ref, lb, rows):
    z = z_ref[rows, :]
    sig, nsig = _sigmoid_pair(z)
    fg = lb + (1.0 - lb) * sig
    log_f = jnp.log(jnp.maximum(fg, F_MIN))
    key = (1.0 - lb) * nsig
    qs, _ = _silu_and_grad(q_ref[rows, :])
    return qs, key, log_f, sig, fg


HG_UNROLL_TERMS = 32
HG_UNROLL_FWD = 32
HG_UNROLL_BWD = 32
HG_HALF = HG_CHUNK // 2
HG_STACK = HG_CHUNK * HG_HALF
assert HG_HALF == SUBLANES


def _half_of(x, s):
    return x[:HG_HALF, :] if s < HG_HALF else x[HG_HALF:, :]


def _hg_decay(g_ref, r0, g_rows, first_row, s):
    rid = lax.broadcasted_iota(jnp.int32, g_rows.shape, 0) + first_row
    gs = g_ref[pl.ds(r0 + s, 1), :]
    return jnp.where(rid >= s, jnp.exp(g_rows - gs), 0.0)


def _half_start(s):
    return 0 if s < HG_HALF else HG_HALF


def _hg_cross_decays(gc):
    rid = lax.broadcasted_iota(jnp.int32, (HG_CHUNK, HG_DK), 0)
    g_mid = gc[HG_HALF - 1:HG_HALF, :]
    e_hi = jnp.where(rid >= HG_HALF, jnp.exp(gc - g_mid), 0.0)
    e_lo = jnp.where(rid < HG_HALF, jnp.exp(g_mid - gc), 0.0)
    return e_hi, e_lo


def _hg_cross(qc, kc, gc):
    e_hi, e_lo = _hg_cross_decays(gc)
    return (qc * e_hi).astype(BF16), (kc * e_lo).astype(BF16)


def _stack(slabs):
    return jnp.concatenate(slabs, axis=0).astype(BF16)


def _slab_row_sums():
    row = lax.broadcasted_iota(jnp.int32, (HG_CHUNK, HG_STACK), 0)
    col = lax.broadcasted_iota(jnp.int32, (HG_CHUNK, HG_STACK), 1)
    lo = row * HG_HALF
    return jnp.where((col >= lo) & (col < lo + HG_HALF), 1.0, 0.0).astype(BF16)


def _for_chunks(n, unroll, *stages, after_trip=None):
    unroll = min(unroll, n)
    assert n % unroll == 0

    def trip(i, _):
        chunks = [i * unroll + u for u in range(unroll)]
        carried = [stages[0](c) for c in chunks]
        for stage in stages[1:]:
            carried = [stage(c, x) for c, x in zip(chunks, carried)]
        if after_trip is not None:
            after_trip(carried)
        return 0

    lax.fori_loop(0, n // unroll, trip, 0)


def _hg_state_chain(states, g_ref, carry_ref, n_chunks, reverse):
    unroll = min(8, n_chunks)
    assert n_chunks % unroll == 0
    carry_ref[...] = jnp.zeros_like(carry_ref)

    def trip(i, _):
        st = carry_ref[...]
        for u in range(unroll):
            k = i * unroll + u
            c = n_chunks - 1 - k if reverse else k
            term = states[c]
            states[c] = st
            st = st * jnp.exp(g_ref[pl.ds(c * HG_CHUNK + HG_CHUNK - 1, 1), :]) + term
        carry_ref[...] = st
        return 0

    lax.fori_loop(0, n_chunks // unroll, trip, 0)


def _hgrn_fwd(proj, lower_bound, hg_gain, seq, carry=None):
    _, t_rows, d = proj.shape
    n_seq, n_head = t_rows // seq, d // HG_DK
    ch = _row_tile(seq, ROW_CHUNK)
    n_chunks = seq // HG_CHUNK

    def body(q_ref, z_ref, v_ref, lb_ref, gain_ref, o_ref, on_ref, qs_s, k_s, g_s, states, st_ref):
        lb = lb_ref[...]
        tri = _group_cumsum_matrix(ch, HG_CHUNK)

        per_block = ch // HG_CHUNK

        def prepare(c):
            rows = pl.ds(pl.multiple_of(c * ch, ch), ch)
            qs, key, log_f, _, _ = _hg_prepare(q_ref, z_ref, lb, rows)
            qs_s[rows, :] = qs
            k_s[rows, :] = key
            return key, _group_cumsum_mxu(log_f, tri)

        def state_terms(c, carried):
            key, g = carried
            r0 = pl.multiple_of(c * ch, ch)
            g_s[pl.ds(r0, ch), :] = g
            terms = []
            for u in range(per_block):
                sl = slice(u * HG_CHUNK, (u + 1) * HG_CHUNK)
                k_end = key[sl, :] * jnp.exp(g[(u + 1) * HG_CHUNK - 1:(u + 1) * HG_CHUNK, :] - g[sl, :])
                vc = v_ref[pl.ds(r0 + u * HG_CHUNK, HG_CHUNK), :]
                terms.append(_dot_tn(vc.astype(BF16), k_end.astype(BF16)))
            return terms

        def store_terms(c, terms):
            for u, term in enumerate(terms):
                states[c * per_block + u] = term

        _for_chunks(seq // ch, 2, prepare, state_terms, store_terms)
        _hg_state_chain(states, g_s, st_ref, n_chunks, reverse=False)
        ones = jnp.ones((HG_DK, HG_DK), BF16)

        def issue(c):
            r0 = pl.multiple_of(c * HG_CHUNK, HG_CHUNK)
            rows = pl.ds(r0, HG_CHUNK)
            qc, gc = qs_s[rows, :], g_s[rows, :]
            o = _dot_nt((qc * jnp.exp(gc)).astype(BF16), states[c].astype(BF16))
            pairs = [qc[_half_start(s):, :] * _hg_decay(g_s, r0, gc[_half_start(s):, :], _half_start(s), s)
                     * k_s[pl.ds(r0 + s, 1), :] for s in range(HG_CHUNK)]
            score = _dot(_stack(pairs), ones)
            return o, score

        def combine(c, issued):
            o, score = issued
            r0 = pl.multiple_of(c * HG_CHUNK, HG_CHUNK)
            halves = [o[:HG_HALF, :], o[HG_HALF:, :]]
            first = 0
            for s in range(HG_CHUNK):
                vs = v_ref[pl.ds(r0 + s, 1), :]
                if s < HG_HALF:
                    halves[0] += score[first:first + HG_HALF, :] * vs
                    first += HG_HALF
                halves[1] += score[first:first + HG_HALF, :] * vs
                first += HG_HALF
            o = jnp.concatenate(halves, axis=0)
            o_ref[pl.ds(r0, HG_CHUNK), :] = o
            on_ref[pl.ds(r0, HG_CHUNK), :] = (o * _rstd(o) * gain_ref[...]).astype(BF16)

        _for_chunks(n_chunks, HG_UNROLL_FWD, issue, combine)

    seg = lambda k: pl.BlockSpec((None, seq, HG_DK), lambda s, h, k=k: (k, s, h))
    blk = pl.BlockSpec((seq, HG_DK), lambda s, h: (s, h))
    full = pltpu.VMEM((seq, HG_DK), F32)
    return _call_carrying(
        body, carry, name="hgrn_fwd", grid=(n_seq, n_head),
        in_specs=[seg(2), seg(3), seg(4), pl.BlockSpec((1, HG_DK), lambda s, h: (0, h)),
                  pl.BlockSpec((1, HG_DK), lambda s, h: (0, 0))],
        out_specs=[blk, blk],
        out_shape=[jax.ShapeDtypeStruct((t_rows, d), F32), jax.ShapeDtypeStruct((t_rows, d), BF16)],
        scratch_shapes=[full, full, full, pltpu.VMEM((n_chunks, HG_DK, HG_DK), F32),
                        pltpu.VMEM((HG_DK, HG_DK), F32)],
        semantics=("parallel", "parallel"), args=(proj, proj, proj, lower_bound, hg_gain))


def _hgrn_bwd(proj, lower_bound, hg_gain, o, d_on, seq, carry=None):
    _, t_rows, d = proj.shape
    n_seq, n_head = t_rows // seq, d // HG_DK
    ch = _row_tile(seq, ROW_CHUNK)
    n_chunks = seq // HG_CHUNK
    cc = HG_CHUNK

    def body(q_ref, z_ref, v_ref, lb_ref, gain_ref, o_ref, don_ref, dp_ref, dlb_ref, dgain_ref,
             qs_s, k_s, g_s, do_s, dlb_acc, states, dstates, carry_ref):
        hh, ss = pl.program_id(0), pl.program_id(1)
        lb = lb_ref[...]

        @pl.when(ss == 0)
        def _():
            dlb_ref[...] = jnp.zeros_like(dlb_ref)

        @pl.when((ss == 0) & (hh == 0))
        def _():
            dgain_ref[...] = jnp.zeros_like(dgain_ref)

        tri = _group_cumsum_matrix(ch, cc)

        def prepare(c):
            rows = pl.ds(pl.multiple_of(c * ch, ch), ch)
            qs, key, log_f, _, _ = _hg_prepare(q_ref, z_ref, lb, rows)
            qs_s[rows, :] = qs
            k_s[rows, :] = key
            do, dgain = _rms_bwd(don_ref[rows, :], o_ref[rows, :], gain_ref[...])
            do_s[rows, :] = do
            dgain_ref[...] += dgain
            return _group_cumsum_mxu(log_f, tri)

        def store_cumsum(c, g):
            g_s[pl.ds(pl.multiple_of(c * ch, ch), ch), :] = g

        _for_chunks(seq // ch, 4, prepare, store_cumsum)

        def chain_terms(c):
            rows = pl.ds(pl.multiple_of(c * cc, cc), cc)
            gc = g_s[rows, :]
            k_end = k_s[rows, :] * jnp.exp(gc[cc - 1:cc, :] - gc)
            q_in = qs_s[rows, :] * jnp.exp(gc)
            return (_dot_tn(v_ref[rows, :].astype(BF16), k_end.astype(BF16)),
                    _dot_tn(do_s[rows, :].astype(BF16), q_in.astype(BF16)))

        def store_terms(c, terms):
            states[c], dstates[c] = terms

        _for_chunks(n_chunks, HG_UNROLL_TERMS, chain_terms, store_terms)
        _hg_state_chain(states, g_s, carry_ref, n_chunks, reverse=False)
        _hg_state_chain(dstates, g_s, carry_ref, n_chunks, reverse=True)
        ones = jnp.ones((HG_DK, HG_DK), BF16)
        row_sums = _slab_row_sums()

        def chunk_rows(c):
            r0 = pl.multiple_of(c * cc, cc)
            return r0, pl.ds(r0, cc)

        def through_state(c):
            r0, rows = chunk_rows(c)
            qc, kc, gc, vc, doc = qs_s[rows, :], k_s[rows, :], g_s[rows, :], v_ref[rows, :], do_s[rows, :]
            st, dst = states[c], dstates[c]
            g_last = gc[cc - 1:cc, :]
            e_last, e_end = jnp.exp(g_last), jnp.exp(g_last - gc)
            dob, vcb, dstb = doc.astype(BF16), vc.astype(BF16), dst.astype(BF16)
            dqs = _dot(dob, st.astype(BF16))
            dk_state = _dot(vcb, dstb)
            dv = _dot_nt((kc * e_end).astype(BF16), dstb)
            cots = [_half_of(doc, s) * v_ref[pl.ds(r0 + s, 1), :] for s in range(cc)]
            d_score = _dot(_stack(cots), ones)
            x, y = _hg_cross(qc, kc, gc)
            cross = (_dot_nt(dob, vcb), _dot_nt(vcb, dob), _dot_nt(y, x))
            return dqs, dk_state, dv, d_score, e_last * jnp.sum(dst * st, axis=0, keepdims=True), cross

        def pair_terms(c, carried):
            dqs, dk_state, dv, d_score, d_glast, (da_cross, da_cross_t, a_cross_t) = carried
            r0, rows = chunk_rows(c)
            qc, kc, gc = qs_s[rows, :], k_s[rows, :], g_s[rows, :]
            dqs = dqs * jnp.exp(gc)
            dk_state = dk_state * jnp.exp(gc[cc - 1:cc, :] - gc)
            d_glast = d_glast + jnp.sum(kc * dk_state, axis=0, keepdims=True)
            dqs_half = [dqs[:HG_HALF, :], dqs[HG_HALF:, :]]
            pairs, dk_terms = [], []
            for s in range(cc):
                qv = _half_of(qc, s)
                decay = _hg_decay(g_s, r0, _half_of(gc, s), _half_start(s), s)
                ks = k_s[pl.ds(r0 + s, 1), :]
                da_decay = d_score[s * HG_HALF:(s + 1) * HG_HALF, :] * decay
                pairs.append(qv * decay * ks)
                dk_terms.append(da_decay * qv)
                dqs_half[s // HG_HALF] += da_decay * ks
            score = _dot(_stack(pairs), ones)
            dk = dk_state + _dot(row_sums, _stack(dk_terms))
            x, y = _hg_cross(qc, kc, gc)
            cross = (_dot(da_cross.astype(BF16), y), _dot(da_cross_t.astype(BF16), x),
                     _dot(a_cross_t.astype(BF16), do_s[rows, :].astype(BF16)))
            return jnp.concatenate(dqs_half, axis=0), dk, dv, score, d_glast, cross

        def value_terms(c, carried):
            dqs, dk, dv, score, d_glast, (dx_cross, dy_cross, dv_cross) = carried
            _, rows = chunk_rows(c)
            doc, gc = do_s[rows, :], g_s[rows, :]
            dv_terms = [score[s * HG_HALF:(s + 1) * HG_HALF, :] * _half_of(doc, s) for s in range(cc)]
            e_hi, e_lo = _hg_cross_decays(gc)
            return (dqs + dx_cross * e_hi, dk + dy_cross * e_lo,
                    dv + dv_cross + _dot(row_sums, _stack(dv_terms)), d_glast)

        dlb_acc[...] = jnp.zeros_like(dlb_acc)

        def store(c, x):
            dqs, dk, dv, d_glast = x
            _, rows = chunk_rows(c)
            d_g = qs_s[rows, :] * dqs - k_s[rows, :] * dk
            d_logf = _seg_cumsum(d_g, cc, reverse=True) + d_glast
            sig, nsig = _sigmoid_pair(z_ref[rows, :])
            fg = lb + (1.0 - lb) * sig
            _, dsilu = _silu_and_grad(q_ref[rows, :])
            d_gate = jnp.where(fg > F_MIN, d_logf / fg, 0.0) - dk
            dp_ref[0, rows, :] = (dqs * dsilu).astype(BF16)
            dp_ref[1, rows, :] = (d_gate * (1.0 - lb) * sig * nsig).astype(BF16)
            dp_ref[2, rows, :] = dv.astype(BF16)
            return d_gate * nsig

        def add_lower_bound_grads(per_chunk):
            while len(per_chunk) > 1:
                per_chunk = [a + b for a, b in zip(per_chunk[::2], per_chunk[1::2])]
            dlb_acc[...] += per_chunk[0]

        _for_chunks(n_chunks, HG_UNROLL_BWD, through_state, pair_terms, value_terms, store,
                    after_trip=add_lower_bound_grads)
        dlb_ref[...] += jnp.sum(dlb_acc[...], axis=0, keepdims=True)

    seg = lambda k: pl.BlockSpec((None, seq, HG_DK), lambda h, s, k=k: (k, s, h))
    blk = pl.BlockSpec((seq, HG_DK), lambda h, s: (s, h))
    full = pltpu.VMEM((seq, HG_DK), F32)
    return _call_carrying(
        body, carry, name="hgrn_bwd", grid=(n_head, n_seq),
        in_specs=[seg(2), seg(3), seg(4), pl.BlockSpec((1, HG_DK), lambda h, s: (0, h)),
                  pl.BlockSpec((1, HG_DK), lambda h, s: (0, 0)), blk, blk],
        out_specs=[pl.BlockSpec((3, seq, HG_DK), lambda h, s: (0, s, h)),
                   pl.BlockSpec((1, HG_DK), lambda h, s: (0, h)),
                   pl.BlockSpec((1, HG_DK), lambda h, s: (0, 0))],
        out_shape=[jax.ShapeDtypeStruct((3, t_rows, d), BF16), jax.ShapeDtypeStruct((1, d), F32),
                   jax.ShapeDtypeStruct((1, HG_DK), F32)],
        scratch_shapes=[full, full, full, full, pltpu.VMEM((cc, HG_DK), F32),
                        pltpu.VMEM((n_chunks, HG_DK, HG_DK), F32), pltpu.VMEM((n_chunks, HG_DK, HG_DK), F32),
                        pltpu.VMEM((HG_DK, HG_DK), F32)],
        semantics=("arbitrary", "arbitrary"), args=(proj, proj, proj, lower_bound, hg_gain, o, d_on))


def _mesh_place():
    x, y, c = lax.axis_index("x"), lax.axis_index("y"), lax.axis_index("c")
    return x, y, c


def _peer(place, k):
    x, y, c = place
    px = 1 - x if k & 4 else x
    py = 1 - y if k & 2 else y
    pc = 1 - c if k & 1 else c
    return (px, py, pc), 4 * px + 2 * py + pc


class _Exchange:
    def __init__(self, srcs, gather):
        self.n = len(srcs)
        self.gather = gather
        self.out_shape = [jax.ShapeDtypeStruct((N_DEV,) + tuple(s.shape if gather else s.shape[1:]), s.dtype)
                          for s in srcs]
        self.scratch = [pltpu.SemaphoreType.DMA((self.n * (N_DEV - 1),)),
                        pltpu.SemaphoreType.DMA((self.n * (N_DEV - 1),)),
                        pltpu.SemaphoreType.DMA((self.n,))]

    def _copies(self, src_refs, out_refs, sems):
        send_sems, recv_sems, local_sems = sems
        place = _mesh_place()
        me = 4 * place[0] + 2 * place[1] + place[2]
        local, sends, recvs = [], [], []
        for a, (src, out) in enumerate(zip(src_refs, out_refs)):
            outgoing = (lambda idx, src=src: src) if self.gather else (lambda idx, src=src: src.at[idx])
            local.append(pltpu.make_async_copy(outgoing(me), out.at[me], local_sems.at[a]))
            for k in range(1, N_DEV):
                peer, peer_idx = _peer(place, k)
                sem = a * (N_DEV - 1) + k - 1
                sends.append(pltpu.make_async_remote_copy(
                    src_ref=outgoing(peer_idx), dst_ref=out.at[me], send_sem=send_sems.at[sem],
                    recv_sem=recv_sems.at[sem], device_id=peer, device_id_type=MESH_ID))
                recvs.append(pltpu.make_async_remote_copy(
                    src_ref=outgoing(peer_idx), dst_ref=out.at[peer_idx], send_sem=send_sems.at[sem],
                    recv_sem=recv_sems.at[sem], device_id=peer, device_id_type=MESH_ID))
        return local, sends, recvs

    def start(self, src_refs, out_refs, sems):
        local, sends, _ = self._copies(src_refs, out_refs, sems)
        for cp in local + sends:
            cp.start()

    def wait(self, src_refs, out_refs, sems):
        local, sends, recvs = self._copies(src_refs, out_refs, sems)
        for cp in recvs:
            cp.wait_recv()
        for cp in sends:
            cp.wait_send()
        for cp in local:
            cp.wait()


def _call_carrying(body, carry, *, name, grid, in_specs, out_specs, out_shape, scratch_shapes, semantics, args):
    if carry is None:
        outs = pl.pallas_call(body, name=name, grid=grid, in_specs=in_specs, out_specs=out_specs,
                              out_shape=out_shape, scratch_shapes=scratch_shapes,
                              compiler_params=_cp(*semantics))(*args)
        return outs, []
    srcs, gather = carry
    ex = _Exchange(srcs, gather)
    n, n_in, n_out, n_scr = ex.n, len(in_specs), len(out_specs), len(scratch_shapes)

    def wrapped(*refs):
        ins, refs = refs[:n_in], refs[n_in:]
        src_refs, refs = refs[:n], refs[n:]
        outs, refs = refs[:n_out], refs[n_out:]
        dst_refs, refs = refs[:n], refs[n:]
        scratch, sems = refs[:n_scr], refs[n_scr:]
        first, last = None, None
        for axis, size in enumerate(grid):
            i = pl.program_id(axis)
            first = (i == 0) if first is None else first & (i == 0)
            last = (i == size - 1) if last is None else last & (i == size - 1)

        @pl.when(first)
        def _():
            ex.start(src_refs, dst_refs, sems)

        body(*ins, *outs, *scratch)

        @pl.when(last)
        def _():
            ex.wait(src_refs, dst_refs, sems)

    any_space = pl.BlockSpec(memory_space=pl.ANY)
    res = pl.pallas_call(
        wrapped, name=name + "_carrying", grid=grid, in_specs=list(in_specs) + [any_space] * n,
        out_specs=list(out_specs) + [any_space] * n, out_shape=list(out_shape) + ex.out_shape,
        scratch_shapes=list(scratch_shapes) + ex.scratch,
        compiler_params=_cp(*(["arbitrary"] * len(grid))))(*args, *srcs)
    return res[:n_out], res[n_out:]


def _exchange(srcs, name, gather):
    ex = _Exchange(srcs, gather)
    n = ex.n

    def body(*refs):
        src_refs, out_refs, sems = refs[:n], refs[n:2 * n], refs[2 * n:]
        ex.start(src_refs, out_refs, sems)
        ex.wait(src_refs, out_refs, sems)

    any_space = pl.BlockSpec(memory_space=pl.ANY)
    return pl.pallas_call(
        body, name=name, in_specs=[any_space] * n, out_specs=[any_space] * n,
        out_shape=ex.out_shape, scratch_shapes=ex.scratch)(*srcs)


def _gather_two_level(src, name):
    def body(x_ref, out_ref, send_sems, recv_sems, local_sem):
        x, y, c = _mesh_place()
        me, sibling = (x, y, c), (x, y, 1 - c)
        chips = [(1 - x, y), (x, 1 - y), (1 - x, 1 - y)]

        def slot(px, py, pc):
            return out_ref.at[4 * px + 2 * py + pc]

        def copy(k, block, to, source=None):
            return pltpu.make_async_remote_copy(
                src_ref=slot(*block) if source is None else source, dst_ref=slot(*block),
                send_sem=send_sems.at[k], recv_sem=recv_sems.at[k], device_id=to, device_id_type=MESH_ID)

        mine = pltpu.make_async_copy(x_ref, slot(*me), local_sem)
        mine.start()
        first = [copy(0, me, sibling, source=x_ref)]
        first += [copy(1 + j, me, (*chip, c), source=x_ref) for j, chip in enumerate(chips)]
        for cp in first:
            cp.start()
        passed = [copy(4 + j, (*chip, c), sibling) for j, chip in enumerate(chips)]
        for j, chip in enumerate(chips):
            copy(1 + j, (*chip, c), me).wait_recv()
            passed[j].start()
        copy(0, sibling, me).wait_recv()
        for j, chip in enumerate(chips):
            copy(4 + j, (*chip, 1 - c), me).wait_recv()
        for cp in first + passed:
            cp.wait_send()
        mine.wait()

    any_space = pl.BlockSpec(memory_space=pl.ANY)
    n_copies = N_DEV - 1
    return pl.pallas_call(
        body, name=name, in_specs=[any_space], out_specs=any_space,
        out_shape=jax.ShapeDtypeStruct((N_DEV,) + tuple(src.shape), src.dtype),
        scratch_shapes=[pltpu.SemaphoreType.DMA((n_copies,)), pltpu.SemaphoreType.DMA((n_copies,)),
                        pltpu.SemaphoreType.DMA])(src)


def _reduce_adamw(parts, w, m, v, name):
    rows, cols = w.shape
    n_seg = len(parts)
    seg_rows = rows // n_seg
    tr = _row_tile(seg_rows, 128)
    per_seg = seg_rows // tr
    c1 = np.float32(1.0 - ADAM_B1 ** ADAM_STEP)
    c2 = np.float32(1.0 - ADAM_B2 ** ADAM_STEP)

    def body(*refs):
        p_refs = refs[:n_seg]
        w_ref, m_ref, v_ref, g_ref, d_ref, nm_ref, nv_ref = refs[n_seg:]
        seg = pl.program_id(0)
        for k, p_ref in enumerate(p_refs):
            @pl.when(seg == k)
            def _(p_ref=p_ref):
                g = p_ref[0].astype(F32)
                for dev in range(1, N_DEV):
                    g = g + p_ref[dev].astype(F32)
                g_ref[...] = g

        g = g_ref[...]
        nm = ADAM_B1 * m_ref[...] + (1.0 - ADAM_B1) * g
        nv = ADAM_B2 * v_ref[...] + (1.0 - ADAM_B2) * (g * g)
        nm_ref[...] = nm
        nv_ref[...] = nv
        d_ref[...] = -ADAM_LR * ((nm / c1) / (jnp.sqrt(nv / c2) + ADAM_EPS) + ADAM_WD * w_ref[...])

    def part_spec(k):
        return pl.BlockSpec((N_DEV, tr, cols), lambda s, i, k=k: (0, jnp.where(s == k, i, 0), 0))

    blk = pl.BlockSpec((tr, cols), lambda s, i: (s * per_seg + i, 0))
    shp = jax.ShapeDtypeStruct((rows, cols), F32)
    return pl.pallas_call(
        body, name=name, grid=(n_seg, per_seg),
        in_specs=[part_spec(k) for k in range(n_seg)] + [blk, blk, blk],
        out_specs=[blk, blk, blk, blk], out_shape=[shp, shp, shp, shp],
        compiler_params=_cp("arbitrary", "arbitrary"))(*parts, w, m, v)


def _pack(arrays, lead=0):
    parts = []
    for a in arrays:
        f = a.reshape(a.shape[:lead] + (-1, LANES))
        pad = -f.shape[lead] % PACK_ROWS
        if pad:
            f = jnp.pad(f, [(0, 0)] * lead + [(0, pad), (0, 0)])
        parts.append(f)
    return jnp.concatenate(parts, axis=lead)


def _unpack(buf, shapes, lead=0):
    out, r = [], 0
    for shp in shapes:
        n = int(np.prod(shp)) // LANES
        part = lax.slice_in_dim(buf, r, r + n, axis=lead)
        out.append(part.reshape(buf.shape[:lead] + tuple(shp)))
        r += n + (-n % PACK_ROWS)
    return out


REPLICATED = ("lb_logits", "norm_mix", "conv_b", "b_r", "b_i", "lam", "hg_norm", "norm_mlp", "norm_final")
SMALL_SHARDED = ("conv_w", "w_r", "w_i")
LARGE_SHARDED = ("w_in", "w_out", "w_up", "w_down")
WEIGHTS = ("lb_logits", "norm_mix", "w_in", "conv_w", "conv_b", "w_r", "b_r", "w_i", "b_i", "lam", "hg_norm",
           "w_out", "norm_mlp", "w_up", "w_down", "norm_final")


def _matmul_weight_shards(p):
    depth = p["w_in"].shape[0]
    cast = {k: p[k].astype(BF16) for k in LARGE_SHARDED}
    return ([cast["w_in"][l] for l in range(depth)],
            [[cast[k][l] for k in ("w_out", "w_up", "w_down")] for l in range(depth)])


def _gathered_rest(got):
    w_out, w_up, w_down = got
    d = w_out.shape[2]
    return dict(w_out=w_out.reshape(d, d), w_up=w_up, w_down=w_down)


def _unpack_mixer_weights(small, p):
    depth, d, _ = p["w_in"].shape
    n_blk = d // RG_BLOCK_W
    conv_w, w_r, w_i = _unpack(small, [p["conv_w"].shape, p["w_r"].shape, p["w_i"].shape], lead=1)
    conv_w = conv_w.transpose(1, 2, 0, 3).reshape(depth, CONV_W, d)
    w_r = w_r.transpose(1, 2, 0, 3, 4).reshape(depth, n_blk, RG_BLOCK_W, RG_BLOCK_W).astype(BF16)
    w_i = w_i.transpose(1, 2, 0, 3, 4).reshape(depth, n_blk, RG_BLOCK_W, RG_BLOCK_W).astype(BF16)
    return conv_w, w_r, w_i


def _local_step(x, target, p):
    bl, seq, d = x.shape
    depth = p["w_in"].shape[0]
    t_rows = bl * seq
    row = lambda a, l: a[l:l + 1]
    lbs = _lower_bounds_fwd(p["lb_logits"])
    shard_in, shard_rest = _matmul_weight_shards(p)
    w_in = _gather_two_level(shard_in[0], "gather_w_in")
    cur = x.reshape(t_rows, d)
    saved, layers = [], []
    for l in range(depth):
        if l == 0:
            (proj, gates, h), small = _inproj_fwd(cur, row(p["norm_mix"], l), w_in,
                                                  carry=([_pack([p["conv_w"], p["w_r"], p["w_i"]])], True))
            conv_w, w_r, w_i = _unpack_mixer_weights(small[0], p)
        else:
            (proj, gates, h), _ = _inproj_fwd(cur, row(p["norm_mix"], l), w_in)
        w = dict(w_in=w_in, conv_w=conv_w[l], w_r=w_r[l], w_i=w_i[l])
        hs, y_a = _mixer_a_fwd(proj, w["conv_w"], row(p["conv_b"], l), w["w_r"], row(p["b_r"], l), w["w_i"],
                               row(p["b_i"], l), row(p["lam"], l), seq)
        (o, o_n), got = _hgrn_fwd(proj, row(lbs, l), row(p["hg_norm"], l), seq,
                                  carry=(shard_rest[l] + ([shard_in[l + 1]] if l + 1 < depth else []), True))
        w.update(_gathered_rest(got[:3]))
        w_in = got[3] if l + 1 < depth else None
        layers.append(w)
        x_mid, y = _merge_out_fwd(gates, y_a, o_n, cur, w["w_out"])
        x_out, u, h2 = _mlp_fwd(x_mid, row(p["norm_mlp"], l), w["w_up"], w["w_down"])
        saved.append(dict(x_in=cur, proj=proj, gates=gates, h=h, hs=hs, y_a=y_a, o=o, o_n=o_n, x_mid=x_mid, y=y, u=u, h2=h2))
        cur = x_out
    loss8, dx, dxb, g_norm_final = _loss_head(cur, p["norm_final"].reshape(1, d), target.reshape(t_rows, d))
    small = ("norm_mix", "conv_w", "conv_b", "w_r", "b_r", "w_i", "b_i", "lam", "hg_norm", "norm_mlp")
    g = {k: [None] * depth for k in small}
    d_lbs, received = [None] * depth, [None] * depth
    g_w_in = None
    for l in reversed(range(depth)):
        s, w = saved[l], layers[l]
        dx_mid, dx_mid_b, du, act, g["norm_mlp"][l] = _mlp_bwd(dx, dxb, s["u"], s["x_mid"], row(p["norm_mlp"], l),
                                                               w["w_up"], w["w_down"])
        g_w_down = _wgrad(act, dxb[None], "wgrad_down")
        g_w_up = _wgrad(s["h2"][None], du, "wgrad_up")
        d_ya, d_on, dp_c = _outproj_bwd(dx_mid_b, w["w_out"], s["gates"], s["y_a"], s["o_n"])
        g_w_out = _wgrad(s["y"][None], dx_mid_b[None], "wgrad_out").reshape(N_DEV, d // N_DEV, d)
        (dp_b, d_lbs[l], g["hg_norm"][l]), got = _hgrn_bwd(
            s["proj"], row(lbs, l), row(p["hg_norm"], l), s["o"], d_on, seq,
            carry=([g_w_out, g_w_up, g_w_down] + ([g_w_in] if g_w_in is not None else []), False))
        received[l] = [None] + list(got[:3])
        if g_w_in is not None:
            received[l + 1][0] = got[3]
        (dp_a, g["w_r"][l], g["w_i"][l], g["b_r"][l], g["b_i"][l], g["lam"][l], g["conv_w"][l],
         g["conv_b"][l]) = _mixer_a_bwd(s["proj"], s["hs"], d_ya, w["conv_w"], row(p["conv_b"], l), w["w_r"],
                                        row(p["b_r"], l), w["w_i"], row(p["b_i"], l), row(p["lam"], l), seq)
        hb = s["h"][None]
        g_w_in = jnp.concatenate([_wgrad(hb, dp_a, "wgrad_in_pair"), _wgrad(hb, dp_b, "wgrad_in_triple"),
                                  _wgrad(hb, dp_c, "wgrad_in_triple")], axis=0)
        carry = None
        if l == 0:
            carry = ([g_w_in, _mixer_grads_by_owner(g, d)], False)
        (dx, dxb, g["norm_mix"][l]), got = _inproj_bwd(dx_mid, dp_a, dp_b, dp_c, w["w_in"], s["x_in"],
                                                       row(p["norm_mix"], l), carry=carry)
    received[0][0], received_mixer = got
    grads = {k: jnp.stack(v) for k, v in g.items()}
    for k in ("norm_mix", "conv_b", "b_r", "b_i", "lam", "hg_norm", "norm_mlp"):
        grads[k] = grads[k][:, 0]
    grads["lb_logits"] = _lower_bounds_bwd(p["lb_logits"], jnp.concatenate(d_lbs, axis=0))
    grads["norm_final"] = g_norm_final[0]
    return loss8[0, 0], dx.reshape(bl, seq, d), grads, received, received_mixer


def _mixer_grads_by_owner(g, d):
    d8, n_blk, rb = d // N_DEV, d // RG_BLOCK_W, RG_BLOCK_W // N_DEV
    depth = len(g["conv_w"])
    conv_w, w_r, w_i = (jnp.stack(g[k]) for k in SMALL_SHARDED)
    return _pack([conv_w.reshape(depth, CONV_W, N_DEV, d8).transpose(2, 0, 1, 3),
                  w_r.reshape(depth, n_blk, N_DEV, rb, RG_BLOCK_W).transpose(2, 0, 1, 3, 4),
                  w_i.reshape(depth, n_blk, N_DEV, rb, RG_BLOCK_W).transpose(2, 0, 1, 3, 4)],
                 lead=1).astype(BF16)


def _update(p, mom1, mom2, grads, received, received_mixer):
    depth = p["w_in"].shape[0]
    out = {}

    for i, k in enumerate(LARGE_SHARDED):
        shp = p[k].shape
        flat = lambda a: a.reshape(shp[0] * shp[1], shp[2])
        parts = [received[l][i] for l in range(depth)]
        res = _reduce_adamw(parts, flat(p[k]), flat(mom1[k]), flat(mom2[k]), "adamw_" + k)
        out[k] = [r.reshape(shp) for r in res]

    res = _reduce_adamw([received_mixer], *[_pack([src[k] for k in SMALL_SHARDED]) for src in (p, mom1, mom2)],
                        "adamw_mixer")
    shapes = [p[k].shape for k in SMALL_SHARDED]
    for i, vals in enumerate(zip(*[_unpack(r, shapes) for r in res])):
        out[SMALL_SHARDED[i]] = list(vals)

    parts = _exchange([_pack([grads[k] for k in REPLICATED])], "gather_grad_replicated", gather=True)
    res = _reduce_adamw(parts, *[_pack([src[k] for k in REPLICATED]) for src in (p, mom1, mom2)],
                        "adamw_replicated")
    shapes = [p[k].shape for k in REPLICATED]
    for i, vals in enumerate(zip(*[_unpack(r, shapes) for r in res])):
        out[REPLICATED[i]] = list(vals)

    return tuple(out[k][i] for i in range(4) for k in WEIGHTS)


def kernel(x, lb_logits, norm_mix, w_in, conv_w, conv_b, w_r, b_r, w_i, b_i, lam, hg_norm, w_out, norm_mlp, w_up, w_down, norm_final, loss_target, m_lb_logits, m_norm_mix, m_w_in, m_conv_w, m_conv_b, m_w_r, m_b_r, m_w_i, m_b_i, m_lam, m_hg_norm, m_w_out, m_norm_mlp, m_w_up, m_w_down, m_norm_final, v_lb_logits, v_norm_mix, v_w_in, v_conv_w, v_conv_b, v_w_r, v_b_r, v_w_i, v_b_i, v_lam, v_hg_norm, v_w_out, v_norm_mlp, v_w_up, v_w_down, v_norm_final):
    p = dict(lb_logits=lb_logits, norm_mix=norm_mix, w_in=w_in, conv_w=conv_w, conv_b=conv_b, w_r=w_r, b_r=b_r,
             w_i=w_i, b_i=b_i, lam=lam, hg_norm=hg_norm, w_out=w_out, norm_mlp=norm_mlp, w_up=w_up,
             w_down=w_down, norm_final=norm_final)
    mom1 = dict(lb_logits=m_lb_logits, norm_mix=m_norm_mix, w_in=m_w_in, conv_w=m_conv_w, conv_b=m_conv_b,
                w_r=m_w_r, b_r=m_b_r, w_i=m_w_i, b_i=m_b_i, lam=m_lam, hg_norm=m_hg_norm, w_out=m_w_out,
                norm_mlp=m_norm_mlp, w_up=m_w_up, w_down=m_w_down, norm_final=m_norm_final)
    mom2 = dict(lb_logits=v_lb_logits, norm_mix=v_norm_mix, w_in=v_w_in, conv_w=v_conv_w, conv_b=v_conv_b,
                w_r=v_w_r, b_r=v_b_r, w_i=v_w_i, b_i=v_b_i, lam=v_lam, hg_norm=v_hg_norm, w_out=v_w_out,
                norm_mlp=v_norm_mlp, w_up=v_w_up, w_down=v_w_down, norm_final=v_norm_final)
    loss, grad_x, grads, received, received_mixer = _local_step(x, loss_target, p)
    loss = lax.psum(loss, ("x", "y", "c"))
    return (loss, grad_x) + _update(p, mom1, mom2, grads, received, received_mixer)
```

```python
import numpy as np

import jax
import jax.numpy as jnp
from jax import lax
from jax.experimental import pallas as pl
from jax.experimental.pallas import tpu as pltpu

F32 = jnp.float32
BF16 = jnp.bfloat16
MESH_ID = pl.DeviceIdType.MESH

N_DEV = 8
N_MIXER_SEGMENTS = 5
NORM_EPS = 1e-6
RG_C = 8.0
RG_BLOCK_W = 256
CONV_W = 4
HG_DK = 128
F_MIN = 1e-30
HG_CHUNK = 16
SUBLANES = 8
LANES = 128
PACK_ROWS = 16
SCAN_GROUP = 16
ROW_CHUNK = 256
ROW_TILE_WEIGHT_STREAM = 1024
WGRAD_TOKEN_TILE = 2048
VMEM_LIMIT_V7X = 56 * 1024 * 1024

ADAM_LR = 0.001
ADAM_B1 = 0.9
ADAM_B2 = 0.999
ADAM_EPS = 1e-08
ADAM_WD = 0.01
ADAM_STEP = 10

GELU_C = 0.7978845608028654
GELU_K = 0.044715


def _cp(*sem):
    return pltpu.CompilerParams(dimension_semantics=sem, vmem_limit_bytes=VMEM_LIMIT_V7X)


def _row_tile(n, cap):
    if n <= cap:
        return n
    t = cap - cap % 16
    while n % t:
        t -= 16
    return t


def _dot(a, b):
    return jnp.dot(a, b, preferred_element_type=F32)


def _dot_nt(a, b):
    return lax.dot_general(a, b, (((1,), (1,)), ((), ())), preferred_element_type=F32)


def _dot_tn(a, b):
    return lax.dot_general(a, b, (((0,), (0,)), ((), ())), preferred_element_type=F32)


def _sigmoid(x):
    return jax.nn.sigmoid(x)


def _sigmoid_pair(x):
    e = jnp.exp(-jnp.abs(x))
    r = 1.0 / (1.0 + e)
    er = e * r
    pos = x >= 0.0
    return jnp.where(pos, r, er), jnp.where(pos, er, r)


def _log1p_pos(y):
    return jnp.where(y < 0.01, y * (1.0 - y * (0.5 - y * (1.0 / 3.0))), jnp.log(1.0 + y))


def _softplus(x):
    return jnp.maximum(x, 0.0) + _log1p_pos(jnp.exp(-jnp.abs(x)))


def _one_minus_exp(x):
    series = -x * (1.0 + x * 0.5 * (1.0 + x * (1.0 / 3.0) * (1.0 + x * 0.25 * (1.0 + x * 0.2))))
    return jnp.where(x > -0.1, series, 1.0 - jnp.exp(x))


def _gelu_and_grad(x):
    x2 = x * x
    t = jnp.tanh(GELU_C * x * (1.0 + GELU_K * x2))
    g = 0.5 * x * (1.0 + t)
    dg = 0.5 * (1.0 + t) + 0.5 * x * (1.0 - t * t) * GELU_C * (1.0 + 3.0 * GELU_K * x2)
    return g, dg


def _silu_and_grad(x):
    s = _sigmoid(x)
    return x * s, s * (1.0 + x * (1.0 - s))


def _rstd(x):
    return lax.rsqrt(jnp.mean(x * x, axis=-1, keepdims=True) + NORM_EPS)


def _rms_bwd(dh, x, g):
    rstd = _rstd(x)
    xh = x * rstd
    dxh = dh * g
    dx = rstd * (dxh - xh * jnp.mean(dxh * xh, axis=-1, keepdims=True))
    return dx, jnp.sum(dh * xh, axis=0, keepdims=True)


def _shift_rows(x, k):
    n = x.shape[0]
    k = k % n
    return x if k == 0 else pltpu.roll(x, k, axis=0)


def _seg_cumsum(x, seg, reverse=False):
    n = x.shape[0]
    rid = lax.broadcasted_iota(jnp.int32, x.shape, 0) & (seg - 1)
    d = 1
    while d < seg:
        if reverse:
            x = jnp.where(rid < seg - d, x + _shift_rows(x, n - d), x)
        else:
            x = jnp.where(rid >= d, x + _shift_rows(x, d), x)
        d *= 2
    return x


def _group_cumsum_matrix(n, seg):
    row = lax.broadcasted_iota(jnp.int32, (n, n), 0)
    col = lax.broadcasted_iota(jnp.int32, (n, n), 1)
    same_group = (row & ~(seg - 1)) == (col & ~(seg - 1))
    return jnp.where(same_group & (col <= row), 1.0, 0.0).astype(BF16)


def _group_cumsum_mxu(x, tri):
    hi = x.astype(BF16)
    lo = (x - hi.astype(F32)).astype(BF16)
    return _dot(tri, hi) + _dot(tri, lo)


def _scan_rows(a_ref, b_ref, out_ref, n_rows, width, reverse):
    gr = min(SCAN_GROUP, n_rows)
    rid = lax.broadcasted_iota(jnp.int32, (gr, width), 0)
    n_groups = n_rows // gr
    per_trip = min(4, n_groups)
    assert n_groups % per_trip == 0

    def local_scan(g):
        r0 = pl.multiple_of(g * gr, gr)
        a = a_ref[pl.ds(r0, gr), :]
        b = b_ref[pl.ds(r0, gr), :]
        d = 1
        while d < gr:
            if reverse:
                keep = rid < gr - d
                a_sh, b_sh = _shift_rows(a, gr - d), _shift_rows(b, gr - d)
            else:
                keep = rid >= d
                a_sh, b_sh = _shift_rows(a, d), _shift_rows(b, d)
            b = jnp.where(keep, a * b_sh + b, b)
            a = jnp.where(keep, a * a_sh, a)
            d *= 2
        return r0, a, b

    def trip(i, carry):
        first = i * per_trip
        groups = [n_groups - 1 - (first + u) if reverse else first + u for u in range(per_trip)]
        for r0, a, b in [local_scan(g) for g in groups]:
            out = a * carry + b
            out_ref[pl.ds(r0, gr), :] = out
            edge = out[0:1, :] if reverse else out[gr - 1:gr, :]
            carry = jnp.broadcast_to(edge, (gr, width))
        return carry

    lax.fori_loop(0, n_groups // per_trip, trip, jnp.zeros((gr, width), F32))


def _lb_softmax_rows(x_ref, depth):
    rows = [x_ref[pl.ds(l, 1), :] for l in range(depth)]
    top = rows[0]
    for r in rows[1:]:
        top = jnp.maximum(top, r)
    e = [jnp.exp(r - top) for r in rows]
    tot = e[0]
    for r in e[1:]:
        tot = tot + r
    return [r / tot for r in e]


def _lower_bounds_fwd(lb_logits):
    depth, d = lb_logits.shape

    def body(x_ref, o_ref):
        sm = _lb_softmax_rows(x_ref, depth)
        cum = jnp.zeros((1, d), F32)
        for l in range(depth):
            cum = cum + sm[l]
            o_ref[pl.ds(l, 1), :] = jnp.clip(cum - sm[0], 0.0, 1.0)

    return pl.pallas_call(body, name="lower_bounds_fwd",
                          out_shape=jax.ShapeDtypeStruct((depth, d), F32))(lb_logits)


def _lower_bounds_bwd(lb_logits, d_lbs):
    depth, d = lb_logits.shape

    def body(x_ref, g_ref, o_ref):
        sm = _lb_softmax_rows(x_ref, depth)
        cum = jnp.zeros((1, d), F32)
        d_cum = []
        for l in range(depth):
            cum = cum + sm[l]
            v = cum - sm[0]
            d_cum.append(jnp.where((v > 0.0) & (v < 1.0), g_ref[pl.ds(l, 1), :], 0.0))
        d_sm = []
        tail = jnp.zeros((1, d), F32)
        for l in reversed(range(depth)):
            tail = tail + d_cum[l]
            d_sm.append(tail)
        d_sm = d_sm[::-1]
        d_sm[0] = d_sm[0] - tail
        inner = jnp.zeros((1, d), F32)
        for l in range(depth):
            inner = inner + sm[l] * d_sm[l]
        for l in range(depth):
            o_ref[pl.ds(l, 1), :] = sm[l] * (d_sm[l] - inner)

    return pl.pallas_call(body, name="lower_bounds_bwd",
                          out_shape=jax.ShapeDtypeStruct((depth, d), F32))(lb_logits, d_lbs)


def _inproj_fwd(x, gain, w_seg, carry=None):
    t_rows, d = x.shape
    tm = _row_tile(t_rows, ROW_TILE_WEIGHT_STREAM)
    n_gate = N_DEV - N_MIXER_SEGMENTS

    def body(x_ref, g_ref, w_ref, proj_ref, gates_ref, h_ref):
        j = pl.program_id(1)

        @pl.when(j == 0)
        def _():
            xv = x_ref[...]
            h_ref[...] = (xv * _rstd(xv) * g_ref[...]).astype(BF16)

        @pl.when(j < N_MIXER_SEGMENTS)
        def _():
            proj_ref[...] = _dot(h_ref[...], w_ref[...])

        @pl.when(j >= N_MIXER_SEGMENTS)
        def _():
            gates_ref[...] = _dot(h_ref[...], w_ref[...]).astype(BF16)

    return _call_carrying(
        body, carry, name="inproj_fwd", grid=(t_rows // tm, N_DEV),
        in_specs=[pl.BlockSpec((tm, d), lambda i, j: (i, 0)),
                  pl.BlockSpec((1, d), lambda i, j: (0, 0)),
                  pl.BlockSpec((None, d, d), lambda i, j: (j, 0, 0))],
        out_specs=[pl.BlockSpec((None, tm, d), lambda i, j: (jnp.minimum(j, N_MIXER_SEGMENTS - 1), i, 0)),
                   pl.BlockSpec((None, tm, d), lambda i, j: (jnp.maximum(j - N_MIXER_SEGMENTS, 0), i, 0)),
                   pl.BlockSpec((tm, d), lambda i, j: (i, 0))],
        out_shape=[jax.ShapeDtypeStruct((N_MIXER_SEGMENTS, t_rows, d), F32),
                   jax.ShapeDtypeStruct((n_gate, t_rows, d), BF16),
                   jax.ShapeDtypeStruct((t_rows, d), BF16)],
        scratch_shapes=[], semantics=("parallel", "arbitrary"), args=(x, gain, w_seg))


def _merge_out_fwd(gates, y_a, o_n, x, w_out):
    t_rows, d = x.shape
    tm = _row_tile(t_rows, 256)

    def body(g_ref, ma_ref, mb_ref, ya_ref, on_ref, x_ref, w_ref, xmid_ref, y_ref):
        g = g_ref[...].astype(F32)
        ya, on = ya_ref[...].astype(F32), on_ref[...].astype(F32)
        y = (_sigmoid(ma_ref[...].astype(F32)) * ya
             + _sigmoid(mb_ref[...].astype(F32)) * (on * (g * _sigmoid(g))))
        yb = y.astype(BF16)
        y_ref[...] = yb
        xmid_ref[...] = x_ref[...] + _dot(yb, w_ref[...])

    seg = lambda k: pl.BlockSpec((None, tm, d), lambda i, k=k: (k, i, 0))
    row = pl.BlockSpec((tm, d), lambda i: (i, 0))
    return pl.pallas_call(
        body, name="merge_out_fwd", grid=(t_rows // tm,),
        in_specs=[seg(0), seg(1), seg(2), row, row, row, pl.BlockSpec((d, d), lambda i: (0, 0))],
        out_specs=[row, row],
        out_shape=[jax.ShapeDtypeStruct((t_rows, d), F32), jax.ShapeDtypeStruct((t_rows, d), BF16)],
        compiler_params=_cp("parallel"))(gates, gates, gates, y_a, o_n, x, w_out)


def _mlp_fwd(x_mid, gain, w_up, w_down):
    t_rows, d = x_mid.shape
    f8 = w_up.shape[2]
    tm = _row_tile(t_rows, ROW_TILE_WEIGHT_STREAM)
    per_step = 2

    def body(x_ref, g_ref, wu_ref, wd_ref, out_ref, u_ref, h_ref):
        @pl.when(pl.program_id(1) == 0)
        def _():
            xv = x_ref[...]
            h_ref[...] = (xv * _rstd(xv) * g_ref[...]).astype(BF16)
            out_ref[...] = xv

        h = h_ref[...]
        down = None
        for k in range(per_step):
            u = _dot(h, wu_ref[k])
            u_ref[k] = u.astype(BF16)
            r = jnp.maximum(u, 0.0)
            part = _dot((r * r).astype(BF16), wd_ref[k])
            down = part if down is None else down + part
        out_ref[...] += down

    row = pl.BlockSpec((tm, d), lambda i, j: (i, 0))
    return pl.pallas_call(
        body, name="mlp_fwd", grid=(t_rows // tm, N_DEV // per_step),
        in_specs=[row, pl.BlockSpec((1, d), lambda i, j: (0, 0)),
                  pl.BlockSpec((per_step, d, f8), lambda i, j: (j, 0, 0)),
                  pl.BlockSpec((per_step, f8, d), lambda i, j: (j, 0, 0))],
        out_specs=[row, pl.BlockSpec((per_step, tm, f8), lambda i, j: (j, i, 0)), row],
        out_shape=[jax.ShapeDtypeStruct((t_rows, d), F32),
                   jax.ShapeDtypeStruct((N_DEV, t_rows, f8), BF16),
                   jax.ShapeDtypeStruct((t_rows, d), BF16)],
        compiler_params=_cp("parallel", "arbitrary"))(x_mid, gain, w_up, w_down)


def _loss_head(x, gain, target):
    t_rows, d = x.shape
    tm = _row_tile(t_rows, 512)

    def body(x_ref, g_ref, t_ref, loss_ref, dx_ref, dxb_ref, dg_ref):
        @pl.when(pl.program_id(0) == 0)
        def _():
            loss_ref[...] = jnp.zeros_like(loss_ref)
            dg_ref[...] = jnp.zeros_like(dg_ref)

        xv = x_ref[...]
        g = g_ref[...]
        err = xv * _rstd(xv) * g - t_ref[...]
        loss_ref[...] += (0.5 / d) * jnp.sum(err * err)
        dx, dg = _rms_bwd(err * (1.0 / d), xv, g)
        dx_ref[...] = dx
        dxb_ref[...] = dx.astype(BF16)
        dg_ref[...] += dg

    row = pl.BlockSpec((tm, d), lambda i: (i, 0))
    vec = pl.BlockSpec((1, d), lambda i: (0, 0))
    return pl.pallas_call(
        body, name="loss_head", grid=(t_rows // tm,),
        in_specs=[row, vec, row],
        out_specs=[pl.BlockSpec((SUBLANES, LANES), lambda i: (0, 0)), row, row, vec],
        out_shape=[jax.ShapeDtypeStruct((SUBLANES, LANES), F32),
                   jax.ShapeDtypeStruct((t_rows, d), F32),
                   jax.ShapeDtypeStruct((t_rows, d), BF16),
                   jax.ShapeDtypeStruct((1, d), F32)],
        compiler_params=_cp("arbitrary"))(x, gain, target)


def _mlp_bwd(d_out, d_out_b, u, x_mid, gain, w_up, w_down):
    t_rows, d = x_mid.shape
    f8 = w_up.shape[2]
    tm = _row_tile(t_rows, ROW_TILE_WEIGHT_STREAM)
    sub = _row_tile(tm, ROW_CHUNK)

    def body(do_ref, dob_ref, u_ref, x_ref, g_ref, wu_ref, wd_ref, dx_ref, dxb_ref, du_ref, act_ref, dg_ref):
        j = pl.program_id(1)

        @pl.when((pl.program_id(0) == 0) & (j == 0))
        def _():
            dg_ref[...] = jnp.zeros_like(dg_ref)

        @pl.when(j == 0)
        def _():
            dx_ref[...] = jnp.zeros_like(dx_ref)

        r = jnp.maximum(u_ref[...].astype(F32), 0.0)
        act_ref[...] = (r * r).astype(BF16)
        du = (_dot_nt(dob_ref[...], wd_ref[...]) * (2.0 * r)).astype(BF16)
        du_ref[...] = du
        dx_ref[...] += _dot_nt(du, wu_ref[...])

        @pl.when(j == N_DEV - 1)
        def _():
            def finish(c, _):
                rows = pl.ds(pl.multiple_of(c * sub, sub), sub)
                dx, dg = _rms_bwd(dx_ref[rows, :], x_ref[rows, :], g_ref[...])
                dx = dx + do_ref[rows, :]
                dx_ref[rows, :] = dx
                dxb_ref[rows, :] = dx.astype(BF16)
                dg_ref[...] += dg
                return 0

            lax.fori_loop(0, tm // sub, finish, 0)

    row = pl.BlockSpec((tm, d), lambda i, j: (i, 0))
    vec = pl.BlockSpec((1, d), lambda i, j: (0, 0))
    hid = pl.BlockSpec((None, tm, f8), lambda i, j: (j, i, 0))
    return pl.pallas_call(
        body, name="mlp_bwd", grid=(t_rows // tm, N_DEV),
        in_specs=[row, row, hid, row, vec,
                  pl.BlockSpec((None, d, f8), lambda i, j: (j, 0, 0)),
                  pl.BlockSpec((None, f8, d), lambda i, j: (j, 0, 0))],
        out_specs=[row, row, hid, hid, vec],
        out_shape=[jax.ShapeDtypeStruct((t_rows, d), F32),
                   jax.ShapeDtypeStruct((t_rows, d), BF16),
                   jax.ShapeDtypeStruct((N_DEV, t_rows, f8), BF16),
                   jax.ShapeDtypeStruct((N_DEV, t_rows, f8), BF16),
                   jax.ShapeDtypeStruct((1, d), F32)],
        compiler_params=_cp("arbitrary", "arbitrary"))(d_out, d_out_b, u, x_mid, gain, w_up, w_down)


def _outproj_bwd(dx_mid_b, w_out, gates, y_a, o_n):
    t_rows, d = y_a.shape
    tm = _row_tile(t_rows, 256)

    def body(dx_ref, w_ref, g_ref, ma_ref, mb_ref, ya_ref, on_ref, dya_ref, don_ref, dp_ref):
        dy = _dot_nt(dx_ref[...], w_ref[...])
        sa = _sigmoid(ma_ref[...].astype(F32))
        sb = _sigmoid(mb_ref[...].astype(F32))
        sg, dsg = _silu_and_grad(g_ref[...].astype(F32))
        ya = ya_ref[...].astype(F32)
        on = on_ref[...].astype(F32)
        dya_ref[...] = dy * sa
        t = dy * sb
        don_ref[...] = t * sg
        dp_ref[0] = (t * on * dsg).astype(BF16)
        dp_ref[1] = (dy * ya * sa * (1.0 - sa)).astype(BF16)
        dp_ref[2] = (dy * on * sg * sb * (1.0 - sb)).astype(BF16)

    seg = lambda k: pl.BlockSpec((None, tm, d), lambda i, k=k: (k, i, 0))
    row = pl.BlockSpec((tm, d), lambda i: (i, 0))
    return pl.pallas_call(
        body, name="outproj_bwd", grid=(t_rows // tm,),
        in_specs=[row, pl.BlockSpec((d, d), lambda i: (0, 0)), seg(0), seg(1), seg(2), row, row],
        out_specs=[row, row, pl.BlockSpec((3, tm, d), lambda i: (0, i, 0))],
        out_shape=[jax.ShapeDtypeStruct((t_rows, d), F32),
                   jax.ShapeDtypeStruct((t_rows, d), F32),
                   jax.ShapeDtypeStruct((3, t_rows, d), BF16)],
        compiler_params=_cp("parallel"))(dx_mid_b, w_out, gates, gates, gates, y_a, o_n)


def _inproj_bwd(dx_mid, dp_a, dp_b, dp_c, w_seg, x_in, gain, carry=None):
    t_rows, d = x_in.shape
    tm = _row_tile(t_rows, ROW_TILE_WEIGHT_STREAM)
    sub = _row_tile(tm, ROW_CHUNK)
    n_a, n_b = dp_a.shape[0], dp_b.shape[0]

    def body(dxm_ref, a_ref, b_ref, c_ref, w_ref, x_ref, g_ref, dx_ref, dxb_ref, dg_ref):
        j = pl.program_id(1)

        @pl.when((pl.program_id(0) == 0) & (j == 0))
        def _():
            dg_ref[...] = jnp.zeros_like(dg_ref)

        @pl.when(j == 0)
        def _():
            dx_ref[...] = jnp.zeros_like(dx_ref)

        @pl.when(j < n_a)
        def _():
            dx_ref[...] += _dot_nt(a_ref[...], w_ref[...])

        @pl.when((j >= n_a) & (j < n_a + n_b))
        def _():
            dx_ref[...] += _dot_nt(b_ref[...], w_ref[...])

        @pl.when(j >= n_a + n_b)
        def _():
            dx_ref[...] += _dot_nt(c_ref[...], w_ref[...])

        @pl.when(j == N_DEV - 1)
        def _():
            def finish(c, _):
                rows = pl.ds(pl.multiple_of(c * sub, sub), sub)
                dx, dg = _rms_bwd(dx_ref[rows, :], x_ref[rows, :], g_ref[...])
                dx = dx + dxm_ref[rows, :]
                dx_ref[rows, :] = dx
                dxb_ref[rows, :] = dx.astype(BF16)
                dg_ref[...] += dg
                return 0

            lax.fori_loop(0, tm // sub, finish, 0)

    def part(first, n):
        return pl.BlockSpec((None, tm, d), lambda i, j: (jnp.clip(j - first, 0, n - 1), i, 0))

    row = pl.BlockSpec((tm, d), lambda i, j: (i, 0))
    vec = pl.BlockSpec((1, d), lambda i, j: (0, 0))
    return _call_carrying(
        body, carry, name="inproj_bwd", grid=(t_rows // tm, N_DEV),
        in_specs=[row, part(0, n_a), part(n_a, n_b), part(n_a + n_b, dp_c.shape[0]),
                  pl.BlockSpec((None, d, d), lambda i, j: (j, 0, 0)), row, vec],
        out_specs=[row, row, vec],
        out_shape=[jax.ShapeDtypeStruct((t_rows, d), F32),
                   jax.ShapeDtypeStruct((t_rows, d), BF16),
                   jax.ShapeDtypeStruct((1, d), F32)],
        scratch_shapes=[], semantics=("arbitrary", "arbitrary"),
        args=(dx_mid, dp_a, dp_b, dp_c, w_seg, x_in, gain))


def _wgrad(a3, b3, name):
    n_a, t_rows, k_a = a3.shape
    n_b, _, n_cols = b3.shape
    n = max(n_a, n_b)
    bk = _row_tile(k_a, 1024)
    bn = n_cols if n_cols <= 1024 else 1024
    tt = _row_tile(t_rows, WGRAD_TOKEN_TILE)
    n_t = t_rows // tt

    def body(a_ref, b_ref, o_ref, acc_ref):
        t, j = pl.program_id(2), pl.program_id(3)
        part = _dot_tn(a_ref[...], b_ref[...])

        @pl.when(t == 0)
        def _():
            acc_ref[j] = part

        @pl.when(t > 0)
        def _():
            acc_ref[j] += part

        @pl.when(t == n_t - 1)
        def _():
            o_ref[...] = acc_ref[j].astype(BF16)

    def out_map(p, q, t, j):
        return (jnp.where(t == n_t - 1, j, 0), p, q)

    return pl.pallas_call(
        body, name=name, grid=(k_a // bk, n_cols // bn, n_t, n),
        in_specs=[pl.BlockSpec((None, tt, bk), lambda p, q, t, j: (j if n_a > 1 else 0, t, p)),
                  pl.BlockSpec((None, tt, bn), lambda p, q, t, j: (j if n_b > 1 else 0, t, q))],
        out_specs=pl.BlockSpec((None, bk, bn), out_map),
        out_shape=jax.ShapeDtypeStruct((n, k_a, n_cols), BF16),
        scratch_shapes=[pltpu.VMEM((n, bk, bn), F32)],
        compiler_params=_cp("parallel", "parallel", "arbitrary", "arbitrary"))(a3, b3)


def _conv_taps(xe, n):
    return [_shift_rows(xe, CONV_W - 1 - j)[SUBLANES:SUBLANES + n, :] for j in range(CONV_W)]


def _rg_gates(xc, w_r, b_r, w_i, b_i, sp8):
    xb = xc.astype(BF16)
    r = _sigmoid(_dot(xb, w_r) + b_r)
    i = _sigmoid(_dot(xb, w_i) + b_i)
    return r, i


def _mixer_a_fwd(proj, conv_w, conv_b, w_r, b_r, w_i, b_i, lam, seq):
    _, t_rows, d = proj.shape
    n_seq, n_blk = t_rows // seq, d // RG_BLOCK_W
    wb = RG_BLOCK_W
    ch = _row_tile(seq, ROW_CHUNK)

    def body(xa_ref, ga_ref, cw_ref, cb_ref, wr_ref, br_ref, wi_ref, bi_ref, lam_ref, h_ref, ya_ref,
             xpad, a_s, u_s):
        xpad[0:SUBLANES, :] = jnp.zeros((SUBLANES, wb), F32)
        xpad[SUBLANES:, :] = xa_ref[...]
        sp8 = RG_C * _softplus(-lam_ref[...])

        def gates(c, _):
            r0 = pl.multiple_of(c * ch, ch)
            taps = _conv_taps(xpad[pl.ds(r0, ch + SUBLANES), :], ch)
            xc = cb_ref[...] + sum(cw_ref[pl.ds(j, 1), :] * taps[j] for j in range(CONV_W))
            r, i = _rg_gates(xc, wr_ref[...], br_ref[...], wi_ref[...], bi_ref[...], sp8)
            log_a = -(r * sp8)
            a_s[pl.ds(r0, ch), :] = jnp.exp(log_a)
            u_s[pl.ds(r0, ch), :] = jnp.sqrt(jnp.maximum(_one_minus_exp(2.0 * log_a), 0.0)) * (i * xc)
            return 0

        lax.fori_loop(0, seq // ch, gates, 0)
        _scan_rows(a_s, u_s, h_ref, seq, wb, reverse=False)

        def gate_out(c, _):
            r0 = pl.multiple_of(c * ch, ch)
            gl, _ = _gelu_and_grad(ga_ref[pl.ds(r0, ch), :])
            ya_ref[pl.ds(r0, ch), :] = (h_ref[pl.ds(r0, ch), :] * gl).astype(BF16)
            return 0

        lax.fori_loop(0, seq // ch, gate_out, 0)

    seg = lambda k: pl.BlockSpec((None, seq, wb), lambda s, b, k=k: (k, s, b))
    blk = pl.BlockSpec((seq, wb), lambda s, b: (s, b))
    vec = pl.BlockSpec((1, wb), lambda s, b: (0, b))
    wsp = pl.BlockSpec((None, wb, wb), lambda s, b: (b, 0, 0))
    return pl.pallas_call(
        body, name="mixer_a_fwd", grid=(n_seq, n_blk),
        in_specs=[seg(0), seg(1), pl.BlockSpec((CONV_W, wb), lambda s, b: (0, b)), vec, wsp, vec, wsp, vec, vec],
        out_specs=[blk, blk],
        out_shape=[jax.ShapeDtypeStruct((t_rows, d), F32), jax.ShapeDtypeStruct((t_rows, d), BF16)],
        scratch_shapes=[pltpu.VMEM((seq + SUBLANES, wb), F32), pltpu.VMEM((seq, wb), F32),
                        pltpu.VMEM((seq, wb), F32)],
        compiler_params=_cp("parallel", "parallel"))(proj, proj, conv_w, conv_b, w_r, b_r, w_i, b_i, lam)


def _mixer_a_bwd(proj, h, d_ya, conv_w, conv_b, w_r, b_r, w_i, b_i, lam, seq):
    _, t_rows, d = proj.shape
    n_seq, n_blk = t_rows // seq, d // RG_BLOCK_W
    wb = RG_BLOCK_W
    ch = _row_tile(seq, ROW_CHUNK)
    n_ch = seq // ch

    def body(xa_ref, ga_ref, h_ref, dya_ref, cw_ref, cb_ref, wr_ref, br_ref, wi_ref, bi_ref, lam_ref,
             dp_ref, dwr_ref, dwi_ref, dbr_ref, dbi_ref, dlam_ref, dcw_ref, dcb_ref,
             xpad, hpad, a_s, e_pad, g_s, xc_s, r_s, i_s, dxc_pad):
        @pl.when(pl.program_id(1) == 0)
        def _():
            for ref in (dwr_ref, dwi_ref, dbr_ref, dbi_ref, dlam_ref, dcw_ref, dcb_ref):
                ref[...] = jnp.zeros_like(ref)

        zeros8 = jnp.zeros((SUBLANES, wb), F32)
        xpad[0:SUBLANES, :] = zeros8
        xpad[SUBLANES:, :] = xa_ref[...]
        hpad[0:SUBLANES, :] = zeros8
        hpad[SUBLANES:, :] = h_ref[...]
        e_pad[seq:, :] = zeros8
        dxc_pad[seq:, :] = zeros8
        lam_v = lam_ref[...]
        sp8 = RG_C * _softplus(-lam_v)

        def recompute(c, _):
            r0 = pl.multiple_of(c * ch, ch)
            rows = pl.ds(r0, ch)
            taps = _conv_taps(xpad[pl.ds(r0, ch + SUBLANES), :], ch)
            xc = cb_ref[...] + sum(cw_ref[pl.ds(j, 1), :] * taps[j] for j in range(CONV_W))
            r, i = _rg_gates(xc, wr_ref[...], br_ref[...], wi_ref[...], bi_ref[...], sp8)
            a = jnp.exp(-(r * sp8))
            gl, dgl = _gelu_and_grad(ga_ref[rows, :])
            dya = dya_ref[rows, :]
            g = dya * gl
            dp_ref[1, rows, :] = (dya * h_ref[rows, :] * dgl).astype(BF16)
            a_s[rows, :] = a
            e_pad[rows, :] = a * g
            g_s[rows, :] = g
            xc_s[rows, :] = xc
            r_s[rows, :] = r
            i_s[rows, :] = i
            return 0

        lax.fori_loop(0, n_ch, recompute, 0)
        _scan_rows(a_s, e_pad, e_pad, seq, wb, reverse=True)

        def grads(c, _):
            r0 = pl.multiple_of(c * ch, ch)
            rows = pl.ds(r0, ch)
            halo = pl.ds(r0, ch + SUBLANES)
            dh = g_s[rows, :] + _shift_rows(e_pad[halo, :], ch + SUBLANES - 1)[0:ch, :]
            h_prev = _shift_rows(hpad[halo, :], 1)[SUBLANES:, :]
            xc, r, i = xc_s[rows, :], r_s[rows, :], i_s[rows, :]
            log_a = -(r * sp8)
            a = jnp.exp(log_a)
            om = _one_minus_exp(2.0 * log_a)
            sq = jnp.sqrt(jnp.maximum(om, 0.0))
            t1 = dh * xc
            d_i = t1 * sq
            d_la = dh * h_prev * a + jnp.where(om > 0.0, -(t1 * i) * (1.0 - om) / sq, 0.0)
            dpr = -(d_la * sp8) * r * (1.0 - r)
            dpi = d_i * i * (1.0 - i)
            dprb, dpib, xb = dpr.astype(BF16), dpi.astype(BF16), xc.astype(BF16)
            dxc = dh * sq * i + _dot_nt(dprb, wr_ref[...]) + _dot_nt(dpib, wi_ref[...])
            dwr_ref[...] += _dot_tn(xb, dprb)
            dwi_ref[...] += _dot_tn(xb, dpib)
            dbr_ref[...] += jnp.sum(dpr, axis=0, keepdims=True)
            dbi_ref[...] += jnp.sum(dpi, axis=0, keepdims=True)
            dlam_ref[...] += jnp.sum(d_la * r, axis=0, keepdims=True) * (RG_C * _sigmoid(-lam_v))
            dcb_ref[...] += jnp.sum(dxc, axis=0, keepdims=True)
            taps = _conv_taps(xpad[halo, :], ch)
            for j in range(CONV_W):
                dcw_ref[pl.ds(j, 1), :] += jnp.sum(dxc * taps[j], axis=0, keepdims=True)
            dxc_pad[rows, :] = dxc
            return 0

        lax.fori_loop(0, n_ch, grads, 0)

        def conv_bwd(c, _):
            r0 = pl.multiple_of(c * ch, ch)
            de = dxc_pad[pl.ds(r0, ch + SUBLANES), :]
            dxa = sum(cw_ref[pl.ds(j, 1), :] * _shift_rows(de, ch + SUBLANES - (CONV_W - 1 - j))[0:ch, :]
                      for j in range(CONV_W))
            dp_ref[0, pl.ds(r0, ch), :] = dxa.astype(BF16)
            return 0

        lax.fori_loop(0, n_ch, conv_bwd, 0)

    seg = lambda k: pl.BlockSpec((None, seq, wb), lambda b, s, k=k: (k, s, b))
    blk = pl.BlockSpec((seq, wb), lambda b, s: (s, b))
    vec = pl.BlockSpec((1, wb), lambda b, s: (0, b))
    taps = pl.BlockSpec((CONV_W, wb), lambda b, s: (0, b))
    wsp = pl.BlockSpec((None, wb, wb), lambda b, s: (b, 0, 0))
    vec_shape = jax.ShapeDtypeStruct((1, d), F32)
    w_shape = jax.ShapeDtypeStruct((n_blk, wb, wb), F32)
    pad = pltpu.VMEM((seq + SUBLANES, wb), F32)
    full = pltpu.VMEM((seq, wb), F32)
    return pl.pallas_call(
        body, name="mixer_a_bwd", grid=(n_blk, n_seq),
        in_specs=[seg(0), seg(1), blk, blk, taps, vec, wsp, vec, wsp, vec, vec],
        out_specs=[pl.BlockSpec((2, seq, wb), lambda b, s: (0, s, b)), wsp, wsp, vec, vec, vec, taps, vec],
        out_shape=[jax.ShapeDtypeStruct((2, t_rows, d), BF16), w_shape, w_shape, vec_shape, vec_shape,
                   vec_shape, jax.ShapeDtypeStruct((CONV_W, d), F32), vec_shape],
        scratch_shapes=[pad, pad, full, pad, full, full, full, full, pad],
        compiler_params=_cp("parallel", "arbitrary"))(
            proj, proj, h, d_ya, conv_w, conv_b, w_r, b_r, w_i, b_i, lam)


def _hg_prepare(q_ref, z_ref, lb, rows):
    z = z_ref[rows, :]
    sig, nsig = _sigmoid_pair(z)
    fg = lb + (1.0 - lb) * sig
    log_f = jnp.log(jnp.maximum(fg, F_MIN))
    key = (1.0 - lb) * nsig
    qs, _ = _silu_and_grad(q_ref[rows, :])
    return qs, key, log_f, sig, fg


HG_UNROLL_TERMS = 64
HG_UNROLL_FWD = 32
HG_UNROLL_BWD = 32
HG_HALF = HG_CHUNK // 2
HG_STACK = HG_CHUNK * HG_HALF
assert HG_HALF == SUBLANES


def _half_of(x, s):
    return x[:HG_HALF, :] if s < HG_HALF else x[HG_HALF:, :]


def _hg_decay(g_ref, r0, g_rows, first_row, s):
    rid = lax.broadcasted_iota(jnp.int32, g_rows.shape, 0) + first_row
    gs = g_ref[pl.ds(r0 + s, 1), :]
    return jnp.where(rid >= s, jnp.exp(g_rows - gs), 0.0)


def _half_start(s):
    return 0 if s < HG_HALF else HG_HALF


def _hg_cross_decays(gc):
    rid = lax.broadcasted_iota(jnp.int32, (HG_CHUNK, HG_DK), 0)
    g_mid = gc[HG_HALF - 1:HG_HALF, :]
    e_hi = jnp.where(rid >= HG_HALF, jnp.exp(gc - g_mid), 0.0)
    e_lo = jnp.where(rid < HG_HALF, jnp.exp(g_mid - gc), 0.0)
    return e_hi, e_lo


def _hg_cross(qc, kc, gc):
    e_hi, e_lo = _hg_cross_decays(gc)
    return (qc * e_hi).astype(BF16), (kc * e_lo).astype(BF16)


def _stack(slabs):
    return jnp.concatenate(slabs, axis=0).astype(BF16)


def _slab_row_sums():
    row = lax.broadcasted_iota(jnp.int32, (HG_CHUNK, HG_STACK), 0)
    col = lax.broadcasted_iota(jnp.int32, (HG_CHUNK, HG_STACK), 1)
    lo = row * HG_HALF
    return jnp.where((col >= lo) & (col < lo + HG_HALF), 1.0, 0.0).astype(BF16)


def _for_chunks(n, unroll, *stages, after_trip=None):
    unroll = min(unroll, n)
    assert n % unroll == 0

    def trip(i, _):
        chunks = [i * unroll + u for u in range(unroll)]
        carried = [stages[0](c) for c in chunks]
        for stage in stages[1:]:
            carried = [stage(c, x) for c, x in zip(chunks, carried)]
        if after_trip is not None:
            after_trip(carried)
        return 0

    lax.fori_loop(0, n // unroll, trip, 0)


def _hg_state_chain(states, g_ref, carry_ref, n_chunks, reverse):
    unroll = min(8, n_chunks)
    assert n_chunks % unroll == 0
    carry_ref[...] = jnp.zeros_like(carry_ref)

    def trip(i, _):
        st = carry_ref[...]
        for u in range(unroll):
            k = i * unroll + u
            c = n_chunks - 1 - k if reverse else k
            term = states[c]
            states[c] = st
            st = st * jnp.exp(g_ref[pl.ds(c * HG_CHUNK + HG_CHUNK - 1, 1), :]) + term
        carry_ref[...] = st
        return 0

    lax.fori_loop(0, n_chunks // unroll, trip, 0)


def _hgrn_fwd(proj, lower_bound, hg_gain, seq, carry=None):
    _, t_rows, d = proj.shape
    n_seq, n_head = t_rows // seq, d // HG_DK
    ch = _row_tile(seq, ROW_CHUNK)
    n_chunks = seq // HG_CHUNK

    def body(q_ref, z_ref, v_ref, lb_ref, gain_ref, o_ref, on_ref, qs_s, k_s, g_s, states, st_ref):
        lb = lb_ref[...]
        tri = _group_cumsum_matrix(ch, HG_CHUNK)

        per_block = ch // HG_CHUNK

        def prepare(c):
            rows = pl.ds(pl.multiple_of(c * ch, ch), ch)
            qs, key, log_f, _, _ = _hg_prepare(q_ref, z_ref, lb, rows)
            qs_s[rows, :] = qs
            k_s[rows, :] = key
            return key, _group_cumsum_mxu(log_f, tri)

        def state_terms(c, carried):
            key, g = carried
            r0 = pl.multiple_of(c * ch, ch)
            g_s[pl.ds(r0, ch), :] = g
            terms = []
            for u in range(per_block):
                sl = slice(u * HG_CHUNK, (u + 1) * HG_CHUNK)
                k_end = key[sl, :] * jnp.exp(g[(u + 1) * HG_CHUNK - 1:(u + 1) * HG_CHUNK, :] - g[sl, :])
                vc = v_ref[pl.ds(r0 + u * HG_CHUNK, HG_CHUNK), :]
                terms.append(_dot_tn(vc.astype(BF16), k_end.astype(BF16)))
            return terms

        def store_terms(c, terms):
            for u, term in enumerate(terms):
                states[c * per_block + u] = term

        _for_chunks(seq // ch, 2, prepare, state_terms, store_terms)
        _hg_state_chain(states, g_s, st_ref, n_chunks, reverse=False)
        ones = jnp.ones((HG_DK, HG_DK), BF16)

        def issue(c):
            r0 = pl.multiple_of(c * HG_CHUNK, HG_CHUNK)
            rows = pl.ds(r0, HG_CHUNK)
            qc, gc = qs_s[rows, :], g_s[rows, :]
            o = _dot_nt((qc * jnp.exp(gc)).astype(BF16), states[c].astype(BF16))
            pairs = [qc[_half_start(s):, :] * _hg_decay(g_s, r0, gc[_half_start(s):, :], _half_start(s), s)
                     * k_s[pl.ds(r0 + s, 1), :] for s in range(HG_CHUNK)]
            score = _dot(_stack(pairs), ones)
            return o, score

        def combine(c, issued):
            o, score = issued
            r0 = pl.multiple_of(c * HG_CHUNK, HG_CHUNK)
            halves = [o[:HG_HALF, :], o[HG_HALF:, :]]
            first = 0
            for s in range(HG_CHUNK):
                vs = v_ref[pl.ds(r0 + s, 1), :]
                if s < HG_HALF:
                    halves[0] += score[first:first + HG_HALF, :] * vs
                    first += HG_HALF
                halves[1] += score[first:first + HG_HALF, :] * vs
                first += HG_HALF
            o = jnp.concatenate(halves, axis=0)
            o_ref[pl.ds(r0, HG_CHUNK), :] = o
            on_ref[pl.ds(r0, HG_CHUNK), :] = (o * _rstd(o) * gain_ref[...]).astype(BF16)

        _for_chunks(n_chunks, HG_UNROLL_FWD, issue, combine)

    seg = lambda k: pl.BlockSpec((None, seq, HG_DK), lambda s, h, k=k: (k, s, h))
    blk = pl.BlockSpec((seq, HG_DK), lambda s, h: (s, h))
    full = pltpu.VMEM((seq, HG_DK), F32)
    return _call_carrying(
        body, carry, name="hgrn_fwd", grid=(n_seq, n_head),
        in_specs=[seg(2), seg(3), seg(4), pl.BlockSpec((1, HG_DK), lambda s, h: (0, h)),
                  pl.BlockSpec((1, HG_DK), lambda s, h: (0, 0))],
        out_specs=[blk, blk],
        out_shape=[jax.ShapeDtypeStruct((t_rows, d), F32), jax.ShapeDtypeStruct((t_rows, d), BF16)],
        scratch_shapes=[full, full, full, pltpu.VMEM((n_chunks, HG_DK, HG_DK), F32),
                        pltpu.VMEM((HG_DK, HG_DK), F32)],
        semantics=("parallel", "parallel"), args=(proj, proj, proj, lower_bound, hg_gain))


def _hgrn_bwd(proj, lower_bound, hg_gain, o, d_on, seq, carry=None):
    _, t_rows, d = proj.shape
    n_seq, n_head = t_rows // seq, d // HG_DK
    ch = _row_tile(seq, ROW_CHUNK)
    n_chunks = seq // HG_CHUNK
    cc = HG_CHUNK

    def body(q_ref, z_ref, v_ref, lb_ref, gain_ref, o_ref, don_ref, dp_ref, dlb_ref, dgain_ref,
             qs_s, k_s, g_s, do_s, dlb_acc, states, dstates, carry_ref):
        hh, ss = pl.program_id(0), pl.program_id(1)
        lb = lb_ref[...]

        @pl.when(ss == 0)
        def _():
            dlb_ref[...] = jnp.zeros_like(dlb_ref)

        @pl.when((ss == 0) & (hh == 0))
        def _():
            dgain_ref[...] = jnp.zeros_like(dgain_ref)

        tri = _group_cumsum_matrix(ch, cc)

        def prepare(c):
            rows = pl.ds(pl.multiple_of(c * ch, ch), ch)
            qs, key, log_f, _, _ = _hg_prepare(q_ref, z_ref, lb, rows)
            qs_s[rows, :] = qs
            k_s[rows, :] = key
            do, dgain = _rms_bwd(don_ref[rows, :], o_ref[rows, :], gain_ref[...])
            do_s[rows, :] = do
            dgain_ref[...] += dgain
            return _group_cumsum_mxu(log_f, tri)

        def store_cumsum(c, g):
            g_s[pl.ds(pl.multiple_of(c * ch, ch), ch), :] = g

        _for_chunks(seq // ch, 4, prepare, store_cumsum)

        def chain_terms(c):
            rows = pl.ds(pl.multiple_of(c * cc, cc), cc)
            gc = g_s[rows, :]
            k_end = k_s[rows, :] * jnp.exp(gc[cc - 1:cc, :] - gc)
            q_in = qs_s[rows, :] * jnp.exp(gc)
            return (_dot_tn(v_ref[rows, :].astype(BF16), k_end.astype(BF16)),
                    _dot_tn(do_s[rows, :].astype(BF16), q_in.astype(BF16)))

        def store_terms(c, terms):
            states[c], dstates[c] = terms

        _for_chunks(n_chunks, HG_UNROLL_TERMS, chain_terms, store_terms)
        _hg_state_chain(states, g_s, carry_ref, n_chunks, reverse=False)
        _hg_state_chain(dstates, g_s, carry_ref, n_chunks, reverse=True)
        ones = jnp.ones((HG_DK, HG_DK), BF16)
        row_sums = _slab_row_sums()

        def chunk_rows(c):
            r0 = pl.multiple_of(c * cc, cc)
            return r0, pl.ds(r0, cc)

        def through_state(c):
            r0, rows = chunk_rows(c)
            qc, kc, gc, vc, doc = qs_s[rows, :], k_s[rows, :], g_s[rows, :], v_ref[rows, :], do_s[rows, :]
            st, dst = states[c], dstates[c]
            g_last = gc[cc - 1:cc, :]
            e_last, e_end = jnp.exp(g_last), jnp.exp(g_last - gc)
            dob, vcb, dstb = doc.astype(BF16), vc.astype(BF16), dst.astype(BF16)
            dqs = _dot(dob, st.astype(BF16))
            dk_state = _dot(vcb, dstb)
            dv = _dot_nt((kc * e_end).astype(BF16), dstb)
            cots = [_half_of(doc, s) * v_ref[pl.ds(r0 + s, 1), :] for s in range(cc)]
            d_score = _dot(_stack(cots), ones)
            x, y = _hg_cross(qc, kc, gc)
            cross = (_dot_nt(dob, vcb), _dot_nt(vcb, dob), _dot_nt(y, x))
            return dqs, dk_state, dv, d_score, e_last * jnp.sum(dst * st, axis=0, keepdims=True), cross

        def pair_terms(c, carried):
            dqs, dk_state, dv, d_score, d_glast, (da_cross, da_cross_t, a_cross_t) = carried
            r0, rows = chunk_rows(c)
            qc, kc, gc = qs_s[rows, :], k_s[rows, :], g_s[rows, :]
            dqs = dqs * jnp.exp(gc)
            dk_state = dk_state * jnp.exp(gc[cc - 1:cc, :] - gc)
            d_glast = d_glast + jnp.sum(kc * dk_state, axis=0, keepdims=True)
            dqs_half = [dqs[:HG_HALF, :], dqs[HG_HALF:, :]]
            pairs, dk_terms = [], []
            for s in range(cc):
                qv = _half_of(qc, s)
                decay = _hg_decay(g_s, r0, _half_of(gc, s), _half_start(s), s)
                ks = k_s[pl.ds(r0 + s, 1), :]
                da_decay = d_score[s * HG_HALF:(s + 1) * HG_HALF, :] * decay
                pairs.append(qv * decay * ks)
                dk_terms.append(da_decay * qv)
                dqs_half[s // HG_HALF] += da_decay * ks
            score = _dot(_stack(pairs), ones)
            dk = dk_state + _dot(row_sums, _stack(dk_terms))
            x, y = _hg_cross(qc, kc, gc)
            cross = (_dot(da_cross.astype(BF16), y), _dot(da_cross_t.astype(BF16), x),
                     _dot(a_cross_t.astype(BF16), do_s[rows, :].astype(BF16)))
            return jnp.concatenate(dqs_half, axis=0), dk, dv, score, d_glast, cross

        def value_terms(c, carried):
            dqs, dk, dv, score, d_glast, (dx_cross, dy_cross, dv_cross) = carried
            _, rows = chunk_rows(c)
            doc, gc = do_s[rows, :], g_s[rows, :]
            dv_terms = [score[s * HG_HALF:(s + 1) * HG_HALF, :] * _half_of(doc, s) for s in range(cc)]
            e_hi, e_lo = _hg_cross_decays(gc)
            return (dqs + dx_cross * e_hi, dk + dy_cross * e_lo,
                    dv + dv_cross + _dot(row_sums, _stack(dv_terms)), d_glast)

        dlb_acc[...] = jnp.zeros_like(dlb_acc)

        def store(c, x):
            dqs, dk, dv, d_glast = x
            _, rows = chunk_rows(c)
            d_g = qs_s[rows, :] * dqs - k_s[rows, :] * dk
            d_logf = _seg_cumsum(d_g, cc, reverse=True) + d_glast
            sig, nsig = _sigmoid_pair(z_ref[rows, :])
            fg = lb + (1.0 - lb) * sig
            _, dsilu = _silu_and_grad(q_ref[rows, :])
            d_gate = jnp.where(fg > F_MIN, d_logf / fg, 0.0) - dk
            dp_ref[0, rows, :] = (dqs * dsilu).astype(BF16)
            dp_ref[1, rows, :] = (d_gate * (1.0 - lb) * sig * nsig).astype(BF16)
            dp_ref[2, rows, :] = dv.astype(BF16)
            return d_gate * nsig

        def add_lower_bound_grads(per_chunk):
            while len(per_chunk) > 1:
                per_chunk = [a + b for a, b in zip(per_chunk[::2], per_chunk[1::2])]
            dlb_acc[...] += per_chunk[0]

        _for_chunks(n_chunks, HG_UNROLL_BWD, through_state, pair_terms, value_terms, store,
                    after_trip=add_lower_bound_grads)
        dlb_ref[...] += jnp.sum(dlb_acc[...], axis=0, keepdims=True)

    seg = lambda k: pl.BlockSpec((None, seq, HG_DK), lambda h, s, k=k: (k, s, h))
    blk = pl.BlockSpec((seq, HG_DK), lambda h, s: (s, h))
    full = pltpu.VMEM((seq, HG_DK), F32)
    return _call_carrying(
        body, carry, name="hgrn_bwd", grid=(n_head, n_seq),
        in_specs=[seg(2), seg(3), seg(4), pl.BlockSpec((1, HG_DK), lambda h, s: (0, h)),
                  pl.BlockSpec((1, HG_DK), lambda h, s: (0, 0)), blk, blk],
        out_specs=[pl.BlockSpec((3, seq, HG_DK), lambda h, s: (0, s, h)),
                   pl.BlockSpec((1, HG_DK), lambda h, s: (0, h)),
                   pl.BlockSpec((1, HG_DK), lambda h, s: (0, 0))],
        out_shape=[jax.ShapeDtypeStruct((3, t_rows, d), BF16), jax.ShapeDtypeStruct((1, d), F32),
                   jax.ShapeDtypeStruct((1, HG_DK), F32)],
        scratch_shapes=[full, full, full, full, pltpu.VMEM((cc, HG_DK), F32),
                        pltpu.VMEM((n_chunks, HG_DK, HG_DK), F32), pltpu.VMEM((n_chunks, HG_DK, HG_DK), F32),
                        pltpu.VMEM((HG_DK, HG_DK), F32)],
        semantics=("arbitrary", "arbitrary"), args=(proj, proj, proj, lower_bound, hg_gain, o, d_on))


def _mesh_place():
    x, y, c = lax.axis_index("x"), lax.axis_index("y"), lax.axis_index("c")
    return x, y, c


def _peer(place, k):
    x, y, c = place
    px = 1 - x if k & 4 else x
    py = 1 - y if k & 2 else y
    pc = 1 - c if k & 1 else c
    return (px, py, pc), 4 * px + 2 * py + pc


class _Exchange:
    def __init__(self, srcs, gather):
        self.n = len(srcs)
        self.gather = gather
        self.out_shape = [jax.ShapeDtypeStruct((N_DEV,) + tuple(s.shape if gather else s.shape[1:]), s.dtype)
                          for s in srcs]
        self.scratch = [pltpu.SemaphoreType.DMA((self.n * (N_DEV - 1),)),
                        pltpu.SemaphoreType.DMA((self.n * (N_DEV - 1),)),
                        pltpu.SemaphoreType.DMA((self.n,))]

    def _copies(self, src_refs, out_refs, sems):
        send_sems, recv_sems, local_sems = sems
        place = _mesh_place()
        me = 4 * place[0] + 2 * place[1] + place[2]
        local, sends, recvs = [], [], []
        for a, (src, out) in enumerate(zip(src_refs, out_refs)):
            outgoing = (lambda idx, src=src: src) if self.gather else (lambda idx, src=src: src.at[idx])
            local.append(pltpu.make_async_copy(outgoing(me), out.at[me], local_sems.at[a]))
            for k in range(1, N_DEV):
                peer, peer_idx = _peer(place, k)
                sem = a * (N_DEV - 1) + k - 1
                sends.append(pltpu.make_async_remote_copy(
                    src_ref=outgoing(peer_idx), dst_ref=out.at[me], send_sem=send_sems.at[sem],
                    recv_sem=recv_sems.at[sem], device_id=peer, device_id_type=MESH_ID))
                recvs.append(pltpu.make_async_remote_copy(
                    src_ref=outgoing(peer_idx), dst_ref=out.at[peer_idx], send_sem=send_sems.at[sem],
                    recv_sem=recv_sems.at[sem], device_id=peer, device_id_type=MESH_ID))
        return local, sends, recvs

    def start(self, src_refs, out_refs, sems):
        local, sends, _ = self._copies(src_refs, out_refs, sems)
        for cp in local + sends:
            cp.start()

    def wait(self, src_refs, out_refs, sems):
        local, sends, recvs = self._copies(src_refs, out_refs, sems)
        for cp in recvs:
            cp.wait_recv()
        for cp in sends:
            cp.wait_send()
        for cp in local:
            cp.wait()


def _call_carrying(body, carry, *, name, grid, in_specs, out_specs, out_shape, scratch_shapes, semantics, args):
    if carry is None:
        outs = pl.pallas_call(body, name=name, grid=grid, in_specs=in_specs, out_specs=out_specs,
                              out_shape=out_shape, scratch_shapes=scratch_shapes,
                              compiler_params=_cp(*semantics))(*args)
        return outs, []
    srcs, gather = carry
    ex = _Exchange(srcs, gather)
    n, n_in, n_out, n_scr = ex.n, len(in_specs), len(out_specs), len(scratch_shapes)

    def wrapped(*refs):
        ins, refs = refs[:n_in], refs[n_in:]
        src_refs, refs = refs[:n], refs[n:]
        outs, refs = refs[:n_out], refs[n_out:]
        dst_refs, refs = refs[:n], refs[n:]
        scratch, sems = refs[:n_scr], refs[n_scr:]
        first, last = None, None
        for axis, size in enumerate(grid):
            i = pl.program_id(axis)
            first = (i == 0) if first is None else first & (i == 0)
            last = (i == size - 1) if last is None else last & (i == size - 1)

        @pl.when(first)
        def _():
            ex.start(src_refs, dst_refs, sems)

        body(*ins, *outs, *scratch)

        @pl.when(last)
        def _():
            ex.wait(src_refs, dst_refs, sems)

    any_space = pl.BlockSpec(memory_space=pl.ANY)
    res = pl.pallas_call(
        wrapped, name=name + "_carrying", grid=grid, in_specs=list(in_specs) + [any_space] * n,
        out_specs=list(out_specs) + [any_space] * n, out_shape=list(out_shape) + ex.out_shape,
        scratch_shapes=list(scratch_shapes) + ex.scratch,
        compiler_params=_cp(*(["arbitrary"] * len(grid))))(*args, *srcs)
    return res[:n_out], res[n_out:]


def _exchange(srcs, name, gather):
    ex = _Exchange(srcs, gather)
    n = ex.n

    def body(*refs):
        src_refs, out_refs, sems = refs[:n], refs[n:2 * n], refs[2 * n:]
        ex.start(src_refs, out_refs, sems)
        ex.wait(src_refs, out_refs, sems)

    any_space = pl.BlockSpec(memory_space=pl.ANY)
    return pl.pallas_call(
        body, name=name, in_specs=[any_space] * n, out_specs=[any_space] * n,
        out_shape=ex.out_shape, scratch_shapes=ex.scratch)(*srcs)


def _gather_two_level(src, name):
    def body(x_ref, out_ref, send_sems, recv_sems, local_sem):
        x, y, c = _mesh_place()
        me, sibling = (x, y, c), (x, y, 1 - c)
        chips = [(1 - x, y), (x, 1 - y), (1 - x, 1 - y)]

        def slot(px, py, pc):
            return out_ref.at[4 * px + 2 * py + pc]

        def copy(k, block, to, source=None):
            return pltpu.make_async_remote_copy(
                src_ref=slot(*block) if source is None else source, dst_ref=slot(*block),
                send_sem=send_sems.at[k], recv_sem=recv_sems.at[k], device_id=to, device_id_type=MESH_ID)

        mine = pltpu.make_async_copy(x_ref, slot(*me), local_sem)
        mine.start()
        first = [copy(0, me, sibling, source=x_ref)]
        first += [copy(1 + j, me, (*chip, c), source=x_ref) for j, chip in enumerate(chips)]
        for cp in first:
            cp.start()
        passed = [copy(4 + j, (*chip, c), sibling) for j, chip in enumerate(chips)]
        for j, chip in enumerate(chips):
            copy(1 + j, (*chip, c), me).wait_recv()
            passed[j].start()
        copy(0, sibling, me).wait_recv()
        for j, chip in enumerate(chips):
            copy(4 + j, (*chip, 1 - c), me).wait_recv()
        for cp in first + passed:
            cp.wait_send()
        mine.wait()

    any_space = pl.BlockSpec(memory_space=pl.ANY)
    n_copies = N_DEV - 1
    return pl.pallas_call(
        body, name=name, in_specs=[any_space], out_specs=any_space,
        out_shape=jax.ShapeDtypeStruct((N_DEV,) + tuple(src.shape), src.dtype),
        scratch_shapes=[pltpu.SemaphoreType.DMA((n_copies,)), pltpu.SemaphoreType.DMA((n_copies,)),
                        pltpu.SemaphoreType.DMA])(src)


def _reduce_adamw(parts, w, m, v, name):
    rows, cols = w.shape
    n_seg = len(parts)
    seg_rows = rows // n_seg
    tr = _row_tile(seg_rows, 128)
    per_seg = seg_rows // tr
    c1 = np.float32(1.0 - ADAM_B1 ** ADAM_STEP)
    c2 = np.float32(1.0 - ADAM_B2 ** ADAM_STEP)

    def body(*refs):
        p_refs = refs[:n_seg]
        w_ref, m_ref, v_ref, g_ref, d_ref, nm_ref, nv_ref = refs[n_seg:]
        seg = pl.program_id(0)
        for k, p_ref in enumerate(p_refs):
            @pl.when(seg == k)
            def _(p_ref=p_ref):
                g = p_ref[0].astype(F32)
                for dev in range(1, N_DEV):
                    g = g + p_ref[dev].astype(F32)
                g_ref[...] = g

        g = g_ref[...]
        nm = ADAM_B1 * m_ref[...] + (1.0 - ADAM_B1) * g
        nv = ADAM_B2 * v_ref[...] + (1.0 - ADAM_B2) * (g * g)
        nm_ref[...] = nm
        nv_ref[...] = nv
        d_ref[...] = -ADAM_LR * ((nm / c1) / (jnp.sqrt(nv / c2) + ADAM_EPS) + ADAM_WD * w_ref[...])

    def part_spec(k):
        return pl.BlockSpec((N_DEV, tr, cols), lambda s, i, k=k: (0, jnp.where(s == k, i, 0), 0))

    blk = pl.BlockSpec((tr, cols), lambda s, i: (s * per_seg + i, 0))
    shp = jax.ShapeDtypeStruct((rows, cols), F32)
    return pl.pallas_call(
        body, name=name, grid=(n_seg, per_seg),
        in_specs=[part_spec(k) for k in range(n_seg)] + [blk, blk, blk],
        out_specs=[blk, blk, blk, blk], out_shape=[shp, shp, shp, shp],
        compiler_params=_cp("arbitrary", "arbitrary"))(*parts, w, m, v)


def _pack(arrays, lead=0):
    parts = []
    for a in arrays:
        f = a.reshape(a.shape[:lead] + (-1, LANES))
        pad = -f.shape[lead] % PACK_ROWS
        if pad:
            f = jnp.pad(f, [(0, 0)] * lead + [(0, pad), (0, 0)])
        parts.append(f)
    return jnp.concatenate(parts, axis=lead)


def _unpack(buf, shapes, lead=0):
    out, r = [], 0
    for shp in shapes:
        n = int(np.prod(shp)) // LANES
        part = lax.slice_in_dim(buf, r, r + n, axis=lead)
        out.append(part.reshape(buf.shape[:lead] + tuple(shp)))
        r += n + (-n % PACK_ROWS)
    return out


REPLICATED = ("lb_logits", "norm_mix", "conv_b", "b_r", "b_i", "lam", "hg_norm", "norm_mlp", "norm_final")
SMALL_SHARDED = ("conv_w", "w_r", "w_i")
LARGE_SHARDED = ("w_in", "w_out", "w_up", "w_down")
WEIGHTS = ("lb_logits", "norm_mix", "w_in", "conv_w", "conv_b", "w_r", "b_r", "w_i", "b_i", "lam", "hg_norm",
           "w_out", "norm_mlp", "w_up", "w_down", "norm_final")


def _matmul_weight_shards(p):
    depth = p["w_in"].shape[0]
    cast = {k: p[k].astype(BF16) for k in LARGE_SHARDED}
    return ([cast["w_in"][l] for l in range(depth)],
            [[cast[k][l] for k in ("w_out", "w_up", "w_down")] for l in range(depth)])


def _gathered_rest(got):
    w_out, w_up, w_down = got
    d = w_out.shape[2]
    return dict(w_out=w_out.reshape(d, d), w_up=w_up, w_down=w_down)


def _unpack_mixer_weights(small, p):
    depth, d, _ = p["w_in"].shape
    n_blk = d // RG_BLOCK_W
    conv_w, w_r, w_i = _unpack(small, [p["conv_w"].shape, p["w_r"].shape, p["w_i"].shape], lead=1)
    conv_w = conv_w.transpose(1, 2, 0, 3).reshape(depth, CONV_W, d)
    w_r = w_r.transpose(1, 2, 0, 3, 4).reshape(depth, n_blk, RG_BLOCK_W, RG_BLOCK_W).astype(BF16)
    w_i = w_i.transpose(1, 2, 0, 3, 4).reshape(depth, n_blk, RG_BLOCK_W, RG_BLOCK_W).astype(BF16)
    return conv_w, w_r, w_i


def _local_step(x, target, p):
    bl, seq, d = x.shape
    depth = p["w_in"].shape[0]
    t_rows = bl * seq
    row = lambda a, l: a[l:l + 1]
    lbs = _lower_bounds_fwd(p["lb_logits"])
    shard_in, shard_rest = _matmul_weight_shards(p)
    w_in = _gather_two_level(shard_in[0], "gather_w_in")
    cur = x.reshape(t_rows, d)
    saved, layers = [], []
    for l in range(depth):
        if l == 0:
            (proj, gates, h), small = _inproj_fwd(cur, row(p["norm_mix"], l), w_in,
                                                  carry=([_pack([p["conv_w"], p["w_r"], p["w_i"]])], True))
            conv_w, w_r, w_i = _unpack_mixer_weights(small[0], p)
        else:
            (proj, gates, h), _ = _inproj_fwd(cur, row(p["norm_mix"], l), w_in)
        w = dict(w_in=w_in, conv_w=conv_w[l], w_r=w_r[l], w_i=w_i[l])
        hs, y_a = _mixer_a_fwd(proj, w["conv_w"], row(p["conv_b"], l), w["w_r"], row(p["b_r"], l), w["w_i"],
                               row(p["b_i"], l), row(p["lam"], l), seq)
        (o, o_n), got = _hgrn_fwd(proj, row(lbs, l), row(p["hg_norm"], l), seq,
                                  carry=(shard_rest[l] + ([shard_in[l + 1]] if l + 1 < depth else []), True))
        w.update(_gathered_rest(got[:3]))
        w_in = got[3] if l + 1 < depth else None
        layers.append(w)
        x_mid, y = _merge_out_fwd(gates, y_a, o_n, cur, w["w_out"])
        x_out, u, h2 = _mlp_fwd(x_mid, row(p["norm_mlp"], l), w["w_up"], w["w_down"])
        saved.append(dict(x_in=cur, proj=proj, gates=gates, h=h, hs=hs, y_a=y_a, o=o, o_n=o_n, x_mid=x_mid, y=y, u=u, h2=h2))
        cur = x_out
    loss8, dx, dxb, g_norm_final = _loss_head(cur, p["norm_final"].reshape(1, d), target.reshape(t_rows, d))
    small = ("norm_mix", "conv_w", "conv_b", "w_r", "b_r", "w_i", "b_i", "lam", "hg_norm", "norm_mlp")
    g = {k: [None] * depth for k in small}
    d_lbs, received = [None] * depth, [None] * depth
    g_w_in = None
    for l in reversed(range(depth)):
        s, w = saved[l], layers[l]
        dx_mid, dx_mid_b, du, act, g["norm_mlp"][l] = _mlp_bwd(dx, dxb, s["u"], s["x_mid"], row(p["norm_mlp"], l),
                                                               w["w_up"], w["w_down"])
        g_w_down = _wgrad(act, dxb[None], "wgrad_down")
        g_w_up = _wgrad(s["h2"][None], du, "wgrad_up")
        d_ya, d_on, dp_c = _outproj_bwd(dx_mid_b, w["w_out"], s["gates"], s["y_a"], s["o_n"])
        g_w_out = _wgrad(s["y"][None], dx_mid_b[None], "wgrad_out").reshape(N_DEV, d // N_DEV, d)
        (dp_b, d_lbs[l], g["hg_norm"][l]), got = _hgrn_bwd(
            s["proj"], row(lbs, l), row(p["hg_norm"], l), s["o"], d_on, seq,
            carry=([g_w_out, g_w_up, g_w_down] + ([g_w_in] if g_w_in is not None else []), False))
        received[l] = [None] + list(got[:3])
        if g_w_in is not None:
            received[l + 1][0] = got[3]
        (dp_a, g["w_r"][l], g["w_i"][l], g["b_r"][l], g["b_i"][l], g["lam"][l], g["conv_w"][l],
         g["conv_b"][l]) = _mixer_a_bwd(s["proj"], s["hs"], d_ya, w["conv_w"], row(p["conv_b"], l), w["w_r"],
                                        row(p["b_r"], l), w["w_i"], row(p["b_i"], l), row(p["lam"], l), seq)
        hb = s["h"][None]
        g_w_in = jnp.concatenate([_wgrad(hb, dp_a, "wgrad_in_pair"), _wgrad(hb, dp_b, "wgrad_in_triple"),
                                  _wgrad(hb, dp_c, "wgrad_in_triple")], axis=0)
        carry = None
        if l == 0:
            carry = ([g_w_in, _mixer_grads_by_owner(g, d)], False)
        (dx, dxb, g["norm_mix"][l]), got = _inproj_bwd(dx_mid, dp_a, dp_b, dp_c, w["w_in"], s["x_in"],
                                                       row(p["norm_mix"], l), carry=carry)
    received[0][0], received_mixer = got
    grads = {k: jnp.stack(v) for k, v in g.items()}
    for k in ("norm_mix", "conv_b", "b_r", "b_i", "lam", "hg_norm", "norm_mlp"):
        grads[k] = grads[k][:, 0]
    grads["lb_logits"] = _lower_bounds_bwd(p["lb_logits"], jnp.concatenate(d_lbs, axis=0))
    grads["norm_final"] = g_norm_final[0]
    return loss8[0, 0], dx.reshape(bl, seq, d), grads, received, received_mixer


def _mixer_grads_by_owner(g, d):
    d8, n_blk, rb = d // N_DEV, d // RG_BLOCK_W, RG_BLOCK_W // N_DEV
    depth = len(g["conv_w"])
    conv_w, w_r, w_i = (jnp.stack(g[k]) for k in SMALL_SHARDED)
    return _pack([conv_w.reshape(depth, CONV_W, N_DEV, d8).transpose(2, 0, 1, 3),
                  w_r.reshape(depth, n_blk, N_DEV, rb, RG_BLOCK_W).transpose(2, 0, 1, 3, 4),
                  w_i.reshape(depth, n_blk, N_DEV, rb, RG_BLOCK_W).transpose(2, 0, 1, 3, 4)],
                 lead=1).astype(BF16)


def _update(p, mom1, mom2, grads, received, received_mixer):
    depth = p["w_in"].shape[0]
    out = {}

    for i, k in enumerate(LARGE_SHARDED):
        shp = p[k].shape
        flat = lambda a: a.reshape(shp[0] * shp[1], shp[2])
        parts = [received[l][i] for l in range(depth)]
        res = _reduce_adamw(parts, flat(p[k]), flat(mom1[k]), flat(mom2[k]), "adamw_" + k)
        out[k] = [r.reshape(shp) for r in res]

    res = _reduce_adamw([received_mixer], *[_pack([src[k] for k in SMALL_SHARDED]) for src in (p, mom1, mom2)],
                        "adamw_mixer")
    shapes = [p[k].shape for k in SMALL_SHARDED]
    for i, vals in enumerate(zip(*[_unpack(r, shapes) for r in res])):
        out[SMALL_SHARDED[i]] = list(vals)

    parts = _exchange([_pack([grads[k] for k in REPLICATED])], "gather_grad_replicated", gather=True)
    res = _reduce_adamw(parts, *[_pack([src[k] for k in REPLICATED]) for src in (p, mom1, mom2)],
                        "adamw_replicated")
    shapes = [p[k].shape for k in REPLICATED]
    for i, vals in enumerate(zip(*[_unpack(r, shapes) for r in res])):
        out[REPLICATED[i]] = list(vals)

    return tuple(out[k][i] for i in range(4) for k in WEIGHTS)


def kernel(x, lb_logits, norm_mix, w_in, conv_w, conv_b, w_r, b_r, w_i, b_i, lam, hg_norm, w_out, norm_mlp, w_up, w_down, norm_final, loss_target, m_lb_logits, m_norm_mix, m_w_in, m_conv_w, m_conv_b, m_w_r, m_b_r, m_w_i, m_b_i, m_lam, m_hg_norm, m_w_out, m_norm_mlp, m_w_up, m_w_down, m_norm_final, v_lb_logits, v_norm_mix, v_w_in, v_conv_w, v_conv_b, v_w_r, v_b_r, v_w_i, v_b_i, v_lam, v_hg_norm, v_w_out, v_norm_mlp, v_w_up, v_w_down, v_norm_final):
    p = dict(lb_logits=lb_logits, norm_mix=norm_mix, w_in=w_in, conv_w=conv_w, conv_b=conv_b, w_r=w_r, b_r=b_r,
             w_i=w_i, b_i=b_i, lam=lam, hg_norm=hg_norm, w_out=w_out, norm_mlp=norm_mlp, w_up=w_up,
             w_down=w_down, norm_final=norm_final)
    mom1 = dict(lb_logits=m_lb_logits, norm_mix=m_norm_mix, w_in=m_w_in, conv_w=m_conv_w, conv_b=m_conv_b,
                w_r=m_w_r, b_r=m_b_r, w_i=m_w_i, b_i=m_b_i, lam=m_lam, hg_norm=m_hg_norm, w_out=m_w_out,
                norm_mlp=m_norm_mlp, w_up=m_w_up, w_down=m_w_down, norm_final=m_norm_final)
    mom2 = dict(lb_logits=v_lb_logits, norm_mix=v_norm_mix, w_in=v_w_in, conv_w=v_conv_w, conv_b=v_conv_b,
                w_r=v_w_r, b_r=v_b_r, w_i=v_w_i, b_i=v_b_i, lam=v_lam, hg_norm=v_hg_norm, w_out=v_w_out,
                norm_mlp=v_norm_mlp, w_up=v_w_up, w_down=v_w_down, norm_final=v_norm_final)
    loss, grad_x, grads, received, received_mixer = _local_step(x, loss_target, p)
    loss = lax.psum(loss, ("x", "y", "c"))
    return (loss, grad_x) + _update(p, mom1, mom2, grads, received, received_mixer)
```

```python
import numpy as np

import jax
import jax.numpy as jnp
from jax import lax
from jax.experimental import pallas as pl
from jax.experimental.pallas import tpu as pltpu

F32 = jnp.float32
BF16 = jnp.bfloat16
MESH_ID = pl.DeviceIdType.MESH

N_DEV = 8
N_MIXER_SEGMENTS = 5
NORM_EPS = 1e-6
RG_C = 8.0
RG_BLOCK_W = 256
CONV_W = 4
HG_DK = 128
F_MIN = 1e-30
HG_CHUNK = 16
SUBLANES = 8
LANES = 128
PACK_ROWS = 16
SCAN_GROUP = 16
ROW_CHUNK = 256
ROW_TILE_WEIGHT_STREAM = 1024
WGRAD_TOKEN_TILE = 2048
VMEM_LIMIT_V7X = 56 * 1024 * 1024

ADAM_LR = 0.001
ADAM_B1 = 0.9
ADAM_B2 = 0.999
ADAM_EPS = 1e-08
ADAM_WD = 0.01
ADAM_STEP = 10

GELU_C = 0.7978845608028654
GELU_K = 0.044715


def _cp(*sem):
    return pltpu.CompilerParams(dimension_semantics=sem, vmem_limit_bytes=VMEM_LIMIT_V7X)


def _row_tile(n, cap):
    if n <= cap:
        return n
    t = cap - cap % 16
    while n % t:
        t -= 16
    return t


def _dot(a, b):
    return jnp.dot(a, b, preferred_element_type=F32)


def _dot_nt(a, b):
    return lax.dot_general(a, b, (((1,), (1,)), ((), ())), preferred_element_type=F32)


def _dot_tn(a, b):
    return lax.dot_general(a, b, (((0,), (0,)), ((), ())), preferred_element_type=F32)


def _sigmoid(x):
    return jax.nn.sigmoid(x)


def _sigmoid_pair(x):
    e = jnp.exp(-jnp.abs(x))
    r = 1.0 / (1.0 + e)
    er = e * r
    pos = x >= 0.0
    return jnp.where(pos, r, er), jnp.where(pos, er, r)


def _log1p_pos(y):
    return jnp.where(y < 0.01, y * (1.0 - y * (0.5 - y * (1.0 / 3.0))), jnp.log(1.0 + y))


def _softplus(x):
    return jnp.maximum(x, 0.0) + _log1p_pos(jnp.exp(-jnp.abs(x)))


def _one_minus_exp(x):
    series = -x * (1.0 + x * 0.5 * (1.0 + x * (1.0 / 3.0) * (1.0 + x * 0.25 * (1.0 + x * 0.2))))
    return jnp.where(x > -0.1, series, 1.0 - jnp.exp(x))


def _gelu_and_grad(x):
    x2 = x * x
    t = jnp.tanh(GELU_C * x * (1.0 + GELU_K * x2))
    g = 0.5 * x * (1.0 + t)
    dg = 0.5 * (1.0 + t) + 0.5 * x * (1.0 - t * t) * GELU_C * (1.0 + 3.0 * GELU_K * x2)
    return g, dg


def _silu_and_grad(x):
    s = _sigmoid(x)
    return x * s, s * (1.0 + x * (1.0 - s))


def _rstd(x):
    return lax.rsqrt(jnp.mean(x * x, axis=-1, keepdims=True) + NORM_EPS)


def _rms_bwd(dh, x, g):
    rstd = _rstd(x)
    xh = x * rstd
    dxh = dh * g
    dx = rstd * (dxh - xh * jnp.mean(dxh * xh, axis=-1, keepdims=True))
    return dx, jnp.sum(dh * xh, axis=0, keepdims=True)


def _shift_rows(x, k):
    n = x.shape[0]
    k = k % n
    return x if k == 0 else pltpu.roll(x, k, axis=0)


def _seg_cumsum(x, seg, reverse=False):
    n = x.shape[0]
    rid = lax.broadcasted_iota(jnp.int32, x.shape, 0) & (seg - 1)
    d = 1
    while d < seg:
        if reverse:
            x = jnp.where(rid < seg - d, x + _shift_rows(x, n - d), x)
        else:
            x = jnp.where(rid >= d, x + _shift_rows(x, d), x)
        d *= 2
    return x


def _group_cumsum_matrix(n, seg):
    row = lax.broadcasted_iota(jnp.int32, (n, n), 0)
    col = lax.broadcasted_iota(jnp.int32, (n, n), 1)
    same_group = (row & ~(seg - 1)) == (col & ~(seg - 1))
    return jnp.where(same_group & (col <= row), 1.0, 0.0).astype(BF16)


def _group_cumsum_mxu(x, tri):
    hi = x.astype(BF16)
    lo = (x - hi.astype(F32)).astype(BF16)
    return _dot(tri, hi) + _dot(tri, lo)


def _scan_rows(a_ref, b_ref, out_ref, n_rows, width, reverse):
    gr = min(SCAN_GROUP, n_rows)
    rid = lax.broadcasted_iota(jnp.int32, (gr, width), 0)
    n_groups = n_rows // gr
    per_trip = min(4, n_groups)
    assert n_groups % per_trip == 0

    def local_scan(g):
        r0 = pl.multiple_of(g * gr, gr)
        a = a_ref[pl.ds(r0, gr), :]
        b = b_ref[pl.ds(r0, gr), :]
        d = 1
        while d < gr:
            if reverse:
                keep = rid < gr - d
                a_sh, b_sh = _shift_rows(a, gr - d), _shift_rows(b, gr - d)
            else:
                keep = rid >= d
                a_sh, b_sh = _shift_rows(a, d), _shift_rows(b, d)
            b = jnp.where(keep, a * b_sh + b, b)
            a = jnp.where(keep, a * a_sh, a)
            d *= 2
        return r0, a, b

    def trip(i, carry):
        first = i * per_trip
        groups = [n_groups - 1 - (first + u) if reverse else first + u for u in range(per_trip)]
        for r0, a, b in [local_scan(g) for g in groups]:
            out = a * carry + b
            out_ref[pl.ds(r0, gr), :] = out
            edge = out[0:1, :] if reverse else out[gr - 1:gr, :]
            carry = jnp.broadcast_to(edge, (gr, width))
        return carry

    lax.fori_loop(0, n_groups // per_trip, trip, jnp.zeros((gr, width), F32))


def _lb_softmax_rows(x_ref, depth):
    rows = [x_ref[pl.ds(l, 1), :] for l in range(depth)]
    top = rows[0]
    for r in rows[1:]:
        top = jnp.maximum(top, r)
    e = [jnp.exp(r - top) for r in rows]
    tot = e[0]
    for r in e[1:]:
        tot = tot + r
    return [r / tot for r in e]


def _lower_bounds_fwd(lb_logits):
    depth, d = lb_logits.shape

    def body(x_ref, o_ref):
        sm = _lb_softmax_rows(x_ref, depth)
        cum = jnp.zeros((1, d), F32)
        for l in range(depth):
            cum = cum + sm[l]
            o_ref[pl.ds(l, 1), :] = jnp.clip(cum - sm[0], 0.0, 1.0)

    return pl.pallas_call(body, name="lower_bounds_fwd",
                          out_shape=jax.ShapeDtypeStruct((depth, d), F32))(lb_logits)


def _lower_bounds_bwd(lb_logits, d_lbs):
    depth, d = lb_logits.shape

    def body(x_ref, g_ref, o_ref):
        sm = _lb_softmax_rows(x_ref, depth)
        cum = jnp.zeros((1, d), F32)
        d_cum = []
        for l in range(depth):
            cum = cum + sm[l]
            v = cum - sm[0]
            d_cum.append(jnp.where((v > 0.0) & (v < 1.0), g_ref[pl.ds(l, 1), :], 0.0))
        d_sm = []
        tail = jnp.zeros((1, d), F32)
        for l in reversed(range(depth)):
            tail = tail + d_cum[l]
            d_sm.append(tail)
        d_sm = d_sm[::-1]
        d_sm[0] = d_sm[0] - tail
        inner = jnp.zeros((1, d), F32)
        for l in range(depth):
            inner = inner + sm[l] * d_sm[l]
        for l in range(depth):
            o_ref[pl.ds(l, 1), :] = sm[l] * (d_sm[l] - inner)

    return pl.pallas_call(body, name="lower_bounds_bwd",
                          out_shape=jax.ShapeDtypeStruct((depth, d), F32))(lb_logits, d_lbs)


def _inproj_fwd(x, gain, w_seg, carry=None):
    t_rows, d = x.shape
    tm = _row_tile(t_rows, ROW_TILE_WEIGHT_STREAM)
    n_gate = N_DEV - N_MIXER_SEGMENTS

    def body(x_ref, g_ref, w_ref, proj_ref, gates_ref, h_ref):
        j = pl.program_id(1)

        @pl.when(j == 0)
        def _():
            xv = x_ref[...]
            h_ref[...] = (xv * _rstd(xv) * g_ref[...]).astype(BF16)

        @pl.when(j < N_MIXER_SEGMENTS)
        def _():
            proj_ref[...] = _dot(h_ref[...], w_ref[...])

        @pl.when(j >= N_MIXER_SEGMENTS)
        def _():
            gates_ref[...] = _dot(h_ref[...], w_ref[...]).astype(BF16)

    return _call_carrying(
        body, carry, name="inproj_fwd", grid=(t_rows // tm, N_DEV),
        in_specs=[pl.BlockSpec((tm, d), lambda i, j: (i, 0)),
                  pl.BlockSpec((1, d), lambda i, j: (0, 0)),
                  pl.BlockSpec((None, d, d), lambda i, j: (j, 0, 0))],
        out_specs=[pl.BlockSpec((None, tm, d), lambda i, j: (jnp.minimum(j, N_MIXER_SEGMENTS - 1), i, 0)),
                   pl.BlockSpec((None, tm, d), lambda i, j: (jnp.maximum(j - N_MIXER_SEGMENTS, 0), i, 0)),
                   pl.BlockSpec((tm, d), lambda i, j: (i, 0))],
        out_shape=[jax.ShapeDtypeStruct((N_MIXER_SEGMENTS, t_rows, d), F32),
                   jax.ShapeDtypeStruct((n_gate, t_rows, d), BF16),
                   jax.ShapeDtypeStruct((t_rows, d), BF16)],
        scratch_shapes=[], semantics=("parallel", "arbitrary"), args=(x, gain, w_seg))


def _merge_out_fwd(gates, y_a, o_n, x, w_out):
    t_rows, d = x.shape
    tm = _row_tile(t_rows, 256)

    def body(g_ref, ma_ref, mb_ref, ya_ref, on_ref, x_ref, w_ref, xmid_ref, y_ref):
        g = g_ref[...].astype(F32)
        ya, on = ya_ref[...].astype(F32), on_ref[...].astype(F32)
        y = (_sigmoid(ma_ref[...].astype(F32)) * ya
             + _sigmoid(mb_ref[...].astype(F32)) * (on * (g * _sigmoid(g))))
        yb = y.astype(BF16)
        y_ref[...] = yb
        xmid_ref[...] = x_ref[...] + _dot(yb, w_ref[...])

    seg = lambda k: pl.BlockSpec((None, tm, d), lambda i, k=k: (k, i, 0))
    row = pl.BlockSpec((tm, d), lambda i: (i, 0))
    return pl.pallas_call(
        body, name="merge_out_fwd", grid=(t_rows // tm,),
        in_specs=[seg(0), seg(1), seg(2), row, row, row, pl.BlockSpec((d, d), lambda i: (0, 0))],
        out_specs=[row, row],
        out_shape=[jax.ShapeDtypeStruct((t_rows, d), F32), jax.ShapeDtypeStruct((t_rows, d), BF16)],
        compiler_params=_cp("parallel"))(gates, gates, gates, y_a, o_n, x, w_out)


def _mlp_fwd(x_mid, gain, w_up, w_down):
    t_rows, d = x_mid.shape
    f8 = w_up.shape[2]
    tm = _row_tile(t_rows, ROW_TILE_WEIGHT_STREAM)
    per_step = 2

    def body(x_ref, g_ref, wu_ref, wd_ref, out_ref, u_ref, h_ref):
        @pl.when(pl.program_id(1) == 0)
        def _():
            xv = x_ref[...]
            h_ref[...] = (xv * _rstd(xv) * g_ref[...]).astype(BF16)
            out_ref[...] = xv

        h = h_ref[...]
        down = None
        for k in range(per_step):
            u = _dot(h, wu_ref[k])
            u_ref[k] = u.astype(BF16)
            r = jnp.maximum(u, 0.0)
            part = _dot((r * r).astype(BF16), wd_ref[k])
            down = part if down is None else down + part
        out_ref[...] += down

    row = pl.BlockSpec((tm, d), lambda i, j: (i, 0))
    return pl.pallas_call(
        body, name="mlp_fwd", grid=(t_rows // tm, N_DEV // per_step),
        in_specs=[row, pl.BlockSpec((1, d), lambda i, j: (0, 0)),
                  pl.BlockSpec((per_step, d, f8), lambda i, j: (j, 0, 0)),
                  pl.BlockSpec((per_step, f8, d), lambda i, j: (j, 0, 0))],
        out_specs=[row, pl.BlockSpec((per_step, tm, f8), lambda i, j: (j, i, 0)), row],
        out_shape=[jax.ShapeDtypeStruct((t_rows, d), F32),
                   jax.ShapeDtypeStruct((N_DEV, t_rows, f8), BF16),
                   jax.ShapeDtypeStruct((t_rows, d), BF16)],
        compiler_params=_cp("parallel", "arbitrary"))(x_mid, gain, w_up, w_down)


def _loss_head(x, gain, target):
    t_rows, d = x.shape
    tm = _row_tile(t_rows, 512)

    def body(x_ref, g_ref, t_ref, loss_ref, dx_ref, dxb_ref, dg_ref):
        @pl.when(pl.program_id(0) == 0)
        def _():
            loss_ref[...] = jnp.zeros_like(loss_ref)
            dg_ref[...] = jnp.zeros_like(dg_ref)

        xv = x_ref[...]
        g = g_ref[...]
        err = xv * _rstd(xv) * g - t_ref[...]
        loss_ref[...] += (0.5 / d) * jnp.sum(err * err)
        dx, dg = _rms_bwd(err * (1.0 / d), xv, g)
        dx_ref[...] = dx
        dxb_ref[...] = dx.astype(BF16)
        dg_ref[...] += dg

    row = pl.BlockSpec((tm, d), lambda i: (i, 0))
    vec = pl.BlockSpec((1, d), lambda i: (0, 0))
    return pl.pallas_call(
        body, name="loss_head", grid=(t_rows // tm,),
        in_specs=[row, vec, row],
        out_specs=[pl.BlockSpec((SUBLANES, LANES), lambda i: (0, 0)), row, row, vec],
        out_shape=[jax.ShapeDtypeStruct((SUBLANES, LANES), F32),
                   jax.ShapeDtypeStruct((t_rows, d), F32),
                   jax.ShapeDtypeStruct((t_rows, d), BF16),
                   jax.ShapeDtypeStruct((1, d), F32)],
        compiler_params=_cp("arbitrary"))(x, gain, target)


def _mlp_bwd(d_out, d_out_b, u, x_mid, gain, w_up, w_down):
    t_rows, d = x_mid.shape
    f8 = w_up.shape[2]
    tm = _row_tile(t_rows, ROW_TILE_WEIGHT_STREAM)
    sub = _row_tile(tm, ROW_CHUNK)

    def body(do_ref, dob_ref, u_ref, x_ref, g_ref, wu_ref, wd_ref, dx_ref, dxb_ref, du_ref, act_ref, dg_ref):
        j = pl.program_id(1)

        @pl.when((pl.program_id(0) == 0) & (j == 0))
        def _():
            dg_ref[...] = jnp.zeros_like(dg_ref)

        @pl.when(j == 0)
        def _():
            dx_ref[...] = jnp.zeros_like(dx_ref)

        r = jnp.maximum(u_ref[...].astype(F32), 0.0)
        act_ref[...] = (r * r).astype(BF16)
        du = (_dot_nt(dob_ref[...], wd_ref[...]) * (2.0 * r)).astype(BF16)
        du_ref[...] = du
        dx_ref[...] += _dot_nt(du, wu_ref[...])

        @pl.when(j == N_DEV - 1)
        def _():
            def finish(c, _):
                rows = pl.ds(pl.multiple_of(c * sub, sub), sub)
                dx, dg = _rms_bwd(dx_ref[rows, :], x_ref[rows, :], g_ref[...])
                dx = dx + do_ref[rows, :]
                dx_ref[rows, :] = dx
                dxb_ref[rows, :] = dx.astype(BF16)
                dg_ref[...] += dg
                return 0

            lax.fori_loop(0, tm // sub, finish, 0)

    row = pl.BlockSpec((tm, d), lambda i, j: (i, 0))
    vec = pl.BlockSpec((1, d), lambda i, j: (0, 0))
    hid = pl.BlockSpec((None, tm, f8), lambda i, j: (j, i, 0))
    return pl.pallas_call(
        body, name="mlp_bwd", grid=(t_rows // tm, N_DEV),
        in_specs=[row, row, hid, row, vec,
                  pl.BlockSpec((None, d, f8), lambda i, j: (j, 0, 0)),
                  pl.BlockSpec((None, f8, d), lambda i, j: (j, 0, 0))],
        out_specs=[row, row, hid, hid, vec],
        out_shape=[jax.ShapeDtypeStruct((t_rows, d), F32),
                   jax.ShapeDtypeStruct((t_rows, d), BF16),
                   jax.ShapeDtypeStruct((N_DEV, t_rows, f8), BF16),
                   jax.ShapeDtypeStruct((N_DEV, t_rows, f8), BF16),
                   jax.ShapeDtypeStruct((1, d), F32)],
        compiler_params=_cp("arbitrary", "arbitrary"))(d_out, d_out_b, u, x_mid, gain, w_up, w_down)


def _outproj_bwd(dx_mid_b, w_out, gates, y_a, o_n):
    t_rows, d = y_a.shape
    tm = _row_tile(t_rows, 256)

    def body(dx_ref, w_ref, g_ref, ma_ref, mb_ref, ya_ref, on_ref, dya_ref, don_ref, dp_ref):
        dy = _dot_nt(dx_ref[...], w_ref[...])
        sa = _sigmoid(ma_ref[...].astype(F32))
        sb = _sigmoid(mb_ref[...].astype(F32))
        sg, dsg = _silu_and_grad(g_ref[...].astype(F32))
        ya = ya_ref[...].astype(F32)
        on = on_ref[...].astype(F32)
        dya_ref[...] = dy * sa
        t = dy * sb
        don_ref[...] = t * sg
        dp_ref[0] = (t * on * dsg).astype(BF16)
        dp_ref[1] = (dy * ya * sa * (1.0 - sa)).astype(BF16)
        dp_ref[2] = (dy * on * sg * sb * (1.0 - sb)).astype(BF16)

    seg = lambda k: pl.BlockSpec((None, tm, d), lambda i, k=k: (k, i, 0))
    row = pl.BlockSpec((tm, d), lambda i: (i, 0))
    return pl.pallas_call(
        body, name="outproj_bwd", grid=(t_rows // tm,),
        in_specs=[row, pl.BlockSpec((d, d), lambda i: (0, 0)), seg(0), seg(1), seg(2), row, row],
        out_specs=[row, row, pl.BlockSpec((3, tm, d), lambda i: (0, i, 0))],
        out_shape=[jax.ShapeDtypeStruct((t_rows, d), F32),
                   jax.ShapeDtypeStruct((t_rows, d), F32),
                   jax.ShapeDtypeStruct((3, t_rows, d), BF16)],
        compiler_params=_cp("parallel"))(dx_mid_b, w_out, gates, gates, gates, y_a, o_n)


def _inproj_bwd(dx_mid, dp_a, dp_b, dp_c, w_seg, x_in, gain, carry=None):
    t_rows, d = x_in.shape
    tm = _row_tile(t_rows, ROW_TILE_WEIGHT_STREAM)
    sub = _row_tile(tm, ROW_CHUNK)
    n_a, n_b = dp_a.shape[0], dp_b.shape[0]

    def body(dxm_ref, a_ref, b_ref, c_ref, w_ref, x_ref, g_ref, dx_ref, dxb_ref, dg_ref):
        j = pl.program_id(1)

        @pl.when((pl.program_id(0) == 0) & (j == 0))
        def _():
            dg_ref[...] = jnp.zeros_like(dg_ref)

        @pl.when(j == 0)
        def _():
            dx_ref[...] = jnp.zeros_like(dx_ref)

        @pl.when(j < n_a)
        def _():
            dx_ref[...] += _dot_nt(a_ref[...], w_ref[...])

        @pl.when((j >= n_a) & (j < n_a + n_b))
        def _():
            dx_ref[...] += _dot_nt(b_ref[...], w_ref[...])

        @pl.when(j >= n_a + n_b)
        def _():
            dx_ref[...] += _dot_nt(c_ref[...], w_ref[...])

        @pl.when(j == N_DEV - 1)
        def _():
            def finish(c, _):
                rows = pl.ds(pl.multiple_of(c * sub, sub), sub)
                dx, dg = _rms_bwd(dx_ref[rows, :], x_ref[rows, :], g_ref[...])
                dx = dx + dxm_ref[rows, :]
                dx_ref[rows, :] = dx
                dxb_ref[rows, :] = dx.astype(BF16)
                dg_ref[...] += dg
                return 0

            lax.fori_loop(0, tm // sub, finish, 0)

    def part(first, n):
        return pl.BlockSpec((None, tm, d), lambda i, j: (jnp.clip(j - first, 0, n - 1), i, 0))

    row = pl.BlockSpec((tm, d), lambda i, j: (i, 0))
    vec = pl.BlockSpec((1, d), lambda i, j: (0, 0))
    return _call_carrying(
        body, carry, name="inproj_bwd", grid=(t_rows // tm, N_DEV),
        in_specs=[row, part(0, n_a), part(n_a, n_b), part(n_a + n_b, dp_c.shape[0]),
                  pl.BlockSpec((None, d, d), lambda i, j: (j, 0, 0)), row, vec],
        out_specs=[row, row, vec],
        out_shape=[jax.ShapeDtypeStruct((t_rows, d), F32),
                   jax.ShapeDtypeStruct((t_rows, d), BF16),
                   jax.ShapeDtypeStruct((1, d), F32)],
        scratch_shapes=[], semantics=("arbitrary", "arbitrary"),
        args=(dx_mid, dp_a, dp_b, dp_c, w_seg, x_in, gain))


def _wgrad(a3, b3, name):
    n_a, t_rows, k_a = a3.shape
    n_b, _, n_cols = b3.shape
    n = max(n_a, n_b)
    bk = _row_tile(k_a, 1024)
    bn = n_cols if n_cols <= 1024 else 1024
    tt = _row_tile(t_rows, WGRAD_TOKEN_TILE)
    n_t = t_rows // tt

    def body(a_ref, b_ref, o_ref, acc_ref):
        t, j = pl.program_id(2), pl.program_id(3)
        part = _dot_tn(a_ref[...], b_ref[...])

        @pl.when(t == 0)
        def _():
            acc_ref[j] = part

        @pl.when(t > 0)
        def _():
            acc_ref[j] += part

        @pl.when(t == n_t - 1)
        def _():
            o_ref[...] = acc_ref[j].astype(BF16)

    def out_map(p, q, t, j):
        return (jnp.where(t == n_t - 1, j, 0), p, q)

    return pl.pallas_call(
        body, name=name, grid=(k_a // bk, n_cols // bn, n_t, n),
        in_specs=[pl.BlockSpec((None, tt, bk), lambda p, q, t, j: (j if n_a > 1 else 0, t, p)),
                  pl.BlockSpec((None, tt, bn), lambda p, q, t, j: (j if n_b > 1 else 0, t, q))],
        out_specs=pl.BlockSpec((None, bk, bn), out_map),
        out_shape=jax.ShapeDtypeStruct((n, k_a, n_cols), BF16),
        scratch_shapes=[pltpu.VMEM((n, bk, bn), F32)],
        compiler_params=_cp("parallel", "parallel", "arbitrary", "arbitrary"))(a3, b3)


def _conv_taps(xe, n):
    return [_shift_rows(xe, CONV_W - 1 - j)[SUBLANES:SUBLANES + n, :] for j in range(CONV_W)]


def _rg_gates(xc, w_r, b_r, w_i, b_i, sp8):
    xb = xc.astype(BF16)
    r = _sigmoid(_dot(xb, w_r) + b_r)
    i = _sigmoid(_dot(xb, w_i) + b_i)
    return r, i


def _mixer_a_fwd(proj, conv_w, conv_b, w_r, b_r, w_i, b_i, lam, seq):
    _, t_rows, d = proj.shape
    n_seq, n_blk = t_rows // seq, d // RG_BLOCK_W
    wb = RG_BLOCK_W
    ch = _row_tile(seq, ROW_CHUNK)

    def body(xa_ref, ga_ref, cw_ref, cb_ref, wr_ref, br_ref, wi_ref, bi_ref, lam_ref, h_ref, ya_ref,
             xpad, a_s, u_s):
        xpad[0:SUBLANES, :] = jnp.zeros((SUBLANES, wb), F32)
        xpad[SUBLANES:, :] = xa_ref[...]
        sp8 = RG_C * _softplus(-lam_ref[...])

        def gates(c, _):
            r0 = pl.multiple_of(c * ch, ch)
            taps = _conv_taps(xpad[pl.ds(r0, ch + SUBLANES), :], ch)
            xc = cb_ref[...] + sum(cw_ref[pl.ds(j, 1), :] * taps[j] for j in range(CONV_W))
            r, i = _rg_gates(xc, wr_ref[...], br_ref[...], wi_ref[...], bi_ref[...], sp8)
            log_a = -(r * sp8)
            a_s[pl.ds(r0, ch), :] = jnp.exp(log_a)
            u_s[pl.ds(r0, ch), :] = jnp.sqrt(jnp.maximum(_one_minus_exp(2.0 * log_a), 0.0)) * (i * xc)
            return 0

        lax.fori_loop(0, seq // ch, gates, 0)
        _scan_rows(a_s, u_s, h_ref, seq, wb, reverse=False)

        def gate_out(c, _):
            r0 = pl.multiple_of(c * ch, ch)
            gl, _ = _gelu_and_grad(ga_ref[pl.ds(r0, ch), :])
            ya_ref[pl.ds(r0, ch), :] = (h_ref[pl.ds(r0, ch), :] * gl).astype(BF16)
            return 0

        lax.fori_loop(0, seq // ch, gate_out, 0)

    seg = lambda k: pl.BlockSpec((None, seq, wb), lambda s, b, k=k: (k, s, b))
    blk = pl.BlockSpec((seq, wb), lambda s, b: (s, b))
    vec = pl.BlockSpec((1, wb), lambda s, b: (0, b))
    wsp = pl.BlockSpec((None, wb, wb), lambda s, b: (b, 0, 0))
    return pl.pallas_call(
        body, name="mixer_a_fwd", grid=(n_seq, n_blk),
        in_specs=[seg(0), seg(1), pl.BlockSpec((CONV_W, wb), lambda s, b: (0, b)), vec, wsp, vec, wsp, vec, vec],
        out_specs=[blk, blk],
        out_shape=[jax.ShapeDtypeStruct((t_rows, d), F32), jax.ShapeDtypeStruct((t_rows, d), BF16)],
        scratch_shapes=[pltpu.VMEM((seq + SUBLANES, wb), F32), pltpu.VMEM((seq, wb), F32),
                        pltpu.VMEM((seq, wb), F32)],
        compiler_params=_cp("parallel", "parallel"))(proj, proj, conv_w, conv_b, w_r, b_r, w_i, b_i, lam)


def _mixer_a_bwd(proj, h, d_ya, conv_w, conv_b, w_r, b_r, w_i, b_i, lam, seq):
    _, t_rows, d = proj.shape
    n_seq, n_blk = t_rows // seq, d // RG_BLOCK_W
    wb = RG_BLOCK_W
    ch = _row_tile(seq, ROW_CHUNK)
    n_ch = seq // ch

    def body(xa_ref, ga_ref, h_ref, dya_ref, cw_ref, cb_ref, wr_ref, br_ref, wi_ref, bi_ref, lam_ref,
             dp_ref, dwr_ref, dwi_ref, dbr_ref, dbi_ref, dlam_ref, dcw_ref, dcb_ref,
             xpad, hpad, a_s, e_pad, g_s, xc_s, r_s, i_s, dxc_pad):
        @pl.when(pl.program_id(1) == 0)
        def _():
            for ref in (dwr_ref, dwi_ref, dbr_ref, dbi_ref, dlam_ref, dcw_ref, dcb_ref):
                ref[...] = jnp.zeros_like(ref)

        zeros8 = jnp.zeros((SUBLANES, wb), F32)
        xpad[0:SUBLANES, :] = zeros8
        xpad[SUBLANES:, :] = xa_ref[...]
        hpad[0:SUBLANES, :] = zeros8
        hpad[SUBLANES:, :] = h_ref[...]
        e_pad[seq:, :] = zeros8
        dxc_pad[seq:, :] = zeros8
        lam_v = lam_ref[...]
        sp8 = RG_C * _softplus(-lam_v)

        def recompute(c, _):
            r0 = pl.multiple_of(c * ch, ch)
            rows = pl.ds(r0, ch)
            taps = _conv_taps(xpad[pl.ds(r0, ch + SUBLANES), :], ch)
            xc = cb_ref[...] + sum(cw_ref[pl.ds(j, 1), :] * taps[j] for j in range(CONV_W))
            r, i = _rg_gates(xc, wr_ref[...], br_ref[...], wi_ref[...], bi_ref[...], sp8)
            a = jnp.exp(-(r * sp8))
            gl, dgl = _gelu_and_grad(ga_ref[rows, :])
            dya = dya_ref[rows, :]
            g = dya * gl
            dp_ref[1, rows, :] = (dya * h_ref[rows, :] * dgl).astype(BF16)
            a_s[rows, :] = a
            e_pad[rows, :] = a * g
            g_s[rows, :] = g
            xc_s[rows, :] = xc
            r_s[rows, :] = r
            i_s[rows, :] = i
            return 0

        lax.fori_loop(0, n_ch, recompute, 0)
        _scan_rows(a_s, e_pad, e_pad, seq, wb, reverse=True)

        def grads(c, _):
            r0 = pl.multiple_of(c * ch, ch)
            rows = pl.ds(r0, ch)
            halo = pl.ds(r0, ch + SUBLANES)
            dh = g_s[rows, :] + _shift_rows(e_pad[halo, :], ch + SUBLANES - 1)[0:ch, :]
            h_prev = _shift_rows(hpad[halo, :], 1)[SUBLANES:, :]
            xc, r, i = xc_s[rows, :], r_s[rows, :], i_s[rows, :]
            log_a = -(r * sp8)
            a = jnp.exp(log_a)
            om = _one_minus_exp(2.0 * log_a)
            sq = jnp.sqrt(jnp.maximum(om, 0.0))
            t1 = dh * xc
            d_i = t1 * sq
            d_la = dh * h_prev * a + jnp.where(om > 0.0, -(t1 * i) * (1.0 - om) / sq, 0.0)
            dpr = -(d_la * sp8) * r * (1.0 - r)
            dpi = d_i * i * (1.0 - i)
            dprb, dpib, xb = dpr.astype(BF16), dpi.astype(BF16), xc.astype(BF16)
            dxc = dh * sq * i + _dot_nt(dprb, wr_ref[...]) + _dot_nt(dpib, wi_ref[...])
            dwr_ref[...] += _dot_tn(xb, dprb)
            dwi_ref[...] += _dot_tn(xb, dpib)
            dbr_ref[...] += jnp.sum(dpr, axis=0, keepdims=True)
            dbi_ref[...] += jnp.sum(dpi, axis=0, keepdims=True)
            dlam_ref[...] += jnp.sum(d_la * r, axis=0, keepdims=True) * (RG_C * _sigmoid(-lam_v))
            dcb_ref[...] += jnp.sum(dxc, axis=0, keepdims=True)
            taps = _conv_taps(xpad[halo, :], ch)
            for j in range(CONV_W):
                dcw_ref[pl.ds(j, 1), :] += jnp.sum(dxc * taps[j], axis=0, keepdims=True)
            dxc_pad[rows, :] = dxc
            return 0

        lax.fori_loop(0, n_ch, grads, 0)

        def conv_bwd(c, _):
            r0 = pl.multiple_of(c * ch, ch)
            de = dxc_pad[pl.ds(r0, ch + SUBLANES), :]
            dxa = sum(cw_ref[pl.ds(j, 1), :] * _shift_rows(de, ch + SUBLANES - (CONV_W - 1 - j))[0:ch, :]
                      for j in range(CONV_W))
            dp_ref[0, pl.ds(r0, ch), :] = dxa.astype(BF16)
            return 0

        lax.fori_loop(0, n_ch, conv_bwd, 0)

    seg = lambda k: pl.BlockSpec((None, seq, wb), lambda b, s, k=k: (k, s, b))
    blk = pl.BlockSpec((seq, wb), lambda b, s: (s, b))
    vec = pl.BlockSpec((1, wb), lambda b, s: (0, b))
    taps = pl.BlockSpec((CONV_W, wb), lambda b, s: (0, b))
    wsp = pl.BlockSpec((None, wb, wb), lambda b, s: (b, 0, 0))
    vec_shape = jax.ShapeDtypeStruct((1, d), F32)
    w_shape = jax.ShapeDtypeStruct((n_blk, wb, wb), F32)
    pad = pltpu.VMEM((seq + SUBLANES, wb), F32)
    full = pltpu.VMEM((seq, wb), F32)
    return pl.pallas_call(
        body, name="mixer_a_bwd", grid=(n_blk, n_seq),
        in_specs=[seg(0), seg(1), blk, blk, taps, vec, wsp, vec, wsp, vec, vec],
        out_specs=[pl.BlockSpec((2, seq, wb), lambda b, s: (0, s, b)), wsp, wsp, vec, vec, vec, taps, vec],
        out_shape=[jax.ShapeDtypeStruct((2, t_rows, d), BF16), w_shape, w_shape, vec_shape, vec_shape,
                   vec_shape, jax.ShapeDtypeStruct((CONV_W, d), F32), vec_shape],
        scratch_shapes=[pad, pad, full, pad, full, full, full, full, pad],
        compiler_params=_cp("parallel", "arbitrary"))(
            proj, proj, h, d_ya, conv_w, conv_b, w_r, b_r, w_i, b_i, lam)


def _hg_prepare(q_ref, z_ref, lb, rows):
    z = z_ref[rows, :]
    sig, nsig = _sigmoid_pair(z)
    fg = lb + (1.0 - lb) * sig
    log_f = jnp.log(jnp.maximum(fg, F_MIN))
    key = (1.0 - lb) * nsig
    qs, _ = _silu_and_grad(q_ref[rows, :])
    return qs, key, log_f, sig, fg


HG_UNROLL_TERMS = 64
HG_UNROLL_FWD = 32
HG_UNROLL_BWD = 32
HG_HALF = HG_CHUNK // 2
HG_STACK = HG_CHUNK * HG_HALF
assert HG_HALF == SUBLANES


def _half_of(x, s):
    return x[:HG_HALF, :] if s < HG_HALF else x[HG_HALF:, :]


def _hg_decay(g_ref, r0, g_rows, first_row, s):
    rid = lax.broadcasted_iota(jnp.int32, g_rows.shape, 0) + first_row
    gs = g_ref[pl.ds(r0 + s, 1), :]
    return jnp.where(rid >= s, jnp.exp(g_rows - gs), 0.0)


def _half_start(s):
    return 0 if s < HG_HALF else HG_HALF


def _hg_cross_decays(gc):
    rid = lax.broadcasted_iota(jnp.int32, (HG_CHUNK, HG_DK), 0)
    g_mid = gc[HG_HALF - 1:HG_HALF, :]
    e_hi = jnp.where(rid >= HG_HALF, jnp.exp(gc - g_mid), 0.0)
    e_lo = jnp.where(rid < HG_HALF, jnp.exp(g_mid - gc), 0.0)
    return e_hi, e_lo


def _hg_cross(qc, kc, gc):
    e_hi, e_lo = _hg_cross_decays(gc)
    return (qc * e_hi).astype(BF16), (kc * e_lo).astype(BF16)


def _stack(slabs):
    return jnp.concatenate(slabs, axis=0).astype(BF16)


def _slab_row_sums():
    row = lax.broadcasted_iota(jnp.int32, (HG_CHUNK, HG_STACK), 0)
    col = lax.broadcasted_iota(jnp.int32, (HG_CHUNK, HG_STACK), 1)
    lo = row * HG_HALF
    return jnp.where((col >= lo) & (col < lo + HG_HALF), 1.0, 0.0).astype(BF16)


def _for_chunks(n, unroll, *stages, after_trip=None):
    unroll = min(unroll, n)
    assert n % unroll == 0

    def trip(i, _):
        chunks = [i * unroll + u for u in range(unroll)]
        carried = [stages[0](c) for c in chunks]
        for stage in stages[1:]:
            carried = [stage(c, x) for c, x in zip(chunks, carried)]
        if after_trip is not None:
            after_trip(carried)
        return 0

    lax.fori_loop(0, n // unroll, trip, 0)


def _hg_state_chain(states, g_ref, carry_ref, n_chunks, reverse):
    unroll = min(8, n_chunks)
    assert n_chunks % unroll == 0
    carry_ref[...] = jnp.zeros_like(carry_ref)

    def trip(i, _):
        st = carry_ref[...]
        for u in range(unroll):
            k = i * unroll + u
            c = n_chunks - 1 - k if reverse else k
            term = states[c]
            states[c] = st
            st = st * jnp.exp(g_ref[pl.ds(c * HG_CHUNK + HG_CHUNK - 1, 1), :]) + term
        carry_ref[...] = st
        return 0

    lax.fori_loop(0, n_chunks // unroll, trip, 0)


def _hgrn_fwd(proj, lower_bound, hg_gain, seq, carry=None):
    _, t_rows, d = proj.shape
    n_seq, n_head = t_rows // seq, d // HG_DK
    ch = _row_tile(seq, ROW_CHUNK)
    n_chunks = seq // HG_CHUNK

    def body(q_ref, z_ref, v_ref, lb_ref, gain_ref, o_ref, on_ref, qs_s, k_s, g_s, states, st_ref):
        lb = lb_ref[...]
        tri = _group_cumsum_matrix(ch, HG_CHUNK)

        per_block = ch // HG_CHUNK

        def prepare(c):
            rows = pl.ds(pl.multiple_of(c * ch, ch), ch)
            qs, key, log_f, _, _ = _hg_prepare(q_ref, z_ref, lb, rows)
            qs_s[rows, :] = qs
            k_s[rows, :] = key
            return key, _group_cumsum_mxu(log_f, tri)

        def state_terms(c, carried):
            key, g = carried
            r0 = pl.multiple_of(c * ch, ch)
            g_s[pl.ds(r0, ch), :] = g
            terms = []
            for u in range(per_block):
                sl = slice(u * HG_CHUNK, (u + 1) * HG_CHUNK)
                k_end = key[sl, :] * jnp.exp(g[(u + 1) * HG_CHUNK - 1:(u + 1) * HG_CHUNK, :] - g[sl, :])
                vc = v_ref[pl.ds(r0 + u * HG_CHUNK, HG_CHUNK), :]
                terms.append(_dot_tn(vc.astype(BF16), k_end.astype(BF16)))
            return terms

        def store_terms(c, terms):
            for u, term in enumerate(terms):
                states[c * per_block + u] = term

        _for_chunks(seq // ch, 4, prepare, state_terms, store_terms)
        _hg_state_chain(states, g_s, st_ref, n_chunks, reverse=False)
        ones = jnp.ones((HG_DK, HG_DK), BF16)

        def issue(c):
            r0 = pl.multiple_of(c * HG_CHUNK, HG_CHUNK)
            rows = pl.ds(r0, HG_CHUNK)
            qc, gc = qs_s[rows, :], g_s[rows, :]
            o = _dot_nt((qc * jnp.exp(gc)).astype(BF16), states[c].astype(BF16))
            pairs = [qc[_half_start(s):, :] * _hg_decay(g_s, r0, gc[_half_start(s):, :], _half_start(s), s)
                     * k_s[pl.ds(r0 + s, 1), :] for s in range(HG_CHUNK)]
            score = _dot(_stack(pairs), ones)
            return o, score

        def combine(c, issued):
            o, score = issued
            r0 = pl.multiple_of(c * HG_CHUNK, HG_CHUNK)
            halves = [o[:HG_HALF, :], o[HG_HALF:, :]]
            first = 0
            for s in range(HG_CHUNK):
                vs = v_ref[pl.ds(r0 + s, 1), :]
                if s < HG_HALF:
                    halves[0] += score[first:first + HG_HALF, :] * vs
                    first += HG_HALF
                halves[1] += score[first:first + HG_HALF, :] * vs
                first += HG_HALF
            o = jnp.concatenate(halves, axis=0)
            o_ref[pl.ds(r0, HG_CHUNK), :] = o
            on_ref[pl.ds(r0, HG_CHUNK), :] = (o * _rstd(o) * gain_ref[...]).astype(BF16)

        _for_chunks(n_chunks, HG_UNROLL_FWD, issue, combine)

    seg = lambda k: pl.BlockSpec((None, seq, HG_DK), lambda s, h, k=k: (k, s, h))
    blk = pl.BlockSpec((seq, HG_DK), lambda s, h: (s, h))
    full = pltpu.VMEM((seq, HG_DK), F32)
    return _call_carrying(
        body, carry, name="hgrn_fwd", grid=(n_seq, n_head),
        in_specs=[seg(2), seg(3), seg(4), pl.BlockSpec((1, HG_DK), lambda s, h: (0, h)),
                  pl.BlockSpec((1, HG_DK), lambda s, h: (0, 0))],
        out_specs=[blk, blk],
        out_shape=[jax.ShapeDtypeStruct((t_rows, d), F32), jax.ShapeDtypeStruct((t_rows, d), BF16)],
        scratch_shapes=[full, full, full, pltpu.VMEM((n_chunks, HG_DK, HG_DK), F32),
                        pltpu.VMEM((HG_DK, HG_DK), F32)],
        semantics=("parallel", "parallel"), args=(proj, proj, proj, lower_bound, hg_gain))


def _hgrn_bwd(proj, lower_bound, hg_gain, o, d_on, seq, carry=None):
    _, t_rows, d = proj.shape
    n_seq, n_head = t_rows // seq, d // HG_DK
    ch = _row_tile(seq, ROW_CHUNK)
    n_chunks = seq // HG_CHUNK
    cc = HG_CHUNK

    def body(q_ref, z_ref, v_ref, lb_ref, gain_ref, o_ref, don_ref, dp_ref, dlb_ref, dgain_ref,
             qs_s, k_s, g_s, do_s, dlb_acc, states, dstates, carry_ref):
        hh, ss = pl.program_id(0), pl.program_id(1)
        lb = lb_ref[...]

        @pl.when(ss == 0)
        def _():
            dlb_ref[...] = jnp.zeros_like(dlb_ref)

        @pl.when((ss == 0) & (hh == 0))
        def _():
            dgain_ref[...] = jnp.zeros_like(dgain_ref)

        tri = _group_cumsum_matrix(ch, cc)

        def prepare(c):
            rows = pl.ds(pl.multiple_of(c * ch, ch), ch)
            qs, key, log_f, _, _ = _hg_prepare(q_ref, z_ref, lb, rows)
            qs_s[rows, :] = qs
            k_s[rows, :] = key
            do, dgain = _rms_bwd(don_ref[rows, :], o_ref[rows, :], gain_ref[...])
            do_s[rows, :] = do
            dgain_ref[...] += dgain
            return _group_cumsum_mxu(log_f, tri)

        def store_cumsum(c, g):
            g_s[pl.ds(pl.multiple_of(c * ch, ch), ch), :] = g

        _for_chunks(seq // ch, 4, prepare, store_cumsum)

        def chain_terms(c):
            rows = pl.ds(pl.multiple_of(c * cc, cc), cc)
            gc = g_s[rows, :]
            k_end = k_s[rows, :] * jnp.exp(gc[cc - 1:cc, :] - gc)
            q_in = qs_s[rows, :] * jnp.exp(gc)
            return (_dot_tn(v_ref[rows, :].astype(BF16), k_end.astype(BF16)),
                    _dot_tn(do_s[rows, :].astype(BF16), q_in.astype(BF16)))

        def store_terms(c, terms):
            states[c], dstates[c] = terms

        _for_chunks(n_chunks, HG_UNROLL_TERMS, chain_terms, store_terms)
        _hg_state_chain(states, g_s, carry_ref, n_chunks, reverse=False)
        _hg_state_chain(dstates, g_s, carry_ref, n_chunks, reverse=True)
        ones = jnp.ones((HG_DK, HG_DK), BF16)
        row_sums = _slab_row_sums()

        def chunk_rows(c):
            r0 = pl.multiple_of(c * cc, cc)
            return r0, pl.ds(r0, cc)

        def through_state(c):
            r0, rows = chunk_rows(c)
            qc, kc, gc, vc, doc = qs_s[rows, :], k_s[rows, :], g_s[rows, :], v_ref[rows, :], do_s[rows, :]
            st, dst = states[c], dstates[c]
            g_last = gc[cc - 1:cc, :]
            e_last, e_end = jnp.exp(g_last), jnp.exp(g_last - gc)
            dob, vcb, dstb = doc.astype(BF16), vc.astype(BF16), dst.astype(BF16)
            dqs = _dot(dob, st.astype(BF16))
            dk_state = _dot(vcb, dstb)
            dv = _dot_nt((kc * e_end).astype(BF16), dstb)
            cots = [_half_of(doc, s) * v_ref[pl.ds(r0 + s, 1), :] for s in range(cc)]
            d_score = _dot(_stack(cots), ones)
            x, y = _hg_cross(qc, kc, gc)
            cross = (_dot_nt(dob, vcb), _dot_nt(vcb, dob), _dot_nt(y, x))
            return dqs, dk_state, dv, d_score, e_last * jnp.sum(dst * st, axis=0, keepdims=True), cross

        def pair_terms(c, carried):
            dqs, dk_state, dv, d_score, d_glast, (da_cross, da_cross_t, a_cross_t) = carried
            r0, rows = chunk_rows(c)
            qc, kc, gc = qs_s[rows, :], k_s[rows, :], g_s[rows, :]
            dqs = dqs * jnp.exp(gc)
            dk_state = dk_state * jnp.exp(gc[cc - 1:cc, :] - gc)
            d_glast = d_glast + jnp.sum(kc * dk_state, axis=0, keepdims=True)
            dqs_half = [dqs[:HG_HALF, :], dqs[HG_HALF:, :]]
            pairs, dk_terms = [], []
            for s in range(cc):
                qv = _half_of(qc, s)
                decay = _hg_decay(g_s, r0, _half_of(gc, s), _half_start(s), s)
                ks = k_s[pl.ds(r0 + s, 1), :]
                da_decay = d_score[s * HG_HALF:(s + 1) * HG_HALF, :] * decay
                pairs.append(qv * decay * ks)
                dk_terms.append(da_decay * qv)
                dqs_half[s // HG_HALF] += da_decay * ks
            score = _dot(_stack(pairs), ones)
            dk = dk_state + _dot(row_sums, _stack(dk_terms))
            x, y = _hg_cross(qc, kc, gc)
            cross = (_dot(da_cross.astype(BF16), y), _dot(da_cross_t.astype(BF16), x),
                     _dot(a_cross_t.astype(BF16), do_s[rows, :].astype(BF16)))
            return jnp.concatenate(dqs_half, axis=0), dk, dv, score, d_glast, cross

        def value_terms(c, carried):
            dqs, dk, dv, score, d_glast, (dx_cross, dy_cross, dv_cross) = carried
            _, rows = chunk_rows(c)
            doc, gc = do_s[rows, :], g_s[rows, :]
            dv_terms = [score[s * HG_HALF:(s + 1) * HG_HALF, :] * _half_of(doc, s) for s in range(cc)]
            e_hi, e_lo = _hg_cross_decays(gc)
            return (dqs + dx_cross * e_hi, dk + dy_cross * e_lo,
                    dv + dv_cross + _dot(row_sums, _stack(dv_terms)), d_glast)

        dlb_acc[...] = jnp.zeros_like(dlb_acc)

        def store(c, x):
            dqs, dk, dv, d_glast = x
            _, rows = chunk_rows(c)
            d_g = qs_s[rows, :] * dqs - k_s[rows, :] * dk
            d_logf = _seg_cumsum(d_g, cc, reverse=True) + d_glast
            sig, nsig = _sigmoid_pair(z_ref[rows, :])
            fg = lb + (1.0 - lb) * sig
            _, dsilu = _silu_and_grad(q_ref[rows, :])
            d_gate = jnp.where(fg > F_MIN, d_logf / fg, 0.0) - dk
            dp_ref[0, rows, :] = (dqs * dsilu).astype(BF16)
            dp_ref[1, rows, :] = (d_gate * (1.0 - lb) * sig * nsig).astype(BF16)
            dp_ref[2, rows, :] = dv.astype(BF16)
            return d_gate * nsig

        def add_lower_bound_grads(per_chunk):
            while len(per_chunk) > 1:
                per_chunk = [a + b for a, b in zip(per_chunk[::2], per_chunk[1::2])]
            dlb_acc[...] += per_chunk[0]

        _for_chunks(n_chunks, HG_UNROLL_BWD, through_state, pair_terms, value_terms, store,
                    after_trip=add_lower_bound_grads)
        dlb_ref[...] += jnp.sum(dlb_acc[...], axis=0, keepdims=True)

    seg = lambda k: pl.BlockSpec((None, seq, HG_DK), lambda h, s, k=k: (k, s, h))
    blk = pl.BlockSpec((seq, HG_DK), lambda h, s: (s, h))
    full = pltpu.VMEM((seq, HG_DK), F32)
    return _call_carrying(
        body, carry, name="hgrn_bwd", grid=(n_head, n_seq),
        in_specs=[seg(2), seg(3), seg(4), pl.BlockSpec((1, HG_DK), lambda h, s: (0, h)),
                  pl.BlockSpec((1, HG_DK), lambda h, s: (0, 0)), blk, blk],
        out_specs=[pl.BlockSpec((3, seq, HG_DK), lambda h, s: (0, s, h)),
                   pl.BlockSpec((1, HG_DK), lambda h, s: (0, h)),
                   pl.BlockSpec((1, HG_DK), lambda h, s: (0, 0))],
        out_shape=[jax.ShapeDtypeStruct((3, t_rows, d), BF16), jax.ShapeDtypeStruct((1, d), F32),
                   jax.ShapeDtypeStruct((1, HG_DK), F32)],
        scratch_shapes=[full, full, full, full, pltpu.VMEM((cc, HG_DK), F32),
                        pltpu.VMEM((n_chunks, HG_DK, HG_DK), F32), pltpu.VMEM((n_chunks, HG_DK, HG_DK), F32),
                        pltpu.VMEM((HG_DK, HG_DK), F32)],
        semantics=("arbitrary", "arbitrary"), args=(proj, proj, proj, lower_bound, hg_gain, o, d_on))


def _mesh_place():
    x, y, c = lax.axis_index("x"), lax.axis_index("y"), lax.axis_index("c")
    return x, y, c


def _peer(place, k):
    x, y, c = place
    px = 1 - x if k & 4 else x
    py = 1 - y if k & 2 else y
    pc = 1 - c if k & 1 else c
    return (px, py, pc), 4 * px + 2 * py + pc


class _Exchange:
    def __init__(self, srcs, gather):
        self.n = len(srcs)
        self.gather = gather
        self.out_shape = [jax.ShapeDtypeStruct((N_DEV,) + tuple(s.shape if gather else s.shape[1:]), s.dtype)
                          for s in srcs]
        self.scratch = [pltpu.SemaphoreType.DMA((self.n * (N_DEV - 1),)),
                        pltpu.SemaphoreType.DMA((self.n * (N_DEV - 1),)),
                        pltpu.SemaphoreType.DMA((self.n,))]

    def _copies(self, src_refs, out_refs, sems):
        send_sems, recv_sems, local_sems = sems
        place = _mesh_place()
        me = 4 * place[0] + 2 * place[1] + place[2]
        local, sends, recvs = [], [], []
        for a, (src, out) in enumerate(zip(src_refs, out_refs)):
            outgoing = (lambda idx, src=src: src) if self.gather else (lambda idx, src=src: src.at[idx])
            local.append(pltpu.make_async_copy(outgoing(me), out.at[me], local_sems.at[a]))
            for k in range(1, N_DEV):
                peer, peer_idx = _peer(place, k)
                sem = a * (N_DEV - 1) + k - 1
                sends.append(pltpu.make_async_remote_copy(
                    src_ref=outgoing(peer_idx), dst_ref=out.at[me], send_sem=send_sems.at[sem],
                    recv_sem=recv_sems.at[sem], device_id=peer, device_id_type=MESH_ID))
                recvs.append(pltpu.make_async_remote_copy(
                    src_ref=outgoing(peer_idx), dst_ref=out.at[peer_idx], send_sem=send_sems.at[sem],
                    recv_sem=recv_sems.at[sem], device_id=peer, device_id_type=MESH_ID))
        return local, sends, recvs

    def start(self, src_refs, out_refs, sems):
        local, sends, _ = self._copies(src_refs, out_refs, sems)
        for cp in local + sends:
            cp.start()

    def wait(self, src_refs, out_refs, sems):
        local, sends, recvs = self._copies(src_refs, out_refs, sems)
        for cp in recvs:
            cp.wait_recv()
        for cp in sends:
            cp.wait_send()
        for cp in local:
            cp.wait()


def _call_carrying(body, carry, *, name, grid, in_specs, out_specs, out_shape, scratch_shapes, semantics, args):
    if carry is None:
        outs = pl.pallas_call(body, name=name, grid=grid, in_specs=in_specs, out_specs=out_specs,
                              out_shape=out_shape, scratch_shapes=scratch_shapes,
                              compiler_params=_cp(*semantics))(*args)
        return outs, []
    srcs, gather = carry
    ex = _Exchange(srcs, gather)
    n, n_in, n_out, n_scr = ex.n, len(in_specs), len(out_specs), len(scratch_shapes)

    def wrapped(*refs):
        ins, refs = refs[:n_in], refs[n_in:]
        src_refs, refs = refs[:n], refs[n:]
        outs, refs = refs[:n_out], refs[n_out:]
        dst_refs, refs = refs[:n], refs[n:]
        scratch, sems = refs[:n_scr], refs[n_scr:]
        first, last = None, None
        for axis, size in enumerate(grid):
            i = pl.program_id(axis)
            first = (i == 0) if first is None else first & (i == 0)
            last = (i == size - 1) if last is None else last & (i == size - 1)

        @pl.when(first)
        def _():
            ex.start(src_refs, dst_refs, sems)

        body(*ins, *outs, *scratch)

        @pl.when(last)
        def _():
            ex.wait(src_refs, dst_refs, sems)

    any_space = pl.BlockSpec(memory_space=pl.ANY)
    res = pl.pallas_call(
        wrapped, name=name + "_carrying", grid=grid, in_specs=list(in_specs) + [any_space] * n,
        out_specs=list(out_specs) + [any_space] * n, out_shape=list(out_shape) + ex.out_shape,
        scratch_shapes=list(scratch_shapes) + ex.scratch,
        compiler_params=_cp(*(["arbitrary"] * len(grid))))(*args, *srcs)
    return res[:n_out], res[n_out:]


def _exchange(srcs, name, gather):
    ex = _Exchange(srcs, gather)
    n = ex.n

    def body(*refs):
        src_refs, out_refs, sems = refs[:n], refs[n:2 * n], refs[2 * n:]
        ex.start(src_refs, out_refs, sems)
        ex.wait(src_refs, out_refs, sems)

    any_space = pl.BlockSpec(memory_space=pl.ANY)
    return pl.pallas_call(
        body, name=name, in_specs=[any_space] * n, out_specs=[any_space] * n,
        out_shape=ex.out_shape, scratch_shapes=ex.scratch)(*srcs)


def _gather_two_level(src, name):
    def body(x_ref, out_ref, send_sems, recv_sems, local_sem):
        x, y, c = _mesh_place()
        me, sibling = (x, y, c), (x, y, 1 - c)
        chips = [(1 - x, y), (x, 1 - y), (1 - x, 1 - y)]

        def slot(px, py, pc):
            return out_ref.at[4 * px + 2 * py + pc]

        def copy(k, block, to, source=None):
            return pltpu.make_async_remote_copy(
                src_ref=slot(*block) if source is None else source, dst_ref=slot(*block),
                send_sem=send_sems.at[k], recv_sem=recv_sems.at[k], device_id=to, device_id_type=MESH_ID)

        mine = pltpu.make_async_copy(x_ref, slot(*me), local_sem)
        mine.start()
        first = [copy(0, me, sibling, source=x_ref)]
        first += [copy(1 + j, me, (*chip, c), source=x_ref) for j, chip in enumerate(chips)]
        for cp in first:
            cp.start()
        passed = [copy(4 + j, (*chip, c), sibling) for j, chip in enumerate(chips)]
        for j, chip in enumerate(chips):
            copy(1 + j, (*chip, c), me).wait_recv()
            passed[j].start()
        copy(0, sibling, me).wait_recv()
        for j, chip in enumerate(chips):
            copy(4 + j, (*chip, 1 - c), me).wait_recv()
        for cp in first + passed:
            cp.wait_send()
        mine.wait()

    any_space = pl.BlockSpec(memory_space=pl.ANY)
    n_copies = N_DEV - 1
    return pl.pallas_call(
        body, name=name, in_specs=[any_space], out_specs=any_space,
        out_shape=jax.ShapeDtypeStruct((N_DEV,) + tuple(src.shape), src.dtype),
        scratch_shapes=[pltpu.SemaphoreType.DMA((n_copies,)), pltpu.SemaphoreType.DMA((n_copies,)),
                        pltpu.SemaphoreType.DMA])(src)


def _reduce_adamw(parts, w, m, v, name):
    rows, cols = w.shape
    n_seg = len(parts)
    seg_rows = rows // n_seg
    tr = _row_tile(seg_rows, 128)
    per_seg = seg_rows // tr
    c1 = np.float32(1.0 - ADAM_B1 ** ADAM_STEP)
    c2 = np.float32(1.0 - ADAM_B2 ** ADAM_STEP)

    def body(*refs):
        p_refs = refs[:n_seg]
        w_ref, m_ref, v_ref, g_ref, d_ref, nm_ref, nv_ref = refs[n_seg:]
        seg = pl.program_id(0)
        for k, p_ref in enumerate(p_refs):
            @pl.when(seg == k)
            def _(p_ref=p_ref):
                g = p_ref[0].astype(F32)
                for dev in range(1, N_DEV):
                    g = g + p_ref[dev].astype(F32)
                g_ref[...] = g

        g = g_ref[...]
        nm = ADAM_B1 * m_ref[...] + (1.0 - ADAM_B1) * g
        nv = ADAM_B2 * v_ref[...] + (1.0 - ADAM_B2) * (g * g)
        nm_ref[...] = nm
        nv_ref[...] = nv
        d_ref[...] = -ADAM_LR * ((nm / c1) / (jnp.sqrt(nv / c2) + ADAM_EPS) + ADAM_WD * w_ref[...])

    def part_spec(k):
        return pl.BlockSpec((N_DEV, tr, cols), lambda s, i, k=k: (0, jnp.where(s == k, i, 0), 0))

    blk = pl.BlockSpec((tr, cols), lambda s, i: (s * per_seg + i, 0))
    shp = jax.ShapeDtypeStruct((rows, cols), F32)
    return pl.pallas_call(
        body, name=name, grid=(n_seg, per_seg),
        in_specs=[part_spec(k) for k in range(n_seg)] + [blk, blk, blk],
        out_specs=[blk, blk, blk, blk], out_shape=[shp, shp, shp, shp],
        compiler_params=_cp("arbitrary", "arbitrary"))(*parts, w, m, v)


def _pack(arrays, lead=0):
    parts = []
    for a in arrays:
        f = a.reshape(a.shape[:lead] + (-1, LANES))
        pad = -f.shape[lead] % PACK_ROWS
        if pad:
            f = jnp.pad(f, [(0, 0)] * lead + [(0, pad), (0, 0)])
        parts.append(f)
    return jnp.concatenate(parts, axis=lead)


def _unpack(buf, shapes, lead=0):
    out, r = [], 0
    for shp in shapes:
        n = int(np.prod(shp)) // LANES
        part = lax.slice_in_dim(buf, r, r + n, axis=lead)
        out.append(part.reshape(buf.shape[:lead] + tuple(shp)))
        r += n + (-n % PACK_ROWS)
    return out


REPLICATED = ("lb_logits", "norm_mix", "conv_b", "b_r", "b_i", "lam", "hg_norm", "norm_mlp", "norm_final")
SMALL_SHARDED = ("conv_w", "w_r", "w_i")
LARGE_SHARDED = ("w_in", "w_out", "w_up", "w_down")
WEIGHTS = ("lb_logits", "norm_mix", "w_in", "conv_w", "conv_b", "w_r", "b_r", "w_i", "b_i", "lam", "hg_norm",
           "w_out", "norm_mlp", "w_up", "w_down", "norm_final")


def _matmul_weight_shards(p):
    depth = p["w_in"].shape[0]
    cast = {k: p[k].astype(BF16) for k in LARGE_SHARDED}
    return ([cast["w_in"][l] for l in range(depth)],
            [[cast[k][l] for k in ("w_out", "w_up", "w_down")] for l in range(depth)])


def _gathered_rest(got):
    w_out, w_up, w_down = got
    d = w_out.shape[2]
    return dict(w_out=w_out.reshape(d, d), w_up=w_up, w_down=w_down)


def _unpack_mixer_weights(small, p):
    depth, d, _ = p["w_in"].shape
    n_blk = d // RG_BLOCK_W
    conv_w, w_r, w_i = _unpack(small, [p["conv_w"].shape, p["w_r"].shape, p["w_i"].shape], lead=1)
    conv_w = conv_w.transpose(1, 2, 0, 3).reshape(depth, CONV_W, d)
    w_r = w_r.transpose(1, 2, 0, 3, 4).reshape(depth, n_blk, RG_BLOCK_W, RG_BLOCK_W).astype(BF16)
    w_i = w_i.transpose(1, 2, 0, 3, 4).reshape(depth, n_blk, RG_BLOCK_W, RG_BLOCK_W).astype(BF16)
    return conv_w, w_r, w_i


def _local_step(x, target, p):
    bl, seq, d = x.shape
    depth = p["w_in"].shape[0]
    t_rows = bl * seq
    row = lambda a, l: a[l:l + 1]
    lbs = _lower_bounds_fwd(p["lb_logits"])
    shard_in, shard_rest = _matmul_weight_shards(p)
    w_in = _gather_two_level(shard_in[0], "gather_w_in")
    cur = x.reshape(t_rows, d)
    saved, layers = [], []
    for l in range(depth):
        if l == 0:
            (proj, gates, h), small = _inproj_fwd(cur, row(p["norm_mix"], l), w_in,
                                                  carry=([_pack([p["conv_w"], p["w_r"], p["w_i"]])], True))
            conv_w, w_r, w_i = _unpack_mixer_weights(small[0], p)
        else:
            (proj, gates, h), _ = _inproj_fwd(cur, row(p["norm_mix"], l), w_in)
        w = dict(w_in=w_in, conv_w=conv_w[l], w_r=w_r[l], w_i=w_i[l])
        hs, y_a = _mixer_a_fwd(proj, w["conv_w"], row(p["conv_b"], l), w["w_r"], row(p["b_r"], l), w["w_i"],
                               row(p["b_i"], l), row(p["lam"], l), seq)
        (o, o_n), got = _hgrn_fwd(proj, row(lbs, l), row(p["hg_norm"], l), seq,
                                  carry=(shard_rest[l] + ([shard_in[l + 1]] if l + 1 < depth else []), True))
        w.update(_gathered_rest(got[:3]))
        w_in = got[3] if l + 1 < depth else None
        layers.append(w)
        x_mid, y = _merge_out_fwd(gates, y_a, o_n, cur, w["w_out"])
        x_out, u, h2 = _mlp_fwd(x_mid, row(p["norm_mlp"], l), w["w_up"], w["w_down"])
        saved.append(dict(x_in=cur, proj=proj, gates=gates, h=h, hs=hs, y_a=y_a, o=o, o_n=o_n, x_mid=x_mid, y=y, u=u, h2=h2))
        cur = x_out
    loss8, dx, dxb, g_norm_final = _loss_head(cur, p["norm_final"].reshape(1, d), target.reshape(t_rows, d))
    small = ("norm_mix", "conv_w", "conv_b", "w_r", "b_r", "w_i", "b_i", "lam", "hg_norm", "norm_mlp")
    g = {k: [None] * depth for k in small}
    d_lbs, received = [None] * depth, [None] * depth
    g_w_in = None
    for l in reversed(range(depth)):
        s, w = saved[l], layers[l]
        dx_mid, dx_mid_b, du, act, g["norm_mlp"][l] = _mlp_bwd(dx, dxb, s["u"], s["x_mid"], row(p["norm_mlp"], l),
                                                               w["w_up"], w["w_down"])
        g_w_down = _wgrad(act, dxb[None], "wgrad_down")
        g_w_up = _wgrad(s["h2"][None], du, "wgrad_up")
        d_ya, d_on, dp_c = _outproj_bwd(dx_mid_b, w["w_out"], s["gates"], s["y_a"], s["o_n"])
        g_w_out = _wgrad(s["y"][None], dx_mid_b[None], "wgrad_out").reshape(N_DEV, d // N_DEV, d)
        (dp_b, d_lbs[l], g["hg_norm"][l]), got = _hgrn_bwd(
            s["proj"], row(lbs, l), row(p["hg_norm"], l), s["o"], d_on, seq,
            carry=([g_w_out, g_w_up, g_w_down] + ([g_w_in] if g_w_in is not None else []), False))
        received[l] = [None] + list(got[:3])
        if g_w_in is not None:
            received[l + 1][0] = got[3]
        (dp_a, g["w_r"][l], g["w_i"][l], g["b_r"][l], g["b_i"][l], g["lam"][l], g["conv_w"][l],
         g["conv_b"][l]) = _mixer_a_bwd(s["proj"], s["hs"], d_ya, w["conv_w"], row(p["conv_b"], l), w["w_r"],
                                        row(p["b_r"], l), w["w_i"], row(p["b_i"], l), row(p["lam"], l), seq)
        hb = s["h"][None]
        g_w_in = jnp.concatenate([_wgrad(hb, dp_a, "wgrad_in_pair"), _wgrad(hb, dp_b, "wgrad_in_triple"),
                                  _wgrad(hb, dp_c, "wgrad_in_triple")], axis=0)
        carry = None
        if l == 0:
            carry = ([g_w_in, _mixer_grads_by_owner(g, d)], False)
        (dx, dxb, g["norm_mix"][l]), got = _inproj_bwd(dx_mid, dp_a, dp_b, dp_c, w["w_in"], s["x_in"],
                                                       row(p["norm_mix"], l), carry=carry)
    received[0][0], received_mixer = got
    grads = {k: jnp.stack(v) for k, v in g.items()}
    for k in ("norm_mix", "conv_b", "b_r", "b_i", "lam", "hg_norm", "norm_mlp"):
        grads[k] = grads[k][:, 0]
    grads["lb_logits"] = _lower_bounds_bwd(p["lb_logits"], jnp.concatenate(d_lbs, axis=0))
    grads["norm_final"] = g_norm_final[0]
    return loss8[0, 0], dx.reshape(bl, seq, d), grads, received, received_mixer


def _mixer_grads_by_owner(g, d):
    d8, n_blk, rb = d // N_DEV, d // RG_BLOCK_W, RG_BLOCK_W // N_DEV
    depth = len(g["conv_w"])
    conv_w, w_r, w_i = (jnp.stack(g[k]) for k in SMALL_SHARDED)
    return _pack([conv_w.reshape(depth, CONV_W, N_DEV, d8).transpose(2, 0, 1, 3),
                  w_r.reshape(depth, n_blk, N_DEV, rb, RG_BLOCK_W).transpose(2, 0, 1, 3, 4),
                  w_i.reshape(depth, n_blk, N_DEV, rb, RG_BLOCK_W).transpose(2, 0, 1, 3, 4)],
                 lead=1).astype(BF16)


def _update(p, mom1, mom2, grads, received, received_mixer):
    depth = p["w_in"].shape[0]
    out = {}

    for i, k in enumerate(LARGE_SHARDED):
        shp = p[k].shape
        flat = lambda a: a.reshape(shp[0] * shp[1], shp[2])
        parts = [received[l][i] for l in range(depth)]
        res = _reduce_adamw(parts, flat(p[k]), flat(mom1[k]), flat(mom2[k]), "adamw_" + k)
        out[k] = [r.reshape(shp) for r in res]

    res = _reduce_adamw([received_mixer], *[_pack([src[k] for k in SMALL_SHARDED]) for src in (p, mom1, mom2)],
                        "adamw_mixer")
    shapes = [p[k].shape for k in SMALL_SHARDED]
    for i, vals in enumerate(zip(*[_unpack(r, shapes) for r in res])):
        out[SMALL_SHARDED[i]] = list(vals)

    parts = _exchange([_pack([grads[k] for k in REPLICATED])], "gather_grad_replicated", gather=True)
    res = _reduce_adamw(parts, *[_pack([src[k] for k in REPLICATED]) for src in (p, mom1, mom2)],
                        "adamw_replicated")
    shapes = [p[k].shape for k in REPLICATED]
    for i, vals in enumerate(zip(*[_unpack(r, shapes) for r in res])):
        out[REPLICATED[i]] = list(vals)

    return tuple(out[k][i] for i in range(4) for k in WEIGHTS)


def kernel(x, lb_logits, norm_mix, w_in, conv_w, conv_b, w_r, b_r, w_i, b_i, lam, hg_norm, w_out, norm_mlp, w_up, w_down, norm_final, loss_target, m_lb_logits, m_norm_mix, m_w_in, m_conv_w, m_conv_b, m_w_r, m_b_r, m_w_i, m_b_i, m_lam, m_hg_norm, m_w_out, m_norm_mlp, m_w_up, m_w_down, m_norm_final, v_lb_logits, v_norm_mix, v_w_in, v_conv_w, v_conv_b, v_w_r, v_b_r, v_w_i, v_b_i, v_lam, v_hg_norm, v_w_out, v_norm_mlp, v_w_up, v_w_down, v_norm_final):
    p = dict(lb_logits=lb_logits, norm_mix=norm_mix, w_in=w_in, conv_w=conv_w, conv_b=conv_b, w_r=w_r, b_r=b_r,
             w_i=w_i, b_i=b_i, lam=lam, hg_norm=hg_norm, w_out=w_out, norm_mlp=norm_mlp, w_up=w_up,
             w_down=w_down, norm_final=norm_final)
    mom1 = dict(lb_logits=m_lb_logits, norm_mix=m_norm_mix, w_in=m_w_in, conv_w=m_conv_w, conv_b=m_conv_b,
                w_r=m_w_r, b_r=m_b_r, w_i=m_w_i, b_i=m_b_i, lam=m_lam, hg_norm=m_hg_norm, w_out=m_w_out,
                norm_mlp=m_norm_mlp, w_up=m_w_up, w_down=m_w_down, norm_final=m_norm_final)
    mom2 = dict(lb_logits=v_lb_logits, norm_mix=v_norm_mix, w_in=v_w_in, conv_w=v_conv_w, conv_b=v_conv_b,
                w_r=v_w_r, b_r=v_b_r, w_i=v_w_i, b_i=v_b_i, lam=v_lam, hg_norm=v_hg_norm, w_out=v_w_out,
                norm_mlp=v_norm_mlp, w_up=v_w_up, w_down=v_w_down, norm_final=v_norm_final)
    loss, grad_x, grads, received, received_mixer = _local_step(x, loss_target, p)
    loss = lax.psum(loss, ("x", "y", "c"))
    return (loss, grad_x) + _update(p, mom1, mom2, grads, received, received_mixer)
```

```python
import numpy as np

import jax
import jax.numpy as jnp
from jax import lax
from jax.experimental import pallas as pl
from jax.experimental.pallas import tpu as pltpu

F32 = jnp.float32
BF16 = jnp.bfloat16
MESH_ID = pl.DeviceIdType.MESH

N_DEV = 8
N_MIXER_SEGMENTS = 5
NORM_EPS = 1e-6
RG_C = 8.0
RG_BLOCK_W = 256
CONV_W = 4
HG_DK = 128
F_MIN = 1e-30
HG_CHUNK = 16
SUBLANES = 8
LANES = 128
PACK_ROWS = 16
SCAN_GROUP = 16
ROW_CHUNK = 256
ROW_TILE_WEIGHT_STREAM = 1024
WGRAD_TOKEN_TILE = 2048
VMEM_LIMIT_V7X = 56 * 1024 * 1024

ADAM_LR = 0.001
ADAM_B1 = 0.9
ADAM_B2 = 0.999
ADAM_EPS = 1e-08
ADAM_WD = 0.01
ADAM_STEP = 10

GELU_C = 0.7978845608028654
GELU_K = 0.044715


def _cp(*sem):
    return pltpu.CompilerParams(dimension_semantics=sem, vmem_limit_bytes=VMEM_LIMIT_V7X)


def _row_tile(n, cap):
    if n <= cap:
        return n
    t = cap - cap % 16
    while n % t:
        t -= 16
    return t


def _dot(a, b):
    return jnp.dot(a, b, preferred_element_type=F32)


def _dot_nt(a, b):
    return lax.dot_general(a, b, (((1,), (1,)), ((), ())), preferred_element_type=F32)


def _dot_tn(a, b):
    return lax.dot_general(a, b, (((0,), (0,)), ((), ())), preferred_element_type=F32)


def _sigmoid(x):
    return jax.nn.sigmoid(x)


def _sigmoid_pair(x):
    e = jnp.exp(-jnp.abs(x))
    r = 1.0 / (1.0 + e)
    er = e * r
    pos = x >= 0.0
    return jnp.where(pos, r, er), jnp.where(pos, er, r)


def _log1p_pos(y):
    return jnp.where(y < 0.01, y * (1.0 - y * (0.5 - y * (1.0 / 3.0))), jnp.log(1.0 + y))


def _softplus(x):
    return jnp.maximum(x, 0.0) + _log1p_pos(jnp.exp(-jnp.abs(x)))


def _one_minus_exp(x):
    series = -x * (1.0 + x * 0.5 * (1.0 + x * (1.0 / 3.0) * (1.0 + x * 0.25 * (1.0 + x * 0.2))))
    return jnp.where(x > -0.1, series, 1.0 - jnp.exp(x))


def _gelu_and_grad(x):
    x2 = x * x
    t = jnp.tanh(GELU_C * x * (1.0 + GELU_K * x2))
    g = 0.5 * x * (1.0 + t)
    dg = 0.5 * (1.0 + t) + 0.5 * x * (1.0 - t * t) * GELU_C * (1.0 + 3.0 * GELU_K * x2)
    return g, dg


def _silu_and_grad(x):
    s = _sigmoid(x)
    return x * s, s * (1.0 + x * (1.0 - s))


def _rstd(x):
    return lax.rsqrt(jnp.mean(x * x, axis=-1, keepdims=True) + NORM_EPS)


def _rms_bwd(dh, x, g):
    rstd = _rstd(x)
    xh = x * rstd
    dxh = dh * g
    dx = rstd * (dxh - xh * jnp.mean(dxh * xh, axis=-1, keepdims=True))
    return dx, jnp.sum(dh * xh, axis=0, keepdims=True)


def _shift_rows(x, k):
    n = x.shape[0]
    k = k % n
    return x if k == 0 else pltpu.roll(x, k, axis=0)


def _seg_cumsum(x, seg, reverse=False):
    n = x.shape[0]
    rid = lax.broadcasted_iota(jnp.int32, x.shape, 0) & (seg - 1)
    d = 1
    while d < seg:
        if reverse:
            x = jnp.where(rid < seg - d, x + _shift_rows(x, n - d), x)
        else:
            x = jnp.where(rid >= d, x + _shift_rows(x, d), x)
        d *= 2
    return x


def _group_cumsum_matrix(n, seg):
    row = lax.broadcasted_iota(jnp.int32, (n, n), 0)
    col = lax.broadcasted_iota(jnp.int32, (n, n), 1)
    same_group = (row & ~(seg - 1)) == (col & ~(seg - 1))
    return jnp.where(same_group & (col <= row), 1.0, 0.0).astype(BF16)


def _group_cumsum_mxu(x, tri):
    hi = x.astype(BF16)
    lo = (x - hi.astype(F32)).astype(BF16)
    return _dot(tri, hi) + _dot(tri, lo)


def _scan_rows(a_ref, b_ref, out_ref, n_rows, width, reverse):
    gr = min(SCAN_GROUP, n_rows)
    rid = lax.broadcasted_iota(jnp.int32, (gr, width), 0)
    n_groups = n_rows // gr
    per_trip = min(4, n_groups)
    assert n_groups % per_trip == 0

    def local_scan(g):
        r0 = pl.multiple_of(g * gr, gr)
        a = a_ref[pl.ds(r0, gr), :]
        b = b_ref[pl.ds(r0, gr), :]
        d = 1
        while d < gr:
            if reverse:
                keep = rid < gr - d
                a_sh, b_sh = _shift_rows(a, gr - d), _shift_rows(b, gr - d)
            else:
                keep = rid >= d
                a_sh, b_sh = _shift_rows(a, d), _shift_rows(b, d)
            b = jnp.where(keep, a * b_sh + b, b)
            a = jnp.where(keep, a * a_sh, a)
            d *= 2
        return r0, a, b

    def trip(i, carry):
        first = i * per_trip
        groups = [n_groups - 1 - (first + u) if reverse else first + u for u in range(per_trip)]
        for r0, a, b in [local_scan(g) for g in groups]:
            out = a * carry + b
            out_ref[pl.ds(r0, gr), :] = out
            edge = out[0:1, :] if reverse else out[gr - 1:gr, :]
            carry = jnp.broadcast_to(edge, (gr, width))
        return carry

    lax.fori_loop(0, n_groups // per_trip, trip, jnp.zeros((gr, width), F32))


def _lb_softmax_rows(x_ref, depth):
    rows = [x_ref[pl.ds(l, 1), :] for l in range(depth)]
    top = rows[0]
    for r in rows[1:]:
        top = jnp.maximum(top, r)
    e = [jnp.exp(r - top) for r in rows]
    tot = e[0]
    for r in e[1:]:
        tot = tot + r
    return [r / tot for r in e]


def _lower_bounds_fwd(lb_logits):
    depth, d = lb_logits.shape

    def body(x_ref, o_ref):
        sm = _lb_softmax_rows(x_ref, depth)
        cum = jnp.zeros((1, d), F32)
        for l in range(depth):
            cum = cum + sm[l]
            o_ref[pl.ds(l, 1), :] = jnp.clip(cum - sm[0], 0.0, 1.0)

    return pl.pallas_call(body, name="lower_bounds_fwd",
                          out_shape=jax.ShapeDtypeStruct((depth, d), F32))(lb_logits)


def _lower_bounds_bwd(lb_logits, d_lbs):
    depth, d = lb_logits.shape

    def body(x_ref, g_ref, o_ref):
        sm = _lb_softmax_rows(x_ref, depth)
        cum = jnp.zeros((1, d), F32)
        d_cum = []
        for l in range(depth):
            cum = cum + sm[l]
            v = cum - sm[0]
            d_cum.append(jnp.where((v > 0.0) & (v < 1.0), g_ref[pl.ds(l, 1), :], 0.0))
        d_sm = []
        tail = jnp.zeros((1, d), F32)
        for l in reversed(range(depth)):
            tail = tail + d_cum[l]
            d_sm.append(tail)
        d_sm = d_sm[::-1]
        d_sm[0] = d_sm[0] - tail
        inner = jnp.zeros((1, d), F32)
        for l in range(depth):
            inner = inner + sm[l] * d_sm[l]
        for l in range(depth):
            o_ref[pl.ds(l, 1), :] = sm[l] * (d_sm[l] - inner)

    return pl.pallas_call(body, name="lower_bounds_bwd",
                          out_shape=jax.ShapeDtypeStruct((depth, d), F32))(lb_logits, d_lbs)


def _inproj_fwd(x, gain, w_seg, carry=None):
    t_rows, d = x.shape
    tm = _row_tile(t_rows, ROW_TILE_WEIGHT_STREAM)
    n_gate = N_DEV - N_MIXER_SEGMENTS

    def body(x_ref, g_ref, w_ref, proj_ref, gates_ref, h_ref):
        j = pl.program_id(1)

        @pl.when(j == 0)
        def _():
            xv = x_ref[...]
            h_ref[...] = (xv * _rstd(xv) * g_ref[...]).astype(BF16)

        @pl.when(j < N_MIXER_SEGMENTS)
        def _():
            proj_ref[...] = _dot(h_ref[...], w_ref[...])

        @pl.when(j >= N_MIXER_SEGMENTS)
        def _():
            gates_ref[...] = _dot(h_ref[...], w_ref[...]).astype(BF16)

    return _call_carrying(
        body, carry, name="inproj_fwd", grid=(t_rows // tm, N_DEV),
        in_specs=[pl.BlockSpec((tm, d), lambda i, j: (i, 0)),
                  pl.BlockSpec((1, d), lambda i, j: (0, 0)),
                  pl.BlockSpec((None, d, d), lambda i, j: (j, 0, 0))],
        out_specs=[pl.BlockSpec((None, tm, d), lambda i, j: (jnp.minimum(j, N_MIXER_SEGMENTS - 1), i, 0)),
                   pl.BlockSpec((None, tm, d), lambda i, j: (jnp.maximum(j - N_MIXER_SEGMENTS, 0), i, 0)),
                   pl.BlockSpec((tm, d), lambda i, j: (i, 0))],
        out_shape=[jax.ShapeDtypeStruct((N_MIXER_SEGMENTS, t_rows, d), F32),
                   jax.ShapeDtypeStruct((n_gate, t_rows, d), BF16),
                   jax.ShapeDtypeStruct((t_rows, d), BF16)],
        scratch_shapes=[], semantics=("parallel", "arbitrary"), args=(x, gain, w_seg))


def _merge_out_fwd(gates, y_a, o_n, x, w_out):
    t_rows, d = x.shape
    tm = _row_tile(t_rows, 256)

    def body(g_ref, ma_ref, mb_ref, ya_ref, on_ref, x_ref, w_ref, xmid_ref, y_ref):
        g = g_ref[...].astype(F32)
        ya, on = ya_ref[...].astype(F32), on_ref[...].astype(F32)
        y = (_sigmoid(ma_ref[...].astype(F32)) * ya
             + _sigmoid(mb_ref[...].astype(F32)) * (on * (g * _sigmoid(g))))
        yb = y.astype(BF16)
        y_ref[...] = yb
        xmid_ref[...] = x_ref[...] + _dot(yb, w_ref[...])

    seg = lambda k: pl.BlockSpec((None, tm, d), lambda i, k=k: (k, i, 0))
    row = pl.BlockSpec((tm, d), lambda i: (i, 0))
    return pl.pallas_call(
        body, name="merge_out_fwd", grid=(t_rows // tm,),
        in_specs=[seg(0), seg(1), seg(2), row, row, row, pl.BlockSpec((d, d), lambda i: (0, 0))],
        out_specs=[row, row],
        out_shape=[jax.ShapeDtypeStruct((t_rows, d), F32), jax.ShapeDtypeStruct((t_rows, d), BF16)],
        compiler_params=_cp("parallel"))(gates, gates, gates, y_a, o_n, x, w_out)


def _mlp_fwd(x_mid, gain, w_up, w_down):
    t_rows, d = x_mid.shape
    f8 = w_up.shape[2]
    tm = _row_tile(t_rows, ROW_TILE_WEIGHT_STREAM)
    per_step = 2

    def body(x_ref, g_ref, wu_ref, wd_ref, out_ref, u_ref, h_ref):
        @pl.when(pl.program_id(1) == 0)
        def _():
            xv = x_ref[...]
            h_ref[...] = (xv * _rstd(xv) * g_ref[...]).astype(BF16)
            out_ref[...] = xv

        h = h_ref[...]
        down = None
        for k in range(per_step):
            u = _dot(h, wu_ref[k])
            u_ref[k] = u.astype(BF16)
            r = jnp.maximum(u, 0.0)
            part = _dot((r * r).astype(BF16), wd_ref[k])
            down = part if down is None else down + part
        out_ref[...] += down

    row = pl.BlockSpec((tm, d), lambda i, j: (i, 0))
    return pl.pallas_call(
        body, name="mlp_fwd", grid=(t_rows // tm, N_DEV // per_step),
        in_specs=[row, pl.BlockSpec((1, d), lambda i, j: (0, 0)),
                  pl.BlockSpec((per_step, d, f8), lambda i, j: (j, 0, 0)),
                  pl.BlockSpec((per_step, f8, d), lambda i, j: (j, 0, 0))],
        out_specs=[row, pl.BlockSpec((per_step, tm, f8), lambda i, j: (j, i, 0)), row],
        out_shape=[jax.ShapeDtypeStruct((t_rows, d), F32),
                   jax.ShapeDtypeStruct((N_DEV, t_rows, f8), BF16),
                   jax.ShapeDtypeStruct((t_rows, d), BF16)],
        compiler_params=_cp("parallel", "arbitrary"))(x_mid, gain, w_up, w_down)


def _loss_head(x, gain, target):
    t_rows, d = x.shape
    tm = _row_tile(t_rows, 512)

    def body(x_ref, g_ref, t_ref, loss_ref, dx_ref, dxb_ref, dg_ref):
        @pl.when(pl.program_id(0) == 0)
        def _():
            loss_ref[...] = jnp.zeros_like(loss_ref)
            dg_ref[...] = jnp.zeros_like(dg_ref)

        xv = x_ref[...]
        g = g_ref[...]
        err = xv * _rstd(xv) * g - t_ref[...]
        loss_ref[...] += (0.5 / d) * jnp.sum(err * err)
        dx, dg = _rms_bwd(err * (1.0 / d), xv, g)
        dx_ref[...] = dx
        dxb_ref[...] = dx.astype(BF16)
        dg_ref[...] += dg

    row = pl.BlockSpec((tm, d), lambda i: (i, 0))
    vec = pl.BlockSpec((1, d), lambda i: (0, 0))
    return pl.pallas_call(
        body, name="loss_head", grid=(t_rows // tm,),
        in_specs=[row, vec, row],
        out_specs=[pl.BlockSpec((SUBLANES, LANES), lambda i: (0, 0)), row, row, vec],
        out_shape=[jax.ShapeDtypeStruct((SUBLANES, LANES), F32),
                   jax.ShapeDtypeStruct((t_rows, d), F32),
                   jax.ShapeDtypeStruct((t_rows, d), BF16),
                   jax.ShapeDtypeStruct((1, d), F32)],
        compiler_params=_cp("arbitrary"))(x, gain, target)


def _mlp_bwd(d_out, d_out_b, u, x_mid, gain, w_up, w_down):
    t_rows, d = x_mid.shape
    f8 = w_up.shape[2]
    tm = _row_tile(t_rows, ROW_TILE_WEIGHT_STREAM)
    sub = _row_tile(tm, ROW_CHUNK)

    def body(do_ref, dob_ref, u_ref, x_ref, g_ref, wu_ref, wd_ref, dx_ref, dxb_ref, du_ref, act_ref, dg_ref):
        j = pl.program_id(1)

        @pl.when((pl.program_id(0) == 0) & (j == 0))
        def _():
            dg_ref[...] = jnp.zeros_like(dg_ref)

        @pl.when(j == 0)
        def _():
            dx_ref[...] = jnp.zeros_like(dx_ref)

        r = jnp.maximum(u_ref[...].astype(F32), 0.0)
        act_ref[...] = (r * r).astype(BF16)
        du = (_dot_nt(dob_ref[...], wd_ref[...]) * (2.0 * r)).astype(BF16)
        du_ref[...] = du
        dx_ref[...] += _dot_nt(du, wu_ref[...])

        @pl.when(j == N_DEV - 1)
        def _():
            def finish(c, _):
                rows = pl.ds(pl.multiple_of(c * sub, sub), sub)
                dx, dg = _rms_bwd(dx_ref[rows, :], x_ref[rows, :], g_ref[...])
                dx = dx + do_ref[rows, :]
                dx_ref[rows, :] = dx
                dxb_ref[rows, :] = dx.astype(BF16)
                dg_ref[...] += dg
                return 0

            lax.fori_loop(0, tm // sub, finish, 0)

    row = pl.BlockSpec((tm, d), lambda i, j: (i, 0))
    vec = pl.BlockSpec((1, d), lambda i, j: (0, 0))
    hid = pl.BlockSpec((None, tm, f8), lambda i, j: (j, i, 0))
    return pl.pallas_call(
        body, name="mlp_bwd", grid=(t_rows // tm, N_DEV),
        in_specs=[row, row, hid, row, vec,
                  pl.BlockSpec((None, d, f8), lambda i, j: (j, 0, 0)),
                  pl.BlockSpec((None, f8, d), lambda i, j: (j, 0, 0))],
        out_specs=[row, row, hid, hid, vec],
        out_shape=[jax.ShapeDtypeStruct((t_rows, d), F32),
                   jax.ShapeDtypeStruct((t_rows, d), BF16),
                   jax.ShapeDtypeStruct((N_DEV, t_rows, f8), BF16),
                   jax.ShapeDtypeStruct((N_DEV, t_rows, f8), BF16),
                   jax.ShapeDtypeStruct((1, d), F32)],
        compiler_params=_cp("arbitrary", "arbitrary"))(d_out, d_out_b, u, x_mid, gain, w_up, w_down)


def _outproj_bwd(dx_mid_b, w_out, gates, y_a, o_n):
    t_rows, d = y_a.shape
    tm = _row_tile(t_rows, 256)

    def body(dx_ref, w_ref, g_ref, ma_ref, mb_ref, ya_ref, on_ref, dya_ref, don_ref, dp_ref):
        dy = _dot_nt(dx_ref[...], w_ref[...])
        sa = _sigmoid(ma_ref[...].astype(F32))
        sb = _sigmoid(mb_ref[...].astype(F32))
        sg, dsg = _silu_and_grad(g_ref[...].astype(F32))
        ya = ya_ref[...].astype(F32)
        on = on_ref[...].astype(F32)
        dya_ref[...] = dy * sa
        t = dy * sb
        don_ref[...] = t * sg
        dp_ref[0] = (t * on * dsg).astype(BF16)
        dp_ref[1] = (dy * ya * sa * (1.0 - sa)).astype(BF16)
        dp_ref[2] = (dy * on * sg * sb * (1.0 - sb)).astype(BF16)

    seg = lambda k: pl.BlockSpec((None, tm, d), lambda i, k=k: (k, i, 0))
    row = pl.BlockSpec((tm, d), lambda i: (i, 0))
    return pl.pallas_call(
        body, name="outproj_bwd", grid=(t_rows // tm,),
        in_specs=[row, pl.BlockSpec((d, d), lambda i: (0, 0)), seg(0), seg(1), seg(2), row, row],
        out_specs=[row, row, pl.BlockSpec((3, tm, d), lambda i: (0, i, 0))],
        out_shape=[jax.ShapeDtypeStruct((t_rows, d), F32),
                   jax.ShapeDtypeStruct((t_rows, d), F32),
                   jax.ShapeDtypeStruct((3, t_rows, d), BF16)],
        compiler_params=_cp("parallel"))(dx_mid_b, w_out, gates, gates, gates, y_a, o_n)


def _inproj_bwd(dx_mid, dp_a, dp_b, dp_c, w_seg, x_in, gain, carry=None):
    t_rows, d = x_in.shape
    tm = _row_tile(t_rows, ROW_TILE_WEIGHT_STREAM)
    sub = _row_tile(tm, ROW_CHUNK)
    n_a, n_b = dp_a.shape[0], dp_b.shape[0]

    def body(dxm_ref, a_ref, b_ref, c_ref, w_ref, x_ref, g_ref, dx_ref, dxb_ref, dg_ref):
        j = pl.program_id(1)

        @pl.when((pl.program_id(0) == 0) & (j == 0))
        def _():
            dg_ref[...] = jnp.zeros_like(dg_ref)

        @pl.when(j == 0)
        def _():
            dx_ref[...] = jnp.zeros_like(dx_ref)

        @pl.when(j < n_a)
        def _():
            dx_ref[...] += _dot_nt(a_ref[...], w_ref[...])

        @pl.when((j >= n_a) & (j < n_a + n_b))
        def _():
            dx_ref[...] += _dot_nt(b_ref[...], w_ref[...])

        @pl.when(j >= n_a + n_b)
        def _():
            dx_ref[...] += _dot_nt(c_ref[...], w_ref[...])

        @pl.when(j == N_DEV - 1)
        def _():
            def finish(c, _):
                rows = pl.ds(pl.multiple_of(c * sub, sub), sub)
                dx, dg = _rms_bwd(dx_ref[rows, :], x_ref[rows, :], g_ref[...])
                dx = dx + dxm_ref[rows, :]
                dx_ref[rows, :] = dx
                dxb_ref[rows, :] = dx.astype(BF16)
                dg_ref[...] += dg
                return 0

            lax.fori_loop(0, tm // sub, finish, 0)

    def part(first, n):
        return pl.BlockSpec((None, tm, d), lambda i, j: (jnp.clip(j - first, 0, n - 1), i, 0))

    row = pl.BlockSpec((tm, d), lambda i, j: (i, 0))
    vec = pl.BlockSpec((1, d), lambda i, j: (0, 0))
    return _call_carrying(
        body, carry, name="inproj_bwd", grid=(t_rows // tm, N_DEV),
        in_specs=[row, part(0, n_a), part(n_a, n_b), part(n_a + n_b, dp_c.shape[0]),
                  pl.BlockSpec((None, d, d), lambda i, j: (j, 0, 0)), row, vec],
        out_specs=[row, row, vec],
        out_shape=[jax.ShapeDtypeStruct((t_rows, d), F32),
                   jax.ShapeDtypeStruct((t_rows, d), BF16),
                   jax.ShapeDtypeStruct((1, d), F32)],
        scratch_shapes=[], semantics=("arbitrary", "arbitrary"),
        args=(dx_mid, dp_a, dp_b, dp_c, w_seg, x_in, gain))


def _wgrad(a3, b3, name):
    n_a, t_rows, k_a = a3.shape
    n_b, _, n_cols = b3.shape
    n = max(n_a, n_b)
    bk = _row_tile(k_a, 1024)
    bn = n_cols if n_cols <= 1024 else 1024
    tt = _row_tile(t_rows, WGRAD_TOKEN_TILE)
    n_t = t_rows // tt

    def body(a_ref, b_ref, o_ref, acc_ref):
        t, j = pl.program_id(2), pl.program_id(3)
        part = _dot_tn(a_ref[...], b_ref[...])

        @pl.when(t == 0)
        def _():
            acc_ref[j] = part

        @pl.when(t > 0)
        def _():
            acc_ref[j] += part

        @pl.when(t == n_t - 1)
        def _():
            o_ref[...] = acc_ref[j].astype(BF16)

    def out_map(p, q, t, j):
        return (jnp.where(t == n_t - 1, j, 0), p, q)

    return pl.pallas_call(
        body, name=name, grid=(k_a // bk, n_cols // bn, n_t, n),
        in_specs=[pl.BlockSpec((None, tt, bk), lambda p, q, t, j: (j if n_a > 1 else 0, t, p)),
                  pl.BlockSpec((None, tt, bn), lambda p, q, t, j: (j if n_b > 1 else 0, t, q))],
        out_specs=pl.BlockSpec((None, bk, bn), out_map),
        out_shape=jax.ShapeDtypeStruct((n, k_a, n_cols), BF16),
        scratch_shapes=[pltpu.VMEM((n, bk, bn), F32)],
        compiler_params=_cp("parallel", "parallel", "arbitrary", "arbitrary"))(a3, b3)


def _conv_taps(xe, n):
    return [_shift_rows(xe, CONV_W - 1 - j)[SUBLANES:SUBLANES + n, :] for j in range(CONV_W)]


def _rg_gates(xc, w_r, b_r, w_i, b_i, sp8):
    xb = xc.astype(BF16)
    r = _sigmoid(_dot(xb, w_r) + b_r)
    i = _sigmoid(_dot(xb, w_i) + b_i)
    return r, i


def _mixer_a_fwd(proj, conv_w, conv_b, w_r, b_r, w_i, b_i, lam, seq):
    _, t_rows, d = proj.shape
    n_seq, n_blk = t_rows // seq, d // RG_BLOCK_W
    wb = RG_BLOCK_W
    ch = _row_tile(seq, ROW_CHUNK)

    def body(xa_ref, ga_ref, cw_ref, cb_ref, wr_ref, br_ref, wi_ref, bi_ref, lam_ref, h_ref, ya_ref,
             xpad, a_s, u_s):
        xpad[0:SUBLANES, :] = jnp.zeros((SUBLANES, wb), F32)
        xpad[SUBLANES:, :] = xa_ref[...]
        sp8 = RG_C * _softplus(-lam_ref[...])

        def gates(c, _):
            r0 = pl.multiple_of(c * ch, ch)
            taps = _conv_taps(xpad[pl.ds(r0, ch + SUBLANES), :], ch)
            xc = cb_ref[...] + sum(cw_ref[pl.ds(j, 1), :] * taps[j] for j in range(CONV_W))
            r, i = _rg_gates(xc, wr_ref[...], br_ref[...], wi_ref[...], bi_ref[...], sp8)
            log_a = -(r * sp8)
            a_s[pl.ds(r0, ch), :] = jnp.exp(log_a)
            u_s[pl.ds(r0, ch), :] = jnp.sqrt(jnp.maximum(_one_minus_exp(2.0 * log_a), 0.0)) * (i * xc)
            return 0

        lax.fori_loop(0, seq // ch, gates, 0)
        _scan_rows(a_s, u_s, h_ref, seq, wb, reverse=False)

        def gate_out(c, _):
            r0 = pl.multiple_of(c * ch, ch)
            gl, _ = _gelu_and_grad(ga_ref[pl.ds(r0, ch), :])
            ya_ref[pl.ds(r0, ch), :] = (h_ref[pl.ds(r0, ch), :] * gl).astype(BF16)
            return 0

        lax.fori_loop(0, seq // ch, gate_out, 0)

    seg = lambda k: pl.BlockSpec((None, seq, wb), lambda s, b, k=k: (k, s, b))
    blk = pl.BlockSpec((seq, wb), lambda s, b: (s, b))
    vec = pl.BlockSpec((1, wb), lambda s, b: (0, b))
    wsp = pl.BlockSpec((None, wb, wb), lambda s, b: (b, 0, 0))
    return pl.pallas_call(
        body, name="mixer_a_fwd", grid=(n_seq, n_blk),
        in_specs=[seg(0), seg(1), pl.BlockSpec((CONV_W, wb), lambda s, b: (0, b)), vec, wsp, vec, wsp, vec, vec],
        out_specs=[blk, blk],
        out_shape=[jax.ShapeDtypeStruct((t_rows, d), F32), jax.ShapeDtypeStruct((t_rows, d), BF16)],
        scratch_shapes=[pltpu.VMEM((seq + SUBLANES, wb), F32), pltpu.VMEM((seq, wb), F32),
                        pltpu.VMEM((seq, wb), F32)],
        compiler_params=_cp("parallel", "parallel"))(proj, proj, conv_w, conv_b, w_r, b_r, w_i, b_i, lam)


def _mixer_a_bwd(proj, h, d_ya, conv_w, conv_b, w_r, b_r, w_i, b_i, lam, seq):
    _, t_rows, d = proj.shape
    n_seq, n_blk = t_rows // seq, d // RG_BLOCK_W
    wb = RG_BLOCK_W
    ch = _row_tile(seq, ROW_CHUNK)
    n_ch = seq // ch

    def body(xa_ref, ga_ref, h_ref, dya_ref, cw_ref, cb_ref, wr_ref, br_ref, wi_ref, bi_ref, lam_ref,
             dp_ref, dwr_ref, dwi_ref, dbr_ref, dbi_ref, dlam_ref, dcw_ref, dcb_ref,
             xpad, hpad, a_s, e_pad, g_s, xc_s, r_s, i_s, dxc_pad):
        @pl.when(pl.program_id(1) == 0)
        def _():
            for ref in (dwr_ref, dwi_ref, dbr_ref, dbi_ref, dlam_ref, dcw_ref, dcb_ref):
                ref[...] = jnp.zeros_like(ref)

        zeros8 = jnp.zeros((SUBLANES, wb), F32)
        xpad[0:SUBLANES, :] = zeros8
        xpad[SUBLANES:, :] = xa_ref[...]
        hpad[0:SUBLANES, :] = zeros8
        hpad[SUBLANES:, :] = h_ref[...]
        e_pad[seq:, :] = zeros8
        dxc_pad[seq:, :] = zeros8
        lam_v = lam_ref[...]
        sp8 = RG_C * _softplus(-lam_v)

        def recompute(c, _):
            r0 = pl.multiple_of(c * ch, ch)
            rows = pl.ds(r0, ch)
            taps = _conv_taps(xpad[pl.ds(r0, ch + SUBLANES), :], ch)
            xc = cb_ref[...] + sum(cw_ref[pl.ds(j, 1), :] * taps[j] for j in range(CONV_W))
            r, i = _rg_gates(xc, wr_ref[...], br_ref[...], wi_ref[...], bi_ref[...], sp8)
            a = jnp.exp(-(r * sp8))
            gl, dgl = _gelu_and_grad(ga_ref[rows, :])
            dya = dya_ref[rows, :]
            g = dya * gl
            dp_ref[1, rows, :] = (dya * h_ref[rows, :] * dgl).astype(BF16)
            a_s[rows, :] = a
            e_pad[rows, :] = a * g
            g_s[rows, :] = g
            xc_s[rows, :] = xc
            r_s[rows, :] = r
            i_s[rows, :] = i
            return 0

        lax.fori_loop(0, n_ch, recompute, 0)
        _scan_rows(a_s, e_pad, e_pad, seq, wb, reverse=True)

        def grads(c, _):
            r0 = pl.multiple_of(c * ch, ch)
            rows = pl.ds(r0, ch)
            halo = pl.ds(r0, ch + SUBLANES)
            dh = g_s[rows, :] + _shift_rows(e_pad[halo, :], ch + SUBLANES - 1)[0:ch, :]
            h_prev = _shift_rows(hpad[halo, :], 1)[SUBLANES:, :]
            xc, r, i = xc_s[rows, :], r_s[rows, :], i_s[rows, :]
            log_a = -(r * sp8)
            a = jnp.exp(log_a)
            om = _one_minus_exp(2.0 * log_a)
            sq = jnp.sqrt(jnp.maximum(om, 0.0))
            t1 = dh * xc
            d_i = t1 * sq
            d_la = dh * h_prev * a + jnp.where(om > 0.0, -(t1 * i) * (1.0 - om) / sq, 0.0)
            dpr = -(d_la * sp8) * r * (1.0 - r)
            dpi = d_i * i * (1.0 - i)
            dprb, dpib, xb = dpr.astype(BF16), dpi.astype(BF16), xc.astype(BF16)
            dxc = dh * sq * i + _dot_nt(dprb, wr_ref[...]) + _dot_nt(dpib, wi_ref[...])
            dwr_ref[...] += _dot_tn(xb, dprb)
            dwi_ref[...] += _dot_tn(xb, dpib)
            dbr_ref[...] += jnp.sum(dpr, axis=0, keepdims=True)
            dbi_ref[...] += jnp.sum(dpi, axis=0, keepdims=True)
            dlam_ref[...] += jnp.sum(d_la * r, axis=0, keepdims=True) * (RG_C * _sigmoid(-lam_v))
            dcb_ref[...] += jnp.sum(dxc, axis=0, keepdims=True)
            taps = _conv_taps(xpad[halo, :], ch)
            for j in range(CONV_W):
                dcw_ref[pl.ds(j, 1), :] += jnp.sum(dxc * taps[j], axis=0, keepdims=True)
            dxc_pad[rows, :] = dxc
            return 0

        lax.fori_loop(0, n_ch, grads, 0)

        def conv_bwd(c, _):
            r0 = pl.multiple_of(c * ch, ch)
            de = dxc_pad[pl.ds(r0, ch + SUBLANES), :]
            dxa = sum(cw_ref[pl.ds(j, 1), :] * _shift_rows(de, ch + SUBLANES - (CONV_W - 1 - j))[0:ch, :]
                      for j in range(CONV_W))
            dp_ref[0, pl.ds(r0, ch), :] = dxa.astype(BF16)
            return 0

        lax.fori_loop(0, n_ch, conv_bwd, 0)

    seg = lambda k: pl.BlockSpec((None, seq, wb), lambda b, s, k=k: (k, s, b))
    blk = pl.BlockSpec((seq, wb), lambda b, s: (s, b))
    vec = pl.BlockSpec((1, wb), lambda b, s: (0, b))
    taps = pl.BlockSpec((CONV_W, wb), lambda b, s: (0, b))
    wsp = pl.BlockSpec((None, wb, wb), lambda b, s: (b, 0, 0))
    vec_shape = jax.ShapeDtypeStruct((1, d), F32)
    w_shape = jax.ShapeDtypeStruct((n_blk, wb, wb), F32)
    pad = pltpu.VMEM((seq + SUBLANES, wb), F32)
    full = pltpu.VMEM((seq, wb), F32)
    return pl.pallas_call(
        body, name="mixer_a_bwd", grid=(n_blk, n_seq),
        in_specs=[seg(0), seg(1), blk, blk, taps, vec, wsp, vec, wsp, vec, vec],
        out_specs=[pl.BlockSpec((2, seq, wb), lambda b, s: (0, s, b)), wsp, wsp, vec, vec, vec, taps, vec],
        out_shape=[jax.ShapeDtypeStruct((2, t_rows, d), BF16), w_shape, w_shape, vec_shape, vec_shape,
                   vec_shape, jax.ShapeDtypeStruct((CONV_W, d), F32), vec_shape],
        scratch_shapes=[pad, pad, full, pad, full, full, full, full, pad],
        compiler_params=_cp("parallel", "arbitrary"))(
            proj, proj, h, d_ya, conv_w, conv_b, w_r, b_r, w_i, b_i, lam)


def _hg_prepare(q_ref, z_ref, lb, rows):
    z = z_ref[rows, :]
    sig, nsig = _sigmoid_pair(z)
    fg = lb + (1.0 - lb) * sig
    log_f = jnp.log(jnp.maximum(fg, F_MIN))
    key = (1.0 - lb) * nsig
    qs, _ = _silu_and_grad(q_ref[rows, :])
    return qs, key, log_f, sig, fg


HG_UNROLL_TERMS = 64
HG_UNROLL_FWD = 32
HG_UNROLL_BWD = 32
HG_HALF = HG_CHUNK // 2
HG_STACK = HG_CHUNK * HG_HALF
assert HG_HALF == SUBLANES


def _half_of(x, s):
    return x[:HG_HALF, :] if s < HG_HALF else x[HG_HALF:, :]


def _hg_decay(g_ref, r0, g_rows, first_row, s):
    rid = lax.broadcasted_iota(jnp.int32, g_rows.shape, 0) + first_row
    gs = g_ref[pl.ds(r0 + s, 1), :]
    return jnp.where(rid >= s, jnp.exp(g_rows - gs), 0.0)


def _half_start(s):
    return 0 if s < HG_HALF else HG_HALF


def _hg_cross_decays(gc):
    rid = lax.broadcasted_iota(jnp.int32, (HG_CHUNK, HG_DK), 0)
    g_mid = gc[HG_HALF - 1:HG_HALF, :]
    e_hi = jnp.where(rid >= HG_HALF, jnp.exp(gc - g_mid), 0.0)
    e_lo = jnp.where(rid < HG_HALF, jnp.exp(g_mid - gc), 0.0)
    return e_hi, e_lo


def _hg_cross(qc, kc, gc):
    e_hi, e_lo = _hg_cross_decays(gc)
    return (qc * e_hi).astype(BF16), (kc * e_lo).astype(BF16)


def _stack(slabs):
    return jnp.concatenate(slabs, axis=0).astype(BF16)


def _slab_row_sums():
    row = lax.broadcasted_iota(jnp.int32, (HG_CHUNK, HG_STACK), 0)
    col = lax.broadcasted_iota(jnp.int32, (HG_CHUNK, HG_STACK), 1)
    lo = row * HG_HALF
    return jnp.where((col >= lo) & (col < lo + HG_HALF), 1.0, 0.0).astype(BF16)


def _for_chunks(n, unroll, *stages, after_trip=None):
    unroll = min(unroll, n)
    assert n % unroll == 0

    def trip(i, _):
        chunks = [i * unroll + u for u in range(unroll)]
        carried = [stages[0](c) for c in chunks]
        for stage in stages[1:]:
            carried = [stage(c, x) for c, x in zip(chunks, carried)]
        if after_trip is not None:
            after_trip(carried)
        return 0

    lax.fori_loop(0, n // unroll, trip, 0)


def _hg_state_chain(states, g_ref, carry_ref, n_chunks, reverse):
    unroll = min(8, n_chunks)
    assert n_chunks % unroll == 0
    carry_ref[...] = jnp.zeros_like(carry_ref)

    def trip(i, _):
        st = carry_ref[...]
        for u in range(unroll):
            k = i * unroll + u
            c = n_chunks - 1 - k if reverse else k
            term = states[c]
            states[c] = st
            st = st * jnp.exp(g_ref[pl.ds(c * HG_CHUNK + HG_CHUNK - 1, 1), :]) + term
        carry_ref[...] = st
        return 0

    lax.fori_loop(0, n_chunks // unroll, trip, 0)


def _hgrn_fwd(proj, lower_bound, hg_gain, seq, carry=None):
    _, t_rows, d = proj.shape
    n_seq, n_head = t_rows // seq, d // HG_DK
    ch = _row_tile(seq, ROW_CHUNK)
    n_chunks = seq // HG_CHUNK

    def body(q_ref, z_ref, v_ref, lb_ref, gain_ref, o_ref, on_ref, qs_s, k_s, g_s, states, st_ref):
        lb = lb_ref[...]
        tri = _group_cumsum_matrix(ch, HG_CHUNK)

        per_block = ch // HG_CHUNK

        def prepare(c):
            rows = pl.ds(pl.multiple_of(c * ch, ch), ch)
            qs, key, log_f, _, _ = _hg_prepare(q_ref, z_ref, lb, rows)
            qs_s[rows, :] = qs
            k_s[rows, :] = key
            return key, _group_cumsum_mxu(log_f, tri)

        def state_terms(c, carried):
            key, g = carried
            r0 = pl.multiple_of(c * ch, ch)
            g_s[pl.ds(r0, ch), :] = g
            terms = []
            for u in range(per_block):
                sl = slice(u * HG_CHUNK, (u + 1) * HG_CHUNK)
                k_end = key[sl, :] * jnp.exp(g[(u + 1) * HG_CHUNK - 1:(u + 1) * HG_CHUNK, :] - g[sl, :])
                vc = v_ref[pl.ds(r0 + u * HG_CHUNK, HG_CHUNK), :]
                terms.append(_dot_tn(vc.astype(BF16), k_end.astype(BF16)))
            return terms

        def store_terms(c, terms):
            for u, term in enumerate(terms):
                states[c * per_block + u] = term

        _for_chunks(seq // ch, 4, prepare, state_terms, store_terms)
        _hg_state_chain(states, g_s, st_ref, n_chunks, reverse=False)
        ones = jnp.ones((HG_DK, HG_DK), BF16)

        def issue(c):
            r0 = pl.multiple_of(c * HG_CHUNK, HG_CHUNK)
            rows = pl.ds(r0, HG_CHUNK)
            qc, gc = qs_s[rows, :], g_s[rows, :]
            o = _dot_nt((qc * jnp.exp(gc)).astype(BF16), states[c].astype(BF16))
            pairs = [qc[_half_start(s):, :] * _hg_decay(g_s, r0, gc[_half_start(s):, :], _half_start(s), s)
                     * k_s[pl.ds(r0 + s, 1), :] for s in range(HG_CHUNK)]
            score = _dot(_stack(pairs), ones)
            return o, score

        def combine(c, issued):
            o, score = issued
            r0 = pl.multiple_of(c * HG_CHUNK, HG_CHUNK)
            halves = [o[:HG_HALF, :], o[HG_HALF:, :]]
            first = 0
            for s in range(HG_CHUNK):
                vs = v_ref[pl.ds(r0 + s, 1), :]
                if s < HG_HALF:
                    halves[0] += score[first:first + HG_HALF, :] * vs
                    first += HG_HALF
                halves[1] += score[first:first + HG_HALF, :] * vs
                first += HG_HALF
            o = jnp.concatenate(halves, axis=0)
            o_ref[pl.ds(r0, HG_CHUNK), :] = o
            on_ref[pl.ds(r0, HG_CHUNK), :] = (o * _rstd(o) * gain_ref[...]).astype(BF16)

        _for_chunks(n_chunks, HG_UNROLL_FWD, issue, combine)

    seg = lambda k: pl.BlockSpec((None, seq, HG_DK), lambda s, h, k=k: (k, s, h))
    blk = pl.BlockSpec((seq, HG_DK), lambda s, h: (s, h))
    full = pltpu.VMEM((seq, HG_DK), F32)
    return _call_carrying(
        body, carry, name="hgrn_fwd", grid=(n_seq, n_head),
        in_specs=[seg(2), seg(3), seg(4), pl.BlockSpec((1, HG_DK), lambda s, h: (0, h)),
                  pl.BlockSpec((1, HG_DK), lambda s, h: (0, 0))],
        out_specs=[blk, blk],
        out_shape=[jax.ShapeDtypeStruct((t_rows, d), F32), jax.ShapeDtypeStruct((t_rows, d), BF16)],
        scratch_shapes=[full, full, full, pltpu.VMEM((n_chunks, HG_DK, HG_DK), F32),
                        pltpu.VMEM((HG_DK, HG_DK), F32)],
        semantics=("parallel", "parallel"), args=(proj, proj, proj, lower_bound, hg_gain))


def _hgrn_bwd(proj, lower_bound, hg_gain, o, d_on, seq, carry=None):
    _, t_rows, d = proj.shape
    n_seq, n_head = t_rows // seq, d // HG_DK
    ch = _row_tile(seq, ROW_CHUNK)
    n_chunks = seq // HG_CHUNK
    cc = HG_CHUNK

    def body(q_ref, z_ref, v_ref, lb_ref, gain_ref, o_ref, don_ref, dp_ref, dlb_ref, dgain_ref,
             qs_s, k_s, g_s, do_s, dlb_acc, states, dstates, carry_ref):
        hh, ss = pl.program_id(0), pl.program_id(1)
        lb = lb_ref[...]

        @pl.when(ss == 0)
        def _():
            dlb_ref[...] = jnp.zeros_like(dlb_ref)

        @pl.when((ss == 0) & (hh == 0))
        def _():
            dgain_ref[...] = jnp.zeros_like(dgain_ref)

        tri = _group_cumsum_matrix(ch, cc)

        def prepare(c):
            rows = pl.ds(pl.multiple_of(c * ch, ch), ch)
            qs, key, log_f, _, _ = _hg_prepare(q_ref, z_ref, lb, rows)
            qs_s[rows, :] = qs
            k_s[rows, :] = key
            do, dgain = _rms_bwd(don_ref[rows, :], o_ref[rows, :], gain_ref[...])
            do_s[rows, :] = do
            dgain_ref[...] += dgain
            return _group_cumsum_mxu(log_f, tri)

        def store_cumsum(c, g):
            g_s[pl.ds(pl.multiple_of(c * ch, ch), ch), :] = g

        _for_chunks(seq // ch, 4, prepare, store_cumsum)

        def chain_terms(c):
            rows = pl.ds(pl.multiple_of(c * cc, cc), cc)
            gc = g_s[rows, :]
            k_end = k_s[rows, :] * jnp.exp(gc[cc - 1:cc, :] - gc)
            q_in = qs_s[rows, :] * jnp.exp(gc)
            return (_dot_tn(v_ref[rows, :].astype(BF16), k_end.astype(BF16)),
                    _dot_tn(do_s[rows, :].astype(BF16), q_in.astype(BF16)))

        def store_terms(c, terms):
            states[c], dstates[c] = terms

        _for_chunks(n_chunks, HG_UNROLL_TERMS, chain_terms, store_terms)
        _hg_state_chain(states, g_s, carry_ref, n_chunks, reverse=False)
        _hg_state_chain(dstates, g_s, carry_ref, n_chunks, reverse=True)
        ones = jnp.ones((HG_DK, HG_DK), BF16)
        row_sums = _slab_row_sums()

        def chunk_rows(c):
            r0 = pl.multiple_of(c * cc, cc)
            return r0, pl.ds(r0, cc)

        def through_state(c):
            r0, rows = chunk_rows(c)
            qc, kc, gc, vc, doc = qs_s[rows, :], k_s[rows, :], g_s[rows, :], v_ref[rows, :], do_s[rows, :]
            st, dst = states[c], dstates[c]
            g_last = gc[cc - 1:cc, :]
            e_last, e_end = jnp.exp(g_last), jnp.exp(g_last - gc)
            dob, vcb, dstb = doc.astype(BF16), vc.astype(BF16), dst.astype(BF16)
            dqs = _dot(dob, st.astype(BF16))
            dk_state = _dot(vcb, dstb)
            dv = _dot_nt((kc * e_end).astype(BF16), dstb)
            cots = [_half_of(doc, s) * v_ref[pl.ds(r0 + s, 1), :] for s in range(cc)]
            d_score = _dot(_stack(cots), ones)
            x, y = _hg_cross(qc, kc, gc)
            cross = (_dot_nt(dob, vcb), _dot_nt(vcb, dob), _dot_nt(y, x))
            return dqs, dk_state, dv, d_score, e_last * jnp.sum(dst * st, axis=0, keepdims=True), cross

        def pair_terms(c, carried):
            dqs, dk_state, dv, d_score, d_glast, (da_cross, da_cross_t, a_cross_t) = carried
            r0, rows = chunk_rows(c)
            qc, kc, gc = qs_s[rows, :], k_s[rows, :], g_s[rows, :]
            dqs = dqs * jnp.exp(gc)
            dk_state = dk_state * jnp.exp(gc[cc - 1:cc, :] - gc)
            d_glast = d_glast + jnp.sum(kc * dk_state, axis=0, keepdims=True)
            dqs_half = [dqs[:HG_HALF, :], dqs[HG_HALF:, :]]
            pairs, dk_terms = [], []
            for s in range(cc):
                qv = _half_of(qc, s)
                decay = _hg_decay(g_s, r0, _half_of(gc, s), _half_start(s), s)
                ks = k_s[pl.ds(r0 + s, 1), :]
                da_decay = d_score[s * HG_HALF:(s + 1) * HG_HALF, :] * decay
                pairs.append(qv * decay * ks)
                dk_terms.append(da_decay * qv)
                dqs_half[s // HG_HALF] += da_decay * ks
            score = _dot(_stack(pairs), ones)
            dk = dk_state + _dot(row_sums, _stack(dk_terms))
            x, y = _hg_cross(qc, kc, gc)
            cross = (_dot(da_cross.astype(BF16), y), _dot(da_cross_t.astype(BF16), x),
                     _dot(a_cross_t.astype(BF16), do_s[rows, :].astype(BF16)))
            return jnp.concatenate(dqs_half, axis=0), dk, dv, score, d_glast, cross

        def value_terms(c, carried):
            dqs, dk, dv, score, d_glast, (dx_cross, dy_cross, dv_cross) = carried
            _, rows = chunk_rows(c)
            doc, gc = do_s[rows, :], g_s[rows, :]
            dv_terms = [score[s * HG_HALF:(s + 1) * HG_HALF, :] * _half_of(doc, s) for s in range(cc)]
            e_hi, e_lo = _hg_cross_decays(gc)
            return (dqs + dx_cross * e_hi, dk + dy_cross * e_lo,
                    dv + dv_cross + _dot(row_sums, _stack(dv_terms)), d_glast)

        dlb_acc[...] = jnp.zeros_like(dlb_acc)

        def store(c, x):
            dqs, dk, dv, d_glast = x
            _, rows = chunk_rows(c)
            d_g = qs_s[rows, :] * dqs - k_s[rows, :] * dk
            d_logf = _seg_cumsum(d_g, cc, reverse=True) + d_glast
            sig, nsig = _sigmoid_pair(z_ref[rows, :])
            fg = lb + (1.0 - lb) * sig
            _, dsilu = _silu_and_grad(q_ref[rows, :])
            d_gate = jnp.where(fg > F_MIN, d_logf / fg, 0.0) - dk
            dp_ref[0, rows, :] = (dqs * dsilu).astype(BF16)
            dp_ref[1, rows, :] = (d_gate * (1.0 - lb) * sig * nsig).astype(BF16)
            dp_ref[2, rows, :] = dv.astype(BF16)
            return d_gate * nsig

        def add_lower_bound_grads(per_chunk):
            while len(per_chunk) > 1:
                per_chunk = [a + b for a, b in zip(per_chunk[::2], per_chunk[1::2])]
            dlb_acc[...] += per_chunk[0]

        _for_chunks(n_chunks, HG_UNROLL_BWD, through_state, pair_terms, value_terms, store,
                    after_trip=add_lower_bound_grads)
        dlb_ref[...] += jnp.sum(dlb_acc[...], axis=0, keepdims=True)

    seg = lambda k: pl.BlockSpec((None, seq, HG_DK), lambda h, s, k=k: (k, s, h))
    blk = pl.BlockSpec((seq, HG_DK), lambda h, s: (s, h))
    full = pltpu.VMEM((seq, HG_DK), F32)
    return _call_carrying(
        body, carry, name="hgrn_bwd", grid=(n_head, n_seq),
        in_specs=[seg(2), seg(3), seg(4), pl.BlockSpec((1, HG_DK), lambda h, s: (0, h)),
                  pl.BlockSpec((1, HG_DK), lambda h, s: (0, 0)), blk, blk],
        out_specs=[pl.BlockSpec((3, seq, HG_DK), lambda h, s: (0, s, h)),
                   pl.BlockSpec((1, HG_DK), lambda h, s: (0, h)),
                   pl.BlockSpec((1, HG_DK), lambda h, s: (0, 0))],
        out_shape=[jax.ShapeDtypeStruct((3, t_rows, d), BF16), jax.ShapeDtypeStruct((1, d), F32),
                   jax.ShapeDtypeStruct((1, HG_DK), F32)],
        scratch_shapes=[full, full, full, full, pltpu.VMEM((cc, HG_DK), F32),
                        pltpu.VMEM((n_chunks, HG_DK, HG_DK), F32), pltpu.VMEM((n_chunks, HG_DK, HG_DK), F32),
                        pltpu.VMEM((HG_DK, HG_DK), F32)],
        semantics=("arbitrary", "arbitrary"), args=(proj, proj, proj, lower_bound, hg_gain, o, d_on))


def _mesh_place():
    x, y, c = lax.axis_index("x"), lax.axis_index("y"), lax.axis_index("c")
    return x, y, c


def _peer(place, k):
    x, y, c = place
    px = 1 - x if k & 4 else x
    py = 1 - y if k & 2 else y
    pc = 1 - c if k & 1 else c
    return (px, py, pc), 4 * px + 2 * py + pc


class _Exchange:
    def __init__(self, srcs, gather):
        self.n = len(srcs)
        self.gather = gather
        self.out_shape = [jax.ShapeDtypeStruct((N_DEV,) + tuple(s.shape if gather else s.shape[1:]), s.dtype)
                          for s in srcs]
        self.scratch = [pltpu.SemaphoreType.DMA((self.n * (N_DEV - 1),)),
                        pltpu.SemaphoreType.DMA((self.n * (N_DEV - 1),)),
                        pltpu.SemaphoreType.DMA((self.n,))]

    def _copies(self, src_refs, out_refs, sems):
        send_sems, recv_sems, local_sems = sems
        place = _mesh_place()
        me = 4 * place[0] + 2 * place[1] + place[2]
        local, sends, recvs = [], [], []
        for a, (src, out) in enumerate(zip(src_refs, out_refs)):
            outgoing = (lambda idx, src=src: src) if self.gather else (lambda idx, src=src: src.at[idx])
            local.append(pltpu.make_async_copy(outgoing(me), out.at[me], local_sems.at[a]))
            for k in range(1, N_DEV):
                peer, peer_idx = _peer(place, k)
                sem = a * (N_DEV - 1) + k - 1
                sends.append(pltpu.make_async_remote_copy(
                    src_ref=outgoing(peer_idx), dst_ref=out.at[me], send_sem=send_sems.at[sem],
                    recv_sem=recv_sems.at[sem], device_id=peer, device_id_type=MESH_ID))
                recvs.append(pltpu.make_async_remote_copy(
                    src_ref=outgoing(peer_idx), dst_ref=out.at[peer_idx], send_sem=send_sems.at[sem],
                    recv_sem=recv_sems.at[sem], device_id=peer, device_id_type=MESH_ID))
        return local, sends, recvs

    def start(self, src_refs, out_refs, sems):
        local, sends, _ = self._copies(src_refs, out_refs, sems)
        for cp in local + sends:
            cp.start()

    def wait(self, src_refs, out_refs, sems):
        local, sends, recvs = self._copies(src_refs, out_refs, sems)
        for cp in recvs:
            cp.wait_recv()
        for cp in sends:
            cp.wait_send()
        for cp in local:
            cp.wait()


def _call_carrying(body, carry, *, name, grid, in_specs, out_specs, out_shape, scratch_shapes, semantics, args):
    if carry is None:
        outs = pl.pallas_call(body, name=name, grid=grid, in_specs=in_specs, out_specs=out_specs,
                              out_shape=out_shape, scratch_shapes=scratch_shapes,
                              compiler_params=_cp(*semantics))(*args)
        return outs, []
    srcs, gather = carry
    ex = _Exchange(srcs, gather)
    n, n_in, n_out, n_scr = ex.n, len(in_specs), len(out_specs), len(scratch_shapes)

    def wrapped(*refs):
        ins, refs = refs[:n_in], refs[n_in:]
        src_refs, refs = refs[:n], refs[n:]
        outs, refs = refs[:n_out], refs[n_out:]
        dst_refs, refs = refs[:n], refs[n:]
        scratch, sems = refs[:n_scr], refs[n_scr:]
        first, last = None, None
        for axis, size in enumerate(grid):
            i = pl.program_id(axis)
            first = (i == 0) if first is None else first & (i == 0)
            last = (i == size - 1) if last is None else last & (i == size - 1)

        @pl.when(first)
        def _():
            ex.start(src_refs, dst_refs, sems)

        body(*ins, *outs, *scratch)

        @pl.when(last)
        def _():
            ex.wait(src_refs, dst_refs, sems)

    any_space = pl.BlockSpec(memory_space=pl.ANY)
    res = pl.pallas_call(
        wrapped, name=name + "_carrying", grid=grid, in_specs=list(in_specs) + [any_space] * n,
        out_specs=list(out_specs) + [any_space] * n, out_shape=list(out_shape) + ex.out_shape,
        scratch_shapes=list(scratch_shapes) + ex.scratch,
        compiler_params=_cp(*(["arbitrary"] * len(grid))))(*args, *srcs)
    return res[:n_out], res[n_out:]


def _exchange(srcs, name, gather):
    ex = _Exchange(srcs, gather)
    n = ex.n

    def body(*refs):
        src_refs, out_refs, sems = refs[:n], refs[n:2 * n], refs[2 * n:]
        ex.start(src_refs, out_refs, sems)
        ex.wait(src_refs, out_refs, sems)

    any_space = pl.BlockSpec(memory_space=pl.ANY)
    return pl.pallas_call(
        body, name=name, in_specs=[any_space] * n, out_specs=[any_space] * n,
        out_shape=ex.out_shape, scratch_shapes=ex.scratch)(*srcs)


def _gather_two_level(src, name):
    def body(x_ref, out_ref, send_sems, recv_sems, local_sem):
        x, y, c = _mesh_place()
        me, sibling = (x, y, c), (x, y, 1 - c)
        chips = [(1 - x, y), (x, 1 - y), (1 - x, 1 - y)]

        def slot(px, py, pc):
            return out_ref.at[4 * px + 2 * py + pc]

        def copy(k, block, to, source=None):
            return pltpu.make_async_remote_copy(
                src_ref=slot(*block) if source is None else source, dst_ref=slot(*block),
                send_sem=send_sems.at[k], recv_sem=recv_sems.at[k], device_id=to, device_id_type=MESH_ID)

        mine = pltpu.make_async_copy(x_ref, slot(*me), local_sem)
        mine.start()
        first = [copy(0, me, sibling, source=x_ref)]
        first += [copy(1 + j, me, (*chip, c), source=x_ref) for j, chip in enumerate(chips)]
        for cp in first:
            cp.start()
        passed = [copy(4 + j, (*chip, c), sibling) for j, chip in enumerate(chips)]
        for j, chip in enumerate(chips):
            copy(1 + j, (*chip, c), me).wait_recv()
            passed[j].start()
        copy(0, sibling, me).wait_recv()
        for j, chip in enumerate(chips):
            copy(4 + j, (*chip, 1 - c), me).wait_recv()
        for cp in first + passed:
            cp.wait_send()
        mine.wait()

    any_space = pl.BlockSpec(memory_space=pl.ANY)
    n_copies = N_DEV - 1
    return pl.pallas_call(
        body, name=name, in_specs=[any_space], out_specs=any_space,
        out_shape=jax.ShapeDtypeStruct((N_DEV,) + tuple(src.shape), src.dtype),
        scratch_shapes=[pltpu.SemaphoreType.DMA((n_copies,)), pltpu.SemaphoreType.DMA((n_copies,)),
                        pltpu.SemaphoreType.DMA])(src)


def _reduce_adamw(parts, w, m, v, name, carry=None):
    rows, cols = w.shape
    n_seg = len(parts)
    seg_rows = rows // n_seg
    tr = _row_tile(seg_rows, 128)
    per_seg = seg_rows // tr
    c1 = np.float32(1.0 - ADAM_B1 ** ADAM_STEP)
    c2 = np.float32(1.0 - ADAM_B2 ** ADAM_STEP)

    def body(*refs):
        p_refs = refs[:n_seg]
        w_ref, m_ref, v_ref, g_ref, d_ref, nm_ref, nv_ref = refs[n_seg:]
        seg = pl.program_id(0)
        for k, p_ref in enumerate(p_refs):
            @pl.when(seg == k)
            def _(p_ref=p_ref):
                g = p_ref[0].astype(F32)
                for dev in range(1, N_DEV):
                    g = g + p_ref[dev].astype(F32)
                g_ref[...] = g

        g = g_ref[...]
        nm = ADAM_B1 * m_ref[...] + (1.0 - ADAM_B1) * g
        nv = ADAM_B2 * v_ref[...] + (1.0 - ADAM_B2) * (g * g)
        nm_ref[...] = nm
        nv_ref[...] = nv
        d_ref[...] = -ADAM_LR * ((nm / c1) / (jnp.sqrt(nv / c2) + ADAM_EPS) + ADAM_WD * w_ref[...])

    def part_spec(k):
        return pl.BlockSpec((N_DEV, tr, cols), lambda s, i, k=k: (0, jnp.where(s == k, i, 0), 0))

    blk = pl.BlockSpec((tr, cols), lambda s, i: (s * per_seg + i, 0))
    shp = jax.ShapeDtypeStruct((rows, cols), F32)
    return _call_carrying(
        body, carry, name=name, grid=(n_seg, per_seg),
        in_specs=[part_spec(k) for k in range(n_seg)] + [blk, blk, blk],
        out_specs=[blk, blk, blk, blk], out_shape=[shp, shp, shp, shp], scratch_shapes=[],
        semantics=("arbitrary", "arbitrary"), args=(*parts, w, m, v))


def _pack(arrays, lead=0):
    parts = []
    for a in arrays:
        f = a.reshape(a.shape[:lead] + (-1, LANES))
        pad = -f.shape[lead] % PACK_ROWS
        if pad:
            f = jnp.pad(f, [(0, 0)] * lead + [(0, pad), (0, 0)])
        parts.append(f)
    return jnp.concatenate(parts, axis=lead)


def _unpack(buf, shapes, lead=0):
    out, r = [], 0
    for shp in shapes:
        n = int(np.prod(shp)) // LANES
        part = lax.slice_in_dim(buf, r, r + n, axis=lead)
        out.append(part.reshape(buf.shape[:lead] + tuple(shp)))
        r += n + (-n % PACK_ROWS)
    return out


REPLICATED = ("lb_logits", "norm_mix", "conv_b", "b_r", "b_i", "lam", "hg_norm", "norm_mlp", "norm_final")
SMALL_SHARDED = ("conv_w", "w_r", "w_i")
LARGE_SHARDED = ("w_in", "w_out", "w_up", "w_down")
WEIGHTS = ("lb_logits", "norm_mix", "w_in", "conv_w", "conv_b", "w_r", "b_r", "w_i", "b_i", "lam", "hg_norm",
           "w_out", "norm_mlp", "w_up", "w_down", "norm_final")


def _matmul_weight_shards(p):
    depth = p["w_in"].shape[0]
    cast = {k: p[k].astype(BF16) for k in LARGE_SHARDED}
    return ([cast["w_in"][l] for l in range(depth)],
            [[cast[k][l] for k in ("w_out", "w_up", "w_down")] for l in range(depth)])


def _gathered_rest(got):
    w_out, w_up, w_down = got
    d = w_out.shape[2]
    return dict(w_out=w_out.reshape(d, d), w_up=w_up, w_down=w_down)


def _unpack_mixer_weights(small, p):
    depth, d, _ = p["w_in"].shape
    n_blk = d // RG_BLOCK_W
    conv_w, w_r, w_i = _unpack(small, [p["conv_w"].shape, p["w_r"].shape, p["w_i"].shape], lead=1)
    conv_w = conv_w.transpose(1, 2, 0, 3).reshape(depth, CONV_W, d)
    w_r = w_r.transpose(1, 2, 0, 3, 4).reshape(depth, n_blk, RG_BLOCK_W, RG_BLOCK_W).astype(BF16)
    w_i = w_i.transpose(1, 2, 0, 3, 4).reshape(depth, n_blk, RG_BLOCK_W, RG_BLOCK_W).astype(BF16)
    return conv_w, w_r, w_i


def _local_step(x, target, p):
    bl, seq, d = x.shape
    depth = p["w_in"].shape[0]
    t_rows = bl * seq
    row = lambda a, l: a[l:l + 1]
    lbs = _lower_bounds_fwd(p["lb_logits"])
    shard_in, shard_rest = _matmul_weight_shards(p)
    w_in = _gather_two_level(shard_in[0], "gather_w_in")
    cur = x.reshape(t_rows, d)
    saved, layers = [], []
    for l in range(depth):
        if l == 0:
            (proj, gates, h), small = _inproj_fwd(cur, row(p["norm_mix"], l), w_in,
                                                  carry=([_pack([p["conv_w"], p["w_r"], p["w_i"]])], True))
            conv_w, w_r, w_i = _unpack_mixer_weights(small[0], p)
        else:
            (proj, gates, h), _ = _inproj_fwd(cur, row(p["norm_mix"], l), w_in)
        w = dict(w_in=w_in, conv_w=conv_w[l], w_r=w_r[l], w_i=w_i[l])
        hs, y_a = _mixer_a_fwd(proj, w["conv_w"], row(p["conv_b"], l), w["w_r"], row(p["b_r"], l), w["w_i"],
                               row(p["b_i"], l), row(p["lam"], l), seq)
        (o, o_n), got = _hgrn_fwd(proj, row(lbs, l), row(p["hg_norm"], l), seq,
                                  carry=(shard_rest[l] + ([shard_in[l + 1]] if l + 1 < depth else []), True))
        w.update(_gathered_rest(got[:3]))
        w_in = got[3] if l + 1 < depth else None
        layers.append(w)
        x_mid, y = _merge_out_fwd(gates, y_a, o_n, cur, w["w_out"])
        x_out, u, h2 = _mlp_fwd(x_mid, row(p["norm_mlp"], l), w["w_up"], w["w_down"])
        saved.append(dict(x_in=cur, proj=proj, gates=gates, h=h, hs=hs, y_a=y_a, o=o, o_n=o_n, x_mid=x_mid, y=y, u=u, h2=h2))
        cur = x_out
    loss8, dx, dxb, g_norm_final = _loss_head(cur, p["norm_final"].reshape(1, d), target.reshape(t_rows, d))
    small = ("norm_mix", "conv_w", "conv_b", "w_r", "b_r", "w_i", "b_i", "lam", "hg_norm", "norm_mlp")
    g = {k: [None] * depth for k in small}
    d_lbs, received = [None] * depth, [None] * depth
    g_w_in = None
    for l in reversed(range(depth)):
        s, w = saved[l], layers[l]
        dx_mid, dx_mid_b, du, act, g["norm_mlp"][l] = _mlp_bwd(dx, dxb, s["u"], s["x_mid"], row(p["norm_mlp"], l),
                                                               w["w_up"], w["w_down"])
        g_w_down = _wgrad(act, dxb[None], "wgrad_down")
        g_w_up = _wgrad(s["h2"][None], du, "wgrad_up")
        d_ya, d_on, dp_c = _outproj_bwd(dx_mid_b, w["w_out"], s["gates"], s["y_a"], s["o_n"])
        g_w_out = _wgrad(s["y"][None], dx_mid_b[None], "wgrad_out").reshape(N_DEV, d // N_DEV, d)
        (dp_b, d_lbs[l], g["hg_norm"][l]), got = _hgrn_bwd(
            s["proj"], row(lbs, l), row(p["hg_norm"], l), s["o"], d_on, seq,
            carry=([g_w_out, g_w_up, g_w_down] + ([g_w_in] if g_w_in is not None else []), False))
        received[l] = [None] + list(got[:3])
        if g_w_in is not None:
            received[l + 1][0] = got[3]
        (dp_a, g["w_r"][l], g["w_i"][l], g["b_r"][l], g["b_i"][l], g["lam"][l], g["conv_w"][l],
         g["conv_b"][l]) = _mixer_a_bwd(s["proj"], s["hs"], d_ya, w["conv_w"], row(p["conv_b"], l), w["w_r"],
                                        row(p["b_r"], l), w["w_i"], row(p["b_i"], l), row(p["lam"], l), seq)
        hb = s["h"][None]
        g_w_in = jnp.concatenate([_wgrad(hb, dp_a, "wgrad_in_pair"), _wgrad(hb, dp_b, "wgrad_in_triple"),
                                  _wgrad(hb, dp_c, "wgrad_in_triple")], axis=0)
        carry = None
        if l == 0:
            carry = ([g_w_in, _mixer_grads_by_owner(g, d)], False)
        (dx, dxb, g["norm_mix"][l]), got = _inproj_bwd(dx_mid, dp_a, dp_b, dp_c, w["w_in"], s["x_in"],
                                                       row(p["norm_mix"], l), carry=carry)
    received[0][0], received_mixer = got
    grads = {k: jnp.stack(v) for k, v in g.items()}
    for k in ("norm_mix", "conv_b", "b_r", "b_i", "lam", "hg_norm", "norm_mlp"):
        grads[k] = grads[k][:, 0]
    grads["lb_logits"] = _lower_bounds_bwd(p["lb_logits"], jnp.concatenate(d_lbs, axis=0))
    grads["norm_final"] = g_norm_final[0]
    return loss8[0, 0], dx.reshape(bl, seq, d), grads, received, received_mixer


def _mixer_grads_by_owner(g, d):
    d8, n_blk, rb = d // N_DEV, d // RG_BLOCK_W, RG_BLOCK_W // N_DEV
    depth = len(g["conv_w"])
    conv_w, w_r, w_i = (jnp.stack(g[k]) for k in SMALL_SHARDED)
    return _pack([conv_w.reshape(depth, CONV_W, N_DEV, d8).transpose(2, 0, 1, 3),
                  w_r.reshape(depth, n_blk, N_DEV, rb, RG_BLOCK_W).transpose(2, 0, 1, 3, 4),
                  w_i.reshape(depth, n_blk, N_DEV, rb, RG_BLOCK_W).transpose(2, 0, 1, 3, 4)],
                 lead=1).astype(BF16)


def _update(p, mom1, mom2, grads, received, received_mixer):
    depth = p["w_in"].shape[0]
    out = {}

    replicated_parts = None
    for i, k in enumerate(LARGE_SHARDED):
        shp = p[k].shape
        flat = lambda a: a.reshape(shp[0] * shp[1], shp[2])
        parts = [received[l][i] for l in range(depth)]
        carry = ([_pack([grads[r] for r in REPLICATED])], True) if i == 0 else None
        res, got = _reduce_adamw(parts, flat(p[k]), flat(mom1[k]), flat(mom2[k]), "adamw_" + k, carry=carry)
        if i == 0:
            replicated_parts = got
        out[k] = [r.reshape(shp) for r in res]

    res, _ = _reduce_adamw([received_mixer], *[_pack([src[k] for k in SMALL_SHARDED]) for src in (p, mom1, mom2)],
                           "adamw_mixer")
    shapes = [p[k].shape for k in SMALL_SHARDED]
    for i, vals in enumerate(zip(*[_unpack(r, shapes) for r in res])):
        out[SMALL_SHARDED[i]] = list(vals)

    res, _ = _reduce_adamw(replicated_parts, *[_pack([src[k] for k in REPLICATED]) for src in (p, mom1, mom2)],
                           "adamw_replicated")
    shapes = [p[k].shape for k in REPLICATED]
    for i, vals in enumerate(zip(*[_unpack(r, shapes) for r in res])):
        out[REPLICATED[i]] = list(vals)

    return tuple(out[k][i] for i in range(4) for k in WEIGHTS)


def kernel(x, lb_logits, norm_mix, w_in, conv_w, conv_b, w_r, b_r, w_i, b_i, lam, hg_norm, w_out, norm_mlp, w_up, w_down, norm_final, loss_target, m_lb_logits, m_norm_mix, m_w_in, m_conv_w, m_conv_b, m_w_r, m_b_r, m_w_i, m_b_i, m_lam, m_hg_norm, m_w_out, m_norm_mlp, m_w_up, m_w_down, m_norm_final, v_lb_logits, v_norm_mix, v_w_in, v_conv_w, v_conv_b, v_w_r, v_b_r, v_w_i, v_b_i, v_lam, v_hg_norm, v_w_out, v_norm_mlp, v_w_up, v_w_down, v_norm_final):
    p = dict(lb_logits=lb_logits, norm_mix=norm_mix, w_in=w_in, conv_w=conv_w, conv_b=conv_b, w_r=w_r, b_r=b_r,
             w_i=w_i, b_i=b_i, lam=lam, hg_norm=hg_norm, w_out=w_out, norm_mlp=norm_mlp, w_up=w_up,
             w_down=w_down, norm_final=norm_final)
    mom1 = dict(lb_logits=m_lb_logits, norm_mix=m_norm_mix, w_in=m_w_in, conv_w=m_conv_w, conv_b=m_conv_b,
                w_r=m_w_r, b_r=m_b_r, w_i=m_w_i, b_i=m_b_i, lam=m_lam, hg_norm=m_hg_norm, w_out=m_w_out,
                norm_mlp=m_norm_mlp, w_up=m_w_up, w_down=m_w_down, norm_final=m_norm_final)
    mom2 = dict(lb_logits=v_lb_logits, norm_mix=v_norm_mix, w_in=v_w_in, conv_w=v_conv_w, conv_b=v_conv_b,
                w_r=v_w_r, b_r=v_b_r, w_i=v_w_i, b_i=v_b_i, lam=v_lam, hg_norm=v_hg_norm, w_out=v_w_out,
                norm_mlp=v_norm_mlp, w_up=v_w_up, w_down=v_w_down, norm_final=v_norm_final)
    loss, grad_x, grads, received, received_mixer = _local_step(x, loss_target, p)
    loss = lax.psum(loss, ("x", "y", "c"))
    return (loss, grad_x) + _update(p, mom1, mom2, grads, received, received_mixer)
```
